```python
import jax, jax.numpy as jnp
from jax import lax
import numpy as np

D_MODEL = 2048
BATCH = 8
SEQ = 4096
DEPTH = 2

EPS = 1e-6
N_BRANCH = 4
D_FF = 4 * D_MODEL
POOL_DIM = 512
POOL_WINDOWS = (2, 4, 8, 16)
POOL_GROUPS = len(POOL_WINDOWS)
POOL_GDIM = POOL_DIM // POOL_GROUPS
CONV_DIM = 512
CONV_WIDTH = 31
SGU_DIM = 512
SGU_GROUPS = 4
SGU_GDIM = SGU_DIM // SGU_GROUPS
CHUNK = 128
MLA_HEADS = 8
Q_LORA = 512
KV_LORA = 512
QK_NOPE = 128
QK_ROPE = 64
V_DIM = 128
ROPE_THETA = 10000.0
ATTN_BLOCK = 128
OFF_POOL = 0
OFF_CONV = OFF_POOL + POOL_DIM
OFF_SGU = OFF_CONV + 2 * CONV_DIM
OFF_Q = OFF_SGU + 2 * SGU_DIM
OFF_KV = OFF_Q + Q_LORA
OFF_KR = OFF_KV + KV_LORA
OFF_GATE = OFF_KR + QK_ROPE
N_IN = OFF_GATE + N_BRANCH * D_MODEL

kernel_name = 'hybrid_gated_pool_conv_sgu_mla_block'


def rmsnorm(x, g):
    xf = x.astype(jnp.float32)
    y = xf * lax.rsqrt(jnp.mean(xf * xf, axis=-1, keepdims=True) + EPS)
    return (y * g.astype(jnp.float32)).astype(x.dtype)


def layernorm(x, g, b):
    xf = x.astype(jnp.float32)
    mu = jnp.mean(xf, axis=-1, keepdims=True)
    var = jnp.mean(jnp.square(xf - mu), axis=-1, keepdims=True)
    y = (xf - mu) * lax.rsqrt(var + EPS)
    return (y * g.astype(jnp.float32) + b.astype(jnp.float32)).astype(x.dtype)


def pool_mixer(a, pool_w, pool_scale):
    B, S, _ = a.shape
    af = a.astype(jnp.float32)
    csum = jnp.cumsum(af, axis=1)
    t = jnp.arange(S)
    means = []
    for gi, w in enumerate(POOL_WINDOWS):
        cs = csum[..., gi * POOL_GDIM:(gi + 1) * POOL_GDIM]
        lagged = jnp.pad(cs, ((0, 0), (w, 0), (0, 0)))[:, :S]
        count = jnp.minimum(t + 1, w).astype(jnp.float32)
        means.append((cs - lagged) / count[None, :, None])
    pooled = (jnp.concatenate(means, axis=-1) - af).astype(a.dtype)
    pooled = pooled.reshape(B, S, POOL_GROUPS, POOL_GDIM)
    mixed = jnp.einsum('bsgc,gcd->bsgd', pooled, pool_w).reshape(B, S, POOL_DIM)
    return mixed * pool_scale


def conformer_conv(c, conv_w, conv_b, norm_g, norm_b):
    a, gate = jnp.split(c, 2, axis=-1)
    glu = a * jax.nn.sigmoid(gate)
    padded = jnp.pad(glu, ((0, 0), (CONV_WIDTH - 1, 0), (0, 0)))
    y = lax.conv_general_dilated(
        padded, conv_w[:, None, :], window_strides=(1,), padding='VALID',
        dimension_numbers=('NWC', 'WIO', 'NWC'), feature_group_count=CONV_DIM)
    y = layernorm(y + conv_b, norm_g, norm_b)
    return jax.nn.silu(y)


def spatial_gating(z, norm_g, norm_b, w_s, b_s):
    z = jax.nn.gelu(z)
    u, v = jnp.split(z, 2, axis=-1)
    v = layernorm(v, norm_g, norm_b)
    B, S, _ = v.shape
    v = v.reshape(B, S // CHUNK, CHUNK, SGU_GROUPS, SGU_GDIM)
    mask = jnp.tril(jnp.ones((CHUNK, CHUNK), dtype=bool))
    w = jnp.where(mask[None], w_s, 0)
    sp = jnp.einsum('gts,bnsgc->bntgc', w, v) + b_s.T[None, None, :, :, None]
    return u * sp.reshape(B, S, SGU_DIM)


def apply_rope(x, cos, sin):
    x1, x2 = jnp.split(x.astype(jnp.float32), 2, axis=-1)
    return jnp.concatenate([x1 * cos - x2 * sin, x2 * cos + x1 * sin], axis=-1).astype(x.dtype)


def latent_attention(cq, ckv, kr, cos, sin, q_norm_g, w_uq, kv_norm_g, w_ukv, attn_proj):
    B, S, _ = cq.shape
    q = (rmsnorm(cq, q_norm_g) @ w_uq).reshape(B, S, MLA_HEADS, QK_NOPE + QK_ROPE)
    q_nope = q[..., :QK_NOPE]
    q_rope = apply_rope(q[..., QK_NOPE:], cos[:, :, None], sin[:, :, None])
    kv = (rmsnorm(ckv, kv_norm_g) @ w_ukv).reshape(B, S, MLA_HEADS, QK_NOPE + V_DIM)
    k_nope, v = kv[..., :QK_NOPE], kv[..., QK_NOPE:]
    k_rope = apply_rope(kr, cos, sin)
    scale = (QK_NOPE + QK_ROPE) ** -0.5
    outs = []
    for i in range(S // ATTN_BLOCK):
        q0, q1 = i * ATTN_BLOCK, (i + 1) * ATTN_BLOCK
        s = (jnp.einsum('bqhd,bkhd->bhqk', q_nope[:, q0:q1], k_nope[:, :q1])
             + jnp.einsum('bqhd,bkd->bhqk', q_rope[:, q0:q1], k_rope[:, :q1]))
        s = s.astype(jnp.float32) * scale
        mask = jnp.arange(q1)[None, :] <= jnp.arange(q0, q1)[:, None]
        s = jnp.where(mask, s, jnp.finfo(jnp.float32).min)
        p = jax.nn.softmax(s, axis=-1).astype(v.dtype)
        outs.append(jnp.einsum('bhqk,bkhd->bqhd', p, v[:, :q1]))
    o = jnp.concatenate(outs, axis=1).reshape(B, S, MLA_HEADS * V_DIM)
    return o @ attn_proj


def _normal(key, shape, scale):
    return jax.random.normal(key, shape, jnp.float32) * scale


def _gain(key, n):
    return 1.0 + 0.02 * jax.random.normal(key, (DEPTH, n), jnp.float32)


def _bias(key, n):
    return 0.02 * jax.random.normal(key, (DEPTH, n), jnp.float32)


def _fwd_setup_inputs(seed: int = 0) -> dict:
    key = jax.random.key(seed)
    ks = jax.random.split(key, 27)
    L = DEPTH
    return {
        'x': _normal(ks[0], (BATCH, SEQ, D_MODEL), 1.0),
        'positions': jnp.broadcast_to(jnp.arange(SEQ, dtype=jnp.int32)[None, :], (BATCH, SEQ)),
        'pre_mix_g': _gain(ks[1], D_MODEL),
        'w_in': _normal(ks[2], (L, D_MODEL, N_IN), D_MODEL ** -0.5),
        'pool_w': _normal(ks[3], (L, POOL_GROUPS, POOL_GDIM, POOL_GDIM), POOL_GDIM ** -0.5),
        'pool_scale': _gain(ks[4], POOL_DIM),
        'pool_proj': _normal(ks[5], (L, POOL_DIM, D_MODEL), POOL_DIM ** -0.5),
        'conv_w': _normal(ks[6], (L, CONV_WIDTH, CONV_DIM), CONV_WIDTH ** -0.5),
        'conv_b': _bias(ks[7], CONV_DIM),
        'conv_norm_g': _gain(ks[8], CONV_DIM),
        'conv_norm_b': _bias(ks[9], CONV_DIM),
        'conv_proj': _normal(ks[10], (L, CONV_DIM, D_MODEL), CONV_DIM ** -0.5),
        'sgu_norm_g': _gain(ks[11], SGU_DIM),
        'sgu_norm_b': _bias(ks[12], SGU_DIM),
        'sgu_w': _normal(ks[13], (L, SGU_GROUPS, CHUNK, CHUNK), CHUNK ** -0.5),
        'sgu_b': 1.0 + 0.02 * jax.random.normal(ks[14], (L, SGU_GROUPS, CHUNK), jnp.float32),
        'sgu_proj': _normal(ks[15], (L, SGU_DIM, D_MODEL), SGU_DIM ** -0.5),
        'q_norm_g': _gain(ks[16], Q_LORA),
        'w_uq': _normal(ks[17], (L, Q_LORA, MLA_HEADS * (QK_NOPE + QK_ROPE)), Q_LORA ** -0.5),
        'kv_norm_g': _gain(ks[18], KV_LORA),
        'w_ukv': _normal(ks[19], (L, KV_LORA, MLA_HEADS * (QK_NOPE + V_DIM)), KV_LORA ** -0.5),
        'attn_proj': _normal(ks[20], (L, MLA_HEADS * V_DIM, D_MODEL), (MLA_HEADS * V_DIM) ** -0.5),
        'w_out': _normal(ks[21], (L, D_MODEL, D_MODEL), D_MODEL ** -0.5),
        'post_mix_g': _gain(ks[22], D_MODEL),
        'pre_mlp_g': _gain(ks[23], D_MODEL),
        'w_up': _normal(ks[24], (L, D_MODEL, D_FF), D_MODEL ** -0.5),
        'w_down': _normal(ks[25], (L, D_FF, D_MODEL), D_FF ** -0.5),
        'post_mlp_g': _gain(ks[26], D_MODEL),
    }


def _fwd_reference(x, positions, pre_mix_g, w_in, pool_w, pool_scale, pool_proj, conv_w, conv_b,
              conv_norm_g, conv_norm_b, conv_proj, sgu_norm_g, sgu_norm_b, sgu_w, sgu_b,
              sgu_proj, q_norm_g, w_uq, kv_norm_g, w_ukv, attn_proj, w_out, post_mix_g,
              pre_mlp_g, w_up, w_down, post_mlp_g):
    B, S, _ = x.shape
    inv_freq = ROPE_THETA ** (-jnp.arange(0, QK_ROPE, 2, dtype=jnp.float32) / QK_ROPE)
    ang = positions.astype(jnp.float32)[..., None] * inv_freq
    cos, sin = jnp.cos(ang), jnp.sin(ang)
    for l in range(DEPTH):
        h = rmsnorm(x, pre_mix_g[l])
        z = h @ w_in[l]
        y_pool = pool_mixer(z[..., OFF_POOL:OFF_CONV], pool_w[l], pool_scale[l]) @ pool_proj[l]
        y_conv = conformer_conv(z[..., OFF_CONV:OFF_SGU], conv_w[l], conv_b[l],
                                conv_norm_g[l], conv_norm_b[l]) @ conv_proj[l]
        y_sgu = spatial_gating(z[..., OFF_SGU:OFF_Q], sgu_norm_g[l], sgu_norm_b[l],
                               sgu_w[l], sgu_b[l]) @ sgu_proj[l]
        y_attn = latent_attention(z[..., OFF_Q:OFF_KV], z[..., OFF_KV:OFF_KR],
                                  z[..., OFF_KR:OFF_GATE], cos, sin, q_norm_g[l], w_uq[l],
                                  kv_norm_g[l], w_ukv[l], attn_proj[l])
        gates = jax.nn.sigmoid(z[..., OFF_GATE:].reshape(B, S, N_BRANCH, D_MODEL))
        merged = (gates[:, :, 0] * y_pool + gates[:, :, 1] * y_conv
                  + gates[:, :, 2] * y_sgu + gates[:, :, 3] * y_attn)
        x = x + rmsnorm(merged @ w_out[l], post_mix_g[l])
        h = rmsnorm(x, pre_mlp_g[l])
        f = jnp.square(jax.nn.relu(h @ w_up[l])) @ w_down[l]
        x = x + rmsnorm(f, post_mlp_g[l])
    return x


import jax as _jax
import jax.numpy as _jnp

TWIN_FORMAT = 'train_step'
FWD_PARAMS = ['x', 'positions', 'pre_mix_g', 'w_in', 'pool_w', 'pool_scale', 'pool_proj', 'conv_w', 'conv_b', 'conv_norm_g', 'conv_norm_b', 'conv_proj', 'sgu_norm_g', 'sgu_norm_b', 'sgu_w', 'sgu_b', 'sgu_proj', 'q_norm_g', 'w_uq', 'kv_norm_g', 'w_ukv', 'attn_proj', 'w_out', 'post_mix_g', 'pre_mlp_g', 'w_up', 'w_down', 'post_mlp_g']
TWIN_WEIGHTS = ['pre_mix_g', 'w_in', 'pool_w', 'pool_scale', 'pool_proj', 'conv_w', 'conv_b', 'conv_norm_g', 'conv_norm_b', 'conv_proj', 'sgu_norm_g', 'sgu_norm_b', 'sgu_w', 'sgu_b', 'sgu_proj', 'q_norm_g', 'w_uq', 'kv_norm_g', 'w_ukv', 'attn_proj', 'w_out', 'post_mix_g', 'pre_mlp_g', 'w_up', 'w_down', 'post_mlp_g']
TWIN_DIFF_INPUT = 'x'
TWIN_INPUTS = ['x', 'positions', 'pre_mix_g', 'w_in', 'pool_w', 'pool_scale', 'pool_proj', 'conv_w', 'conv_b', 'conv_norm_g', 'conv_norm_b', 'conv_proj', 'sgu_norm_g', 'sgu_norm_b', 'sgu_w', 'sgu_b', 'sgu_proj', 'q_norm_g', 'w_uq', 'kv_norm_g', 'w_ukv', 'attn_proj', 'w_out', 'post_mix_g', 'pre_mlp_g', 'w_up', 'w_down', 'post_mlp_g', 'loss_target', 'm_pre_mix_g', 'm_w_in', 'm_pool_w', 'm_pool_scale', 'm_pool_proj', 'm_conv_w', 'm_conv_b', 'm_conv_norm_g', 'm_conv_norm_b', 'm_conv_proj', 'm_sgu_norm_g', 'm_sgu_norm_b', 'm_sgu_w', 'm_sgu_b', 'm_sgu_proj', 'm_q_norm_g', 'm_w_uq', 'm_kv_norm_g', 'm_w_ukv', 'm_attn_proj', 'm_w_out', 'm_post_mix_g', 'm_pre_mlp_g', 'm_w_up', 'm_w_down', 'm_post_mlp_g', 'v_pre_mix_g', 'v_w_in', 'v_pool_w', 'v_pool_scale', 'v_pool_proj', 'v_conv_w', 'v_conv_b', 'v_conv_norm_g', 'v_conv_norm_b', 'v_conv_proj', 'v_sgu_norm_g', 'v_sgu_norm_b', 'v_sgu_w', 'v_sgu_b', 'v_sgu_proj', 'v_q_norm_g', 'v_w_uq', 'v_kv_norm_g', 'v_w_ukv', 'v_attn_proj', 'v_w_out', 'v_post_mix_g', 'v_pre_mlp_g', 'v_w_up', 'v_w_down', 'v_post_mlp_g']
TWIN_OUTPUTS = ['loss', 'grad_x', 'grad_pre_mix_g', 'grad_w_in', 'grad_pool_w', 'grad_pool_scale', 'grad_pool_proj', 'grad_conv_w', 'grad_conv_b', 'grad_conv_norm_g', 'grad_conv_norm_b', 'grad_conv_proj', 'grad_sgu_norm_g', 'grad_sgu_norm_b', 'grad_sgu_w', 'grad_sgu_b', 'grad_sgu_proj', 'grad_q_norm_g', 'grad_w_uq', 'grad_kv_norm_g', 'grad_w_ukv', 'grad_attn_proj', 'grad_w_out', 'grad_post_mix_g', 'grad_pre_mlp_g', 'grad_w_up', 'grad_w_down', 'grad_post_mlp_g', 'delta_pre_mix_g', 'delta_w_in', 'delta_pool_w', 'delta_pool_scale', 'delta_pool_proj', 'delta_conv_w', 'delta_conv_b', 'delta_conv_norm_g', 'delta_conv_norm_b', 'delta_conv_proj', 'delta_sgu_norm_g', 'delta_sgu_norm_b', 'delta_sgu_w', 'delta_sgu_b', 'delta_sgu_proj', 'delta_q_norm_g', 'delta_w_uq', 'delta_kv_norm_g', 'delta_w_ukv', 'delta_attn_proj', 'delta_w_out', 'delta_post_mix_g', 'delta_pre_mlp_g', 'delta_w_up', 'delta_w_down', 'delta_post_mlp_g', 'new_m_pre_mix_g', 'new_m_w_in', 'new_m_pool_w', 'new_m_pool_scale', 'new_m_pool_proj', 'new_m_conv_w', 'new_m_conv_b', 'new_m_conv_norm_g', 'new_m_conv_norm_b', 'new_m_conv_proj', 'new_m_sgu_norm_g', 'new_m_sgu_norm_b', 'new_m_sgu_w', 'new_m_sgu_b', 'new_m_sgu_proj', 'new_m_q_norm_g', 'new_m_w_uq', 'new_m_kv_norm_g', 'new_m_w_ukv', 'new_m_attn_proj', 'new_m_w_out', 'new_m_post_mix_g', 'new_m_pre_mlp_g', 'new_m_w_up', 'new_m_w_down', 'new_m_post_mlp_g', 'new_v_pre_mix_g', 'new_v_w_in', 'new_v_pool_w', 'new_v_pool_scale', 'new_v_pool_proj', 'new_v_conv_w', 'new_v_conv_b', 'new_v_conv_norm_g', 'new_v_conv_norm_b', 'new_v_conv_proj', 'new_v_sgu_norm_g', 'new_v_sgu_norm_b', 'new_v_sgu_w', 'new_v_sgu_b', 'new_v_sgu_proj', 'new_v_q_norm_g', 'new_v_w_uq', 'new_v_kv_norm_g', 'new_v_w_ukv', 'new_v_attn_proj', 'new_v_w_out', 'new_v_post_mix_g', 'new_v_pre_mlp_g', 'new_v_w_up', 'new_v_w_down', 'new_v_post_mlp_g']
TWIN_LEAF_KINDS = {'loss': 'loss', 'grad_x': 'grad_x', 'grad_pre_mix_g': 'grad_w', 'grad_w_in': 'grad_w', 'grad_pool_w': 'grad_w', 'grad_pool_scale': 'grad_w', 'grad_pool_proj': 'grad_w', 'grad_conv_w': 'grad_w', 'grad_conv_b': 'grad_w', 'grad_conv_norm_g': 'grad_w', 'grad_conv_norm_b': 'grad_w', 'grad_conv_proj': 'grad_w', 'grad_sgu_norm_g': 'grad_w', 'grad_sgu_norm_b': 'grad_w', 'grad_sgu_w': 'grad_w', 'grad_sgu_b': 'grad_w', 'grad_sgu_proj': 'grad_w', 'grad_q_norm_g': 'grad_w', 'grad_w_uq': 'grad_w', 'grad_kv_norm_g': 'grad_w', 'grad_w_ukv': 'grad_w', 'grad_attn_proj': 'grad_w', 'grad_w_out': 'grad_w', 'grad_post_mix_g': 'grad_w', 'grad_pre_mlp_g': 'grad_w', 'grad_w_up': 'grad_w', 'grad_w_down': 'grad_w', 'grad_post_mlp_g': 'grad_w', 'delta_pre_mix_g': 'delta_w', 'delta_w_in': 'delta_w', 'delta_pool_w': 'delta_w', 'delta_pool_scale': 'delta_w', 'delta_pool_proj': 'delta_w', 'delta_conv_w': 'delta_w', 'delta_conv_b': 'delta_w', 'delta_conv_norm_g': 'delta_w', 'delta_conv_norm_b': 'delta_w', 'delta_conv_proj': 'delta_w', 'delta_sgu_norm_g': 'delta_w', 'delta_sgu_norm_b': 'delta_w', 'delta_sgu_w': 'delta_w', 'delta_sgu_b': 'delta_w', 'delta_sgu_proj': 'delta_w', 'delta_q_norm_g': 'delta_w', 'delta_w_uq': 'delta_w', 'delta_kv_norm_g': 'delta_w', 'delta_w_ukv': 'delta_w', 'delta_attn_proj': 'delta_w', 'delta_w_out': 'delta_w', 'delta_post_mix_g': 'delta_w', 'delta_pre_mlp_g': 'delta_w', 'delta_w_up': 'delta_w', 'delta_w_down': 'delta_w', 'delta_post_mlp_g': 'delta_w', 'new_m_pre_mix_g': 'new_m', 'new_m_w_in': 'new_m', 'new_m_pool_w': 'new_m', 'new_m_pool_scale': 'new_m', 'new_m_pool_proj': 'new_m', 'new_m_conv_w': 'new_m', 'new_m_conv_b': 'new_m', 'new_m_conv_norm_g': 'new_m', 'new_m_conv_norm_b': 'new_m', 'new_m_conv_proj': 'new_m', 'new_m_sgu_norm_g': 'new_m', 'new_m_sgu_norm_b': 'new_m', 'new_m_sgu_w': 'new_m', 'new_m_sgu_b': 'new_m', 'new_m_sgu_proj': 'new_m', 'new_m_q_norm_g': 'new_m', 'new_m_w_uq': 'new_m', 'new_m_kv_norm_g': 'new_m', 'new_m_w_ukv': 'new_m', 'new_m_attn_proj': 'new_m', 'new_m_w_out': 'new_m', 'new_m_post_mix_g': 'new_m', 'new_m_pre_mlp_g': 'new_m', 'new_m_w_up': 'new_m', 'new_m_w_down': 'new_m', 'new_m_post_mlp_g': 'new_m', 'new_v_pre_mix_g': 'new_v', 'new_v_w_in': 'new_v', 'new_v_pool_w': 'new_v', 'new_v_pool_scale': 'new_v', 'new_v_pool_proj': 'new_v', 'new_v_conv_w': 'new_v', 'new_v_conv_b': 'new_v', 'new_v_conv_norm_g': 'new_v', 'new_v_conv_norm_b': 'new_v', 'new_v_conv_proj': 'new_v', 'new_v_sgu_norm_g': 'new_v', 'new_v_sgu_norm_b': 'new_v', 'new_v_sgu_w': 'new_v', 'new_v_sgu_b': 'new_v', 'new_v_sgu_proj': 'new_v', 'new_v_q_norm_g': 'new_v', 'new_v_w_uq': 'new_v', 'new_v_kv_norm_g': 'new_v', 'new_v_w_ukv': 'new_v', 'new_v_attn_proj': 'new_v', 'new_v_w_out': 'new_v', 'new_v_post_mix_g': 'new_v', 'new_v_pre_mlp_g': 'new_v', 'new_v_w_up': 'new_v', 'new_v_w_down': 'new_v', 'new_v_post_mlp_g': 'new_v'}


def _forward(args):
    return _fwd_reference(*[args[k] for k in FWD_PARAMS])


def _output_shape():
    out = _jax.eval_shape(lambda: _forward(_fwd_setup_inputs(0)))
    return out.shape, out.dtype

N_MICROBATCH = 1
ADAM_LR = 0.001
ADAM_B1 = 0.9
ADAM_B2 = 0.999
ADAM_EPS = 1e-08
ADAM_WD = 0.01
ADAM_STEP = 10
PER_EXAMPLE_BATCH_AXIS = {'x': 0, 'positions': 0, 'loss_target': 0}
SHARED_INPUTS = []
_WEIGHT_DTYPES = {'pre_mix_g': _jnp.float32, 'w_in': _jnp.float32, 'pool_w': _jnp.float32, 'pool_scale': _jnp.float32, 'pool_proj': _jnp.float32, 'conv_w': _jnp.float32, 'conv_b': _jnp.float32, 'conv_norm_g': _jnp.float32, 'conv_norm_b': _jnp.float32, 'conv_proj': _jnp.float32, 'sgu_norm_g': _jnp.float32, 'sgu_norm_b': _jnp.float32, 'sgu_w': _jnp.float32, 'sgu_b': _jnp.float32, 'sgu_proj': _jnp.float32, 'q_norm_g': _jnp.float32, 'w_uq': _jnp.float32, 'kv_norm_g': _jnp.float32, 'w_ukv': _jnp.float32, 'attn_proj': _jnp.float32, 'w_out': _jnp.float32, 'post_mix_g': _jnp.float32, 'pre_mlp_g': _jnp.float32, 'w_up': _jnp.float32, 'w_down': _jnp.float32, 'post_mlp_g': _jnp.float32}
MOMENT_SCALE = {'pre_mix_g': 1.727393e+00, 'w_in': 6.872504e-01, 'pool_w': 7.366863e-01, 'pool_scale': 9.305509e-01, 'pool_proj': 4.087239e-01, 'conv_w': 1.580255e+00, 'conv_b': 1.505505e+01, 'conv_norm_g': 6.101928e+00, 'conv_norm_b': 8.835231e+00, 'conv_proj': 2.015865e+00, 'sgu_norm_g': 2.495644e-01, 'sgu_norm_b': 2.752298e-01, 'sgu_w': 2.063568e-01, 'sgu_b': 3.295505e-01, 'sgu_proj': 2.536902e+00, 'q_norm_g': 8.798703e-02, 'w_uq': 5.035018e-02, 'kv_norm_g': 3.179127e+00, 'w_ukv': 1.579348e+00, 'attn_proj': 1.558434e+00, 'w_out': 3.538153e+00, 'post_mix_g': 1.655087e+01, 'pre_mlp_g': 1.606856e+00, 'w_up': 8.025084e-01, 'w_down': 4.665120e+00, 'post_mlp_g': 1.739073e+01}


def _to_microbatches(a, axis):
    t = _jnp.moveaxis(a, axis, 0)
    t = t.reshape((N_MICROBATCH, t.shape[0] // N_MICROBATCH) + t.shape[1:])
    return _jnp.moveaxis(t, 1, axis + 1)


def setup_inputs(seed: int = 0) -> dict:
    inp = _fwd_setup_inputs(seed)
    key = _jax.random.fold_in(_jax.random.key(seed), 7919)
    shape, _ = _output_shape()
    out = dict(inp)
    out["loss_target"] = _jax.random.normal(_jax.random.fold_in(key, 0), shape, _jnp.float32)
    for i, name in enumerate(TWIN_WEIGHTS):
        w = inp[name].astype(_jnp.float32)
        if MOMENT_SCALE is None:
            s = _jnp.sqrt(_jnp.mean(_jnp.square(w)) + 1e-30)
        else:
            s = MOMENT_SCALE[name]
        km, kv = _jax.random.split(_jax.random.fold_in(key, i + 1))
        out[name] = w
        out["m_" + name] = s * _jax.random.normal(km, w.shape, _jnp.float32)
        out["v_" + name] = (s * s) * _jax.random.uniform(kv, w.shape, _jnp.float32, 0.5, 1.5)
    if N_MICROBATCH > 1:
        for name, axis in PER_EXAMPLE_BATCH_AXIS.items():
            out[name] = _to_microbatches(out[name], axis)
    return {'x': out['x'], 'positions': out['positions'], 'pre_mix_g': out['pre_mix_g'], 'w_in': out['w_in'], 'pool_w': out['pool_w'], 'pool_scale': out['pool_scale'], 'pool_proj': out['pool_proj'], 'conv_w': out['conv_w'], 'conv_b': out['conv_b'], 'conv_norm_g': out['conv_norm_g'], 'conv_norm_b': out['conv_norm_b'], 'conv_proj': out['conv_proj'], 'sgu_norm_g': out['sgu_norm_g'], 'sgu_norm_b': out['sgu_norm_b'], 'sgu_w': out['sgu_w'], 'sgu_b': out['sgu_b'], 'sgu_proj': out['sgu_proj'], 'q_norm_g': out['q_norm_g'], 'w_uq': out['w_uq'], 'kv_norm_g': out['kv_norm_g'], 'w_ukv': out['w_ukv'], 'attn_proj': out['attn_proj'], 'w_out': out['w_out'], 'post_mix_g': out['post_mix_g'], 'pre_mlp_g': out['pre_mlp_g'], 'w_up': out['w_up'], 'w_down': out['w_down'], 'post_mlp_g': out['post_mlp_g'], 'loss_target': out['loss_target'], 'm_pre_mix_g': out['m_pre_mix_g'], 'm_w_in': out['m_w_in'], 'm_pool_w': out['m_pool_w'], 'm_pool_scale': out['m_pool_scale'], 'm_pool_proj': out['m_pool_proj'], 'm_conv_w': out['m_conv_w'], 'm_conv_b': out['m_conv_b'], 'm_conv_norm_g': out['m_conv_norm_g'], 'm_conv_norm_b': out['m_conv_norm_b'], 'm_conv_proj': out['m_conv_proj'], 'm_sgu_norm_g': out['m_sgu_norm_g'], 'm_sgu_norm_b': out['m_sgu_norm_b'], 'm_sgu_w': out['m_sgu_w'], 'm_sgu_b': out['m_sgu_b'], 'm_sgu_proj': out['m_sgu_proj'], 'm_q_norm_g': out['m_q_norm_g'], 'm_w_uq': out['m_w_uq'], 'm_kv_norm_g': out['m_kv_norm_g'], 'm_w_ukv': out['m_w_ukv'], 'm_attn_proj': out['m_attn_proj'], 'm_w_out': out['m_w_out'], 'm_post_mix_g': out['m_post_mix_g'], 'm_pre_mlp_g': out['m_pre_mlp_g'], 'm_w_up': out['m_w_up'], 'm_w_down': out['m_w_down'], 'm_post_mlp_g': out['m_post_mlp_g'], 'v_pre_mix_g': out['v_pre_mix_g'], 'v_w_in': out['v_w_in'], 'v_pool_w': out['v_pool_w'], 'v_pool_scale': out['v_pool_scale'], 'v_pool_proj': out['v_pool_proj'], 'v_conv_w': out['v_conv_w'], 'v_conv_b': out['v_conv_b'], 'v_conv_norm_g': out['v_conv_norm_g'], 'v_conv_norm_b': out['v_conv_norm_b'], 'v_conv_proj': out['v_conv_proj'], 'v_sgu_norm_g': out['v_sgu_norm_g'], 'v_sgu_norm_b': out['v_sgu_norm_b'], 'v_sgu_w': out['v_sgu_w'], 'v_sgu_b': out['v_sgu_b'], 'v_sgu_proj': out['v_sgu_proj'], 'v_q_norm_g': out['v_q_norm_g'], 'v_w_uq': out['v_w_uq'], 'v_kv_norm_g': out['v_kv_norm_g'], 'v_w_ukv': out['v_w_ukv'], 'v_attn_proj': out['v_attn_proj'], 'v_w_out': out['v_w_out'], 'v_post_mix_g': out['v_post_mix_g'], 'v_pre_mlp_g': out['v_pre_mlp_g'], 'v_w_up': out['v_w_up'], 'v_w_down': out['v_w_down'], 'v_post_mlp_g': out['v_post_mlp_g']}


def _loss(weights, diff, rest, loss_target):
    with _jax.named_scope("forward"):
        args = {**rest, TWIN_DIFF_INPUT: diff, **{k: w.astype(_WEIGHT_DTYPES[k]) for k, w in weights.items()}}
        y = _forward(args)
    with _jax.named_scope("loss_head"):
        err = _jnp.square(y.astype(_jnp.float32) - loss_target)
        return 0.5 * _jnp.sum(_jnp.mean(err, axis=-1)) if err.ndim else 0.5 * err


def _adamw(w, g, m, v):
    m = ADAM_B1 * m + (1.0 - ADAM_B1) * g
    v = ADAM_B2 * v + (1.0 - ADAM_B2) * _jnp.square(g)
    m_hat = m / (1.0 - ADAM_B1 ** ADAM_STEP)
    v_hat = v / (1.0 - ADAM_B2 ** ADAM_STEP)
    delta = -ADAM_LR * (m_hat / (_jnp.sqrt(v_hat) + ADAM_EPS) + ADAM_WD * w)
    return delta, m, v


def reference(x, positions, pre_mix_g, w_in, pool_w, pool_scale, pool_proj, conv_w, conv_b, conv_norm_g, conv_norm_b, conv_proj, sgu_norm_g, sgu_norm_b, sgu_w, sgu_b, sgu_proj, q_norm_g, w_uq, kv_norm_g, w_ukv, attn_proj, w_out, post_mix_g, pre_mlp_g, w_up, w_down, post_mlp_g, loss_target, m_pre_mix_g, m_w_in, m_pool_w, m_pool_scale, m_pool_proj, m_conv_w, m_conv_b, m_conv_norm_g, m_conv_norm_b, m_conv_proj, m_sgu_norm_g, m_sgu_norm_b, m_sgu_w, m_sgu_b, m_sgu_proj, m_q_norm_g, m_w_uq, m_kv_norm_g, m_w_ukv, m_attn_proj, m_w_out, m_post_mix_g, m_pre_mlp_g, m_w_up, m_w_down, m_post_mlp_g, v_pre_mix_g, v_w_in, v_pool_w, v_pool_scale, v_pool_proj, v_conv_w, v_conv_b, v_conv_norm_g, v_conv_norm_b, v_conv_proj, v_sgu_norm_g, v_sgu_norm_b, v_sgu_w, v_sgu_b, v_sgu_proj, v_q_norm_g, v_w_uq, v_kv_norm_g, v_w_ukv, v_attn_proj, v_w_out, v_post_mix_g, v_pre_mlp_g, v_w_up, v_w_down, v_post_mlp_g):
    given = dict(x=x, positions=positions, pre_mix_g=pre_mix_g, w_in=w_in, pool_w=pool_w, pool_scale=pool_scale, pool_proj=pool_proj, conv_w=conv_w, conv_b=conv_b, conv_norm_g=conv_norm_g, conv_norm_b=conv_norm_b, conv_proj=conv_proj, sgu_norm_g=sgu_norm_g, sgu_norm_b=sgu_norm_b, sgu_w=sgu_w, sgu_b=sgu_b, sgu_proj=sgu_proj, q_norm_g=q_norm_g, w_uq=w_uq, kv_norm_g=kv_norm_g, w_ukv=w_ukv, attn_proj=attn_proj, w_out=w_out, post_mix_g=post_mix_g, pre_mlp_g=pre_mlp_g, w_up=w_up, w_down=w_down, post_mlp_g=post_mlp_g, loss_target=loss_target, m_pre_mix_g=m_pre_mix_g, m_w_in=m_w_in, m_pool_w=m_pool_w, m_pool_scale=m_pool_scale, m_pool_proj=m_pool_proj, m_conv_w=m_conv_w, m_conv_b=m_conv_b, m_conv_norm_g=m_conv_norm_g, m_conv_norm_b=m_conv_norm_b, m_conv_proj=m_conv_proj, m_sgu_norm_g=m_sgu_norm_g, m_sgu_norm_b=m_sgu_norm_b, m_sgu_w=m_sgu_w, m_sgu_b=m_sgu_b, m_sgu_proj=m_sgu_proj, m_q_norm_g=m_q_norm_g, m_w_uq=m_w_uq, m_kv_norm_g=m_kv_norm_g, m_w_ukv=m_w_ukv, m_attn_proj=m_attn_proj, m_w_out=m_w_out, m_post_mix_g=m_post_mix_g, m_pre_mlp_g=m_pre_mlp_g, m_w_up=m_w_up, m_w_down=m_w_down, m_post_mlp_g=m_post_mlp_g, v_pre_mix_g=v_pre_mix_g, v_w_in=v_w_in, v_pool_w=v_pool_w, v_pool_scale=v_pool_scale, v_pool_proj=v_pool_proj, v_conv_w=v_conv_w, v_conv_b=v_conv_b, v_conv_norm_g=v_conv_norm_g, v_conv_norm_b=v_conv_norm_b, v_conv_proj=v_conv_proj, v_sgu_norm_g=v_sgu_norm_g, v_sgu_norm_b=v_sgu_norm_b, v_sgu_w=v_sgu_w, v_sgu_b=v_sgu_b, v_sgu_proj=v_sgu_proj, v_q_norm_g=v_q_norm_g, v_w_uq=v_w_uq, v_kv_norm_g=v_kv_norm_g, v_w_ukv=v_w_ukv, v_attn_proj=v_attn_proj, v_w_out=v_w_out, v_post_mix_g=v_post_mix_g, v_pre_mlp_g=v_pre_mlp_g, v_w_up=v_w_up, v_w_down=v_w_down, v_post_mlp_g=v_post_mlp_g)
    weights = {n: given[n] for n in TWIN_WEIGHTS}
    shared = {n: given[n] for n in SHARED_INPUTS}
    per_example = {n: given[n] for n in ['x', 'positions']}
    grad_fn = _jax.value_and_grad(_loss, argnums=(0, 1))

    def one_microbatch(ex, loss_target):
        ex = dict(ex)
        diff = ex.pop(TWIN_DIFF_INPUT)
        return grad_fn(weights, diff, {**shared, **ex}, loss_target)

    if N_MICROBATCH == 1:
        loss, (grad_w, grad_x) = one_microbatch(per_example, given["loss_target"])
    else:
        def body(carry, xs):
            loss_sum, grad_sum = carry
            l_k, (gw_k, gx_k) = one_microbatch(xs[0], xs[1])
            with _jax.named_scope("update"):
                return (loss_sum + l_k, _jax.tree.map(_jnp.add, grad_sum, gw_k)), gx_k

        init = (_jnp.zeros((), _jnp.float32), _jax.tree.map(_jnp.zeros_like, weights))
        (loss, grad_w), grad_x = _jax.lax.scan(body, init, (per_example, given["loss_target"]))
    with _jax.named_scope("update"):
        delta_w, new_m, new_v = {}, {}, {}
        for n in TWIN_WEIGHTS:
            delta_w[n], new_m[n], new_v[n] = _adamw(weights[n], grad_w[n], given["m_" + n], given["v_" + n])
    return (loss, grad_x, *[grad_w[n] for n in TWIN_WEIGHTS], *[delta_w[n] for n in TWIN_WEIGHTS],
            *[new_m[n] for n in TWIN_WEIGHTS], *[new_v[n] for n in TWIN_WEIGHTS])
```

```python
import functools
import math

import jax
import jax.numpy as jnp
from jax import lax
from jax.experimental import pallas as pl
from jax.experimental.pallas import tpu as pltpu

F32 = jnp.float32
BF16 = jnp.bfloat16

D_MODEL = 2048
DEPTH = 2
EPS = 1e-6
N_BRANCH = 4
D_FF = 4 * D_MODEL
POOL_WINDOWS = (2, 4, 8, 16)
CONV_WIDTH = 31
CHUNK = 128
MLA_HEADS = 8
QK_NOPE = 128
QK_ROPE = 64
V_DIM = 128
ROPE_THETA = 10000.0
GDIM = 128
BR = 512
N_IN_A = 3648
ZA = 3712
N_GATE = N_BRANCH * D_MODEL
N_IN = N_IN_A + N_GATE
QW = 256
ACT_CAT = 3 * BR + MLA_HEADS * V_DIM
ATT_SCALE = (QK_NOPE + QK_ROPE) ** -0.5
NEG = -1e30

ADAM_LR = 0.001
ADAM_B1 = 0.9
ADAM_B2 = 0.999
ADAM_EPS = 1e-08
ADAM_WD = 0.01
ADAM_STEP = 10

N_DEV = 8
PACK_W = 1024
VMEM_BIG = 48 * 1024 * 1024
MESH = pl.DeviceIdType.MESH

BIG = (
    ("w_in", 1, (2048, 1480)),
    ("pool_proj", 1, (512, 256)),
    ("conv_proj", 1, (512, 256)),
    ("sgu_proj", 1, (512, 256)),
    ("w_uq", 1, (512, 192)),
    ("w_ukv", 1, (512, 256)),
    ("attn_proj", 1, (1024, 256)),
    ("w_out", 0, (256, 2048)),
    ("w_up", 1, (2048, 1024)),
    ("w_down", 0, (1024, 2048)),
)
SMALL = (
    ("pre_mix_g", (2048,)), ("pool_w", (4, 128, 128)), ("pool_scale", (512,)), ("conv_b", (512,)),
    ("conv_norm_g", (512,)), ("conv_norm_b", (512,)), ("sgu_norm_g", (512,)), ("sgu_norm_b", (512,)),
    ("sgu_w", (4, 128, 128)), ("sgu_b", (4, 128)), ("q_norm_g", (512,)), ("kv_norm_g", (512,)),
    ("post_mix_g", (2048,)), ("pre_mlp_g", (2048,)), ("post_mlp_g", (2048,)),
)
WEIGHTS = ("pre_mix_g", "w_in", "pool_w", "pool_scale", "pool_proj", "conv_w", "conv_b", "conv_norm_g", "conv_norm_b",
           "conv_proj", "sgu_norm_g", "sgu_norm_b", "sgu_w", "sgu_b", "sgu_proj", "q_norm_g", "w_uq", "kv_norm_g",
           "w_ukv", "attn_proj", "w_out", "post_mix_g", "pre_mlp_g", "w_up", "w_down", "post_mlp_g")


def _params(sem=None, vmem=None):
    return pltpu.CompilerParams(dimension_semantics=sem, vmem_limit_bytes=vmem)


def _tile(dim, pref):
    if dim <= pref:
        return dim
    best = 0
    for t in range(128, pref + 1, 128):
        if dim % t == 0:
            best = t
    return best if best >= 256 else dim


def _sigmoid(x):
    return 1.0 / (1.0 + jnp.exp(-x))


def _gelu(x):
    k = math.sqrt(2.0 / math.pi)
    return 0.5 * x * (1.0 + jnp.tanh(k * (x + 0.044715 * x * x * x)))


def _gelu_grad(x):
    k = math.sqrt(2.0 / math.pi)
    t = jnp.tanh(k * (x + 0.044715 * x * x * x))
    return 0.5 * (1.0 + t) + 0.5 * x * (1.0 - t * t) * k * (1.0 + 3.0 * 0.044715 * x * x)


def _rms(x, g):
    r = lax.rsqrt(jnp.mean(x * x, axis=-1, keepdims=True) + EPS)
    return x * r * g


def _rms_bwd(x, g, dy):
    r = lax.rsqrt(jnp.mean(x * x, axis=-1, keepdims=True) + EPS)
    dyg = dy * g
    dx = r * dyg - x * (r * r * r) * jnp.mean(dyg * x, axis=-1, keepdims=True)
    return dx, dy * x * r


def _ln_stats(x):
    mu = jnp.mean(x, axis=-1, keepdims=True)
    xc = x - mu
    r = lax.rsqrt(jnp.mean(xc * xc, axis=-1, keepdims=True) + EPS)
    return xc * r, r


def _ln_bwd(xh, r, g, dy):
    dxh = dy * g
    return r * (dxh - jnp.mean(dxh, axis=-1, keepdims=True) - xh * jnp.mean(dxh * xh, axis=-1, keepdims=True))


def _rot_half(x, width, off):
    n = x.shape[-1]
    lane = lax.broadcasted_iota(jnp.int32, x.shape, x.ndim - 1) % width
    return jnp.where(lane - off < QK_ROPE // 2, -pltpu.roll(x, n - QK_ROPE // 2, x.ndim - 1),
                     pltpu.roll(x, QK_ROPE // 2, x.ndim - 1))


def _colsum_into(ref, val, first):
    s = jnp.sum(val, axis=0, keepdims=True)

    @pl.when(first)
    def _():
        ref[...] = s

    @pl.when(jnp.logical_not(first))
    def _():
        ref[...] += s


_DIMS = {"nn": ((1,), (0,)), "nt": ((1,), (1,)), "tn": ((0,), (0,))}


def _mm_call(a, b, *, mode, name, grid, kaxis, nk, a_spec, b_spec, o_specs, out_shape, acc_shape,
             extras=(), e_specs=(), epilogue=None, active=None):
    ne, no = len(extras), len(out_shape)

    def body(a_ref, b_ref, *rest):
        e_refs, o_refs, acc_ref = rest[:ne], rest[ne:ne + no], rest[ne + no]
        ids = [pl.program_id(ax) for ax in range(len(grid))]
        k = ids[kaxis]

        def finish(acc):
            outs = (acc,) if epilogue is None else epilogue(acc, *[e[...] for e in e_refs])
            for o_ref, val in zip(o_refs, outs):
                o_ref[...] = val.astype(o_ref.dtype)

        def step():
            prod = lax.dot_general(a_ref[...], b_ref[...], (_DIMS[mode], ((), ())), preferred_element_type=F32)
            if nk == 1:
                finish(prod)
                return

            @pl.when(k == 0)
            def _():
                acc_ref[...] = prod

            @pl.when(k > 0)
            def _():
                acc_ref[...] += prod

        if active is None:
            step()
        else:
            pl.when(active(*ids))(step)
        if nk > 1:
            @pl.when(k == nk - 1)
            def _():
                finish(acc_ref[...])

    sem = tuple("arbitrary" if ax == kaxis else "parallel" for ax in range(len(grid)))
    outs = pl.pallas_call(
        body, name=name, grid=grid, in_specs=[a_spec, b_spec, *e_specs], out_specs=list(o_specs),
        out_shape=list(out_shape), scratch_shapes=[pltpu.VMEM(acc_shape, F32)],
        compiler_params=_params(sem, VMEM_BIG))(a, b, *extras)
    return outs


def _mm(a, b, *, mode, name, out_dtypes=(F32,), extras=(), epilogue=None, tm=1024, tn=1024, tk=1024):
    if mode == "nn":
        (m, k), (_, n) = a.shape, b.shape
    elif mode == "nt":
        (m, k), (n, _) = a.shape, b.shape
    else:
        (k, m), (_, n) = a.shape, b.shape
    tm, tn, tk = _tile(m, tm), _tile(n, tn), _tile(k, tk)
    if tn > 2048:
        tm, tk = _tile(m, 512), _tile(k, 512)
    if tk > 2048:
        tm, tn = _tile(m, 512), _tile(n, 512)
    nk = k // tk
    if mode == "tn":
        a_spec = pl.BlockSpec((tk, tm), lambda i, j, kk: (kk, i))
    else:
        a_spec = pl.BlockSpec((tm, tk), lambda i, j, kk: (i, kk))
    if mode == "nt":
        b_spec = pl.BlockSpec((tn, tk), lambda i, j, kk: (j, kk))
    else:
        b_spec = pl.BlockSpec((tk, tn), lambda i, j, kk: (kk, j))
    def e_spec(e):
        if e.shape[1] == tn and n != tn:
            return pl.BlockSpec((tm, tn), lambda i, j, kk: (i, 0))
        return pl.BlockSpec((tm, tn), lambda i, j, kk: (i, j))

    e_specs = [e_spec(e) for e in extras]
    o_specs = [pl.BlockSpec((tm, tn), lambda i, j, kk: (i, j)) for _ in out_dtypes]
    out_shape = [jax.ShapeDtypeStruct((m, n), dt) for dt in out_dtypes]
    outs = _mm_call(a, b, mode=mode, name=name, grid=(m // tm, n // tn, nk), kaxis=2, nk=nk, a_spec=a_spec,
                    b_spec=b_spec, o_specs=o_specs, out_shape=out_shape, acc_shape=(tm, tn), extras=extras,
                    e_specs=e_specs, epilogue=epilogue)
    return outs[0] if len(outs) == 1 else outs


def _branch_of(kb):
    return jnp.minimum(kb, N_BRANCH - 1)


def _proj_fwd(act_cat, proj_cat, name):
    s = act_cat.shape[0]
    tm, tn = _tile(s, 1024), 1024
    nj = D_MODEL // tn

    def kb(b, k):
        return jnp.where(b < N_BRANCH - 1, b, N_BRANCH - 1 + k)

    out = _mm_call(
        act_cat, proj_cat, mode="nn", name=name, grid=(s // tm, nj, N_BRANCH, 2), kaxis=3, nk=2,
        a_spec=pl.BlockSpec((tm, BR), lambda i, j, b, k: (i, kb(b, k))),
        b_spec=pl.BlockSpec((BR, tn), lambda i, j, b, k: (kb(b, k), j)),
        o_specs=[pl.BlockSpec((tm, tn), lambda i, j, b, k: (i, b * nj + j))],
        out_shape=[jax.ShapeDtypeStruct((s, N_GATE), F32)], acc_shape=(tm, tn),
        active=lambda i, j, b, k: jnp.logical_or(b == N_BRANCH - 1, k == 0))
    return out[0]


def _proj_bwd_act(dy, proj_cat, name):
    s = dy.shape[0]
    tm, tk = _tile(s, 1024), 1024
    nkk = D_MODEL // tk
    nkb = ACT_CAT // BR
    out = _mm_call(
        dy, proj_cat, mode="nt", name=name, grid=(s // tm, nkb, nkk), kaxis=2, nk=nkk,
        a_spec=pl.BlockSpec((tm, tk), lambda i, kb, k: (i, _branch_of(kb) * nkk + k)),
        b_spec=pl.BlockSpec((BR, tk), lambda i, kb, k: (kb, k)),
        o_specs=[pl.BlockSpec((tm, BR), lambda i, kb, k: (i, kb))],
        out_shape=[jax.ShapeDtypeStruct((s, ACT_CAT), F32)], acc_shape=(tm, BR))
    return out[0]


def _proj_bwd_w(act_cat, dy, name):
    s = dy.shape[0]
    tms, tn = _tile(s, 1024), 1024
    nj = D_MODEL // tn
    nkb = ACT_CAT // BR
    nm = s // tms
    out = _mm_call(
        act_cat, dy, mode="tn", name=name, grid=(nkb, nj, nm), kaxis=2, nk=nm,
        a_spec=pl.BlockSpec((tms, BR), lambda kb, j, m: (m, kb)),
        b_spec=pl.BlockSpec((tms, tn), lambda kb, j, m: (m, _branch_of(kb) * nj + j)),
        o_specs=[pl.BlockSpec((BR, tn), lambda kb, j, m: (kb, j))],
        out_shape=[jax.ShapeDtypeStruct((ACT_CAT, D_MODEL), BF16)], acc_shape=(BR, tn))
    return out[0]


def _row_specs(ts, n_full, n_vec):
    return ([pl.BlockSpec((ts, D_MODEL), lambda i: (i, 0))] * n_full
            + [pl.BlockSpec((1, D_MODEL), lambda i: (0, 0))] * n_vec)


def _pre_norm(x, g, name):
    s = x.shape[0]
    ts = _tile(s, 256)

    def body(x_ref, g_ref, h_ref):
        h_ref[...] = _rms(x_ref[...], g_ref[...]).astype(BF16)

    return pl.pallas_call(body, name=name, grid=(s // ts,), in_specs=_row_specs(ts, 1, 1),
                          out_specs=pl.BlockSpec((ts, D_MODEL), lambda i: (i, 0)),
                          out_shape=jax.ShapeDtypeStruct((s, D_MODEL), BF16), compiler_params=_params(("parallel",)))(x, g)


def _post_pre(x, r, g_post, g_next, name):
    s = x.shape[0]
    ts = _tile(s, 256)

    def body(x_ref, r_ref, gp_ref, gn_ref, xn_ref, h_ref):
        xn = x_ref[...] + _rms(r_ref[...], gp_ref[...])
        xn_ref[...] = xn
        h_ref[...] = _rms(xn, gn_ref[...]).astype(BF16)

    spec = pl.BlockSpec((ts, D_MODEL), lambda i: (i, 0))
    return pl.pallas_call(body, name=name, grid=(s // ts,), in_specs=_row_specs(ts, 2, 2), out_specs=[spec, spec],
                          out_shape=[jax.ShapeDtypeStruct((s, D_MODEL), F32), jax.ShapeDtypeStruct((s, D_MODEL), BF16)],
                          compiler_params=_params(("parallel",)))(x, r, g_post, g_next)


def _final_loss(x, r, g_post, target, name):
    s = x.shape[0]
    ts = _tile(s, 256)

    def body(x_ref, r_ref, gp_ref, t_ref, dy_ref, dr_ref, dg_ref, loss_ref):
        first = pl.program_id(0) == 0
        rv, gp = r_ref[...], gp_ref[...]
        diff = x_ref[...] + _rms(rv, gp) - t_ref[...]
        part = 0.5 * jnp.sum(jnp.mean(diff * diff, axis=-1, keepdims=True), axis=0, keepdims=True)
        dy = diff * (1.0 / D_MODEL)
        dy_ref[...] = dy
        dr, dg_rows = _rms_bwd(rv, gp, dy)
        dr_ref[...] = dr.astype(BF16)
        _colsum_into(dg_ref, dg_rows, first)
        _colsum_into(loss_ref, jnp.broadcast_to(part, (1, 128)), first)

    spec = pl.BlockSpec((ts, D_MODEL), lambda i: (i, 0))
    vec = pl.BlockSpec((1, D_MODEL), lambda i: (0, 0))
    return pl.pallas_call(
        body, name=name, grid=(s // ts,), in_specs=[spec, spec, vec, spec],
        out_specs=[spec, spec, vec, pl.BlockSpec((1, 128), lambda i: (0, 0))],
        out_shape=[jax.ShapeDtypeStruct((s, D_MODEL), F32), jax.ShapeDtypeStruct((s, D_MODEL), BF16),
                   jax.ShapeDtypeStruct((1, D_MODEL), F32), jax.ShapeDtypeStruct((1, 128), F32)],
        compiler_params=_params(("arbitrary",)))(x, r, g_post, target)


def _pre_bwd(dh, x, g_pre, dx_res, name, r_prev=None, g_post_prev=None):
    s = x.shape[0]
    ts = _tile(s, 256)
    chain = r_prev is not None

    def body(*refs):
        if chain:
            dh_ref, x_ref, res_ref, r_ref, g_ref, gp_ref, dx_ref, dr_ref, dg_ref, dgp_ref = refs
        else:
            dh_ref, x_ref, res_ref, g_ref, dx_ref, dg_ref = refs
        first = pl.program_id(0) == 0
        dxn, dg_rows = _rms_bwd(x_ref[...], g_ref[...], dh_ref[...])
        dx = res_ref[...] + dxn
        dx_ref[...] = dx
        _colsum_into(dg_ref, dg_rows, first)
        if chain:
            dr, dgp_rows = _rms_bwd(r_ref[...], gp_ref[...], dx)
            dr_ref[...] = dr.astype(BF16)
            _colsum_into(dgp_ref, dgp_rows, first)

    spec = pl.BlockSpec((ts, D_MODEL), lambda i: (i, 0))
    vec = pl.BlockSpec((1, D_MODEL), lambda i: (0, 0))
    full = jax.ShapeDtypeStruct((s, D_MODEL), F32)
    vshape = jax.ShapeDtypeStruct((1, D_MODEL), F32)
    if chain:
        return pl.pallas_call(
            body, name=name, grid=(s // ts,), in_specs=[spec] * 4 + [vec] * 2, out_specs=[spec, spec, vec, vec],
            out_shape=[full, jax.ShapeDtypeStruct((s, D_MODEL), BF16), vshape, vshape],
            compiler_params=_params(("arbitrary",)))(dh, x, dx_res, r_prev, g_pre, g_post_prev)
    return pl.pallas_call(
        body, name=name, grid=(s // ts,), in_specs=[spec] * 3 + [vec], out_specs=[spec, vec],
        out_shape=[full, vshape], compiler_params=_params(("arbitrary",)))(dh, x, dx_res, g_pre)


def _merge_fwd(y, zg, name):
    s = y.shape[0]
    ts, tc = _tile(s, 512), 512
    nj = D_MODEL // tc

    def body(y_ref, z_ref, o_ref, acc_ref):
        b = pl.program_id(2)
        val = _sigmoid(z_ref[...]) * y_ref[...]

        @pl.when(b == 0)
        def _():
            acc_ref[...] = val

        @pl.when(b > 0)
        def _():
            acc_ref[...] += val

        @pl.when(b == N_BRANCH - 1)
        def _():
            o_ref[...] = acc_ref[...].astype(BF16)

    blk = pl.BlockSpec((ts, tc), lambda i, j, b: (i, b * nj + j))
    return pl.pallas_call(
        body, name=name, grid=(s // ts, nj, N_BRANCH), in_specs=[blk, blk],
        out_specs=pl.BlockSpec((ts, tc), lambda i, j, b: (i, j)), out_shape=jax.ShapeDtypeStruct((s, D_MODEL), BF16),
        scratch_shapes=[pltpu.VMEM((ts, tc), F32)],
        compiler_params=_params(("parallel", "parallel", "arbitrary")))(y, zg)


def _merge_bwd(dm, y, zg, name):
    s = y.shape[0]
    ts, tc = _tile(s, 512), 512
    nj = D_MODEL // tc

    def body(dm_ref, y_ref, z_ref, dy_ref, dz_ref):
        g = _sigmoid(z_ref[...])
        d = dm_ref[...]
        dy_ref[...] = (d * g).astype(BF16)
        dz_ref[...] = (d * y_ref[...] * g * (1.0 - g)).astype(BF16)

    blk = pl.BlockSpec((ts, tc), lambda i, j, b: (i, b * nj + j))
    shape = jax.ShapeDtypeStruct((s, N_GATE), BF16)
    return pl.pallas_call(
        body, name=name, grid=(s // ts, nj, N_BRANCH),
        in_specs=[pl.BlockSpec((ts, tc), lambda i, j, b: (i, j)), blk, blk], out_specs=[blk, blk],
        out_shape=[shape, shape], compiler_params=_params(("parallel", "parallel", "parallel")))(dm, y, zg)


POOL_HALO = 16


def _pool_windows(ext_ref, ts, first_row):
    outs = []
    t = first_row + lax.broadcasted_iota(jnp.int32, (ts, GDIM), 0)
    for gi, w in enumerate(POOL_WINDOWS):
        cols = pl.ds(gi * GDIM, GDIM)
        acc = ext_ref[pl.ds(POOL_HALO, ts), cols]
        cur = acc
        for k in range(1, w):
            acc = acc + ext_ref[pl.ds(POOL_HALO - k, ts), cols]
        cnt = jnp.minimum(t + 1, w).astype(F32)
        outs.append(acc / cnt - cur)
    return outs


def _pool_fwd(za, pool_w, pool_scale, name):
    s = za.shape[0]
    ts = _tile(s, 512)
    hb = ts // POOL_HALO

    def body(a_ref, halo_ref, w_ref, sc_ref, o_ref, ext_ref):
        i = pl.program_id(0)
        ext_ref[pl.ds(0, POOL_HALO), :] = jnp.where(i > 0, halo_ref[...], 0.0)
        ext_ref[pl.ds(POOL_HALO, ts), :] = a_ref[...]
        pooled = _pool_windows(ext_ref, ts, i * ts)
        for gi in range(len(POOL_WINDOWS)):
            mixed = jnp.dot(pooled[gi].astype(BF16), w_ref[gi].astype(BF16), preferred_element_type=F32)
            o_ref[:, pl.ds(gi * GDIM, GDIM)] = (mixed * sc_ref[:, pl.ds(gi * GDIM, GDIM)]).astype(BF16)

    return pl.pallas_call(
        body, name=name, grid=(s // ts,),
        in_specs=[pl.BlockSpec((ts, BR), lambda i: (i, 0)),
                  pl.BlockSpec((POOL_HALO, BR), lambda i: (jnp.maximum(i * hb - 1, 0), 0)),
                  pl.BlockSpec((4, GDIM, GDIM), lambda i: (0, 0, 0)), pl.BlockSpec((1, BR), lambda i: (0, 0))],
        out_specs=pl.BlockSpec((ts, BR), lambda i: (i, 0)), out_shape=jax.ShapeDtypeStruct((s, BR), BF16),
        scratch_shapes=[pltpu.VMEM((POOL_HALO + ts, BR), F32)], compiler_params=_params(("parallel",)))(
            za, za, pool_w, pool_scale)


def _pool_bwd(dact, za, pool_w, pool_scale, name):
    s = za.shape[0]
    ts = _tile(s, 512)
    hb = ts // POOL_HALO
    n_t = s // ts

    def body(d_ref, dhalo_ref, a_ref, halo_ref, w_ref, sc_ref, dz_ref, dw_ref, dsc_ref, ext_ref, f_ref):
        i = pl.program_id(0)
        first = i == 0
        ext_ref[pl.ds(0, POOL_HALO), :] = jnp.where(i > 0, halo_ref[...], 0.0)
        ext_ref[pl.ds(POOL_HALO, ts), :] = a_ref[...]
        pooled = _pool_windows(ext_ref, ts, i * ts)
        d_tile = d_ref[...]
        d_next = jnp.where(i < n_t - 1, dhalo_ref[...], 0.0)
        t_ext = i * ts + lax.broadcasted_iota(jnp.int32, (ts + POOL_HALO, GDIM), 0)
        dsc = []
        for gi, w in enumerate(POOL_WINDOWS):
            cols = pl.ds(gi * GDIM, GDIM)
            wg = w_ref[gi].astype(BF16)
            sc = sc_ref[:, cols]
            pg = pooled[gi].astype(BF16)
            mixed = jnp.dot(pg, wg, preferred_element_type=F32)
            dsc.append(jnp.sum(d_tile[:, gi * GDIM:(gi + 1) * GDIM] * mixed, axis=0, keepdims=True))
            dmix = jnp.concatenate([d_tile[:, gi * GDIM:(gi + 1) * GDIM], d_next[:, gi * GDIM:(gi + 1) * GDIM]], axis=0) * sc
            dmix = dmix.astype(BF16)
            dwg = lax.dot_general(pg, dmix[:ts], (((0,), (0,)), ((), ())), preferred_element_type=F32)

            @pl.when(first)
            def _():
                dw_ref[gi] = dwg

            @pl.when(jnp.logical_not(first))
            def _():
                dw_ref[gi] += dwg

            dpool = lax.dot_general(dmix, wg, (((1,), (1,)), ((), ())), preferred_element_type=F32)
            f_ref[:, cols] = dpool / jnp.minimum(t_ext + 1, w).astype(F32)
            acc = f_ref[pl.ds(0, ts), cols]
            for k in range(1, w):
                acc = acc + f_ref[pl.ds(k, ts), cols]
            dz_ref[:, cols] = (acc - dpool[:ts]).astype(BF16)
        dsc_all = jnp.concatenate(dsc, axis=1)

        @pl.when(first)
        def _():
            dsc_ref[...] = dsc_all

        @pl.when(jnp.logical_not(first))
        def _():
            dsc_ref[...] += dsc_all

    n_hb = s // POOL_HALO
    return pl.pallas_call(
        body, name=name, grid=(n_t,),
        in_specs=[pl.BlockSpec((ts, BR), lambda i: (i, 0)),
                  pl.BlockSpec((POOL_HALO, BR), lambda i: (jnp.minimum((i + 1) * hb, n_hb - 1), 0)),
                  pl.BlockSpec((ts, BR), lambda i: (i, 0)),
                  pl.BlockSpec((POOL_HALO, BR), lambda i: (jnp.maximum(i * hb - 1, 0), 0)),
                  pl.BlockSpec((4, GDIM, GDIM), lambda i: (0, 0, 0)), pl.BlockSpec((1, BR), lambda i: (0, 0))],
        out_specs=[pl.BlockSpec((ts, BR), lambda i: (i, 0)), pl.BlockSpec((4, GDIM, GDIM), lambda i: (0, 0, 0)),
                   pl.BlockSpec((1, BR), lambda i: (0, 0))],
        out_shape=[jax.ShapeDtypeStruct((s, BR), BF16), jax.ShapeDtypeStruct((4, GDIM, GDIM), F32),
                   jax.ShapeDtypeStruct((1, BR), F32)],
        scratch_shapes=[pltpu.VMEM((POOL_HALO + ts, BR), F32), pltpu.VMEM((ts + POOL_HALO, BR), F32)],
        compiler_params=_params(("arbitrary",)))(dact, dact, za, za, pool_w, pool_scale)


CONV_HALO = 32
CONV_LEAD = CONV_HALO - (CONV_WIDTH - 1)


def _conv_fwd(za, conv_w, conv_b, ng, nb, name):
    s = za.shape[0]
    ts = _tile(s, 512)
    hb = ts // CONV_HALO

    def body(a_ref, g_ref, ah_ref, gh_ref, w_ref, b_ref, ng_ref, nb_ref, yc_ref, act_ref, ext_ref):
        i = pl.program_id(0)
        ext_ref[pl.ds(0, CONV_HALO), :] = jnp.where(i > 0, ah_ref[...] * _sigmoid(gh_ref[...]), 0.0)
        ext_ref[pl.ds(CONV_HALO, ts), :] = a_ref[...] * _sigmoid(g_ref[...])
        acc = jnp.zeros((ts, BR), F32) + b_ref[...]
        for k in range(CONV_WIDTH):
            acc = acc + w_ref[pl.ds(k, 1), :] * ext_ref[pl.ds(CONV_LEAD + k, ts), :]
        yc_ref[...] = acc
        xh, _ = _ln_stats(acc)
        ln = xh * ng_ref[...] + nb_ref[...]
        act_ref[...] = (ln * _sigmoid(ln)).astype(BF16)

    tile = lambda c: pl.BlockSpec((ts, BR), lambda i: (i, c))
    halo = lambda c: pl.BlockSpec((CONV_HALO, BR), lambda i: (jnp.maximum(i * hb - 1, 0), c))
    vec = pl.BlockSpec((1, BR), lambda i: (0, 0))
    return pl.pallas_call(
        body, name=name, grid=(s // ts,),
        in_specs=[tile(1), tile(2), halo(1), halo(2), pl.BlockSpec((CONV_WIDTH, BR), lambda i: (0, 0)), vec, vec, vec],
        out_specs=[pl.BlockSpec((ts, BR), lambda i: (i, 0))] * 2,
        out_shape=[jax.ShapeDtypeStruct((s, BR), F32), jax.ShapeDtypeStruct((s, BR), BF16)],
        scratch_shapes=[pltpu.VMEM((CONV_HALO + ts, BR), F32)], compiler_params=_params(("parallel",)))(
            za, za, za, za, conv_w, conv_b, ng, nb)


def _conv_bwd_norm(dact, yc, ng, nb, name):
    s = yc.shape[0]
    ts = _tile(s, 512)

    def body(d_ref, y_ref, ng_ref, nb_ref, dy_ref, db_ref, dng_ref, dnb_ref):
        first = pl.program_id(0) == 0
        xh, r = _ln_stats(y_ref[...])
        g = ng_ref[...]
        ln = xh * g + nb_ref[...]
        sg = _sigmoid(ln)
        dln = d_ref[...] * sg * (1.0 + ln * (1.0 - sg))
        dy = _ln_bwd(xh, r, g, dln)
        dy_ref[...] = dy
        _colsum_into(db_ref, dy, first)
        _colsum_into(dng_ref, dln * xh, first)
        _colsum_into(dnb_ref, dln, first)

    vec = pl.BlockSpec((1, BR), lambda i: (0, 0))
    vshape = jax.ShapeDtypeStruct((1, BR), F32)
    return pl.pallas_call(
        body, name=name, grid=(s // ts,),
        in_specs=[pl.BlockSpec((ts, BR), lambda i: (i, 1)), pl.BlockSpec((ts, BR), lambda i: (i, 0)), vec, vec],
        out_specs=[pl.BlockSpec((ts, BR), lambda i: (i, 0)), vec, vec, vec],
        out_shape=[jax.ShapeDtypeStruct((s, BR), F32), vshape, vshape, vshape],
        compiler_params=_params(("arbitrary",)))(dact, yc, ng, nb)


def _conv_bwd_taps(dyc, za, conv_w, name):
    s = za.shape[0]
    ts = _tile(s, 512)
    hb = ts // CONV_HALO
    n_t = s // ts
    n_hb = s // CONV_HALO

    def body(d_ref, dh_ref, a_ref, g_ref, ah_ref, gh_ref, w_ref, dz_ref, dw_ref, ext_ref, f_ref):
        i = pl.program_id(0)
        first = i == 0
        a, sg = a_ref[...], _sigmoid(g_ref[...])
        ext_ref[pl.ds(0, CONV_HALO), :] = jnp.where(i > 0, ah_ref[...] * _sigmoid(gh_ref[...]), 0.0)
        ext_ref[pl.ds(CONV_HALO, ts), :] = a * sg
        d = d_ref[...]
        f_ref[pl.ds(0, ts), :] = d
        f_ref[pl.ds(ts, CONV_HALO), :] = jnp.where(i < n_t - 1, dh_ref[...], 0.0)
        dglu = jnp.zeros((ts, BR), F32)
        rows = []
        for k in range(CONV_WIDTH):
            rows.append(jnp.sum(d * ext_ref[pl.ds(CONV_LEAD + k, ts), :], axis=0, keepdims=True))
            dglu = dglu + w_ref[pl.ds(k, 1), :] * f_ref[pl.ds(CONV_WIDTH - 1 - k, ts), :]
        rows.append(jnp.zeros((1, BR), F32))
        dw = jnp.concatenate(rows, axis=0)

        @pl.when(first)
        def _():
            dw_ref[...] = dw

        @pl.when(jnp.logical_not(first))
        def _():
            dw_ref[...] += dw

        dz_ref[:, pl.ds(0, BR)] = (dglu * sg).astype(BF16)
        dz_ref[:, pl.ds(BR, BR)] = (dglu * a * sg * (1.0 - sg)).astype(BF16)

    tile = lambda c: pl.BlockSpec((ts, BR), lambda i: (i, c))
    halo = lambda c: pl.BlockSpec((CONV_HALO, BR), lambda i: (jnp.maximum(i * hb - 1, 0), c))
    return pl.pallas_call(
        body, name=name, grid=(n_t,),
        in_specs=[pl.BlockSpec((ts, BR), lambda i: (i, 0)),
                  pl.BlockSpec((CONV_HALO, BR), lambda i: (jnp.minimum((i + 1) * hb, n_hb - 1), 0)),
                  tile(1), tile(2), halo(1), halo(2), pl.BlockSpec((CONV_WIDTH, BR), lambda i: (0, 0))],
        out_specs=[pl.BlockSpec((ts, 2 * BR), lambda i: (i, 0)), pl.BlockSpec((CONV_WIDTH + 1, BR), lambda i: (0, 0))],
        out_shape=[jax.ShapeDtypeStruct((s, 2 * BR), BF16), jax.ShapeDtypeStruct((CONV_WIDTH + 1, BR), F32)],
        scratch_shapes=[pltpu.VMEM((CONV_HALO + ts, BR), F32), pltpu.VMEM((ts + CONV_HALO, BR), F32)],
        compiler_params=_params(("arbitrary",)))(dyc, dyc, za, za, za, za, conv_w)


def _tril(w):
    r = lax.broadcasted_iota(jnp.int32, (CHUNK, CHUNK), 0)
    c = lax.broadcasted_iota(jnp.int32, (CHUNK, CHUNK), 1)
    return jnp.where(c <= r, w, 0.0)


def _sgu_fwd(za, ng, nb, sgu_w, bias_b, name):
    s = za.shape[0]
    ts = _tile(s, 512)

    def body(u_ref, v_ref, ng_ref, nb_ref, w_ref, b_ref, o_ref):
        u = _gelu(u_ref[...])
        xh, _ = _ln_stats(_gelu(v_ref[...]))
        vln = (xh * ng_ref[...] + nb_ref[...]).astype(BF16)
        for gi in range(4):
            wg = _tril(w_ref[gi]).astype(BF16)
            for n in range(ts // CHUNK):
                blk = vln[n * CHUNK:(n + 1) * CHUNK, gi * GDIM:(gi + 1) * GDIM]
                sp = jnp.dot(wg, blk, preferred_element_type=F32) + b_ref[gi]
                o_ref[pl.ds(n * CHUNK, CHUNK), pl.ds(gi * GDIM, GDIM)] = (
                    u[n * CHUNK:(n + 1) * CHUNK, gi * GDIM:(gi + 1) * GDIM] * sp).astype(BF16)

    vec = pl.BlockSpec((1, BR), lambda i: (0, 0))
    cube = pl.BlockSpec((4, CHUNK, GDIM), lambda i: (0, 0, 0))
    return pl.pallas_call(
        body, name=name, grid=(s // ts,),
        in_specs=[pl.BlockSpec((ts, BR), lambda i: (i, 3)), pl.BlockSpec((ts, BR), lambda i: (i, 4)), vec, vec, cube, cube],
        out_specs=pl.BlockSpec((ts, BR), lambda i: (i, 0)), out_shape=jax.ShapeDtypeStruct((s, BR), BF16),
        compiler_params=_params(("parallel",)))(za, za, ng, nb, sgu_w, bias_b)


def _sgu_bwd(dact, za, ng, nb, sgu_w, bias_b, name):
    s = za.shape[0]
    ts = _tile(s, 512)

    def body(d_ref, u_ref, v_ref, ng_ref, nb_ref, w_ref, b_ref, dz_ref, dw_ref, db_ref, dng_ref, dnb_ref, dv_ref):
        first = pl.program_id(0) == 0
        u_raw, v_raw = u_ref[...], v_ref[...]
        u = _gelu(u_raw)
        xh, r = _ln_stats(_gelu(v_raw))
        g = ng_ref[...]
        vln = (xh * g + nb_ref[...]).astype(BF16)
        d = d_ref[...]
        dsp = d * u
        dsp16 = dsp.astype(BF16)
        for gi in range(4):
            wg = _tril(w_ref[gi]).astype(BF16)
            dwg = jnp.zeros((CHUNK, CHUNK), F32)
            dbg = jnp.zeros((CHUNK, 1), F32)
            for n in range(ts // CHUNK):
                rows, cols = slice(n * CHUNK, (n + 1) * CHUNK), slice(gi * GDIM, (gi + 1) * GDIM)
                blk = vln[rows, cols]
                sp = jnp.dot(wg, blk, preferred_element_type=F32) + b_ref[gi]
                dz_ref[pl.ds(n * CHUNK, CHUNK), pl.ds(gi * GDIM, GDIM)] = (
                    d[rows, cols] * sp * _gelu_grad(u_raw[rows, cols])).astype(BF16)
                dv_ref[pl.ds(n * CHUNK, CHUNK), pl.ds(gi * GDIM, GDIM)] = lax.dot_general(
                    wg, dsp16[rows, cols], (((0,), (0,)), ((), ())), preferred_element_type=F32)
                dwg = dwg + lax.dot_general(dsp16[rows, cols], blk, (((1,), (1,)), ((), ())), preferred_element_type=F32)
                dbg = dbg + jnp.sum(dsp[rows, cols], axis=1, keepdims=True)
            dwg = _tril(dwg)

            @pl.when(first)
            def _():
                dw_ref[gi] = dwg
                db_ref[gi] = dbg

            @pl.when(jnp.logical_not(first))
            def _():
                dw_ref[gi] += dwg
                db_ref[gi] += dbg

        dvln = dv_ref[...]
        dz_ref[:, pl.ds(BR, BR)] = (_ln_bwd(xh, r, g, dvln) * _gelu_grad(v_raw)).astype(BF16)
        _colsum_into(dng_ref, dvln * xh, first)
        _colsum_into(dnb_ref, dvln, first)

    vec = pl.BlockSpec((1, BR), lambda i: (0, 0))
    cube = pl.BlockSpec((4, CHUNK, GDIM), lambda i: (0, 0, 0))
    vshape = jax.ShapeDtypeStruct((1, BR), F32)
    return pl.pallas_call(
        body, name=name, grid=(s // ts,),
        in_specs=[pl.BlockSpec((ts, BR), lambda i: (i, 2)), pl.BlockSpec((ts, BR), lambda i: (i, 3)),
                  pl.BlockSpec((ts, BR), lambda i: (i, 4)), vec, vec, cube, cube],
        out_specs=[pl.BlockSpec((ts, 2 * BR), lambda i: (i, 0)), cube, pl.BlockSpec((4, CHUNK, 1), lambda i: (0, 0, 0)),
                   vec, vec],
        out_shape=[jax.ShapeDtypeStruct((s, 2 * BR), BF16), jax.ShapeDtypeStruct((4, CHUNK, CHUNK), F32),
                   jax.ShapeDtypeStruct((4, CHUNK, 1), F32), vshape, vshape],
        scratch_shapes=[pltpu.VMEM((ts, BR), F32)], compiler_params=_params(("arbitrary",)))(
            dact, za, za, ng, nb, sgu_w, bias_b)


KR_BLOCK = 3584 // 128


def _mla_prep(za, qg, kvg, ck, sk, name):
    s = za.shape[0]
    ts = _tile(s, 512)

    def body(cq_ref, ckv_ref, kr_ref, qg_ref, kvg_ref, c_ref, s_ref, qn_ref, kvn_ref, krr_ref):
        qn_ref[...] = _rms(cq_ref[...], qg_ref[...]).astype(BF16)
        kvn_ref[...] = _rms(ckv_ref[...], kvg_ref[...]).astype(BF16)
        kr = kr_ref[...]
        krr_ref[...] = (kr * c_ref[...] + _rot_half(kr, 128, 0) * s_ref[...]).astype(BF16)

    vec = pl.BlockSpec((1, BR), lambda i: (0, 0))
    tab = pl.BlockSpec((ts, 128), lambda i: (i, 0))
    row = pl.BlockSpec((ts, BR), lambda i: (i, 0))
    return pl.pallas_call(
        body, name=name, grid=(s // ts,),
        in_specs=[pl.BlockSpec((ts, BR), lambda i: (i, 5)), pl.BlockSpec((ts, BR), lambda i: (i, 6)),
                  pl.BlockSpec((ts, 128), lambda i: (i, KR_BLOCK)), vec, vec, tab, tab],
        out_specs=[row, row, tab],
        out_shape=[jax.ShapeDtypeStruct((s, BR), BF16), jax.ShapeDtypeStruct((s, BR), BF16),
                   jax.ShapeDtypeStruct((s, 128), BF16)],
        compiler_params=_params(("parallel",)))(za, za, za, qg, kvg, ck, sk)


def _mla_prep_bwd(dqn, dkvn, dkr_heads, za, qg, kvg, ck, sk, name):
    s = za.shape[0]
    ts = _tile(s, 512)

    def body(dq_ref, dkv_ref, dkr_ref, cq_ref, ckv_ref, qg_ref, kvg_ref, c_ref, s_ref, dz_ref, dqg_ref, dkvg_ref):
        first = pl.program_id(0) == 0
        dcq, rows_q = _rms_bwd(cq_ref[...], qg_ref[...], dq_ref[...])
        dckv, rows_kv = _rms_bwd(ckv_ref[...], kvg_ref[...], dkv_ref[...])
        dz_ref[:, pl.ds(0, BR)] = dcq.astype(BF16)
        dz_ref[:, pl.ds(BR, BR)] = dckv.astype(BF16)
        dk = dkr_ref[:, pl.ds(0, 128)]
        for h in range(1, MLA_HEADS):
            dk = dk + dkr_ref[:, pl.ds(h * 128, 128)]
        dz_ref[:, pl.ds(2 * BR, 128)] = (dk * c_ref[...] - _rot_half(dk, 128, 0) * s_ref[...]).astype(BF16)
        _colsum_into(dqg_ref, rows_q, first)
        _colsum_into(dkvg_ref, rows_kv, first)

    vec = pl.BlockSpec((1, BR), lambda i: (0, 0))
    tab = pl.BlockSpec((ts, 128), lambda i: (i, 0))
    row = pl.BlockSpec((ts, BR), lambda i: (i, 0))
    wide = 2 * BR + 128
    vshape = jax.ShapeDtypeStruct((1, BR), F32)
    return pl.pallas_call(
        body, name=name, grid=(s // ts,),
        in_specs=[row, row, pl.BlockSpec((ts, MLA_HEADS * 128), lambda i: (i, 0)),
                  pl.BlockSpec((ts, BR), lambda i: (i, 5)), pl.BlockSpec((ts, BR), lambda i: (i, 6)), vec, vec, tab, tab],
        out_specs=[pl.BlockSpec((ts, wide), lambda i: (i, 0)), vec, vec],
        out_shape=[jax.ShapeDtypeStruct((s, wide), BF16), vshape, vshape],
        compiler_params=_params(("arbitrary",)))(dqn, dkvn, dkr_heads, za, za, qg, kvg, ck, sk)


def _attn_tiles(s):
    t = _tile(s, 512)
    return t, s // t


def _causal(qi, ki, t):
    row = qi * t + lax.broadcasted_iota(jnp.int32, (t, t), 0)
    col = ki * t + lax.broadcasted_iota(jnp.int32, (t, t), 1)
    return col <= row


def _flash_fwd(q, kv, krr, name):
    s = q.shape[0]
    t, n = _attn_tiles(s)

    def body(q_ref, kn_ref, kr_ref, v_ref, o_ref, lse_ref, m_sc, l_sc, acc_sc):
        qi, ki = pl.program_id(1), pl.program_id(2)

        @pl.when(ki == 0)
        def _():
            m_sc[...] = jnp.full((t, 1), NEG, F32)
            l_sc[...] = jnp.zeros((t, 1), F32)
            acc_sc[...] = jnp.zeros((t, V_DIM), F32)

        @pl.when(ki <= qi)
        def _():
            k = jnp.concatenate([kn_ref[...], kr_ref[...]], axis=1)
            sc = lax.dot_general(q_ref[...], k, (((1,), (1,)), ((), ())), preferred_element_type=F32) * ATT_SCALE
            sc = jnp.where(_causal(qi, ki, t), sc, NEG)
            m_prev = m_sc[...]
            m_new = jnp.maximum(m_prev, jnp.max(sc, axis=1, keepdims=True))
            alpha = jnp.exp(m_prev - m_new)
            p = jnp.exp(sc - m_new)
            l_sc[...] = alpha * l_sc[...] + jnp.sum(p, axis=1, keepdims=True)
            acc_sc[...] = alpha * acc_sc[...] + jnp.dot(p.astype(BF16), v_ref[...], preferred_element_type=F32)
            m_sc[...] = m_new

        @pl.when(ki == n - 1)
        def _():
            o_ref[...] = (acc_sc[...] / l_sc[...]).astype(BF16)
            lse_ref[...] = jnp.broadcast_to(m_sc[...] + jnp.log(l_sc[...]), (t, 128))

    kmap = lambda c: (lambda h, qi, ki: (jnp.minimum(ki, qi), c(h)))
    out_blk = pl.BlockSpec((t, 128), lambda h, qi, ki: (qi, h))
    return pl.pallas_call(
        body, name=name, grid=(MLA_HEADS, n, n),
        in_specs=[pl.BlockSpec((t, QW), lambda h, qi, ki: (qi, h)), pl.BlockSpec((t, 128), kmap(lambda h: 2 * h)),
                  pl.BlockSpec((t, 128), kmap(lambda h: 0)), pl.BlockSpec((t, 128), kmap(lambda h: 2 * h + 1))],
        out_specs=[out_blk, out_blk],
        out_shape=[jax.ShapeDtypeStruct((s, MLA_HEADS * V_DIM), BF16), jax.ShapeDtypeStruct((s, MLA_HEADS * 128), F32)],
        scratch_shapes=[pltpu.VMEM((t, 1), F32), pltpu.VMEM((t, 1), F32), pltpu.VMEM((t, V_DIM), F32)],
        compiler_params=_params(("parallel", "parallel", "arbitrary")))(q, kv, krr, kv)


def _flash_probs(q_ref, kn_ref, kr_ref, v_ref, do_ref, o_ref, lse_ref, qi, ki, t):
    k = jnp.concatenate([kn_ref[...], kr_ref[...]], axis=1)
    q = q_ref[...]
    sc = lax.dot_general(q, k, (((1,), (1,)), ((), ())), preferred_element_type=F32) * ATT_SCALE
    lse = jnp.max(lse_ref[...], axis=1, keepdims=True)
    p = jnp.where(_causal(qi, ki, t), jnp.exp(sc - lse), 0.0)
    do = do_ref[...]
    delta = jnp.sum(do.astype(F32) * o_ref[...].astype(F32), axis=1, keepdims=True)
    dp = lax.dot_general(do, v_ref[...], (((1,), (1,)), ((), ())), preferred_element_type=F32)
    ds = (p * (dp - delta) * ATT_SCALE).astype(BF16)
    return q, k, p, do, ds


def _flash_bwd_kv(q, kv, krr, do, o, lse, name):
    s = q.shape[0]
    t, n = _attn_tiles(s)

    def body(q_ref, kn_ref, kr_ref, v_ref, do_ref, o_ref, lse_ref, dkv_ref, dkr_ref, dk_sc, dv_sc):
        ki, qi = pl.program_id(1), pl.program_id(2)

        @pl.when(qi == 0)
        def _():
            dk_sc[...] = jnp.zeros((t, QW), F32)
            dv_sc[...] = jnp.zeros((t, V_DIM), F32)

        @pl.when(qi >= ki)
        def _():
            qv, _, p, dov, ds = _flash_probs(q_ref, kn_ref, kr_ref, v_ref, do_ref, o_ref, lse_ref, qi, ki, t)
            dv_sc[...] += lax.dot_general(p.astype(BF16), dov, (((0,), (0,)), ((), ())), preferred_element_type=F32)
            dk_sc[...] += lax.dot_general(ds, qv, (((0,), (0,)), ((), ())), preferred_element_type=F32)

        @pl.when(qi == n - 1)
        def _():
            dkv_ref[:, pl.ds(0, 128)] = dk_sc[:, pl.ds(0, 128)].astype(BF16)
            dkv_ref[:, pl.ds(128, 128)] = dv_sc[...].astype(BF16)
            dkr_ref[...] = dk_sc[:, pl.ds(128, 128)]

    qmap = lambda c: (lambda h, ki, qi: (jnp.maximum(qi, ki), c(h)))
    kmap = lambda c: (lambda h, ki, qi: (ki, c(h)))
    return pl.pallas_call(
        body, name=name, grid=(MLA_HEADS, n, n),
        in_specs=[pl.BlockSpec((t, QW), qmap(lambda h: h)), pl.BlockSpec((t, 128), kmap(lambda h: 2 * h)),
                  pl.BlockSpec((t, 128), kmap(lambda h: 0)), pl.BlockSpec((t, 128), kmap(lambda h: 2 * h + 1)),
                  pl.BlockSpec((t, 128), qmap(lambda h: h)), pl.BlockSpec((t, 128), qmap(lambda h: h)),
                  pl.BlockSpec((t, 128), qmap(lambda h: h))],
        out_specs=[pl.BlockSpec((t, QW), kmap(lambda h: h)), pl.BlockSpec((t, 128), kmap(lambda h: h))],
        out_shape=[jax.ShapeDtypeStruct((s, MLA_HEADS * QW), BF16), jax.ShapeDtypeStruct((s, MLA_HEADS * 128), F32)],
        scratch_shapes=[pltpu.VMEM((t, QW), F32), pltpu.VMEM((t, V_DIM), F32)],
        compiler_params=_params(("parallel", "parallel", "arbitrary")))(q, kv, krr, kv, do, o, lse)


def _flash_bwd_q(q, kv, krr, do, o, lse, cq, sq, name):
    s = q.shape[0]
    t, n = _attn_tiles(s)

    def body(q_ref, kn_ref, kr_ref, v_ref, do_ref, o_ref, lse_ref, c_ref, s_ref, dq_ref, dq_sc):
        qi, ki = pl.program_id(1), pl.program_id(2)

        @pl.when(ki == 0)
        def _():
            dq_sc[...] = jnp.zeros((t, QW), F32)

        @pl.when(ki <= qi)
        def _():
            _, k, _, _, ds = _flash_probs(q_ref, kn_ref, kr_ref, v_ref, do_ref, o_ref, lse_ref, qi, ki, t)
            dq_sc[...] += jnp.dot(ds, k, preferred_element_type=F32)

        @pl.when(ki == n - 1)
        def _():
            dq = dq_sc[...]
            dq_ref[...] = (dq * c_ref[...] - _rot_half(dq, QW, QK_NOPE) * s_ref[...]).astype(BF16)

    qmap = lambda c: (lambda h, qi, ki: (qi, c(h)))
    kmap = lambda c: (lambda h, qi, ki: (jnp.minimum(ki, qi), c(h)))
    return pl.pallas_call(
        body, name=name, grid=(MLA_HEADS, n, n),
        in_specs=[pl.BlockSpec((t, QW), qmap(lambda h: h)), pl.BlockSpec((t, 128), kmap(lambda h: 2 * h)),
                  pl.BlockSpec((t, 128), kmap(lambda h: 0)), pl.BlockSpec((t, 128), kmap(lambda h: 2 * h + 1)),
                  pl.BlockSpec((t, 128), qmap(lambda h: h)), pl.BlockSpec((t, 128), qmap(lambda h: h)),
                  pl.BlockSpec((t, 128), qmap(lambda h: h)), pl.BlockSpec((t, QW), qmap(lambda h: 0)),
                  pl.BlockSpec((t, QW), qmap(lambda h: 0))],
        out_specs=pl.BlockSpec((t, QW), qmap(lambda h: h)),
        out_shape=jax.ShapeDtypeStruct((s, MLA_HEADS * QW), BF16),
        scratch_shapes=[pltpu.VMEM((t, QW), F32)],
        compiler_params=_params(("parallel", "parallel", "arbitrary")))(q, kv, krr, kv, do, o, lse, cq, sq)


def _rope_tables(positions):
    inv_freq = ROPE_THETA ** (-jnp.arange(0, QK_ROPE, 2, dtype=F32) / QK_ROPE)
    ang = positions.reshape(-1).astype(F32)[:, None] * inv_freq
    cos, sin = jnp.cos(ang), jnp.sin(ang)
    s = cos.shape[0]
    one, zero = jnp.ones((s, 64), F32), jnp.zeros((s, 64), F32)
    ck = jnp.concatenate([cos, cos, one], axis=1)
    sk = jnp.concatenate([sin, sin, zero], axis=1)
    cq = jnp.concatenate([one, one, ck], axis=1)
    sq = jnp.concatenate([zero, zero, sk], axis=1)
    return ck, sk, cq, sq


def _layer_weights(full, small, l):
    w_in = full["w_in"][l]
    w = {
        "w_a": jnp.concatenate([w_in[:, :N_IN_A], jnp.zeros((D_MODEL, ZA - N_IN_A), BF16)], axis=1),
        "w_g": w_in[:, N_IN_A:],
        "proj_cat": jnp.concatenate([full["pool_proj"][l], full["conv_proj"][l], full["sgu_proj"][l],
                                     full["attn_proj"][l]], axis=0),
        "w_uq": jnp.pad(full["w_uq"][l].reshape(BR, MLA_HEADS, QK_NOPE + QK_ROPE),
                        ((0, 0), (0, 0), (0, QW - QK_NOPE - QK_ROPE))).reshape(BR, MLA_HEADS * QW),
        "w_ukv": full["w_ukv"][l], "w_out": full["w_out"][l], "w_up": full["w_up"][l], "w_down": full["w_down"][l],
        "conv_w": small["conv_w"][l],
        "pool_w": small["pool_w"][l], "sgu_w": small["sgu_w"][l],
        "sgu_bias": jnp.broadcast_to(small["sgu_b"][l][:, :, None], (4, CHUNK, GDIM)),
    }
    for name in ("pre_mix_g", "pool_scale", "conv_b", "conv_norm_g", "conv_norm_b", "sgu_norm_g", "sgu_norm_b",
                 "q_norm_g", "kv_norm_g", "post_mix_g", "pre_mlp_g", "post_mlp_g"):
        w[name] = small[name][l][None, :]
    return w


def _layer_fwd(x, h1, w, tabs, l):
    ck, sk, cq, sq = tabs
    n = f"l{l}_"
    za = _mm(h1, w["w_a"], mode="nn", name=n + "za")
    zg = _mm(h1, w["w_g"], mode="nn", name=n + "zg")
    a_pool = _pool_fwd(za, w["pool_w"], w["pool_scale"], n + "pool")
    yc, a_conv = _conv_fwd(za, w["conv_w"], w["conv_b"], w["conv_norm_g"], w["conv_norm_b"], n + "conv")
    a_sgu = _sgu_fwd(za, w["sgu_norm_g"], w["sgu_norm_b"], w["sgu_w"], w["sgu_bias"], n + "sgu")
    qn, kvn, krr = _mla_prep(za, w["q_norm_g"], w["kv_norm_g"], ck, sk, n + "mla_prep")
    q = _mm(qn, w["w_uq"], mode="nn", name=n + "q", out_dtypes=(BF16,), extras=(cq, sq), tn=QW,
            epilogue=lambda acc, c, sn: (acc * c + _rot_half(acc, QW, QK_NOPE) * sn,))
    kv = _mm(kvn, w["w_ukv"], mode="nn", name=n + "kv", out_dtypes=(BF16,))
    o, lse = _flash_fwd(q, kv, krr, n + "flash")
    act_cat = jnp.concatenate([a_pool, a_conv, a_sgu, o], axis=1)
    y = _proj_fwd(act_cat, w["proj_cat"], n + "proj")
    merged = _merge_fwd(y, zg, n + "merge")
    m2 = _mm(merged, w["w_out"], mode="nn", name=n + "out")
    x1, h2 = _post_pre(x, m2, w["post_mix_g"], w["pre_mlp_g"], n + "post_mix")
    up, act = _mm(h2, w["w_up"], mode="nn", name=n + "up", out_dtypes=(F32, BF16),
                  epilogue=lambda acc: (acc, jnp.square(jnp.maximum(acc, 0.0))))
    f = _mm(act, w["w_down"], mode="nn", name=n + "down")
    saved = dict(x=x, h1=h1, za=za, zg=zg, yc=yc, qn=qn, kvn=kvn, krr=krr, q=q, kv=kv, o=o, lse=lse, act_cat=act_cat,
                 y=y, merged=merged, m2=m2, x1=x1, h2=h2, up=up, act=act, f=f)
    return x1, f, saved


def _layer_bwd(dx_out, df, sv, w, tabs, l, prev):
    ck, sk, cq, sq = tabs
    n = f"l{l}_b_"
    g = {}
    g["w_down"] = _mm(sv["act"], df, mode="tn", name=n + "dw_down", out_dtypes=(BF16,))
    dup = _mm(df, w["w_down"], mode="nt", name=n + "dup", out_dtypes=(BF16,), extras=(sv["up"],),
              epilogue=lambda acc, up: (acc * 2.0 * jnp.maximum(up, 0.0),))
    g["w_up"] = _mm(sv["h2"], dup, mode="tn", name=n + "dw_up", out_dtypes=(BF16,))
    dh2 = _mm(dup, w["w_up"], mode="nt", name=n + "dh2")
    dx1, dm2, g["pre_mlp_g"], g["post_mix_g"] = _pre_bwd(dh2, sv["x1"], w["pre_mlp_g"], dx_out, n + "pre_mlp",
                                                           r_prev=sv["m2"], g_post_prev=w["post_mix_g"])
    g["w_out"] = _mm(sv["merged"], dm2, mode="tn", name=n + "dw_out", out_dtypes=(BF16,))
    dmerged = _mm(dm2, w["w_out"], mode="nt", name=n + "dmerged")
    dy, dzg = _merge_bwd(dmerged, sv["y"], sv["zg"], n + "merge")
    g["proj_cat"] = _proj_bwd_w(sv["act_cat"], dy, n + "dw_proj")
    dact = _proj_bwd_act(dy, w["proj_cat"], n + "dact")
    dz_pool, g["pool_w"], g["pool_scale"] = _pool_bwd(dact, sv["za"], w["pool_w"], w["pool_scale"], n + "pool")
    dyc, g["conv_b"], g["conv_norm_g"], g["conv_norm_b"] = _conv_bwd_norm(dact, sv["yc"], w["conv_norm_g"],
                                                                          w["conv_norm_b"], n + "conv_norm")
    dz_conv, g["conv_w"] = _conv_bwd_taps(dyc, sv["za"], w["conv_w"], n + "conv_taps")
    dz_sgu, g["sgu_w"], g["sgu_b"], g["sgu_norm_g"], g["sgu_norm_b"] = _sgu_bwd(
        dact, sv["za"], w["sgu_norm_g"], w["sgu_norm_b"], w["sgu_w"], w["sgu_bias"], n + "sgu")
    do = dact[:, 3 * BR:].astype(BF16)
    dkv, dkr_heads = _flash_bwd_kv(sv["q"], sv["kv"], sv["krr"], do, sv["o"], sv["lse"], n + "flash_kv")
    dq = _flash_bwd_q(sv["q"], sv["kv"], sv["krr"], do, sv["o"], sv["lse"], cq, sq, n + "flash_q")
    g["w_uq"] = _mm(sv["qn"], dq, mode="tn", name=n + "dw_uq", out_dtypes=(BF16,))
    dqn = _mm(dq, w["w_uq"], mode="nt", name=n + "dqn")
    g["w_ukv"] = _mm(sv["kvn"], dkv, mode="tn", name=n + "dw_ukv", out_dtypes=(BF16,))
    dkvn = _mm(dkv, w["w_ukv"], mode="nt", name=n + "dkvn")
    dz_mla, g["q_norm_g"], g["kv_norm_g"] = _mla_prep_bwd(dqn, dkvn, dkr_heads, sv["za"], w["q_norm_g"],
                                                           w["kv_norm_g"], ck, sk, n + "mla_prep")
    dza = jnp.concatenate([dz_pool, dz_conv, dz_sgu, dz_mla], axis=1)
    g["w_a"] = _mm(sv["h1"], dza, mode="tn", name=n + "dw_a", out_dtypes=(BF16,))
    g["w_g"] = _mm(sv["h1"], dzg, mode="tn", name=n + "dw_g", out_dtypes=(BF16,))
    dh1 = _mm(dza, w["w_a"], mode="nt", name=n + "dh1_a")
    dh1 = _mm(dzg, w["w_g"], mode="nt", name=n + "dh1_g", extras=(dh1,), epilogue=lambda acc, e: (acc + e,))
    if prev is None:
        dx, g["pre_mix_g"] = _pre_bwd(dh1, sv["x"], w["pre_mix_g"], dx1, n + "pre_mix")
        return dx, None, g
    dx, df_prev, g["pre_mix_g"], g_prev_post = _pre_bwd(dh1, sv["x"], w["pre_mix_g"], dx1, n + "pre_mix",
                                                         r_prev=prev[0], g_post_prev=prev[1])
    g["prev_post_mlp_g"] = g_prev_post
    return dx, df_prev, g


def _local_step(x, positions, target, full, small):
    tabs = _rope_tables(positions)
    ws = [_layer_weights(full, small, l) for l in range(DEPTH)]
    saved = []
    h = _pre_norm(x, ws[0]["pre_mix_g"], "l0_pre_mix")
    cur = x
    for l in range(DEPTH):
        x1, f, sv = _layer_fwd(cur, h, ws[l], tabs, l)
        saved.append(sv)
        if l + 1 < DEPTH:
            cur, h = _post_pre(x1, f, ws[l]["post_mlp_g"], ws[l + 1]["pre_mix_g"], f"l{l}_post_mlp")
    top = DEPTH - 1
    dx, df, dg_post, loss = _final_loss(saved[top]["x1"], saved[top]["f"], ws[top]["post_mlp_g"], target, "loss")
    grads = [None] * DEPTH
    post_mlp = {top: dg_post}
    for l in range(top, -1, -1):
        prev = (saved[l - 1]["f"], ws[l - 1]["post_mlp_g"]) if l > 0 else None
        dx, df, g = _layer_bwd(dx, df, saved[l], ws[l], tabs, l, prev)
        if l > 0:
            post_mlp[l - 1] = g.pop("prev_post_mlp_g")
        grads[l] = g
    for l in range(DEPTH):
        grads[l]["post_mlp_g"] = post_mlp[l]
    return loss[0, 0], dx, grads


def _grads_to_param_layout(grads):
    big, small = {}, {}

    def stack(fn):
        return jnp.stack([fn(grads[l]) for l in range(DEPTH)])

    big["w_in"] = stack(lambda g: jnp.concatenate([g["w_a"][:, :N_IN_A], g["w_g"]], axis=1))
    big["pool_proj"] = stack(lambda g: g["proj_cat"][0:BR])
    big["conv_proj"] = stack(lambda g: g["proj_cat"][BR:2 * BR])
    big["sgu_proj"] = stack(lambda g: g["proj_cat"][2 * BR:3 * BR])
    big["attn_proj"] = stack(lambda g: g["proj_cat"][3 * BR:])
    big["w_uq"] = stack(lambda g: g["w_uq"].reshape(BR, MLA_HEADS, QW)[:, :, :QK_NOPE + QK_ROPE].reshape(BR, -1))
    for name in ("w_ukv", "w_out", "w_up", "w_down"):
        big[name] = stack(lambda g, name=name: g[name])
    for name, shape in SMALL:
        if name == "sgu_b":
            small[name] = stack(lambda g: g["sgu_b"][:, :, 0])
        else:
            small[name] = stack(lambda g, name=name, shape=shape: g[name].reshape(shape))
    small["conv_w"] = stack(lambda g: g["conv_w"][:CONV_WIDTH])
    return big, small


def _big_rows(shape):
    return DEPTH * shape[0] * shape[1] // PACK_W


BIG_ROWS = sum(_big_rows(shape) for _, _, shape in BIG)


def _pack_local(parts, dtype):
    return jnp.concatenate([parts[name].astype(dtype).reshape(-1, PACK_W) for name, _, _ in BIG], axis=0)


def _unpack_local(buf):
    out, off = {}, 0
    for name, _, shape in BIG:
        rows = _big_rows(shape)
        out[name] = buf[off:off + rows].reshape((DEPTH,) + shape)
        off += rows
    return out


def _unpack_gathered(buf):
    out, off = {}, 0
    for name, axis, (r, c) in BIG:
        rows = _big_rows((r, c))
        blk = buf[:, off:off + rows].reshape(N_DEV, DEPTH, r, c)
        if axis == 1:
            out[name] = blk.transpose(1, 2, 0, 3).reshape(DEPTH, r, N_DEV * c)
        else:
            out[name] = blk.transpose(1, 0, 2, 3).reshape(DEPTH, N_DEV * r, c)
        off += rows
    return out


def _pack_by_owner(full):
    cols = []
    for name, axis, (r, c) in BIG:
        a = full[name]
        if axis == 1:
            blk = a.reshape(DEPTH, r, N_DEV, c).transpose(2, 0, 1, 3)
        else:
            blk = a.reshape(DEPTH, N_DEV, r, c).transpose(1, 0, 2, 3)
        cols.append(blk.reshape(N_DEV, -1, PACK_W))
    return jnp.concatenate(cols, axis=1)


SMALL_ROWS = sum(DEPTH * math.prod(shape) // 128 for _, shape in SMALL)
CONVW_ROWS = DEPTH * CONV_WIDTH * BR // 128


def _pack_small(parts):
    return jnp.concatenate([parts[name].astype(F32).reshape(-1, 128) for name, _ in SMALL], axis=0)


def _unpack_small(buf):
    out, off = {}, 0
    for name, shape in SMALL:
        rows = DEPTH * math.prod(shape) // 128
        out[name] = buf[off:off + rows].reshape((DEPTH,) + shape)
        off += rows
    return out


def _mesh_pos():
    return lax.axis_index("x"), lax.axis_index("y"), lax.axis_index("c")


def _all_gather(shard, name):
    r, c_ = shard.shape

    def body(x_ref, out_ref, send_sems, recv_sems, local_sem):
        x, y, c = _mesh_pos()
        me, sibling = (x, y, c), (x, y, 1 - c)
        chips = [(1 - x, y), (x, 1 - y), (1 - x, 1 - y)]

        def rows(px, py, pc):
            return out_ref.at[4 * px + 2 * py + pc]

        def copy(k, block, to, src=None):
            return pltpu.make_async_remote_copy(
                src_ref=rows(*block) if src is None else src, dst_ref=rows(*block), send_sem=send_sems.at[k],
                recv_sem=recv_sems.at[k], device_id=to, device_id_type=MESH)

        mine = pltpu.make_async_copy(x_ref, rows(*me), local_sem)
        mine.start()
        first = [copy(0, me, sibling, src=x_ref)]
        first += [copy(1 + j, me, (*chip, c), src=x_ref) for j, chip in enumerate(chips)]
        for cp in first:
            cp.start()
        passed = [copy(4 + j, (*chip, c), sibling) for j, chip in enumerate(chips)]
        for j, chip in enumerate(chips):
            copy(1 + j, (*chip, c), me).wait_recv()
            passed[j].start()
        copy(0, sibling, me).wait_recv()
        for j, chip in enumerate(chips):
            copy(4 + j, (*chip, 1 - c), me).wait_recv()
        for cp in first + passed:
            cp.wait_send()
        mine.wait()

    return pl.pallas_call(
        body, name=name, out_shape=jax.ShapeDtypeStruct((N_DEV, r, c_), shard.dtype),
        in_specs=[pl.BlockSpec(memory_space=pl.ANY)], out_specs=pl.BlockSpec(memory_space=pl.ANY),
        scratch_shapes=[pltpu.SemaphoreType.DMA((7,)), pltpu.SemaphoreType.DMA((7,)), pltpu.SemaphoreType.DMA(())])(shard)


def _rs_sibling(g, name):
    _, r, c_ = g.shape

    def body(g_ref, t_ref, send_sems, recv_sems):
        x, y, c = _mesh_pos()
        copies = [pltpu.make_async_remote_copy(
            src_ref=g_ref.at[2 * k + 1 - c], dst_ref=t_ref.at[k], send_sem=send_sems.at[k], recv_sem=recv_sems.at[k],
            device_id=(x, y, 1 - c), device_id_type=MESH) for k in range(4)]
        for cp in copies:
            cp.start()
        for cp in copies:
            cp.wait_recv()
        for cp in copies:
            cp.wait_send()

    return pl.pallas_call(
        body, name=name, out_shape=jax.ShapeDtypeStruct((4, r, c_), g.dtype),
        in_specs=[pl.BlockSpec(memory_space=pl.ANY)], out_specs=pl.BlockSpec(memory_space=pl.ANY),
        scratch_shapes=[pltpu.SemaphoreType.DMA((4,)), pltpu.SemaphoreType.DMA((4,))])(g)


def _rs_chip_sum(g, t, name):
    _, r, c_ = g.shape
    tr = 272 if r % 272 == 0 else r
    core = lax.axis_index("c").astype(jnp.int32).reshape(1)

    def body(core_ref, g_ref, t_ref, p_ref):
        p_ref[...] = (g_ref[...].astype(F32) + t_ref[...].astype(F32)).astype(p_ref.dtype)

    return pl.pallas_call(
        body, name=name, out_shape=jax.ShapeDtypeStruct((4, r, c_), g.dtype),
        grid_spec=pltpu.PrefetchScalarGridSpec(
            num_scalar_prefetch=1, grid=(4, r // tr),
            in_specs=[pl.BlockSpec((1, tr, c_), lambda k, i, core_ref: (2 * k + core_ref[0], i, 0)),
                      pl.BlockSpec((1, tr, c_), lambda k, i, core_ref: (k, i, 0))],
            out_specs=pl.BlockSpec((1, tr, c_), lambda k, i, core_ref: (k, i, 0))),
        compiler_params=_params(("parallel", "parallel")))(core, g, t)


def _rs_chips(p, name):
    _, r, c_ = p.shape

    def body(p_ref, t_ref, send_sems, recv_sems):
        x, y, c = _mesh_pos()
        chips = [(1 - x, y), (x, 1 - y), (1 - x, 1 - y)]
        copies = [pltpu.make_async_remote_copy(
            src_ref=p_ref.at[2 * cx + cy], dst_ref=t_ref.at[j], send_sem=send_sems.at[j], recv_sem=recv_sems.at[j],
            device_id=(cx, cy, c), device_id_type=MESH) for j, (cx, cy) in enumerate(chips)]
        for cp in copies:
            cp.start()
        for cp in copies:
            cp.wait_recv()
        for cp in copies:
            cp.wait_send()

    return pl.pallas_call(
        body, name=name, out_shape=jax.ShapeDtypeStruct((3, r, c_), p.dtype),
        in_specs=[pl.BlockSpec(memory_space=pl.ANY)], out_specs=pl.BlockSpec(memory_space=pl.ANY),
        scratch_shapes=[pltpu.SemaphoreType.DMA((3,)), pltpu.SemaphoreType.DMA((3,))])(p)


def _adamw_math(w, g, m, v):
    m = ADAM_B1 * m + (1.0 - ADAM_B1) * g
    v = ADAM_B2 * v + (1.0 - ADAM_B2) * jnp.square(g)
    m_hat = m / (1.0 - ADAM_B1 ** ADAM_STEP)
    v_hat = v / (1.0 - ADAM_B2 ** ADAM_STEP)
    delta = -ADAM_LR * (m_hat / (jnp.sqrt(v_hat) + ADAM_EPS) + ADAM_WD * w)
    return delta, m, v


def _adamw_big(p, t, w, m, v, name):
    r, c_ = w.shape
    tr = 272 if r % 272 == 0 else r
    chip = (2 * lax.axis_index("x") + lax.axis_index("y")).astype(jnp.int32).reshape(1)

    def body(chip_ref, p_ref, t_ref, w_ref, m_ref, v_ref, g_out, d_out, m_out, v_out):
        g = p_ref[0].astype(F32) + t_ref[0].astype(F32) + t_ref[1].astype(F32) + t_ref[2].astype(F32)
        g_out[...] = g
        d_out[...], m_out[...], v_out[...] = _adamw_math(w_ref[...], g, m_ref[...], v_ref[...])

    flat = pl.BlockSpec((tr, c_), lambda i, chip_ref: (i, 0))
    shape = jax.ShapeDtypeStruct((r, c_), F32)
    return pl.pallas_call(
        body, name=name, out_shape=[shape] * 4,
        grid_spec=pltpu.PrefetchScalarGridSpec(
            num_scalar_prefetch=1, grid=(r // tr,),
            in_specs=[pl.BlockSpec((1, tr, c_), lambda i, chip_ref: (chip_ref[0], i, 0)),
                      pl.BlockSpec((3, tr, c_), lambda i, chip_ref: (0, i, 0)), flat, flat, flat],
            out_specs=[flat] * 4),
        compiler_params=_params(("parallel",), VMEM_BIG))(chip, p, t, w, m, v)


def _sum_devices(parts, name):
    _, r, c_ = parts.shape

    def body(p_ref, o_ref):
        acc = p_ref[0]
        for d in range(1, N_DEV):
            acc = acc + p_ref[d]
        o_ref[...] = acc

    return pl.pallas_call(body, name=name, out_shape=jax.ShapeDtypeStruct((r, c_), F32),
                          compiler_params=_params(None, VMEM_BIG))(parts)


def _adamw_small(w, g, m, v, name):
    def body(w_ref, g_ref, m_ref, v_ref, d_out, m_out, v_out):
        d_out[...], m_out[...], v_out[...] = _adamw_math(w_ref[...], g_ref[...], m_ref[...], v_ref[...])

    shape = jax.ShapeDtypeStruct(w.shape, F32)
    return pl.pallas_call(body, name=name, out_shape=[shape] * 3)(w, g, m, v)


def kernel(x, positions, pre_mix_g, w_in, pool_w, pool_scale, pool_proj, conv_w, conv_b, conv_norm_g, conv_norm_b, conv_proj, sgu_norm_g, sgu_norm_b, sgu_w, sgu_b, sgu_proj, q_norm_g, w_uq, kv_norm_g, w_ukv, attn_proj, w_out, post_mix_g, pre_mlp_g, w_up, w_down, post_mlp_g, loss_target, m_pre_mix_g, m_w_in, m_pool_w, m_pool_scale, m_pool_proj, m_conv_w, m_conv_b, m_conv_norm_g, m_conv_norm_b, m_conv_proj, m_sgu_norm_g, m_sgu_norm_b, m_sgu_w, m_sgu_b, m_sgu_proj, m_q_norm_g, m_w_uq, m_kv_norm_g, m_w_ukv, m_attn_proj, m_w_out, m_post_mix_g, m_pre_mlp_g, m_w_up, m_w_down, m_post_mlp_g, v_pre_mix_g, v_w_in, v_pool_w, v_pool_scale, v_pool_proj, v_conv_w, v_conv_b, v_conv_norm_g, v_conv_norm_b, v_conv_proj, v_sgu_norm_g, v_sgu_norm_b, v_sgu_w, v_sgu_b, v_sgu_proj, v_q_norm_g, v_w_uq, v_kv_norm_g, v_w_ukv, v_attn_proj, v_w_out, v_post_mix_g, v_pre_mlp_g, v_w_up, v_w_down, v_post_mlp_g):
    args = dict(locals())
    wts = {n: args[n] for n in WEIGHTS}
    mom1 = {n: args["m_" + n] for n in WEIGHTS}
    mom2 = {n: args["v_" + n] for n in WEIGHTS}
    dev = 4 * lax.axis_index("x") + 2 * lax.axis_index("y") + lax.axis_index("c")

    gathered = _all_gather(_pack_local(wts, BF16), "gather_weights")
    full = _unpack_gathered(gathered)
    taps = jnp.pad(conv_w.reshape(-1, 128), ((0, 1), (0, 0)))
    taps = _all_gather(taps, "gather_taps")[:, :CONV_WIDTH].reshape(N_DEV, DEPTH, CONV_WIDTH, BR // N_DEV)
    small = {n: wts[n] for n, _ in SMALL}
    small["conv_w"] = taps.transpose(1, 2, 0, 3).reshape(DEPTH, CONV_WIDTH, BR)

    loss_part, grad_x, grads = _local_step(x[0], positions, loss_target[0], full, small)
    big_g, small_g = _grads_to_param_layout(grads)
    loss = lax.psum(loss_part, ("x", "y", "c"))

    by_owner = _pack_by_owner(big_g)
    from_sibling = _rs_sibling(by_owner, "rs_sibling")
    chip_sum = _rs_chip_sum(by_owner, from_sibling, "rs_chip_sum")
    from_chips = _rs_chips(chip_sum, "rs_chips")
    g_big, d_big, m_big, v_big = _adamw_big(chip_sum, from_chips, _pack_local(wts, F32), _pack_local(mom1, F32),
                                            _pack_local(mom2, F32), "adamw_big")
    out = {"grad": _unpack_local(g_big), "delta": _unpack_local(d_big), "new_m": _unpack_local(m_big),
           "new_v": _unpack_local(v_big)}

    part = jnp.concatenate([_pack_small(small_g), small_g["conv_w"].reshape(-1, 128)], axis=0)
    total = _sum_devices(_all_gather(part, "gather_small_grads"), "sum_small_grads")
    g_small = total[:SMALL_ROWS]
    d_small, m_small, v_small = _adamw_small(_pack_small(wts), g_small, _pack_small(mom1), _pack_small(mom2), "adamw_small")
    for key, buf in (("grad", g_small), ("delta", d_small), ("new_m", m_small), ("new_v", v_small)):
        out[key].update(_unpack_small(buf))
    g_taps = total[SMALL_ROWS:].reshape(DEPTH, CONV_WIDTH, N_DEV, BR // N_DEV)
    g_taps = lax.dynamic_index_in_dim(g_taps, dev, axis=2, keepdims=False)
    flat = lambda a: a.reshape(-1, 128)
    d_taps, m_taps, v_taps = _adamw_small(flat(conv_w), flat(g_taps), flat(m_conv_w), flat(v_conv_w), "adamw_taps")
    for key, buf in (("grad", g_taps), ("delta", d_taps), ("new_m", m_taps), ("new_v", v_taps)):
        out[key]["conv_w"] = buf.reshape(conv_w.shape)

    return (loss, grad_x[None], *[out["grad"][n] for n in WEIGHTS], *[out["delta"][n] for n in WEIGHTS],
            *[out["new_m"][n] for n in WEIGHTS], *[out["new_v"][n] for n in WEIGHTS])
```

```python
import functools
import math

import jax
import jax.numpy as jnp
from jax import lax
from jax.experimental import pallas as pl
from jax.experimental.pallas import tpu as pltpu

F32 = jnp.float32
BF16 = jnp.bfloat16

D_MODEL = 2048
DEPTH = 2
EPS = 1e-6
N_BRANCH = 4
D_FF = 4 * D_MODEL
UP_SHARD = D_FF // 8
POOL_WINDOWS = (2, 4, 8, 16)
CONV_WIDTH = 31
CHUNK = 128
MLA_HEADS = 8
QK_NOPE = 128
QK_ROPE = 64
V_DIM = 128
ROPE_THETA = 10000.0
GDIM = 128
BR = 512
N_IN_A = 3648
ZA = 3712
N_GATE = N_BRANCH * D_MODEL
N_IN = N_IN_A + N_GATE
QW = 256
ACT_CAT = 3 * BR + MLA_HEADS * V_DIM
ATT_SCALE = (QK_NOPE + QK_ROPE) ** -0.5
NEG = -1e30

ADAM_LR = 0.001
ADAM_B1 = 0.9
ADAM_B2 = 0.999
ADAM_EPS = 1e-08
ADAM_WD = 0.01
ADAM_STEP = 10

N_DEV = 8
PACK_W = 1024
VMEM_BIG = 48 * 1024 * 1024
MESH = pl.DeviceIdType.MESH

BIG = (
    ("w_in", 1, (2048, 1480)),
    ("pool_proj", 1, (512, 256)),
    ("conv_proj", 1, (512, 256)),
    ("sgu_proj", 1, (512, 256)),
    ("w_uq", 1, (512, 192)),
    ("w_ukv", 1, (512, 256)),
    ("attn_proj", 1, (1024, 256)),
    ("w_out", 0, (256, 2048)),
    ("w_up", 1, (2048, 1024)),
    ("w_down", 0, (1024, 2048)),
)
SMALL = (
    ("pre_mix_g", (2048,)), ("pool_w", (4, 128, 128)), ("pool_scale", (512,)), ("conv_b", (512,)),
    ("conv_norm_g", (512,)), ("conv_norm_b", (512,)), ("sgu_norm_g", (512,)), ("sgu_norm_b", (512,)),
    ("sgu_w", (4, 128, 128)), ("sgu_b", (4, 128)), ("q_norm_g", (512,)), ("kv_norm_g", (512,)),
    ("post_mix_g", (2048,)), ("pre_mlp_g", (2048,)), ("post_mlp_g", (2048,)),
)
WEIGHTS = ("pre_mix_g", "w_in", "pool_w", "pool_scale", "pool_proj", "conv_w", "conv_b", "conv_norm_g", "conv_norm_b",
           "conv_proj", "sgu_norm_g", "sgu_norm_b", "sgu_w", "sgu_b", "sgu_proj", "q_norm_g", "w_uq", "kv_norm_g",
           "w_ukv", "attn_proj", "w_out", "post_mix_g", "pre_mlp_g", "w_up", "w_down", "post_mlp_g")


def _params(sem=None, vmem=None):
    return pltpu.CompilerParams(dimension_semantics=sem, vmem_limit_bytes=vmem)


def _tile(dim, pref):
    if dim <= pref:
        return dim
    best = 0
    for t in range(128, pref + 1, 128):
        if dim % t == 0:
            best = t
    return best if best >= 256 else dim


def _sigmoid(x):
    return 1.0 / (1.0 + jnp.exp(-x))


def _gelu(x):
    k = math.sqrt(2.0 / math.pi)
    return 0.5 * x * (1.0 + jnp.tanh(k * (x + 0.044715 * x * x * x)))


def _gelu_grad(x):
    k = math.sqrt(2.0 / math.pi)
    t = jnp.tanh(k * (x + 0.044715 * x * x * x))
    return 0.5 * (1.0 + t) + 0.5 * x * (1.0 - t * t) * k * (1.0 + 3.0 * 0.044715 * x * x)


def _rms(x, g):
    r = lax.rsqrt(jnp.mean(x * x, axis=-1, keepdims=True) + EPS)
    return x * r * g


def _rms_bwd(x, g, dy):
    r = lax.rsqrt(jnp.mean(x * x, axis=-1, keepdims=True) + EPS)
    dyg = dy * g
    dx = r * dyg - x * (r * r * r) * jnp.mean(dyg * x, axis=-1, keepdims=True)
    return dx, dy * x * r


def _ln_stats(x):
    mu = jnp.mean(x, axis=-1, keepdims=True)
    xc = x - mu
    r = lax.rsqrt(jnp.mean(xc * xc, axis=-1, keepdims=True) + EPS)
    return xc * r, r


def _ln_bwd(xh, r, g, dy):
    dxh = dy * g
    return r * (dxh - jnp.mean(dxh, axis=-1, keepdims=True) - xh * jnp.mean(dxh * xh, axis=-1, keepdims=True))


def _rot_half(x, width, off):
    n = x.shape[-1]
    lane = lax.broadcasted_iota(jnp.int32, x.shape, x.ndim - 1) % width
    return jnp.where(lane - off < QK_ROPE // 2, -pltpu.roll(x, n - QK_ROPE // 2, x.ndim - 1),
                     pltpu.roll(x, QK_ROPE // 2, x.ndim - 1))


def _colsum_into(ref, val, first):
    s = jnp.sum(val, axis=0, keepdims=True)

    @pl.when(first)
    def _():
        ref[...] = s

    @pl.when(jnp.logical_not(first))
    def _():
        ref[...] += s


_DIMS = {"nn": ((1,), (0,)), "nt": ((1,), (1,)), "tn": ((0,), (0,))}


def _mm_call(a, b, *, mode, name, grid, kaxis, nk, a_spec, b_spec, o_specs, out_shape, acc_shape,
             extras=(), e_specs=(), epilogue=None, active=None):
    ne, no = len(extras), len(out_shape)

    def body(a_ref, b_ref, *rest):
        e_refs, o_refs, acc_ref = rest[:ne], rest[ne:ne + no], rest[ne + no]
        ids = [pl.program_id(ax) for ax in range(len(grid))]
        k = ids[kaxis]

        def finish(acc):
            outs = (acc,) if epilogue is None else epilogue(acc, *[e[...] for e in e_refs])
            for o_ref, val in zip(o_refs, outs):
                o_ref[...] = val.astype(o_ref.dtype)

        def step():
            prod = lax.dot_general(a_ref[...], b_ref[...], (_DIMS[mode], ((), ())), preferred_element_type=F32)
            if nk == 1:
                finish(prod)
                return

            @pl.when(k == 0)
            def _():
                acc_ref[...] = prod

            @pl.when(k > 0)
            def _():
                acc_ref[...] += prod

        if active is None:
            step()
        else:
            pl.when(active(*ids))(step)
        if nk > 1:
            @pl.when(k == nk - 1)
            def _():
                finish(acc_ref[...])

    sem = tuple("arbitrary" if ax == kaxis else "parallel" for ax in range(len(grid)))
    outs = pl.pallas_call(
        body, name=name, grid=grid, in_specs=[a_spec, b_spec, *e_specs], out_specs=list(o_specs),
        out_shape=list(out_shape), scratch_shapes=[pltpu.VMEM(acc_shape, F32)],
        compiler_params=_params(sem, VMEM_BIG))(a, b, *extras)
    return outs


def _mm(a, b, *, mode, name, out_dtypes=(F32,), extras=(), epilogue=None, tm=1024, tn=1024, tk=1024,
        mnk=None, b_spec=None, o_spec=None, out_struct=None):
    if mnk is not None:
        m, n, k = mnk
    elif mode == "nn":
        (m, k), (_, n) = a.shape, b.shape
    elif mode == "nt":
        (m, k), (n, _) = a.shape, b.shape
    else:
        (k, m), (_, n) = a.shape, b.shape
    tm, tn, tk = _tile(m, tm), _tile(n, tn), _tile(k, tk)
    if tn > 2048:
        tm, tk = _tile(m, 512), _tile(k, 512)
    if tk > 2048:
        tm, tn = _tile(m, 512), _tile(n, 512)
    nk = k // tk
    if mode == "tn":
        a_spec = pl.BlockSpec((tk, tm), lambda i, j, kk: (kk, i))
    else:
        a_spec = pl.BlockSpec((tm, tk), lambda i, j, kk: (i, kk))
    if b_spec is not None:
        b_spec = b_spec(tn, tk)
    elif mode == "nt":
        b_spec = pl.BlockSpec((tn, tk), lambda i, j, kk: (j, kk))
    else:
        b_spec = pl.BlockSpec((tk, tn), lambda i, j, kk: (kk, j))

    def e_spec(e):
        if e.shape[1] == tn and n != tn:
            return pl.BlockSpec((tm, tn), lambda i, j, kk: (i, 0))
        return pl.BlockSpec((tm, tn), lambda i, j, kk: (i, j))

    e_specs = [e_spec(e) for e in extras]
    if o_spec is not None:
        o_specs = [o_spec(tm, tn)]
        out_shape = [out_struct]
    else:
        o_specs = [pl.BlockSpec((tm, tn), lambda i, j, kk: (i, j)) for _ in out_dtypes]
        out_shape = [jax.ShapeDtypeStruct((m, n), dt) for dt in out_dtypes]
    outs = _mm_call(a, b, mode=mode, name=name, grid=(m // tm, n // tn, nk), kaxis=2, nk=nk, a_spec=a_spec,
                    b_spec=b_spec, o_specs=o_specs, out_shape=out_shape, acc_shape=(tm, tn), extras=extras,
                    e_specs=e_specs, epilogue=epilogue)
    return outs[0] if len(outs) == 1 else outs


def _branch_of(kb):
    return jnp.minimum(kb, N_BRANCH - 1)


def _proj_fwd(act_cat, proj_cat, name):
    s = act_cat.shape[0]
    tm, tn = _tile(s, 1024), 1024
    nj = D_MODEL // tn

    def kb(b, k):
        return jnp.where(b < N_BRANCH - 1, b, N_BRANCH - 1 + k)

    out = _mm_call(
        act_cat, proj_cat, mode="nn", name=name, grid=(s // tm, nj, N_BRANCH, 2), kaxis=3, nk=2,
        a_spec=pl.BlockSpec((tm, BR), lambda i, j, b, k: (i, kb(b, k))),
        b_spec=pl.BlockSpec((BR, tn), lambda i, j, b, k: (kb(b, k), j)),
        o_specs=[pl.BlockSpec((tm, tn), lambda i, j, b, k: (i, b * nj + j))],
        out_shape=[jax.ShapeDtypeStruct((s, N_GATE), F32)], acc_shape=(tm, tn),
        active=lambda i, j, b, k: jnp.logical_or(b == N_BRANCH - 1, k == 0))
    return out[0]


def _proj_bwd_act(dy, proj_cat, name):
    s = dy.shape[0]
    tm, tk = _tile(s, 1024), 1024
    nkk = D_MODEL // tk
    nkb = ACT_CAT // BR
    out = _mm_call(
        dy, proj_cat, mode="nt", name=name, grid=(s // tm, nkb, nkk), kaxis=2, nk=nkk,
        a_spec=pl.BlockSpec((tm, tk), lambda i, kb, k: (i, _branch_of(kb) * nkk + k)),
        b_spec=pl.BlockSpec((BR, tk), lambda i, kb, k: (kb, k)),
        o_specs=[pl.BlockSpec((tm, BR), lambda i, kb, k: (i, kb))],
        out_shape=[jax.ShapeDtypeStruct((s, ACT_CAT), F32)], acc_shape=(tm, BR))
    return out[0]


def _proj_bwd_w(act_cat, dy, name):
    s = dy.shape[0]
    tms, tn = _tile(s, 1024), 1024
    nj = D_MODEL // tn
    nkb = ACT_CAT // BR
    nm = s // tms
    out = _mm_call(
        act_cat, dy, mode="tn", name=name, grid=(nkb, nj, nm), kaxis=2, nk=nm,
        a_spec=pl.BlockSpec((tms, BR), lambda kb, j, m: (m, kb)),
        b_spec=pl.BlockSpec((tms, tn), lambda kb, j, m: (m, _branch_of(kb) * nj + j)),
        o_specs=[pl.BlockSpec((BR, tn), lambda kb, j, m: (kb, j))],
        out_shape=[jax.ShapeDtypeStruct((ACT_CAT, D_MODEL), BF16)], acc_shape=(BR, tn))
    return out[0]


def _row_specs(ts, n_full, n_vec):
    return ([pl.BlockSpec((ts, D_MODEL), lambda i: (i, 0))] * n_full
            + [pl.BlockSpec((1, D_MODEL), lambda i: (0, 0))] * n_vec)


def _pre_norm(x, g, name):
    s = x.shape[0]
    ts = _tile(s, 256)

    def body(x_ref, g_ref, h_ref):
        h_ref[...] = _rms(x_ref[...], g_ref[...]).astype(BF16)

    return pl.pallas_call(body, name=name, grid=(s // ts,), in_specs=_row_specs(ts, 1, 1),
                          out_specs=pl.BlockSpec((ts, D_MODEL), lambda i: (i, 0)),
                          out_shape=jax.ShapeDtypeStruct((s, D_MODEL), BF16), compiler_params=_params(("parallel",)))(x, g)


def _post_pre(x, r, g_post, g_next, name):
    s = x.shape[0]
    ts = _tile(s, 256)

    def body(x_ref, r_ref, gp_ref, gn_ref, xn_ref, h_ref):
        xn = x_ref[...] + _rms(r_ref[...], gp_ref[...])
        xn_ref[...] = xn
        h_ref[...] = _rms(xn, gn_ref[...]).astype(BF16)

    spec = pl.BlockSpec((ts, D_MODEL), lambda i: (i, 0))
    return pl.pallas_call(body, name=name, grid=(s // ts,), in_specs=_row_specs(ts, 2, 2), out_specs=[spec, spec],
                          out_shape=[jax.ShapeDtypeStruct((s, D_MODEL), F32), jax.ShapeDtypeStruct((s, D_MODEL), BF16)],
                          compiler_params=_params(("parallel",)))(x, r, g_post, g_next)


def _final_loss(x, r, g_post, target, name):
    s = x.shape[0]
    ts = _tile(s, 256)

    def body(x_ref, r_ref, gp_ref, t_ref, dy_ref, dr_ref, dg_ref, loss_ref):
        first = pl.program_id(0) == 0
        rv, gp = r_ref[...], gp_ref[...]
        diff = x_ref[...] + _rms(rv, gp) - t_ref[...]
        part = 0.5 * jnp.sum(jnp.mean(diff * diff, axis=-1, keepdims=True), axis=0, keepdims=True)
        dy = diff * (1.0 / D_MODEL)
        dy_ref[...] = dy
        dr, dg_rows = _rms_bwd(rv, gp, dy)
        dr_ref[...] = dr.astype(BF16)
        _colsum_into(dg_ref, dg_rows, first)
        _colsum_into(loss_ref, jnp.broadcast_to(part, (1, 128)), first)

    spec = pl.BlockSpec((ts, D_MODEL), lambda i: (i, 0))
    vec = pl.BlockSpec((1, D_MODEL), lambda i: (0, 0))
    return pl.pallas_call(
        body, name=name, grid=(s // ts,), in_specs=[spec, spec, vec, spec],
        out_specs=[spec, spec, vec, pl.BlockSpec((1, 128), lambda i: (0, 0))],
        out_shape=[jax.ShapeDtypeStruct((s, D_MODEL), F32), jax.ShapeDtypeStruct((s, D_MODEL), BF16),
                   jax.ShapeDtypeStruct((1, D_MODEL), F32), jax.ShapeDtypeStruct((1, 128), F32)],
        compiler_params=_params(("arbitrary",)))(x, r, g_post, target)


def _pre_bwd(dh, x, g_pre, dx_res, name, r_prev=None, g_post_prev=None):
    s = x.shape[0]
    ts = _tile(s, 256)
    chain = r_prev is not None

    def body(*refs):
        if chain:
            dh_ref, x_ref, res_ref, r_ref, g_ref, gp_ref, dx_ref, dr_ref, dg_ref, dgp_ref = refs
        else:
            dh_ref, x_ref, res_ref, g_ref, dx_ref, dg_ref = refs
        first = pl.program_id(0) == 0
        dxn, dg_rows = _rms_bwd(x_ref[...], g_ref[...], dh_ref[...])
        dx = res_ref[...] + dxn
        dx_ref[...] = dx
        _colsum_into(dg_ref, dg_rows, first)
        if chain:
            dr, dgp_rows = _rms_bwd(r_ref[...], gp_ref[...], dx)
            dr_ref[...] = dr.astype(BF16)
            _colsum_into(dgp_ref, dgp_rows, first)

    spec = pl.BlockSpec((ts, D_MODEL), lambda i: (i, 0))
    vec = pl.BlockSpec((1, D_MODEL), lambda i: (0, 0))
    full = jax.ShapeDtypeStruct((s, D_MODEL), F32)
    vshape = jax.ShapeDtypeStruct((1, D_MODEL), F32)
    if chain:
        return pl.pallas_call(
            body, name=name, grid=(s // ts,), in_specs=[spec] * 4 + [vec] * 2, out_specs=[spec, spec, vec, vec],
            out_shape=[full, jax.ShapeDtypeStruct((s, D_MODEL), BF16), vshape, vshape],
            compiler_params=_params(("arbitrary",)))(dh, x, dx_res, r_prev, g_pre, g_post_prev)
    return pl.pallas_call(
        body, name=name, grid=(s // ts,), in_specs=[spec] * 3 + [vec], out_specs=[spec, vec],
        out_shape=[full, vshape], compiler_params=_params(("arbitrary",)))(dh, x, dx_res, g_pre)


def _merge_fwd(y, zg, name):
    s = y.shape[0]
    ts, tc = _tile(s, 512), 512
    nj = D_MODEL // tc

    def body(y_ref, z_ref, o_ref, acc_ref):
        b = pl.program_id(2)
        val = _sigmoid(z_ref[...]) * y_ref[...]

        @pl.when(b == 0)
        def _():
            acc_ref[...] = val

        @pl.when(b > 0)
        def _():
            acc_ref[...] += val

        @pl.when(b == N_BRANCH - 1)
        def _():
            o_ref[...] = acc_ref[...].astype(BF16)

    blk = pl.BlockSpec((ts, tc), lambda i, j, b: (i, b * nj + j))
    return pl.pallas_call(
        body, name=name, grid=(s // ts, nj, N_BRANCH), in_specs=[blk, blk],
        out_specs=pl.BlockSpec((ts, tc), lambda i, j, b: (i, j)), out_shape=jax.ShapeDtypeStruct((s, D_MODEL), BF16),
        scratch_shapes=[pltpu.VMEM((ts, tc), F32)],
        compiler_params=_params(("parallel", "parallel", "arbitrary")))(y, zg)


def _merge_bwd(dm, y, zg, name):
    s = y.shape[0]
    ts, tc = _tile(s, 512), 512
    nj = D_MODEL // tc

    def body(dm_ref, y_ref, z_ref, dy_ref, dz_ref):
        g = _sigmoid(z_ref[...])
        d = dm_ref[...]
        dy_ref[...] = (d * g).astype(BF16)
        dz_ref[...] = (d * y_ref[...] * g * (1.0 - g)).astype(BF16)

    blk = pl.BlockSpec((ts, tc), lambda i, j, b: (i, b * nj + j))
    shape = jax.ShapeDtypeStruct((s, N_GATE), BF16)
    return pl.pallas_call(
        body, name=name, grid=(s // ts, nj, N_BRANCH),
        in_specs=[pl.BlockSpec((ts, tc), lambda i, j, b: (i, j)), blk, blk], out_specs=[blk, blk],
        out_shape=[shape, shape], compiler_params=_params(("parallel", "parallel", "parallel")))(dm, y, zg)


POOL_HALO = 16


def _pool_windows(ext_ref, ts, first_row):
    outs = []
    t = first_row + lax.broadcasted_iota(jnp.int32, (ts, GDIM), 0)
    for gi, w in enumerate(POOL_WINDOWS):
        cols = pl.ds(gi * GDIM, GDIM)
        acc = ext_ref[pl.ds(POOL_HALO, ts), cols]
        cur = acc
        for k in range(1, w):
            acc = acc + ext_ref[pl.ds(POOL_HALO - k, ts), cols]
        cnt = jnp.minimum(t + 1, w).astype(F32)
        outs.append(acc / cnt - cur)
    return outs


def _pool_fwd(za, pool_w, pool_scale, name):
    s = za.shape[0]
    ts = _tile(s, 512)
    hb = ts // POOL_HALO

    def body(a_ref, halo_ref, w_ref, sc_ref, o_ref, ext_ref):
        i = pl.program_id(0)
        ext_ref[pl.ds(0, POOL_HALO), :] = jnp.where(i > 0, halo_ref[...], 0.0)
        ext_ref[pl.ds(POOL_HALO, ts), :] = a_ref[...]
        pooled = _pool_windows(ext_ref, ts, i * ts)
        for gi in range(len(POOL_WINDOWS)):
            mixed = jnp.dot(pooled[gi].astype(BF16), w_ref[gi].astype(BF16), preferred_element_type=F32)
            o_ref[:, pl.ds(gi * GDIM, GDIM)] = (mixed * sc_ref[:, pl.ds(gi * GDIM, GDIM)]).astype(BF16)

    return pl.pallas_call(
        body, name=name, grid=(s // ts,),
        in_specs=[pl.BlockSpec((ts, BR), lambda i: (i, 0)),
                  pl.BlockSpec((POOL_HALO, BR), lambda i: (jnp.maximum(i * hb - 1, 0), 0)),
                  pl.BlockSpec((4, GDIM, GDIM), lambda i: (0, 0, 0)), pl.BlockSpec((1, BR), lambda i: (0, 0))],
        out_specs=pl.BlockSpec((ts, BR), lambda i: (i, 0)), out_shape=jax.ShapeDtypeStruct((s, BR), BF16),
        scratch_shapes=[pltpu.VMEM((POOL_HALO + ts, BR), F32)], compiler_params=_params(("parallel",)))(
            za, za, pool_w, pool_scale)


def _pool_bwd(dact, za, pool_w, pool_scale, name):
    s = za.shape[0]
    ts = _tile(s, 512)
    hb = ts // POOL_HALO
    n_t = s // ts

    def body(d_ref, dhalo_ref, a_ref, halo_ref, w_ref, sc_ref, dz_ref, dw_ref, dsc_ref, ext_ref, f_ref):
        i = pl.program_id(0)
        first = i == 0
        ext_ref[pl.ds(0, POOL_HALO), :] = jnp.where(i > 0, halo_ref[...], 0.0)
        ext_ref[pl.ds(POOL_HALO, ts), :] = a_ref[...]
        pooled = _pool_windows(ext_ref, ts, i * ts)
        d_tile = d_ref[...]
        d_next = jnp.where(i < n_t - 1, dhalo_ref[...], 0.0)
        t_ext = i * ts + lax.broadcasted_iota(jnp.int32, (ts + POOL_HALO, GDIM), 0)
        dsc = []
        for gi, w in enumerate(POOL_WINDOWS):
            cols = pl.ds(gi * GDIM, GDIM)
            wg = w_ref[gi].astype(BF16)
            sc = sc_ref[:, cols]
            pg = pooled[gi].astype(BF16)
            mixed = jnp.dot(pg, wg, preferred_element_type=F32)
            dsc.append(jnp.sum(d_tile[:, gi * GDIM:(gi + 1) * GDIM] * mixed, axis=0, keepdims=True))
            dmix = jnp.concatenate([d_tile[:, gi * GDIM:(gi + 1) * GDIM], d_next[:, gi * GDIM:(gi + 1) * GDIM]], axis=0) * sc
            dmix = dmix.astype(BF16)
            dwg = lax.dot_general(pg, dmix[:ts], (((0,), (0,)), ((), ())), preferred_element_type=F32)

            @pl.when(first)
            def _():
                dw_ref[gi] = dwg

            @pl.when(jnp.logical_not(first))
            def _():
                dw_ref[gi] += dwg

            dpool = lax.dot_general(dmix, wg, (((1,), (1,)), ((), ())), preferred_element_type=F32)
            f_ref[:, cols] = dpool / jnp.minimum(t_ext + 1, w).astype(F32)
            acc = f_ref[pl.ds(0, ts), cols]
            for k in range(1, w):
                acc = acc + f_ref[pl.ds(k, ts), cols]
            dz_ref[:, cols] = (acc - dpool[:ts]).astype(BF16)
        dsc_all = jnp.concatenate(dsc, axis=1)

        @pl.when(first)
        def _():
            dsc_ref[...] = dsc_all

        @pl.when(jnp.logical_not(first))
        def _():
            dsc_ref[...] += dsc_all

    n_hb = s // POOL_HALO
    return pl.pallas_call(
        body, name=name, grid=(n_t,),
        in_specs=[pl.BlockSpec((ts, BR), lambda i: (i, 0)),
                  pl.BlockSpec((POOL_HALO, BR), lambda i: (jnp.minimum((i + 1) * hb, n_hb - 1), 0)),
                  pl.BlockSpec((ts, BR), lambda i: (i, 0)),
                  pl.BlockSpec((POOL_HALO, BR), lambda i: (jnp.maximum(i * hb - 1, 0), 0)),
                  pl.BlockSpec((4, GDIM, GDIM), lambda i: (0, 0, 0)), pl.BlockSpec((1, BR), lambda i: (0, 0))],
        out_specs=[pl.BlockSpec((ts, BR), lambda i: (i, 0)), pl.BlockSpec((4, GDIM, GDIM), lambda i: (0, 0, 0)),
                   pl.BlockSpec((1, BR), lambda i: (0, 0))],
        out_shape=[jax.ShapeDtypeStruct((s, BR), BF16), jax.ShapeDtypeStruct((4, GDIM, GDIM), F32),
                   jax.ShapeDtypeStruct((1, BR), F32)],
        scratch_shapes=[pltpu.VMEM((POOL_HALO + ts, BR), F32), pltpu.VMEM((ts + POOL_HALO, BR), F32)],
        compiler_params=_params(("arbitrary",)))(dact, dact, za, za, pool_w, pool_scale)


CONV_HALO = 32
CONV_LEAD = CONV_HALO - (CONV_WIDTH - 1)


def _conv_fwd(za, conv_w, conv_b, ng, nb, name):
    s = za.shape[0]
    ts = _tile(s, 512)
    hb = ts // CONV_HALO

    def body(a_ref, g_ref, ah_ref, gh_ref, w_ref, b_ref, ng_ref, nb_ref, yc_ref, act_ref, ext_ref):
        i = pl.program_id(0)
        ext_ref[pl.ds(0, CONV_HALO), :] = jnp.where(i > 0, ah_ref[...] * _sigmoid(gh_ref[...]), 0.0)
        ext_ref[pl.ds(CONV_HALO, ts), :] = a_ref[...] * _sigmoid(g_ref[...])
        acc = jnp.zeros((ts, BR), F32) + b_ref[...]
        for k in range(CONV_WIDTH):
            acc = acc + w_ref[pl.ds(k, 1), :] * ext_ref[pl.ds(CONV_LEAD + k, ts), :]
        yc_ref[...] = acc
        xh, _ = _ln_stats(acc)
        ln = xh * ng_ref[...] + nb_ref[...]
        act_ref[...] = (ln * _sigmoid(ln)).astype(BF16)

    tile = lambda c: pl.BlockSpec((ts, BR), lambda i: (i, c))
    halo = lambda c: pl.BlockSpec((CONV_HALO, BR), lambda i: (jnp.maximum(i * hb - 1, 0), c))
    vec = pl.BlockSpec((1, BR), lambda i: (0, 0))
    return pl.pallas_call(
        body, name=name, grid=(s // ts,),
        in_specs=[tile(1), tile(2), halo(1), halo(2), pl.BlockSpec((CONV_WIDTH, BR), lambda i: (0, 0)), vec, vec, vec],
        out_specs=[pl.BlockSpec((ts, BR), lambda i: (i, 0))] * 2,
        out_shape=[jax.ShapeDtypeStruct((s, BR), F32), jax.ShapeDtypeStruct((s, BR), BF16)],
        scratch_shapes=[pltpu.VMEM((CONV_HALO + ts, BR), F32)], compiler_params=_params(("parallel",)))(
            za, za, za, za, conv_w, conv_b, ng, nb)


def _conv_bwd_norm(dact, yc, ng, nb, name):
    s = yc.shape[0]
    ts = _tile(s, 512)

    def body(d_ref, y_ref, ng_ref, nb_ref, dy_ref, db_ref, dng_ref, dnb_ref):
        first = pl.program_id(0) == 0
        xh, r = _ln_stats(y_ref[...])
        g = ng_ref[...]
        ln = xh * g + nb_ref[...]
        sg = _sigmoid(ln)
        dln = d_ref[...] * sg * (1.0 + ln * (1.0 - sg))
        dy = _ln_bwd(xh, r, g, dln)
        dy_ref[...] = dy
        _colsum_into(db_ref, dy, first)
        _colsum_into(dng_ref, dln * xh, first)
        _colsum_into(dnb_ref, dln, first)

    vec = pl.BlockSpec((1, BR), lambda i: (0, 0))
    vshape = jax.ShapeDtypeStruct((1, BR), F32)
    return pl.pallas_call(
        body, name=name, grid=(s // ts,),
        in_specs=[pl.BlockSpec((ts, BR), lambda i: (i, 1)), pl.BlockSpec((ts, BR), lambda i: (i, 0)), vec, vec],
        out_specs=[pl.BlockSpec((ts, BR), lambda i: (i, 0)), vec, vec, vec],
        out_shape=[jax.ShapeDtypeStruct((s, BR), F32), vshape, vshape, vshape],
        compiler_params=_params(("arbitrary",)))(dact, yc, ng, nb)


def _conv_bwd_taps(dyc, za, conv_w, name):
    s = za.shape[0]
    ts = _tile(s, 512)
    hb = ts // CONV_HALO
    n_t = s // ts
    n_hb = s // CONV_HALO

    def body(d_ref, dh_ref, a_ref, g_ref, ah_ref, gh_ref, w_ref, dz_ref, dw_ref, ext_ref, f_ref):
        i = pl.program_id(0)
        first = i == 0
        a, sg = a_ref[...], _sigmoid(g_ref[...])
        ext_ref[pl.ds(0, CONV_HALO), :] = jnp.where(i > 0, ah_ref[...] * _sigmoid(gh_ref[...]), 0.0)
        ext_ref[pl.ds(CONV_HALO, ts), :] = a * sg
        d = d_ref[...]
        f_ref[pl.ds(0, ts), :] = d
        f_ref[pl.ds(ts, CONV_HALO), :] = jnp.where(i < n_t - 1, dh_ref[...], 0.0)
        dglu = jnp.zeros((ts, BR), F32)
        rows = []
        for k in range(CONV_WIDTH):
            rows.append(jnp.sum(d * ext_ref[pl.ds(CONV_LEAD + k, ts), :], axis=0, keepdims=True))
            dglu = dglu + w_ref[pl.ds(k, 1), :] * f_ref[pl.ds(CONV_WIDTH - 1 - k, ts), :]
        rows.append(jnp.zeros((1, BR), F32))
        dw = jnp.concatenate(rows, axis=0)

        @pl.when(first)
        def _():
            dw_ref[...] = dw

        @pl.when(jnp.logical_not(first))
        def _():
            dw_ref[...] += dw

        dz_ref[:, pl.ds(0, BR)] = (dglu * sg).astype(BF16)
        dz_ref[:, pl.ds(BR, BR)] = (dglu * a * sg * (1.0 - sg)).astype(BF16)

    tile = lambda c: pl.BlockSpec((ts, BR), lambda i: (i, c))
    halo = lambda c: pl.BlockSpec((CONV_HALO, BR), lambda i: (jnp.maximum(i * hb - 1, 0), c))
    return pl.pallas_call(
        body, name=name, grid=(n_t,),
        in_specs=[pl.BlockSpec((ts, BR), lambda i: (i, 0)),
                  pl.BlockSpec((CONV_HALO, BR), lambda i: (jnp.minimum((i + 1) * hb, n_hb - 1), 0)),
                  tile(1), tile(2), halo(1), halo(2), pl.BlockSpec((CONV_WIDTH, BR), lambda i: (0, 0))],
        out_specs=[pl.BlockSpec((ts, 2 * BR), lambda i: (i, 0)), pl.BlockSpec((CONV_WIDTH + 1, BR), lambda i: (0, 0))],
        out_shape=[jax.ShapeDtypeStruct((s, 2 * BR), BF16), jax.ShapeDtypeStruct((CONV_WIDTH + 1, BR), F32)],
        scratch_shapes=[pltpu.VMEM((CONV_HALO + ts, BR), F32), pltpu.VMEM((ts + CONV_HALO, BR), F32)],
        compiler_params=_params(("arbitrary",)))(dyc, dyc, za, za, za, za, conv_w)


def _tril(w):
    r = lax.broadcasted_iota(jnp.int32, (CHUNK, CHUNK), 0)
    c = lax.broadcasted_iota(jnp.int32, (CHUNK, CHUNK), 1)
    return jnp.where(c <= r, w, 0.0)


def _sgu_fwd(za, ng, nb, sgu_w, bias_b, name):
    s = za.shape[0]
    ts = _tile(s, 512)

    def body(u_ref, v_ref, ng_ref, nb_ref, w_ref, b_ref, o_ref):
        u = _gelu(u_ref[...])
        xh, _ = _ln_stats(_gelu(v_ref[...]))
        vln = (xh * ng_ref[...] + nb_ref[...]).astype(BF16)
        for gi in range(4):
            wg = _tril(w_ref[gi]).astype(BF16)
            for n in range(ts // CHUNK):
                blk = vln[n * CHUNK:(n + 1) * CHUNK, gi * GDIM:(gi + 1) * GDIM]
                sp = jnp.dot(wg, blk, preferred_element_type=F32) + b_ref[gi]
                o_ref[pl.ds(n * CHUNK, CHUNK), pl.ds(gi * GDIM, GDIM)] = (
                    u[n * CHUNK:(n + 1) * CHUNK, gi * GDIM:(gi + 1) * GDIM] * sp).astype(BF16)

    vec = pl.BlockSpec((1, BR), lambda i: (0, 0))
    cube = pl.BlockSpec((4, CHUNK, GDIM), lambda i: (0, 0, 0))
    return pl.pallas_call(
        body, name=name, grid=(s // ts,),
        in_specs=[pl.BlockSpec((ts, BR), lambda i: (i, 3)), pl.BlockSpec((ts, BR), lambda i: (i, 4)), vec, vec, cube, cube],
        out_specs=pl.BlockSpec((ts, BR), lambda i: (i, 0)), out_shape=jax.ShapeDtypeStruct((s, BR), BF16),
        compiler_params=_params(("parallel",)))(za, za, ng, nb, sgu_w, bias_b)


def _sgu_bwd(dact, za, ng, nb, sgu_w, bias_b, name):
    s = za.shape[0]
    ts = _tile(s, 512)

    def body(d_ref, u_ref, v_ref, ng_ref, nb_ref, w_ref, b_ref, dz_ref, dw_ref, db_ref, dng_ref, dnb_ref, dv_ref):
        first = pl.program_id(0) == 0
        u_raw, v_raw = u_ref[...], v_ref[...]
        u = _gelu(u_raw)
        xh, r = _ln_stats(_gelu(v_raw))
        g = ng_ref[...]
        vln = (xh * g + nb_ref[...]).astype(BF16)
        d = d_ref[...]
        dsp = d * u
        dsp16 = dsp.astype(BF16)
        for gi in range(4):
            wg = _tril(w_ref[gi]).astype(BF16)
            dwg = jnp.zeros((CHUNK, CHUNK), F32)
            dbg = jnp.zeros((CHUNK, 1), F32)
            for n in range(ts // CHUNK):
                rows, cols = slice(n * CHUNK, (n + 1) * CHUNK), slice(gi * GDIM, (gi + 1) * GDIM)
                blk = vln[rows, cols]
                sp = jnp.dot(wg, blk, preferred_element_type=F32) + b_ref[gi]
                dz_ref[pl.ds(n * CHUNK, CHUNK), pl.ds(gi * GDIM, GDIM)] = (
                    d[rows, cols] * sp * _gelu_grad(u_raw[rows, cols])).astype(BF16)
                dv_ref[pl.ds(n * CHUNK, CHUNK), pl.ds(gi * GDIM, GDIM)] = lax.dot_general(
                    wg, dsp16[rows, cols], (((0,), (0,)), ((), ())), preferred_element_type=F32)
                dwg = dwg + lax.dot_general(dsp16[rows, cols], blk, (((1,), (1,)), ((), ())), preferred_element_type=F32)
                dbg = dbg + jnp.sum(dsp[rows, cols], axis=1, keepdims=True)
            dwg = _tril(dwg)

            @pl.when(first)
            def _():
                dw_ref[gi] = dwg
                db_ref[gi] = dbg

            @pl.when(jnp.logical_not(first))
            def _():
                dw_ref[gi] += dwg
                db_ref[gi] += dbg

        dvln = dv_ref[...]
        dz_ref[:, pl.ds(BR, BR)] = (_ln_bwd(xh, r, g, dvln) * _gelu_grad(v_raw)).astype(BF16)
        _colsum_into(dng_ref, dvln * xh, first)
        _colsum_into(dnb_ref, dvln, first)

    vec = pl.BlockSpec((1, BR), lambda i: (0, 0))
    cube = pl.BlockSpec((4, CHUNK, GDIM), lambda i: (0, 0, 0))
    vshape = jax.ShapeDtypeStruct((1, BR), F32)
    return pl.pallas_call(
        body, name=name, grid=(s // ts,),
        in_specs=[pl.BlockSpec((ts, BR), lambda i: (i, 2)), pl.BlockSpec((ts, BR), lambda i: (i, 3)),
                  pl.BlockSpec((ts, BR), lambda i: (i, 4)), vec, vec, cube, cube],
        out_specs=[pl.BlockSpec((ts, 2 * BR), lambda i: (i, 0)), cube, pl.BlockSpec((4, CHUNK, 1), lambda i: (0, 0, 0)),
                   vec, vec],
        out_shape=[jax.ShapeDtypeStruct((s, 2 * BR), BF16), jax.ShapeDtypeStruct((4, CHUNK, CHUNK), F32),
                   jax.ShapeDtypeStruct((4, CHUNK, 1), F32), vshape, vshape],
        scratch_shapes=[pltpu.VMEM((ts, BR), F32)], compiler_params=_params(("arbitrary",)))(
            dact, za, za, ng, nb, sgu_w, bias_b)


KR_BLOCK = 3584 // 128


def _mla_prep(za, qg, kvg, ck, sk, name):
    s = za.shape[0]
    ts = _tile(s, 512)

    def body(cq_ref, ckv_ref, kr_ref, qg_ref, kvg_ref, c_ref, s_ref, qn_ref, kvn_ref, krr_ref):
        qn_ref[...] = _rms(cq_ref[...], qg_ref[...]).astype(BF16)
        kvn_ref[...] = _rms(ckv_ref[...], kvg_ref[...]).astype(BF16)
        kr = kr_ref[...]
        krr_ref[...] = (kr * c_ref[...] + _rot_half(kr, 128, 0) * s_ref[...]).astype(BF16)

    vec = pl.BlockSpec((1, BR), lambda i: (0, 0))
    tab = pl.BlockSpec((ts, 128), lambda i: (i, 0))
    row = pl.BlockSpec((ts, BR), lambda i: (i, 0))
    return pl.pallas_call(
        body, name=name, grid=(s // ts,),
        in_specs=[pl.BlockSpec((ts, BR), lambda i: (i, 5)), pl.BlockSpec((ts, BR), lambda i: (i, 6)),
                  pl.BlockSpec((ts, 128), lambda i: (i, KR_BLOCK)), vec, vec, tab, tab],
        out_specs=[row, row, tab],
        out_shape=[jax.ShapeDtypeStruct((s, BR), BF16), jax.ShapeDtypeStruct((s, BR), BF16),
                   jax.ShapeDtypeStruct((s, 128), BF16)],
        compiler_params=_params(("parallel",)))(za, za, za, qg, kvg, ck, sk)


def _mla_prep_bwd(dqn, dkvn, dkr_heads, za, qg, kvg, ck, sk, name):
    s = za.shape[0]
    ts = _tile(s, 512)

    def body(dq_ref, dkv_ref, dkr_ref, cq_ref, ckv_ref, qg_ref, kvg_ref, c_ref, s_ref, dz_ref, dqg_ref, dkvg_ref):
        first = pl.program_id(0) == 0
        dcq, rows_q = _rms_bwd(cq_ref[...], qg_ref[...], dq_ref[...])
        dckv, rows_kv = _rms_bwd(ckv_ref[...], kvg_ref[...], dkv_ref[...])
        dz_ref[:, pl.ds(0, BR)] = dcq.astype(BF16)
        dz_ref[:, pl.ds(BR, BR)] = dckv.astype(BF16)
        dk = dkr_ref[:, pl.ds(0, 128)]
        for h in range(1, MLA_HEADS):
            dk = dk + dkr_ref[:, pl.ds(h * 128, 128)]
        dz_ref[:, pl.ds(2 * BR, 128)] = (dk * c_ref[...] - _rot_half(dk, 128, 0) * s_ref[...]).astype(BF16)
        _colsum_into(dqg_ref, rows_q, first)
        _colsum_into(dkvg_ref, rows_kv, first)

    vec = pl.BlockSpec((1, BR), lambda i: (0, 0))
    tab = pl.BlockSpec((ts, 128), lambda i: (i, 0))
    row = pl.BlockSpec((ts, BR), lambda i: (i, 0))
    wide = 2 * BR + 128
    vshape = jax.ShapeDtypeStruct((1, BR), F32)
    return pl.pallas_call(
        body, name=name, grid=(s // ts,),
        in_specs=[row, row, pl.BlockSpec((ts, MLA_HEADS * 128), lambda i: (i, 0)),
                  pl.BlockSpec((ts, BR), lambda i: (i, 5)), pl.BlockSpec((ts, BR), lambda i: (i, 6)), vec, vec, tab, tab],
        out_specs=[pl.BlockSpec((ts, wide), lambda i: (i, 0)), vec, vec],
        out_shape=[jax.ShapeDtypeStruct((s, wide), BF16), vshape, vshape],
        compiler_params=_params(("arbitrary",)))(dqn, dkvn, dkr_heads, za, za, qg, kvg, ck, sk)


def _attn_tiles(s):
    tq, tk = _tile(s, 1024), _tile(s, 512)
    return tq, tk, tq // tk


def _causal(qi, ki, tq, tk):
    row = qi * tq + lax.broadcasted_iota(jnp.int32, (tq, tk), 0)
    col = ki * tk + lax.broadcasted_iota(jnp.int32, (tq, tk), 1)
    return col <= row


def _on_tiles(qi, ki, r, step):
    pl.when(ki < qi * r)(functools.partial(step, False))
    pl.when(jnp.logical_and(ki >= qi * r, ki < (qi + 1) * r))(functools.partial(step, True))


def _flash_fwd(q, kv, krr, name):
    s = q.shape[0]
    tq, tk, r = _attn_tiles(s)
    nq, nk = s // tq, s // tk

    def body(q_ref, kn_ref, kr_ref, v_ref, o_ref, lse_ref, m_sc, l_sc, acc_sc):
        qi, ki = pl.program_id(1), pl.program_id(2)

        @pl.when(ki == 0)
        def _():
            m_sc[...] = jnp.full((tq, 1), NEG, F32)
            l_sc[...] = jnp.zeros((tq, 1), F32)
            acc_sc[...] = jnp.zeros((tq, V_DIM), F32)

        def step(masked):
            k = jnp.concatenate([kn_ref[...], kr_ref[...]], axis=1)
            sc = lax.dot_general(q_ref[...], k, (((1,), (1,)), ((), ())), preferred_element_type=F32)
            if masked:
                sc = jnp.where(_causal(qi, ki, tq, tk), sc, NEG)
            m_prev = m_sc[...]
            m_new = jnp.maximum(m_prev, jnp.max(sc, axis=1, keepdims=True))
            alpha = jnp.exp(m_prev - m_new)
            p = jnp.exp(sc - m_new)
            l_sc[...] = alpha * l_sc[...] + jnp.sum(p, axis=1, keepdims=True)
            acc_sc[...] = alpha * acc_sc[...] + jnp.dot(p.astype(BF16), v_ref[...], preferred_element_type=F32)
            m_sc[...] = m_new

        _on_tiles(qi, ki, r, step)

        @pl.when(ki == nk - 1)
        def _():
            o_ref[...] = (acc_sc[...] / l_sc[...]).astype(BF16)
            lse_ref[...] = jnp.broadcast_to(m_sc[...] + jnp.log(l_sc[...]), (tq, 128))

    kmap = lambda c: (lambda h, qi, ki: (jnp.minimum(ki, (qi + 1) * r - 1), c(h)))
    out_blk = pl.BlockSpec((tq, 128), lambda h, qi, ki: (qi, h))
    return pl.pallas_call(
        body, name=name, grid=(MLA_HEADS, nq, nk),
        in_specs=[pl.BlockSpec((tq, QW), lambda h, qi, ki: (qi, h)), pl.BlockSpec((tk, 128), kmap(lambda h: 2 * h)),
                  pl.BlockSpec((tk, 128), kmap(lambda h: 0)), pl.BlockSpec((tk, 128), kmap(lambda h: 2 * h + 1))],
        out_specs=[out_blk, out_blk],
        out_shape=[jax.ShapeDtypeStruct((s, MLA_HEADS * V_DIM), BF16), jax.ShapeDtypeStruct((s, MLA_HEADS * 128), F32)],
        scratch_shapes=[pltpu.VMEM((tq, 1), F32), pltpu.VMEM((tq, 1), F32), pltpu.VMEM((tq, V_DIM), F32)],
        compiler_params=_params(("parallel", "parallel", "arbitrary"), VMEM_BIG))(q, kv, krr, kv)


DO_BLOCK = 3 * BR // 128


def _flash_probs(q_ref, kn_ref, kr_ref, v_ref, do_ref, o_ref, lse_ref, qi, ki, tq, tk, masked):
    k = jnp.concatenate([kn_ref[...], kr_ref[...]], axis=1)
    q = q_ref[...]
    sc = lax.dot_general(q, k, (((1,), (1,)), ((), ())), preferred_element_type=F32)
    lse = jnp.max(lse_ref[...], axis=1, keepdims=True)
    p = jnp.exp(sc - lse)
    if masked:
        p = jnp.where(_causal(qi, ki, tq, tk), p, 0.0)
    do = do_ref[...]
    delta = jnp.sum(do * o_ref[...].astype(F32), axis=1, keepdims=True)
    do = do.astype(BF16)
    dp = lax.dot_general(do, v_ref[...], (((1,), (1,)), ((), ())), preferred_element_type=F32)
    ds = (p * (dp - delta)).astype(BF16)
    return q, k, p, do, ds


def _flash_bwd_kv(q, kv, krr, dact, o, lse, name):
    s = q.shape[0]
    tq, tk, r = _attn_tiles(s)
    nq, nk = s // tq, s // tk

    def body(q_ref, kn_ref, kr_ref, v_ref, do_ref, o_ref, lse_ref, dkv_ref, dkr_ref, dk_sc, dv_sc):
        ki, qi = pl.program_id(1), pl.program_id(2)

        @pl.when(qi == 0)
        def _():
            dk_sc[...] = jnp.zeros((tk, QW), F32)
            dv_sc[...] = jnp.zeros((tk, V_DIM), F32)

        def step(masked):
            qv, _, p, dov, ds = _flash_probs(q_ref, kn_ref, kr_ref, v_ref, do_ref, o_ref, lse_ref, qi, ki, tq, tk, masked)
            dv_sc[...] += lax.dot_general(p.astype(BF16), dov, (((0,), (0,)), ((), ())), preferred_element_type=F32)
            dk_sc[...] += lax.dot_general(ds, qv, (((0,), (0,)), ((), ())), preferred_element_type=F32)

        _on_tiles(qi, ki, r, step)

        @pl.when(qi == nq - 1)
        def _():
            dkv_ref[:, pl.ds(0, 128)] = dk_sc[:, pl.ds(0, 128)].astype(BF16)
            dkv_ref[:, pl.ds(128, 128)] = dv_sc[...].astype(BF16)
            dkr_ref[...] = dk_sc[:, pl.ds(128, 128)]

    qmap = lambda c: (lambda h, ki, qi: (jnp.maximum(qi, ki // r), c(h)))
    kmap = lambda c: (lambda h, ki, qi: (ki, c(h)))
    return pl.pallas_call(
        body, name=name, grid=(MLA_HEADS, nk, nq),
        in_specs=[pl.BlockSpec((tq, QW), qmap(lambda h: h)), pl.BlockSpec((tk, 128), kmap(lambda h: 2 * h)),
                  pl.BlockSpec((tk, 128), kmap(lambda h: 0)), pl.BlockSpec((tk, 128), kmap(lambda h: 2 * h + 1)),
                  pl.BlockSpec((tq, 128), qmap(lambda h: DO_BLOCK + h)), pl.BlockSpec((tq, 128), qmap(lambda h: h)),
                  pl.BlockSpec((tq, 128), qmap(lambda h: h))],
        out_specs=[pl.BlockSpec((tk, QW), kmap(lambda h: h)), pl.BlockSpec((tk, 128), kmap(lambda h: h))],
        out_shape=[jax.ShapeDtypeStruct((s, MLA_HEADS * QW), BF16), jax.ShapeDtypeStruct((s, MLA_HEADS * 128), F32)],
        scratch_shapes=[pltpu.VMEM((tk, QW), F32), pltpu.VMEM((tk, V_DIM), F32)],
        compiler_params=_params(("parallel", "parallel", "arbitrary"), VMEM_BIG))(q, kv, krr, kv, dact, o, lse)


def _flash_bwd_q(q, kv, krr, dact, o, lse, cq, sq, name):
    s = q.shape[0]
    tq, tk, r = _attn_tiles(s)
    nq, nk = s // tq, s // tk

    def body(q_ref, kn_ref, kr_ref, v_ref, do_ref, o_ref, lse_ref, c_ref, s_ref, dq_ref, dq_sc):
        qi, ki = pl.program_id(1), pl.program_id(2)

        @pl.when(ki == 0)
        def _():
            dq_sc[...] = jnp.zeros((tq, QW), F32)

        def step(masked):
            _, k, _, _, ds = _flash_probs(q_ref, kn_ref, kr_ref, v_ref, do_ref, o_ref, lse_ref, qi, ki, tq, tk, masked)
            dq_sc[...] += jnp.dot(ds, k, preferred_element_type=F32)

        _on_tiles(qi, ki, r, step)

        @pl.when(ki == nk - 1)
        def _():
            dq = dq_sc[...] * ATT_SCALE
            dq_ref[...] = (dq * c_ref[...] - _rot_half(dq, QW, QK_NOPE) * s_ref[...]).astype(BF16)

    qmap = lambda c: (lambda h, qi, ki: (qi, c(h)))
    kmap = lambda c: (lambda h, qi, ki: (jnp.minimum(ki, (qi + 1) * r - 1), c(h)))
    return pl.pallas_call(
        body, name=name, grid=(MLA_HEADS, nq, nk),
        in_specs=[pl.BlockSpec((tq, QW), qmap(lambda h: h)), pl.BlockSpec((tk, 128), kmap(lambda h: 2 * h)),
                  pl.BlockSpec((tk, 128), kmap(lambda h: 0)), pl.BlockSpec((tk, 128), kmap(lambda h: 2 * h + 1)),
                  pl.BlockSpec((tq, 128), qmap(lambda h: DO_BLOCK + h)), pl.BlockSpec((tq, 128), qmap(lambda h: h)),
                  pl.BlockSpec((tq, 128), qmap(lambda h: h)), pl.BlockSpec((tq, QW), qmap(lambda h: 0)),
                  pl.BlockSpec((tq, QW), qmap(lambda h: 0))],
        out_specs=pl.BlockSpec((tq, QW), qmap(lambda h: h)),
        out_shape=jax.ShapeDtypeStruct((s, MLA_HEADS * QW), BF16),
        scratch_shapes=[pltpu.VMEM((tq, QW), F32)],
        compiler_params=_params(("parallel", "parallel", "arbitrary"), VMEM_BIG))(q, kv, krr, kv, dact, o, lse, cq, sq)


def _rope_tables(positions):
    inv_freq = ROPE_THETA ** (-jnp.arange(0, QK_ROPE, 2, dtype=F32) / QK_ROPE)
    ang = positions.reshape(-1).astype(F32)[:, None] * inv_freq
    cos, sin = jnp.cos(ang), jnp.sin(ang)
    s = cos.shape[0]
    one, zero = jnp.ones((s, 64), F32), jnp.zeros((s, 64), F32)
    ck = jnp.concatenate([cos, cos, one], axis=1)
    sk = jnp.concatenate([sin, sin, zero], axis=1)
    cq = jnp.concatenate([one, one, ck], axis=1)
    sq = jnp.concatenate([zero, zero, sk], axis=1)
    return ck, sk, cq, sq


def _cols_full(gathered, l):
    _, _, r, c = gathered.shape
    return gathered[:, l].transpose(1, 0, 2).reshape(r, N_DEV * c)


def _cols_by_owner(full):
    r, n = full.shape
    return full.reshape(r, N_DEV, n // N_DEV).transpose(1, 0, 2)


def _layer_weights(gat, small, l):
    w_in = _cols_full(gat["w_in"], l)
    w = {
        "w_a": jnp.concatenate([w_in[:, :N_IN_A], jnp.zeros((D_MODEL, ZA - N_IN_A), BF16)], axis=1),
        "w_g": w_in[:, N_IN_A:],
        "proj_cat": jnp.concatenate([_cols_full(gat[n], l) for n in ("pool_proj", "conv_proj", "sgu_proj", "attn_proj")],
                                    axis=0),
        "w_uq": jnp.pad(gat["w_uq"][:, l].transpose(1, 0, 2),
                        ((0, 0), (0, 0), (0, QW - QK_NOPE - QK_ROPE))).reshape(BR, MLA_HEADS * QW),
        "w_ukv": _cols_full(gat["w_ukv"], l),
        "w_out": gat["w_out"][l].reshape(D_MODEL, D_MODEL), "w_down": gat["w_down"][l].reshape(D_FF, D_MODEL),
        "w_up": gat["w_up"],
        "conv_w": small["conv_w"][l],
        "pool_w": small["pool_w"][l], "sgu_w": small["sgu_w"][l],
        "sgu_bias": jnp.broadcast_to(small["sgu_b"][l][:, :, None], (4, CHUNK, GDIM)),
    }
    for name in ("pre_mix_g", "pool_scale", "conv_b", "conv_norm_g", "conv_norm_b", "sgu_norm_g", "sgu_norm_b",
                 "q_norm_g", "kv_norm_g", "post_mix_g", "pre_mlp_g", "post_mlp_g"):
        w[name] = small[name][l][None, :]
    return w


def _layer_fwd(x, h1, w, tabs, l):
    ck, sk, cq, sq = tabs
    n = f"l{l}_"
    za = _mm(h1, w["w_a"], mode="nn", name=n + "za")
    zg = _mm(h1, w["w_g"], mode="nn", name=n + "zg")
    a_pool = _pool_fwd(za, w["pool_w"], w["pool_scale"], n + "pool")
    yc, a_conv = _conv_fwd(za, w["conv_w"], w["conv_b"], w["conv_norm_g"], w["conv_norm_b"], n + "conv")
    a_sgu = _sgu_fwd(za, w["sgu_norm_g"], w["sgu_norm_b"], w["sgu_w"], w["sgu_bias"], n + "sgu")
    qn, kvn, krr = _mla_prep(za, w["q_norm_g"], w["kv_norm_g"], ck, sk, n + "mla_prep")
    q = _mm(qn, w["w_uq"], mode="nn", name=n + "q", out_dtypes=(BF16,), extras=(cq, sq), tn=QW,
            epilogue=lambda acc, c, sn: ((acc * c + _rot_half(acc, QW, QK_NOPE) * sn) * ATT_SCALE,))
    kv = _mm(kvn, w["w_ukv"], mode="nn", name=n + "kv", out_dtypes=(BF16,))
    o, lse = _flash_fwd(q, kv, krr, n + "flash")
    act_cat = jnp.concatenate([a_pool, a_conv, a_sgu, o], axis=1)
    y = _proj_fwd(act_cat, w["proj_cat"], n + "proj")
    merged = _merge_fwd(y, zg, n + "merge")
    m2 = _mm(merged, w["w_out"], mode="nn", name=n + "out")
    x1, h2 = _post_pre(x, m2, w["post_mix_g"], w["pre_mlp_g"], n + "post_mix")
    up, act = _mm(h2, w["w_up"], mode="nn", name=n + "up", out_dtypes=(F32, BF16), mnk=(x.shape[0], D_FF, D_MODEL),
                  tn=UP_SHARD, b_spec=lambda tn, tk: pl.BlockSpec((None, None, tk, tn), lambda i, j, kk: (j, l, kk, 0)),
                  epilogue=lambda acc: (acc, jnp.square(jnp.maximum(acc, 0.0))))
    f = _mm(act, w["w_down"], mode="nn", name=n + "down")
    saved = dict(x=x, h1=h1, za=za, zg=zg, yc=yc, qn=qn, kvn=kvn, krr=krr, q=q, kv=kv, o=o, lse=lse, act_cat=act_cat,
                 y=y, merged=merged, m2=m2, x1=x1, h2=h2, up=up, act=act, f=f)
    return x1, f, saved


def _layer_bwd(dx_out, df, sv, w, tabs, l, prev):
    ck, sk, cq, sq = tabs
    n = f"l{l}_b_"
    g = {}
    g["w_down"] = _mm(sv["act"], df, mode="tn", name=n + "dw_down", out_dtypes=(BF16,))
    dup = _mm(df, w["w_down"], mode="nt", name=n + "dup", out_dtypes=(BF16,), extras=(sv["up"],),
              epilogue=lambda acc, up: (acc * 2.0 * jnp.maximum(up, 0.0),))
    g["w_up"] = _mm(sv["h2"], dup, mode="tn", name=n + "dw_up", tn=UP_SHARD,
                    o_spec=lambda tm, tn: pl.BlockSpec((None, tm, tn), lambda i, j, kk: (j, i, 0)),
                    out_struct=jax.ShapeDtypeStruct((N_DEV, D_MODEL, UP_SHARD), BF16))
    dh2 = _mm(dup, w["w_up"], mode="nt", name=n + "dh2", mnk=(dup.shape[0], D_MODEL, D_FF), tk=UP_SHARD,
              b_spec=lambda tn, tk: pl.BlockSpec((None, None, tn, tk), lambda i, j, kk: (kk, l, j, 0)))
    dx1, dm2, g["pre_mlp_g"], g["post_mix_g"] = _pre_bwd(dh2, sv["x1"], w["pre_mlp_g"], dx_out, n + "pre_mlp",
                                                           r_prev=sv["m2"], g_post_prev=w["post_mix_g"])
    g["w_out"] = _mm(sv["merged"], dm2, mode="tn", name=n + "dw_out", out_dtypes=(BF16,))
    dmerged = _mm(dm2, w["w_out"], mode="nt", name=n + "dmerged")
    dy, dzg = _merge_bwd(dmerged, sv["y"], sv["zg"], n + "merge")
    g["proj_cat"] = _proj_bwd_w(sv["act_cat"], dy, n + "dw_proj")
    dact = _proj_bwd_act(dy, w["proj_cat"], n + "dact")
    dz_pool, g["pool_w"], g["pool_scale"] = _pool_bwd(dact, sv["za"], w["pool_w"], w["pool_scale"], n + "pool")
    dyc, g["conv_b"], g["conv_norm_g"], g["conv_norm_b"] = _conv_bwd_norm(dact, sv["yc"], w["conv_norm_g"],
                                                                          w["conv_norm_b"], n + "conv_norm")
    dz_conv, g["conv_w"] = _conv_bwd_taps(dyc, sv["za"], w["conv_w"], n + "conv_taps")
    dz_sgu, g["sgu_w"], g["sgu_b"], g["sgu_norm_g"], g["sgu_norm_b"] = _sgu_bwd(
        dact, sv["za"], w["sgu_norm_g"], w["sgu_norm_b"], w["sgu_w"], w["sgu_bias"], n + "sgu")
    dkv, dkr_heads = _flash_bwd_kv(sv["q"], sv["kv"], sv["krr"], dact, sv["o"], sv["lse"], n + "flash_kv")
    dq = _flash_bwd_q(sv["q"], sv["kv"], sv["krr"], dact, sv["o"], sv["lse"], cq, sq, n + "flash_q")
    g["w_uq"] = _mm(sv["qn"], dq, mode="tn", name=n + "dw_uq", out_dtypes=(BF16,))
    dqn = _mm(dq, w["w_uq"], mode="nt", name=n + "dqn")
    g["w_ukv"] = _mm(sv["kvn"], dkv, mode="tn", name=n + "dw_ukv", out_dtypes=(BF16,))
    dkvn = _mm(dkv, w["w_ukv"], mode="nt", name=n + "dkvn")
    dz_mla, g["q_norm_g"], g["kv_norm_g"] = _mla_prep_bwd(dqn, dkvn, dkr_heads, sv["za"], w["q_norm_g"],
                                                           w["kv_norm_g"], ck, sk, n + "mla_prep")
    dza = jnp.concatenate([dz_pool, dz_conv, dz_sgu, dz_mla], axis=1)
    g["w_a"] = _mm(sv["h1"], dza, mode="tn", name=n + "dw_a", out_dtypes=(BF16,))
    g["w_g"] = _mm(sv["h1"], dzg, mode="tn", name=n + "dw_g", out_dtypes=(BF16,))
    dh1 = _mm(dza, w["w_a"], mode="nt", name=n + "dh1_a")
    dh1 = _mm(dzg, w["w_g"], mode="nt", name=n + "dh1_g", extras=(dh1,), epilogue=lambda acc, e: (acc + e,))
    if prev is None:
        dx, g["pre_mix_g"] = _pre_bwd(dh1, sv["x"], w["pre_mix_g"], dx1, n + "pre_mix")
        return dx, None, g
    dx, df_prev, g["pre_mix_g"], g_prev_post = _pre_bwd(dh1, sv["x"], w["pre_mix_g"], dx1, n + "pre_mix",
                                                         r_prev=prev[0], g_post_prev=prev[1])
    g["prev_post_mlp_g"] = g_prev_post
    return dx, df_prev, g


def _local_step(x, positions, target, gat, small):
    tabs = _rope_tables(positions)
    ws = [_layer_weights(gat, small, l) for l in range(DEPTH)]
    saved = []
    h = _pre_norm(x, ws[0]["pre_mix_g"], "l0_pre_mix")
    cur = x
    for l in range(DEPTH):
        x1, f, sv = _layer_fwd(cur, h, ws[l], tabs, l)
        saved.append(sv)
        if l + 1 < DEPTH:
            cur, h = _post_pre(x1, f, ws[l]["post_mlp_g"], ws[l + 1]["pre_mix_g"], f"l{l}_post_mlp")
    top = DEPTH - 1
    dx, df, dg_post, loss = _final_loss(saved[top]["x1"], saved[top]["f"], ws[top]["post_mlp_g"], target, "loss")
    grads = [None] * DEPTH
    post_mlp = {top: dg_post}
    for l in range(top, -1, -1):
        prev = (saved[l - 1]["f"], ws[l - 1]["post_mlp_g"]) if l > 0 else None
        dx, df, g = _layer_bwd(dx, df, saved[l], ws[l], tabs, l, prev)
        if l > 0:
            post_mlp[l - 1] = g.pop("prev_post_mlp_g")
        grads[l] = g
    for l in range(DEPTH):
        grads[l]["post_mlp_g"] = post_mlp[l]
    return loss[0, 0], dx, grads


def _big_grads_by_owner(g):
    out = {
        "w_in": _cols_by_owner(jnp.concatenate([g["w_a"][:, :N_IN_A], g["w_g"]], axis=1)),
        "pool_proj": _cols_by_owner(g["proj_cat"][0:BR]),
        "conv_proj": _cols_by_owner(g["proj_cat"][BR:2 * BR]),
        "sgu_proj": _cols_by_owner(g["proj_cat"][2 * BR:3 * BR]),
        "attn_proj": _cols_by_owner(g["proj_cat"][3 * BR:]),
        "w_uq": g["w_uq"].reshape(BR, MLA_HEADS, QW)[:, :, :QK_NOPE + QK_ROPE].transpose(1, 0, 2),
        "w_ukv": _cols_by_owner(g["w_ukv"]),
        "w_out": g["w_out"].reshape(N_DEV, D_MODEL // N_DEV, D_MODEL),
        "w_up": g["w_up"],
        "w_down": g["w_down"].reshape(N_DEV, D_FF // N_DEV, D_MODEL),
    }
    return [out[name] for name, _, _ in BIG]


def _small_grads(grads):
    small = {}

    def stack(fn):
        return jnp.stack([fn(grads[l]) for l in range(DEPTH)])

    for name, shape in SMALL:
        if name == "sgu_b":
            small[name] = stack(lambda g: g["sgu_b"][:, :, 0])
        else:
            small[name] = stack(lambda g, name=name, shape=shape: g[name].reshape(shape))
    small["conv_w"] = stack(lambda g: g["conv_w"][:CONV_WIDTH])
    return small


SMALL_ROWS = sum(DEPTH * math.prod(shape) // 128 for _, shape in SMALL)
CONVW_ROWS = DEPTH * CONV_WIDTH * BR // 128


def _pack_small(parts):
    return jnp.concatenate([parts[name].astype(F32).reshape(-1, 128) for name, _ in SMALL], axis=0)


def _unpack_small(buf):
    out, off = {}, 0
    for name, shape in SMALL:
        rows = DEPTH * math.prod(shape) // 128
        out[name] = buf[off:off + rows].reshape((DEPTH,) + shape)
        off += rows
    return out


def _mesh_pos():
    return lax.axis_index("x"), lax.axis_index("y"), lax.axis_index("c")


def _all_gather(shards, name):
    n = len(shards)

    def body(*refs):
        x_refs, out_refs = refs[:n], refs[n:2 * n]
        send_sems, recv_sems, local_sems = refs[2 * n:]
        x, y, c = _mesh_pos()
        me, sibling = (x, y, c), (x, y, 1 - c)
        chips = [(1 - x, y), (x, 1 - y), (1 - x, 1 - y)]

        def rows(t, px, py, pc):
            return out_refs[t].at[4 * px + 2 * py + pc]

        def copy(t, k, block, to, src=None):
            return pltpu.make_async_remote_copy(
                src_ref=rows(t, *block) if src is None else src, dst_ref=rows(t, *block), send_sem=send_sems.at[t, k],
                recv_sem=recv_sems.at[t, k], device_id=to, device_id_type=MESH)

        mine = [pltpu.make_async_copy(x_refs[t], rows(t, *me), local_sems.at[t]) for t in range(n)]
        for cp in mine:
            cp.start()
        first = []
        for t in range(n):
            first.append(copy(t, 0, me, sibling, src=x_refs[t]))
            first += [copy(t, 1 + j, me, (*chip, c), src=x_refs[t]) for j, chip in enumerate(chips)]
        for cp in first:
            cp.start()
        passed = []
        for t in range(n):
            for j, chip in enumerate(chips):
                copy(t, 1 + j, (*chip, c), me).wait_recv()
                passed.append(copy(t, 4 + j, (*chip, c), sibling))
                passed[-1].start()
        for t in range(n):
            copy(t, 0, sibling, me).wait_recv()
            for j, chip in enumerate(chips):
                copy(t, 4 + j, (*chip, 1 - c), me).wait_recv()
        for cp in first + passed:
            cp.wait_send()
        for cp in mine:
            cp.wait()

    hbm = pl.BlockSpec(memory_space=pl.ANY)
    return pl.pallas_call(
        body, name=name, out_shape=[jax.ShapeDtypeStruct((N_DEV,) + a.shape, a.dtype) for a in shards],
        in_specs=[hbm] * n, out_specs=[hbm] * n,
        scratch_shapes=[pltpu.SemaphoreType.DMA((n, 7)), pltpu.SemaphoreType.DMA((n, 7)),
                        pltpu.SemaphoreType.DMA((n,))])(*shards)


def _rs_sibling(gs, name):
    n = len(gs)

    def body(*refs):
        g_refs, t_refs, send_sems, recv_sems = refs[:n], refs[n:2 * n], refs[2 * n], refs[2 * n + 1]
        x, y, c = _mesh_pos()
        copies = [pltpu.make_async_remote_copy(
            src_ref=g_refs[t].at[2 * k + 1 - c], dst_ref=t_refs[t].at[k], send_sem=send_sems.at[t, k],
            recv_sem=recv_sems.at[t, k], device_id=(x, y, 1 - c), device_id_type=MESH)
            for t in range(n) for k in range(4)]
        for cp in copies:
            cp.start()
        for cp in copies:
            cp.wait_recv()
        for cp in copies:
            cp.wait_send()

    hbm = pl.BlockSpec(memory_space=pl.ANY)
    return pl.pallas_call(
        body, name=name, out_shape=[jax.ShapeDtypeStruct((4,) + g.shape[1:], g.dtype) for g in gs],
        in_specs=[hbm] * n, out_specs=[hbm] * n,
        scratch_shapes=[pltpu.SemaphoreType.DMA((n, 4)), pltpu.SemaphoreType.DMA((n, 4))])(*gs)


def _row_tile(r, c_, cap_bytes=1 << 20):
    best = 0
    for t in range(16, r + 1, 16):
        if r % t == 0 and t * c_ * 4 <= cap_bytes:
            best = t
    return best if best else r


def _rs_chip_sum(g, t, name):
    _, r, c_ = g.shape
    tr = _row_tile(r, c_)
    core = lax.axis_index("c").astype(jnp.int32).reshape(1)

    def body(core_ref, g_ref, t_ref, p_ref):
        p_ref[...] = (g_ref[...].astype(F32) + t_ref[...].astype(F32)).astype(p_ref.dtype)

    return pl.pallas_call(
        body, name=name, out_shape=jax.ShapeDtypeStruct((4, r, c_), g.dtype),
        grid_spec=pltpu.PrefetchScalarGridSpec(
            num_scalar_prefetch=1, grid=(4, r // tr),
            in_specs=[pl.BlockSpec((1, tr, c_), lambda k, i, core_ref: (2 * k + core_ref[0], i, 0)),
                      pl.BlockSpec((1, tr, c_), lambda k, i, core_ref: (k, i, 0))],
            out_specs=pl.BlockSpec((1, tr, c_), lambda k, i, core_ref: (k, i, 0))),
        compiler_params=_params(("parallel", "parallel")))(core, g, t)


def _rs_chips(ps, name):
    n = len(ps)

    def body(*refs):
        p_refs, t_refs, send_sems, recv_sems = refs[:n], refs[n:2 * n], refs[2 * n], refs[2 * n + 1]
        x, y, c = _mesh_pos()
        chips = [(1 - x, y), (x, 1 - y), (1 - x, 1 - y)]
        copies = [pltpu.make_async_remote_copy(
            src_ref=p_refs[t].at[2 * cx + cy], dst_ref=t_refs[t].at[j], send_sem=send_sems.at[t, j],
            recv_sem=recv_sems.at[t, j], device_id=(cx, cy, c), device_id_type=MESH)
            for t in range(n) for j, (cx, cy) in enumerate(chips)]
        for cp in copies:
            cp.start()
        for cp in copies:
            cp.wait_recv()
        for cp in copies:
            cp.wait_send()

    hbm = pl.BlockSpec(memory_space=pl.ANY)
    return pl.pallas_call(
        body, name=name, out_shape=[jax.ShapeDtypeStruct((3,) + p.shape[1:], p.dtype) for p in ps],
        in_specs=[hbm] * n, out_specs=[hbm] * n,
        scratch_shapes=[pltpu.SemaphoreType.DMA((n, 3)), pltpu.SemaphoreType.DMA((n, 3))])(*ps)


def _adamw_math(w, g, m, v):
    m = ADAM_B1 * m + (1.0 - ADAM_B1) * g
    v = ADAM_B2 * v + (1.0 - ADAM_B2) * jnp.square(g)
    m_hat = m / (1.0 - ADAM_B1 ** ADAM_STEP)
    v_hat = v / (1.0 - ADAM_B2 ** ADAM_STEP)
    delta = -ADAM_LR * (m_hat / (jnp.sqrt(v_hat) + ADAM_EPS) + ADAM_WD * w)
    return delta, m, v


def _adamw_big(ps, ts, w, m, v, name):
    _, r, c_ = w.shape
    tr = _row_tile(r, c_)
    chip = (2 * lax.axis_index("x") + lax.axis_index("y")).astype(jnp.int32).reshape(1)

    def body(chip_ref, p0, t0, p1, t1, w_ref, m_ref, v_ref, g_out, d_out, m_out, v_out):
        def update(p_ref, t_ref):
            g = p_ref[0].astype(F32) + t_ref[0].astype(F32) + t_ref[1].astype(F32) + t_ref[2].astype(F32)
            g_out[0] = g
            d_out[0], m_out[0], v_out[0] = _adamw_math(w_ref[0], g, m_ref[0], v_ref[0])

        pl.when(pl.program_id(0) == 0)(functools.partial(update, p0, t0))
        pl.when(pl.program_id(0) == 1)(functools.partial(update, p1, t1))

    def grad_specs(layer):
        return [pl.BlockSpec((1, tr, c_), lambda l, i, chip_ref: (chip_ref[0], jnp.where(l == layer, i, 0), 0)),
                pl.BlockSpec((3, tr, c_), lambda l, i, chip_ref: (0, jnp.where(l == layer, i, 0), 0))]

    nat = pl.BlockSpec((1, tr, c_), lambda l, i, chip_ref: (l, i, 0))
    shape = jax.ShapeDtypeStruct(w.shape, F32)
    return pl.pallas_call(
        body, name=name, out_shape=[shape] * 4,
        grid_spec=pltpu.PrefetchScalarGridSpec(
            num_scalar_prefetch=1, grid=(DEPTH, r // tr),
            in_specs=grad_specs(0) + grad_specs(1) + [nat, nat, nat], out_specs=[nat] * 4),
        compiler_params=_params(("parallel", "parallel"), VMEM_BIG))(chip, ps[0], ts[0], ps[1], ts[1], w, m, v)


def _sum_devices(parts, name):
    _, r, c_ = parts.shape

    def body(p_ref, o_ref):
        acc = p_ref[0]
        for d in range(1, N_DEV):
            acc = acc + p_ref[d]
        o_ref[...] = acc

    return pl.pallas_call(body, name=name, out_shape=jax.ShapeDtypeStruct((r, c_), F32),
                          compiler_params=_params(None, VMEM_BIG))(parts)


def _adamw_small(w, g, m, v, name):
    def body(w_ref, g_ref, m_ref, v_ref, d_out, m_out, v_out):
        d_out[...], m_out[...], v_out[...] = _adamw_math(w_ref[...], g_ref[...], m_ref[...], v_ref[...])

    shape = jax.ShapeDtypeStruct(w.shape, F32)
    return pl.pallas_call(body, name=name, out_shape=[shape] * 3)(w, g, m, v)


def kernel(x, positions, pre_mix_g, w_in, pool_w, pool_scale, pool_proj, conv_w, conv_b, conv_norm_g, conv_norm_b, conv_proj, sgu_norm_g, sgu_norm_b, sgu_w, sgu_b, sgu_proj, q_norm_g, w_uq, kv_norm_g, w_ukv, attn_proj, w_out, post_mix_g, pre_mlp_g, w_up, w_down, post_mlp_g, loss_target, m_pre_mix_g, m_w_in, m_pool_w, m_pool_scale, m_pool_proj, m_conv_w, m_conv_b, m_conv_norm_g, m_conv_norm_b, m_conv_proj, m_sgu_norm_g, m_sgu_norm_b, m_sgu_w, m_sgu_b, m_sgu_proj, m_q_norm_g, m_w_uq, m_kv_norm_g, m_w_ukv, m_attn_proj, m_w_out, m_post_mix_g, m_pre_mlp_g, m_w_up, m_w_down, m_post_mlp_g, v_pre_mix_g, v_w_in, v_pool_w, v_pool_scale, v_pool_proj, v_conv_w, v_conv_b, v_conv_norm_g, v_conv_norm_b, v_conv_proj, v_sgu_norm_g, v_sgu_norm_b, v_sgu_w, v_sgu_b, v_sgu_proj, v_q_norm_g, v_w_uq, v_kv_norm_g, v_w_ukv, v_attn_proj, v_w_out, v_post_mix_g, v_pre_mlp_g, v_w_up, v_w_down, v_post_mlp_g):
    args = dict(locals())
    wts = {n: args[n] for n in WEIGHTS}
    mom1 = {n: args["m_" + n] for n in WEIGHTS}
    mom2 = {n: args["v_" + n] for n in WEIGHTS}
    dev = 4 * lax.axis_index("x") + 2 * lax.axis_index("y") + lax.axis_index("c")

    shards, slots = [], []
    for name, axis, (r, c) in BIG:
        if axis == 1:
            shards.append(wts[name].astype(BF16).reshape(DEPTH * r, c))
            slots.append((name, None))
        else:
            for l in range(DEPTH):
                shards.append(wts[name][l].astype(BF16))
                slots.append((name, l))
    taps = jnp.pad(conv_w.reshape(-1, 128), ((0, 1), (0, 0)))
    gathered = _all_gather(shards + [taps], "gather_weights")
    gat = {}
    for (name, l), arr in zip(slots, gathered[:-1]):
        if l is None:
            gat[name] = arr.reshape(N_DEV, DEPTH, arr.shape[1] // DEPTH, arr.shape[2])
        else:
            gat.setdefault(name, [None] * DEPTH)[l] = arr
    taps = gathered[-1][:, :CONV_WIDTH].reshape(N_DEV, DEPTH, CONV_WIDTH, BR // N_DEV)
    small = {n: wts[n] for n, _ in SMALL}
    small["conv_w"] = taps.transpose(1, 2, 0, 3).reshape(DEPTH, CONV_WIDTH, BR)

    loss_part, grad_x, grads = _local_step(x[0], positions, loss_target[0], gat, small)
    small_g = _small_grads(grads)
    loss = lax.psum(loss_part, ("x", "y", "c"))

    chip_sums, from_chips = [], []
    for l in range(DEPTH):
        by_owner = _big_grads_by_owner(grads[l])
        from_sibling = _rs_sibling(by_owner, f"rs_sibling_l{l}")
        sums = [_rs_chip_sum(g, t, f"rs_chip_sum_l{l}_{name}") for g, t, (name, _, _) in zip(by_owner, from_sibling, BIG)]
        chip_sums.append(sums)
        from_chips.append(_rs_chips(sums, f"rs_chips_l{l}"))
    out = {"grad": {}, "delta": {}, "new_m": {}, "new_v": {}}
    for i, (name, _, _) in enumerate(BIG):
        res = _adamw_big([chip_sums[l][i] for l in range(DEPTH)], [from_chips[l][i] for l in range(DEPTH)],
                         wts[name], mom1[name], mom2[name], "adamw_" + name)
        for key, buf in zip(("grad", "delta", "new_m", "new_v"), res):
            out[key][name] = buf

    part = jnp.concatenate([_pack_small(small_g), small_g["conv_w"].reshape(-1, 128)], axis=0)
    total = _sum_devices(_all_gather([part], "gather_small_grads")[0], "sum_small_grads")
    g_small = total[:SMALL_ROWS]
    d_small, m_small, v_small = _adamw_small(_pack_small(wts), g_small, _pack_small(mom1), _pack_small(mom2), "adamw_small")
    for key, buf in (("grad", g_small), ("delta", d_small), ("new_m", m_small), ("new_v", v_small)):
        out[key].update(_unpack_small(buf))
    g_taps = total[SMALL_ROWS:].reshape(DEPTH, CONV_WIDTH, N_DEV, BR // N_DEV)
    g_taps = lax.dynamic_index_in_dim(g_taps, dev, axis=2, keepdims=False)
    flat = lambda a: a.reshape(-1, 128)
    d_taps, m_taps, v_taps = _adamw_small(flat(conv_w), flat(g_taps), flat(m_conv_w), flat(v_conv_w), "adamw_taps")
    for key, buf in (("grad", g_taps), ("delta", d_taps), ("new_m", m_taps), ("new_v", v_taps)):
        out[key]["conv_w"] = buf.reshape(conv_w.shape)

    return (loss, grad_x[None], *[out["grad"][n] for n in WEIGHTS], *[out["delta"][n] for n in WEIGHTS],
            *[out["new_m"][n] for n in WEIGHTS], *[out["new_v"][n] for n in WEIGHTS])
```

```python
import collections
import functools
import math

import jax
import jax.numpy as jnp
from jax import lax
from jax.experimental import pallas as pl
from jax.experimental.pallas import tpu as pltpu

F32 = jnp.float32
BF16 = jnp.bfloat16

D_MODEL = 2048
DEPTH = 2
EPS = 1e-6
N_BRANCH = 4
D_FF = 4 * D_MODEL
UP_SHARD = D_FF // 8
POOL_WINDOWS = (2, 4, 8, 16)
CONV_WIDTH = 31
CHUNK = 128
MLA_HEADS = 8
QK_NOPE = 128
QK_ROPE = 64
V_DIM = 128
ROPE_THETA = 10000.0
GDIM = 128
BR = 512
N_IN_A = 3648
ZA = 3712
N_GATE = N_BRANCH * D_MODEL
N_IN = N_IN_A + N_GATE
QW = 256
ACT_CAT = 3 * BR + MLA_HEADS * V_DIM
ATT_SCALE = (QK_NOPE + QK_ROPE) ** -0.5
NEG = -1e30

ADAM_LR = 0.001
ADAM_B1 = 0.9
ADAM_B2 = 0.999
ADAM_EPS = 1e-08
ADAM_WD = 0.01
ADAM_STEP = 10

N_DEV = 8
PACK_W = 1024
VMEM_BIG = 48 * 1024 * 1024
MESH = pl.DeviceIdType.MESH

BIG = (
    ("w_in", 1, (2048, 1480)),
    ("pool_proj", 1, (512, 256)),
    ("conv_proj", 1, (512, 256)),
    ("sgu_proj", 1, (512, 256)),
    ("w_uq", 1, (512, 192)),
    ("w_ukv", 1, (512, 256)),
    ("attn_proj", 1, (1024, 256)),
    ("w_out", 0, (256, 2048)),
    ("w_up", 1, (2048, 1024)),
    ("w_down", 0, (1024, 2048)),
)
SMALL = (
    ("pre_mix_g", (2048,)), ("pool_w", (4, 128, 128)), ("pool_scale", (512,)), ("conv_b", (512,)),
    ("conv_norm_g", (512,)), ("conv_norm_b", (512,)), ("sgu_norm_g", (512,)), ("sgu_norm_b", (512,)),
    ("sgu_w", (4, 128, 128)), ("sgu_b", (4, 128)), ("q_norm_g", (512,)), ("kv_norm_g", (512,)),
    ("post_mix_g", (2048,)), ("pre_mlp_g", (2048,)), ("post_mlp_g", (2048,)),
)
WEIGHTS = ("pre_mix_g", "w_in", "pool_w", "pool_scale", "pool_proj", "conv_w", "conv_b", "conv_norm_g", "conv_norm_b",
           "conv_proj", "sgu_norm_g", "sgu_norm_b", "sgu_w", "sgu_b", "sgu_proj", "q_norm_g", "w_uq", "kv_norm_g",
           "w_ukv", "attn_proj", "w_out", "post_mix_g", "pre_mlp_g", "w_up", "w_down", "post_mlp_g")


def _params(sem=None, vmem=None):
    return pltpu.CompilerParams(dimension_semantics=sem, vmem_limit_bytes=vmem)


def _tile(dim, pref):
    if dim <= pref:
        return dim
    best = 0
    for t in range(128, pref + 1, 128):
        if dim % t == 0:
            best = t
    return best if best >= 256 else dim


def _sigmoid(x):
    return 1.0 / (1.0 + jnp.exp(-x))


def _gelu(x):
    k = math.sqrt(2.0 / math.pi)
    return 0.5 * x * (1.0 + jnp.tanh(k * (x + 0.044715 * x * x * x)))


def _gelu_grad(x):
    k = math.sqrt(2.0 / math.pi)
    t = jnp.tanh(k * (x + 0.044715 * x * x * x))
    return 0.5 * (1.0 + t) + 0.5 * x * (1.0 - t * t) * k * (1.0 + 3.0 * 0.044715 * x * x)


def _rms(x, g):
    r = lax.rsqrt(jnp.mean(x * x, axis=-1, keepdims=True) + EPS)
    return x * r * g


def _rms_bwd(x, g, dy):
    r = lax.rsqrt(jnp.mean(x * x, axis=-1, keepdims=True) + EPS)
    dyg = dy * g
    dx = r * dyg - x * (r * r * r) * jnp.mean(dyg * x, axis=-1, keepdims=True)
    return dx, dy * x * r


def _ln_stats(x):
    mu = jnp.mean(x, axis=-1, keepdims=True)
    xc = x - mu
    r = lax.rsqrt(jnp.mean(xc * xc, axis=-1, keepdims=True) + EPS)
    return xc * r, r


def _ln_bwd(xh, r, g, dy):
    dxh = dy * g
    return r * (dxh - jnp.mean(dxh, axis=-1, keepdims=True) - xh * jnp.mean(dxh * xh, axis=-1, keepdims=True))


def _rot_half(x, width, off):
    n = x.shape[-1]
    lane = lax.broadcasted_iota(jnp.int32, x.shape, x.ndim - 1) % width
    return jnp.where(lane - off < QK_ROPE // 2, -pltpu.roll(x, n - QK_ROPE // 2, x.ndim - 1),
                     pltpu.roll(x, QK_ROPE // 2, x.ndim - 1))


def _colsum_into(ref, val, first):
    s = jnp.sum(val, axis=0, keepdims=True)

    @pl.when(first)
    def _():
        ref[...] = s

    @pl.when(jnp.logical_not(first))
    def _():
        ref[...] += s


Exchange = collections.namedtuple("Exchange", "inputs out_shapes aliases n_pairs n_local build")


class _Plan:
    def __init__(self):
        self.jobs = {}

    def at(self, kernel, make, done):
        self.jobs.setdefault(kernel, []).append((make, done))

    def take(self, kernel):
        return self.jobs.pop(kernel, [])


def _pcall(body, operands, *, name, grid, in_specs, out_specs, out_shape, scratch_shapes=(), sem=None, vmem=None,
           plan=None):
    jobs = plan.take(name) if plan is not None else []
    if not jobs:
        return pl.pallas_call(body, name=name, grid=grid, in_specs=list(in_specs), out_specs=list(out_specs),
                              out_shape=list(out_shape), scratch_shapes=list(scratch_shapes),
                              compiler_params=_params(sem, vmem))(*operands)
    made = [(make(), done) for make, done in jobs]
    comm = [op for ops, _ in made for op in ops]
    n_in, n_out, n_scr = len(in_specs), len(out_shape), len(scratch_shapes)
    c_in = [a for op in comm for a in op.inputs]
    c_out = [s for op in comm for s in op.out_shapes]
    sems, aliases, i_off, o_off = [], {}, n_in, n_out
    for op in comm:
        sems += [pltpu.SemaphoreType.DMA((op.n_pairs,)), pltpu.SemaphoreType.DMA((op.n_pairs,)),
                 pltpu.SemaphoreType.DMA((max(op.n_local, 1),))]
        for src, dst in op.aliases.items():
            aliases[i_off + src] = o_off + dst
        i_off += len(op.inputs)
        o_off += len(op.out_shapes)

    def carrier(*refs):
        ins, cins = refs[:n_in], refs[n_in:n_in + len(c_in)]
        base = n_in + len(c_in)
        outs, couts = refs[base:base + n_out], refs[base + n_out:base + n_out + len(c_out)]
        base += n_out + len(c_out)
        scr, csems = refs[base:base + n_scr], refs[base + n_scr:]
        ids = [pl.program_id(ax) for ax in range(len(grid))]
        first = functools.reduce(jnp.logical_and, [i == 0 for i in ids])
        last = functools.reduce(jnp.logical_and, [i == g - 1 for i, g in zip(ids, grid)])

        def pieces():
            res, ci, co = [], 0, 0
            for k, op in enumerate(comm):
                res.append(op.build(cins[ci:ci + len(op.inputs)], couts[co:co + len(op.out_shapes)],
                                    *csems[3 * k:3 * k + 3]))
                ci += len(op.inputs)
                co += len(op.out_shapes)
            return res

        @pl.when(first)
        def _():
            for sends, _, local in pieces():
                for cp in local + sends:
                    cp.start()

        body(*ins, *outs, *scr)

        @pl.when(last)
        def _():
            for sends, recvs, local in pieces():
                for cp in recvs:
                    cp.wait_recv()
                for cp in sends:
                    cp.wait_send()
                for cp in local:
                    cp.wait()

    hbm = pl.BlockSpec(memory_space=pl.ANY)
    res = pl.pallas_call(
        carrier, name=name, grid=grid, in_specs=list(in_specs) + [hbm] * len(c_in),
        out_specs=list(out_specs) + [hbm] * len(c_out), out_shape=list(out_shape) + c_out,
        scratch_shapes=list(scratch_shapes) + sems, input_output_aliases=aliases,
        compiler_params=_params(("arbitrary",) * len(grid), vmem))(*operands, *c_in)
    pos = n_out
    for ops, done in made:
        results = []
        for op in ops:
            results.append(list(res[pos:pos + len(op.out_shapes)]))
            pos += len(op.out_shapes)
        done(results)
    return list(res[:n_out])


def _exchange(ops, name):
    plan, got = _Plan(), []
    plan.at(name, lambda: ops, got.extend)
    _pcall(lambda: None, (), name=name, grid=(1,), in_specs=[], out_specs=[], out_shape=[], plan=plan)
    return got


def _block(ref, px, py, pc):
    return ref.at[4 * px + 2 * py + pc]


def _remote(src, dst, send_sems, recv_sems, k, to):
    return pltpu.make_async_remote_copy(src_ref=src, dst_ref=dst, send_sem=send_sems.at[k], recv_sem=recv_sems.at[k],
                                        device_id=to, device_id_type=MESH)


def _gather_own(shards):
    n = len(shards)

    def build(ins, outs, send_sems, recv_sems, local_sems):
        x, y, c = _mesh_pos()
        peers = [(x, y, 1 - c), (1 - x, y, c), (x, 1 - y, c), (1 - x, 1 - y, c)]
        sends, recvs, local = [], [], []
        for t in range(n):
            local.append(pltpu.make_async_copy(ins[t], _block(outs[t], x, y, c), local_sems.at[t]))
            for k, peer in enumerate(peers):
                sends.append(_remote(ins[t], _block(outs[t], x, y, c), send_sems, recv_sems, 4 * t + k, peer))
                recvs.append(_remote(ins[t], _block(outs[t], *peer), send_sems, recv_sems, 4 * t + k, peer))
        return sends, recvs, local

    return Exchange(list(shards), [jax.ShapeDtypeStruct((N_DEV,) + a.shape, a.dtype) for a in shards], {}, 4 * n, n, build)


def _gather_pass(bufs):
    n = len(bufs)

    def build(ins, outs, send_sems, recv_sems, local_sems):
        x, y, c = _mesh_pos()
        chips = [(1 - x, y), (x, 1 - y), (1 - x, 1 - y)]
        sends, recvs = [], []
        for t in range(n):
            for j, chip in enumerate(chips):
                mine, theirs = _block(outs[t], *chip, c), _block(outs[t], *chip, 1 - c)
                sends.append(_remote(mine, mine, send_sems, recv_sems, 3 * t + j, (x, y, 1 - c)))
                recvs.append(_remote(mine, theirs, send_sems, recv_sems, 3 * t + j, (x, y, 1 - c)))
        return sends, recvs, []

    return Exchange(list(bufs), [jax.ShapeDtypeStruct(a.shape, a.dtype) for a in bufs], {t: t for t in range(n)},
                    3 * n, 0, build)


def _scatter_sibling(gs):
    n = len(gs)

    def build(ins, outs, send_sems, recv_sems, local_sems):
        x, y, c = _mesh_pos()
        sends, recvs = [], []
        for t in range(n):
            for k in range(4):
                cp = _remote(ins[t].at[2 * k + 1 - c], outs[t].at[k], send_sems, recv_sems, 4 * t + k, (x, y, 1 - c))
                sends.append(cp)
                recvs.append(cp)
        return sends, recvs, []

    return Exchange(list(gs), [jax.ShapeDtypeStruct((4,) + g.shape[1:], g.dtype) for g in gs], {}, 4 * n, 0, build)


def _scatter_chips(ps):
    n = len(ps)

    def build(ins, outs, send_sems, recv_sems, local_sems):
        x, y, c = _mesh_pos()
        chips = [(1 - x, y), (x, 1 - y), (1 - x, 1 - y)]
        sends, recvs = [], []
        for t in range(n):
            for j, (cx, cy) in enumerate(chips):
                cp = _remote(ins[t].at[2 * cx + cy], outs[t].at[j], send_sems, recv_sems, 3 * t + j, (cx, cy, c))
                sends.append(cp)
                recvs.append(cp)
        return sends, recvs, []

    return Exchange(list(ps), [jax.ShapeDtypeStruct((3,) + p.shape[1:], p.dtype) for p in ps], {}, 3 * n, 0, build)


_DIMS = {"nn": ((1,), (0,)), "nt": ((1,), (1,)), "tn": ((0,), (0,))}


def _mm_call(a, b, *, mode, name, grid, kaxis, nk, a_spec, b_spec, o_specs, out_shape, acc_shape,
             extras=(), e_specs=(), epilogue=None, active=None, plan=None):
    ne, no = len(extras), len(out_shape)

    def body(a_ref, b_ref, *rest):
        e_refs, o_refs, acc_ref = rest[:ne], rest[ne:ne + no], rest[ne + no]
        ids = [pl.program_id(ax) for ax in range(len(grid))]
        k = ids[kaxis]

        def finish(acc):
            outs = (acc,) if epilogue is None else epilogue(acc, *[e[...] for e in e_refs])
            for o_ref, val in zip(o_refs, outs):
                o_ref[...] = val.astype(o_ref.dtype)

        def step():
            prod = lax.dot_general(a_ref[...], b_ref[...], (_DIMS[mode], ((), ())), preferred_element_type=F32)
            if nk == 1:
                finish(prod)
                return

            @pl.when(k == 0)
            def _():
                acc_ref[...] = prod

            @pl.when(k > 0)
            def _():
                acc_ref[...] += prod

        if active is None:
            step()
        else:
            pl.when(active(*ids))(step)
        if nk > 1:
            @pl.when(k == nk - 1)
            def _():
                finish(acc_ref[...])

    sem = tuple("arbitrary" if ax == kaxis else "parallel" for ax in range(len(grid)))
    return _pcall(body, (a, b, *extras), name=name, grid=grid, in_specs=[a_spec, b_spec, *e_specs],
                  out_specs=list(o_specs), out_shape=list(out_shape), scratch_shapes=[pltpu.VMEM(acc_shape, F32)],
                  sem=sem, vmem=VMEM_BIG, plan=plan)


def _mm(a, b, *, mode, name, out_dtypes=(F32,), extras=(), epilogue=None, tm=1024, tn=1024, tk=1024,
        mnk=None, b_spec=None, o_spec=None, out_struct=None, plan=None):
    if mnk is not None:
        m, n, k = mnk
    elif mode == "nn":
        (m, k), (_, n) = a.shape, b.shape
    elif mode == "nt":
        (m, k), (n, _) = a.shape, b.shape
    else:
        (k, m), (_, n) = a.shape, b.shape
    tm, tn, tk = _tile(m, tm), _tile(n, tn), _tile(k, tk)
    if tn > 2048:
        tm, tk = _tile(m, 512), _tile(k, 512)
    if tk > 2048:
        tm, tn = _tile(m, 512), _tile(n, 512)
    nk = k // tk
    if mode == "tn":
        a_spec = pl.BlockSpec((tk, tm), lambda i, j, kk: (kk, i))
    else:
        a_spec = pl.BlockSpec((tm, tk), lambda i, j, kk: (i, kk))
    if b_spec is not None:
        b_spec = b_spec(tn, tk)
    elif mode == "nt":
        b_spec = pl.BlockSpec((tn, tk), lambda i, j, kk: (j, kk))
    else:
        b_spec = pl.BlockSpec((tk, tn), lambda i, j, kk: (kk, j))

    def e_spec(e):
        if e.shape[1] == tn and n != tn:
            return pl.BlockSpec((tm, tn), lambda i, j, kk: (i, 0))
        return pl.BlockSpec((tm, tn), lambda i, j, kk: (i, j))

    e_specs = [e_spec(e) for e in extras]
    if o_spec is not None:
        o_specs = [o_spec(tm, tn)]
        out_shape = [out_struct]
    else:
        o_specs = [pl.BlockSpec((tm, tn), lambda i, j, kk: (i, j)) for _ in out_dtypes]
        out_shape = [jax.ShapeDtypeStruct((m, n), dt) for dt in out_dtypes]
    outs = _mm_call(a, b, mode=mode, name=name, grid=(m // tm, n // tn, nk), kaxis=2, nk=nk, a_spec=a_spec,
                    b_spec=b_spec, o_specs=o_specs, out_shape=out_shape, acc_shape=(tm, tn), extras=extras,
                    e_specs=e_specs, epilogue=epilogue, plan=plan)
    return outs[0] if len(outs) == 1 else outs


def _branch_of(kb):
    return jnp.minimum(kb, N_BRANCH - 1)


def _proj_fwd(act_cat, proj_cat, name, plan=None):
    s = act_cat.shape[0]
    tm, tn = _tile(s, 1024), 1024
    nj = D_MODEL // tn

    def kb(b, k):
        return jnp.where(b < N_BRANCH - 1, b, N_BRANCH - 1 + k)

    out = _mm_call(
        act_cat, proj_cat, mode="nn", name=name, grid=(s // tm, nj, N_BRANCH, 2), kaxis=3, nk=2,
        a_spec=pl.BlockSpec((tm, BR), lambda i, j, b, k: (i, kb(b, k))),
        b_spec=pl.BlockSpec((BR, tn), lambda i, j, b, k: (kb(b, k), j)),
        o_specs=[pl.BlockSpec((tm, tn), lambda i, j, b, k: (i, b * nj + j))],
        out_shape=[jax.ShapeDtypeStruct((s, N_GATE), F32)], acc_shape=(tm, tn),
        active=lambda i, j, b, k: jnp.logical_or(b == N_BRANCH - 1, k == 0), plan=plan)
    return out[0]


def _proj_bwd_act(dy, proj_cat, name, plan=None):
    s = dy.shape[0]
    tm, tk = _tile(s, 1024), 1024
    nkk = D_MODEL // tk
    nkb = ACT_CAT // BR
    out = _mm_call(
        dy, proj_cat, mode="nt", name=name, grid=(s // tm, nkb, nkk), kaxis=2, nk=nkk,
        a_spec=pl.BlockSpec((tm, tk), lambda i, kb, k: (i, _branch_of(kb) * nkk + k)),
        b_spec=pl.BlockSpec((BR, tk), lambda i, kb, k: (kb, k)),
        o_specs=[pl.BlockSpec((tm, BR), lambda i, kb, k: (i, kb))],
        out_shape=[jax.ShapeDtypeStruct((s, ACT_CAT), F32)], acc_shape=(tm, BR), plan=plan)
    return out[0]


def _proj_bwd_w(act_cat, dy, name):
    s = dy.shape[0]
    tms, tn = _tile(s, 1024), 1024
    nj = D_MODEL // tn
    nkb = ACT_CAT // BR
    nm = s // tms
    out = _mm_call(
        act_cat, dy, mode="tn", name=name, grid=(nkb, nj, nm), kaxis=2, nk=nm,
        a_spec=pl.BlockSpec((tms, BR), lambda kb, j, m: (m, kb)),
        b_spec=pl.BlockSpec((tms, tn), lambda kb, j, m: (m, _branch_of(kb) * nj + j)),
        o_specs=[pl.BlockSpec((BR, tn), lambda kb, j, m: (kb, j))],
        out_shape=[jax.ShapeDtypeStruct((ACT_CAT, D_MODEL), BF16)], acc_shape=(BR, tn))
    return out[0]


def _row_specs(ts, n_full, n_vec):
    return ([pl.BlockSpec((ts, D_MODEL), lambda i: (i, 0))] * n_full
            + [pl.BlockSpec((1, D_MODEL), lambda i: (0, 0))] * n_vec)


def _pre_norm(x, g, name):
    s = x.shape[0]
    ts = _tile(s, 256)

    def body(x_ref, g_ref, h_ref):
        h_ref[...] = _rms(x_ref[...], g_ref[...]).astype(BF16)

    return pl.pallas_call(body, name=name, grid=(s // ts,), in_specs=_row_specs(ts, 1, 1),
                          out_specs=pl.BlockSpec((ts, D_MODEL), lambda i: (i, 0)),
                          out_shape=jax.ShapeDtypeStruct((s, D_MODEL), BF16), compiler_params=_params(("parallel",)))(x, g)


def _post_pre(x, r, g_post, g_next, name):
    s = x.shape[0]
    ts = _tile(s, 256)

    def body(x_ref, r_ref, gp_ref, gn_ref, xn_ref, h_ref):
        xn = x_ref[...] + _rms(r_ref[...], gp_ref[...])
        xn_ref[...] = xn
        h_ref[...] = _rms(xn, gn_ref[...]).astype(BF16)

    spec = pl.BlockSpec((ts, D_MODEL), lambda i: (i, 0))
    return pl.pallas_call(body, name=name, grid=(s // ts,), in_specs=_row_specs(ts, 2, 2), out_specs=[spec, spec],
                          out_shape=[jax.ShapeDtypeStruct((s, D_MODEL), F32), jax.ShapeDtypeStruct((s, D_MODEL), BF16)],
                          compiler_params=_params(("parallel",)))(x, r, g_post, g_next)


def _final_loss(x, r, g_post, target, name):
    s = x.shape[0]
    ts = _tile(s, 256)

    def body(x_ref, r_ref, gp_ref, t_ref, dy_ref, dr_ref, dg_ref, loss_ref):
        first = pl.program_id(0) == 0
        rv, gp = r_ref[...], gp_ref[...]
        diff = x_ref[...] + _rms(rv, gp) - t_ref[...]
        part = 0.5 * jnp.sum(jnp.mean(diff * diff, axis=-1, keepdims=True), axis=0, keepdims=True)
        dy = diff * (1.0 / D_MODEL)
        dy_ref[...] = dy
        dr, dg_rows = _rms_bwd(rv, gp, dy)
        dr_ref[...] = dr.astype(BF16)
        _colsum_into(dg_ref, dg_rows, first)
        _colsum_into(loss_ref, jnp.broadcast_to(part, (1, 128)), first)

    spec = pl.BlockSpec((ts, D_MODEL), lambda i: (i, 0))
    vec = pl.BlockSpec((1, D_MODEL), lambda i: (0, 0))
    return pl.pallas_call(
        body, name=name, grid=(s // ts,), in_specs=[spec, spec, vec, spec],
        out_specs=[spec, spec, vec, pl.BlockSpec((1, 128), lambda i: (0, 0))],
        out_shape=[jax.ShapeDtypeStruct((s, D_MODEL), F32), jax.ShapeDtypeStruct((s, D_MODEL), BF16),
                   jax.ShapeDtypeStruct((1, D_MODEL), F32), jax.ShapeDtypeStruct((1, 128), F32)],
        compiler_params=_params(("arbitrary",)))(x, r, g_post, target)


def _pre_bwd(dh, x, g_pre, dx_res, name, r_prev=None, g_post_prev=None):
    s = x.shape[0]
    ts = _tile(s, 256)
    chain = r_prev is not None

    def body(*refs):
        if chain:
            dh_ref, x_ref, res_ref, r_ref, g_ref, gp_ref, dx_ref, dr_ref, dg_ref, dgp_ref = refs
        else:
            dh_ref, x_ref, res_ref, g_ref, dx_ref, dg_ref = refs
        first = pl.program_id(0) == 0
        dxn, dg_rows = _rms_bwd(x_ref[...], g_ref[...], dh_ref[...])
        dx = res_ref[...] + dxn
        dx_ref[...] = dx
        _colsum_into(dg_ref, dg_rows, first)
        if chain:
            dr, dgp_rows = _rms_bwd(r_ref[...], gp_ref[...], dx)
            dr_ref[...] = dr.astype(BF16)
            _colsum_into(dgp_ref, dgp_rows, first)

    spec = pl.BlockSpec((ts, D_MODEL), lambda i: (i, 0))
    vec = pl.BlockSpec((1, D_MODEL), lambda i: (0, 0))
    full = jax.ShapeDtypeStruct((s, D_MODEL), F32)
    vshape = jax.ShapeDtypeStruct((1, D_MODEL), F32)
    if chain:
        return pl.pallas_call(
            body, name=name, grid=(s // ts,), in_specs=[spec] * 4 + [vec] * 2, out_specs=[spec, spec, vec, vec],
            out_shape=[full, jax.ShapeDtypeStruct((s, D_MODEL), BF16), vshape, vshape],
            compiler_params=_params(("arbitrary",)))(dh, x, dx_res, r_prev, g_pre, g_post_prev)
    return pl.pallas_call(
        body, name=name, grid=(s // ts,), in_specs=[spec] * 3 + [vec], out_specs=[spec, vec],
        out_shape=[full, vshape], compiler_params=_params(("arbitrary",)))(dh, x, dx_res, g_pre)


def _merge_fwd(y, zg, name):
    s = y.shape[0]
    ts, tc = _tile(s, 512), 512
    nj = D_MODEL // tc

    def body(y_ref, z_ref, o_ref, acc_ref):
        b = pl.program_id(2)
        val = _sigmoid(z_ref[...]) * y_ref[...]

        @pl.when(b == 0)
        def _():
            acc_ref[...] = val

        @pl.when(b > 0)
        def _():
            acc_ref[...] += val

        @pl.when(b == N_BRANCH - 1)
        def _():
            o_ref[...] = acc_ref[...].astype(BF16)

    blk = pl.BlockSpec((ts, tc), lambda i, j, b: (i, b * nj + j))
    return pl.pallas_call(
        body, name=name, grid=(s // ts, nj, N_BRANCH), in_specs=[blk, blk],
        out_specs=pl.BlockSpec((ts, tc), lambda i, j, b: (i, j)), out_shape=jax.ShapeDtypeStruct((s, D_MODEL), BF16),
        scratch_shapes=[pltpu.VMEM((ts, tc), F32)],
        compiler_params=_params(("parallel", "parallel", "arbitrary")))(y, zg)


def _merge_bwd(dm, y, zg, name):
    s = y.shape[0]
    ts, tc = _tile(s, 512), 512
    nj = D_MODEL // tc

    def body(dm_ref, y_ref, z_ref, dy_ref, dz_ref):
        g = _sigmoid(z_ref[...])
        d = dm_ref[...]
        dy_ref[...] = (d * g).astype(BF16)
        dz_ref[...] = (d * y_ref[...] * g * (1.0 - g)).astype(BF16)

    blk = pl.BlockSpec((ts, tc), lambda i, j, b: (i, b * nj + j))
    shape = jax.ShapeDtypeStruct((s, N_GATE), BF16)
    return pl.pallas_call(
        body, name=name, grid=(s // ts, nj, N_BRANCH),
        in_specs=[pl.BlockSpec((ts, tc), lambda i, j, b: (i, j)), blk, blk], out_specs=[blk, blk],
        out_shape=[shape, shape], compiler_params=_params(("parallel", "parallel", "parallel")))(dm, y, zg)


POOL_HALO = 16


def _pool_windows(ext_ref, ts, first_row):
    outs = []
    t = first_row + lax.broadcasted_iota(jnp.int32, (ts, GDIM), 0)
    for gi, w in enumerate(POOL_WINDOWS):
        cols = pl.ds(gi * GDIM, GDIM)
        acc = ext_ref[pl.ds(POOL_HALO, ts), cols]
        cur = acc
        for k in range(1, w):
            acc = acc + ext_ref[pl.ds(POOL_HALO - k, ts), cols]
        cnt = jnp.minimum(t + 1, w).astype(F32)
        outs.append(acc / cnt - cur)
    return outs


def _pool_fwd(za, pool_w, pool_scale, name):
    s = za.shape[0]
    ts = _tile(s, 512)
    hb = ts // POOL_HALO

    def body(a_ref, halo_ref, w_ref, sc_ref, o_ref, ext_ref):
        i = pl.program_id(0)
        ext_ref[pl.ds(0, POOL_HALO), :] = jnp.where(i > 0, halo_ref[...], 0.0)
        ext_ref[pl.ds(POOL_HALO, ts), :] = a_ref[...]
        pooled = _pool_windows(ext_ref, ts, i * ts)
        for gi in range(len(POOL_WINDOWS)):
            mixed = jnp.dot(pooled[gi].astype(BF16), w_ref[gi].astype(BF16), preferred_element_type=F32)
            o_ref[:, pl.ds(gi * GDIM, GDIM)] = (mixed * sc_ref[:, pl.ds(gi * GDIM, GDIM)]).astype(BF16)

    return pl.pallas_call(
        body, name=name, grid=(s // ts,),
        in_specs=[pl.BlockSpec((ts, BR), lambda i: (i, 0)),
                  pl.BlockSpec((POOL_HALO, BR), lambda i: (jnp.maximum(i * hb - 1, 0), 0)),
                  pl.BlockSpec((4, GDIM, GDIM), lambda i: (0, 0, 0)), pl.BlockSpec((1, BR), lambda i: (0, 0))],
        out_specs=pl.BlockSpec((ts, BR), lambda i: (i, 0)), out_shape=jax.ShapeDtypeStruct((s, BR), BF16),
        scratch_shapes=[pltpu.VMEM((POOL_HALO + ts, BR), F32)], compiler_params=_params(("parallel",)))(
            za, za, pool_w, pool_scale)


def _pool_bwd(dact, za, pool_w, pool_scale, name):
    s = za.shape[0]
    ts = _tile(s, 512)
    hb = ts // POOL_HALO
    n_t = s // ts

    def body(d_ref, dhalo_ref, a_ref, halo_ref, w_ref, sc_ref, dz_ref, dw_ref, dsc_ref, ext_ref, f_ref):
        i = pl.program_id(0)
        first = i == 0
        ext_ref[pl.ds(0, POOL_HALO), :] = jnp.where(i > 0, halo_ref[...], 0.0)
        ext_ref[pl.ds(POOL_HALO, ts), :] = a_ref[...]
        pooled = _pool_windows(ext_ref, ts, i * ts)
        d_tile = d_ref[...]
        d_next = jnp.where(i < n_t - 1, dhalo_ref[...], 0.0)
        t_ext = i * ts + lax.broadcasted_iota(jnp.int32, (ts + POOL_HALO, GDIM), 0)
        dsc = []
        for gi, w in enumerate(POOL_WINDOWS):
            cols = pl.ds(gi * GDIM, GDIM)
            wg = w_ref[gi].astype(BF16)
            sc = sc_ref[:, cols]
            pg = pooled[gi].astype(BF16)
            mixed = jnp.dot(pg, wg, preferred_element_type=F32)
            dsc.append(jnp.sum(d_tile[:, gi * GDIM:(gi + 1) * GDIM] * mixed, axis=0, keepdims=True))
            dmix = jnp.concatenate([d_tile[:, gi * GDIM:(gi + 1) * GDIM], d_next[:, gi * GDIM:(gi + 1) * GDIM]], axis=0) * sc
            dmix = dmix.astype(BF16)
            dwg = lax.dot_general(pg, dmix[:ts], (((0,), (0,)), ((), ())), preferred_element_type=F32)

            @pl.when(first)
            def _():
                dw_ref[gi] = dwg

            @pl.when(jnp.logical_not(first))
            def _():
                dw_ref[gi] += dwg

            dpool = lax.dot_general(dmix, wg, (((1,), (1,)), ((), ())), preferred_element_type=F32)
            f_ref[:, cols] = dpool / jnp.minimum(t_ext + 1, w).astype(F32)
            acc = f_ref[pl.ds(0, ts), cols]
            for k in range(1, w):
                acc = acc + f_ref[pl.ds(k, ts), cols]
            dz_ref[:, cols] = (acc - dpool[:ts]).astype(BF16)
        dsc_all = jnp.concatenate(dsc, axis=1)

        @pl.when(first)
        def _():
            dsc_ref[...] = dsc_all

        @pl.when(jnp.logical_not(first))
        def _():
            dsc_ref[...] += dsc_all

    n_hb = s // POOL_HALO
    return pl.pallas_call(
        body, name=name, grid=(n_t,),
        in_specs=[pl.BlockSpec((ts, BR), lambda i: (i, 0)),
                  pl.BlockSpec((POOL_HALO, BR), lambda i: (jnp.minimum((i + 1) * hb, n_hb - 1), 0)),
                  pl.BlockSpec((ts, BR), lambda i: (i, 0)),
                  pl.BlockSpec((POOL_HALO, BR), lambda i: (jnp.maximum(i * hb - 1, 0), 0)),
                  pl.BlockSpec((4, GDIM, GDIM), lambda i: (0, 0, 0)), pl.BlockSpec((1, BR), lambda i: (0, 0))],
        out_specs=[pl.BlockSpec((ts, BR), lambda i: (i, 0)), pl.BlockSpec((4, GDIM, GDIM), lambda i: (0, 0, 0)),
                   pl.BlockSpec((1, BR), lambda i: (0, 0))],
        out_shape=[jax.ShapeDtypeStruct((s, BR), BF16), jax.ShapeDtypeStruct((4, GDIM, GDIM), F32),
                   jax.ShapeDtypeStruct((1, BR), F32)],
        scratch_shapes=[pltpu.VMEM((POOL_HALO + ts, BR), F32), pltpu.VMEM((ts + POOL_HALO, BR), F32)],
        compiler_params=_params(("arbitrary",)))(dact, dact, za, za, pool_w, pool_scale)


CONV_HALO = 32
CONV_LEAD = CONV_HALO - (CONV_WIDTH - 1)


def _conv_fwd(za, conv_w, conv_b, ng, nb, name):
    s = za.shape[0]
    ts = _tile(s, 512)
    hb = ts // CONV_HALO

    def body(a_ref, g_ref, ah_ref, gh_ref, w_ref, b_ref, ng_ref, nb_ref, yc_ref, act_ref, ext_ref):
        i = pl.program_id(0)
        ext_ref[pl.ds(0, CONV_HALO), :] = jnp.where(i > 0, ah_ref[...] * _sigmoid(gh_ref[...]), 0.0)
        ext_ref[pl.ds(CONV_HALO, ts), :] = a_ref[...] * _sigmoid(g_ref[...])
        acc = jnp.zeros((ts, BR), F32) + b_ref[...]
        for k in range(CONV_WIDTH):
            acc = acc + w_ref[pl.ds(k, 1), :] * ext_ref[pl.ds(CONV_LEAD + k, ts), :]
        yc_ref[...] = acc
        xh, _ = _ln_stats(acc)
        ln = xh * ng_ref[...] + nb_ref[...]
        act_ref[...] = (ln * _sigmoid(ln)).astype(BF16)

    tile = lambda c: pl.BlockSpec((ts, BR), lambda i: (i, c))
    halo = lambda c: pl.BlockSpec((CONV_HALO, BR), lambda i: (jnp.maximum(i * hb - 1, 0), c))
    vec = pl.BlockSpec((1, BR), lambda i: (0, 0))
    return pl.pallas_call(
        body, name=name, grid=(s // ts,),
        in_specs=[tile(1), tile(2), halo(1), halo(2), pl.BlockSpec((CONV_WIDTH, BR), lambda i: (0, 0)), vec, vec, vec],
        out_specs=[pl.BlockSpec((ts, BR), lambda i: (i, 0))] * 2,
        out_shape=[jax.ShapeDtypeStruct((s, BR), F32), jax.ShapeDtypeStruct((s, BR), BF16)],
        scratch_shapes=[pltpu.VMEM((CONV_HALO + ts, BR), F32)], compiler_params=_params(("parallel",)))(
            za, za, za, za, conv_w, conv_b, ng, nb)


def _conv_bwd_norm(dact, yc, ng, nb, name):
    s = yc.shape[0]
    ts = _tile(s, 512)

    def body(d_ref, y_ref, ng_ref, nb_ref, dy_ref, db_ref, dng_ref, dnb_ref):
        first = pl.program_id(0) == 0
        xh, r = _ln_stats(y_ref[...])
        g = ng_ref[...]
        ln = xh * g + nb_ref[...]
        sg = _sigmoid(ln)
        dln = d_ref[...] * sg * (1.0 + ln * (1.0 - sg))
        dy = _ln_bwd(xh, r, g, dln)
        dy_ref[...] = dy
        _colsum_into(db_ref, dy, first)
        _colsum_into(dng_ref, dln * xh, first)
        _colsum_into(dnb_ref, dln, first)

    vec = pl.BlockSpec((1, BR), lambda i: (0, 0))
    vshape = jax.ShapeDtypeStruct((1, BR), F32)
    return pl.pallas_call(
        body, name=name, grid=(s // ts,),
        in_specs=[pl.BlockSpec((ts, BR), lambda i: (i, 1)), pl.BlockSpec((ts, BR), lambda i: (i, 0)), vec, vec],
        out_specs=[pl.BlockSpec((ts, BR), lambda i: (i, 0)), vec, vec, vec],
        out_shape=[jax.ShapeDtypeStruct((s, BR), F32), vshape, vshape, vshape],
        compiler_params=_params(("arbitrary",)))(dact, yc, ng, nb)


def _conv_bwd_taps(dyc, za, conv_w, name):
    s = za.shape[0]
    ts = _tile(s, 512)
    hb = ts // CONV_HALO
    n_t = s // ts
    n_hb = s // CONV_HALO

    def body(d_ref, dh_ref, a_ref, g_ref, ah_ref, gh_ref, w_ref, dz_ref, dw_ref, ext_ref, f_ref):
        i = pl.program_id(0)
        first = i == 0
        a, sg = a_ref[...], _sigmoid(g_ref[...])
        ext_ref[pl.ds(0, CONV_HALO), :] = jnp.where(i > 0, ah_ref[...] * _sigmoid(gh_ref[...]), 0.0)
        ext_ref[pl.ds(CONV_HALO, ts), :] = a * sg
        d = d_ref[...]
        f_ref[pl.ds(0, ts), :] = d
        f_ref[pl.ds(ts, CONV_HALO), :] = jnp.where(i < n_t - 1, dh_ref[...], 0.0)
        dglu = jnp.zeros((ts, BR), F32)
        rows = []
        for k in range(CONV_WIDTH):
            rows.append(jnp.sum(d * ext_ref[pl.ds(CONV_LEAD + k, ts), :], axis=0, keepdims=True))
            dglu = dglu + w_ref[pl.ds(k, 1), :] * f_ref[pl.ds(CONV_WIDTH - 1 - k, ts), :]
        rows.append(jnp.zeros((1, BR), F32))
        dw = jnp.concatenate(rows, axis=0)

        @pl.when(first)
        def _():
            dw_ref[...] = dw

        @pl.when(jnp.logical_not(first))
        def _():
            dw_ref[...] += dw

        dz_ref[:, pl.ds(0, BR)] = (dglu * sg).astype(BF16)
        dz_ref[:, pl.ds(BR, BR)] = (dglu * a * sg * (1.0 - sg)).astype(BF16)

    tile = lambda c: pl.BlockSpec((ts, BR), lambda i: (i, c))
    halo = lambda c: pl.BlockSpec((CONV_HALO, BR), lambda i: (jnp.maximum(i * hb - 1, 0), c))
    return pl.pallas_call(
        body, name=name, grid=(n_t,),
        in_specs=[pl.BlockSpec((ts, BR), lambda i: (i, 0)),
                  pl.BlockSpec((CONV_HALO, BR), lambda i: (jnp.minimum((i + 1) * hb, n_hb - 1), 0)),
                  tile(1), tile(2), halo(1), halo(2), pl.BlockSpec((CONV_WIDTH, BR), lambda i: (0, 0))],
        out_specs=[pl.BlockSpec((ts, 2 * BR), lambda i: (i, 0)), pl.BlockSpec((CONV_WIDTH + 1, BR), lambda i: (0, 0))],
        out_shape=[jax.ShapeDtypeStruct((s, 2 * BR), BF16), jax.ShapeDtypeStruct((CONV_WIDTH + 1, BR), F32)],
        scratch_shapes=[pltpu.VMEM((CONV_HALO + ts, BR), F32), pltpu.VMEM((ts + CONV_HALO, BR), F32)],
        compiler_params=_params(("arbitrary",)))(dyc, dyc, za, za, za, za, conv_w)


def _tril(w):
    r = lax.broadcasted_iota(jnp.int32, (CHUNK, CHUNK), 0)
    c = lax.broadcasted_iota(jnp.int32, (CHUNK, CHUNK), 1)
    return jnp.where(c <= r, w, 0.0)


def _sgu_fwd(za, ng, nb, sgu_w, bias_b, name):
    s = za.shape[0]
    ts = _tile(s, 512)

    def body(u_ref, v_ref, ng_ref, nb_ref, w_ref, b_ref, o_ref):
        u = _gelu(u_ref[...])
        xh, _ = _ln_stats(_gelu(v_ref[...]))
        vln = (xh * ng_ref[...] + nb_ref[...]).astype(BF16)
        for gi in range(4):
            wg = _tril(w_ref[gi]).astype(BF16)
            for n in range(ts // CHUNK):
                blk = vln[n * CHUNK:(n + 1) * CHUNK, gi * GDIM:(gi + 1) * GDIM]
                sp = jnp.dot(wg, blk, preferred_element_type=F32) + b_ref[gi]
                o_ref[pl.ds(n * CHUNK, CHUNK), pl.ds(gi * GDIM, GDIM)] = (
                    u[n * CHUNK:(n + 1) * CHUNK, gi * GDIM:(gi + 1) * GDIM] * sp).astype(BF16)

    vec = pl.BlockSpec((1, BR), lambda i: (0, 0))
    cube = pl.BlockSpec((4, CHUNK, GDIM), lambda i: (0, 0, 0))
    return pl.pallas_call(
        body, name=name, grid=(s // ts,),
        in_specs=[pl.BlockSpec((ts, BR), lambda i: (i, 3)), pl.BlockSpec((ts, BR), lambda i: (i, 4)), vec, vec, cube, cube],
        out_specs=pl.BlockSpec((ts, BR), lambda i: (i, 0)), out_shape=jax.ShapeDtypeStruct((s, BR), BF16),
        compiler_params=_params(("parallel",)))(za, za, ng, nb, sgu_w, bias_b)


def _sgu_bwd(dact, za, ng, nb, sgu_w, bias_b, name):
    s = za.shape[0]
    ts = _tile(s, 512)

    def body(d_ref, u_ref, v_ref, ng_ref, nb_ref, w_ref, b_ref, dz_ref, dw_ref, db_ref, dng_ref, dnb_ref, dv_ref):
        first = pl.program_id(0) == 0
        u_raw, v_raw = u_ref[...], v_ref[...]
        u = _gelu(u_raw)
        xh, r = _ln_stats(_gelu(v_raw))
        g = ng_ref[...]
        vln = (xh * g + nb_ref[...]).astype(BF16)
        d = d_ref[...]
        dsp = d * u
        dsp16 = dsp.astype(BF16)
        for gi in range(4):
            wg = _tril(w_ref[gi]).astype(BF16)
            dwg = jnp.zeros((CHUNK, CHUNK), F32)
            dbg = jnp.zeros((CHUNK, 1), F32)
            for n in range(ts // CHUNK):
                rows, cols = slice(n * CHUNK, (n + 1) * CHUNK), slice(gi * GDIM, (gi + 1) * GDIM)
                blk = vln[rows, cols]
                sp = jnp.dot(wg, blk, preferred_element_type=F32) + b_ref[gi]
                dz_ref[pl.ds(n * CHUNK, CHUNK), pl.ds(gi * GDIM, GDIM)] = (
                    d[rows, cols] * sp * _gelu_grad(u_raw[rows, cols])).astype(BF16)
                dv_ref[pl.ds(n * CHUNK, CHUNK), pl.ds(gi * GDIM, GDIM)] = lax.dot_general(
                    wg, dsp16[rows, cols], (((0,), (0,)), ((), ())), preferred_element_type=F32)
                dwg = dwg + lax.dot_general(dsp16[rows, cols], blk, (((1,), (1,)), ((), ())), preferred_element_type=F32)
                dbg = dbg + jnp.sum(dsp[rows, cols], axis=1, keepdims=True)
            dwg = _tril(dwg)

            @pl.when(first)
            def _():
                dw_ref[gi] = dwg
                db_ref[gi] = dbg

            @pl.when(jnp.logical_not(first))
            def _():
                dw_ref[gi] += dwg
                db_ref[gi] += dbg

        dvln = dv_ref[...]
        dz_ref[:, pl.ds(BR, BR)] = (_ln_bwd(xh, r, g, dvln) * _gelu_grad(v_raw)).astype(BF16)
        _colsum_into(dng_ref, dvln * xh, first)
        _colsum_into(dnb_ref, dvln, first)

    vec = pl.BlockSpec((1, BR), lambda i: (0, 0))
    cube = pl.BlockSpec((4, CHUNK, GDIM), lambda i: (0, 0, 0))
    vshape = jax.ShapeDtypeStruct((1, BR), F32)
    return pl.pallas_call(
        body, name=name, grid=(s // ts,),
        in_specs=[pl.BlockSpec((ts, BR), lambda i: (i, 2)), pl.BlockSpec((ts, BR), lambda i: (i, 3)),
                  pl.BlockSpec((ts, BR), lambda i: (i, 4)), vec, vec, cube, cube],
        out_specs=[pl.BlockSpec((ts, 2 * BR), lambda i: (i, 0)), cube, pl.BlockSpec((4, CHUNK, 1), lambda i: (0, 0, 0)),
                   vec, vec],
        out_shape=[jax.ShapeDtypeStruct((s, 2 * BR), BF16), jax.ShapeDtypeStruct((4, CHUNK, CHUNK), F32),
                   jax.ShapeDtypeStruct((4, CHUNK, 1), F32), vshape, vshape],
        scratch_shapes=[pltpu.VMEM((ts, BR), F32)], compiler_params=_params(("arbitrary",)))(
            dact, za, za, ng, nb, sgu_w, bias_b)


KR_BLOCK = 3584 // 128


def _mla_prep(za, qg, kvg, ck, sk, name):
    s = za.shape[0]
    ts = _tile(s, 512)

    def body(cq_ref, ckv_ref, kr_ref, qg_ref, kvg_ref, c_ref, s_ref, qn_ref, kvn_ref, krr_ref):
        qn_ref[...] = _rms(cq_ref[...], qg_ref[...]).astype(BF16)
        kvn_ref[...] = _rms(ckv_ref[...], kvg_ref[...]).astype(BF16)
        kr = kr_ref[...]
        krr_ref[...] = (kr * c_ref[...] + _rot_half(kr, 128, 0) * s_ref[...]).astype(BF16)

    vec = pl.BlockSpec((1, BR), lambda i: (0, 0))
    tab = pl.BlockSpec((ts, 128), lambda i: (i, 0))
    row = pl.BlockSpec((ts, BR), lambda i: (i, 0))
    return pl.pallas_call(
        body, name=name, grid=(s // ts,),
        in_specs=[pl.BlockSpec((ts, BR), lambda i: (i, 5)), pl.BlockSpec((ts, BR), lambda i: (i, 6)),
                  pl.BlockSpec((ts, 128), lambda i: (i, KR_BLOCK)), vec, vec, tab, tab],
        out_specs=[row, row, tab],
        out_shape=[jax.ShapeDtypeStruct((s, BR), BF16), jax.ShapeDtypeStruct((s, BR), BF16),
                   jax.ShapeDtypeStruct((s, 128), BF16)],
        compiler_params=_params(("parallel",)))(za, za, za, qg, kvg, ck, sk)


def _mla_prep_bwd(dqn, dkvn, dkr_heads, za, qg, kvg, ck, sk, name):
    s = za.shape[0]
    ts = _tile(s, 512)

    def body(dq_ref, dkv_ref, dkr_ref, cq_ref, ckv_ref, qg_ref, kvg_ref, c_ref, s_ref, dz_ref, dqg_ref, dkvg_ref):
        first = pl.program_id(0) == 0
        dcq, rows_q = _rms_bwd(cq_ref[...], qg_ref[...], dq_ref[...])
        dckv, rows_kv = _rms_bwd(ckv_ref[...], kvg_ref[...], dkv_ref[...])
        dz_ref[:, pl.ds(0, BR)] = dcq.astype(BF16)
        dz_ref[:, pl.ds(BR, BR)] = dckv.astype(BF16)
        dk = dkr_ref[:, pl.ds(0, 128)]
        for h in range(1, MLA_HEADS):
            dk = dk + dkr_ref[:, pl.ds(h * 128, 128)]
        dz_ref[:, pl.ds(2 * BR, 128)] = (dk * c_ref[...] - _rot_half(dk, 128, 0) * s_ref[...]).astype(BF16)
        _colsum_into(dqg_ref, rows_q, first)
        _colsum_into(dkvg_ref, rows_kv, first)

    vec = pl.BlockSpec((1, BR), lambda i: (0, 0))
    tab = pl.BlockSpec((ts, 128), lambda i: (i, 0))
    row = pl.BlockSpec((ts, BR), lambda i: (i, 0))
    wide = 2 * BR + 128
    vshape = jax.ShapeDtypeStruct((1, BR), F32)
    return pl.pallas_call(
        body, name=name, grid=(s // ts,),
        in_specs=[row, row, pl.BlockSpec((ts, MLA_HEADS * 128), lambda i: (i, 0)),
                  pl.BlockSpec((ts, BR), lambda i: (i, 5)), pl.BlockSpec((ts, BR), lambda i: (i, 6)), vec, vec, tab, tab],
        out_specs=[pl.BlockSpec((ts, wide), lambda i: (i, 0)), vec, vec],
        out_shape=[jax.ShapeDtypeStruct((s, wide), BF16), vshape, vshape],
        compiler_params=_params(("arbitrary",)))(dqn, dkvn, dkr_heads, za, za, qg, kvg, ck, sk)


def _attn_tiles(s):
    tq, tk = _tile(s, 1024), _tile(s, 512)
    return tq, tk, tq // tk


def _causal(qi, ki, tq, tk):
    row = qi * tq + lax.broadcasted_iota(jnp.int32, (tq, tk), 0)
    col = ki * tk + lax.broadcasted_iota(jnp.int32, (tq, tk), 1)
    return col <= row


def _on_tiles(qi, ki, r, step):
    pl.when(ki < qi * r)(functools.partial(step, False))
    pl.when(jnp.logical_and(ki >= qi * r, ki < (qi + 1) * r))(functools.partial(step, True))


def _flash_fwd(q, kv, krr, name, plan=None):
    s = q.shape[0]
    tq, tk, r = _attn_tiles(s)
    nq, nk = s // tq, s // tk

    def body(q_ref, kn_ref, kr_ref, v_ref, o_ref, lse_ref, m_sc, l_sc, acc_sc):
        qi, ki = pl.program_id(1), pl.program_id(2)

        @pl.when(ki == 0)
        def _():
            m_sc[...] = jnp.full((tq, 1), NEG, F32)
            l_sc[...] = jnp.zeros((tq, 1), F32)
            acc_sc[...] = jnp.zeros((tq, V_DIM), F32)

        def step(masked):
            k = jnp.concatenate([kn_ref[...], kr_ref[...]], axis=1)
            sc = lax.dot_general(q_ref[...], k, (((1,), (1,)), ((), ())), preferred_element_type=F32)
            if masked:
                sc = jnp.where(_causal(qi, ki, tq, tk), sc, NEG)
            m_prev = m_sc[...]
            m_new = jnp.maximum(m_prev, jnp.max(sc, axis=1, keepdims=True))
            alpha = jnp.exp(m_prev - m_new)
            p = jnp.exp(sc - m_new)
            l_sc[...] = alpha * l_sc[...] + jnp.sum(p, axis=1, keepdims=True)
            acc_sc[...] = alpha * acc_sc[...] + jnp.dot(p.astype(BF16), v_ref[...], preferred_element_type=F32)
            m_sc[...] = m_new

        _on_tiles(qi, ki, r, step)

        @pl.when(ki == nk - 1)
        def _():
            o_ref[...] = (acc_sc[...] / l_sc[...]).astype(BF16)
            lse_ref[...] = jnp.broadcast_to(m_sc[...] + jnp.log(l_sc[...]), (tq, 128))

    kmap = lambda c: (lambda h, qi, ki: (jnp.minimum(ki, (qi + 1) * r - 1), c(h)))
    out_blk = pl.BlockSpec((tq, 128), lambda h, qi, ki: (qi, h))
    return _pcall(
        body, (q, kv, krr, kv), name=name, grid=(MLA_HEADS, nq, nk),
        in_specs=[pl.BlockSpec((tq, QW), lambda h, qi, ki: (qi, h)), pl.BlockSpec((tk, 128), kmap(lambda h: 2 * h)),
                  pl.BlockSpec((tk, 128), kmap(lambda h: 0)), pl.BlockSpec((tk, 128), kmap(lambda h: 2 * h + 1))],
        out_specs=[out_blk, out_blk],
        out_shape=[jax.ShapeDtypeStruct((s, MLA_HEADS * V_DIM), BF16), jax.ShapeDtypeStruct((s, MLA_HEADS * 128), F32)],
        scratch_shapes=[pltpu.VMEM((tq, 1), F32), pltpu.VMEM((tq, 1), F32), pltpu.VMEM((tq, V_DIM), F32)],
        sem=("parallel", "parallel", "arbitrary"), vmem=VMEM_BIG, plan=plan)


DO_BLOCK = 3 * BR // 128


def _flash_probs(q_ref, kn_ref, kr_ref, v_ref, do_ref, o_ref, lse_ref, qi, ki, tq, tk, masked):
    k = jnp.concatenate([kn_ref[...], kr_ref[...]], axis=1)
    q = q_ref[...]
    sc = lax.dot_general(q, k, (((1,), (1,)), ((), ())), preferred_element_type=F32)
    lse = jnp.max(lse_ref[...], axis=1, keepdims=True)
    p = jnp.exp(sc - lse)
    if masked:
        p = jnp.where(_causal(qi, ki, tq, tk), p, 0.0)
    do = do_ref[...]
    delta = jnp.sum(do * o_ref[...].astype(F32), axis=1, keepdims=True)
    do = do.astype(BF16)
    dp = lax.dot_general(do, v_ref[...], (((1,), (1,)), ((), ())), preferred_element_type=F32)
    ds = (p * (dp - delta)).astype(BF16)
    return q, k, p, do, ds


def _flash_bwd_kv(q, kv, krr, dact, o, lse, name, plan=None):
    s = q.shape[0]
    tq, tk, r = _attn_tiles(s)
    nq, nk = s // tq, s // tk

    def body(q_ref, kn_ref, kr_ref, v_ref, do_ref, o_ref, lse_ref, dkv_ref, dkr_ref, dk_sc, dv_sc):
        ki, qi = pl.program_id(1), pl.program_id(2)

        @pl.when(qi == 0)
        def _():
            dk_sc[...] = jnp.zeros((tk, QW), F32)
            dv_sc[...] = jnp.zeros((tk, V_DIM), F32)

        def step(masked):
            qv, _, p, dov, ds = _flash_probs(q_ref, kn_ref, kr_ref, v_ref, do_ref, o_ref, lse_ref, qi, ki, tq, tk, masked)
            dv_sc[...] += lax.dot_general(p.astype(BF16), dov, (((0,), (0,)), ((), ())), preferred_element_type=F32)
            dk_sc[...] += lax.dot_general(ds, qv, (((0,), (0,)), ((), ())), preferred_element_type=F32)

        _on_tiles(qi, ki, r, step)

        @pl.when(qi == nq - 1)
        def _():
            dkv_ref[:, pl.ds(0, 128)] = dk_sc[:, pl.ds(0, 128)].astype(BF16)
            dkv_ref[:, pl.ds(128, 128)] = dv_sc[...].astype(BF16)
            dkr_ref[...] = dk_sc[:, pl.ds(128, 128)]

    qmap = lambda c: (lambda h, ki, qi: (jnp.maximum(qi, ki // r), c(h)))
    kmap = lambda c: (lambda h, ki, qi: (ki, c(h)))
    return _pcall(
        body, (q, kv, krr, kv, dact, o, lse), name=name, grid=(MLA_HEADS, nk, nq),
        in_specs=[pl.BlockSpec((tq, QW), qmap(lambda h: h)), pl.BlockSpec((tk, 128), kmap(lambda h: 2 * h)),
                  pl.BlockSpec((tk, 128), kmap(lambda h: 0)), pl.BlockSpec((tk, 128), kmap(lambda h: 2 * h + 1)),
                  pl.BlockSpec((tq, 128), qmap(lambda h: DO_BLOCK + h)), pl.BlockSpec((tq, 128), qmap(lambda h: h)),
                  pl.BlockSpec((tq, 128), qmap(lambda h: h))],
        out_specs=[pl.BlockSpec((tk, QW), kmap(lambda h: h)), pl.BlockSpec((tk, 128), kmap(lambda h: h))],
        out_shape=[jax.ShapeDtypeStruct((s, MLA_HEADS * QW), BF16), jax.ShapeDtypeStruct((s, MLA_HEADS * 128), F32)],
        scratch_shapes=[pltpu.VMEM((tk, QW), F32), pltpu.VMEM((tk, V_DIM), F32)],
        sem=("parallel", "parallel", "arbitrary"), vmem=VMEM_BIG, plan=plan)


def _flash_bwd_q(q, kv, krr, dact, o, lse, cq, sq, name, plan=None):
    s = q.shape[0]
    tq, tk, r = _attn_tiles(s)
    nq, nk = s // tq, s // tk

    def body(q_ref, kn_ref, kr_ref, v_ref, do_ref, o_ref, lse_ref, c_ref, s_ref, dq_ref, dq_sc):
        qi, ki = pl.program_id(1), pl.program_id(2)

        @pl.when(ki == 0)
        def _():
            dq_sc[...] = jnp.zeros((tq, QW), F32)

        def step(masked):
            _, k, _, _, ds = _flash_probs(q_ref, kn_ref, kr_ref, v_ref, do_ref, o_ref, lse_ref, qi, ki, tq, tk, masked)
            dq_sc[...] += jnp.dot(ds, k, preferred_element_type=F32)

        _on_tiles(qi, ki, r, step)

        @pl.when(ki == nk - 1)
        def _():
            dq = dq_sc[...] * ATT_SCALE
            dq_ref[...] = (dq * c_ref[...] - _rot_half(dq, QW, QK_NOPE) * s_ref[...]).astype(BF16)

    qmap = lambda c: (lambda h, qi, ki: (qi, c(h)))
    kmap = lambda c: (lambda h, qi, ki: (jnp.minimum(ki, (qi + 1) * r - 1), c(h)))
    return _pcall(
        body, (q, kv, krr, kv, dact, o, lse, cq, sq), name=name, grid=(MLA_HEADS, nq, nk),
        in_specs=[pl.BlockSpec((tq, QW), qmap(lambda h: h)), pl.BlockSpec((tk, 128), kmap(lambda h: 2 * h)),
                  pl.BlockSpec((tk, 128), kmap(lambda h: 0)), pl.BlockSpec((tk, 128), kmap(lambda h: 2 * h + 1)),
                  pl.BlockSpec((tq, 128), qmap(lambda h: DO_BLOCK + h)), pl.BlockSpec((tq, 128), qmap(lambda h: h)),
                  pl.BlockSpec((tq, 128), qmap(lambda h: h)), pl.BlockSpec((tq, QW), qmap(lambda h: 0)),
                  pl.BlockSpec((tq, QW), qmap(lambda h: 0))],
        out_specs=[pl.BlockSpec((tq, QW), qmap(lambda h: h))],
        out_shape=[jax.ShapeDtypeStruct((s, MLA_HEADS * QW), BF16)],
        scratch_shapes=[pltpu.VMEM((tq, QW), F32)],
        sem=("parallel", "parallel", "arbitrary"), vmem=VMEM_BIG, plan=plan)[0]


def _rope_tables(positions):
    inv_freq = ROPE_THETA ** (-jnp.arange(0, QK_ROPE, 2, dtype=F32) / QK_ROPE)
    ang = positions.reshape(-1).astype(F32)[:, None] * inv_freq
    cos, sin = jnp.cos(ang), jnp.sin(ang)
    s = cos.shape[0]
    one, zero = jnp.ones((s, 64), F32), jnp.zeros((s, 64), F32)
    ck = jnp.concatenate([cos, cos, one], axis=1)
    sk = jnp.concatenate([sin, sin, zero], axis=1)
    cq = jnp.concatenate([one, one, ck], axis=1)
    sq = jnp.concatenate([zero, zero, sk], axis=1)
    return ck, sk, cq, sq


def _cols_full(gathered):
    _, r, c = gathered.shape
    return gathered.transpose(1, 0, 2).reshape(r, N_DEV * c)


def _cols_by_owner(full):
    r, n = full.shape
    return full.reshape(r, N_DEV, n // N_DEV).transpose(1, 0, 2)


def _layer_weights(gat, small, l):
    w_in = _cols_full(gat[("w_in", l)])
    w = {
        "gat": gat, "layer": l,
        "w_a": jnp.concatenate([w_in[:, :N_IN_A], jnp.zeros((D_MODEL, ZA - N_IN_A), BF16)], axis=1),
        "w_g": w_in[:, N_IN_A:],
        "proj_cat": jnp.concatenate([_cols_full(gat[(n, l)]) for n in ("pool_proj", "conv_proj", "sgu_proj", "attn_proj")],
                                    axis=0),
        "w_uq": jnp.pad(gat[("w_uq", l)].transpose(1, 0, 2),
                        ((0, 0), (0, 0), (0, QW - QK_NOPE - QK_ROPE))).reshape(BR, MLA_HEADS * QW),
        "w_ukv": _cols_full(gat[("w_ukv", l)]),
        "conv_w": small["conv_w"][l],
        "pool_w": small["pool_w"][l], "sgu_w": small["sgu_w"][l],
        "sgu_bias": jnp.broadcast_to(small["sgu_b"][l][:, :, None], (4, CHUNK, GDIM)),
    }
    for name in ("pre_mix_g", "pool_scale", "conv_b", "conv_norm_g", "conv_norm_b", "sgu_norm_g", "sgu_norm_b",
                 "q_norm_g", "kv_norm_g", "post_mix_g", "pre_mlp_g", "post_mlp_g"):
        w[name] = small[name][l][None, :]
    return w


def _late(w, name):
    arr = w["gat"][(name, w["layer"])]
    return arr if name == "w_up" else arr.reshape(N_DEV * arr.shape[1], arr.shape[2])


def _layer_fwd(x, h1, w, tabs, l, plan):
    ck, sk, cq, sq = tabs
    n = f"l{l}_"
    za = _mm(h1, w["w_a"], mode="nn", name=n + "za", plan=plan)
    zg = _mm(h1, w["w_g"], mode="nn", name=n + "zg", plan=plan)
    a_pool = _pool_fwd(za, w["pool_w"], w["pool_scale"], n + "pool")
    yc, a_conv = _conv_fwd(za, w["conv_w"], w["conv_b"], w["conv_norm_g"], w["conv_norm_b"], n + "conv")
    a_sgu = _sgu_fwd(za, w["sgu_norm_g"], w["sgu_norm_b"], w["sgu_w"], w["sgu_bias"], n + "sgu")
    qn, kvn, krr = _mla_prep(za, w["q_norm_g"], w["kv_norm_g"], ck, sk, n + "mla_prep")
    q = _mm(qn, w["w_uq"], mode="nn", name=n + "q", out_dtypes=(BF16,), extras=(cq, sq), tn=QW,
            epilogue=lambda acc, c, sn: ((acc * c + _rot_half(acc, QW, QK_NOPE) * sn) * ATT_SCALE,))
    kv = _mm(kvn, w["w_ukv"], mode="nn", name=n + "kv", out_dtypes=(BF16,))
    o, lse = _flash_fwd(q, kv, krr, n + "flash", plan=plan)
    act_cat = jnp.concatenate([a_pool, a_conv, a_sgu, o], axis=1)
    y = _proj_fwd(act_cat, w["proj_cat"], n + "proj", plan=plan)
    merged = _merge_fwd(y, zg, n + "merge")
    m2 = _mm(merged, _late(w, "w_out"), mode="nn", name=n + "out")
    x1, h2 = _post_pre(x, m2, w["post_mix_g"], w["pre_mlp_g"], n + "post_mix")
    up, act = _mm(h2, _late(w, "w_up"), mode="nn", name=n + "up", out_dtypes=(F32, BF16), plan=plan,
                  mnk=(x.shape[0], D_FF, D_MODEL), tn=UP_SHARD,
                  b_spec=lambda tn, tk: pl.BlockSpec((None, tk, tn), lambda i, j, kk: (j, kk, 0)),
                  epilogue=lambda acc: (acc, jnp.square(jnp.maximum(acc, 0.0))))
    f = _mm(act, _late(w, "w_down"), mode="nn", name=n + "down", plan=plan)
    saved = dict(x=x, h1=h1, za=za, zg=zg, yc=yc, qn=qn, kvn=kvn, krr=krr, q=q, kv=kv, o=o, lse=lse, act_cat=act_cat,
                 y=y, merged=merged, m2=m2, x1=x1, h2=h2, up=up, act=act, f=f)
    return x1, f, saved


def _layer_bwd(dx_out, df, sv, w, tabs, l, prev, plan, rs):
    ck, sk, cq, sq = tabs
    n = f"l{l}_b_"
    g = {}
    own, half = {}, {}

    def to_sibling(kernel, names):
        def done(results):
            for name, t in zip(names, results[0]):
                half[name] = _rs_chip_sum(own[name], t, f"l{l}_chip_sum_{name}")
        plan.at(n + kernel, lambda: [_scatter_sibling([own[name] for name in names])], done)

    def to_chips(kernel, names, prefix=n):
        def done(results):
            for name, t in zip(names, results[0]):
                rs[(name, l)] = (half[name], t)
        plan.at(prefix + kernel, lambda: [_scatter_chips([half[name] for name in names])], done)

    own["w_down"] = _mm(sv["act"], df, mode="tn", name=n + "dw_down", out_dtypes=(BF16,), plan=plan).reshape(
        N_DEV, D_FF // N_DEV, D_MODEL)
    to_sibling("dup", ["w_down"])
    dup = _mm(df, _late(w, "w_down"), mode="nt", name=n + "dup", out_dtypes=(BF16,), extras=(sv["up"],), plan=plan,
              epilogue=lambda acc, up: (acc * 2.0 * jnp.maximum(up, 0.0),))
    own["w_up"] = _mm(sv["h2"], dup, mode="tn", name=n + "dw_up", tn=UP_SHARD,
                      o_spec=lambda tm, tn: pl.BlockSpec((None, tm, tn), lambda i, j, kk: (j, i, 0)),
                      out_struct=jax.ShapeDtypeStruct((N_DEV, D_MODEL, UP_SHARD), BF16))
    to_chips("dh2", ["w_down"])
    to_sibling("dh2", ["w_up"])
    dh2 = _mm(dup, _late(w, "w_up"), mode="nt", name=n + "dh2", mnk=(dup.shape[0], D_MODEL, D_FF), tk=UP_SHARD,
              plan=plan, b_spec=lambda tn, tk: pl.BlockSpec((None, tn, tk), lambda i, j, kk: (kk, j, 0)))
    dx1, dm2, g["pre_mlp_g"], g["post_mix_g"] = _pre_bwd(dh2, sv["x1"], w["pre_mlp_g"], dx_out, n + "pre_mlp",
                                                           r_prev=sv["m2"], g_post_prev=w["post_mix_g"])
    own["w_out"] = _mm(sv["merged"], dm2, mode="tn", name=n + "dw_out", out_dtypes=(BF16,)).reshape(
        N_DEV, D_MODEL // N_DEV, D_MODEL)
    to_sibling("dmerged", ["w_out"])
    dmerged = _mm(dm2, _late(w, "w_out"), mode="nt", name=n + "dmerged", plan=plan)
    dy, dzg = _merge_bwd(dmerged, sv["y"], sv["zg"], n + "merge")
    d_proj = _proj_bwd_w(sv["act_cat"], dy, n + "dw_proj")
    projs = ["pool_proj", "conv_proj", "sgu_proj", "attn_proj"]
    for i, name in enumerate(projs):
        own[name] = _cols_by_owner(d_proj[i * BR:(i + 1) * BR] if i < 3 else d_proj[3 * BR:])
    to_chips("dact", ["w_up", "w_out"])
    dact = _proj_bwd_act(dy, w["proj_cat"], n + "dact", plan=plan)
    dz_pool, g["pool_w"], g["pool_scale"] = _pool_bwd(dact, sv["za"], w["pool_w"], w["pool_scale"], n + "pool")
    dyc, g["conv_b"], g["conv_norm_g"], g["conv_norm_b"] = _conv_bwd_norm(dact, sv["yc"], w["conv_norm_g"],
                                                                          w["conv_norm_b"], n + "conv_norm")
    dz_conv, g["conv_w"] = _conv_bwd_taps(dyc, sv["za"], w["conv_w"], n + "conv_taps")
    dz_sgu, g["sgu_w"], g["sgu_b"], g["sgu_norm_g"], g["sgu_norm_b"] = _sgu_bwd(
        dact, sv["za"], w["sgu_norm_g"], w["sgu_norm_b"], w["sgu_w"], w["sgu_bias"], n + "sgu")
    to_sibling("flash_kv", projs)
    dkv, dkr_heads = _flash_bwd_kv(sv["q"], sv["kv"], sv["krr"], dact, sv["o"], sv["lse"], n + "flash_kv", plan=plan)
    to_chips("flash_q", projs)
    dq = _flash_bwd_q(sv["q"], sv["kv"], sv["krr"], dact, sv["o"], sv["lse"], cq, sq, n + "flash_q", plan=plan)
    d_uq = _mm(sv["qn"], dq, mode="tn", name=n + "dw_uq", out_dtypes=(BF16,))
    own["w_uq"] = d_uq.reshape(BR, MLA_HEADS, QW)[:, :, :QK_NOPE + QK_ROPE].transpose(1, 0, 2)
    dqn = _mm(dq, w["w_uq"], mode="nt", name=n + "dqn")
    own["w_ukv"] = _cols_by_owner(_mm(sv["kvn"], dkv, mode="tn", name=n + "dw_ukv", out_dtypes=(BF16,)))
    to_sibling("dkvn", ["w_uq", "w_ukv"])
    dkvn = _mm(dkv, w["w_ukv"], mode="nt", name=n + "dkvn", plan=plan)
    dz_mla, g["q_norm_g"], g["kv_norm_g"] = _mla_prep_bwd(dqn, dkvn, dkr_heads, sv["za"], w["q_norm_g"],
                                                           w["kv_norm_g"], ck, sk, n + "mla_prep")
    dza = jnp.concatenate([dz_pool, dz_conv, dz_sgu, dz_mla], axis=1)
    to_chips("dw_a", ["w_uq", "w_ukv"])
    d_a = _mm(sv["h1"], dza, mode="tn", name=n + "dw_a", out_dtypes=(BF16,), plan=plan)
    d_g = _mm(sv["h1"], dzg, mode="tn", name=n + "dw_g", out_dtypes=(BF16,))
    own["w_in"] = _cols_by_owner(jnp.concatenate([d_a[:, :N_IN_A], d_g], axis=1))
    to_sibling("dh1_a", ["w_in"])
    dh1 = _mm(dza, w["w_a"], mode="nt", name=n + "dh1_a", plan=plan)
    dh1 = _mm(dzg, w["w_g"], mode="nt", name=n + "dh1_g", extras=(dh1,), epilogue=lambda acc, e: (acc + e,))
    if prev is None:
        dx, g["pre_mix_g"] = _pre_bwd(dh1, sv["x"], w["pre_mix_g"], dx1, n + "pre_mix")
        rs[("w_in", l)] = (half["w_in"], _exchange([_scatter_chips([half["w_in"]])], n + "rs_w_in")[0][0])
        return dx, None, g
    to_chips("dw_down", ["w_in"], prefix=f"l{l - 1}_b_")
    dx, df_prev, g["pre_mix_g"], g_prev_post = _pre_bwd(dh1, sv["x"], w["pre_mix_g"], dx1, n + "pre_mix",
                                                         r_prev=prev[0], g_post_prev=prev[1])
    g["prev_post_mlp_g"] = g_prev_post
    return dx, df_prev, g


EARLY = ("w_in", "pool_proj", "conv_proj", "sgu_proj", "w_uq", "w_ukv", "attn_proj")
GATHER_STEPS = (
    ("l0_za", (("w_up", 0), ("w_out", 0))),
    ("l0_zg", (("w_down", 0),)),
    ("l0_flash", (("w_in", 1),)),
    ("l0_proj", (("w_up", 1),) + tuple((name, 1) for name in EARLY[1:])),
    ("l0_up", (("w_down", 1), ("w_out", 1))),
    ("l0_down", ()),
)


def _plan_gathers(plan, gat, shards):
    first_half = {}
    for step, (kernel, keys) in enumerate(GATHER_STEPS):
        before = GATHER_STEPS[step - 1][1] if step else ()

        def make(keys=keys, before=before):
            ops = [_gather_pass(first_half[before])] if before else []
            return ops + ([_gather_own([shards[key] for key in keys])] if keys else [])

        def done(results, keys=keys, before=before):
            if before:
                gat.update(zip(before, results[0]))
            if keys:
                first_half[keys] = results[-1]

        plan.at(kernel, make, done)


def _local_step(x, positions, target, gat, small, shards, plan):
    tabs = _rope_tables(positions)
    _plan_gathers(plan, gat, shards)
    ws = [_layer_weights(gat, small, 0)]
    saved = []
    h = _pre_norm(x, ws[0]["pre_mix_g"], "l0_pre_mix")
    cur = x
    for l in range(DEPTH):
        x1, f, sv = _layer_fwd(cur, h, ws[l], tabs, l, plan)
        saved.append(sv)
        if l + 1 < DEPTH:
            ws.append(_layer_weights(gat, small, l + 1))
            cur, h = _post_pre(x1, f, ws[l]["post_mlp_g"], ws[l + 1]["pre_mix_g"], f"l{l}_post_mlp")
    top = DEPTH - 1
    dx, df, dg_post, loss = _final_loss(saved[top]["x1"], saved[top]["f"], ws[top]["post_mlp_g"], target, "loss")
    grads = [None] * DEPTH
    post_mlp = {top: dg_post}
    rs = {}
    for l in range(top, -1, -1):
        prev = (saved[l - 1]["f"], ws[l - 1]["post_mlp_g"]) if l > 0 else None
        dx, df, g = _layer_bwd(dx, df, saved[l], ws[l], tabs, l, prev, plan, rs)
        if l > 0:
            post_mlp[l - 1] = g.pop("prev_post_mlp_g")
        grads[l] = g
    for l in range(DEPTH):
        grads[l]["post_mlp_g"] = post_mlp[l]
    assert not plan.jobs, sorted(plan.jobs)
    return loss[0, 0], dx, grads, rs


def _small_grads(grads):
    small = {}

    def stack(fn):
        return jnp.stack([fn(grads[l]) for l in range(DEPTH)])

    for name, shape in SMALL:
        if name == "sgu_b":
            small[name] = stack(lambda g: g["sgu_b"][:, :, 0])
        else:
            small[name] = stack(lambda g, name=name, shape=shape: g[name].reshape(shape))
    small["conv_w"] = stack(lambda g: g["conv_w"][:CONV_WIDTH])
    return small


SMALL_ROWS = sum(DEPTH * math.prod(shape) // 128 for _, shape in SMALL)
CONVW_ROWS = DEPTH * CONV_WIDTH * BR // 128


def _pack_small(parts):
    return jnp.concatenate([parts[name].astype(F32).reshape(-1, 128) for name, _ in SMALL], axis=0)


def _unpack_small(buf):
    out, off = {}, 0
    for name, shape in SMALL:
        rows = DEPTH * math.prod(shape) // 128
        out[name] = buf[off:off + rows].reshape((DEPTH,) + shape)
        off += rows
    return out


def _mesh_pos():
    return lax.axis_index("x"), lax.axis_index("y"), lax.axis_index("c")


def _all_gather(shards, name):
    n = len(shards)

    def body(*refs):
        x_refs, out_refs = refs[:n], refs[n:2 * n]
        send_sems, recv_sems, local_sems = refs[2 * n:]
        x, y, c = _mesh_pos()
        me, sibling = (x, y, c), (x, y, 1 - c)
        chips = [(1 - x, y), (x, 1 - y), (1 - x, 1 - y)]

        def rows(t, px, py, pc):
            return out_refs[t].at[4 * px + 2 * py + pc]

        def copy(t, k, block, to, src=None):
            return pltpu.make_async_remote_copy(
                src_ref=rows(t, *block) if src is None else src, dst_ref=rows(t, *block), send_sem=send_sems.at[t, k],
                recv_sem=recv_sems.at[t, k], device_id=to, device_id_type=MESH)

        mine = [pltpu.make_async_copy(x_refs[t], rows(t, *me), local_sems.at[t]) for t in range(n)]
        for cp in mine:
            cp.start()
        first = []
        for t in range(n):
            first.append(copy(t, 0, me, sibling, src=x_refs[t]))
            first += [copy(t, 1 + j, me, (*chip, c), src=x_refs[t]) for j, chip in enumerate(chips)]
        for cp in first:
            cp.start()
        passed = []
        for t in range(n):
            for j, chip in enumerate(chips):
                copy(t, 1 + j, (*chip, c), me).wait_recv()
                passed.append(copy(t, 4 + j, (*chip, c), sibling))
                passed[-1].start()
        for t in range(n):
            copy(t, 0, sibling, me).wait_recv()
            for j, chip in enumerate(chips):
                copy(t, 4 + j, (*chip, 1 - c), me).wait_recv()
        for cp in first + passed:
            cp.wait_send()
        for cp in mine:
            cp.wait()

    hbm = pl.BlockSpec(memory_space=pl.ANY)
    return pl.pallas_call(
        body, name=name, out_shape=[jax.ShapeDtypeStruct((N_DEV,) + a.shape, a.dtype) for a in shards],
        in_specs=[hbm] * n, out_specs=[hbm] * n,
        scratch_shapes=[pltpu.SemaphoreType.DMA((n, 7)), pltpu.SemaphoreType.DMA((n, 7)),
                        pltpu.SemaphoreType.DMA((n,))])(*shards)


def _row_tile(r, c_, cap_bytes=1 << 20):
    best = 0
    for t in range(16, r + 1, 16):
        if r % t == 0 and t * c_ * 4 <= cap_bytes:
            best = t
    return best if best else r


def _rs_chip_sum(g, t, name):
    _, r, c_ = g.shape
    tr = _row_tile(r, c_)
    core = lax.axis_index("c").astype(jnp.int32).reshape(1)

    def body(core_ref, g_ref, t_ref, p_ref):
        p_ref[...] = (g_ref[...].astype(F32) + t_ref[...].astype(F32)).astype(p_ref.dtype)

    return pl.pallas_call(
        body, name=name, out_shape=jax.ShapeDtypeStruct((4, r, c_), g.dtype),
        grid_spec=pltpu.PrefetchScalarGridSpec(
            num_scalar_prefetch=1, grid=(4, r // tr),
            in_specs=[pl.BlockSpec((1, tr, c_), lambda k, i, core_ref: (2 * k + core_ref[0], i, 0)),
                      pl.BlockSpec((1, tr, c_), lambda k, i, core_ref: (k, i, 0))],
            out_specs=pl.BlockSpec((1, tr, c_), lambda k, i, core_ref: (k, i, 0))),
        compiler_params=_params(("parallel", "parallel")))(core, g, t)


def _adamw_math(w, g, m, v):
    m = ADAM_B1 * m + (1.0 - ADAM_B1) * g
    v = ADAM_B2 * v + (1.0 - ADAM_B2) * jnp.square(g)
    m_hat = m / (1.0 - ADAM_B1 ** ADAM_STEP)
    v_hat = v / (1.0 - ADAM_B2 ** ADAM_STEP)
    delta = -ADAM_LR * (m_hat / (jnp.sqrt(v_hat) + ADAM_EPS) + ADAM_WD * w)
    return delta, m, v


def _adamw_big(ps, ts, w, m, v, name):
    _, r, c_ = w.shape
    tr = _row_tile(r, c_)
    chip = (2 * lax.axis_index("x") + lax.axis_index("y")).astype(jnp.int32).reshape(1)

    def body(chip_ref, p0, t0, p1, t1, w_ref, m_ref, v_ref, g_out, d_out, m_out, v_out):
        def update(p_ref, t_ref):
            g = p_ref[0].astype(F32) + t_ref[0].astype(F32) + t_ref[1].astype(F32) + t_ref[2].astype(F32)
            g_out[0] = g
            d_out[0], m_out[0], v_out[0] = _adamw_math(w_ref[0], g, m_ref[0], v_ref[0])

        pl.when(pl.program_id(0) == 0)(functools.partial(update, p0, t0))
        pl.when(pl.program_id(0) == 1)(functools.partial(update, p1, t1))

    def grad_specs(layer):
        return [pl.BlockSpec((1, tr, c_), lambda l, i, chip_ref: (chip_ref[0], jnp.where(l == layer, i, 0), 0)),
                pl.BlockSpec((3, tr, c_), lambda l, i, chip_ref: (0, jnp.where(l == layer, i, 0), 0))]

    nat = pl.BlockSpec((1, tr, c_), lambda l, i, chip_ref: (l, i, 0))
    shape = jax.ShapeDtypeStruct(w.shape, F32)
    return pl.pallas_call(
        body, name=name, out_shape=[shape] * 4,
        grid_spec=pltpu.PrefetchScalarGridSpec(
            num_scalar_prefetch=1, grid=(DEPTH, r // tr),
            in_specs=grad_specs(0) + grad_specs(1) + [nat, nat, nat], out_specs=[nat] * 4),
        compiler_params=_params(("parallel", "parallel"), VMEM_BIG))(chip, ps[0], ts[0], ps[1], ts[1], w, m, v)


def _sum_devices(parts, name):
    _, r, c_ = parts.shape

    def body(p_ref, o_ref):
        acc = p_ref[0]
        for d in range(1, N_DEV):
            acc = acc + p_ref[d]
        o_ref[...] = acc

    return pl.pallas_call(body, name=name, out_shape=jax.ShapeDtypeStruct((r, c_), F32),
                          compiler_params=_params(None, VMEM_BIG))(parts)


def _adamw_small(w, g, m, v, name):
    def body(w_ref, g_ref, m_ref, v_ref, d_out, m_out, v_out):
        d_out[...], m_out[...], v_out[...] = _adamw_math(w_ref[...], g_ref[...], m_ref[...], v_ref[...])

    shape = jax.ShapeDtypeStruct(w.shape, F32)
    return pl.pallas_call(body, name=name, out_shape=[shape] * 3)(w, g, m, v)


def kernel(x, positions, pre_mix_g, w_in, pool_w, pool_scale, pool_proj, conv_w, conv_b, conv_norm_g, conv_norm_b, conv_proj, sgu_norm_g, sgu_norm_b, sgu_w, sgu_b, sgu_proj, q_norm_g, w_uq, kv_norm_g, w_ukv, attn_proj, w_out, post_mix_g, pre_mlp_g, w_up, w_down, post_mlp_g, loss_target, m_pre_mix_g, m_w_in, m_pool_w, m_pool_scale, m_pool_proj, m_conv_w, m_conv_b, m_conv_norm_g, m_conv_norm_b, m_conv_proj, m_sgu_norm_g, m_sgu_norm_b, m_sgu_w, m_sgu_b, m_sgu_proj, m_q_norm_g, m_w_uq, m_kv_norm_g, m_w_ukv, m_attn_proj, m_w_out, m_post_mix_g, m_pre_mlp_g, m_w_up, m_w_down, m_post_mlp_g, v_pre_mix_g, v_w_in, v_pool_w, v_pool_scale, v_pool_proj, v_conv_w, v_conv_b, v_conv_norm_g, v_conv_norm_b, v_conv_proj, v_sgu_norm_g, v_sgu_norm_b, v_sgu_w, v_sgu_b, v_sgu_proj, v_q_norm_g, v_w_uq, v_kv_norm_g, v_w_ukv, v_attn_proj, v_w_out, v_post_mix_g, v_pre_mlp_g, v_w_up, v_w_down, v_post_mlp_g):
    args = dict(locals())
    wts = {n: args[n] for n in WEIGHTS}
    mom1 = {n: args["m_" + n] for n in WEIGHTS}
    mom2 = {n: args["v_" + n] for n in WEIGHTS}
    dev = 4 * lax.axis_index("x") + 2 * lax.axis_index("y") + lax.axis_index("c")

    shards = {(name, l): wts[name][l].astype(BF16) for name, _, _ in BIG for l in range(DEPTH)}
    taps = jnp.pad(conv_w.reshape(-1, 128), ((0, 1), (0, 0)))
    first = [(name, 0) for name in EARLY]
    gathered = _all_gather([shards[key] for key in first] + [taps], "gather_first")
    gat = dict(zip(first, gathered[:-1]))
    taps = gathered[-1][:, :CONV_WIDTH].reshape(N_DEV, DEPTH, CONV_WIDTH, BR // N_DEV)
    small = {n: wts[n] for n, _ in SMALL}
    small["conv_w"] = taps.transpose(1, 2, 0, 3).reshape(DEPTH, CONV_WIDTH, BR)

    loss_part, grad_x, grads, rs = _local_step(x[0], positions, loss_target[0], gat, small, shards, _Plan())
    small_g = _small_grads(grads)
    loss = lax.psum(loss_part, ("x", "y", "c"))

    out = {"grad": {}, "delta": {}, "new_m": {}, "new_v": {}}
    for name, _, _ in BIG:
        res = _adamw_big([rs[(name, l)][0] for l in range(DEPTH)], [rs[(name, l)][1] for l in range(DEPTH)],
                         wts[name], mom1[name], mom2[name], "adamw_" + name)
        for key, buf in zip(("grad", "delta", "new_m", "new_v"), res):
            out[key][name] = buf

    part = jnp.concatenate([_pack_small(small_g), small_g["conv_w"].reshape(-1, 128)], axis=0)
    total = _sum_devices(_all_gather([part], "gather_small_grads")[0], "sum_small_grads")
    g_small = total[:SMALL_ROWS]
    d_small, m_small, v_small = _adamw_small(_pack_small(wts), g_small, _pack_small(mom1), _pack_small(mom2), "adamw_small")
    for key, buf in (("grad", g_small), ("delta", d_small), ("new_m", m_small), ("new_v", v_small)):
        out[key].update(_unpack_small(buf))
    g_taps = total[SMALL_ROWS:].reshape(DEPTH, CONV_WIDTH, N_DEV, BR // N_DEV)
    g_taps = lax.dynamic_index_in_dim(g_taps, dev, axis=2, keepdims=False)
    flat = lambda a: a.reshape(-1, 128)
    d_taps, m_taps, v_taps = _adamw_small(flat(conv_w), flat(g_taps), flat(m_conv_w), flat(v_conv_w), "adamw_taps")
    for key, buf in (("grad", g_taps), ("delta", d_taps), ("new_m", m_taps), ("new_v", v_taps)):
        out[key]["conv_w"] = buf.reshape(conv_w.shape)

    return (loss, grad_x[None], *[out["grad"][n] for n in WEIGHTS], *[out["delta"][n] for n in WEIGHTS],
            *[out["new_m"][n] for n in WEIGHTS], *[out["new_v"][n] for n in WEIGHTS])
```

```python
import collections
import functools
import math

import jax
import jax.numpy as jnp
from jax import lax
from jax.experimental import pallas as pl
from jax.experimental.pallas import tpu as pltpu

F32 = jnp.float32
BF16 = jnp.bfloat16

D_MODEL = 2048
DEPTH = 2
EPS = 1e-6
N_BRANCH = 4
D_FF = 4 * D_MODEL
UP_SHARD = D_FF // 8
POOL_WINDOWS = (2, 4, 8, 16)
CONV_WIDTH = 31
CHUNK = 128
MLA_HEADS = 8
QK_NOPE = 128
QK_ROPE = 64
V_DIM = 128
ROPE_THETA = 10000.0
GDIM = 128
BR = 512
N_IN_A = 3648
ZA = 3712
N_GATE = N_BRANCH * D_MODEL
N_IN = N_IN_A + N_GATE
QW = 256
ACT_CAT = 3 * BR + MLA_HEADS * V_DIM
ATT_SCALE = (QK_NOPE + QK_ROPE) ** -0.5
NEG = -1e30

ADAM_LR = 0.001
ADAM_B1 = 0.9
ADAM_B2 = 0.999
ADAM_EPS = 1e-08
ADAM_WD = 0.01
ADAM_STEP = 10

N_DEV = 8
PACK_W = 1024
VMEM_BIG = 48 * 1024 * 1024
MESH = pl.DeviceIdType.MESH

BIG = (
    ("w_in", 1, (2048, 1480)),
    ("pool_proj", 1, (512, 256)),
    ("conv_proj", 1, (512, 256)),
    ("sgu_proj", 1, (512, 256)),
    ("w_uq", 1, (512, 192)),
    ("w_ukv", 1, (512, 256)),
    ("attn_proj", 1, (1024, 256)),
    ("w_out", 0, (256, 2048)),
    ("w_up", 1, (2048, 1024)),
    ("w_down", 0, (1024, 2048)),
)
SMALL = (
    ("pre_mix_g", (2048,)), ("pool_w", (4, 128, 128)), ("pool_scale", (512,)), ("conv_b", (512,)),
    ("conv_norm_g", (512,)), ("conv_norm_b", (512,)), ("sgu_norm_g", (512,)), ("sgu_norm_b", (512,)),
    ("sgu_w", (4, 128, 128)), ("sgu_b", (4, 128)), ("q_norm_g", (512,)), ("kv_norm_g", (512,)),
    ("post_mix_g", (2048,)), ("pre_mlp_g", (2048,)), ("post_mlp_g", (2048,)),
)
WEIGHTS = ("pre_mix_g", "w_in", "pool_w", "pool_scale", "pool_proj", "conv_w", "conv_b", "conv_norm_g", "conv_norm_b",
           "conv_proj", "sgu_norm_g", "sgu_norm_b", "sgu_w", "sgu_b", "sgu_proj", "q_norm_g", "w_uq", "kv_norm_g",
           "w_ukv", "attn_proj", "w_out", "post_mix_g", "pre_mlp_g", "w_up", "w_down", "post_mlp_g")


def _params(sem=None, vmem=None):
    return pltpu.CompilerParams(dimension_semantics=sem, vmem_limit_bytes=vmem)


def _tile(dim, pref):
    if dim <= pref:
        return dim
    best = 0
    for t in range(128, pref + 1, 128):
        if dim % t == 0:
            best = t
    return best if best >= 256 else dim


def _sigmoid(x):
    return 1.0 / (1.0 + jnp.exp(-x))


def _gelu(x):
    k = math.sqrt(2.0 / math.pi)
    return 0.5 * x * (1.0 + jnp.tanh(k * (x + 0.044715 * x * x * x)))


def _gelu_grad(x):
    k = math.sqrt(2.0 / math.pi)
    t = jnp.tanh(k * (x + 0.044715 * x * x * x))
    return 0.5 * (1.0 + t) + 0.5 * x * (1.0 - t * t) * k * (1.0 + 3.0 * 0.044715 * x * x)


def _rms(x, g):
    r = lax.rsqrt(jnp.mean(x * x, axis=-1, keepdims=True) + EPS)
    return x * r * g


def _rms_bwd(x, g, dy):
    r = lax.rsqrt(jnp.mean(x * x, axis=-1, keepdims=True) + EPS)
    dyg = dy * g
    dx = r * dyg - x * (r * r * r) * jnp.mean(dyg * x, axis=-1, keepdims=True)
    return dx, dy * x * r


def _ln_stats(x):
    mu = jnp.mean(x, axis=-1, keepdims=True)
    xc = x - mu
    r = lax.rsqrt(jnp.mean(xc * xc, axis=-1, keepdims=True) + EPS)
    return xc * r, r


def _ln_bwd(xh, r, g, dy):
    dxh = dy * g
    return r * (dxh - jnp.mean(dxh, axis=-1, keepdims=True) - xh * jnp.mean(dxh * xh, axis=-1, keepdims=True))


def _rot_half(x, width, off):
    n = x.shape[-1]
    lane = lax.broadcasted_iota(jnp.int32, x.shape, x.ndim - 1) % width
    return jnp.where(lane - off < QK_ROPE // 2, -pltpu.roll(x, n - QK_ROPE // 2, x.ndim - 1),
                     pltpu.roll(x, QK_ROPE // 2, x.ndim - 1))


def _colsum_into(ref, val, first):
    s = jnp.sum(val, axis=0, keepdims=True)

    @pl.when(first)
    def _():
        ref[...] = s

    @pl.when(jnp.logical_not(first))
    def _():
        ref[...] += s


Exchange = collections.namedtuple("Exchange", "inputs out_shapes aliases n_pairs n_local build")


class _Plan:
    def __init__(self):
        self.jobs = {}

    def at(self, kernel, make, done):
        self.jobs.setdefault(kernel, []).append((make, done))

    def take(self, kernel):
        return self.jobs.pop(kernel, [])


def _pcall(body, operands, *, name, grid, in_specs, out_specs, out_shape, scratch_shapes=(), sem=None, vmem=None,
           plan=None):
    jobs = plan.take(name) if plan is not None else []
    if not jobs:
        return pl.pallas_call(body, name=name, grid=grid, in_specs=list(in_specs), out_specs=list(out_specs),
                              out_shape=list(out_shape), scratch_shapes=list(scratch_shapes),
                              compiler_params=_params(sem, vmem))(*operands)
    made = [(make(), done) for make, done in jobs]
    comm = [op for ops, _ in made for op in ops]
    n_in, n_out, n_scr = len(in_specs), len(out_shape), len(scratch_shapes)
    c_in = [a for op in comm for a in op.inputs]
    c_out = [s for op in comm for s in op.out_shapes]
    sems, aliases, i_off, o_off = [], {}, n_in, n_out
    for op in comm:
        sems += [pltpu.SemaphoreType.DMA((op.n_pairs,)), pltpu.SemaphoreType.DMA((op.n_pairs,)),
                 pltpu.SemaphoreType.DMA((max(op.n_local, 1),))]
        for src, dst in op.aliases.items():
            aliases[i_off + src] = o_off + dst
        i_off += len(op.inputs)
        o_off += len(op.out_shapes)

    def carrier(*refs):
        ins, cins = refs[:n_in], refs[n_in:n_in + len(c_in)]
        base = n_in + len(c_in)
        outs, couts = refs[base:base + n_out], refs[base + n_out:base + n_out + len(c_out)]
        base += n_out + len(c_out)
        scr, csems = refs[base:base + n_scr], refs[base + n_scr:]
        ids = [pl.program_id(ax) for ax in range(len(grid))]
        first = functools.reduce(jnp.logical_and, [i == 0 for i in ids])
        last = functools.reduce(jnp.logical_and, [i == g - 1 for i, g in zip(ids, grid)])

        def pieces():
            res, ci, co = [], 0, 0
            for k, op in enumerate(comm):
                res.append(op.build(cins[ci:ci + len(op.inputs)], couts[co:co + len(op.out_shapes)],
                                    *csems[3 * k:3 * k + 3]))
                ci += len(op.inputs)
                co += len(op.out_shapes)
            return res

        @pl.when(first)
        def _():
            for sends, _, local in pieces():
                for cp in local + sends:
                    cp.start()

        body(*ins, *outs, *scr)

        @pl.when(last)
        def _():
            for sends, recvs, local in pieces():
                for cp in recvs:
                    cp.wait_recv()
                for cp in sends:
                    cp.wait_send()
                for cp in local:
                    cp.wait()

    hbm = pl.BlockSpec(memory_space=pl.ANY)
    res = pl.pallas_call(
        carrier, name=name, grid=grid, in_specs=list(in_specs) + [hbm] * len(c_in),
        out_specs=list(out_specs) + [hbm] * len(c_out), out_shape=list(out_shape) + c_out,
        scratch_shapes=list(scratch_shapes) + sems, input_output_aliases=aliases,
        compiler_params=_params(("arbitrary",) * len(grid), vmem))(*operands, *c_in)
    pos = n_out
    for ops, done in made:
        results = []
        for op in ops:
            results.append(list(res[pos:pos + len(op.out_shapes)]))
            pos += len(op.out_shapes)
        done(results)
    return list(res[:n_out])


def _block(ref, px, py, pc):
    return ref.at[4 * px + 2 * py + pc]


def _remote(src, dst, send_sems, recv_sems, k, to):
    return pltpu.make_async_remote_copy(src_ref=src, dst_ref=dst, send_sem=send_sems.at[k], recv_sem=recv_sems.at[k],
                                        device_id=to, device_id_type=MESH)


def _gather_own(shards):
    n = len(shards)

    def build(ins, outs, send_sems, recv_sems, local_sems):
        x, y, c = _mesh_pos()
        peers = [(x, y, 1 - c), (1 - x, y, c), (x, 1 - y, c), (1 - x, 1 - y, c)]
        sends, recvs, local = [], [], []
        for t in range(n):
            local.append(pltpu.make_async_copy(ins[t], _block(outs[t], x, y, c), local_sems.at[t]))
            for k, peer in enumerate(peers):
                sends.append(_remote(ins[t], _block(outs[t], x, y, c), send_sems, recv_sems, 4 * t + k, peer))
                recvs.append(_remote(ins[t], _block(outs[t], *peer), send_sems, recv_sems, 4 * t + k, peer))
        return sends, recvs, local

    return Exchange(list(shards), [jax.ShapeDtypeStruct((N_DEV,) + a.shape, a.dtype) for a in shards], {}, 4 * n, n, build)


def _gather_pass(bufs):
    n = len(bufs)

    def build(ins, outs, send_sems, recv_sems, local_sems):
        x, y, c = _mesh_pos()
        chips = [(1 - x, y), (x, 1 - y), (1 - x, 1 - y)]
        sends, recvs = [], []
        for t in range(n):
            for j, chip in enumerate(chips):
                mine, theirs = _block(outs[t], *chip, c), _block(outs[t], *chip, 1 - c)
                sends.append(_remote(mine, mine, send_sems, recv_sems, 3 * t + j, (x, y, 1 - c)))
                recvs.append(_remote(mine, theirs, send_sems, recv_sems, 3 * t + j, (x, y, 1 - c)))
        return sends, recvs, []

    return Exchange(list(bufs), [jax.ShapeDtypeStruct(a.shape, a.dtype) for a in bufs], {t: t for t in range(n)},
                    3 * n, 0, build)


def _scatter_sibling(gs):
    n = len(gs)

    def build(ins, outs, send_sems, recv_sems, local_sems):
        x, y, c = _mesh_pos()
        sends, recvs = [], []
        for t in range(n):
            for k in range(4):
                cp = _remote(ins[t].at[2 * k + 1 - c], outs[t].at[k], send_sems, recv_sems, 4 * t + k, (x, y, 1 - c))
                sends.append(cp)
                recvs.append(cp)
        return sends, recvs, []

    return Exchange(list(gs), [jax.ShapeDtypeStruct((4,) + g.shape[1:], g.dtype) for g in gs], {}, 4 * n, 0, build)


def _scatter_chips(ps):
    n = len(ps)

    def build(ins, outs, send_sems, recv_sems, local_sems):
        x, y, c = _mesh_pos()
        chips = [(1 - x, y), (x, 1 - y), (1 - x, 1 - y)]
        sends, recvs = [], []
        for t in range(n):
            for j, (cx, cy) in enumerate(chips):
                cp = _remote(ins[t].at[2 * cx + cy], outs[t].at[j], send_sems, recv_sems, 3 * t + j, (cx, cy, c))
                sends.append(cp)
                recvs.append(cp)
        return sends, recvs, []

    return Exchange(list(ps), [jax.ShapeDtypeStruct((3,) + p.shape[1:], p.dtype) for p in ps], {}, 3 * n, 0, build)


_DIMS = {"nn": ((1,), (0,)), "nt": ((1,), (1,)), "tn": ((0,), (0,))}


def _mm_call(a, b, *, mode, name, grid, kaxis, nk, a_spec, b_spec, o_specs, out_shape, acc_shape,
             extras=(), e_specs=(), epilogue=None, active=None, plan=None):
    ne, no = len(extras), len(out_shape)

    def body(a_ref, b_ref, *rest):
        e_refs, o_refs, acc_ref = rest[:ne], rest[ne:ne + no], rest[ne + no]
        ids = [pl.program_id(ax) for ax in range(len(grid))]
        k = ids[kaxis]

        def finish(acc):
            outs = (acc,) if epilogue is None else epilogue(acc, *[e[...] for e in e_refs])
            for o_ref, val in zip(o_refs, outs):
                o_ref[...] = val.astype(o_ref.dtype)

        def step():
            prod = lax.dot_general(a_ref[...], b_ref[...], (_DIMS[mode], ((), ())), preferred_element_type=F32)
            if nk == 1:
                finish(prod)
                return

            @pl.when(k == 0)
            def _():
                acc_ref[...] = prod

            @pl.when(k > 0)
            def _():
                acc_ref[...] += prod

        if active is None:
            step()
        else:
            pl.when(active(*ids))(step)
        if nk > 1:
            @pl.when(k == nk - 1)
            def _():
                finish(acc_ref[...])

    sem = tuple("arbitrary" if ax == kaxis else "parallel" for ax in range(len(grid)))
    return _pcall(body, (a, b, *extras), name=name, grid=grid, in_specs=[a_spec, b_spec, *e_specs],
                  out_specs=list(o_specs), out_shape=list(out_shape), scratch_shapes=[pltpu.VMEM(acc_shape, F32)],
                  sem=sem, vmem=VMEM_BIG, plan=plan)


def _mm(a, b, *, mode, name, out_dtypes=(F32,), extras=(), epilogue=None, tm=1024, tn=1024, tk=1024,
        mnk=None, b_spec=None, o_spec=None, out_struct=None, plan=None):
    if mnk is not None:
        m, n, k = mnk
    elif mode == "nn":
        (m, k), (_, n) = a.shape, b.shape
    elif mode == "nt":
        (m, k), (n, _) = a.shape, b.shape
    else:
        (k, m), (_, n) = a.shape, b.shape
    tm, tn, tk = _tile(m, tm), _tile(n, tn), _tile(k, tk)
    if tn > 2048:
        tm, tk = _tile(m, 512), _tile(k, 512)
    if tk > 2048:
        tm, tn = _tile(m, 512), _tile(n, 512)
    nk = k // tk
    if mode == "tn":
        a_spec = pl.BlockSpec((tk, tm), lambda i, j, kk: (kk, i))
    else:
        a_spec = pl.BlockSpec((tm, tk), lambda i, j, kk: (i, kk))
    if b_spec is not None:
        b_spec = b_spec(tn, tk)
    elif mode == "nt":
        b_spec = pl.BlockSpec((tn, tk), lambda i, j, kk: (j, kk))
    else:
        b_spec = pl.BlockSpec((tk, tn), lambda i, j, kk: (kk, j))

    def e_spec(e):
        if e.shape[1] == tn and n != tn:
            return pl.BlockSpec((tm, tn), lambda i, j, kk: (i, 0))
        return pl.BlockSpec((tm, tn), lambda i, j, kk: (i, j))

    e_specs = [e_spec(e) for e in extras]
    if o_spec is not None:
        o_specs = [o_spec(tm, tn)]
        out_shape = [out_struct]
    else:
        o_specs = [pl.BlockSpec((tm, tn), lambda i, j, kk: (i, j)) for _ in out_dtypes]
        out_shape = [jax.ShapeDtypeStruct((m, n), dt) for dt in out_dtypes]
    outs = _mm_call(a, b, mode=mode, name=name, grid=(m // tm, n // tn, nk), kaxis=2, nk=nk, a_spec=a_spec,
                    b_spec=b_spec, o_specs=o_specs, out_shape=out_shape, acc_shape=(tm, tn), extras=extras,
                    e_specs=e_specs, epilogue=epilogue, plan=plan)
    return outs[0] if len(outs) == 1 else outs


def _branch_of(kb):
    return jnp.minimum(kb, N_BRANCH - 1)


def _proj_fwd(act_cat, proj_cat, name, plan=None):
    s = act_cat.shape[0]
    tm, tn = _tile(s, 1024), 1024
    nj = D_MODEL // tn

    def kb(b, k):
        return jnp.where(b < N_BRANCH - 1, b, N_BRANCH - 1 + k)

    out = _mm_call(
        act_cat, proj_cat, mode="nn", name=name, grid=(s // tm, nj, N_BRANCH, 2), kaxis=3, nk=2,
        a_spec=pl.BlockSpec((tm, BR), lambda i, j, b, k: (i, kb(b, k))),
        b_spec=pl.BlockSpec((BR, tn), lambda i, j, b, k: (kb(b, k), j)),
        o_specs=[pl.BlockSpec((tm, tn), lambda i, j, b, k: (i, b * nj + j))],
        out_shape=[jax.ShapeDtypeStruct((s, N_GATE), F32)], acc_shape=(tm, tn),
        active=lambda i, j, b, k: jnp.logical_or(b == N_BRANCH - 1, k == 0), plan=plan)
    return out[0]


def _proj_bwd_act(dy, proj_cat, name, plan=None):
    s = dy.shape[0]
    tm, tk = _tile(s, 1024), 1024
    nkk = D_MODEL // tk
    nkb = ACT_CAT // BR
    out = _mm_call(
        dy, proj_cat, mode="nt", name=name, grid=(s // tm, nkb, nkk), kaxis=2, nk=nkk,
        a_spec=pl.BlockSpec((tm, tk), lambda i, kb, k: (i, _branch_of(kb) * nkk + k)),
        b_spec=pl.BlockSpec((BR, tk), lambda i, kb, k: (kb, k)),
        o_specs=[pl.BlockSpec((tm, BR), lambda i, kb, k: (i, kb))],
        out_shape=[jax.ShapeDtypeStruct((s, ACT_CAT), F32)], acc_shape=(tm, BR), plan=plan)
    return out[0]


def _proj_bwd_w(act_cat, dy, name):
    s = dy.shape[0]
    tms, tn = _tile(s, 1024), 1024
    nj = D_MODEL // tn
    nkb = ACT_CAT // BR
    nm = s // tms
    out = _mm_call(
        act_cat, dy, mode="tn", name=name, grid=(nkb, nj, nm), kaxis=2, nk=nm,
        a_spec=pl.BlockSpec((tms, BR), lambda kb, j, m: (m, kb)),
        b_spec=pl.BlockSpec((tms, tn), lambda kb, j, m: (m, _branch_of(kb) * nj + j)),
        o_specs=[pl.BlockSpec((BR, tn), lambda kb, j, m: (kb, j))],
        out_shape=[jax.ShapeDtypeStruct((ACT_CAT, D_MODEL), BF16)], acc_shape=(BR, tn))
    return out[0]


def _row_specs(ts, n_full, n_vec):
    return ([pl.BlockSpec((ts, D_MODEL), lambda i: (i, 0))] * n_full
            + [pl.BlockSpec((1, D_MODEL), lambda i: (0, 0))] * n_vec)


def _pre_norm(x, g, name):
    s = x.shape[0]
    ts = _tile(s, 256)

    def body(x_ref, g_ref, h_ref):
        h_ref[...] = _rms(x_ref[...], g_ref[...]).astype(BF16)

    return pl.pallas_call(body, name=name, grid=(s // ts,), in_specs=_row_specs(ts, 1, 1),
                          out_specs=pl.BlockSpec((ts, D_MODEL), lambda i: (i, 0)),
                          out_shape=jax.ShapeDtypeStruct((s, D_MODEL), BF16), compiler_params=_params(("parallel",)))(x, g)


def _post_pre(x, r, g_post, g_next, name):
    s = x.shape[0]
    ts = _tile(s, 256)

    def body(x_ref, r_ref, gp_ref, gn_ref, xn_ref, h_ref):
        xn = x_ref[...] + _rms(r_ref[...], gp_ref[...])
        xn_ref[...] = xn
        h_ref[...] = _rms(xn, gn_ref[...]).astype(BF16)

    spec = pl.BlockSpec((ts, D_MODEL), lambda i: (i, 0))
    return pl.pallas_call(body, name=name, grid=(s // ts,), in_specs=_row_specs(ts, 2, 2), out_specs=[spec, spec],
                          out_shape=[jax.ShapeDtypeStruct((s, D_MODEL), F32), jax.ShapeDtypeStruct((s, D_MODEL), BF16)],
                          compiler_params=_params(("parallel",)))(x, r, g_post, g_next)


def _final_loss(x, r, g_post, target, name):
    s = x.shape[0]
    ts = _tile(s, 256)

    def body(x_ref, r_ref, gp_ref, t_ref, dy_ref, dr_ref, dg_ref, loss_ref):
        first = pl.program_id(0) == 0
        rv, gp = r_ref[...], gp_ref[...]
        diff = x_ref[...] + _rms(rv, gp) - t_ref[...]
        part = 0.5 * jnp.sum(jnp.mean(diff * diff, axis=-1, keepdims=True), axis=0, keepdims=True)
        dy = diff * (1.0 / D_MODEL)
        dy_ref[...] = dy
        dr, dg_rows = _rms_bwd(rv, gp, dy)
        dr_ref[...] = dr.astype(BF16)
        _colsum_into(dg_ref, dg_rows, first)
        _colsum_into(loss_ref, jnp.broadcast_to(part, (1, 128)), first)

    spec = pl.BlockSpec((ts, D_MODEL), lambda i: (i, 0))
    vec = pl.BlockSpec((1, D_MODEL), lambda i: (0, 0))
    return pl.pallas_call(
        body, name=name, grid=(s // ts,), in_specs=[spec, spec, vec, spec],
        out_specs=[spec, spec, vec, pl.BlockSpec((1, 128), lambda i: (0, 0))],
        out_shape=[jax.ShapeDtypeStruct((s, D_MODEL), F32), jax.ShapeDtypeStruct((s, D_MODEL), BF16),
                   jax.ShapeDtypeStruct((1, D_MODEL), F32), jax.ShapeDtypeStruct((1, 128), F32)],
        compiler_params=_params(("arbitrary",)))(x, r, g_post, target)


def _pre_bwd(dh, x, g_pre, dx_res, name, r_prev=None, g_post_prev=None):
    s = x.shape[0]
    ts = _tile(s, 256)
    chain = r_prev is not None

    def body(*refs):
        if chain:
            dh_ref, x_ref, res_ref, r_ref, g_ref, gp_ref, dx_ref, dr_ref, dg_ref, dgp_ref = refs
        else:
            dh_ref, x_ref, res_ref, g_ref, dx_ref, dg_ref = refs
        first = pl.program_id(0) == 0
        dxn, dg_rows = _rms_bwd(x_ref[...], g_ref[...], dh_ref[...])
        dx = res_ref[...] + dxn
        dx_ref[...] = dx
        _colsum_into(dg_ref, dg_rows, first)
        if chain:
            dr, dgp_rows = _rms_bwd(r_ref[...], gp_ref[...], dx)
            dr_ref[...] = dr.astype(BF16)
            _colsum_into(dgp_ref, dgp_rows, first)

    spec = pl.BlockSpec((ts, D_MODEL), lambda i: (i, 0))
    vec = pl.BlockSpec((1, D_MODEL), lambda i: (0, 0))
    full = jax.ShapeDtypeStruct((s, D_MODEL), F32)
    vshape = jax.ShapeDtypeStruct((1, D_MODEL), F32)
    if chain:
        return pl.pallas_call(
            body, name=name, grid=(s // ts,), in_specs=[spec] * 4 + [vec] * 2, out_specs=[spec, spec, vec, vec],
            out_shape=[full, jax.ShapeDtypeStruct((s, D_MODEL), BF16), vshape, vshape],
            compiler_params=_params(("arbitrary",)))(dh, x, dx_res, r_prev, g_pre, g_post_prev)
    return pl.pallas_call(
        body, name=name, grid=(s // ts,), in_specs=[spec] * 3 + [vec], out_specs=[spec, vec],
        out_shape=[full, vshape], compiler_params=_params(("arbitrary",)))(dh, x, dx_res, g_pre)


def _merge_fwd(y, zg, name):
    s = y.shape[0]
    ts, tc = _tile(s, 512), 512
    nj = D_MODEL // tc

    def body(y_ref, z_ref, o_ref, acc_ref):
        b = pl.program_id(2)
        val = _sigmoid(z_ref[...]) * y_ref[...]

        @pl.when(b == 0)
        def _():
            acc_ref[...] = val

        @pl.when(b > 0)
        def _():
            acc_ref[...] += val

        @pl.when(b == N_BRANCH - 1)
        def _():
            o_ref[...] = acc_ref[...].astype(BF16)

    blk = pl.BlockSpec((ts, tc), lambda i, j, b: (i, b * nj + j))
    return pl.pallas_call(
        body, name=name, grid=(s // ts, nj, N_BRANCH), in_specs=[blk, blk],
        out_specs=pl.BlockSpec((ts, tc), lambda i, j, b: (i, j)), out_shape=jax.ShapeDtypeStruct((s, D_MODEL), BF16),
        scratch_shapes=[pltpu.VMEM((ts, tc), F32)],
        compiler_params=_params(("parallel", "parallel", "arbitrary")))(y, zg)


def _merge_bwd(dm, y, zg, name, plan=None):
    s = y.shape[0]
    ts, tc = _tile(s, 512), 512
    nj = D_MODEL // tc

    def body(dm_ref, y_ref, z_ref, dy_ref, dz_ref):
        g = _sigmoid(z_ref[...])
        d = dm_ref[...]
        dy_ref[...] = (d * g).astype(BF16)
        dz_ref[...] = (d * y_ref[...] * g * (1.0 - g)).astype(BF16)

    blk = pl.BlockSpec((ts, tc), lambda i, j, b: (i, b * nj + j))
    shape = jax.ShapeDtypeStruct((s, N_GATE), BF16)
    return _pcall(body, (dm, y, zg), name=name, grid=(s // ts, nj, N_BRANCH),
                  in_specs=[pl.BlockSpec((ts, tc), lambda i, j, b: (i, j)), blk, blk], out_specs=[blk, blk],
                  out_shape=[shape, shape], sem=("parallel", "parallel", "parallel"), plan=plan)


POOL_HALO = 16


def _pool_windows(ext_ref, ts, first_row):
    outs = []
    t = first_row + lax.broadcasted_iota(jnp.int32, (ts, GDIM), 0)
    for gi, w in enumerate(POOL_WINDOWS):
        cols = pl.ds(gi * GDIM, GDIM)
        acc = ext_ref[pl.ds(POOL_HALO, ts), cols]
        cur = acc
        for k in range(1, w):
            acc = acc + ext_ref[pl.ds(POOL_HALO - k, ts), cols]
        cnt = jnp.minimum(t + 1, w).astype(F32)
        outs.append(acc / cnt - cur)
    return outs


def _pool_fwd(za, pool_w, pool_scale, name):
    s = za.shape[0]
    ts = _tile(s, 512)
    hb = ts // POOL_HALO

    def body(a_ref, halo_ref, w_ref, sc_ref, o_ref, ext_ref):
        i = pl.program_id(0)
        ext_ref[pl.ds(0, POOL_HALO), :] = jnp.where(i > 0, halo_ref[...], 0.0)
        ext_ref[pl.ds(POOL_HALO, ts), :] = a_ref[...]
        pooled = _pool_windows(ext_ref, ts, i * ts)
        for gi in range(len(POOL_WINDOWS)):
            mixed = jnp.dot(pooled[gi].astype(BF16), w_ref[gi].astype(BF16), preferred_element_type=F32)
            o_ref[:, pl.ds(gi * GDIM, GDIM)] = (mixed * sc_ref[:, pl.ds(gi * GDIM, GDIM)]).astype(BF16)

    return pl.pallas_call(
        body, name=name, grid=(s // ts,),
        in_specs=[pl.BlockSpec((ts, BR), lambda i: (i, 0)),
                  pl.BlockSpec((POOL_HALO, BR), lambda i: (jnp.maximum(i * hb - 1, 0), 0)),
                  pl.BlockSpec((4, GDIM, GDIM), lambda i: (0, 0, 0)), pl.BlockSpec((1, BR), lambda i: (0, 0))],
        out_specs=pl.BlockSpec((ts, BR), lambda i: (i, 0)), out_shape=jax.ShapeDtypeStruct((s, BR), BF16),
        scratch_shapes=[pltpu.VMEM((POOL_HALO + ts, BR), F32)], compiler_params=_params(("parallel",)))(
            za, za, pool_w, pool_scale)


def _pool_bwd(dact, za, pool_w, pool_scale, name):
    s = za.shape[0]
    ts = _tile(s, 512)
    hb = ts // POOL_HALO
    n_t = s // ts

    def body(d_ref, dhalo_ref, a_ref, halo_ref, w_ref, sc_ref, dz_ref, dw_ref, dsc_ref, ext_ref, f_ref):
        i = pl.program_id(0)
        first = i == 0
        ext_ref[pl.ds(0, POOL_HALO), :] = jnp.where(i > 0, halo_ref[...], 0.0)
        ext_ref[pl.ds(POOL_HALO, ts), :] = a_ref[...]
        pooled = _pool_windows(ext_ref, ts, i * ts)
        d_tile = d_ref[...]
        d_next = jnp.where(i < n_t - 1, dhalo_ref[...], 0.0)
        t_ext = i * ts + lax.broadcasted_iota(jnp.int32, (ts + POOL_HALO, GDIM), 0)
        dsc = []
        for gi, w in enumerate(POOL_WINDOWS):
            cols = pl.ds(gi * GDIM, GDIM)
            wg = w_ref[gi].astype(BF16)
            sc = sc_ref[:, cols]
            pg = pooled[gi].astype(BF16)
            mixed = jnp.dot(pg, wg, preferred_element_type=F32)
            dsc.append(jnp.sum(d_tile[:, gi * GDIM:(gi + 1) * GDIM] * mixed, axis=0, keepdims=True))
            dmix = jnp.concatenate([d_tile[:, gi * GDIM:(gi + 1) * GDIM], d_next[:, gi * GDIM:(gi + 1) * GDIM]], axis=0) * sc
            dmix = dmix.astype(BF16)
            dwg = lax.dot_general(pg, dmix[:ts], (((0,), (0,)), ((), ())), preferred_element_type=F32)

            @pl.when(first)
            def _():
                dw_ref[gi] = dwg

            @pl.when(jnp.logical_not(first))
            def _():
                dw_ref[gi] += dwg

            dpool = lax.dot_general(dmix, wg, (((1,), (1,)), ((), ())), preferred_element_type=F32)
            f_ref[:, cols] = dpool / jnp.minimum(t_ext + 1, w).astype(F32)
            acc = f_ref[pl.ds(0, ts), cols]
            for k in range(1, w):
                acc = acc + f_ref[pl.ds(k, ts), cols]
            dz_ref[:, cols] = (acc - dpool[:ts]).astype(BF16)
        dsc_all = jnp.concatenate(dsc, axis=1)

        @pl.when(first)
        def _():
            dsc_ref[...] = dsc_all

        @pl.when(jnp.logical_not(first))
        def _():
            dsc_ref[...] += dsc_all

    n_hb = s // POOL_HALO
    return pl.pallas_call(
        body, name=name, grid=(n_t,),
        in_specs=[pl.BlockSpec((ts, BR), lambda i: (i, 0)),
                  pl.BlockSpec((POOL_HALO, BR), lambda i: (jnp.minimum((i + 1) * hb, n_hb - 1), 0)),
                  pl.BlockSpec((ts, BR), lambda i: (i, 0)),
                  pl.BlockSpec((POOL_HALO, BR), lambda i: (jnp.maximum(i * hb - 1, 0), 0)),
                  pl.BlockSpec((4, GDIM, GDIM), lambda i: (0, 0, 0)), pl.BlockSpec((1, BR), lambda i: (0, 0))],
        out_specs=[pl.BlockSpec((ts, BR), lambda i: (i, 0)), pl.BlockSpec((4, GDIM, GDIM), lambda i: (0, 0, 0)),
                   pl.BlockSpec((1, BR), lambda i: (0, 0))],
        out_shape=[jax.ShapeDtypeStruct((s, BR), BF16), jax.ShapeDtypeStruct((4, GDIM, GDIM), F32),
                   jax.ShapeDtypeStruct((1, BR), F32)],
        scratch_shapes=[pltpu.VMEM((POOL_HALO + ts, BR), F32), pltpu.VMEM((ts + POOL_HALO, BR), F32)],
        compiler_params=_params(("arbitrary",)))(dact, dact, za, za, pool_w, pool_scale)


CONV_HALO = 32
CONV_LEAD = CONV_HALO - (CONV_WIDTH - 1)


def _conv_fwd(za, conv_w, conv_b, ng, nb, name):
    s = za.shape[0]
    ts = _tile(s, 512)
    hb = ts // CONV_HALO

    def body(a_ref, g_ref, ah_ref, gh_ref, w_ref, b_ref, ng_ref, nb_ref, yc_ref, act_ref, ext_ref):
        i = pl.program_id(0)
        ext_ref[pl.ds(0, CONV_HALO), :] = jnp.where(i > 0, ah_ref[...] * _sigmoid(gh_ref[...]), 0.0)
        ext_ref[pl.ds(CONV_HALO, ts), :] = a_ref[...] * _sigmoid(g_ref[...])
        acc = jnp.zeros((ts, BR), F32) + b_ref[...]
        for k in range(CONV_WIDTH):
            acc = acc + w_ref[pl.ds(k, 1), :] * ext_ref[pl.ds(CONV_LEAD + k, ts), :]
        yc_ref[...] = acc
        xh, _ = _ln_stats(acc)
        ln = xh * ng_ref[...] + nb_ref[...]
        act_ref[...] = (ln * _sigmoid(ln)).astype(BF16)

    tile = lambda c: pl.BlockSpec((ts, BR), lambda i: (i, c))
    halo = lambda c: pl.BlockSpec((CONV_HALO, BR), lambda i: (jnp.maximum(i * hb - 1, 0), c))
    vec = pl.BlockSpec((1, BR), lambda i: (0, 0))
    return pl.pallas_call(
        body, name=name, grid=(s // ts,),
        in_specs=[tile(1), tile(2), halo(1), halo(2), pl.BlockSpec((CONV_WIDTH, BR), lambda i: (0, 0)), vec, vec, vec],
        out_specs=[pl.BlockSpec((ts, BR), lambda i: (i, 0))] * 2,
        out_shape=[jax.ShapeDtypeStruct((s, BR), F32), jax.ShapeDtypeStruct((s, BR), BF16)],
        scratch_shapes=[pltpu.VMEM((CONV_HALO + ts, BR), F32)], compiler_params=_params(("parallel",)))(
            za, za, za, za, conv_w, conv_b, ng, nb)


def _conv_bwd_norm(dact, yc, ng, nb, name):
    s = yc.shape[0]
    ts = _tile(s, 512)

    def body(d_ref, y_ref, ng_ref, nb_ref, dy_ref, db_ref, dng_ref, dnb_ref):
        first = pl.program_id(0) == 0
        xh, r = _ln_stats(y_ref[...])
        g = ng_ref[...]
        ln = xh * g + nb_ref[...]
        sg = _sigmoid(ln)
        dln = d_ref[...] * sg * (1.0 + ln * (1.0 - sg))
        dy = _ln_bwd(xh, r, g, dln)
        dy_ref[...] = dy
        _colsum_into(db_ref, dy, first)
        _colsum_into(dng_ref, dln * xh, first)
        _colsum_into(dnb_ref, dln, first)

    vec = pl.BlockSpec((1, BR), lambda i: (0, 0))
    vshape = jax.ShapeDtypeStruct((1, BR), F32)
    return pl.pallas_call(
        body, name=name, grid=(s // ts,),
        in_specs=[pl.BlockSpec((ts, BR), lambda i: (i, 1)), pl.BlockSpec((ts, BR), lambda i: (i, 0)), vec, vec],
        out_specs=[pl.BlockSpec((ts, BR), lambda i: (i, 0)), vec, vec, vec],
        out_shape=[jax.ShapeDtypeStruct((s, BR), F32), vshape, vshape, vshape],
        compiler_params=_params(("arbitrary",)))(dact, yc, ng, nb)


def _conv_bwd_taps(dyc, za, conv_w, name):
    s = za.shape[0]
    ts = _tile(s, 512)
    hb = ts // CONV_HALO
    n_t = s // ts
    n_hb = s // CONV_HALO

    def body(d_ref, dh_ref, a_ref, g_ref, ah_ref, gh_ref, w_ref, dz_ref, dw_ref, ext_ref, f_ref):
        i = pl.program_id(0)
        first = i == 0
        a, sg = a_ref[...], _sigmoid(g_ref[...])
        ext_ref[pl.ds(0, CONV_HALO), :] = jnp.where(i > 0, ah_ref[...] * _sigmoid(gh_ref[...]), 0.0)
        ext_ref[pl.ds(CONV_HALO, ts), :] = a * sg
        d = d_ref[...]
        f_ref[pl.ds(0, ts), :] = d
        f_ref[pl.ds(ts, CONV_HALO), :] = jnp.where(i < n_t - 1, dh_ref[...], 0.0)
        dglu = jnp.zeros((ts, BR), F32)
        rows = []
        for k in range(CONV_WIDTH):
            rows.append(jnp.sum(d * ext_ref[pl.ds(CONV_LEAD + k, ts), :], axis=0, keepdims=True))
            dglu = dglu + w_ref[pl.ds(k, 1), :] * f_ref[pl.ds(CONV_WIDTH - 1 - k, ts), :]
        rows.append(jnp.zeros((1, BR), F32))
        dw = jnp.concatenate(rows, axis=0)

        @pl.when(first)
        def _():
            dw_ref[...] = dw

        @pl.when(jnp.logical_not(first))
        def _():
            dw_ref[...] += dw

        dz_ref[:, pl.ds(0, BR)] = (dglu * sg).astype(BF16)
        dz_ref[:, pl.ds(BR, BR)] = (dglu * a * sg * (1.0 - sg)).astype(BF16)

    tile = lambda c: pl.BlockSpec((ts, BR), lambda i: (i, c))
    halo = lambda c: pl.BlockSpec((CONV_HALO, BR), lambda i: (jnp.maximum(i * hb - 1, 0), c))
    return pl.pallas_call(
        body, name=name, grid=(n_t,),
        in_specs=[pl.BlockSpec((ts, BR), lambda i: (i, 0)),
                  pl.BlockSpec((CONV_HALO, BR), lambda i: (jnp.minimum((i + 1) * hb, n_hb - 1), 0)),
                  tile(1), tile(2), halo(1), halo(2), pl.BlockSpec((CONV_WIDTH, BR), lambda i: (0, 0))],
        out_specs=[pl.BlockSpec((ts, 2 * BR), lambda i: (i, 0)), pl.BlockSpec((CONV_WIDTH + 1, BR), lambda i: (0, 0))],
        out_shape=[jax.ShapeDtypeStruct((s, 2 * BR), BF16), jax.ShapeDtypeStruct((CONV_WIDTH + 1, BR), F32)],
        scratch_shapes=[pltpu.VMEM((CONV_HALO + ts, BR), F32), pltpu.VMEM((ts + CONV_HALO, BR), F32)],
        compiler_params=_params(("arbitrary",)))(dyc, dyc, za, za, za, za, conv_w)


def _tril(w):
    r = lax.broadcasted_iota(jnp.int32, (CHUNK, CHUNK), 0)
    c = lax.broadcasted_iota(jnp.int32, (CHUNK, CHUNK), 1)
    return jnp.where(c <= r, w, 0.0)


def _sgu_fwd(za, ng, nb, sgu_w, bias_b, name):
    s = za.shape[0]
    ts = _tile(s, 512)

    def body(u_ref, v_ref, ng_ref, nb_ref, w_ref, b_ref, o_ref):
        u = _gelu(u_ref[...])
        xh, _ = _ln_stats(_gelu(v_ref[...]))
        vln = (xh * ng_ref[...] + nb_ref[...]).astype(BF16)
        for gi in range(4):
            wg = _tril(w_ref[gi]).astype(BF16)
            for n in range(ts // CHUNK):
                blk = vln[n * CHUNK:(n + 1) * CHUNK, gi * GDIM:(gi + 1) * GDIM]
                sp = jnp.dot(wg, blk, preferred_element_type=F32) + b_ref[gi]
                o_ref[pl.ds(n * CHUNK, CHUNK), pl.ds(gi * GDIM, GDIM)] = (
                    u[n * CHUNK:(n + 1) * CHUNK, gi * GDIM:(gi + 1) * GDIM] * sp).astype(BF16)

    vec = pl.BlockSpec((1, BR), lambda i: (0, 0))
    cube = pl.BlockSpec((4, CHUNK, GDIM), lambda i: (0, 0, 0))
    return pl.pallas_call(
        body, name=name, grid=(s // ts,),
        in_specs=[pl.BlockSpec((ts, BR), lambda i: (i, 3)), pl.BlockSpec((ts, BR), lambda i: (i, 4)), vec, vec, cube, cube],
        out_specs=pl.BlockSpec((ts, BR), lambda i: (i, 0)), out_shape=jax.ShapeDtypeStruct((s, BR), BF16),
        compiler_params=_params(("parallel",)))(za, za, ng, nb, sgu_w, bias_b)


def _sgu_bwd(dact, za, ng, nb, sgu_w, bias_b, name):
    s = za.shape[0]
    ts = _tile(s, 512)

    def body(d_ref, u_ref, v_ref, ng_ref, nb_ref, w_ref, b_ref, dz_ref, dw_ref, db_ref, dng_ref, dnb_ref, dv_ref):
        first = pl.program_id(0) == 0
        u_raw, v_raw = u_ref[...], v_ref[...]
        u = _gelu(u_raw)
        xh, r = _ln_stats(_gelu(v_raw))
        g = ng_ref[...]
        vln = (xh * g + nb_ref[...]).astype(BF16)
        d = d_ref[...]
        dsp = d * u
        dsp16 = dsp.astype(BF16)
        for gi in range(4):
            wg = _tril(w_ref[gi]).astype(BF16)
            dwg = jnp.zeros((CHUNK, CHUNK), F32)
            dbg = jnp.zeros((CHUNK, 1), F32)
            for n in range(ts // CHUNK):
                rows, cols = slice(n * CHUNK, (n + 1) * CHUNK), slice(gi * GDIM, (gi + 1) * GDIM)
                blk = vln[rows, cols]
                sp = jnp.dot(wg, blk, preferred_element_type=F32) + b_ref[gi]
                dz_ref[pl.ds(n * CHUNK, CHUNK), pl.ds(gi * GDIM, GDIM)] = (
                    d[rows, cols] * sp * _gelu_grad(u_raw[rows, cols])).astype(BF16)
                dv_ref[pl.ds(n * CHUNK, CHUNK), pl.ds(gi * GDIM, GDIM)] = lax.dot_general(
                    wg, dsp16[rows, cols], (((0,), (0,)), ((), ())), preferred_element_type=F32)
                dwg = dwg + lax.dot_general(dsp16[rows, cols], blk, (((1,), (1,)), ((), ())), preferred_element_type=F32)
                dbg = dbg + jnp.sum(dsp[rows, cols], axis=1, keepdims=True)
            dwg = _tril(dwg)

            @pl.when(first)
            def _():
                dw_ref[gi] = dwg
                db_ref[gi] = dbg

            @pl.when(jnp.logical_not(first))
            def _():
                dw_ref[gi] += dwg
                db_ref[gi] += dbg

        dvln = dv_ref[...]
        dz_ref[:, pl.ds(BR, BR)] = (_ln_bwd(xh, r, g, dvln) * _gelu_grad(v_raw)).astype(BF16)
        _colsum_into(dng_ref, dvln * xh, first)
        _colsum_into(dnb_ref, dvln, first)

    vec = pl.BlockSpec((1, BR), lambda i: (0, 0))
    cube = pl.BlockSpec((4, CHUNK, GDIM), lambda i: (0, 0, 0))
    vshape = jax.ShapeDtypeStruct((1, BR), F32)
    return pl.pallas_call(
        body, name=name, grid=(s // ts,),
        in_specs=[pl.BlockSpec((ts, BR), lambda i: (i, 2)), pl.BlockSpec((ts, BR), lambda i: (i, 3)),
                  pl.BlockSpec((ts, BR), lambda i: (i, 4)), vec, vec, cube, cube],
        out_specs=[pl.BlockSpec((ts, 2 * BR), lambda i: (i, 0)), cube, pl.BlockSpec((4, CHUNK, 1), lambda i: (0, 0, 0)),
                   vec, vec],
        out_shape=[jax.ShapeDtypeStruct((s, 2 * BR), BF16), jax.ShapeDtypeStruct((4, CHUNK, CHUNK), F32),
                   jax.ShapeDtypeStruct((4, CHUNK, 1), F32), vshape, vshape],
        scratch_shapes=[pltpu.VMEM((ts, BR), F32)], compiler_params=_params(("arbitrary",)))(
            dact, za, za, ng, nb, sgu_w, bias_b)


KR_BLOCK = 3584 // 128


def _mla_prep(za, qg, kvg, ck, sk, name):
    s = za.shape[0]
    ts = _tile(s, 512)

    def body(cq_ref, ckv_ref, kr_ref, qg_ref, kvg_ref, c_ref, s_ref, qn_ref, kvn_ref, krr_ref):
        qn_ref[...] = _rms(cq_ref[...], qg_ref[...]).astype(BF16)
        kvn_ref[...] = _rms(ckv_ref[...], kvg_ref[...]).astype(BF16)
        kr = kr_ref[...]
        krr_ref[...] = (kr * c_ref[...] + _rot_half(kr, 128, 0) * s_ref[...]).astype(BF16)

    vec = pl.BlockSpec((1, BR), lambda i: (0, 0))
    tab = pl.BlockSpec((ts, 128), lambda i: (i, 0))
    row = pl.BlockSpec((ts, BR), lambda i: (i, 0))
    return pl.pallas_call(
        body, name=name, grid=(s // ts,),
        in_specs=[pl.BlockSpec((ts, BR), lambda i: (i, 5)), pl.BlockSpec((ts, BR), lambda i: (i, 6)),
                  pl.BlockSpec((ts, 128), lambda i: (i, KR_BLOCK)), vec, vec, tab, tab],
        out_specs=[row, row, tab],
        out_shape=[jax.ShapeDtypeStruct((s, BR), BF16), jax.ShapeDtypeStruct((s, BR), BF16),
                   jax.ShapeDtypeStruct((s, 128), BF16)],
        compiler_params=_params(("parallel",)))(za, za, za, qg, kvg, ck, sk)


def _mla_prep_bwd(dqn, dkvn, dkr_heads, za, qg, kvg, ck, sk, name):
    s = za.shape[0]
    ts = _tile(s, 512)

    def body(dq_ref, dkv_ref, dkr_ref, cq_ref, ckv_ref, qg_ref, kvg_ref, c_ref, s_ref, dz_ref, dqg_ref, dkvg_ref):
        first = pl.program_id(0) == 0
        dcq, rows_q = _rms_bwd(cq_ref[...], qg_ref[...], dq_ref[...])
        dckv, rows_kv = _rms_bwd(ckv_ref[...], kvg_ref[...], dkv_ref[...])
        dz_ref[:, pl.ds(0, BR)] = dcq.astype(BF16)
        dz_ref[:, pl.ds(BR, BR)] = dckv.astype(BF16)
        dk = dkr_ref[:, pl.ds(0, 128)]
        for h in range(1, MLA_HEADS):
            dk = dk + dkr_ref[:, pl.ds(h * 128, 128)]
        dz_ref[:, pl.ds(2 * BR, 128)] = (dk * c_ref[...] - _rot_half(dk, 128, 0) * s_ref[...]).astype(BF16)
        _colsum_into(dqg_ref, rows_q, first)
        _colsum_into(dkvg_ref, rows_kv, first)

    vec = pl.BlockSpec((1, BR), lambda i: (0, 0))
    tab = pl.BlockSpec((ts, 128), lambda i: (i, 0))
    row = pl.BlockSpec((ts, BR), lambda i: (i, 0))
    wide = 2 * BR + 128
    vshape = jax.ShapeDtypeStruct((1, BR), F32)
    return pl.pallas_call(
        body, name=name, grid=(s // ts,),
        in_specs=[row, row, pl.BlockSpec((ts, MLA_HEADS * 128), lambda i: (i, 0)),
                  pl.BlockSpec((ts, BR), lambda i: (i, 5)), pl.BlockSpec((ts, BR), lambda i: (i, 6)), vec, vec, tab, tab],
        out_specs=[pl.BlockSpec((ts, wide), lambda i: (i, 0)), vec, vec],
        out_shape=[jax.ShapeDtypeStruct((s, wide), BF16), vshape, vshape],
        compiler_params=_params(("arbitrary",)))(dqn, dkvn, dkr_heads, za, za, qg, kvg, ck, sk)


def _attn_tiles(s):
    tq, tk = _tile(s, 1024), _tile(s, 512)
    return tq, tk, tq // tk


def _causal(qi, ki, tq, tk):
    row = qi * tq + lax.broadcasted_iota(jnp.int32, (tq, tk), 0)
    col = ki * tk + lax.broadcasted_iota(jnp.int32, (tq, tk), 1)
    return col <= row


def _on_tiles(qi, ki, r, step):
    pl.when(ki < qi * r)(functools.partial(step, False))
    pl.when(jnp.logical_and(ki >= qi * r, ki < (qi + 1) * r))(functools.partial(step, True))


def _flash_fwd(q, kv, krr, name, plan=None):
    s = q.shape[0]
    tq, tk, r = _attn_tiles(s)
    nq, nk = s // tq, s // tk

    def body(q_ref, kn_ref, kr_ref, v_ref, o_ref, lse_ref, m_sc, l_sc, acc_sc):
        qi, ki = pl.program_id(1), pl.program_id(2)

        @pl.when(ki == 0)
        def _():
            m_sc[...] = jnp.full((tq, 1), NEG, F32)
            l_sc[...] = jnp.zeros((tq, 1), F32)
            acc_sc[...] = jnp.zeros((tq, V_DIM), F32)

        def step(masked):
            k = jnp.concatenate([kn_ref[...], kr_ref[...]], axis=1)
            sc = lax.dot_general(q_ref[...], k, (((1,), (1,)), ((), ())), preferred_element_type=F32)
            if masked:
                sc = jnp.where(_causal(qi, ki, tq, tk), sc, NEG)
            m_prev = m_sc[...]
            m_new = jnp.maximum(m_prev, jnp.max(sc, axis=1, keepdims=True))
            alpha = jnp.exp(m_prev - m_new)
            p = jnp.exp(sc - m_new)
            l_sc[...] = alpha * l_sc[...] + jnp.sum(p, axis=1, keepdims=True)
            acc_sc[...] = alpha * acc_sc[...] + jnp.dot(p.astype(BF16), v_ref[...], preferred_element_type=F32)
            m_sc[...] = m_new

        _on_tiles(qi, ki, r, step)

        @pl.when(ki == nk - 1)
        def _():
            o_ref[...] = (acc_sc[...] / l_sc[...]).astype(BF16)
            lse_ref[...] = jnp.broadcast_to(m_sc[...] + jnp.log(l_sc[...]), (tq, 128))

    kmap = lambda c: (lambda h, qi, ki: (jnp.minimum(ki, (qi + 1) * r - 1), c(h)))
    out_blk = pl.BlockSpec((tq, 128), lambda h, qi, ki: (qi, h))
    return _pcall(
        body, (q, kv, krr, kv), name=name, grid=(MLA_HEADS, nq, nk),
        in_specs=[pl.BlockSpec((tq, QW), lambda h, qi, ki: (qi, h)), pl.BlockSpec((tk, 128), kmap(lambda h: 2 * h)),
                  pl.BlockSpec((tk, 128), kmap(lambda h: 0)), pl.BlockSpec((tk, 128), kmap(lambda h: 2 * h + 1))],
        out_specs=[out_blk, out_blk],
        out_shape=[jax.ShapeDtypeStruct((s, MLA_HEADS * V_DIM), BF16), jax.ShapeDtypeStruct((s, MLA_HEADS * 128), F32)],
        scratch_shapes=[pltpu.VMEM((tq, 1), F32), pltpu.VMEM((tq, 1), F32), pltpu.VMEM((tq, V_DIM), F32)],
        sem=("parallel", "parallel", "arbitrary"), vmem=VMEM_BIG, plan=plan)


DO_BLOCK = 3 * BR // 128


def _flash_probs(q_ref, kn_ref, kr_ref, v_ref, do_ref, o_ref, lse_ref, qi, ki, tq, tk, masked):
    k = jnp.concatenate([kn_ref[...], kr_ref[...]], axis=1)
    q = q_ref[...]
    sc = lax.dot_general(q, k, (((1,), (1,)), ((), ())), preferred_element_type=F32)
    lse = jnp.max(lse_ref[...], axis=1, keepdims=True)
    p = jnp.exp(sc - lse)
    if masked:
        p = jnp.where(_causal(qi, ki, tq, tk), p, 0.0)
    do = do_ref[...]
    delta = jnp.sum(do * o_ref[...].astype(F32), axis=1, keepdims=True)
    do = do.astype(BF16)
    dp = lax.dot_general(do, v_ref[...], (((1,), (1,)), ((), ())), preferred_element_type=F32)
    ds = (p * (dp - delta)).astype(BF16)
    return q, k, p, do, ds


def _flash_bwd(q, kv, krr, dact, o, lse, cq, sq, name, plan=None):
    s = q.shape[0]
    tq, tk, r = _attn_tiles(s)
    nq, nk = s // tq, s // tk

    def body(q_ref, kn_ref, kr_ref, v_ref, do_ref, o_ref, lse_ref, c_ref, s_ref, dkv_ref, dkr_ref, dq_ref,
             dk_sc, dv_sc, dq_sc):
        ki, qi = pl.program_id(1), pl.program_id(2)
        rows = pl.ds(pl.multiple_of(qi * tq, tq), tq)

        @pl.when(qi == 0)
        def _():
            dk_sc[...] = jnp.zeros((tk, QW), F32)
            dv_sc[...] = jnp.zeros((tk, V_DIM), F32)

        @pl.when(ki == 0)
        def _():
            dq_sc[rows, :] = jnp.zeros((tq, QW), F32)

        def step(masked):
            qv, k, p, dov, ds = _flash_probs(q_ref, kn_ref, kr_ref, v_ref, do_ref, o_ref, lse_ref, qi, ki, tq, tk, masked)
            dv_sc[...] += lax.dot_general(p.astype(BF16), dov, (((0,), (0,)), ((), ())), preferred_element_type=F32)
            dk_sc[...] += lax.dot_general(ds, qv, (((0,), (0,)), ((), ())), preferred_element_type=F32)
            dq_sc[rows, :] += jnp.dot(ds, k, preferred_element_type=F32)

        _on_tiles(qi, ki, r, step)

        @pl.when(qi == nq - 1)
        def _():
            dkv_ref[:, pl.ds(0, 128)] = dk_sc[:, pl.ds(0, 128)].astype(BF16)
            dkv_ref[:, pl.ds(128, 128)] = dv_sc[...].astype(BF16)
            dkr_ref[...] = dk_sc[:, pl.ds(128, 128)]

        @pl.when(ki == nk - 1)
        def _():
            dq = dq_sc[rows, :] * ATT_SCALE
            dq_ref[...] = (dq * c_ref[...] - _rot_half(dq, QW, QK_NOPE) * s_ref[...]).astype(BF16)

    qmap = lambda c: (lambda h, ki, qi: (jnp.maximum(qi, ki // r), c(h)))
    kmap = lambda c: (lambda h, ki, qi: (ki, c(h)))
    last = lambda c: (lambda h, ki, qi: (jnp.where(ki == nk - 1, qi, 0), c(h)))
    return _pcall(
        body, (q, kv, krr, kv, dact, o, lse, cq, sq), name=name, grid=(MLA_HEADS, nk, nq),
        in_specs=[pl.BlockSpec((tq, QW), qmap(lambda h: h)), pl.BlockSpec((tk, 128), kmap(lambda h: 2 * h)),
                  pl.BlockSpec((tk, 128), kmap(lambda h: 0)), pl.BlockSpec((tk, 128), kmap(lambda h: 2 * h + 1)),
                  pl.BlockSpec((tq, 128), qmap(lambda h: DO_BLOCK + h)), pl.BlockSpec((tq, 128), qmap(lambda h: h)),
                  pl.BlockSpec((tq, 128), qmap(lambda h: h)), pl.BlockSpec((tq, QW), last(lambda h: 0)),
                  pl.BlockSpec((tq, QW), last(lambda h: 0))],
        out_specs=[pl.BlockSpec((tk, QW), kmap(lambda h: h)), pl.BlockSpec((tk, 128), kmap(lambda h: h)),
                   pl.BlockSpec((tq, QW), last(lambda h: h))],
        out_shape=[jax.ShapeDtypeStruct((s, MLA_HEADS * QW), BF16), jax.ShapeDtypeStruct((s, MLA_HEADS * 128), F32),
                   jax.ShapeDtypeStruct((s, MLA_HEADS * QW), BF16)],
        scratch_shapes=[pltpu.VMEM((tk, QW), F32), pltpu.VMEM((tk, V_DIM), F32), pltpu.VMEM((s, QW), F32)],
        sem=("parallel", "arbitrary", "arbitrary"), vmem=VMEM_BIG, plan=plan)


def _rope_tables(positions):
    inv_freq = ROPE_THETA ** (-jnp.arange(0, QK_ROPE, 2, dtype=F32) / QK_ROPE)
    ang = positions.reshape(-1).astype(F32)[:, None] * inv_freq
    cos, sin = jnp.cos(ang), jnp.sin(ang)
    s = cos.shape[0]
    one, zero = jnp.ones((s, 64), F32), jnp.zeros((s, 64), F32)
    ck = jnp.concatenate([cos, cos, one], axis=1)
    sk = jnp.concatenate([sin, sin, zero], axis=1)
    cq = jnp.concatenate([one, one, ck], axis=1)
    sq = jnp.concatenate([zero, zero, sk], axis=1)
    return ck, sk, cq, sq


def _cols_full(gathered):
    _, r, c = gathered.shape
    return gathered.transpose(1, 0, 2).reshape(r, N_DEV * c)


def _cols_by_owner(full):
    r, n = full.shape
    return full.reshape(r, N_DEV, n // N_DEV).transpose(1, 0, 2)


def _layer_weights(gat, small, l):
    w_in = _cols_full(gat[("w_in", l)])
    w = {
        "gat": gat, "layer": l,
        "w_a": jnp.concatenate([w_in[:, :N_IN_A], jnp.zeros((D_MODEL, ZA - N_IN_A), BF16)], axis=1),
        "w_g": w_in[:, N_IN_A:],
        "conv_w": small["conv_w"][l],
        "pool_w": small["pool_w"][l], "sgu_w": small["sgu_w"][l],
        "sgu_bias": jnp.broadcast_to(small["sgu_b"][l][:, :, None], (4, CHUNK, GDIM)),
    }
    for name in ("pre_mix_g", "pool_scale", "conv_b", "conv_norm_g", "conv_norm_b", "sgu_norm_g", "sgu_norm_b",
                 "q_norm_g", "kv_norm_g", "post_mix_g", "pre_mlp_g", "post_mlp_g"):
        w[name] = small[name][l][None, :]
    return w


def _late(w, name):
    if name not in w:
        gat, l = w["gat"], w["layer"]
        if name == "proj_cat":
            w[name] = jnp.concatenate([_cols_full(gat[(n, l)]) for n in ("pool_proj", "conv_proj", "sgu_proj", "attn_proj")],
                                      axis=0)
        elif name == "w_uq":
            w[name] = jnp.pad(gat[("w_uq", l)].transpose(1, 0, 2),
                              ((0, 0), (0, 0), (0, QW - QK_NOPE - QK_ROPE))).reshape(BR, MLA_HEADS * QW)
        elif name == "w_ukv":
            w[name] = _cols_full(gat[("w_ukv", l)])
        elif name == "w_up":
            w[name] = gat[("w_up", l)]
        else:
            arr = gat[(name, l)]
            w[name] = arr.reshape(N_DEV * arr.shape[1], arr.shape[2])
    return w[name]


def _layer_fwd(x, h1, w, tabs, l, plan):
    ck, sk, cq, sq = tabs
    n = f"l{l}_"
    za = _mm(h1, w["w_a"], mode="nn", name=n + "za", plan=plan)
    zg = _mm(h1, w["w_g"], mode="nn", name=n + "zg", plan=plan)
    a_pool = _pool_fwd(za, w["pool_w"], w["pool_scale"], n + "pool")
    yc, a_conv = _conv_fwd(za, w["conv_w"], w["conv_b"], w["conv_norm_g"], w["conv_norm_b"], n + "conv")
    a_sgu = _sgu_fwd(za, w["sgu_norm_g"], w["sgu_norm_b"], w["sgu_w"], w["sgu_bias"], n + "sgu")
    qn, kvn, krr = _mla_prep(za, w["q_norm_g"], w["kv_norm_g"], ck, sk, n + "mla_prep")
    q = _mm(qn, _late(w, "w_uq"), mode="nn", name=n + "q", out_dtypes=(BF16,), extras=(cq, sq), tn=QW,
            epilogue=lambda acc, c, sn: ((acc * c + _rot_half(acc, QW, QK_NOPE) * sn) * ATT_SCALE,))
    kv = _mm(kvn, _late(w, "w_ukv"), mode="nn", name=n + "kv", out_dtypes=(BF16,))
    o, lse = _flash_fwd(q, kv, krr, n + "flash", plan=plan)
    act_cat = jnp.concatenate([a_pool, a_conv, a_sgu, o], axis=1)
    y = _proj_fwd(act_cat, _late(w, "proj_cat"), n + "proj", plan=plan)
    merged = _merge_fwd(y, zg, n + "merge")
    m2 = _mm(merged, _late(w, "w_out"), mode="nn", name=n + "out")
    x1, h2 = _post_pre(x, m2, w["post_mix_g"], w["pre_mlp_g"], n + "post_mix")
    up, act = _mm(h2, _late(w, "w_up"), mode="nn", name=n + "up", out_dtypes=(F32, BF16), plan=plan,
                  mnk=(x.shape[0], D_FF, D_MODEL), tn=UP_SHARD,
                  b_spec=lambda tn, tk: pl.BlockSpec((None, tk, tn), lambda i, j, kk: (j, kk, 0)),
                  epilogue=lambda acc: (acc, jnp.square(jnp.maximum(acc, 0.0))))
    f = _mm(act, _late(w, "w_down"), mode="nn", name=n + "down", plan=plan)
    saved = dict(x=x, h1=h1, za=za, zg=zg, yc=yc, qn=qn, kvn=kvn, krr=krr, q=q, kv=kv, o=o, lse=lse, act_cat=act_cat,
                 y=y, merged=merged, m2=m2, x1=x1, h2=h2, up=up, act=act, f=f)
    return x1, f, saved


def _layer_bwd(dx_out, df, sv, w, tabs, l, prev, plan, rs):
    ck, sk, cq, sq = tabs
    n = f"l{l}_b_"
    g = {}
    own, half = {}, {}

    def to_sibling(kernel, names):
        def done(results):
            for name, t in zip(names, results[0]):
                half[name] = _rs_chip_sum(own[name], t, f"l{l}_chip_sum_{name}")
        plan.at(n + kernel, lambda: [_scatter_sibling([own[name] for name in names])], done)

    def to_chips(kernel, names):
        def done(results):
            for name, t in zip(names, results[0]):
                rs[(name, l)] = (half[name], t)
        plan.at(n + kernel, lambda: [_scatter_chips([half[name] for name in names])], done)

    own["w_down"] = _mm(sv["act"], df, mode="tn", name=n + "dw_down", out_dtypes=(BF16,), plan=plan).reshape(
        N_DEV, D_FF // N_DEV, D_MODEL)
    to_sibling("dup", ["w_down"])
    dup = _mm(df, _late(w, "w_down"), mode="nt", name=n + "dup", out_dtypes=(BF16,), extras=(sv["up"],), plan=plan,
              epilogue=lambda acc, up: (acc * 2.0 * jnp.maximum(up, 0.0),))
    to_chips("dw_up", ["w_down"])
    own["w_up"] = _mm(sv["h2"], dup, mode="tn", name=n + "dw_up", tn=UP_SHARD, plan=plan,
                      o_spec=lambda tm, tn: pl.BlockSpec((None, tm, tn), lambda i, j, kk: (j, i, 0)),
                      out_struct=jax.ShapeDtypeStruct((N_DEV, D_MODEL, UP_SHARD), BF16))
    to_sibling("dh2", ["w_up"])
    dh2 = _mm(dup, _late(w, "w_up"), mode="nt", name=n + "dh2", mnk=(dup.shape[0], D_MODEL, D_FF), tk=UP_SHARD,
              plan=plan, b_spec=lambda tn, tk: pl.BlockSpec((None, tn, tk), lambda i, j, kk: (kk, j, 0)))
    dx1, dm2, g["pre_mlp_g"], g["post_mix_g"] = _pre_bwd(dh2, sv["x1"], w["pre_mlp_g"], dx_out, n + "pre_mlp",
                                                           r_prev=sv["m2"], g_post_prev=w["post_mix_g"])
    own["w_out"] = _mm(sv["merged"], dm2, mode="tn", name=n + "dw_out", out_dtypes=(BF16,)).reshape(
        N_DEV, D_MODEL // N_DEV, D_MODEL)
    to_sibling("dmerged", ["w_out"])
    dmerged = _mm(dm2, _late(w, "w_out"), mode="nt", name=n + "dmerged", plan=plan)
    to_chips("merge", ["w_up"])
    dy, dzg = _merge_bwd(dmerged, sv["y"], sv["zg"], n + "merge", plan=plan)
    d_proj = _proj_bwd_w(sv["act_cat"], dy, n + "dw_proj")
    projs = ["pool_proj", "conv_proj", "sgu_proj", "attn_proj"]
    for i, name in enumerate(projs):
        own[name] = _cols_by_owner(d_proj[i * BR:(i + 1) * BR] if i < 3 else d_proj[3 * BR:])
    to_chips("dact", ["w_out"])
    to_sibling("dact", projs)
    dact = _proj_bwd_act(dy, _late(w, "proj_cat"), n + "dact", plan=plan)
    dz_pool, g["pool_w"], g["pool_scale"] = _pool_bwd(dact, sv["za"], w["pool_w"], w["pool_scale"], n + "pool")
    dyc, g["conv_b"], g["conv_norm_g"], g["conv_norm_b"] = _conv_bwd_norm(dact, sv["yc"], w["conv_norm_g"],
                                                                          w["conv_norm_b"], n + "conv_norm")
    dz_conv, g["conv_w"] = _conv_bwd_taps(dyc, sv["za"], w["conv_w"], n + "conv_taps")
    dz_sgu, g["sgu_w"], g["sgu_b"], g["sgu_norm_g"], g["sgu_norm_b"] = _sgu_bwd(
        dact, sv["za"], w["sgu_norm_g"], w["sgu_norm_b"], w["sgu_w"], w["sgu_bias"], n + "sgu")
    to_chips("flash", projs)
    dkv, dkr_heads, dq = _flash_bwd(sv["q"], sv["kv"], sv["krr"], dact, sv["o"], sv["lse"], cq, sq, n + "flash",
                                    plan=plan)
    d_uq = _mm(sv["qn"], dq, mode="tn", name=n + "dw_uq", out_dtypes=(BF16,))
    own["w_uq"] = d_uq.reshape(BR, MLA_HEADS, QW)[:, :, :QK_NOPE + QK_ROPE].transpose(1, 0, 2)
    dqn = _mm(dq, _late(w, "w_uq"), mode="nt", name=n + "dqn")
    own["w_ukv"] = _cols_by_owner(_mm(sv["kvn"], dkv, mode="tn", name=n + "dw_ukv", out_dtypes=(BF16,)))
    to_sibling("dkvn", ["w_uq", "w_ukv"])
    dkvn = _mm(dkv, _late(w, "w_ukv"), mode="nt", name=n + "dkvn", plan=plan)
    dz_mla, g["q_norm_g"], g["kv_norm_g"] = _mla_prep_bwd(dqn, dkvn, dkr_heads, sv["za"], w["q_norm_g"],
                                                           w["kv_norm_g"], ck, sk, n + "mla_prep")
    dza = jnp.concatenate([dz_pool, dz_conv, dz_sgu, dz_mla], axis=1)
    to_chips("dw_a", ["w_uq", "w_ukv"])
    d_a = _mm(sv["h1"], dza, mode="tn", name=n + "dw_a", out_dtypes=(BF16,), plan=plan)
    d_g = _mm(sv["h1"], dzg, mode="tn", name=n + "dw_g", out_dtypes=(BF16,))
    own["w_in"] = _cols_by_owner(jnp.concatenate([d_a[:, :N_IN_A], d_g], axis=1))
    to_sibling("dh1_a", ["w_in"])
    dh1 = _mm(dza, w["w_a"], mode="nt", name=n + "dh1_a", plan=plan)
    to_chips("dh1_g", ["w_in"])
    dh1 = _mm(dzg, w["w_g"], mode="nt", name=n + "dh1_g", extras=(dh1,), epilogue=lambda acc, e: (acc + e,), plan=plan)
    if prev is None:
        dx, g["pre_mix_g"] = _pre_bwd(dh1, sv["x"], w["pre_mix_g"], dx1, n + "pre_mix")
        return dx, None, g
    dx, df_prev, g["pre_mix_g"], g_prev_post = _pre_bwd(dh1, sv["x"], w["pre_mix_g"], dx1, n + "pre_mix",
                                                         r_prev=prev[0], g_post_prev=prev[1])
    g["prev_post_mlp_g"] = g_prev_post
    return dx, df_prev, g


MIDSIZE = ("pool_proj", "conv_proj", "sgu_proj", "w_uq", "w_ukv", "attn_proj")
GATHER_STEPS = (
    ("l0_za", tuple((name, 0) for name in MIDSIZE)),
    ("l0_zg", (("w_up", 0), ("w_out", 0))),
    ("l0_flash", (("w_down", 0), ("w_in", 1))),
    ("l0_proj", tuple((name, 1) for name in MIDSIZE)),
    ("l0_up", (("w_up", 1), ("w_out", 1))),
    ("l0_down", (("w_down", 1),)),
    ("l1_za", ()),
)


def _plan_gathers(plan, gat, shards):
    first_half = {}
    for step, (kernel, keys) in enumerate(GATHER_STEPS):
        before = GATHER_STEPS[step - 1][1] if step else ()

        def make(keys=keys, before=before):
            ops = [_gather_pass(first_half[before])] if before else []
            return ops + ([_gather_own([shards[key] for key in keys])] if keys else [])

        def done(results, keys=keys, before=before):
            if before:
                gat.update(zip(before, results[0]))
            if keys:
                first_half[keys] = results[-1]

        plan.at(kernel, make, done)


def _local_step(x, positions, target, gat, small, shards, plan):
    tabs = _rope_tables(positions)
    _plan_gathers(plan, gat, shards)
    ws = [_layer_weights(gat, small, 0)]
    saved = []
    h = _pre_norm(x, ws[0]["pre_mix_g"], "l0_pre_mix")
    cur = x
    for l in range(DEPTH):
        x1, f, sv = _layer_fwd(cur, h, ws[l], tabs, l, plan)
        saved.append(sv)
        if l + 1 < DEPTH:
            ws.append(_layer_weights(gat, small, l + 1))
            cur, h = _post_pre(x1, f, ws[l]["post_mlp_g"], ws[l + 1]["pre_mix_g"], f"l{l}_post_mlp")
    top = DEPTH - 1
    dx, df, dg_post, loss = _final_loss(saved[top]["x1"], saved[top]["f"], ws[top]["post_mlp_g"], target, "loss")
    grads = [None] * DEPTH
    post_mlp = {top: dg_post}
    rs = {}
    for l in range(top, -1, -1):
        prev = (saved[l - 1]["f"], ws[l - 1]["post_mlp_g"]) if l > 0 else None
        dx, df, g = _layer_bwd(dx, df, saved[l], ws[l], tabs, l, prev, plan, rs)
        if l > 0:
            post_mlp[l - 1] = g.pop("prev_post_mlp_g")
        grads[l] = g
    for l in range(DEPTH):
        grads[l]["post_mlp_g"] = post_mlp[l]
    assert not plan.jobs, sorted(plan.jobs)
    return loss[0, 0], dx, grads, rs


def _small_grads(grads):
    small = {}

    def stack(fn):
        return jnp.stack([fn(grads[l]) for l in range(DEPTH)])

    for name, shape in SMALL:
        if name == "sgu_b":
            small[name] = stack(lambda g: g["sgu_b"][:, :, 0])
        else:
            small[name] = stack(lambda g, name=name, shape=shape: g[name].reshape(shape))
    small["conv_w"] = stack(lambda g: g["conv_w"][:CONV_WIDTH])
    return small


SMALL_ROWS = sum(DEPTH * math.prod(shape) // 128 for _, shape in SMALL)
CONVW_ROWS = DEPTH * CONV_WIDTH * BR // 128


def _pack_small(parts):
    return jnp.concatenate([parts[name].astype(F32).reshape(-1, 128) for name, _ in SMALL], axis=0)


def _unpack_small(buf):
    out, off = {}, 0
    for name, shape in SMALL:
        rows = DEPTH * math.prod(shape) // 128
        out[name] = buf[off:off + rows].reshape((DEPTH,) + shape)
        off += rows
    return out


def _mesh_pos():
    return lax.axis_index("x"), lax.axis_index("y"), lax.axis_index("c")


def _all_gather(shards, name):
    n = len(shards)

    def body(*refs):
        x_refs, out_refs = refs[:n], refs[n:2 * n]
        send_sems, recv_sems, local_sems = refs[2 * n:]
        x, y, c = _mesh_pos()
        me, sibling = (x, y, c), (x, y, 1 - c)
        chips = [(1 - x, y), (x, 1 - y), (1 - x, 1 - y)]

        def rows(t, px, py, pc):
            return out_refs[t].at[4 * px + 2 * py + pc]

        def copy(t, k, block, to, src=None):
            return pltpu.make_async_remote_copy(
                src_ref=rows(t, *block) if src is None else src, dst_ref=rows(t, *block), send_sem=send_sems.at[t, k],
                recv_sem=recv_sems.at[t, k], device_id=to, device_id_type=MESH)

        mine = [pltpu.make_async_copy(x_refs[t], rows(t, *me), local_sems.at[t]) for t in range(n)]
        for cp in mine:
            cp.start()
        first = []
        for t in range(n):
            first.append(copy(t, 0, me, sibling, src=x_refs[t]))
            first += [copy(t, 1 + j, me, (*chip, c), src=x_refs[t]) for j, chip in enumerate(chips)]
        for cp in first:
            cp.start()
        passed = []
        for t in range(n):
            for j, chip in enumerate(chips):
                copy(t, 1 + j, (*chip, c), me).wait_recv()
                passed.append(copy(t, 4 + j, (*chip, c), sibling))
                passed[-1].start()
        for t in range(n):
            copy(t, 0, sibling, me).wait_recv()
            for j, chip in enumerate(chips):
                copy(t, 4 + j, (*chip, 1 - c), me).wait_recv()
        for cp in first + passed:
            cp.wait_send()
        for cp in mine:
            cp.wait()

    hbm = pl.BlockSpec(memory_space=pl.ANY)
    return pl.pallas_call(
        body, name=name, out_shape=[jax.ShapeDtypeStruct((N_DEV,) + a.shape, a.dtype) for a in shards],
        in_specs=[hbm] * n, out_specs=[hbm] * n,
        scratch_shapes=[pltpu.SemaphoreType.DMA((n, 7)), pltpu.SemaphoreType.DMA((n, 7)),
                        pltpu.SemaphoreType.DMA((n,))])(*shards)


def _row_tile(r, c_, cap_bytes=1 << 20):
    best = 0
    for t in range(16, r + 1, 16):
        if r % t == 0 and t * c_ * 4 <= cap_bytes:
            best = t
    return best if best else r


def _rs_chip_sum(g, t, name):
    _, r, c_ = g.shape
    tr = _row_tile(r, c_)
    core = lax.axis_index("c").astype(jnp.int32).reshape(1)

    def body(core_ref, g_ref, t_ref, p_ref):
        p_ref[...] = (g_ref[...].astype(F32) + t_ref[...].astype(F32)).astype(p_ref.dtype)

    return pl.pallas_call(
        body, name=name, out_shape=jax.ShapeDtypeStruct((4, r, c_), g.dtype),
        grid_spec=pltpu.PrefetchScalarGridSpec(
            num_scalar_prefetch=1, grid=(4, r // tr),
            in_specs=[pl.BlockSpec((1, tr, c_), lambda k, i, core_ref: (2 * k + core_ref[0], i, 0)),
                      pl.BlockSpec((1, tr, c_), lambda k, i, core_ref: (k, i, 0))],
            out_specs=pl.BlockSpec((1, tr, c_), lambda k, i, core_ref: (k, i, 0))),
        compiler_params=_params(("parallel", "parallel")))(core, g, t)


def _adamw_math(w, g, m, v):
    m = ADAM_B1 * m + (1.0 - ADAM_B1) * g
    v = ADAM_B2 * v + (1.0 - ADAM_B2) * jnp.square(g)
    m_hat = m / (1.0 - ADAM_B1 ** ADAM_STEP)
    v_hat = v / (1.0 - ADAM_B2 ** ADAM_STEP)
    delta = -ADAM_LR * (m_hat / (jnp.sqrt(v_hat) + ADAM_EPS) + ADAM_WD * w)
    return delta, m, v


def _adamw_big(ps, ts, w, m, v, name):
    _, r, c_ = w.shape
    tr = _row_tile(r, c_)
    chip = (2 * lax.axis_index("x") + lax.axis_index("y")).astype(jnp.int32).reshape(1)

    def body(chip_ref, p0, t0, p1, t1, w_ref, m_ref, v_ref, g_out, d_out, m_out, v_out):
        def update(p_ref, t_ref):
            g = p_ref[0].astype(F32) + t_ref[0].astype(F32) + t_ref[1].astype(F32) + t_ref[2].astype(F32)
            g_out[0] = g
            d_out[0], m_out[0], v_out[0] = _adamw_math(w_ref[0], g, m_ref[0], v_ref[0])

        pl.when(pl.program_id(0) == 0)(functools.partial(update, p0, t0))
        pl.when(pl.program_id(0) == 1)(functools.partial(update, p1, t1))

    def grad_specs(layer):
        return [pl.BlockSpec((1, tr, c_), lambda l, i, chip_ref: (chip_ref[0], jnp.where(l == layer, i, 0), 0)),
                pl.BlockSpec((3, tr, c_), lambda l, i, chip_ref: (0, jnp.where(l == layer, i, 0), 0))]

    nat = pl.BlockSpec((1, tr, c_), lambda l, i, chip_ref: (l, i, 0))
    shape = jax.ShapeDtypeStruct(w.shape, F32)
    return pl.pallas_call(
        body, name=name, out_shape=[shape] * 4,
        grid_spec=pltpu.PrefetchScalarGridSpec(
            num_scalar_prefetch=1, grid=(DEPTH, r // tr),
            in_specs=grad_specs(0) + grad_specs(1) + [nat, nat, nat], out_specs=[nat] * 4),
        compiler_params=_params(("parallel", "parallel"), VMEM_BIG))(chip, ps[0], ts[0], ps[1], ts[1], w, m, v)


def _sum_devices(parts, name):
    _, r, c_ = parts.shape

    def body(p_ref, o_ref):
        acc = p_ref[0]
        for d in range(1, N_DEV):
            acc = acc + p_ref[d]
        o_ref[...] = acc

    return pl.pallas_call(body, name=name, out_shape=jax.ShapeDtypeStruct((r, c_), F32),
                          compiler_params=_params(None, VMEM_BIG))(parts)


def _adamw_small(w, g, m, v, name):
    def body(w_ref, g_ref, m_ref, v_ref, d_out, m_out, v_out):
        d_out[...], m_out[...], v_out[...] = _adamw_math(w_ref[...], g_ref[...], m_ref[...], v_ref[...])

    shape = jax.ShapeDtypeStruct(w.shape, F32)
    return pl.pallas_call(body, name=name, out_shape=[shape] * 3)(w, g, m, v)


def kernel(x, positions, pre_mix_g, w_in, pool_w, pool_scale, pool_proj, conv_w, conv_b, conv_norm_g, conv_norm_b, conv_proj, sgu_norm_g, sgu_norm_b, sgu_w, sgu_b, sgu_proj, q_norm_g, w_uq, kv_norm_g, w_ukv, attn_proj, w_out, post_mix_g, pre_mlp_g, w_up, w_down, post_mlp_g, loss_target, m_pre_mix_g, m_w_in, m_pool_w, m_pool_scale, m_pool_proj, m_conv_w, m_conv_b, m_conv_norm_g, m_conv_norm_b, m_conv_proj, m_sgu_norm_g, m_sgu_norm_b, m_sgu_w, m_sgu_b, m_sgu_proj, m_q_norm_g, m_w_uq, m_kv_norm_g, m_w_ukv, m_attn_proj, m_w_out, m_post_mix_g, m_pre_mlp_g, m_w_up, m_w_down, m_post_mlp_g, v_pre_mix_g, v_w_in, v_pool_w, v_pool_scale, v_pool_proj, v_conv_w, v_conv_b, v_conv_norm_g, v_conv_norm_b, v_conv_proj, v_sgu_norm_g, v_sgu_norm_b, v_sgu_w, v_sgu_b, v_sgu_proj, v_q_norm_g, v_w_uq, v_kv_norm_g, v_w_ukv, v_attn_proj, v_w_out, v_post_mix_g, v_pre_mlp_g, v_w_up, v_w_down, v_post_mlp_g):
    args = dict(locals())
    wts = {n: args[n] for n in WEIGHTS}
    mom1 = {n: args["m_" + n] for n in WEIGHTS}
    mom2 = {n: args["v_" + n] for n in WEIGHTS}
    dev = 4 * lax.axis_index("x") + 2 * lax.axis_index("y") + lax.axis_index("c")

    shards = {(name, l): wts[name][l].astype(BF16) for name, _, _ in BIG for l in range(DEPTH)}
    taps = jnp.pad(conv_w.reshape(-1, 128), ((0, 1), (0, 0)))
    gathered = _all_gather([shards[("w_in", 0)], taps], "gather_first")
    gat = {("w_in", 0): gathered[0]}
    taps = gathered[-1][:, :CONV_WIDTH].reshape(N_DEV, DEPTH, CONV_WIDTH, BR // N_DEV)
    small = {n: wts[n] for n, _ in SMALL}
    small["conv_w"] = taps.transpose(1, 2, 0, 3).reshape(DEPTH, CONV_WIDTH, BR)

    loss_part, grad_x, grads, rs = _local_step(x[0], positions, loss_target[0], gat, small, shards, _Plan())
    small_g = _small_grads(grads)
    loss = lax.psum(loss_part, ("x", "y", "c"))

    out = {"grad": {}, "delta": {}, "new_m": {}, "new_v": {}}
    for name, _, _ in BIG:
        res = _adamw_big([rs[(name, l)][0] for l in range(DEPTH)], [rs[(name, l)][1] for l in range(DEPTH)],
                         wts[name], mom1[name], mom2[name], "adamw_" + name)
        for key, buf in zip(("grad", "delta", "new_m", "new_v"), res):
            out[key][name] = buf

    part = jnp.concatenate([_pack_small(small_g), small_g["conv_w"].reshape(-1, 128)], axis=0)
    total = _sum_devices(_all_gather([part], "gather_small_grads")[0], "sum_small_grads")
    g_small = total[:SMALL_ROWS]
    d_small, m_small, v_small = _adamw_small(_pack_small(wts), g_small, _pack_small(mom1), _pack_small(mom2), "adamw_small")
    for key, buf in (("grad", g_small), ("delta", d_small), ("new_m", m_small), ("new_v", v_small)):
        out[key].update(_unpack_small(buf))
    g_taps = total[SMALL_ROWS:].reshape(DEPTH, CONV_WIDTH, N_DEV, BR // N_DEV)
    g_taps = lax.dynamic_index_in_dim(g_taps, dev, axis=2, keepdims=False)
    flat = lambda a: a.reshape(-1, 128)
    d_taps, m_taps, v_taps = _adamw_small(flat(conv_w), flat(g_taps), flat(m_conv_w), flat(v_conv_w), "adamw_taps")
    for key, buf in (("grad", g_taps), ("delta", d_taps), ("new_m", m_taps), ("new_v", v_taps)):
        out[key]["conv_w"] = buf.reshape(conv_w.shape)

    return (loss, grad_x[None], *[out["grad"][n] for n in WEIGHTS], *[out["delta"][n] for n in WEIGHTS],
            *[out["new_m"][n] for n in WEIGHTS], *[out["new_v"][n] for n in WEIGHTS])
```

```python
import collections
import functools
import math

import jax
import jax.numpy as jnp
from jax import lax
from jax.experimental import pallas as pl
from jax.experimental.pallas import tpu as pltpu

F32 = jnp.float32
BF16 = jnp.bfloat16

D_MODEL = 2048
DEPTH = 2
EPS = 1e-6
N_BRANCH = 4
D_FF = 4 * D_MODEL
UP_SHARD = D_FF // 8
POOL_WINDOWS = (2, 4, 8, 16)
CONV_WIDTH = 31
CHUNK = 128
MLA_HEADS = 8
QK_NOPE = 128
QK_ROPE = 64
V_DIM = 128
ROPE_THETA = 10000.0
GDIM = 128
BR = 512
N_IN_A = 3648
ZA = 3712
N_GATE = N_BRANCH * D_MODEL
N_IN = N_IN_A + N_GATE
QW = 256
ACT_CAT = 3 * BR + MLA_HEADS * V_DIM
ATT_SCALE = (QK_NOPE + QK_ROPE) ** -0.5
NEG = -1e30

ADAM_LR = 0.001
ADAM_B1 = 0.9
ADAM_B2 = 0.999
ADAM_EPS = 1e-08
ADAM_WD = 0.01
ADAM_STEP = 10

N_DEV = 8
PACK_W = 1024
VMEM_BIG = 48 * 1024 * 1024
MESH = pl.DeviceIdType.MESH

BIG = (
    ("w_in", 1, (2048, 1480)),
    ("pool_proj", 1, (512, 256)),
    ("conv_proj", 1, (512, 256)),
    ("sgu_proj", 1, (512, 256)),
    ("w_uq", 1, (512, 192)),
    ("w_ukv", 1, (512, 256)),
    ("attn_proj", 1, (1024, 256)),
    ("w_out", 0, (256, 2048)),
    ("w_up", 1, (2048, 1024)),
    ("w_down", 0, (1024, 2048)),
)
SMALL = (
    ("pre_mix_g", (2048,)), ("pool_w", (4, 128, 128)), ("pool_scale", (512,)), ("conv_b", (512,)),
    ("conv_norm_g", (512,)), ("conv_norm_b", (512,)), ("sgu_norm_g", (512,)), ("sgu_norm_b", (512,)),
    ("sgu_w", (4, 128, 128)), ("sgu_b", (4, 128)), ("q_norm_g", (512,)), ("kv_norm_g", (512,)),
    ("post_mix_g", (2048,)), ("pre_mlp_g", (2048,)), ("post_mlp_g", (2048,)),
)
WEIGHTS = ("pre_mix_g", "w_in", "pool_w", "pool_scale", "pool_proj", "conv_w", "conv_b", "conv_norm_g", "conv_norm_b",
           "conv_proj", "sgu_norm_g", "sgu_norm_b", "sgu_w", "sgu_b", "sgu_proj", "q_norm_g", "w_uq", "kv_norm_g",
           "w_ukv", "attn_proj", "w_out", "post_mix_g", "pre_mlp_g", "w_up", "w_down", "post_mlp_g")


def _params(sem=None, vmem=None):
    return pltpu.CompilerParams(dimension_semantics=sem, vmem_limit_bytes=vmem)


def _tile(dim, pref):
    if dim <= pref:
        return dim
    best = 0
    for t in range(128, pref + 1, 128):
        if dim % t == 0:
            best = t
    return best if best >= 256 else dim


def _sigmoid(x):
    return 1.0 / (1.0 + jnp.exp(-x))


def _gelu(x):
    k = math.sqrt(2.0 / math.pi)
    return 0.5 * x * (1.0 + jnp.tanh(k * (x + 0.044715 * x * x * x)))


def _gelu_grad(x):
    k = math.sqrt(2.0 / math.pi)
    t = jnp.tanh(k * (x + 0.044715 * x * x * x))
    return 0.5 * (1.0 + t) + 0.5 * x * (1.0 - t * t) * k * (1.0 + 3.0 * 0.044715 * x * x)


def _rms(x, g):
    r = lax.rsqrt(jnp.mean(x * x, axis=-1, keepdims=True) + EPS)
    return x * r * g


def _rms_bwd(x, g, dy):
    r = lax.rsqrt(jnp.mean(x * x, axis=-1, keepdims=True) + EPS)
    dyg = dy * g
    dx = r * dyg - x * (r * r * r) * jnp.mean(dyg * x, axis=-1, keepdims=True)
    return dx, dy * x * r


def _ln_stats(x):
    mu = jnp.mean(x, axis=-1, keepdims=True)
    xc = x - mu
    r = lax.rsqrt(jnp.mean(xc * xc, axis=-1, keepdims=True) + EPS)
    return xc * r, r


def _ln_bwd(xh, r, g, dy):
    dxh = dy * g
    return r * (dxh - jnp.mean(dxh, axis=-1, keepdims=True) - xh * jnp.mean(dxh * xh, axis=-1, keepdims=True))


def _rot_half(x, width, off):
    n = x.shape[-1]
    lane = lax.broadcasted_iota(jnp.int32, x.shape, x.ndim - 1) % width
    return jnp.where(lane - off < QK_ROPE // 2, -pltpu.roll(x, n - QK_ROPE // 2, x.ndim - 1),
                     pltpu.roll(x, QK_ROPE // 2, x.ndim - 1))


def _colsum_into(ref, val, first):
    s = jnp.sum(val, axis=0, keepdims=True)

    @pl.when(first)
    def _():
        ref[...] = s

    @pl.when(jnp.logical_not(first))
    def _():
        ref[...] += s


Exchange = collections.namedtuple("Exchange", "inputs out_shapes aliases n_pairs n_local build")


class _Plan:
    def __init__(self):
        self.jobs = {}

    def at(self, kernel, make, done):
        self.jobs.setdefault(kernel, []).append((make, done))

    def take(self, kernel):
        return self.jobs.pop(kernel, [])


def _pcall(body, operands, *, name, grid, in_specs, out_specs, out_shape, scratch_shapes=(), sem=None, vmem=None,
           plan=None):
    jobs = plan.take(name) if plan is not None else []
    if not jobs:
        return pl.pallas_call(body, name=name, grid=grid, in_specs=list(in_specs), out_specs=list(out_specs),
                              out_shape=list(out_shape), scratch_shapes=list(scratch_shapes),
                              compiler_params=_params(sem, vmem))(*operands)
    made = [(make(), done) for make, done in jobs]
    comm = [op for ops, _ in made for op in ops]
    n_in, n_out, n_scr = len(in_specs), len(out_shape), len(scratch_shapes)
    c_in = [a for op in comm for a in op.inputs]
    c_out = [s for op in comm for s in op.out_shapes]
    sems, aliases, i_off, o_off = [], {}, n_in, n_out
    for op in comm:
        sems += [pltpu.SemaphoreType.DMA((op.n_pairs,)), pltpu.SemaphoreType.DMA((op.n_pairs,)),
                 pltpu.SemaphoreType.DMA((max(op.n_local, 1),))]
        for src, dst in op.aliases.items():
            aliases[i_off + src] = o_off + dst
        i_off += len(op.inputs)
        o_off += len(op.out_shapes)

    def carrier(*refs):
        ins, cins = refs[:n_in], refs[n_in:n_in + len(c_in)]
        base = n_in + len(c_in)
        outs, couts = refs[base:base + n_out], refs[base + n_out:base + n_out + len(c_out)]
        base += n_out + len(c_out)
        scr, csems = refs[base:base + n_scr], refs[base + n_scr:]
        ids = [pl.program_id(ax) for ax in range(len(grid))]
        first = functools.reduce(jnp.logical_and, [i == 0 for i in ids])
        last = functools.reduce(jnp.logical_and, [i == g - 1 for i, g in zip(ids, grid)])

        def pieces():
            res, ci, co = [], 0, 0
            for k, op in enumerate(comm):
                res.append(op.build(cins[ci:ci + len(op.inputs)], couts[co:co + len(op.out_shapes)],
                                    *csems[3 * k:3 * k + 3]))
                ci += len(op.inputs)
                co += len(op.out_shapes)
            return res

        @pl.when(first)
        def _():
            for sends, _, local in pieces():
                for cp in local + sends:
                    cp.start()

        body(*ins, *outs, *scr)

        @pl.when(last)
        def _():
            for sends, recvs, local in pieces():
                for cp in recvs:
                    cp.wait_recv()
                for cp in sends:
                    cp.wait_send()
                for cp in local:
                    cp.wait()

    hbm = pl.BlockSpec(memory_space=pl.ANY)
    res = pl.pallas_call(
        carrier, name=name, grid=grid, in_specs=list(in_specs) + [hbm] * len(c_in),
        out_specs=list(out_specs) + [hbm] * len(c_out), out_shape=list(out_shape) + c_out,
        scratch_shapes=list(scratch_shapes) + sems, input_output_aliases=aliases,
        compiler_params=_params(("arbitrary",) * len(grid), vmem))(*operands, *c_in)
    pos = n_out
    for ops, done in made:
        results = []
        for op in ops:
            results.append(list(res[pos:pos + len(op.out_shapes)]))
            pos += len(op.out_shapes)
        done(results)
    return list(res[:n_out])


def _block(ref, px, py, pc):
    return ref.at[4 * px + 2 * py + pc]


def _remote(src, dst, send_sems, recv_sems, k, to):
    return pltpu.make_async_remote_copy(src_ref=src, dst_ref=dst, send_sem=send_sems.at[k], recv_sem=recv_sems.at[k],
                                        device_id=to, device_id_type=MESH)


def _gather_own(shards):
    n = len(shards)

    def build(ins, outs, send_sems, recv_sems, local_sems):
        x, y, c = _mesh_pos()
        peers = [(x, y, 1 - c), (1 - x, y, c), (x, 1 - y, c), (1 - x, 1 - y, c)]
        sends, recvs, local = [], [], []
        for t in range(n):
            local.append(pltpu.make_async_copy(ins[t], _block(outs[t], x, y, c), local_sems.at[t]))
            for k, peer in enumerate(peers):
                sends.append(_remote(ins[t], _block(outs[t], x, y, c), send_sems, recv_sems, 4 * t + k, peer))
                recvs.append(_remote(ins[t], _block(outs[t], *peer), send_sems, recv_sems, 4 * t + k, peer))
        return sends, recvs, local

    return Exchange(list(shards), [jax.ShapeDtypeStruct((N_DEV,) + a.shape, a.dtype) for a in shards], {}, 4 * n, n, build)


def _gather_pass(bufs):
    n = len(bufs)

    def build(ins, outs, send_sems, recv_sems, local_sems):
        x, y, c = _mesh_pos()
        chips = [(1 - x, y), (x, 1 - y), (1 - x, 1 - y)]
        sends, recvs = [], []
        for t in range(n):
            for j, chip in enumerate(chips):
                mine, theirs = _block(outs[t], *chip, c), _block(outs[t], *chip, 1 - c)
                sends.append(_remote(mine, mine, send_sems, recv_sems, 3 * t + j, (x, y, 1 - c)))
                recvs.append(_remote(mine, theirs, send_sems, recv_sems, 3 * t + j, (x, y, 1 - c)))
        return sends, recvs, []

    return Exchange(list(bufs), [jax.ShapeDtypeStruct(a.shape, a.dtype) for a in bufs], {t: t for t in range(n)},
                    3 * n, 0, build)


def _scatter_sibling(gs):
    n = len(gs)

    def build(ins, outs, send_sems, recv_sems, local_sems):
        x, y, c = _mesh_pos()
        sends, recvs = [], []
        for t in range(n):
            for k in range(4):
                cp = _remote(ins[t].at[2 * k + 1 - c], outs[t].at[k], send_sems, recv_sems, 4 * t + k, (x, y, 1 - c))
                sends.append(cp)
                recvs.append(cp)
        return sends, recvs, []

    return Exchange(list(gs), [jax.ShapeDtypeStruct((4,) + g.shape[1:], g.dtype) for g in gs], {}, 4 * n, 0, build)


def _scatter_chips(ps):
    n = len(ps)

    def build(ins, outs, send_sems, recv_sems, local_sems):
        x, y, c = _mesh_pos()
        chips = [(1 - x, y), (x, 1 - y), (1 - x, 1 - y)]
        sends, recvs = [], []
        for t in range(n):
            for j, (cx, cy) in enumerate(chips):
                cp = _remote(ins[t].at[2 * cx + cy], outs[t].at[j], send_sems, recv_sems, 3 * t + j, (cx, cy, c))
                sends.append(cp)
                recvs.append(cp)
        return sends, recvs, []

    return Exchange(list(ps), [jax.ShapeDtypeStruct((3,) + p.shape[1:], p.dtype) for p in ps], {}, 3 * n, 0, build)


_DIMS = {"nn": ((1,), (0,)), "nt": ((1,), (1,)), "tn": ((0,), (0,))}


def _mm_call(a, b, *, mode, name, grid, kaxis, nk, a_spec, b_spec, o_specs, out_shape, acc_shape,
             extras=(), e_specs=(), epilogue=None, active=None, plan=None):
    ne, no = len(extras), len(out_shape)

    def body(a_ref, b_ref, *rest):
        e_refs, o_refs, acc_ref = rest[:ne], rest[ne:ne + no], rest[ne + no]
        ids = [pl.program_id(ax) for ax in range(len(grid))]
        k = ids[kaxis]

        def finish(acc):
            outs = (acc,) if epilogue is None else epilogue(acc, *[e[...] for e in e_refs])
            for o_ref, val in zip(o_refs, outs):
                o_ref[...] = val.astype(o_ref.dtype)

        def step():
            prod = lax.dot_general(a_ref[...], b_ref[...], (_DIMS[mode], ((), ())), preferred_element_type=F32)
            if nk == 1:
                finish(prod)
                return

            @pl.when(k == 0)
            def _():
                acc_ref[...] = prod

            @pl.when(k > 0)
            def _():
                acc_ref[...] += prod

        if active is None:
            step()
        else:
            pl.when(active(*ids))(step)
        if nk > 1:
            @pl.when(k == nk - 1)
            def _():
                finish(acc_ref[...])

    sem = tuple("arbitrary" if ax == kaxis else "parallel" for ax in range(len(grid)))
    scratch = pltpu.VMEM(acc_shape if nk > 1 else (8, 128), F32)
    return _pcall(body, (a, b, *extras), name=name, grid=grid, in_specs=[a_spec, b_spec, *e_specs],
                  out_specs=list(o_specs), out_shape=list(out_shape), scratch_shapes=[scratch],
                  sem=sem, vmem=VMEM_BIG, plan=plan)


def _mm(a, b, *, mode, name, out_dtypes=(F32,), extras=(), epilogue=None, tm=1024, tn=1024, tk=2048,
        mnk=None, b_spec=None, o_spec=None, out_struct=None, plan=None):
    if mnk is not None:
        m, n, k = mnk
    elif mode == "nn":
        (m, k), (_, n) = a.shape, b.shape
    elif mode == "nt":
        (m, k), (n, _) = a.shape, b.shape
    else:
        (k, m), (_, n) = a.shape, b.shape
    tm, tn, tk = _tile(m, tm), _tile(n, tn), _tile(k, tk)
    if tn > 2048:
        tm, tk = _tile(m, 512), _tile(k, 512)
    if tk > 2048:
        tm, tn = _tile(m, 512), _tile(n, 512)
    nk = k // tk
    if mode == "tn":
        a_spec = pl.BlockSpec((tk, tm), lambda i, j, kk: (kk, i))
    else:
        a_spec = pl.BlockSpec((tm, tk), lambda i, j, kk: (i, kk))
    if b_spec is not None:
        b_spec = b_spec(tn, tk)
    elif mode == "nt":
        b_spec = pl.BlockSpec((tn, tk), lambda i, j, kk: (j, kk))
    else:
        b_spec = pl.BlockSpec((tk, tn), lambda i, j, kk: (kk, j))

    def e_spec(e):
        if e.shape[1] == tn and n != tn:
            return pl.BlockSpec((tm, tn), lambda i, j, kk: (i, 0))
        return pl.BlockSpec((tm, tn), lambda i, j, kk: (i, j))

    e_specs = [e_spec(e) for e in extras]
    if o_spec is not None:
        o_specs = [o_spec(tm, tn)]
        out_shape = [out_struct]
    else:
        o_specs = [pl.BlockSpec((tm, tn), lambda i, j, kk: (i, j)) for _ in out_dtypes]
        out_shape = [jax.ShapeDtypeStruct((m, n), dt) for dt in out_dtypes]
    outs = _mm_call(a, b, mode=mode, name=name, grid=(m // tm, n // tn, nk), kaxis=2, nk=nk, a_spec=a_spec,
                    b_spec=b_spec, o_specs=o_specs, out_shape=out_shape, acc_shape=(tm, tn), extras=extras,
                    e_specs=e_specs, epilogue=epilogue, plan=plan)
    return outs[0] if len(outs) == 1 else outs


def _branch_of(kb):
    return jnp.minimum(kb, N_BRANCH - 1)


def _proj_merge_fwd(act_cat, proj_cat, zg, name, plan=None):
    s = act_cat.shape[0]
    tm, tn = _tile(s, 1024), 1024
    nj = D_MODEL // tn
    last = N_BRANCH - 1

    def kb(b, k):
        return jnp.where(b < last, b, last + k)

    def body(a_ref, b_ref, z_ref, y_ref, m_ref, acc_ref, sum_ref):
        b, k = pl.program_id(2), pl.program_id(3)

        @pl.when(jnp.logical_or(b == last, k == 0))
        def _():
            prod = jnp.dot(a_ref[...], b_ref[...], preferred_element_type=F32)

            @pl.when(k == 0)
            def _():
                acc_ref[...] = prod

            @pl.when(k > 0)
            def _():
                acc_ref[...] += prod

        @pl.when(k == 1)
        def _():
            y = acc_ref[...]
            y_ref[...] = y.astype(BF16)
            gated = _sigmoid(z_ref[...]) * y

            @pl.when(b == 0)
            def _():
                sum_ref[...] = gated

            @pl.when(b > 0)
            def _():
                sum_ref[...] += gated

            @pl.when(b == last)
            def _():
                m_ref[...] = sum_ref[...].astype(BF16)

    wide = pl.BlockSpec((tm, tn), lambda i, j, b, k: (i, b * nj + j))
    y, merged = _pcall(
        body, (act_cat, proj_cat, zg), name=name, grid=(s // tm, nj, N_BRANCH, 2),
        in_specs=[pl.BlockSpec((tm, BR), lambda i, j, b, k: (i, kb(b, k))),
                  pl.BlockSpec((BR, tn), lambda i, j, b, k: (kb(b, k), j)), wide],
        out_specs=[wide, pl.BlockSpec((tm, tn), lambda i, j, b, k: (i, j))],
        out_shape=[jax.ShapeDtypeStruct((s, N_GATE), BF16), jax.ShapeDtypeStruct((s, D_MODEL), BF16)],
        scratch_shapes=[pltpu.VMEM((tm, tn), F32), pltpu.VMEM((tm, tn), F32)],
        sem=("parallel", "parallel", "arbitrary", "arbitrary"), vmem=VMEM_BIG, plan=plan)
    return y, merged


def _proj_bwd_act(dy, proj_cat, name, plan=None):
    s = dy.shape[0]
    tm, tk = _tile(s, 1024), 1024
    nkk = D_MODEL // tk
    nkb = ACT_CAT // BR
    out = _mm_call(
        dy, proj_cat, mode="nt", name=name, grid=(s // tm, nkb, nkk), kaxis=2, nk=nkk,
        a_spec=pl.BlockSpec((tm, tk), lambda i, kb, k: (i, _branch_of(kb) * nkk + k)),
        b_spec=pl.BlockSpec((BR, tk), lambda i, kb, k: (kb, k)),
        o_specs=[pl.BlockSpec((tm, BR), lambda i, kb, k: (i, kb))],
        out_shape=[jax.ShapeDtypeStruct((s, ACT_CAT), F32)], acc_shape=(tm, BR), plan=plan)
    return out[0]


def _proj_bwd_w(act_cat, dy, name):
    s = dy.shape[0]
    tms, tn = _tile(s, 1024), 1024
    nj = D_MODEL // tn
    nkb = ACT_CAT // BR
    nm = s // tms
    out = _mm_call(
        act_cat, dy, mode="tn", name=name, grid=(nkb, nj, nm), kaxis=2, nk=nm,
        a_spec=pl.BlockSpec((tms, BR), lambda kb, j, m: (m, kb)),
        b_spec=pl.BlockSpec((tms, tn), lambda kb, j, m: (m, _branch_of(kb) * nj + j)),
        o_specs=[pl.BlockSpec((BR, tn), lambda kb, j, m: (kb, j))],
        out_shape=[jax.ShapeDtypeStruct((ACT_CAT, D_MODEL), BF16)], acc_shape=(BR, tn))
    return out[0]


def _row_specs(ts, n_full, n_vec):
    return ([pl.BlockSpec((ts, D_MODEL), lambda i: (i, 0))] * n_full
            + [pl.BlockSpec((1, D_MODEL), lambda i: (0, 0))] * n_vec)


def _pre_norm(x, g, name):
    s = x.shape[0]
    ts = _tile(s, 256)

    def body(x_ref, g_ref, h_ref):
        h_ref[...] = _rms(x_ref[...], g_ref[...]).astype(BF16)

    return pl.pallas_call(body, name=name, grid=(s // ts,), in_specs=_row_specs(ts, 1, 1),
                          out_specs=pl.BlockSpec((ts, D_MODEL), lambda i: (i, 0)),
                          out_shape=jax.ShapeDtypeStruct((s, D_MODEL), BF16), compiler_params=_params(("parallel",)))(x, g)


def _post_pre(x, r, g_post, g_next, name):
    s = x.shape[0]
    ts = _tile(s, 256)

    def body(x_ref, r_ref, gp_ref, gn_ref, xn_ref, h_ref):
        xn = x_ref[...] + _rms(r_ref[...], gp_ref[...])
        xn_ref[...] = xn
        h_ref[...] = _rms(xn, gn_ref[...]).astype(BF16)

    spec = pl.BlockSpec((ts, D_MODEL), lambda i: (i, 0))
    return pl.pallas_call(body, name=name, grid=(s // ts,), in_specs=_row_specs(ts, 2, 2), out_specs=[spec, spec],
                          out_shape=[jax.ShapeDtypeStruct((s, D_MODEL), F32), jax.ShapeDtypeStruct((s, D_MODEL), BF16)],
                          compiler_params=_params(("parallel",)))(x, r, g_post, g_next)


def _final_loss(x, r, g_post, target, name):
    s = x.shape[0]
    ts = _tile(s, 256)

    def body(x_ref, r_ref, gp_ref, t_ref, dy_ref, dr_ref, dg_ref, loss_ref):
        first = pl.program_id(0) == 0
        rv, gp = r_ref[...], gp_ref[...]
        diff = x_ref[...] + _rms(rv, gp) - t_ref[...]
        part = 0.5 * jnp.sum(jnp.mean(diff * diff, axis=-1, keepdims=True), axis=0, keepdims=True)
        dy = diff * (1.0 / D_MODEL)
        dy_ref[...] = dy
        dr, dg_rows = _rms_bwd(rv, gp, dy)
        dr_ref[...] = dr.astype(BF16)
        _colsum_into(dg_ref, dg_rows, first)
        _colsum_into(loss_ref, jnp.broadcast_to(part, (1, 128)), first)

    spec = pl.BlockSpec((ts, D_MODEL), lambda i: (i, 0))
    vec = pl.BlockSpec((1, D_MODEL), lambda i: (0, 0))
    return pl.pallas_call(
        body, name=name, grid=(s // ts,), in_specs=[spec, spec, vec, spec],
        out_specs=[spec, spec, vec, pl.BlockSpec((1, 128), lambda i: (0, 0))],
        out_shape=[jax.ShapeDtypeStruct((s, D_MODEL), F32), jax.ShapeDtypeStruct((s, D_MODEL), BF16),
                   jax.ShapeDtypeStruct((1, D_MODEL), F32), jax.ShapeDtypeStruct((1, 128), F32)],
        compiler_params=_params(("arbitrary",)))(x, r, g_post, target)


def _pre_bwd(dh, x, g_pre, dx_res, name, r_prev=None, g_post_prev=None):
    s = x.shape[0]
    ts = _tile(s, 256)
    chain = r_prev is not None

    def body(*refs):
        if chain:
            dh_ref, x_ref, res_ref, r_ref, g_ref, gp_ref, dx_ref, dr_ref, dg_ref, dgp_ref = refs
        else:
            dh_ref, x_ref, res_ref, g_ref, dx_ref, dg_ref = refs
        first = pl.program_id(0) == 0
        dxn, dg_rows = _rms_bwd(x_ref[...], g_ref[...], dh_ref[...])
        dx = res_ref[...] + dxn
        dx_ref[...] = dx
        _colsum_into(dg_ref, dg_rows, first)
        if chain:
            dr, dgp_rows = _rms_bwd(r_ref[...], gp_ref[...], dx)
            dr_ref[...] = dr.astype(BF16)
            _colsum_into(dgp_ref, dgp_rows, first)

    spec = pl.BlockSpec((ts, D_MODEL), lambda i: (i, 0))
    vec = pl.BlockSpec((1, D_MODEL), lambda i: (0, 0))
    full = jax.ShapeDtypeStruct((s, D_MODEL), F32)
    vshape = jax.ShapeDtypeStruct((1, D_MODEL), F32)
    if chain:
        return pl.pallas_call(
            body, name=name, grid=(s // ts,), in_specs=[spec] * 4 + [vec] * 2, out_specs=[spec, spec, vec, vec],
            out_shape=[full, jax.ShapeDtypeStruct((s, D_MODEL), BF16), vshape, vshape],
            compiler_params=_params(("arbitrary",)))(dh, x, dx_res, r_prev, g_pre, g_post_prev)
    return pl.pallas_call(
        body, name=name, grid=(s // ts,), in_specs=[spec] * 3 + [vec], out_specs=[spec, vec],
        out_shape=[full, vshape], compiler_params=_params(("arbitrary",)))(dh, x, dx_res, g_pre)


def _merge_bwd(dm, y, zg, name, plan=None):
    s = y.shape[0]
    ts, tc = _tile(s, 512), 512
    nj = D_MODEL // tc

    def body(dm_ref, y_ref, z_ref, dy_ref, dz_ref):
        g = _sigmoid(z_ref[...])
        d = dm_ref[...]
        dy_ref[...] = (d * g).astype(BF16)
        dz_ref[...] = (d * y_ref[...] * g * (1.0 - g)).astype(BF16)

    blk = pl.BlockSpec((ts, tc), lambda i, j, b: (i, b * nj + j))
    shape = jax.ShapeDtypeStruct((s, N_GATE), BF16)
    return _pcall(body, (dm, y, zg), name=name, grid=(s // ts, nj, N_BRANCH),
                  in_specs=[pl.BlockSpec((ts, tc), lambda i, j, b: (i, j)), blk, blk], out_specs=[blk, blk],
                  out_shape=[shape, shape], sem=("parallel", "parallel", "parallel"), plan=plan)


POOL_HALO = 16


def _pool_windows(ext_ref, ts, first_row):
    outs = []
    t = first_row + lax.broadcasted_iota(jnp.int32, (ts, GDIM), 0)
    for gi, w in enumerate(POOL_WINDOWS):
        cols = pl.ds(gi * GDIM, GDIM)
        acc = ext_ref[pl.ds(POOL_HALO, ts), cols]
        cur = acc
        for k in range(1, w):
            acc = acc + ext_ref[pl.ds(POOL_HALO - k, ts), cols]
        cnt = jnp.minimum(t + 1, w).astype(F32)
        outs.append(acc / cnt - cur)
    return outs


def _pool_fwd(za, pool_w, pool_scale, name):
    s = za.shape[0]
    ts = _tile(s, 512)
    hb = ts // POOL_HALO

    def body(a_ref, halo_ref, w_ref, sc_ref, o_ref, ext_ref):
        i = pl.program_id(0)
        ext_ref[pl.ds(0, POOL_HALO), :] = jnp.where(i > 0, halo_ref[...], 0.0)
        ext_ref[pl.ds(POOL_HALO, ts), :] = a_ref[...]
        pooled = _pool_windows(ext_ref, ts, i * ts)
        for gi in range(len(POOL_WINDOWS)):
            mixed = jnp.dot(pooled[gi].astype(BF16), w_ref[gi].astype(BF16), preferred_element_type=F32)
            o_ref[:, pl.ds(gi * GDIM, GDIM)] = (mixed * sc_ref[:, pl.ds(gi * GDIM, GDIM)]).astype(BF16)

    return pl.pallas_call(
        body, name=name, grid=(s // ts,),
        in_specs=[pl.BlockSpec((ts, BR), lambda i: (i, 0)),
                  pl.BlockSpec((POOL_HALO, BR), lambda i: (jnp.maximum(i * hb - 1, 0), 0)),
                  pl.BlockSpec((4, GDIM, GDIM), lambda i: (0, 0, 0)), pl.BlockSpec((1, BR), lambda i: (0, 0))],
        out_specs=pl.BlockSpec((ts, BR), lambda i: (i, 0)), out_shape=jax.ShapeDtypeStruct((s, BR), BF16),
        scratch_shapes=[pltpu.VMEM((POOL_HALO + ts, BR), F32)], compiler_params=_params(("parallel",)))(
            za, za, pool_w, pool_scale)


def _pool_bwd(dact, za, pool_w, pool_scale, name):
    s = za.shape[0]
    ts = _tile(s, 512)
    hb = ts // POOL_HALO
    n_t = s // ts

    def body(d_ref, dhalo_ref, a_ref, halo_ref, w_ref, sc_ref, dz_ref, dw_ref, dsc_ref, ext_ref, f_ref):
        i = pl.program_id(0)
        first = i == 0
        ext_ref[pl.ds(0, POOL_HALO), :] = jnp.where(i > 0, halo_ref[...], 0.0)
        ext_ref[pl.ds(POOL_HALO, ts), :] = a_ref[...]
        pooled = _pool_windows(ext_ref, ts, i * ts)
        d_tile = d_ref[...]
        d_next = jnp.where(i < n_t - 1, dhalo_ref[...], 0.0)
        t_ext = i * ts + lax.broadcasted_iota(jnp.int32, (ts + POOL_HALO, GDIM), 0)
        dsc = []
        for gi, w in enumerate(POOL_WINDOWS):
            cols = pl.ds(gi * GDIM, GDIM)
            wg = w_ref[gi].astype(BF16)
            sc = sc_ref[:, cols]
            pg = pooled[gi].astype(BF16)
            mixed = jnp.dot(pg, wg, preferred_element_type=F32)
            dsc.append(jnp.sum(d_tile[:, gi * GDIM:(gi + 1) * GDIM] * mixed, axis=0, keepdims=True))
            dmix = jnp.concatenate([d_tile[:, gi * GDIM:(gi + 1) * GDIM], d_next[:, gi * GDIM:(gi + 1) * GDIM]], axis=0) * sc
            dmix = dmix.astype(BF16)
            dwg = lax.dot_general(pg, dmix[:ts], (((0,), (0,)), ((), ())), preferred_element_type=F32)

            @pl.when(first)
            def _():
                dw_ref[gi] = dwg

            @pl.when(jnp.logical_not(first))
            def _():
                dw_ref[gi] += dwg

            dpool = lax.dot_general(dmix, wg, (((1,), (1,)), ((), ())), preferred_element_type=F32)
            f_ref[:, cols] = dpool / jnp.minimum(t_ext + 1, w).astype(F32)
            acc = f_ref[pl.ds(0, ts), cols]
            for k in range(1, w):
                acc = acc + f_ref[pl.ds(k, ts), cols]
            dz_ref[:, cols] = (acc - dpool[:ts]).astype(BF16)
        dsc_all = jnp.concatenate(dsc, axis=1)

        @pl.when(first)
        def _():
            dsc_ref[...] = dsc_all

        @pl.when(jnp.logical_not(first))
        def _():
            dsc_ref[...] += dsc_all

    n_hb = s // POOL_HALO
    return pl.pallas_call(
        body, name=name, grid=(n_t,),
        in_specs=[pl.BlockSpec((ts, BR), lambda i: (i, 0)),
                  pl.BlockSpec((POOL_HALO, BR), lambda i: (jnp.minimum((i + 1) * hb, n_hb - 1), 0)),
                  pl.BlockSpec((ts, BR), lambda i: (i, 0)),
                  pl.BlockSpec((POOL_HALO, BR), lambda i: (jnp.maximum(i * hb - 1, 0), 0)),
                  pl.BlockSpec((4, GDIM, GDIM), lambda i: (0, 0, 0)), pl.BlockSpec((1, BR), lambda i: (0, 0))],
        out_specs=[pl.BlockSpec((ts, BR), lambda i: (i, 0)), pl.BlockSpec((4, GDIM, GDIM), lambda i: (0, 0, 0)),
                   pl.BlockSpec((1, BR), lambda i: (0, 0))],
        out_shape=[jax.ShapeDtypeStruct((s, BR), BF16), jax.ShapeDtypeStruct((4, GDIM, GDIM), F32),
                   jax.ShapeDtypeStruct((1, BR), F32)],
        scratch_shapes=[pltpu.VMEM((POOL_HALO + ts, BR), F32), pltpu.VMEM((ts + POOL_HALO, BR), F32)],
        compiler_params=_params(("arbitrary",)))(dact, dact, za, za, pool_w, pool_scale)


CONV_HALO = 32
CONV_LEAD = CONV_HALO - (CONV_WIDTH - 1)


def _conv_fwd(za, conv_w, conv_b, ng, nb, name):
    s = za.shape[0]
    ts = _tile(s, 512)
    hb = ts // CONV_HALO

    def body(a_ref, g_ref, ah_ref, gh_ref, w_ref, b_ref, ng_ref, nb_ref, yc_ref, act_ref, ext_ref):
        i = pl.program_id(0)
        ext_ref[pl.ds(0, CONV_HALO), :] = jnp.where(i > 0, ah_ref[...] * _sigmoid(gh_ref[...]), 0.0)
        ext_ref[pl.ds(CONV_HALO, ts), :] = a_ref[...] * _sigmoid(g_ref[...])
        acc = jnp.zeros((ts, BR), F32) + b_ref[...]
        for k in range(CONV_WIDTH):
            acc = acc + w_ref[pl.ds(k, 1), :] * ext_ref[pl.ds(CONV_LEAD + k, ts), :]
        yc_ref[...] = acc
        xh, _ = _ln_stats(acc)
        ln = xh * ng_ref[...] + nb_ref[...]
        act_ref[...] = (ln * _sigmoid(ln)).astype(BF16)

    tile = lambda c: pl.BlockSpec((ts, BR), lambda i: (i, c))
    halo = lambda c: pl.BlockSpec((CONV_HALO, BR), lambda i: (jnp.maximum(i * hb - 1, 0), c))
    vec = pl.BlockSpec((1, BR), lambda i: (0, 0))
    return pl.pallas_call(
        body, name=name, grid=(s // ts,),
        in_specs=[tile(1), tile(2), halo(1), halo(2), pl.BlockSpec((CONV_WIDTH, BR), lambda i: (0, 0)), vec, vec, vec],
        out_specs=[pl.BlockSpec((ts, BR), lambda i: (i, 0))] * 2,
        out_shape=[jax.ShapeDtypeStruct((s, BR), F32), jax.ShapeDtypeStruct((s, BR), BF16)],
        scratch_shapes=[pltpu.VMEM((CONV_HALO + ts, BR), F32)], compiler_params=_params(("parallel",)))(
            za, za, za, za, conv_w, conv_b, ng, nb)


def _conv_bwd_norm(dact, yc, ng, nb, name):
    s = yc.shape[0]
    ts = _tile(s, 512)

    def body(d_ref, y_ref, ng_ref, nb_ref, dy_ref, db_ref, dng_ref, dnb_ref):
        first = pl.program_id(0) == 0
        xh, r = _ln_stats(y_ref[...])
        g = ng_ref[...]
        ln = xh * g + nb_ref[...]
        sg = _sigmoid(ln)
        dln = d_ref[...] * sg * (1.0 + ln * (1.0 - sg))
        dy = _ln_bwd(xh, r, g, dln)
        dy_ref[...] = dy
        _colsum_into(db_ref, dy, first)
        _colsum_into(dng_ref, dln * xh, first)
        _colsum_into(dnb_ref, dln, first)

    vec = pl.BlockSpec((1, BR), lambda i: (0, 0))
    vshape = jax.ShapeDtypeStruct((1, BR), F32)
    return pl.pallas_call(
        body, name=name, grid=(s // ts,),
        in_specs=[pl.BlockSpec((ts, BR), lambda i: (i, 1)), pl.BlockSpec((ts, BR), lambda i: (i, 0)), vec, vec],
        out_specs=[pl.BlockSpec((ts, BR), lambda i: (i, 0)), vec, vec, vec],
        out_shape=[jax.ShapeDtypeStruct((s, BR), F32), vshape, vshape, vshape],
        compiler_params=_params(("arbitrary",)))(dact, yc, ng, nb)


def _conv_bwd_taps(dyc, za, conv_w, name):
    s = za.shape[0]
    ts = _tile(s, 512)
    hb = ts // CONV_HALO
    n_t = s // ts
    n_hb = s // CONV_HALO

    def body(d_ref, dh_ref, a_ref, g_ref, ah_ref, gh_ref, w_ref, dz_ref, dw_ref, ext_ref, f_ref):
        i = pl.program_id(0)
        first = i == 0
        a, sg = a_ref[...], _sigmoid(g_ref[...])
        ext_ref[pl.ds(0, CONV_HALO), :] = jnp.where(i > 0, ah_ref[...] * _sigmoid(gh_ref[...]), 0.0)
        ext_ref[pl.ds(CONV_HALO, ts), :] = a * sg
        d = d_ref[...]
        f_ref[pl.ds(0, ts), :] = d
        f_ref[pl.ds(ts, CONV_HALO), :] = jnp.where(i < n_t - 1, dh_ref[...], 0.0)
        dglu = jnp.zeros((ts, BR), F32)
        rows = []
        for k in range(CONV_WIDTH):
            rows.append(jnp.sum(d * ext_ref[pl.ds(CONV_LEAD + k, ts), :], axis=0, keepdims=True))
            dglu = dglu + w_ref[pl.ds(k, 1), :] * f_ref[pl.ds(CONV_WIDTH - 1 - k, ts), :]
        rows.append(jnp.zeros((1, BR), F32))
        dw = jnp.concatenate(rows, axis=0)

        @pl.when(first)
        def _():
            dw_ref[...] = dw

        @pl.when(jnp.logical_not(first))
        def _():
            dw_ref[...] += dw

        dz_ref[:, pl.ds(0, BR)] = (dglu * sg).astype(BF16)
        dz_ref[:, pl.ds(BR, BR)] = (dglu * a * sg * (1.0 - sg)).astype(BF16)

    tile = lambda c: pl.BlockSpec((ts, BR), lambda i: (i, c))
    halo = lambda c: pl.BlockSpec((CONV_HALO, BR), lambda i: (jnp.maximum(i * hb - 1, 0), c))
    return pl.pallas_call(
        body, name=name, grid=(n_t,),
        in_specs=[pl.BlockSpec((ts, BR), lambda i: (i, 0)),
                  pl.BlockSpec((CONV_HALO, BR), lambda i: (jnp.minimum((i + 1) * hb, n_hb - 1), 0)),
                  tile(1), tile(2), halo(1), halo(2), pl.BlockSpec((CONV_WIDTH, BR), lambda i: (0, 0))],
        out_specs=[pl.BlockSpec((ts, 2 * BR), lambda i: (i, 0)), pl.BlockSpec((CONV_WIDTH + 1, BR), lambda i: (0, 0))],
        out_shape=[jax.ShapeDtypeStruct((s, 2 * BR), BF16), jax.ShapeDtypeStruct((CONV_WIDTH + 1, BR), F32)],
        scratch_shapes=[pltpu.VMEM((CONV_HALO + ts, BR), F32), pltpu.VMEM((ts + CONV_HALO, BR), F32)],
        compiler_params=_params(("arbitrary",)))(dyc, dyc, za, za, za, za, conv_w)


def _tril(w):
    r = lax.broadcasted_iota(jnp.int32, (CHUNK, CHUNK), 0)
    c = lax.broadcasted_iota(jnp.int32, (CHUNK, CHUNK), 1)
    return jnp.where(c <= r, w, 0.0)


def _sgu_fwd(za, ng, nb, sgu_w, bias_b, name):
    s = za.shape[0]
    ts = _tile(s, 512)

    def body(u_ref, v_ref, ng_ref, nb_ref, w_ref, b_ref, o_ref):
        u = _gelu(u_ref[...])
        xh, _ = _ln_stats(_gelu(v_ref[...]))
        vln = (xh * ng_ref[...] + nb_ref[...]).astype(BF16)
        for gi in range(4):
            wg = _tril(w_ref[gi]).astype(BF16)
            for n in range(ts // CHUNK):
                blk = vln[n * CHUNK:(n + 1) * CHUNK, gi * GDIM:(gi + 1) * GDIM]
                sp = jnp.dot(wg, blk, preferred_element_type=F32) + b_ref[gi]
                o_ref[pl.ds(n * CHUNK, CHUNK), pl.ds(gi * GDIM, GDIM)] = (
                    u[n * CHUNK:(n + 1) * CHUNK, gi * GDIM:(gi + 1) * GDIM] * sp).astype(BF16)

    vec = pl.BlockSpec((1, BR), lambda i: (0, 0))
    cube = pl.BlockSpec((4, CHUNK, GDIM), lambda i: (0, 0, 0))
    return pl.pallas_call(
        body, name=name, grid=(s // ts,),
        in_specs=[pl.BlockSpec((ts, BR), lambda i: (i, 3)), pl.BlockSpec((ts, BR), lambda i: (i, 4)), vec, vec, cube, cube],
        out_specs=pl.BlockSpec((ts, BR), lambda i: (i, 0)), out_shape=jax.ShapeDtypeStruct((s, BR), BF16),
        compiler_params=_params(("parallel",)))(za, za, ng, nb, sgu_w, bias_b)


def _sgu_bwd(dact, za, ng, nb, sgu_w, bias_b, name):
    s = za.shape[0]
    ts = _tile(s, 512)

    def body(d_ref, u_ref, v_ref, ng_ref, nb_ref, w_ref, b_ref, dz_ref, dw_ref, db_ref, dng_ref, dnb_ref, dv_ref):
        first = pl.program_id(0) == 0
        u_raw, v_raw = u_ref[...], v_ref[...]
        u = _gelu(u_raw)
        xh, r = _ln_stats(_gelu(v_raw))
        g = ng_ref[...]
        vln = (xh * g + nb_ref[...]).astype(BF16)
        d = d_ref[...]
        dsp = d * u
        dsp16 = dsp.astype(BF16)
        for gi in range(4):
            wg = _tril(w_ref[gi]).astype(BF16)
            dwg = jnp.zeros((CHUNK, CHUNK), F32)
            dbg = jnp.zeros((CHUNK, 1), F32)
            for n in range(ts // CHUNK):
                rows, cols = slice(n * CHUNK, (n + 1) * CHUNK), slice(gi * GDIM, (gi + 1) * GDIM)
                blk = vln[rows, cols]
                sp = jnp.dot(wg, blk, preferred_element_type=F32) + b_ref[gi]
                dz_ref[pl.ds(n * CHUNK, CHUNK), pl.ds(gi * GDIM, GDIM)] = (
                    d[rows, cols] * sp * _gelu_grad(u_raw[rows, cols])).astype(BF16)
                dv_ref[pl.ds(n * CHUNK, CHUNK), pl.ds(gi * GDIM, GDIM)] = lax.dot_general(
                    wg, dsp16[rows, cols], (((0,), (0,)), ((), ())), preferred_element_type=F32)
                dwg = dwg + lax.dot_general(dsp16[rows, cols], blk, (((1,), (1,)), ((), ())), preferred_element_type=F32)
                dbg = dbg + jnp.sum(dsp[rows, cols], axis=1, keepdims=True)
            dwg = _tril(dwg)

            @pl.when(first)
            def _():
                dw_ref[gi] = dwg
                db_ref[gi] = dbg

            @pl.when(jnp.logical_not(first))
            def _():
                dw_ref[gi] += dwg
                db_ref[gi] += dbg

        dvln = dv_ref[...]
        dz_ref[:, pl.ds(BR, BR)] = (_ln_bwd(xh, r, g, dvln) * _gelu_grad(v_raw)).astype(BF16)
        _colsum_into(dng_ref, dvln * xh, first)
        _colsum_into(dnb_ref, dvln, first)

    vec = pl.BlockSpec((1, BR), lambda i: (0, 0))
    cube = pl.BlockSpec((4, CHUNK, GDIM), lambda i: (0, 0, 0))
    vshape = jax.ShapeDtypeStruct((1, BR), F32)
    return pl.pallas_call(
        body, name=name, grid=(s // ts,),
        in_specs=[pl.BlockSpec((ts, BR), lambda i: (i, 2)), pl.BlockSpec((ts, BR), lambda i: (i, 3)),
                  pl.BlockSpec((ts, BR), lambda i: (i, 4)), vec, vec, cube, cube],
        out_specs=[pl.BlockSpec((ts, 2 * BR), lambda i: (i, 0)), cube, pl.BlockSpec((4, CHUNK, 1), lambda i: (0, 0, 0)),
                   vec, vec],
        out_shape=[jax.ShapeDtypeStruct((s, 2 * BR), BF16), jax.ShapeDtypeStruct((4, CHUNK, CHUNK), F32),
                   jax.ShapeDtypeStruct((4, CHUNK, 1), F32), vshape, vshape],
        scratch_shapes=[pltpu.VMEM((ts, BR), F32)], compiler_params=_params(("arbitrary",)))(
            dact, za, za, ng, nb, sgu_w, bias_b)


KR_BLOCK = 3584 // 128


def _mla_prep(za, qg, kvg, ck, sk, name):
    s = za.shape[0]
    ts = _tile(s, 512)

    def body(cq_ref, ckv_ref, kr_ref, qg_ref, kvg_ref, c_ref, s_ref, qn_ref, kvn_ref, krr_ref):
        qn_ref[...] = _rms(cq_ref[...], qg_ref[...]).astype(BF16)
        kvn_ref[...] = _rms(ckv_ref[...], kvg_ref[...]).astype(BF16)
        kr = kr_ref[...]
        krr_ref[...] = (kr * c_ref[...] + _rot_half(kr, 128, 0) * s_ref[...]).astype(BF16)

    vec = pl.BlockSpec((1, BR), lambda i: (0, 0))
    tab = pl.BlockSpec((ts, 128), lambda i: (i, 0))
    row = pl.BlockSpec((ts, BR), lambda i: (i, 0))
    return pl.pallas_call(
        body, name=name, grid=(s // ts,),
        in_specs=[pl.BlockSpec((ts, BR), lambda i: (i, 5)), pl.BlockSpec((ts, BR), lambda i: (i, 6)),
                  pl.BlockSpec((ts, 128), lambda i: (i, KR_BLOCK)), vec, vec, tab, tab],
        out_specs=[row, row, tab],
        out_shape=[jax.ShapeDtypeStruct((s, BR), BF16), jax.ShapeDtypeStruct((s, BR), BF16),
                   jax.ShapeDtypeStruct((s, 128), BF16)],
        compiler_params=_params(("parallel",)))(za, za, za, qg, kvg, ck, sk)


def _mla_prep_bwd(dqn, dkvn, dkr_heads, za, qg, kvg, ck, sk, name):
    s = za.shape[0]
    ts = _tile(s, 512)

    def body(dq_ref, dkv_ref, dkr_ref, cq_ref, ckv_ref, qg_ref, kvg_ref, c_ref, s_ref, dz_ref, dqg_ref, dkvg_ref):
        first = pl.program_id(0) == 0
        dcq, rows_q = _rms_bwd(cq_ref[...], qg_ref[...], dq_ref[...])
        dckv, rows_kv = _rms_bwd(ckv_ref[...], kvg_ref[...], dkv_ref[...])
        dz_ref[:, pl.ds(0, BR)] = dcq.astype(BF16)
        dz_ref[:, pl.ds(BR, BR)] = dckv.astype(BF16)
        dk = dkr_ref[:, pl.ds(0, 128)]
        for h in range(1, MLA_HEADS):
            dk = dk + dkr_ref[:, pl.ds(h * 128, 128)]
        dz_ref[:, pl.ds(2 * BR, 128)] = (dk * c_ref[...] - _rot_half(dk, 128, 0) * s_ref[...]).astype(BF16)
        _colsum_into(dqg_ref, rows_q, first)
        _colsum_into(dkvg_ref, rows_kv, first)

    vec = pl.BlockSpec((1, BR), lambda i: (0, 0))
    tab = pl.BlockSpec((ts, 128), lambda i: (i, 0))
    row = pl.BlockSpec((ts, BR), lambda i: (i, 0))
    wide = 2 * BR + 128
    vshape = jax.ShapeDtypeStruct((1, BR), F32)
    return pl.pallas_call(
        body, name=name, grid=(s // ts,),
        in_specs=[row, row, pl.BlockSpec((ts, MLA_HEADS * 128), lambda i: (i, 0)),
                  pl.BlockSpec((ts, BR), lambda i: (i, 5)), pl.BlockSpec((ts, BR), lambda i: (i, 6)), vec, vec, tab, tab],
        out_specs=[pl.BlockSpec((ts, wide), lambda i: (i, 0)), vec, vec],
        out_shape=[jax.ShapeDtypeStruct((s, wide), BF16), vshape, vshape],
        compiler_params=_params(("arbitrary",)))(dqn, dkvn, dkr_heads, za, za, qg, kvg, ck, sk)


def _attn_tiles(s):
    tq, tk = _tile(s, 1024), _tile(s, 512)
    return tq, tk, tq // tk


def _causal(qi, ki, tq, tk):
    row = qi * tq + lax.broadcasted_iota(jnp.int32, (tq, tk), 0)
    col = ki * tk + lax.broadcasted_iota(jnp.int32, (tq, tk), 1)
    return col <= row


def _on_tiles(qi, ki, r, step):
    pl.when(ki < qi * r)(functools.partial(step, False))
    pl.when(jnp.logical_and(ki >= qi * r, ki < (qi + 1) * r))(functools.partial(step, True))


def _flash_fwd(q, kv, krr, name, plan=None):
    s = q.shape[0]
    tq, tk, r = _attn_tiles(s)
    nq, nk = s // tq, s // tk

    def body(q_ref, kn_ref, kr_ref, v_ref, o_ref, lse_ref, m_sc, l_sc, acc_sc):
        qi, ki = pl.program_id(1), pl.program_id(2)

        @pl.when(ki == 0)
        def _():
            m_sc[...] = jnp.full((tq, 1), NEG, F32)
            l_sc[...] = jnp.zeros((tq, 1), F32)
            acc_sc[...] = jnp.zeros((tq, V_DIM), F32)

        def step(masked):
            k = jnp.concatenate([kn_ref[...], kr_ref[...]], axis=1)
            sc = lax.dot_general(q_ref[...], k, (((1,), (1,)), ((), ())), preferred_element_type=F32)
            if masked:
                sc = jnp.where(_causal(qi, ki, tq, tk), sc, NEG)
            m_prev = m_sc[...]
            m_new = jnp.maximum(m_prev, jnp.max(sc, axis=1, keepdims=True))
            alpha = jnp.exp(m_prev - m_new)
            p = jnp.exp(sc - m_new)
            l_sc[...] = alpha * l_sc[...] + jnp.sum(p, axis=1, keepdims=True)
            acc_sc[...] = alpha * acc_sc[...] + jnp.dot(p.astype(BF16), v_ref[...], preferred_element_type=F32)
            m_sc[...] = m_new

        _on_tiles(qi, ki, r, step)

        @pl.when(ki == nk - 1)
        def _():
            o_ref[...] = (acc_sc[...] / l_sc[...]).astype(BF16)
            lse_ref[...] = jnp.broadcast_to(m_sc[...] + jnp.log(l_sc[...]), (tq, 128))

    kmap = lambda c: (lambda h, qi, ki: (jnp.minimum(ki, (qi + 1) * r - 1), c(h)))
    out_blk = pl.BlockSpec((tq, 128), lambda h, qi, ki: (qi, h))
    return _pcall(
        body, (q, kv, krr, kv), name=name, grid=(MLA_HEADS, nq, nk),
        in_specs=[pl.BlockSpec((tq, QW), lambda h, qi, ki: (qi, h)), pl.BlockSpec((tk, 128), kmap(lambda h: 2 * h)),
                  pl.BlockSpec((tk, 128), kmap(lambda h: 0)), pl.BlockSpec((tk, 128), kmap(lambda h: 2 * h + 1))],
        out_specs=[out_blk, out_blk],
        out_shape=[jax.ShapeDtypeStruct((s, MLA_HEADS * V_DIM), BF16), jax.ShapeDtypeStruct((s, MLA_HEADS * 128), F32)],
        scratch_shapes=[pltpu.VMEM((tq, 1), F32), pltpu.VMEM((tq, 1), F32), pltpu.VMEM((tq, V_DIM), F32)],
        sem=("parallel", "parallel", "arbitrary"), vmem=VMEM_BIG, plan=plan)


DO_BLOCK = 3 * BR // 128


def _flash_probs(q_ref, kn_ref, kr_ref, v_ref, do_ref, o_ref, lse_ref, qi, ki, tq, tk, masked):
    k = jnp.concatenate([kn_ref[...], kr_ref[...]], axis=1)
    q = q_ref[...]
    sc = lax.dot_general(q, k, (((1,), (1,)), ((), ())), preferred_element_type=F32)
    lse = jnp.max(lse_ref[...], axis=1, keepdims=True)
    p = jnp.exp(sc - lse)
    if masked:
        p = jnp.where(_causal(qi, ki, tq, tk), p, 0.0)
    do = do_ref[...]
    delta = jnp.sum(do * o_ref[...].astype(F32), axis=1, keepdims=True)
    do = do.astype(BF16)
    dp = lax.dot_general(do, v_ref[...], (((1,), (1,)), ((), ())), preferred_element_type=F32)
    ds = (p * (dp - delta)).astype(BF16)
    return q, k, p, do, ds


def _flash_bwd(q, kv, krr, dact, o, lse, cq, sq, name, plan=None):
    s = q.shape[0]
    tq, tk, r = _attn_tiles(s)
    nq, nk = s // tq, s // tk

    def body(q_ref, kn_ref, kr_ref, v_ref, do_ref, o_ref, lse_ref, c_ref, s_ref, dkv_ref, dkr_ref, dq_ref,
             dk_sc, dv_sc, dq_sc):
        ki, qi = pl.program_id(1), pl.program_id(2)
        rows = pl.ds(pl.multiple_of(qi * tq, tq), tq)

        @pl.when(qi == 0)
        def _():
            dk_sc[...] = jnp.zeros((tk, QW), F32)
            dv_sc[...] = jnp.zeros((tk, V_DIM), F32)

        @pl.when(ki == 0)
        def _():
            dq_sc[rows, :] = jnp.zeros((tq, QW), F32)

        def step(masked):
            qv, k, p, dov, ds = _flash_probs(q_ref, kn_ref, kr_ref, v_ref, do_ref, o_ref, lse_ref, qi, ki, tq, tk, masked)
            dv_sc[...] += lax.dot_general(p.astype(BF16), dov, (((0,), (0,)), ((), ())), preferred_element_type=F32)
            dk_sc[...] += lax.dot_general(ds, qv, (((0,), (0,)), ((), ())), preferred_element_type=F32)
            dq_sc[rows, :] += jnp.dot(ds, k, preferred_element_type=F32)

        _on_tiles(qi, ki, r, step)

        @pl.when(qi == nq - 1)
        def _():
            dkv_ref[:, pl.ds(0, 128)] = dk_sc[:, pl.ds(0, 128)].astype(BF16)
            dkv_ref[:, pl.ds(128, 128)] = dv_sc[...].astype(BF16)
            dkr_ref[...] = dk_sc[:, pl.ds(128, 128)]

        @pl.when(ki == nk - 1)
        def _():
            dq = dq_sc[rows, :] * ATT_SCALE
            dq_ref[...] = (dq * c_ref[...] - _rot_half(dq, QW, QK_NOPE) * s_ref[...]).astype(BF16)

    qmap = lambda c: (lambda h, ki, qi: (jnp.maximum(qi, ki // r), c(h)))
    kmap = lambda c: (lambda h, ki, qi: (ki, c(h)))
    last = lambda c: (lambda h, ki, qi: (jnp.where(ki == nk - 1, qi, 0), c(h)))
    return _pcall(
        body, (q, kv, krr, kv, dact, o, lse, cq, sq), name=name, grid=(MLA_HEADS, nk, nq),
        in_specs=[pl.BlockSpec((tq, QW), qmap(lambda h: h)), pl.BlockSpec((tk, 128), kmap(lambda h: 2 * h)),
                  pl.BlockSpec((tk, 128), kmap(lambda h: 0)), pl.BlockSpec((tk, 128), kmap(lambda h: 2 * h + 1)),
                  pl.BlockSpec((tq, 128), qmap(lambda h: DO_BLOCK + h)), pl.BlockSpec((tq, 128), qmap(lambda h: h)),
                  pl.BlockSpec((tq, 128), qmap(lambda h: h)), pl.BlockSpec((tq, QW), last(lambda h: 0)),
                  pl.BlockSpec((tq, QW), last(lambda h: 0))],
        out_specs=[pl.BlockSpec((tk, QW), kmap(lambda h: h)), pl.BlockSpec((tk, 128), kmap(lambda h: h)),
                   pl.BlockSpec((tq, QW), last(lambda h: h))],
        out_shape=[jax.ShapeDtypeStruct((s, MLA_HEADS * QW), BF16), jax.ShapeDtypeStruct((s, MLA_HEADS * 128), F32),
                   jax.ShapeDtypeStruct((s, MLA_HEADS * QW), BF16)],
        scratch_shapes=[pltpu.VMEM((tk, QW), F32), pltpu.VMEM((tk, V_DIM), F32), pltpu.VMEM((s, QW), F32)],
        sem=("parallel", "arbitrary", "arbitrary"), vmem=VMEM_BIG, plan=plan)


def _rope_tables(positions):
    inv_freq = ROPE_THETA ** (-jnp.arange(0, QK_ROPE, 2, dtype=F32) / QK_ROPE)
    ang = positions.reshape(-1).astype(F32)[:, None] * inv_freq
    cos, sin = jnp.cos(ang), jnp.sin(ang)
    s = cos.shape[0]
    one, zero = jnp.ones((s, 64), F32), jnp.zeros((s, 64), F32)
    ck = jnp.concatenate([cos, cos, one], axis=1)
    sk = jnp.concatenate([sin, sin, zero], axis=1)
    cq = jnp.concatenate([one, one, ck], axis=1)
    sq = jnp.concatenate([zero, zero, sk], axis=1)
    return ck, sk, cq, sq


def _cols_full(gathered):
    _, r, c = gathered.shape
    return gathered.transpose(1, 0, 2).reshape(r, N_DEV * c)


def _cols_by_owner(full):
    r, n = full.shape
    return full.reshape(r, N_DEV, n // N_DEV).transpose(1, 0, 2)


def _layer_weights(gat, small, l):
    w_in = _cols_full(gat[("w_in", l)])
    w = {
        "gat": gat, "layer": l,
        "w_a": jnp.concatenate([w_in[:, :N_IN_A], jnp.zeros((D_MODEL, ZA - N_IN_A), BF16)], axis=1),
        "w_g": w_in[:, N_IN_A:],
        "conv_w": small["conv_w"][l],
        "pool_w": small["pool_w"][l], "sgu_w": small["sgu_w"][l],
        "sgu_bias": jnp.broadcast_to(small["sgu_b"][l][:, :, None], (4, CHUNK, GDIM)),
    }
    for name in ("pre_mix_g", "pool_scale", "conv_b", "conv_norm_g", "conv_norm_b", "sgu_norm_g", "sgu_norm_b",
                 "q_norm_g", "kv_norm_g", "post_mix_g", "pre_mlp_g", "post_mlp_g"):
        w[name] = small[name][l][None, :]
    return w


def _late(w, name):
    if name not in w:
        gat, l = w["gat"], w["layer"]
        if name == "proj_cat":
            w[name] = jnp.concatenate([_cols_full(gat[(n, l)]) for n in ("pool_proj", "conv_proj", "sgu_proj", "attn_proj")],
                                      axis=0)
        elif name == "w_uq":
            w[name] = jnp.pad(gat[("w_uq", l)].transpose(1, 0, 2),
                              ((0, 0), (0, 0), (0, QW - QK_NOPE - QK_ROPE))).reshape(BR, MLA_HEADS * QW)
        elif name == "w_ukv":
            w[name] = _cols_full(gat[("w_ukv", l)])
        elif name == "w_up":
            w[name] = gat[("w_up", l)]
        else:
            arr = gat[(name, l)]
            w[name] = arr.reshape(N_DEV * arr.shape[1], arr.shape[2])
    return w[name]


def _layer_fwd(x, h1, w, tabs, l, plan):
    ck, sk, cq, sq = tabs
    n = f"l{l}_"
    za = _mm(h1, w["w_a"], mode="nn", name=n + "za", plan=plan)
    zg = _mm(h1, w["w_g"], mode="nn", name=n + "zg", plan=plan)
    a_pool = _pool_fwd(za, w["pool_w"], w["pool_scale"], n + "pool")
    yc, a_conv = _conv_fwd(za, w["conv_w"], w["conv_b"], w["conv_norm_g"], w["conv_norm_b"], n + "conv")
    a_sgu = _sgu_fwd(za, w["sgu_norm_g"], w["sgu_norm_b"], w["sgu_w"], w["sgu_bias"], n + "sgu")
    qn, kvn, krr = _mla_prep(za, w["q_norm_g"], w["kv_norm_g"], ck, sk, n + "mla_prep")
    q = _mm(qn, _late(w, "w_uq"), mode="nn", name=n + "q", out_dtypes=(BF16,), extras=(cq, sq), tn=QW,
            epilogue=lambda acc, c, sn: ((acc * c + _rot_half(acc, QW, QK_NOPE) * sn) * ATT_SCALE,))
    kv = _mm(kvn, _late(w, "w_ukv"), mode="nn", name=n + "kv", out_dtypes=(BF16,))
    o, lse = _flash_fwd(q, kv, krr, n + "flash", plan=plan)
    act_cat = jnp.concatenate([a_pool, a_conv, a_sgu, o], axis=1)
    y, merged = _proj_merge_fwd(act_cat, _late(w, "proj_cat"), zg, n + "proj", plan=plan)
    m2 = _mm(merged, _late(w, "w_out"), mode="nn", name=n + "out")
    x1, h2 = _post_pre(x, m2, w["post_mix_g"], w["pre_mlp_g"], n + "post_mix")
    up, act = _mm(h2, _late(w, "w_up"), mode="nn", name=n + "up", out_dtypes=(F32, BF16), plan=plan,
                  mnk=(x.shape[0], D_FF, D_MODEL), tn=UP_SHARD,
                  b_spec=lambda tn, tk: pl.BlockSpec((None, tk, tn), lambda i, j, kk: (j, kk, 0)),
                  epilogue=lambda acc: (acc, jnp.square(jnp.maximum(acc, 0.0))))
    f = _mm(act, _late(w, "w_down"), mode="nn", name=n + "down", plan=plan)
    saved = dict(x=x, h1=h1, za=za, zg=zg, yc=yc, qn=qn, kvn=kvn, krr=krr, q=q, kv=kv, o=o, lse=lse, act_cat=act_cat,
                 y=y, merged=merged, m2=m2, x1=x1, h2=h2, up=up, act=act, f=f)
    return x1, f, saved


def _layer_bwd(dx_out, df, sv, w, tabs, l, prev, plan, rs):
    ck, sk, cq, sq = tabs
    n = f"l{l}_b_"
    g = {}
    own, half = {}, {}

    def to_sibling(kernel, names):
        def done(results):
            for name, t in zip(names, results[0]):
                half[name] = _rs_chip_sum(own[name], t, f"l{l}_chip_sum_{name}")
        plan.at(n + kernel, lambda: [_scatter_sibling([own[name] for name in names])], done)

    def to_chips(kernel, names):
        def done(results):
            for name, t in zip(names, results[0]):
                rs[(name, l)] = (half[name], t)
        plan.at(n + kernel, lambda: [_scatter_chips([half[name] for name in names])], done)

    own["w_down"] = _mm(sv["act"], df, mode="tn", name=n + "dw_down", out_dtypes=(BF16,), plan=plan).reshape(
        N_DEV, D_FF // N_DEV, D_MODEL)
    to_sibling("dup", ["w_down"])
    dup = _mm(df, _late(w, "w_down"), mode="nt", name=n + "dup", out_dtypes=(BF16,), extras=(sv["up"],), plan=plan,
              epilogue=lambda acc, up: (acc * 2.0 * jnp.maximum(up, 0.0),))
    to_chips("dw_up", ["w_down"])
    own["w_up"] = _mm(sv["h2"], dup, mode="tn", name=n + "dw_up", tn=UP_SHARD, plan=plan,
                      o_spec=lambda tm, tn: pl.BlockSpec((None, tm, tn), lambda i, j, kk: (j, i, 0)),
                      out_struct=jax.ShapeDtypeStruct((N_DEV, D_MODEL, UP_SHARD), BF16))
    to_sibling("dh2", ["w_up"])
    dh2 = _mm(dup, _late(w, "w_up"), mode="nt", name=n + "dh2", mnk=(dup.shape[0], D_MODEL, D_FF), tk=UP_SHARD,
              plan=plan, b_spec=lambda tn, tk: pl.BlockSpec((None, tn, tk), lambda i, j, kk: (kk, j, 0)))
    dx1, dm2, g["pre_mlp_g"], g["post_mix_g"] = _pre_bwd(dh2, sv["x1"], w["pre_mlp_g"], dx_out, n + "pre_mlp",
                                                           r_prev=sv["m2"], g_post_prev=w["post_mix_g"])
    own["w_out"] = _mm(sv["merged"], dm2, mode="tn", name=n + "dw_out", out_dtypes=(BF16,)).reshape(
        N_DEV, D_MODEL // N_DEV, D_MODEL)
    to_sibling("dmerged", ["w_out"])
    dmerged = _mm(dm2, _late(w, "w_out"), mode="nt", name=n + "dmerged", plan=plan)
    to_chips("merge", ["w_up"])
    dy, dzg = _merge_bwd(dmerged, sv["y"], sv["zg"], n + "merge", plan=plan)
    d_proj = _proj_bwd_w(sv["act_cat"], dy, n + "dw_proj")
    projs = ["pool_proj", "conv_proj", "sgu_proj", "attn_proj"]
    for i, name in enumerate(projs):
        own[name] = _cols_by_owner(d_proj[i * BR:(i + 1) * BR] if i < 3 else d_proj[3 * BR:])
    to_chips("dact", ["w_out"])
    to_sibling("dact", projs)
    dact = _proj_bwd_act(dy, _late(w, "proj_cat"), n + "dact", plan=plan)
    dz_pool, g["pool_w"], g["pool_scale"] = _pool_bwd(dact, sv["za"], w["pool_w"], w["pool_scale"], n + "pool")
    dyc, g["conv_b"], g["conv_norm_g"], g["conv_norm_b"] = _conv_bwd_norm(dact, sv["yc"], w["conv_norm_g"],
                                                                          w["conv_norm_b"], n + "conv_norm")
    dz_conv, g["conv_w"] = _conv_bwd_taps(dyc, sv["za"], w["conv_w"], n + "conv_taps")
    dz_sgu, g["sgu_w"], g["sgu_b"], g["sgu_norm_g"], g["sgu_norm_b"] = _sgu_bwd(
        dact, sv["za"], w["sgu_norm_g"], w["sgu_norm_b"], w["sgu_w"], w["sgu_bias"], n + "sgu")
    to_chips("flash", projs)
    dkv, dkr_heads, dq = _flash_bwd(sv["q"], sv["kv"], sv["krr"], dact, sv["o"], sv["lse"], cq, sq, n + "flash",
                                    plan=plan)
    d_uq = _mm(sv["qn"], dq, mode="tn", name=n + "dw_uq", out_dtypes=(BF16,))
    own["w_uq"] = d_uq.reshape(BR, MLA_HEADS, QW)[:, :, :QK_NOPE + QK_ROPE].transpose(1, 0, 2)
    dqn = _mm(dq, _late(w, "w_uq"), mode="nt", name=n + "dqn")
    own["w_ukv"] = _cols_by_owner(_mm(sv["kvn"], dkv, mode="tn", name=n + "dw_ukv", out_dtypes=(BF16,)))
    to_sibling("dkvn", ["w_uq", "w_ukv"])
    dkvn = _mm(dkv, _late(w, "w_ukv"), mode="nt", name=n + "dkvn", plan=plan)
    dz_mla, g["q_norm_g"], g["kv_norm_g"] = _mla_prep_bwd(dqn, dkvn, dkr_heads, sv["za"], w["q_norm_g"],
                                                           w["kv_norm_g"], ck, sk, n + "mla_prep")
    dza = jnp.concatenate([dz_pool, dz_conv, dz_sgu, dz_mla], axis=1)
    to_chips("dw_a", ["w_uq", "w_ukv"])
    d_a = _mm(sv["h1"], dza, mode="tn", name=n + "dw_a", out_dtypes=(BF16,), plan=plan)
    d_g = _mm(sv["h1"], dzg, mode="tn", name=n + "dw_g", out_dtypes=(BF16,))
    own["w_in"] = _cols_by_owner(jnp.concatenate([d_a[:, :N_IN_A], d_g], axis=1))
    to_sibling("dh1_a", ["w_in"])
    dh1 = _mm(dza, w["w_a"], mode="nt", name=n + "dh1_a", plan=plan)
    to_chips("dh1_g", ["w_in"])
    dh1 = _mm(dzg, w["w_g"], mode="nt", name=n + "dh1_g", extras=(dh1,), epilogue=lambda acc, e: (acc + e,), plan=plan)
    if prev is None:
        dx, g["pre_mix_g"] = _pre_bwd(dh1, sv["x"], w["pre_mix_g"], dx1, n + "pre_mix")
        return dx, None, g
    dx, df_prev, g["pre_mix_g"], g_prev_post = _pre_bwd(dh1, sv["x"], w["pre_mix_g"], dx1, n + "pre_mix",
                                                         r_prev=prev[0], g_post_prev=prev[1])
    g["prev_post_mlp_g"] = g_prev_post
    return dx, df_prev, g


MIDSIZE = ("pool_proj", "conv_proj", "sgu_proj", "w_uq", "w_ukv", "attn_proj")
GATHER_STEPS = (
    ("l0_za", tuple((name, 0) for name in MIDSIZE)),
    ("l0_zg", (("w_up", 0), ("w_out", 0))),
    ("l0_flash", (("w_down", 0), ("w_in", 1))),
    ("l0_proj", tuple((name, 1) for name in MIDSIZE)),
    ("l0_up", (("w_up", 1), ("w_out", 1))),
    ("l0_down", (("w_down", 1),)),
    ("l1_za", ()),
)


def _plan_gathers(plan, gat, shards):
    first_half = {}
    for step, (kernel, keys) in enumerate(GATHER_STEPS):
        before = GATHER_STEPS[step - 1][1] if step else ()

        def make(keys=keys, before=before):
            ops = [_gather_pass(first_half[before])] if before else []
            return ops + ([_gather_own([shards[key] for key in keys])] if keys else [])

        def done(results, keys=keys, before=before):
            if before:
                gat.update(zip(before, results[0]))
            if keys:
                first_half[keys] = results[-1]

        plan.at(kernel, make, done)


def _local_step(x, positions, target, gat, small, shards, plan):
    tabs = _rope_tables(positions)
    _plan_gathers(plan, gat, shards)
    ws = [_layer_weights(gat, small, 0)]
    saved = []
    h = _pre_norm(x, ws[0]["pre_mix_g"], "l0_pre_mix")
    cur = x
    for l in range(DEPTH):
        x1, f, sv = _layer_fwd(cur, h, ws[l], tabs, l, plan)
        saved.append(sv)
        if l + 1 < DEPTH:
            ws.append(_layer_weights(gat, small, l + 1))
            cur, h = _post_pre(x1, f, ws[l]["post_mlp_g"], ws[l + 1]["pre_mix_g"], f"l{l}_post_mlp")
    top = DEPTH - 1
    dx, df, dg_post, loss = _final_loss(saved[top]["x1"], saved[top]["f"], ws[top]["post_mlp_g"], target, "loss")
    grads = [None] * DEPTH
    post_mlp = {top: dg_post}
    rs = {}
    for l in range(top, -1, -1):
        prev = (saved[l - 1]["f"], ws[l - 1]["post_mlp_g"]) if l > 0 else None
        dx, df, g = _layer_bwd(dx, df, saved[l], ws[l], tabs, l, prev, plan, rs)
        if l > 0:
            post_mlp[l - 1] = g.pop("prev_post_mlp_g")
        grads[l] = g
    for l in range(DEPTH):
        grads[l]["post_mlp_g"] = post_mlp[l]
    assert not plan.jobs, sorted(plan.jobs)
    return loss[0, 0], dx, grads, rs


def _small_grads(grads):
    small = {}

    def stack(fn):
        return jnp.stack([fn(grads[l]) for l in range(DEPTH)])

    for name, shape in SMALL:
        if name == "sgu_b":
            small[name] = stack(lambda g: g["sgu_b"][:, :, 0])
        else:
            small[name] = stack(lambda g, name=name, shape=shape: g[name].reshape(shape))
    small["conv_w"] = stack(lambda g: g["conv_w"][:CONV_WIDTH])
    return small


SMALL_ROWS = sum(DEPTH * math.prod(shape) // 128 for _, shape in SMALL)
CONVW_ROWS = DEPTH * CONV_WIDTH * BR // 128


def _pack_small(parts):
    return jnp.concatenate([parts[name].astype(F32).reshape(-1, 128) for name, _ in SMALL], axis=0)


def _unpack_small(buf):
    out, off = {}, 0
    for name, shape in SMALL:
        rows = DEPTH * math.prod(shape) // 128
        out[name] = buf[off:off + rows].reshape((DEPTH,) + shape)
        off += rows
    return out


def _mesh_pos():
    return lax.axis_index("x"), lax.axis_index("y"), lax.axis_index("c")


def _all_gather(shards, name):
    n = len(shards)

    def body(*refs):
        x_refs, out_refs = refs[:n], refs[n:2 * n]
        send_sems, recv_sems, local_sems = refs[2 * n:]
        x, y, c = _mesh_pos()
        me, sibling = (x, y, c), (x, y, 1 - c)
        chips = [(1 - x, y), (x, 1 - y), (1 - x, 1 - y)]

        def rows(t, px, py, pc):
            return out_refs[t].at[4 * px + 2 * py + pc]

        def copy(t, k, block, to, src=None):
            return pltpu.make_async_remote_copy(
                src_ref=rows(t, *block) if src is None else src, dst_ref=rows(t, *block), send_sem=send_sems.at[t, k],
                recv_sem=recv_sems.at[t, k], device_id=to, device_id_type=MESH)

        mine = [pltpu.make_async_copy(x_refs[t], rows(t, *me), local_sems.at[t]) for t in range(n)]
        for cp in mine:
            cp.start()
        first = []
        for t in range(n):
            first.append(copy(t, 0, me, sibling, src=x_refs[t]))
            first += [copy(t, 1 + j, me, (*chip, c), src=x_refs[t]) for j, chip in enumerate(chips)]
        for cp in first:
            cp.start()
        passed = []
        for t in range(n):
            for j, chip in enumerate(chips):
                copy(t, 1 + j, (*chip, c), me).wait_recv()
                passed.append(copy(t, 4 + j, (*chip, c), sibling))
                passed[-1].start()
        for t in range(n):
            copy(t, 0, sibling, me).wait_recv()
            for j, chip in enumerate(chips):
                copy(t, 4 + j, (*chip, 1 - c), me).wait_recv()
        for cp in first + passed:
            cp.wait_send()
        for cp in mine:
            cp.wait()

    hbm = pl.BlockSpec(memory_space=pl.ANY)
    return pl.pallas_call(
        body, name=name, out_shape=[jax.ShapeDtypeStruct((N_DEV,) + a.shape, a.dtype) for a in shards],
        in_specs=[hbm] * n, out_specs=[hbm] * n,
        scratch_shapes=[pltpu.SemaphoreType.DMA((n, 7)), pltpu.SemaphoreType.DMA((n, 7)),
                        pltpu.SemaphoreType.DMA((n,))])(*shards)


def _row_tile(r, c_, cap_bytes=1 << 20):
    best = 0
    for t in range(16, r + 1, 16):
        if r % t == 0 and t * c_ * 4 <= cap_bytes:
            best = t
    return best if best else r


def _rs_chip_sum(g, t, name):
    _, r, c_ = g.shape
    tr = _row_tile(r, c_)
    core = lax.axis_index("c").astype(jnp.int32).reshape(1)

    def body(core_ref, g_ref, t_ref, p_ref):
        p_ref[...] = (g_ref[...].astype(F32) + t_ref[...].astype(F32)).astype(p_ref.dtype)

    return pl.pallas_call(
        body, name=name, out_shape=jax.ShapeDtypeStruct((4, r, c_), g.dtype),
        grid_spec=pltpu.PrefetchScalarGridSpec(
            num_scalar_prefetch=1, grid=(4, r // tr),
            in_specs=[pl.BlockSpec((1, tr, c_), lambda k, i, core_ref: (2 * k + core_ref[0], i, 0)),
                      pl.BlockSpec((1, tr, c_), lambda k, i, core_ref: (k, i, 0))],
            out_specs=pl.BlockSpec((1, tr, c_), lambda k, i, core_ref: (k, i, 0))),
        compiler_params=_params(("parallel", "parallel")))(core, g, t)


def _adamw_math(w, g, m, v):
    m = ADAM_B1 * m + (1.0 - ADAM_B1) * g
    v = ADAM_B2 * v + (1.0 - ADAM_B2) * jnp.square(g)
    m_hat = m / (1.0 - ADAM_B1 ** ADAM_STEP)
    v_hat = v / (1.0 - ADAM_B2 ** ADAM_STEP)
    delta = -ADAM_LR * (m_hat / (jnp.sqrt(v_hat) + ADAM_EPS) + ADAM_WD * w)
    return delta, m, v


def _adamw_big(ps, ts, w, m, v, name):
    _, r, c_ = w.shape
    tr = _row_tile(r, c_)
    chip = (2 * lax.axis_index("x") + lax.axis_index("y")).astype(jnp.int32).reshape(1)

    def body(chip_ref, p0, t0, p1, t1, w_ref, m_ref, v_ref, g_out, d_out, m_out, v_out):
        def update(p_ref, t_ref):
            g = p_ref[0].astype(F32) + t_ref[0].astype(F32) + t_ref[1].astype(F32) + t_ref[2].astype(F32)
            g_out[0] = g
            d_out[0], m_out[0], v_out[0] = _adamw_math(w_ref[0], g, m_ref[0], v_ref[0])

        pl.when(pl.program_id(0) == 0)(functools.partial(update, p0, t0))
        pl.when(pl.program_id(0) == 1)(functools.partial(update, p1, t1))

    def grad_specs(layer):
        return [pl.BlockSpec((1, tr, c_), lambda l, i, chip_ref: (chip_ref[0], jnp.where(l == layer, i, 0), 0)),
                pl.BlockSpec((3, tr, c_), lambda l, i, chip_ref: (0, jnp.where(l == layer, i, 0), 0))]

    nat = pl.BlockSpec((1, tr, c_), lambda l, i, chip_ref: (l, i, 0))
    shape = jax.ShapeDtypeStruct(w.shape, F32)
    return pl.pallas_call(
        body, name=name, out_shape=[shape] * 4,
        grid_spec=pltpu.PrefetchScalarGridSpec(
            num_scalar_prefetch=1, grid=(DEPTH, r // tr),
            in_specs=grad_specs(0) + grad_specs(1) + [nat, nat, nat], out_specs=[nat] * 4),
        compiler_params=_params(("parallel", "parallel"), VMEM_BIG))(chip, ps[0], ts[0], ps[1], ts[1], w, m, v)


def _sum_devices(parts, name):
    _, r, c_ = parts.shape

    def body(p_ref, o_ref):
        acc = p_ref[0]
        for d in range(1, N_DEV):
            acc = acc + p_ref[d]
        o_ref[...] = acc

    return pl.pallas_call(body, name=name, out_shape=jax.ShapeDtypeStruct((r, c_), F32),
                          compiler_params=_params(None, VMEM_BIG))(parts)


def _adamw_small(w, g, m, v, name):
    def body(w_ref, g_ref, m_ref, v_ref, d_out, m_out, v_out):
        d_out[...], m_out[...], v_out[...] = _adamw_math(w_ref[...], g_ref[...], m_ref[...], v_ref[...])

    shape = jax.ShapeDtypeStruct(w.shape, F32)
    return pl.pallas_call(body, name=name, out_shape=[shape] * 3)(w, g, m, v)


def kernel(x, positions, pre_mix_g, w_in, pool_w, pool_scale, pool_proj, conv_w, conv_b, conv_norm_g, conv_norm_b, conv_proj, sgu_norm_g, sgu_norm_b, sgu_w, sgu_b, sgu_proj, q_norm_g, w_uq, kv_norm_g, w_ukv, attn_proj, w_out, post_mix_g, pre_mlp_g, w_up, w_down, post_mlp_g, loss_target, m_pre_mix_g, m_w_in, m_pool_w, m_pool_scale, m_pool_proj, m_conv_w, m_conv_b, m_conv_norm_g, m_conv_norm_b, m_conv_proj, m_sgu_norm_g, m_sgu_norm_b, m_sgu_w, m_sgu_b, m_sgu_proj, m_q_norm_g, m_w_uq, m_kv_norm_g, m_w_ukv, m_attn_proj, m_w_out, m_post_mix_g, m_pre_mlp_g, m_w_up, m_w_down, m_post_mlp_g, v_pre_mix_g, v_w_in, v_pool_w, v_pool_scale, v_pool_proj, v_conv_w, v_conv_b, v_conv_norm_g, v_conv_norm_b, v_conv_proj, v_sgu_norm_g, v_sgu_norm_b, v_sgu_w, v_sgu_b, v_sgu_proj, v_q_norm_g, v_w_uq, v_kv_norm_g, v_w_ukv, v_attn_proj, v_w_out, v_post_mix_g, v_pre_mlp_g, v_w_up, v_w_down, v_post_mlp_g):
    args = dict(locals())
    wts = {n: args[n] for n in WEIGHTS}
    mom1 = {n: args["m_" + n] for n in WEIGHTS}
    mom2 = {n: args["v_" + n] for n in WEIGHTS}
    dev = 4 * lax.axis_index("x") + 2 * lax.axis_index("y") + lax.axis_index("c")

    shards = {(name, l): wts[name][l].astype(BF16) for name, _, _ in BIG for l in range(DEPTH)}
    taps = jnp.pad(conv_w.reshape(-1, 128), ((0, 1), (0, 0)))
    gathered = _all_gather([shards[("w_in", 0)], taps], "gather_first")
    gat = {("w_in", 0): gathered[0]}
    taps = gathered[-1][:, :CONV_WIDTH].reshape(N_DEV, DEPTH, CONV_WIDTH, BR // N_DEV)
    small = {n: wts[n] for n, _ in SMALL}
    small["conv_w"] = taps.transpose(1, 2, 0, 3).reshape(DEPTH, CONV_WIDTH, BR)

    loss_part, grad_x, grads, rs = _local_step(x[0], positions, loss_target[0], gat, small, shards, _Plan())
    small_g = _small_grads(grads)
    loss = lax.psum(loss_part, ("x", "y", "c"))

    out = {"grad": {}, "delta": {}, "new_m": {}, "new_v": {}}
    for name, _, _ in BIG:
        res = _adamw_big([rs[(name, l)][0] for l in range(DEPTH)], [rs[(name, l)][1] for l in range(DEPTH)],
                         wts[name], mom1[name], mom2[name], "adamw_" + name)
        for key, buf in zip(("grad", "delta", "new_m", "new_v"), res):
            out[key][name] = buf

    part = jnp.concatenate([_pack_small(small_g), small_g["conv_w"].reshape(-1, 128)], axis=0)
    total = _sum_devices(_all_gather([part], "gather_small_grads")[0], "sum_small_grads")
    g_small = total[:SMALL_ROWS]
    d_small, m_small, v_small = _adamw_small(_pack_small(wts), g_small, _pack_small(mom1), _pack_small(mom2), "adamw_small")
    for key, buf in (("grad", g_small), ("delta", d_small), ("new_m", m_small), ("new_v", v_small)):
        out[key].update(_unpack_small(buf))
    g_taps = total[SMALL_ROWS:].reshape(DEPTH, CONV_WIDTH, N_DEV, BR // N_DEV)
    g_taps = lax.dynamic_index_in_dim(g_taps, dev, axis=2, keepdims=False)
    flat = lambda a: a.reshape(-1, 128)
    d_taps, m_taps, v_taps = _adamw_small(flat(conv_w), flat(g_taps), flat(m_conv_w), flat(v_conv_w), "adamw_taps")
    for key, buf in (("grad", g_taps), ("delta", d_taps), ("new_m", m_taps), ("new_v", v_taps)):
        out[key]["conv_w"] = buf.reshape(conv_w.shape)

    return (loss, grad_x[None], *[out["grad"][n] for n in WEIGHTS], *[out["delta"][n] for n in WEIGHTS],
            *[out["new_m"][n] for n in WEIGHTS], *[out["new_v"][n] for n in WEIGHTS])
```

```python
import collections
import functools
import math

import jax
import jax.numpy as jnp
from jax import lax
from jax.experimental import pallas as pl
from jax.experimental.pallas import tpu as pltpu

F32 = jnp.float32
BF16 = jnp.bfloat16

D_MODEL = 2048
DEPTH = 2
EPS = 1e-6
N_BRANCH = 4
D_FF = 4 * D_MODEL
UP_SHARD = D_FF // 8
POOL_WINDOWS = (2, 4, 8, 16)
CONV_WIDTH = 31
CHUNK = 128
MLA_HEADS = 8
QK_NOPE = 128
QK_ROPE = 64
V_DIM = 128
ROPE_THETA = 10000.0
GDIM = 128
BR = 512
N_IN_A = 3648
ZA = 3712
N_GATE = N_BRANCH * D_MODEL
N_IN = N_IN_A + N_GATE
QW = 256
ACT_CAT = 3 * BR + MLA_HEADS * V_DIM
ATT_SCALE = (QK_NOPE + QK_ROPE) ** -0.5
NEG = -1e30

ADAM_LR = 0.001
ADAM_B1 = 0.9
ADAM_B2 = 0.999
ADAM_EPS = 1e-08
ADAM_WD = 0.01
ADAM_STEP = 10

N_DEV = 8
PACK_W = 1024
VMEM_BIG = 48 * 1024 * 1024
MESH = pl.DeviceIdType.MESH

BIG = (
    ("w_in", 1, (2048, 1480)),
    ("pool_proj", 1, (512, 256)),
    ("conv_proj", 1, (512, 256)),
    ("sgu_proj", 1, (512, 256)),
    ("w_uq", 1, (512, 192)),
    ("w_ukv", 1, (512, 256)),
    ("attn_proj", 1, (1024, 256)),
    ("w_out", 0, (256, 2048)),
    ("w_up", 1, (2048, 1024)),
    ("w_down", 0, (1024, 2048)),
)
SMALL = (
    ("pre_mix_g", (2048,)), ("pool_w", (4, 128, 128)), ("pool_scale", (512,)), ("conv_b", (512,)),
    ("conv_norm_g", (512,)), ("conv_norm_b", (512,)), ("sgu_norm_g", (512,)), ("sgu_norm_b", (512,)),
    ("sgu_w", (4, 128, 128)), ("sgu_b", (4, 128)), ("q_norm_g", (512,)), ("kv_norm_g", (512,)),
    ("post_mix_g", (2048,)), ("pre_mlp_g", (2048,)), ("post_mlp_g", (2048,)),
)
WEIGHTS = ("pre_mix_g", "w_in", "pool_w", "pool_scale", "pool_proj", "conv_w", "conv_b", "conv_norm_g", "conv_norm_b",
           "conv_proj", "sgu_norm_g", "sgu_norm_b", "sgu_w", "sgu_b", "sgu_proj", "q_norm_g", "w_uq", "kv_norm_g",
           "w_ukv", "attn_proj", "w_out", "post_mix_g", "pre_mlp_g", "w_up", "w_down", "post_mlp_g")


def _params(sem=None, vmem=None):
    return pltpu.CompilerParams(dimension_semantics=sem, vmem_limit_bytes=vmem)


def _tile(dim, pref):
    if dim <= pref:
        return dim
    best = 0
    for t in range(128, pref + 1, 128):
        if dim % t == 0:
            best = t
    return best if best >= 256 else dim


def _sigmoid(x):
    return 1.0 / (1.0 + jnp.exp(-x))


def _gelu(x):
    k = math.sqrt(2.0 / math.pi)
    return 0.5 * x * (1.0 + jnp.tanh(k * (x + 0.044715 * x * x * x)))


def _gelu_grad(x):
    k = math.sqrt(2.0 / math.pi)
    t = jnp.tanh(k * (x + 0.044715 * x * x * x))
    return 0.5 * (1.0 + t) + 0.5 * x * (1.0 - t * t) * k * (1.0 + 3.0 * 0.044715 * x * x)


def _rms(x, g):
    r = lax.rsqrt(jnp.mean(x * x, axis=-1, keepdims=True) + EPS)
    return x * r * g


def _rms_bwd(x, g, dy):
    r = lax.rsqrt(jnp.mean(x * x, axis=-1, keepdims=True) + EPS)
    dyg = dy * g
    dx = r * dyg - x * (r * r * r) * jnp.mean(dyg * x, axis=-1, keepdims=True)
    return dx, dy * x * r


def _ln_stats(x):
    mu = jnp.mean(x, axis=-1, keepdims=True)
    xc = x - mu
    r = lax.rsqrt(jnp.mean(xc * xc, axis=-1, keepdims=True) + EPS)
    return xc * r, r


def _ln_bwd(xh, r, g, dy):
    dxh = dy * g
    return r * (dxh - jnp.mean(dxh, axis=-1, keepdims=True) - xh * jnp.mean(dxh * xh, axis=-1, keepdims=True))


def _rot_half(x, width, off):
    n = x.shape[-1]
    lane = lax.broadcasted_iota(jnp.int32, x.shape, x.ndim - 1) % width
    return jnp.where(lane - off < QK_ROPE // 2, -pltpu.roll(x, n - QK_ROPE // 2, x.ndim - 1),
                     pltpu.roll(x, QK_ROPE // 2, x.ndim - 1))


def _colsum_into(ref, val, first):
    s = jnp.sum(val, axis=0, keepdims=True)

    @pl.when(first)
    def _():
        ref[...] = s

    @pl.when(jnp.logical_not(first))
    def _():
        ref[...] += s


Exchange = collections.namedtuple("Exchange", "inputs out_shapes aliases n_pairs n_local build")


class _Plan:
    def __init__(self):
        self.jobs = {}

    def at(self, kernel, make, done):
        self.jobs.setdefault(kernel, []).append((make, done))

    def take(self, kernel):
        return self.jobs.pop(kernel, [])


def _pcall(body, operands, *, name, grid, in_specs, out_specs, out_shape, scratch_shapes=(), sem=None, vmem=None,
           plan=None):
    jobs = plan.take(name) if plan is not None else []
    if not jobs:
        return pl.pallas_call(body, name=name, grid=grid, in_specs=list(in_specs), out_specs=list(out_specs),
                              out_shape=list(out_shape), scratch_shapes=list(scratch_shapes),
                              compiler_params=_params(sem, vmem))(*operands)
    made = [(make(), done) for make, done in jobs]
    comm = [op for ops, _ in made for op in ops]
    n_in, n_out, n_scr = len(in_specs), len(out_shape), len(scratch_shapes)
    c_in = [a for op in comm for a in op.inputs]
    c_out = [s for op in comm for s in op.out_shapes]
    sems, aliases, i_off, o_off = [], {}, n_in, n_out
    for op in comm:
        sems += [pltpu.SemaphoreType.DMA((op.n_pairs,)), pltpu.SemaphoreType.DMA((op.n_pairs,)),
                 pltpu.SemaphoreType.DMA((max(op.n_local, 1),))]
        for src, dst in op.aliases.items():
            aliases[i_off + src] = o_off + dst
        i_off += len(op.inputs)
        o_off += len(op.out_shapes)

    def carrier(*refs):
        ins, cins = refs[:n_in], refs[n_in:n_in + len(c_in)]
        base = n_in + len(c_in)
        outs, couts = refs[base:base + n_out], refs[base + n_out:base + n_out + len(c_out)]
        base += n_out + len(c_out)
        scr, csems = refs[base:base + n_scr], refs[base + n_scr:]
        ids = [pl.program_id(ax) for ax in range(len(grid))]
        first = functools.reduce(jnp.logical_and, [i == 0 for i in ids])
        last = functools.reduce(jnp.logical_and, [i == g - 1 for i, g in zip(ids, grid)])

        def pieces():
            res, ci, co = [], 0, 0
            for k, op in enumerate(comm):
                res.append(op.build(cins[ci:ci + len(op.inputs)], couts[co:co + len(op.out_shapes)],
                                    *csems[3 * k:3 * k + 3]))
                ci += len(op.inputs)
                co += len(op.out_shapes)
            return res

        @pl.when(first)
        def _():
            for sends, _, local in pieces():
                for cp in local + sends:
                    cp.start()

        body(*ins, *outs, *scr)

        @pl.when(last)
        def _():
            for sends, recvs, local in pieces():
                for cp in recvs:
                    cp.wait_recv()
                for cp in sends:
                    cp.wait_send()
                for cp in local:
                    cp.wait()

    hbm = pl.BlockSpec(memory_space=pl.ANY)
    res = pl.pallas_call(
        carrier, name=name, grid=grid, in_specs=list(in_specs) + [hbm] * len(c_in),
        out_specs=list(out_specs) + [hbm] * len(c_out), out_shape=list(out_shape) + c_out,
        scratch_shapes=list(scratch_shapes) + sems, input_output_aliases=aliases,
        compiler_params=_params(("arbitrary",) * len(grid), vmem))(*operands, *c_in)
    pos = n_out
    for ops, done in made:
        results = []
        for op in ops:
            results.append(list(res[pos:pos + len(op.out_shapes)]))
            pos += len(op.out_shapes)
        done(results)
    return list(res[:n_out])


def _block(ref, px, py, pc):
    return ref.at[4 * px + 2 * py + pc]


def _remote(src, dst, send_sems, recv_sems, k, to):
    return pltpu.make_async_remote_copy(src_ref=src, dst_ref=dst, send_sem=send_sems.at[k], recv_sem=recv_sems.at[k],
                                        device_id=to, device_id_type=MESH)


def _gather_own(shards):
    n = len(shards)

    def build(ins, outs, send_sems, recv_sems, local_sems):
        x, y, c = _mesh_pos()
        peers = [(x, y, 1 - c), (1 - x, y, c), (x, 1 - y, c), (1 - x, 1 - y, c)]
        sends, recvs, local = [], [], []
        for t in range(n):
            local.append(pltpu.make_async_copy(ins[t], _block(outs[t], x, y, c), local_sems.at[t]))
            for k, peer in enumerate(peers):
                sends.append(_remote(ins[t], _block(outs[t], x, y, c), send_sems, recv_sems, 4 * t + k, peer))
                recvs.append(_remote(ins[t], _block(outs[t], *peer), send_sems, recv_sems, 4 * t + k, peer))
        return sends, recvs, local

    return Exchange(list(shards), [jax.ShapeDtypeStruct((N_DEV,) + a.shape, a.dtype) for a in shards], {}, 4 * n, n, build)


def _gather_pass(bufs):
    n = len(bufs)

    def build(ins, outs, send_sems, recv_sems, local_sems):
        x, y, c = _mesh_pos()
        chips = [(1 - x, y), (x, 1 - y), (1 - x, 1 - y)]
        sends, recvs = [], []
        for t in range(n):
            for j, chip in enumerate(chips):
                mine, theirs = _block(outs[t], *chip, c), _block(outs[t], *chip, 1 - c)
                sends.append(_remote(mine, mine, send_sems, recv_sems, 3 * t + j, (x, y, 1 - c)))
                recvs.append(_remote(mine, theirs, send_sems, recv_sems, 3 * t + j, (x, y, 1 - c)))
        return sends, recvs, []

    return Exchange(list(bufs), [jax.ShapeDtypeStruct(a.shape, a.dtype) for a in bufs], {t: t for t in range(n)},
                    3 * n, 0, build)


def _scatter_sibling(gs):
    n = len(gs)

    def build(ins, outs, send_sems, recv_sems, local_sems):
        x, y, c = _mesh_pos()
        sends, recvs = [], []
        for t in range(n):
            for k in range(4):
                cp = _remote(ins[t].at[2 * k + 1 - c], outs[t].at[k], send_sems, recv_sems, 4 * t + k, (x, y, 1 - c))
                sends.append(cp)
                recvs.append(cp)
        return sends, recvs, []

    return Exchange(list(gs), [jax.ShapeDtypeStruct((4,) + g.shape[1:], g.dtype) for g in gs], {}, 4 * n, 0, build)


def _scatter_chips(ps):
    n = len(ps)

    def build(ins, outs, send_sems, recv_sems, local_sems):
        x, y, c = _mesh_pos()
        chips = [(1 - x, y), (x, 1 - y), (1 - x, 1 - y)]
        sends, recvs = [], []
        for t in range(n):
            for j, (cx, cy) in enumerate(chips):
                cp = _remote(ins[t].at[2 * cx + cy], outs[t].at[j], send_sems, recv_sems, 3 * t + j, (cx, cy, c))
                sends.append(cp)
                recvs.append(cp)
        return sends, recvs, []

    return Exchange(list(ps), [jax.ShapeDtypeStruct((3,) + p.shape[1:], p.dtype) for p in ps], {}, 3 * n, 0, build)


_DIMS = {"nn": ((1,), (0,)), "nt": ((1,), (1,)), "tn": ((0,), (0,))}


def _mm_call(a, b, *, mode, name, grid, kaxis, nk, a_spec, b_spec, o_specs, out_shape, acc_shape,
             extras=(), e_specs=(), epilogue=None, active=None, plan=None):
    ne, no = len(extras), len(out_shape)

    def body(a_ref, b_ref, *rest):
        e_refs, o_refs, acc_ref = rest[:ne], rest[ne:ne + no], rest[ne + no]
        ids = [pl.program_id(ax) for ax in range(len(grid))]
        k = ids[kaxis]

        def finish(acc):
            outs = (acc,) if epilogue is None else epilogue(acc, *[e[...] for e in e_refs])
            for o_ref, val in zip(o_refs, outs):
                o_ref[...] = val.astype(o_ref.dtype)

        def step():
            prod = lax.dot_general(a_ref[...], b_ref[...], (_DIMS[mode], ((), ())), preferred_element_type=F32)
            if nk == 1:
                finish(prod)
                return

            @pl.when(k == 0)
            def _():
                acc_ref[...] = prod

            @pl.when(k > 0)
            def _():
                acc_ref[...] += prod

        if active is None:
            step()
        else:
            pl.when(active(*ids))(step)
        if nk > 1:
            @pl.when(k == nk - 1)
            def _():
                finish(acc_ref[...])

    sem = tuple("arbitrary" if ax == kaxis else "parallel" for ax in range(len(grid)))
    scratch = pltpu.VMEM(acc_shape if nk > 1 else (8, 128), F32)
    return _pcall(body, (a, b, *extras), name=name, grid=grid, in_specs=[a_spec, b_spec, *e_specs],
                  out_specs=list(o_specs), out_shape=list(out_shape), scratch_shapes=[scratch],
                  sem=sem, vmem=VMEM_BIG, plan=plan)


def _mm(a, b, *, mode, name, out_dtypes=(F32,), extras=(), epilogue=None, tm=1024, tn=1024, tk=2048,
        mnk=None, b_spec=None, o_spec=None, out_struct=None, plan=None):
    if mnk is not None:
        m, n, k = mnk
    elif mode == "nn":
        (m, k), (_, n) = a.shape, b.shape
    elif mode == "nt":
        (m, k), (n, _) = a.shape, b.shape
    else:
        (k, m), (_, n) = a.shape, b.shape
    tm, tn, tk = _tile(m, tm), _tile(n, tn), _tile(k, tk)
    if tn > 2048:
        tm, tk = _tile(m, 512), _tile(k, 512)
    if tk > 2048:
        tm, tn = _tile(m, 512), _tile(n, 512)
    nk = k // tk
    if mode == "tn":
        a_spec = pl.BlockSpec((tk, tm), lambda i, j, kk: (kk, i))
    else:
        a_spec = pl.BlockSpec((tm, tk), lambda i, j, kk: (i, kk))
    if b_spec is not None:
        b_spec = b_spec(tn, tk)
    elif mode == "nt":
        b_spec = pl.BlockSpec((tn, tk), lambda i, j, kk: (j, kk))
    else:
        b_spec = pl.BlockSpec((tk, tn), lambda i, j, kk: (kk, j))

    def e_spec(e):
        if e.shape[1] == tn and n != tn:
            return pl.BlockSpec((tm, tn), lambda i, j, kk: (i, 0))
        return pl.BlockSpec((tm, tn), lambda i, j, kk: (i, j))

    e_specs = [e_spec(e) for e in extras]
    if o_spec is not None:
        o_specs = [o_spec(tm, tn)]
        out_shape = [out_struct]
    else:
        o_specs = [pl.BlockSpec((tm, tn), lambda i, j, kk: (i, j)) for _ in out_dtypes]
        out_shape = [jax.ShapeDtypeStruct((m, n), dt) for dt in out_dtypes]
    outs = _mm_call(a, b, mode=mode, name=name, grid=(m // tm, n // tn, nk), kaxis=2, nk=nk, a_spec=a_spec,
                    b_spec=b_spec, o_specs=o_specs, out_shape=out_shape, acc_shape=(tm, tn), extras=extras,
                    e_specs=e_specs, epilogue=epilogue, plan=plan)
    return outs[0] if len(outs) == 1 else outs


def _branch_of(kb):
    return jnp.minimum(kb, N_BRANCH - 1)


def _proj_merge_fwd(act_cat, proj_cat, zg, name, plan=None):
    s = act_cat.shape[0]
    tm, tn = _tile(s, 1024), 1024
    nj = D_MODEL // tn
    last = N_BRANCH - 1

    def kb(b, k):
        return jnp.where(b < last, b, last + k)

    def body(a_ref, b_ref, z_ref, y_ref, m_ref, acc_ref, sum_ref):
        b, k = pl.program_id(2), pl.program_id(3)

        @pl.when(jnp.logical_or(b == last, k == 0))
        def _():
            prod = jnp.dot(a_ref[...], b_ref[...], preferred_element_type=F32)

            @pl.when(k == 0)
            def _():
                acc_ref[...] = prod

            @pl.when(k > 0)
            def _():
                acc_ref[...] += prod

        @pl.when(k == 1)
        def _():
            y = acc_ref[...]
            y_ref[...] = y.astype(BF16)
            gated = _sigmoid(z_ref[...]) * y

            @pl.when(b == 0)
            def _():
                sum_ref[...] = gated

            @pl.when(b > 0)
            def _():
                sum_ref[...] += gated

            @pl.when(b == last)
            def _():
                m_ref[...] = sum_ref[...].astype(BF16)

    wide = pl.BlockSpec((tm, tn), lambda i, j, b, k: (i, b * nj + j))
    y, merged = _pcall(
        body, (act_cat, proj_cat, zg), name=name, grid=(s // tm, nj, N_BRANCH, 2),
        in_specs=[pl.BlockSpec((tm, BR), lambda i, j, b, k: (i, kb(b, k))),
                  pl.BlockSpec((BR, tn), lambda i, j, b, k: (kb(b, k), j)), wide],
        out_specs=[wide, pl.BlockSpec((tm, tn), lambda i, j, b, k: (i, j))],
        out_shape=[jax.ShapeDtypeStruct((s, N_GATE), BF16), jax.ShapeDtypeStruct((s, D_MODEL), BF16)],
        scratch_shapes=[pltpu.VMEM((tm, tn), F32), pltpu.VMEM((tm, tn), F32)],
        sem=("parallel", "parallel", "arbitrary", "arbitrary"), vmem=VMEM_BIG, plan=plan)
    return y, merged


def _proj_bwd_act(dy, proj_cat, name, plan=None):
    s = dy.shape[0]
    tm, tk = _tile(s, 1024), 1024
    nkk = D_MODEL // tk
    nkb = ACT_CAT // BR
    out = _mm_call(
        dy, proj_cat, mode="nt", name=name, grid=(s // tm, nkb, nkk), kaxis=2, nk=nkk,
        a_spec=pl.BlockSpec((tm, tk), lambda i, kb, k: (i, _branch_of(kb) * nkk + k)),
        b_spec=pl.BlockSpec((BR, tk), lambda i, kb, k: (kb, k)),
        o_specs=[pl.BlockSpec((tm, BR), lambda i, kb, k: (i, kb))],
        out_shape=[jax.ShapeDtypeStruct((s, ACT_CAT), F32)], acc_shape=(tm, BR), plan=plan)
    return out[0]


def _proj_bwd_w(act_cat, dy, name):
    s = dy.shape[0]
    tms, tn = _tile(s, 1024), 1024
    nj = D_MODEL // tn
    nkb = ACT_CAT // BR
    nm = s // tms
    out = _mm_call(
        act_cat, dy, mode="tn", name=name, grid=(nkb, nj, nm), kaxis=2, nk=nm,
        a_spec=pl.BlockSpec((tms, BR), lambda kb, j, m: (m, kb)),
        b_spec=pl.BlockSpec((tms, tn), lambda kb, j, m: (m, _branch_of(kb) * nj + j)),
        o_specs=[pl.BlockSpec((BR, tn), lambda kb, j, m: (kb, j))],
        out_shape=[jax.ShapeDtypeStruct((ACT_CAT, D_MODEL), BF16)], acc_shape=(BR, tn))
    return out[0]


def _row_specs(ts, n_full, n_vec):
    return ([pl.BlockSpec((ts, D_MODEL), lambda i: (i, 0))] * n_full
            + [pl.BlockSpec((1, D_MODEL), lambda i: (0, 0))] * n_vec)


def _pre_norm(x, g, name):
    s = x.shape[0]
    ts = _tile(s, 256)

    def body(x_ref, g_ref, h_ref):
        h_ref[...] = _rms(x_ref[...], g_ref[...]).astype(BF16)

    return pl.pallas_call(body, name=name, grid=(s // ts,), in_specs=_row_specs(ts, 1, 1),
                          out_specs=pl.BlockSpec((ts, D_MODEL), lambda i: (i, 0)),
                          out_shape=jax.ShapeDtypeStruct((s, D_MODEL), BF16), compiler_params=_params(("parallel",)))(x, g)


def _post_pre(x, r, g_post, g_next, name):
    s = x.shape[0]
    ts = _tile(s, 256)

    def body(x_ref, r_ref, gp_ref, gn_ref, xn_ref, h_ref):
        xn = x_ref[...] + _rms(r_ref[...], gp_ref[...])
        xn_ref[...] = xn
        h_ref[...] = _rms(xn, gn_ref[...]).astype(BF16)

    spec = pl.BlockSpec((ts, D_MODEL), lambda i: (i, 0))
    return pl.pallas_call(body, name=name, grid=(s // ts,), in_specs=_row_specs(ts, 2, 2), out_specs=[spec, spec],
                          out_shape=[jax.ShapeDtypeStruct((s, D_MODEL), F32), jax.ShapeDtypeStruct((s, D_MODEL), BF16)],
                          compiler_params=_params(("parallel",)))(x, r, g_post, g_next)


def _final_loss(x, r, g_post, target, name):
    s = x.shape[0]
    ts = _tile(s, 256)

    def body(x_ref, r_ref, gp_ref, t_ref, dy_ref, dr_ref, dg_ref, loss_ref):
        first = pl.program_id(0) == 0
        rv, gp = r_ref[...], gp_ref[...]
        diff = x_ref[...] + _rms(rv, gp) - t_ref[...]
        part = 0.5 * jnp.sum(jnp.mean(diff * diff, axis=-1, keepdims=True), axis=0, keepdims=True)
        dy = diff * (1.0 / D_MODEL)
        dy_ref[...] = dy
        dr, dg_rows = _rms_bwd(rv, gp, dy)
        dr_ref[...] = dr.astype(BF16)
        _colsum_into(dg_ref, dg_rows, first)
        _colsum_into(loss_ref, jnp.broadcast_to(part, (1, 128)), first)

    spec = pl.BlockSpec((ts, D_MODEL), lambda i: (i, 0))
    vec = pl.BlockSpec((1, D_MODEL), lambda i: (0, 0))
    return pl.pallas_call(
        body, name=name, grid=(s // ts,), in_specs=[spec, spec, vec, spec],
        out_specs=[spec, spec, vec, pl.BlockSpec((1, 128), lambda i: (0, 0))],
        out_shape=[jax.ShapeDtypeStruct((s, D_MODEL), F32), jax.ShapeDtypeStruct((s, D_MODEL), BF16),
                   jax.ShapeDtypeStruct((1, D_MODEL), F32), jax.ShapeDtypeStruct((1, 128), F32)],
        compiler_params=_params(("arbitrary",)))(x, r, g_post, target)


def _pre_bwd(dh, x, g_pre, dx_res, name, r_prev=None, g_post_prev=None):
    s = x.shape[0]
    ts = _tile(s, 256)
    chain = r_prev is not None

    def body(*refs):
        if chain:
            dh_ref, x_ref, res_ref, r_ref, g_ref, gp_ref, dx_ref, dr_ref, dg_ref, dgp_ref = refs
        else:
            dh_ref, x_ref, res_ref, g_ref, dx_ref, dg_ref = refs
        first = pl.program_id(0) == 0
        dxn, dg_rows = _rms_bwd(x_ref[...], g_ref[...], dh_ref[...])
        dx = res_ref[...] + dxn
        dx_ref[...] = dx
        _colsum_into(dg_ref, dg_rows, first)
        if chain:
            dr, dgp_rows = _rms_bwd(r_ref[...], gp_ref[...], dx)
            dr_ref[...] = dr.astype(BF16)
            _colsum_into(dgp_ref, dgp_rows, first)

    spec = pl.BlockSpec((ts, D_MODEL), lambda i: (i, 0))
    vec = pl.BlockSpec((1, D_MODEL), lambda i: (0, 0))
    full = jax.ShapeDtypeStruct((s, D_MODEL), F32)
    vshape = jax.ShapeDtypeStruct((1, D_MODEL), F32)
    if chain:
        return pl.pallas_call(
            body, name=name, grid=(s // ts,), in_specs=[spec] * 4 + [vec] * 2, out_specs=[spec, spec, vec, vec],
            out_shape=[full, jax.ShapeDtypeStruct((s, D_MODEL), BF16), vshape, vshape],
            compiler_params=_params(("arbitrary",)))(dh, x, dx_res, r_prev, g_pre, g_post_prev)
    return pl.pallas_call(
        body, name=name, grid=(s // ts,), in_specs=[spec] * 3 + [vec], out_specs=[spec, vec],
        out_shape=[full, vshape], compiler_params=_params(("arbitrary",)))(dh, x, dx_res, g_pre)


def _merge_bwd(dm, y, zg, name, plan=None):
    s = y.shape[0]
    ts, tc = _tile(s, 512), 512
    nj = D_MODEL // tc

    def body(dm_ref, y_ref, z_ref, dy_ref, dz_ref):
        g = _sigmoid(z_ref[...])
        d = dm_ref[...]
        dy_ref[...] = (d * g).astype(BF16)
        dz_ref[...] = (d * y_ref[...] * g * (1.0 - g)).astype(BF16)

    blk = pl.BlockSpec((ts, tc), lambda i, j, b: (i, b * nj + j))
    shape = jax.ShapeDtypeStruct((s, N_GATE), BF16)
    return _pcall(body, (dm, y, zg), name=name, grid=(s // ts, nj, N_BRANCH),
                  in_specs=[pl.BlockSpec((ts, tc), lambda i, j, b: (i, j)), blk, blk], out_specs=[blk, blk],
                  out_shape=[shape, shape], sem=("parallel", "parallel", "parallel"), plan=plan)


POOL_HALO = 16


def _pool_windows(ext_ref, ts, first_row):
    outs = []
    t = first_row + lax.broadcasted_iota(jnp.int32, (ts, GDIM), 0)
    for gi, w in enumerate(POOL_WINDOWS):
        cols = pl.ds(gi * GDIM, GDIM)
        acc = ext_ref[pl.ds(POOL_HALO, ts), cols]
        cur = acc
        for k in range(1, w):
            acc = acc + ext_ref[pl.ds(POOL_HALO - k, ts), cols]
        cnt = jnp.minimum(t + 1, w).astype(F32)
        outs.append(acc / cnt - cur)
    return outs


def _pool_fwd(za, pool_w, pool_scale, name):
    s = za.shape[0]
    ts = _tile(s, 512)
    hb = ts // POOL_HALO

    def body(a_ref, halo_ref, w_ref, sc_ref, o_ref, ext_ref):
        i = pl.program_id(0)
        ext_ref[pl.ds(0, POOL_HALO), :] = jnp.where(i > 0, halo_ref[...], 0.0)
        ext_ref[pl.ds(POOL_HALO, ts), :] = a_ref[...]
        pooled = _pool_windows(ext_ref, ts, i * ts)
        for gi in range(len(POOL_WINDOWS)):
            mixed = jnp.dot(pooled[gi].astype(BF16), w_ref[gi].astype(BF16), preferred_element_type=F32)
            o_ref[:, pl.ds(gi * GDIM, GDIM)] = (mixed * sc_ref[:, pl.ds(gi * GDIM, GDIM)]).astype(BF16)

    return pl.pallas_call(
        body, name=name, grid=(s // ts,),
        in_specs=[pl.BlockSpec((ts, BR), lambda i: (i, 0)),
                  pl.BlockSpec((POOL_HALO, BR), lambda i: (jnp.maximum(i * hb - 1, 0), 0)),
                  pl.BlockSpec((4, GDIM, GDIM), lambda i: (0, 0, 0)), pl.BlockSpec((1, BR), lambda i: (0, 0))],
        out_specs=pl.BlockSpec((ts, BR), lambda i: (i, 0)), out_shape=jax.ShapeDtypeStruct((s, BR), BF16),
        scratch_shapes=[pltpu.VMEM((POOL_HALO + ts, BR), F32)], compiler_params=_params(("parallel",)))(
            za, za, pool_w, pool_scale)


def _pool_bwd(dact, za, pool_w, pool_scale, name):
    s = za.shape[0]
    ts = _tile(s, 512)
    hb = ts // POOL_HALO
    n_t = s // ts

    def body(d_ref, dhalo_ref, a_ref, halo_ref, w_ref, sc_ref, dz_ref, dw_ref, dsc_ref, ext_ref, f_ref):
        i = pl.program_id(0)
        first = i == 0
        ext_ref[pl.ds(0, POOL_HALO), :] = jnp.where(i > 0, halo_ref[...], 0.0)
        ext_ref[pl.ds(POOL_HALO, ts), :] = a_ref[...]
        pooled = _pool_windows(ext_ref, ts, i * ts)
        d_tile = d_ref[...]
        d_next = jnp.where(i < n_t - 1, dhalo_ref[...], 0.0)
        t_ext = i * ts + lax.broadcasted_iota(jnp.int32, (ts + POOL_HALO, GDIM), 0)
        dsc = []
        for gi, w in enumerate(POOL_WINDOWS):
            cols = pl.ds(gi * GDIM, GDIM)
            wg = w_ref[gi].astype(BF16)
            sc = sc_ref[:, cols]
            pg = pooled[gi].astype(BF16)
            mixed = jnp.dot(pg, wg, preferred_element_type=F32)
            dsc.append(jnp.sum(d_tile[:, gi * GDIM:(gi + 1) * GDIM] * mixed, axis=0, keepdims=True))
            dmix = jnp.concatenate([d_tile[:, gi * GDIM:(gi + 1) * GDIM], d_next[:, gi * GDIM:(gi + 1) * GDIM]], axis=0) * sc
            dmix = dmix.astype(BF16)
            dwg = lax.dot_general(pg, dmix[:ts], (((0,), (0,)), ((), ())), preferred_element_type=F32)

            @pl.when(first)
            def _():
                dw_ref[gi] = dwg

            @pl.when(jnp.logical_not(first))
            def _():
                dw_ref[gi] += dwg

            dpool = lax.dot_general(dmix, wg, (((1,), (1,)), ((), ())), preferred_element_type=F32)
            f_ref[:, cols] = dpool / jnp.minimum(t_ext + 1, w).astype(F32)
            acc = f_ref[pl.ds(0, ts), cols]
            for k in range(1, w):
                acc = acc + f_ref[pl.ds(k, ts), cols]
            dz_ref[:, cols] = (acc - dpool[:ts]).astype(BF16)
        dsc_all = jnp.concatenate(dsc, axis=1)

        @pl.when(first)
        def _():
            dsc_ref[...] = dsc_all

        @pl.when(jnp.logical_not(first))
        def _():
            dsc_ref[...] += dsc_all

    n_hb = s // POOL_HALO
    return pl.pallas_call(
        body, name=name, grid=(n_t,),
        in_specs=[pl.BlockSpec((ts, BR), lambda i: (i, 0)),
                  pl.BlockSpec((POOL_HALO, BR), lambda i: (jnp.minimum((i + 1) * hb, n_hb - 1), 0)),
                  pl.BlockSpec((ts, BR), lambda i: (i, 0)),
                  pl.BlockSpec((POOL_HALO, BR), lambda i: (jnp.maximum(i * hb - 1, 0), 0)),
                  pl.BlockSpec((4, GDIM, GDIM), lambda i: (0, 0, 0)), pl.BlockSpec((1, BR), lambda i: (0, 0))],
        out_specs=[pl.BlockSpec((ts, BR), lambda i: (i, 0)), pl.BlockSpec((4, GDIM, GDIM), lambda i: (0, 0, 0)),
                   pl.BlockSpec((1, BR), lambda i: (0, 0))],
        out_shape=[jax.ShapeDtypeStruct((s, BR), BF16), jax.ShapeDtypeStruct((4, GDIM, GDIM), F32),
                   jax.ShapeDtypeStruct((1, BR), F32)],
        scratch_shapes=[pltpu.VMEM((POOL_HALO + ts, BR), F32), pltpu.VMEM((ts + POOL_HALO, BR), F32)],
        compiler_params=_params(("arbitrary",)))(dact, dact, za, za, pool_w, pool_scale)


CONV_HALO = 32
CONV_LEAD = CONV_HALO - (CONV_WIDTH - 1)


def _conv_fwd(za, conv_w, conv_b, ng, nb, name):
    s = za.shape[0]
    ts = _tile(s, 512)
    hb = ts // CONV_HALO

    def body(a_ref, g_ref, ah_ref, gh_ref, w_ref, b_ref, ng_ref, nb_ref, yc_ref, act_ref, ext_ref):
        i = pl.program_id(0)
        ext_ref[pl.ds(0, CONV_HALO), :] = jnp.where(i > 0, ah_ref[...] * _sigmoid(gh_ref[...]), 0.0)
        ext_ref[pl.ds(CONV_HALO, ts), :] = a_ref[...] * _sigmoid(g_ref[...])
        acc = jnp.zeros((ts, BR), F32) + b_ref[...]
        for k in range(CONV_WIDTH):
            acc = acc + w_ref[pl.ds(k, 1), :] * ext_ref[pl.ds(CONV_LEAD + k, ts), :]
        yc_ref[...] = acc
        xh, _ = _ln_stats(acc)
        ln = xh * ng_ref[...] + nb_ref[...]
        act_ref[...] = (ln * _sigmoid(ln)).astype(BF16)

    tile = lambda c: pl.BlockSpec((ts, BR), lambda i: (i, c))
    halo = lambda c: pl.BlockSpec((CONV_HALO, BR), lambda i: (jnp.maximum(i * hb - 1, 0), c))
    vec = pl.BlockSpec((1, BR), lambda i: (0, 0))
    return pl.pallas_call(
        body, name=name, grid=(s // ts,),
        in_specs=[tile(1), tile(2), halo(1), halo(2), pl.BlockSpec((CONV_WIDTH, BR), lambda i: (0, 0)), vec, vec, vec],
        out_specs=[pl.BlockSpec((ts, BR), lambda i: (i, 0))] * 2,
        out_shape=[jax.ShapeDtypeStruct((s, BR), F32), jax.ShapeDtypeStruct((s, BR), BF16)],
        scratch_shapes=[pltpu.VMEM((CONV_HALO + ts, BR), F32)], compiler_params=_params(("parallel",)))(
            za, za, za, za, conv_w, conv_b, ng, nb)


def _conv_bwd_norm(dact, yc, ng, nb, name):
    s = yc.shape[0]
    ts = _tile(s, 512)

    def body(d_ref, y_ref, ng_ref, nb_ref, dy_ref, db_ref, dng_ref, dnb_ref):
        first = pl.program_id(0) == 0
        xh, r = _ln_stats(y_ref[...])
        g = ng_ref[...]
        ln = xh * g + nb_ref[...]
        sg = _sigmoid(ln)
        dln = d_ref[...] * sg * (1.0 + ln * (1.0 - sg))
        dy = _ln_bwd(xh, r, g, dln)
        dy_ref[...] = dy
        _colsum_into(db_ref, dy, first)
        _colsum_into(dng_ref, dln * xh, first)
        _colsum_into(dnb_ref, dln, first)

    vec = pl.BlockSpec((1, BR), lambda i: (0, 0))
    vshape = jax.ShapeDtypeStruct((1, BR), F32)
    return pl.pallas_call(
        body, name=name, grid=(s // ts,),
        in_specs=[pl.BlockSpec((ts, BR), lambda i: (i, 1)), pl.BlockSpec((ts, BR), lambda i: (i, 0)), vec, vec],
        out_specs=[pl.BlockSpec((ts, BR), lambda i: (i, 0)), vec, vec, vec],
        out_shape=[jax.ShapeDtypeStruct((s, BR), F32), vshape, vshape, vshape],
        compiler_params=_params(("arbitrary",)))(dact, yc, ng, nb)


def _conv_bwd_taps(dyc, za, conv_w, name):
    s = za.shape[0]
    ts = _tile(s, 512)
    hb = ts // CONV_HALO
    n_t = s // ts
    n_hb = s // CONV_HALO

    def body(d_ref, dh_ref, a_ref, g_ref, ah_ref, gh_ref, w_ref, dz_ref, dw_ref, ext_ref, f_ref):
        i = pl.program_id(0)
        first = i == 0
        a, sg = a_ref[...], _sigmoid(g_ref[...])
        ext_ref[pl.ds(0, CONV_HALO), :] = jnp.where(i > 0, ah_ref[...] * _sigmoid(gh_ref[...]), 0.0)
        ext_ref[pl.ds(CONV_HALO, ts), :] = a * sg
        d = d_ref[...]
        f_ref[pl.ds(0, ts), :] = d
        f_ref[pl.ds(ts, CONV_HALO), :] = jnp.where(i < n_t - 1, dh_ref[...], 0.0)
        dglu = jnp.zeros((ts, BR), F32)
        rows = []
        for k in range(CONV_WIDTH):
            rows.append(jnp.sum(d * ext_ref[pl.ds(CONV_LEAD + k, ts), :], axis=0, keepdims=True))
            dglu = dglu + w_ref[pl.ds(k, 1), :] * f_ref[pl.ds(CONV_WIDTH - 1 - k, ts), :]
        rows.append(jnp.zeros((1, BR), F32))
        dw = jnp.concatenate(rows, axis=0)

        @pl.when(first)
        def _():
            dw_ref[...] = dw

        @pl.when(jnp.logical_not(first))
        def _():
            dw_ref[...] += dw

        dz_ref[:, pl.ds(0, BR)] = (dglu * sg).astype(BF16)
        dz_ref[:, pl.ds(BR, BR)] = (dglu * a * sg * (1.0 - sg)).astype(BF16)

    tile = lambda c: pl.BlockSpec((ts, BR), lambda i: (i, c))
    halo = lambda c: pl.BlockSpec((CONV_HALO, BR), lambda i: (jnp.maximum(i * hb - 1, 0), c))
    return pl.pallas_call(
        body, name=name, grid=(n_t,),
        in_specs=[pl.BlockSpec((ts, BR), lambda i: (i, 0)),
                  pl.BlockSpec((CONV_HALO, BR), lambda i: (jnp.minimum((i + 1) * hb, n_hb - 1), 0)),
                  tile(1), tile(2), halo(1), halo(2), pl.BlockSpec((CONV_WIDTH, BR), lambda i: (0, 0))],
        out_specs=[pl.BlockSpec((ts, 2 * BR), lambda i: (i, 0)), pl.BlockSpec((CONV_WIDTH + 1, BR), lambda i: (0, 0))],
        out_shape=[jax.ShapeDtypeStruct((s, 2 * BR), BF16), jax.ShapeDtypeStruct((CONV_WIDTH + 1, BR), F32)],
        scratch_shapes=[pltpu.VMEM((CONV_HALO + ts, BR), F32), pltpu.VMEM((ts + CONV_HALO, BR), F32)],
        compiler_params=_params(("arbitrary",)))(dyc, dyc, za, za, za, za, conv_w)


def _tril(w):
    r = lax.broadcasted_iota(jnp.int32, (CHUNK, CHUNK), 0)
    c = lax.broadcasted_iota(jnp.int32, (CHUNK, CHUNK), 1)
    return jnp.where(c <= r, w, 0.0)


def _sgu_fwd(za, ng, nb, sgu_w, bias_b, name):
    s = za.shape[0]
    ts = _tile(s, 512)

    def body(u_ref, v_ref, ng_ref, nb_ref, w_ref, b_ref, o_ref):
        u = _gelu(u_ref[...])
        xh, _ = _ln_stats(_gelu(v_ref[...]))
        vln = (xh * ng_ref[...] + nb_ref[...]).astype(BF16)
        for gi in range(4):
            wg = _tril(w_ref[gi]).astype(BF16)
            for n in range(ts // CHUNK):
                blk = vln[n * CHUNK:(n + 1) * CHUNK, gi * GDIM:(gi + 1) * GDIM]
                sp = jnp.dot(wg, blk, preferred_element_type=F32) + b_ref[gi]
                o_ref[pl.ds(n * CHUNK, CHUNK), pl.ds(gi * GDIM, GDIM)] = (
                    u[n * CHUNK:(n + 1) * CHUNK, gi * GDIM:(gi + 1) * GDIM] * sp).astype(BF16)

    vec = pl.BlockSpec((1, BR), lambda i: (0, 0))
    cube = pl.BlockSpec((4, CHUNK, GDIM), lambda i: (0, 0, 0))
    return pl.pallas_call(
        body, name=name, grid=(s // ts,),
        in_specs=[pl.BlockSpec((ts, BR), lambda i: (i, 3)), pl.BlockSpec((ts, BR), lambda i: (i, 4)), vec, vec, cube, cube],
        out_specs=pl.BlockSpec((ts, BR), lambda i: (i, 0)), out_shape=jax.ShapeDtypeStruct((s, BR), BF16),
        compiler_params=_params(("parallel",)))(za, za, ng, nb, sgu_w, bias_b)


def _sgu_bwd(dact, za, ng, nb, sgu_w, bias_b, name):
    s = za.shape[0]
    ts = _tile(s, 512)

    def body(d_ref, u_ref, v_ref, ng_ref, nb_ref, w_ref, b_ref, dz_ref, dw_ref, db_ref, dng_ref, dnb_ref, dv_ref):
        first = pl.program_id(0) == 0
        u_raw, v_raw = u_ref[...], v_ref[...]
        u = _gelu(u_raw)
        xh, r = _ln_stats(_gelu(v_raw))
        g = ng_ref[...]
        vln = (xh * g + nb_ref[...]).astype(BF16)
        d = d_ref[...]
        dsp = d * u
        dsp16 = dsp.astype(BF16)
        for gi in range(4):
            wg = _tril(w_ref[gi]).astype(BF16)
            dwg = jnp.zeros((CHUNK, CHUNK), F32)
            dbg = jnp.zeros((CHUNK, 1), F32)
            for n in range(ts // CHUNK):
                rows, cols = slice(n * CHUNK, (n + 1) * CHUNK), slice(gi * GDIM, (gi + 1) * GDIM)
                blk = vln[rows, cols]
                sp = jnp.dot(wg, blk, preferred_element_type=F32) + b_ref[gi]
                dz_ref[pl.ds(n * CHUNK, CHUNK), pl.ds(gi * GDIM, GDIM)] = (
                    d[rows, cols] * sp * _gelu_grad(u_raw[rows, cols])).astype(BF16)
                dv_ref[pl.ds(n * CHUNK, CHUNK), pl.ds(gi * GDIM, GDIM)] = lax.dot_general(
                    wg, dsp16[rows, cols], (((0,), (0,)), ((), ())), preferred_element_type=F32)
                dwg = dwg + lax.dot_general(dsp16[rows, cols], blk, (((1,), (1,)), ((), ())), preferred_element_type=F32)
                dbg = dbg + jnp.sum(dsp[rows, cols], axis=1, keepdims=True)
            dwg = _tril(dwg)

            @pl.when(first)
            def _():
                dw_ref[gi] = dwg
                db_ref[gi] = dbg

            @pl.when(jnp.logical_not(first))
            def _():
                dw_ref[gi] += dwg
                db_ref[gi] += dbg

        dvln = dv_ref[...]
        dz_ref[:, pl.ds(BR, BR)] = (_ln_bwd(xh, r, g, dvln) * _gelu_grad(v_raw)).astype(BF16)
        _colsum_into(dng_ref, dvln * xh, first)
        _colsum_into(dnb_ref, dvln, first)

    vec = pl.BlockSpec((1, BR), lambda i: (0, 0))
    cube = pl.BlockSpec((4, CHUNK, GDIM), lambda i: (0, 0, 0))
    vshape = jax.ShapeDtypeStruct((1, BR), F32)
    return pl.pallas_call(
        body, name=name, grid=(s // ts,),
        in_specs=[pl.BlockSpec((ts, BR), lambda i: (i, 2)), pl.BlockSpec((ts, BR), lambda i: (i, 3)),
                  pl.BlockSpec((ts, BR), lambda i: (i, 4)), vec, vec, cube, cube],
        out_specs=[pl.BlockSpec((ts, 2 * BR), lambda i: (i, 0)), cube, pl.BlockSpec((4, CHUNK, 1), lambda i: (0, 0, 0)),
                   vec, vec],
        out_shape=[jax.ShapeDtypeStruct((s, 2 * BR), BF16), jax.ShapeDtypeStruct((4, CHUNK, CHUNK), F32),
                   jax.ShapeDtypeStruct((4, CHUNK, 1), F32), vshape, vshape],
        scratch_shapes=[pltpu.VMEM((ts, BR), F32)], compiler_params=_params(("arbitrary",)))(
            dact, za, za, ng, nb, sgu_w, bias_b)


KR_BLOCK = 3584 // 128


def _mla_prep(za, qg, kvg, ck, sk, name):
    s = za.shape[0]
    ts = _tile(s, 512)

    def body(cq_ref, ckv_ref, kr_ref, qg_ref, kvg_ref, c_ref, s_ref, qn_ref, kvn_ref, krr_ref):
        qn_ref[...] = _rms(cq_ref[...], qg_ref[...]).astype(BF16)
        kvn_ref[...] = _rms(ckv_ref[...], kvg_ref[...]).astype(BF16)
        kr = kr_ref[...]
        krr_ref[...] = (kr * c_ref[...] + _rot_half(kr, 128, 0) * s_ref[...]).astype(BF16)

    vec = pl.BlockSpec((1, BR), lambda i: (0, 0))
    tab = pl.BlockSpec((ts, 128), lambda i: (i, 0))
    row = pl.BlockSpec((ts, BR), lambda i: (i, 0))
    return pl.pallas_call(
        body, name=name, grid=(s // ts,),
        in_specs=[pl.BlockSpec((ts, BR), lambda i: (i, 5)), pl.BlockSpec((ts, BR), lambda i: (i, 6)),
                  pl.BlockSpec((ts, 128), lambda i: (i, KR_BLOCK)), vec, vec, tab, tab],
        out_specs=[row, row, tab],
        out_shape=[jax.ShapeDtypeStruct((s, BR), BF16), jax.ShapeDtypeStruct((s, BR), BF16),
                   jax.ShapeDtypeStruct((s, 128), BF16)],
        compiler_params=_params(("parallel",)))(za, za, za, qg, kvg, ck, sk)


def _mla_prep_bwd(dqn, dkvn, dkr_heads, za, qg, kvg, ck, sk, name):
    s = za.shape[0]
    ts = _tile(s, 512)

    def body(dq_ref, dkv_ref, dkr_ref, cq_ref, ckv_ref, qg_ref, kvg_ref, c_ref, s_ref, dz_ref, dqg_ref, dkvg_ref):
        first = pl.program_id(0) == 0
        dcq, rows_q = _rms_bwd(cq_ref[...], qg_ref[...], dq_ref[...])
        dckv, rows_kv = _rms_bwd(ckv_ref[...], kvg_ref[...], dkv_ref[...])
        dz_ref[:, pl.ds(0, BR)] = dcq.astype(BF16)
        dz_ref[:, pl.ds(BR, BR)] = dckv.astype(BF16)
        dk = dkr_ref[:, pl.ds(0, 128)]
        for h in range(1, MLA_HEADS):
            dk = dk + dkr_ref[:, pl.ds(h * 128, 128)]
        dz_ref[:, pl.ds(2 * BR, 128)] = (dk * c_ref[...] - _rot_half(dk, 128, 0) * s_ref[...]).astype(BF16)
        _colsum_into(dqg_ref, rows_q, first)
        _colsum_into(dkvg_ref, rows_kv, first)

    vec = pl.BlockSpec((1, BR), lambda i: (0, 0))
    tab = pl.BlockSpec((ts, 128), lambda i: (i, 0))
    row = pl.BlockSpec((ts, BR), lambda i: (i, 0))
    wide = 2 * BR + 128
    vshape = jax.ShapeDtypeStruct((1, BR), F32)
    return pl.pallas_call(
        body, name=name, grid=(s // ts,),
        in_specs=[row, row, pl.BlockSpec((ts, MLA_HEADS * 128), lambda i: (i, 0)),
                  pl.BlockSpec((ts, BR), lambda i: (i, 5)), pl.BlockSpec((ts, BR), lambda i: (i, 6)), vec, vec, tab, tab],
        out_specs=[pl.BlockSpec((ts, wide), lambda i: (i, 0)), vec, vec],
        out_shape=[jax.ShapeDtypeStruct((s, wide), BF16), vshape, vshape],
        compiler_params=_params(("arbitrary",)))(dqn, dkvn, dkr_heads, za, za, qg, kvg, ck, sk)


def _attn_tiles(s):
    tq, tk = _tile(s, 1024), _tile(s, 512)
    return tq, tk, tq // tk


def _causal(qi, ki, tq, tk):
    row = qi * tq + lax.broadcasted_iota(jnp.int32, (tq, tk), 0)
    col = ki * tk + lax.broadcasted_iota(jnp.int32, (tq, tk), 1)
    return col <= row


def _step_count(t, firsts):
    n = jnp.int32(0)
    for f in firsts[1:]:
        n = n + jnp.where(t >= f, 1, 0)
    return n


def _q_major_pairs(nq, r):
    firsts = [r * qq * (qq + 1) // 2 for qq in range(nq)]

    def pair(t):
        qi = _step_count(t, firsts)
        return qi, t - r * qi * (qi + 1) // 2

    return r * nq * (nq + 1) // 2, pair


def _k_major_pairs(nq, nk, r):
    counts = [nq - kk // r for kk in range(nk)]
    firsts = [sum(counts[:kk]) for kk in range(nk)]

    def pair(t):
        ki = _step_count(t, firsts)
        first = jnp.int32(0)
        for kk in range(1, nk):
            first = first + jnp.where(t >= firsts[kk], counts[kk - 1], 0)
        return ki // r + (t - first), ki

    return sum(counts), pair


FLASH_HEADS = 2


def _flash_fwd(q, kv, krr, name, plan=None):
    s = q.shape[0]
    tq, tk, r = _attn_tiles(s)
    n_pairs, pair = _q_major_pairs(s // tq, r)
    hp = FLASH_HEADS

    def body(q_ref, kv_ref, kr_ref, o_ref, lse_ref, m_sc, l_sc, acc_sc):
        qi, ki = pair(pl.program_id(1))

        @pl.when(ki == 0)
        def _():
            m_sc[...] = jnp.full((hp, tq, 1), NEG, F32)
            l_sc[...] = jnp.zeros((hp, tq, 1), F32)
            acc_sc[...] = jnp.zeros((hp, tq, V_DIM), F32)

        def step(masked):
            for h in range(hp):
                k = jnp.concatenate([kv_ref[:, pl.ds(h * QW, QK_NOPE)], kr_ref[...]], axis=1)
                sc = lax.dot_general(q_ref[:, pl.ds(h * QW, QW)], k, (((1,), (1,)), ((), ())), preferred_element_type=F32)
                if masked:
                    sc = jnp.where(_causal(qi, ki, tq, tk), sc, NEG)
                m_prev = m_sc[h]
                m_new = jnp.maximum(m_prev, jnp.max(sc, axis=1, keepdims=True))
                alpha = jnp.exp(m_prev - m_new)
                p = jnp.exp(sc - m_new)
                l_sc[h] = alpha * l_sc[h] + jnp.sum(p, axis=1, keepdims=True)
                acc_sc[h] = alpha * acc_sc[h] + jnp.dot(p.astype(BF16), kv_ref[:, pl.ds(h * QW + QK_NOPE, V_DIM)],
                                                        preferred_element_type=F32)
                m_sc[h] = m_new

        pl.when(ki < qi * r)(functools.partial(step, False))
        pl.when(ki >= qi * r)(functools.partial(step, True))

        @pl.when(ki == (qi + 1) * r - 1)
        def _():
            for h in range(hp):
                o_ref[:, pl.ds(h * V_DIM, V_DIM)] = (acc_sc[h] / l_sc[h]).astype(BF16)
                lse_ref[:, pl.ds(h * 128, 128)] = jnp.broadcast_to(m_sc[h] + jnp.log(l_sc[h]), (tq, 128))

    out_blk = pl.BlockSpec((tq, hp * 128), lambda g, t: (pair(t)[0], g))
    return _pcall(
        body, (q, kv, krr), name=name, grid=(MLA_HEADS // hp, n_pairs),
        in_specs=[pl.BlockSpec((tq, hp * QW), lambda g, t: (pair(t)[0], g)),
                  pl.BlockSpec((tk, hp * QW), lambda g, t: (pair(t)[1], g)),
                  pl.BlockSpec((tk, 128), lambda g, t: (pair(t)[1], 0))],
        out_specs=[out_blk, out_blk],
        out_shape=[jax.ShapeDtypeStruct((s, MLA_HEADS * V_DIM), BF16), jax.ShapeDtypeStruct((s, MLA_HEADS * 128), F32)],
        scratch_shapes=[pltpu.VMEM((hp, tq, 1), F32), pltpu.VMEM((hp, tq, 1), F32), pltpu.VMEM((hp, tq, V_DIM), F32)],
        sem=("parallel", "arbitrary"), vmem=VMEM_BIG, plan=plan)


DO_BLOCK = 3 * BR // 128


def _flash_probs(q_ref, kn_ref, kr_ref, v_ref, do_ref, o_ref, lse_ref, qi, ki, tq, tk, masked):
    k = jnp.concatenate([kn_ref[...], kr_ref[...]], axis=1)
    q = q_ref[...]
    sc = lax.dot_general(q, k, (((1,), (1,)), ((), ())), preferred_element_type=F32)
    lse = jnp.max(lse_ref[...], axis=1, keepdims=True)
    p = jnp.exp(sc - lse)
    if masked:
        p = jnp.where(_causal(qi, ki, tq, tk), p, 0.0)
    do = do_ref[...]
    delta = jnp.sum(do * o_ref[...].astype(F32), axis=1, keepdims=True)
    do = do.astype(BF16)
    dp = lax.dot_general(do, v_ref[...], (((1,), (1,)), ((), ())), preferred_element_type=F32)
    ds = (p * (dp - delta)).astype(BF16)
    return q, k, p, do, ds


def _flash_bwd(q, kv, krr, dact, o, lse, cq, sq, name, plan=None):
    s = q.shape[0]
    tq, tk, r = _attn_tiles(s)
    nq, nk = s // tq, s // tk
    n_pairs, pair = _k_major_pairs(nq, nk, r)

    def body(q_ref, kn_ref, kr_ref, v_ref, do_ref, o_ref, lse_ref, c_ref, s_ref, dkv_ref, dkr_ref, dq_ref,
             dk_sc, dv_sc, dq_sc):
        qi, ki = pair(pl.program_id(1))
        rows = pl.ds(pl.multiple_of(qi * tq, tq), tq)

        @pl.when(qi == ki // r)
        def _():
            dk_sc[...] = jnp.zeros((tk, QW), F32)
            dv_sc[...] = jnp.zeros((tk, V_DIM), F32)

        @pl.when(ki == 0)
        def _():
            dq_sc[rows, :] = jnp.zeros((tq, QW), F32)

        def step(masked):
            qv, k, p, dov, ds = _flash_probs(q_ref, kn_ref, kr_ref, v_ref, do_ref, o_ref, lse_ref, qi, ki, tq, tk, masked)
            dv_sc[...] += lax.dot_general(p.astype(BF16), dov, (((0,), (0,)), ((), ())), preferred_element_type=F32)
            dk_sc[...] += lax.dot_general(ds, qv, (((0,), (0,)), ((), ())), preferred_element_type=F32)
            dq_sc[rows, :] += jnp.dot(ds, k, preferred_element_type=F32)

        pl.when(ki < qi * r)(functools.partial(step, False))
        pl.when(ki >= qi * r)(functools.partial(step, True))

        @pl.when(qi == nq - 1)
        def _():
            dkv_ref[:, pl.ds(0, 128)] = dk_sc[:, pl.ds(0, 128)].astype(BF16)
            dkv_ref[:, pl.ds(128, 128)] = dv_sc[...].astype(BF16)
            dkr_ref[...] = dk_sc[:, pl.ds(128, 128)]

        @pl.when(ki == (qi + 1) * r - 1)
        def _():
            dq = dq_sc[rows, :] * ATT_SCALE
            dq_ref[...] = (dq * c_ref[...] - _rot_half(dq, QW, QK_NOPE) * s_ref[...]).astype(BF16)

    def next_done(t):
        qi, ki = pair(t)
        return jnp.minimum(ki // r + jnp.where(jnp.logical_and(ki % r == r - 1, qi > ki // r), 1, 0), nq - 1)

    qmap = lambda c: (lambda h, t: (pair(t)[0], c(h)))
    kmap = lambda c: (lambda h, t: (pair(t)[1], c(h)))
    last = lambda c: (lambda h, t: (next_done(t), c(h)))
    return _pcall(
        body, (q, kv, krr, kv, dact, o, lse, cq, sq), name=name, grid=(MLA_HEADS, n_pairs),
        in_specs=[pl.BlockSpec((tq, QW), qmap(lambda h: h)), pl.BlockSpec((tk, 128), kmap(lambda h: 2 * h)),
                  pl.BlockSpec((tk, 128), kmap(lambda h: 0)), pl.BlockSpec((tk, 128), kmap(lambda h: 2 * h + 1)),
                  pl.BlockSpec((tq, 128), qmap(lambda h: DO_BLOCK + h)), pl.BlockSpec((tq, 128), qmap(lambda h: h)),
                  pl.BlockSpec((tq, 128), qmap(lambda h: h)), pl.BlockSpec((tq, QW), last(lambda h: 0)),
                  pl.BlockSpec((tq, QW), last(lambda h: 0))],
        out_specs=[pl.BlockSpec((tk, QW), kmap(lambda h: h)), pl.BlockSpec((tk, 128), kmap(lambda h: h)),
                   pl.BlockSpec((tq, QW), last(lambda h: h))],
        out_shape=[jax.ShapeDtypeStruct((s, MLA_HEADS * QW), BF16), jax.ShapeDtypeStruct((s, MLA_HEADS * 128), F32),
                   jax.ShapeDtypeStruct((s, MLA_HEADS * QW), BF16)],
        scratch_shapes=[pltpu.VMEM((tk, QW), F32), pltpu.VMEM((tk, V_DIM), F32), pltpu.VMEM((s, QW), F32)],
        sem=("parallel", "arbitrary"), vmem=VMEM_BIG, plan=plan)


def _rope_tables(positions):
    inv_freq = ROPE_THETA ** (-jnp.arange(0, QK_ROPE, 2, dtype=F32) / QK_ROPE)
    ang = positions.reshape(-1).astype(F32)[:, None] * inv_freq
    cos, sin = jnp.cos(ang), jnp.sin(ang)
    s = cos.shape[0]
    one, zero = jnp.ones((s, 64), F32), jnp.zeros((s, 64), F32)
    ck = jnp.concatenate([cos, cos, one], axis=1)
    sk = jnp.concatenate([sin, sin, zero], axis=1)
    cq = jnp.concatenate([one, one, ck], axis=1)
    sq = jnp.concatenate([zero, zero, sk], axis=1)
    return ck, sk, cq, sq


def _cols_full(gathered):
    _, r, c = gathered.shape
    return gathered.transpose(1, 0, 2).reshape(r, N_DEV * c)


def _cols_by_owner(full):
    r, n = full.shape
    return full.reshape(r, N_DEV, n // N_DEV).transpose(1, 0, 2)


def _layer_weights(gat, small, l):
    w_in = _cols_full(gat[("w_in", l)])
    w = {
        "gat": gat, "layer": l,
        "w_a": jnp.concatenate([w_in[:, :N_IN_A], jnp.zeros((D_MODEL, ZA - N_IN_A), BF16)], axis=1),
        "w_g": w_in[:, N_IN_A:],
        "conv_w": small["conv_w"][l],
        "pool_w": small["pool_w"][l], "sgu_w": small["sgu_w"][l],
        "sgu_bias": jnp.broadcast_to(small["sgu_b"][l][:, :, None], (4, CHUNK, GDIM)),
    }
    for name in ("pre_mix_g", "pool_scale", "conv_b", "conv_norm_g", "conv_norm_b", "sgu_norm_g", "sgu_norm_b",
                 "q_norm_g", "kv_norm_g", "post_mix_g", "pre_mlp_g", "post_mlp_g"):
        w[name] = small[name][l][None, :]
    return w


def _late(w, name):
    if name not in w:
        gat, l = w["gat"], w["layer"]
        if name == "proj_cat":
            w[name] = jnp.concatenate([_cols_full(gat[(n, l)]) for n in ("pool_proj", "conv_proj", "sgu_proj", "attn_proj")],
                                      axis=0)
        elif name == "w_uq":
            w[name] = jnp.pad(gat[("w_uq", l)].transpose(1, 0, 2),
                              ((0, 0), (0, 0), (0, QW - QK_NOPE - QK_ROPE))).reshape(BR, MLA_HEADS * QW)
        elif name == "w_ukv":
            w[name] = _cols_full(gat[("w_ukv", l)])
        elif name == "w_up":
            w[name] = gat[("w_up", l)]
        else:
            arr = gat[(name, l)]
            w[name] = arr.reshape(N_DEV * arr.shape[1], arr.shape[2])
    return w[name]


def _layer_fwd(x, h1, w, tabs, l, plan):
    ck, sk, cq, sq = tabs
    n = f"l{l}_"
    za = _mm(h1, w["w_a"], mode="nn", name=n + "za", plan=plan)
    zg = _mm(h1, w["w_g"], mode="nn", name=n + "zg", plan=plan)
    a_pool = _pool_fwd(za, w["pool_w"], w["pool_scale"], n + "pool")
    yc, a_conv = _conv_fwd(za, w["conv_w"], w["conv_b"], w["conv_norm_g"], w["conv_norm_b"], n + "conv")
    a_sgu = _sgu_fwd(za, w["sgu_norm_g"], w["sgu_norm_b"], w["sgu_w"], w["sgu_bias"], n + "sgu")
    qn, kvn, krr = _mla_prep(za, w["q_norm_g"], w["kv_norm_g"], ck, sk, n + "mla_prep")
    q = _mm(qn, _late(w, "w_uq"), mode="nn", name=n + "q", out_dtypes=(BF16,), extras=(cq, sq), tn=QW,
            epilogue=lambda acc, c, sn: ((acc * c + _rot_half(acc, QW, QK_NOPE) * sn) * ATT_SCALE,))
    kv = _mm(kvn, _late(w, "w_ukv"), mode="nn", name=n + "kv", out_dtypes=(BF16,))
    o, lse = _flash_fwd(q, kv, krr, n + "flash", plan=plan)
    act_cat = jnp.concatenate([a_pool, a_conv, a_sgu, o], axis=1)
    y, merged = _proj_merge_fwd(act_cat, _late(w, "proj_cat"), zg, n + "proj", plan=plan)
    m2 = _mm(merged, _late(w, "w_out"), mode="nn", name=n + "out")
    x1, h2 = _post_pre(x, m2, w["post_mix_g"], w["pre_mlp_g"], n + "post_mix")
    up, act = _mm(h2, _late(w, "w_up"), mode="nn", name=n + "up", out_dtypes=(F32, BF16), plan=plan,
                  mnk=(x.shape[0], D_FF, D_MODEL), tn=UP_SHARD,
                  b_spec=lambda tn, tk: pl.BlockSpec((None, tk, tn), lambda i, j, kk: (j, kk, 0)),
                  epilogue=lambda acc: (acc, jnp.square(jnp.maximum(acc, 0.0))))
    f = _mm(act, _late(w, "w_down"), mode="nn", name=n + "down", plan=plan)
    saved = dict(x=x, h1=h1, za=za, zg=zg, yc=yc, qn=qn, kvn=kvn, krr=krr, q=q, kv=kv, o=o, lse=lse, act_cat=act_cat,
                 y=y, merged=merged, m2=m2, x1=x1, h2=h2, up=up, act=act, f=f)
    return x1, f, saved


def _layer_bwd(dx_out, df, sv, w, tabs, l, prev, plan, rs):
    ck, sk, cq, sq = tabs
    n = f"l{l}_b_"
    g = {}
    own, half = {}, {}

    def to_sibling(kernel, names):
        def done(results):
            for name, t in zip(names, results[0]):
                half[name] = _rs_chip_sum(own[name], t, f"l{l}_chip_sum_{name}")
        plan.at(n + kernel, lambda: [_scatter_sibling([own[name] for name in names])], done)

    def to_chips(kernel, names, prefix=n):
        def done(results):
            for name, t in zip(names, results[0]):
                rs[(name, l)] = (half[name], t)
        plan.at(prefix + kernel, lambda: [_scatter_chips([half[name] for name in names])], done)

    own["w_down"] = _mm(sv["act"], df, mode="tn", name=n + "dw_down", out_dtypes=(BF16,), plan=plan).reshape(
        N_DEV, D_FF // N_DEV, D_MODEL)
    to_sibling("dup", ["w_down"])
    dup = _mm(df, _late(w, "w_down"), mode="nt", name=n + "dup", out_dtypes=(BF16,), extras=(sv["up"],), plan=plan,
              epilogue=lambda acc, up: (acc * 2.0 * jnp.maximum(up, 0.0),))
    to_chips("dw_up", ["w_down"])
    own["w_up"] = _mm(sv["h2"], dup, mode="tn", name=n + "dw_up", tn=UP_SHARD, plan=plan,
                      o_spec=lambda tm, tn: pl.BlockSpec((None, tm, tn), lambda i, j, kk: (j, i, 0)),
                      out_struct=jax.ShapeDtypeStruct((N_DEV, D_MODEL, UP_SHARD), BF16))
    to_sibling("dh2", ["w_up"])
    dh2 = _mm(dup, _late(w, "w_up"), mode="nt", name=n + "dh2", mnk=(dup.shape[0], D_MODEL, D_FF), tk=UP_SHARD,
              plan=plan, b_spec=lambda tn, tk: pl.BlockSpec((None, tn, tk), lambda i, j, kk: (kk, j, 0)))
    dx1, dm2, g["pre_mlp_g"], g["post_mix_g"] = _pre_bwd(dh2, sv["x1"], w["pre_mlp_g"], dx_out, n + "pre_mlp",
                                                           r_prev=sv["m2"], g_post_prev=w["post_mix_g"])
    own["w_out"] = _mm(sv["merged"], dm2, mode="tn", name=n + "dw_out", out_dtypes=(BF16,)).reshape(
        N_DEV, D_MODEL // N_DEV, D_MODEL)
    to_sibling("dmerged", ["w_out"])
    dmerged = _mm(dm2, _late(w, "w_out"), mode="nt", name=n + "dmerged", plan=plan)
    to_chips("merge", ["w_up"])
    dy, dzg = _merge_bwd(dmerged, sv["y"], sv["zg"], n + "merge", plan=plan)
    d_proj = _proj_bwd_w(sv["act_cat"], dy, n + "dw_proj")
    projs = ["pool_proj", "conv_proj", "sgu_proj", "attn_proj"]
    for i, name in enumerate(projs):
        own[name] = _cols_by_owner(d_proj[i * BR:(i + 1) * BR] if i < 3 else d_proj[3 * BR:])
    to_chips("dact", ["w_out"])
    to_sibling("dact", projs)
    dact = _proj_bwd_act(dy, _late(w, "proj_cat"), n + "dact", plan=plan)
    dz_pool, g["pool_w"], g["pool_scale"] = _pool_bwd(dact, sv["za"], w["pool_w"], w["pool_scale"], n + "pool")
    dyc, g["conv_b"], g["conv_norm_g"], g["conv_norm_b"] = _conv_bwd_norm(dact, sv["yc"], w["conv_norm_g"],
                                                                          w["conv_norm_b"], n + "conv_norm")
    dz_conv, g["conv_w"] = _conv_bwd_taps(dyc, sv["za"], w["conv_w"], n + "conv_taps")
    dz_sgu, g["sgu_w"], g["sgu_b"], g["sgu_norm_g"], g["sgu_norm_b"] = _sgu_bwd(
        dact, sv["za"], w["sgu_norm_g"], w["sgu_norm_b"], w["sgu_w"], w["sgu_bias"], n + "sgu")
    to_chips("flash", projs)
    dkv, dkr_heads, dq = _flash_bwd(sv["q"], sv["kv"], sv["krr"], dact, sv["o"], sv["lse"], cq, sq, n + "flash",
                                    plan=plan)
    d_uq = _mm(sv["qn"], dq, mode="tn", name=n + "dw_uq", out_dtypes=(BF16,))
    own["w_uq"] = d_uq.reshape(BR, MLA_HEADS, QW)[:, :, :QK_NOPE + QK_ROPE].transpose(1, 0, 2)
    dqn = _mm(dq, _late(w, "w_uq"), mode="nt", name=n + "dqn")
    own["w_ukv"] = _cols_by_owner(_mm(sv["kvn"], dkv, mode="tn", name=n + "dw_ukv", out_dtypes=(BF16,)))
    to_sibling("dkvn", ["w_uq", "w_ukv"])
    dkvn = _mm(dkv, _late(w, "w_ukv"), mode="nt", name=n + "dkvn", plan=plan)
    dz_mla, g["q_norm_g"], g["kv_norm_g"] = _mla_prep_bwd(dqn, dkvn, dkr_heads, sv["za"], w["q_norm_g"],
                                                           w["kv_norm_g"], ck, sk, n + "mla_prep")
    dza = jnp.concatenate([dz_pool, dz_conv, dz_sgu, dz_mla], axis=1)
    to_chips("dw_a", ["w_uq", "w_ukv"])
    d_a = _mm(sv["h1"], dza, mode="tn", name=n + "dw_a", out_dtypes=(BF16,), plan=plan)
    d_g = _mm(sv["h1"], dzg, mode="tn", name=n + "dw_g", out_dtypes=(BF16,))
    own["w_in"] = _cols_by_owner(jnp.concatenate([d_a[:, :N_IN_A], d_g], axis=1))
    to_sibling("dh1_a", ["w_in"])
    dh1 = _mm(dza, w["w_a"], mode="nt", name=n + "dh1_a", plan=plan)
    if prev is None:
        to_chips("dh1_g", ["w_in"])
    else:
        to_chips("flash", ["w_in"], prefix=f"l{l - 1}_b_")
    dh1 = _mm(dzg, w["w_g"], mode="nt", name=n + "dh1_g", extras=(dh1,), epilogue=lambda acc, e: (acc + e,), plan=plan)
    if prev is None:
        dx, g["pre_mix_g"] = _pre_bwd(dh1, sv["x"], w["pre_mix_g"], dx1, n + "pre_mix")
        return dx, None, g
    dx, df_prev, g["pre_mix_g"], g_prev_post = _pre_bwd(dh1, sv["x"], w["pre_mix_g"], dx1, n + "pre_mix",
                                                         r_prev=prev[0], g_post_prev=prev[1])
    g["prev_post_mlp_g"] = g_prev_post
    return dx, df_prev, g


MIDSIZE = ("pool_proj", "conv_proj", "sgu_proj", "w_uq", "w_ukv", "attn_proj")
GATHER_STEPS = (
    ("l0_za", tuple((name, 0) for name in MIDSIZE)),
    ("l0_zg", (("w_out", 0),) + tuple((name, 1) for name in MIDSIZE)),
    ("l0_flash", (("w_in", 1), ("w_up", 0))),
    ("l0_proj", (("w_down", 0),)),
    ("l0_up", (("w_up", 1),)),
    ("l0_down", (("w_out", 1),)),
    ("l1_za", ()),
    ("l1_zg", ()),
    ("l1_flash", (("w_down", 1),)),
    ("l1_proj", ()),
)


def _plan_gathers(plan, gat, shards):
    first_half = {}
    for step, (kernel, keys) in enumerate(GATHER_STEPS):
        before = GATHER_STEPS[step - 1][1] if step else ()
        if not keys and not before:
            continue

        def make(keys=keys, before=before):
            ops = [_gather_pass(first_half[before])] if before else []
            return ops + ([_gather_own([shards[key] for key in keys])] if keys else [])

        def done(results, keys=keys, before=before):
            if before:
                gat.update(zip(before, results[0]))
            if keys:
                first_half[keys] = results[-1]

        plan.at(kernel, make, done)


def _local_step(x, positions, target, gat, small, shards, plan):
    tabs = _rope_tables(positions)
    _plan_gathers(plan, gat, shards)
    ws = [_layer_weights(gat, small, 0)]
    saved = []
    h = _pre_norm(x, ws[0]["pre_mix_g"], "l0_pre_mix")
    cur = x
    for l in range(DEPTH):
        x1, f, sv = _layer_fwd(cur, h, ws[l], tabs, l, plan)
        saved.append(sv)
        if l + 1 < DEPTH:
            ws.append(_layer_weights(gat, small, l + 1))
            cur, h = _post_pre(x1, f, ws[l]["post_mlp_g"], ws[l + 1]["pre_mix_g"], f"l{l}_post_mlp")
    top = DEPTH - 1
    dx, df, dg_post, loss = _final_loss(saved[top]["x1"], saved[top]["f"], ws[top]["post_mlp_g"], target, "loss")
    grads = [None] * DEPTH
    post_mlp = {top: dg_post}
    rs = {}
    for l in range(top, -1, -1):
        prev = (saved[l - 1]["f"], ws[l - 1]["post_mlp_g"]) if l > 0 else None
        dx, df, g = _layer_bwd(dx, df, saved[l], ws[l], tabs, l, prev, plan, rs)
        if l > 0:
            post_mlp[l - 1] = g.pop("prev_post_mlp_g")
        grads[l] = g
    for l in range(DEPTH):
        grads[l]["post_mlp_g"] = post_mlp[l]
    assert not plan.jobs, sorted(plan.jobs)
    return loss[0, 0], dx, grads, rs


def _small_grads(grads):
    small = {}

    def stack(fn):
        return jnp.stack([fn(grads[l]) for l in range(DEPTH)])

    for name, shape in SMALL:
        if name == "sgu_b":
            small[name] = stack(lambda g: g["sgu_b"][:, :, 0])
        else:
            small[name] = stack(lambda g, name=name, shape=shape: g[name].reshape(shape))
    small["conv_w"] = stack(lambda g: g["conv_w"][:CONV_WIDTH])
    return small


SMALL_ROWS = sum(DEPTH * math.prod(shape) // 128 for _, shape in SMALL)
CONVW_ROWS = DEPTH * CONV_WIDTH * BR // 128


def _pack_small(parts):
    return jnp.concatenate([parts[name].astype(F32).reshape(-1, 128) for name, _ in SMALL], axis=0)


def _unpack_small(buf):
    out, off = {}, 0
    for name, shape in SMALL:
        rows = DEPTH * math.prod(shape) // 128
        out[name] = buf[off:off + rows].reshape((DEPTH,) + shape)
        off += rows
    return out


def _mesh_pos():
    return lax.axis_index("x"), lax.axis_index("y"), lax.axis_index("c")


def _all_gather(shards, name):
    n = len(shards)

    def body(*refs):
        x_refs, out_refs = refs[:n], refs[n:2 * n]
        send_sems, recv_sems, local_sems = refs[2 * n:]
        x, y, c = _mesh_pos()
        me, sibling = (x, y, c), (x, y, 1 - c)
        chips = [(1 - x, y), (x, 1 - y), (1 - x, 1 - y)]

        def rows(t, px, py, pc):
            return out_refs[t].at[4 * px + 2 * py + pc]

        def copy(t, k, block, to, src=None):
            return pltpu.make_async_remote_copy(
                src_ref=rows(t, *block) if src is None else src, dst_ref=rows(t, *block), send_sem=send_sems.at[t, k],
                recv_sem=recv_sems.at[t, k], device_id=to, device_id_type=MESH)

        mine = [pltpu.make_async_copy(x_refs[t], rows(t, *me), local_sems.at[t]) for t in range(n)]
        for cp in mine:
            cp.start()
        first = []
        for t in range(n):
            first.append(copy(t, 0, me, sibling, src=x_refs[t]))
            first += [copy(t, 1 + j, me, (*chip, c), src=x_refs[t]) for j, chip in enumerate(chips)]
        for cp in first:
            cp.start()
        passed = []
        for t in range(n):
            for j, chip in enumerate(chips):
                copy(t, 1 + j, (*chip, c), me).wait_recv()
                passed.append(copy(t, 4 + j, (*chip, c), sibling))
                passed[-1].start()
        for t in range(n):
            copy(t, 0, sibling, me).wait_recv()
            for j, chip in enumerate(chips):
                copy(t, 4 + j, (*chip, 1 - c), me).wait_recv()
        for cp in first + passed:
            cp.wait_send()
        for cp in mine:
            cp.wait()

    hbm = pl.BlockSpec(memory_space=pl.ANY)
    return pl.pallas_call(
        body, name=name, out_shape=[jax.ShapeDtypeStruct((N_DEV,) + a.shape, a.dtype) for a in shards],
        in_specs=[hbm] * n, out_specs=[hbm] * n,
        scratch_shapes=[pltpu.SemaphoreType.DMA((n, 7)), pltpu.SemaphoreType.DMA((n, 7)),
                        pltpu.SemaphoreType.DMA((n,))])(*shards)


def _row_tile(r, c_, cap_bytes=1 << 20):
    best = 0
    for t in range(16, r + 1, 16):
        if r % t == 0 and t * c_ * 4 <= cap_bytes:
            best = t
    return best if best else r


def _rs_chip_sum(g, t, name):
    _, r, c_ = g.shape
    tr = _row_tile(r, c_)
    core = lax.axis_index("c").astype(jnp.int32).reshape(1)

    def body(core_ref, g_ref, t_ref, p_ref):
        p_ref[...] = (g_ref[...].astype(F32) + t_ref[...].astype(F32)).astype(p_ref.dtype)

    return pl.pallas_call(
        body, name=name, out_shape=jax.ShapeDtypeStruct((4, r, c_), g.dtype),
        grid_spec=pltpu.PrefetchScalarGridSpec(
            num_scalar_prefetch=1, grid=(4, r // tr),
            in_specs=[pl.BlockSpec((1, tr, c_), lambda k, i, core_ref: (2 * k + core_ref[0], i, 0)),
                      pl.BlockSpec((1, tr, c_), lambda k, i, core_ref: (k, i, 0))],
            out_specs=pl.BlockSpec((1, tr, c_), lambda k, i, core_ref: (k, i, 0))),
        compiler_params=_params(("parallel", "parallel")))(core, g, t)


def _adamw_math(w, g, m, v):
    m = ADAM_B1 * m + (1.0 - ADAM_B1) * g
    v = ADAM_B2 * v + (1.0 - ADAM_B2) * jnp.square(g)
    m_hat = m / (1.0 - ADAM_B1 ** ADAM_STEP)
    v_hat = v / (1.0 - ADAM_B2 ** ADAM_STEP)
    delta = -ADAM_LR * (m_hat / (jnp.sqrt(v_hat) + ADAM_EPS) + ADAM_WD * w)
    return delta, m, v


def _adamw_big(ps, ts, w, m, v, name):
    _, r, c_ = w.shape
    tr = _row_tile(r, c_)
    chip = (2 * lax.axis_index("x") + lax.axis_index("y")).astype(jnp.int32).reshape(1)

    def body(chip_ref, p0, t0, p1, t1, w_ref, m_ref, v_ref, g_out, d_out, m_out, v_out):
        def update(p_ref, t_ref):
            g = p_ref[0].astype(F32) + t_ref[0].astype(F32) + t_ref[1].astype(F32) + t_ref[2].astype(F32)
            g_out[0] = g
            d_out[0], m_out[0], v_out[0] = _adamw_math(w_ref[0], g, m_ref[0], v_ref[0])

        pl.when(pl.program_id(0) == 0)(functools.partial(update, p0, t0))
        pl.when(pl.program_id(0) == 1)(functools.partial(update, p1, t1))

    def grad_specs(layer):
        return [pl.BlockSpec((1, tr, c_), lambda l, i, chip_ref: (chip_ref[0], jnp.where(l == layer, i, 0), 0)),
                pl.BlockSpec((3, tr, c_), lambda l, i, chip_ref: (0, jnp.where(l == layer, i, 0), 0))]

    nat = pl.BlockSpec((1, tr, c_), lambda l, i, chip_ref: (l, i, 0))
    shape = jax.ShapeDtypeStruct(w.shape, F32)
    return pl.pallas_call(
        body, name=name, out_shape=[shape] * 4,
        grid_spec=pltpu.PrefetchScalarGridSpec(
            num_scalar_prefetch=1, grid=(DEPTH, r // tr),
            in_specs=grad_specs(0) + grad_specs(1) + [nat, nat, nat], out_specs=[nat] * 4),
        compiler_params=_params(("parallel", "parallel"), VMEM_BIG))(chip, ps[0], ts[0], ps[1], ts[1], w, m, v)


def _sum_devices(parts, name):
    _, r, c_ = parts.shape

    def body(p_ref, o_ref):
        acc = p_ref[0]
        for d in range(1, N_DEV):
            acc = acc + p_ref[d]
        o_ref[...] = acc

    return pl.pallas_call(body, name=name, out_shape=jax.ShapeDtypeStruct((r, c_), F32),
                          compiler_params=_params(None, VMEM_BIG))(parts)


def _adamw_small(w, g, m, v, name):
    def body(w_ref, g_ref, m_ref, v_ref, d_out, m_out, v_out):
        d_out[...], m_out[...], v_out[...] = _adamw_math(w_ref[...], g_ref[...], m_ref[...], v_ref[...])

    shape = jax.ShapeDtypeStruct(w.shape, F32)
    return pl.pallas_call(body, name=name, out_shape=[shape] * 3)(w, g, m, v)


def kernel(x, positions, pre_mix_g, w_in, pool_w, pool_scale, pool_proj, conv_w, conv_b, conv_norm_g, conv_norm_b, conv_proj, sgu_norm_g, sgu_norm_b, sgu_w, sgu_b, sgu_proj, q_norm_g, w_uq, kv_norm_g, w_ukv, attn_proj, w_out, post_mix_g, pre_mlp_g, w_up, w_down, post_mlp_g, loss_target, m_pre_mix_g, m_w_in, m_pool_w, m_pool_scale, m_pool_proj, m_conv_w, m_conv_b, m_conv_norm_g, m_conv_norm_b, m_conv_proj, m_sgu_norm_g, m_sgu_norm_b, m_sgu_w, m_sgu_b, m_sgu_proj, m_q_norm_g, m_w_uq, m_kv_norm_g, m_w_ukv, m_attn_proj, m_w_out, m_post_mix_g, m_pre_mlp_g, m_w_up, m_w_down, m_post_mlp_g, v_pre_mix_g, v_w_in, v_pool_w, v_pool_scale, v_pool_proj, v_conv_w, v_conv_b, v_conv_norm_g, v_conv_norm_b, v_conv_proj, v_sgu_norm_g, v_sgu_norm_b, v_sgu_w, v_sgu_b, v_sgu_proj, v_q_norm_g, v_w_uq, v_kv_norm_g, v_w_ukv, v_attn_proj, v_w_out, v_post_mix_g, v_pre_mlp_g, v_w_up, v_w_down, v_post_mlp_g):
    args = dict(locals())
    wts = {n: args[n] for n in WEIGHTS}
    mom1 = {n: args["m_" + n] for n in WEIGHTS}
    mom2 = {n: args["v_" + n] for n in WEIGHTS}
    dev = 4 * lax.axis_index("x") + 2 * lax.axis_index("y") + lax.axis_index("c")

    shards = {(name, l): wts[name][l].astype(BF16) for name, _, _ in BIG for l in range(DEPTH)}
    taps = jnp.pad(conv_w.reshape(-1, 128), ((0, 1), (0, 0)))
    gathered = _all_gather([shards[("w_in", 0)], taps], "gather_first")
    gat = {("w_in", 0): gathered[0]}
    taps = gathered[-1][:, :CONV_WIDTH].reshape(N_DEV, DEPTH, CONV_WIDTH, BR // N_DEV)
    small = {n: wts[n] for n, _ in SMALL}
    small["conv_w"] = taps.transpose(1, 2, 0, 3).reshape(DEPTH, CONV_WIDTH, BR)

    loss_part, grad_x, grads, rs = _local_step(x[0], positions, loss_target[0], gat, small, shards, _Plan())
    small_g = _small_grads(grads)
    loss = lax.psum(loss_part, ("x", "y", "c"))

    out = {"grad": {}, "delta": {}, "new_m": {}, "new_v": {}}
    for name, _, _ in BIG:
        res = _adamw_big([rs[(name, l)][0] for l in range(DEPTH)], [rs[(name, l)][1] for l in range(DEPTH)],
                         wts[name], mom1[name], mom2[name], "adamw_" + name)
        for key, buf in zip(("grad", "delta", "new_m", "new_v"), res):
            out[key][name] = buf

    part = jnp.concatenate([_pack_small(small_g), small_g["conv_w"].reshape(-1, 128)], axis=0)
    total = _sum_devices(_all_gather([part], "gather_small_grads")[0], "sum_small_grads")
    g_small = total[:SMALL_ROWS]
    d_small, m_small, v_small = _adamw_small(_pack_small(wts), g_small, _pack_small(mom1), _pack_small(mom2), "adamw_small")
    for key, buf in (("grad", g_small), ("delta", d_small), ("new_m", m_small), ("new_v", v_small)):
        out[key].update(_unpack_small(buf))
    g_taps = total[SMALL_ROWS:].reshape(DEPTH, CONV_WIDTH, N_DEV, BR // N_DEV)
    g_taps = lax.dynamic_index_in_dim(g_taps, dev, axis=2, keepdims=False)
    flat = lambda a: a.reshape(-1, 128)
    d_taps, m_taps, v_taps = _adamw_small(flat(conv_w), flat(g_taps), flat(m_conv_w), flat(v_conv_w), "adamw_taps")
    for key, buf in (("grad", g_taps), ("delta", d_taps), ("new_m", m_taps), ("new_v", v_taps)):
        out[key]["conv_w"] = buf.reshape(conv_w.shape)

    return (loss, grad_x[None], *[out["grad"][n] for n in WEIGHTS], *[out["delta"][n] for n in WEIGHTS],
            *[out["new_m"][n] for n in WEIGHTS], *[out["new_v"][n] for n in WEIGHTS])
```

```python
import collections
import functools
import math

import jax
import jax.numpy as jnp
from jax import lax
from jax.experimental import pallas as pl
from jax.experimental.pallas import tpu as pltpu

F32 = jnp.float32
BF16 = jnp.bfloat16

D_MODEL = 2048
DEPTH = 2
EPS = 1e-6
N_BRANCH = 4
D_FF = 4 * D_MODEL
UP_SHARD = D_FF // 8
POOL_WINDOWS = (2, 4, 8, 16)
CONV_WIDTH = 31
CHUNK = 128
MLA_HEADS = 8
QK_NOPE = 128
QK_ROPE = 64
V_DIM = 128
ROPE_THETA = 10000.0
GDIM = 128
BR = 512
N_IN_A = 3648
ZA = 3712
N_GATE = N_BRANCH * D_MODEL
N_IN = N_IN_A + N_GATE
QW = 256
ACT_CAT = 3 * BR + MLA_HEADS * V_DIM
ATT_SCALE = (QK_NOPE + QK_ROPE) ** -0.5
NEG = -1e30

ADAM_LR = 0.001
ADAM_B1 = 0.9
ADAM_B2 = 0.999
ADAM_EPS = 1e-08
ADAM_WD = 0.01
ADAM_STEP = 10

N_DEV = 8
PACK_W = 1024
VMEM_BIG = 48 * 1024 * 1024
MESH = pl.DeviceIdType.MESH

BIG = (
    ("w_in", 1, (2048, 1480)),
    ("pool_proj", 1, (512, 256)),
    ("conv_proj", 1, (512, 256)),
    ("sgu_proj", 1, (512, 256)),
    ("w_uq", 1, (512, 192)),
    ("w_ukv", 1, (512, 256)),
    ("attn_proj", 1, (1024, 256)),
    ("w_out", 0, (256, 2048)),
    ("w_up", 1, (2048, 1024)),
    ("w_down", 0, (1024, 2048)),
)
SMALL = (
    ("pre_mix_g", (2048,)), ("pool_w", (4, 128, 128)), ("pool_scale", (512,)), ("conv_b", (512,)),
    ("conv_norm_g", (512,)), ("conv_norm_b", (512,)), ("sgu_norm_g", (512,)), ("sgu_norm_b", (512,)),
    ("sgu_w", (4, 128, 128)), ("sgu_b", (4, 128)), ("q_norm_g", (512,)), ("kv_norm_g", (512,)),
    ("post_mix_g", (2048,)), ("pre_mlp_g", (2048,)), ("post_mlp_g", (2048,)),
)
WEIGHTS = ("pre_mix_g", "w_in", "pool_w", "pool_scale", "pool_proj", "conv_w", "conv_b", "conv_norm_g", "conv_norm_b",
           "conv_proj", "sgu_norm_g", "sgu_norm_b", "sgu_w", "sgu_b", "sgu_proj", "q_norm_g", "w_uq", "kv_norm_g",
           "w_ukv", "attn_proj", "w_out", "post_mix_g", "pre_mlp_g", "w_up", "w_down", "post_mlp_g")


def _params(sem=None, vmem=None):
    return pltpu.CompilerParams(dimension_semantics=sem, vmem_limit_bytes=vmem)


def _tile(dim, pref):
    if dim <= pref:
        return dim
    best = 0
    for t in range(128, pref + 1, 128):
        if dim % t == 0:
            best = t
    return best if best >= 256 else dim


def _sigmoid(x):
    return 1.0 / (1.0 + jnp.exp(-x))


def _gelu(x):
    k = math.sqrt(2.0 / math.pi)
    return 0.5 * x * (1.0 + jnp.tanh(k * (x + 0.044715 * x * x * x)))


def _gelu_grad(x):
    k = math.sqrt(2.0 / math.pi)
    t = jnp.tanh(k * (x + 0.044715 * x * x * x))
    return 0.5 * (1.0 + t) + 0.5 * x * (1.0 - t * t) * k * (1.0 + 3.0 * 0.044715 * x * x)


def _rms(x, g):
    r = lax.rsqrt(jnp.mean(x * x, axis=-1, keepdims=True) + EPS)
    return x * r * g


def _rms_bwd(x, g, dy):
    r = lax.rsqrt(jnp.mean(x * x, axis=-1, keepdims=True) + EPS)
    dyg = dy * g
    dx = r * dyg - x * (r * r * r) * jnp.mean(dyg * x, axis=-1, keepdims=True)
    return dx, dy * x * r


def _ln_stats(x):
    mu = jnp.mean(x, axis=-1, keepdims=True)
    xc = x - mu
    r = lax.rsqrt(jnp.mean(xc * xc, axis=-1, keepdims=True) + EPS)
    return xc * r, r


def _ln_bwd(xh, r, g, dy):
    dxh = dy * g
    return r * (dxh - jnp.mean(dxh, axis=-1, keepdims=True) - xh * jnp.mean(dxh * xh, axis=-1, keepdims=True))


def _rot_half(x, width, off):
    n = x.shape[-1]
    lane = lax.broadcasted_iota(jnp.int32, x.shape, x.ndim - 1) % width
    return jnp.where(lane - off < QK_ROPE // 2, -pltpu.roll(x, n - QK_ROPE // 2, x.ndim - 1),
                     pltpu.roll(x, QK_ROPE // 2, x.ndim - 1))


def _colsum_into(ref, val, first):
    s = jnp.sum(val, axis=0, keepdims=True)

    @pl.when(first)
    def _():
        ref[...] = s

    @pl.when(jnp.logical_not(first))
    def _():
        ref[...] += s


Exchange = collections.namedtuple("Exchange", "inputs out_shapes aliases n_pairs n_local build")


class _Plan:
    def __init__(self):
        self.jobs = {}

    def at(self, kernel, make, done):
        self.jobs.setdefault(kernel, []).append((make, done))

    def take(self, kernel):
        return self.jobs.pop(kernel, [])


def _pcall(body, operands, *, name, grid, in_specs, out_specs, out_shape, scratch_shapes=(), sem=None, vmem=None,
           plan=None):
    jobs = plan.take(name) if plan is not None else []
    if not jobs:
        return pl.pallas_call(body, name=name, grid=grid, in_specs=list(in_specs), out_specs=list(out_specs),
                              out_shape=list(out_shape), scratch_shapes=list(scratch_shapes),
                              compiler_params=_params(sem, vmem))(*operands)
    made = [(make(), done) for make, done in jobs]
    comm = [op for ops, _ in made for op in ops]
    n_in, n_out, n_scr = len(in_specs), len(out_shape), len(scratch_shapes)
    c_in = [a for op in comm for a in op.inputs]
    c_out = [s for op in comm for s in op.out_shapes]
    sems, aliases, i_off, o_off = [], {}, n_in, n_out
    for op in comm:
        sems += [pltpu.SemaphoreType.DMA((op.n_pairs,)), pltpu.SemaphoreType.DMA((op.n_pairs,)),
                 pltpu.SemaphoreType.DMA((max(op.n_local, 1),))]
        for src, dst in op.aliases.items():
            aliases[i_off + src] = o_off + dst
        i_off += len(op.inputs)
        o_off += len(op.out_shapes)

    def carrier(*refs):
        ins, cins = refs[:n_in], refs[n_in:n_in + len(c_in)]
        base = n_in + len(c_in)
        outs, couts = refs[base:base + n_out], refs[base + n_out:base + n_out + len(c_out)]
        base += n_out + len(c_out)
        scr, csems = refs[base:base + n_scr], refs[base + n_scr:]
        ids = [pl.program_id(ax) for ax in range(len(grid))]
        first = functools.reduce(jnp.logical_and, [i == 0 for i in ids])
        last = functools.reduce(jnp.logical_and, [i == g - 1 for i, g in zip(ids, grid)])

        def pieces():
            res, ci, co = [], 0, 0
            for k, op in enumerate(comm):
                res.append(op.build(cins[ci:ci + len(op.inputs)], couts[co:co + len(op.out_shapes)],
                                    *csems[3 * k:3 * k + 3]))
                ci += len(op.inputs)
                co += len(op.out_shapes)
            return res

        @pl.when(first)
        def _():
            for sends, _, local in pieces():
                for cp in local + sends:
                    cp.start()

        body(*ins, *outs, *scr)

        @pl.when(last)
        def _():
            for sends, recvs, local in pieces():
                for cp in recvs:
                    cp.wait_recv()
                for cp in sends:
                    cp.wait_send()
                for cp in local:
                    cp.wait()

    hbm = pl.BlockSpec(memory_space=pl.ANY)
    res = pl.pallas_call(
        carrier, name=name, grid=grid, in_specs=list(in_specs) + [hbm] * len(c_in),
        out_specs=list(out_specs) + [hbm] * len(c_out), out_shape=list(out_shape) + c_out,
        scratch_shapes=list(scratch_shapes) + sems, input_output_aliases=aliases,
        compiler_params=_params(("arbitrary",) * len(grid), vmem))(*operands, *c_in)
    pos = n_out
    for ops, done in made:
        results = []
        for op in ops:
            results.append(list(res[pos:pos + len(op.out_shapes)]))
            pos += len(op.out_shapes)
        done(results)
    return list(res[:n_out])


def _block(ref, px, py, pc):
    return ref.at[4 * px + 2 * py + pc]


def _remote(src, dst, send_sems, recv_sems, k, to):
    return pltpu.make_async_remote_copy(src_ref=src, dst_ref=dst, send_sem=send_sems.at[k], recv_sem=recv_sems.at[k],
                                        device_id=to, device_id_type=MESH)


def _gather_own(shards):
    n = len(shards)

    def build(ins, outs, send_sems, recv_sems, local_sems):
        x, y, c = _mesh_pos()
        peers = [(x, y, 1 - c), (1 - x, y, c), (x, 1 - y, c), (1 - x, 1 - y, c)]
        sends, recvs, local = [], [], []
        for t in range(n):
            local.append(pltpu.make_async_copy(ins[t], _block(outs[t], x, y, c), local_sems.at[t]))
            for k, peer in enumerate(peers):
                sends.append(_remote(ins[t], _block(outs[t], x, y, c), send_sems, recv_sems, 4 * t + k, peer))
                recvs.append(_remote(ins[t], _block(outs[t], *peer), send_sems, recv_sems, 4 * t + k, peer))
        return sends, recvs, local

    return Exchange(list(shards), [jax.ShapeDtypeStruct((N_DEV,) + a.shape, a.dtype) for a in shards], {}, 4 * n, n, build)


def _gather_pass(bufs):
    n = len(bufs)

    def build(ins, outs, send_sems, recv_sems, local_sems):
        x, y, c = _mesh_pos()
        chips = [(1 - x, y), (x, 1 - y), (1 - x, 1 - y)]
        sends, recvs = [], []
        for t in range(n):
            for j, chip in enumerate(chips):
                mine, theirs = _block(outs[t], *chip, c), _block(outs[t], *chip, 1 - c)
                sends.append(_remote(mine, mine, send_sems, recv_sems, 3 * t + j, (x, y, 1 - c)))
                recvs.append(_remote(mine, theirs, send_sems, recv_sems, 3 * t + j, (x, y, 1 - c)))
        return sends, recvs, []

    return Exchange(list(bufs), [jax.ShapeDtypeStruct(a.shape, a.dtype) for a in bufs], {t: t for t in range(n)},
                    3 * n, 0, build)


def _scatter_sibling(gs):
    n = len(gs)

    def build(ins, outs, send_sems, recv_sems, local_sems):
        x, y, c = _mesh_pos()
        sends, recvs = [], []
        for t in range(n):
            for k in range(4):
                cp = _remote(ins[t].at[2 * k + 1 - c], outs[t].at[k], send_sems, recv_sems, 4 * t + k, (x, y, 1 - c))
                sends.append(cp)
                recvs.append(cp)
        return sends, recvs, []

    return Exchange(list(gs), [jax.ShapeDtypeStruct((4,) + g.shape[1:], g.dtype) for g in gs], {}, 4 * n, 0, build)


def _scatter_chips(ps):
    n = len(ps)

    def build(ins, outs, send_sems, recv_sems, local_sems):
        x, y, c = _mesh_pos()
        chips = [(1 - x, y), (x, 1 - y), (1 - x, 1 - y)]
        sends, recvs = [], []
        for t in range(n):
            for j, (cx, cy) in enumerate(chips):
                cp = _remote(ins[t].at[2 * cx + cy], outs[t].at[j], send_sems, recv_sems, 3 * t + j, (cx, cy, c))
                sends.append(cp)
                recvs.append(cp)
        return sends, recvs, []

    return Exchange(list(ps), [jax.ShapeDtypeStruct((3,) + p.shape[1:], p.dtype) for p in ps], {}, 3 * n, 0, build)


_DIMS = {"nn": ((1,), (0,)), "nt": ((1,), (1,)), "tn": ((0,), (0,))}


def _mm_call(a, b, *, mode, name, grid, kaxis, nk, a_spec, b_spec, o_specs, out_shape, acc_shape,
             extras=(), e_specs=(), epilogue=None, active=None, plan=None):
    ne, no = len(extras), len(out_shape)

    def body(a_ref, b_ref, *rest):
        e_refs, o_refs, acc_ref = rest[:ne], rest[ne:ne + no], rest[ne + no]
        ids = [pl.program_id(ax) for ax in range(len(grid))]
        k = ids[kaxis]

        def finish(acc):
            outs = (acc,) if epilogue is None else epilogue(acc, *[e[...] for e in e_refs])
            for o_ref, val in zip(o_refs, outs):
                o_ref[...] = val.astype(o_ref.dtype)

        def step():
            prod = lax.dot_general(a_ref[...], b_ref[...], (_DIMS[mode], ((), ())), preferred_element_type=F32)
            if nk == 1:
                finish(prod)
                return

            @pl.when(k == 0)
            def _():
                acc_ref[...] = prod

            @pl.when(k > 0)
            def _():
                acc_ref[...] += prod

        if active is None:
            step()
        else:
            pl.when(active(*ids))(step)
        if nk > 1:
            @pl.when(k == nk - 1)
            def _():
                finish(acc_ref[...])

    sem = tuple("arbitrary" if ax == kaxis else "parallel" for ax in range(len(grid)))
    scratch = pltpu.VMEM(acc_shape if nk > 1 else (8, 128), F32)
    return _pcall(body, (a, b, *extras), name=name, grid=grid, in_specs=[a_spec, b_spec, *e_specs],
                  out_specs=list(o_specs), out_shape=list(out_shape), scratch_shapes=[scratch],
                  sem=sem, vmem=VMEM_BIG, plan=plan)


def _mm(a, b, *, mode, name, out_dtypes=(F32,), extras=(), epilogue=None, tm=1024, tn=1024, tk=2048,
        mnk=None, b_spec=None, o_spec=None, out_struct=None, plan=None):
    if mnk is not None:
        m, n, k = mnk
    elif mode == "nn":
        (m, k), (_, n) = a.shape, b.shape
    elif mode == "nt":
        (m, k), (n, _) = a.shape, b.shape
    else:
        (k, m), (_, n) = a.shape, b.shape
    tm, tn, tk = _tile(m, tm), _tile(n, tn), _tile(k, tk)
    if tn > 2048:
        tm, tk = _tile(m, 512), _tile(k, 512)
    if tk > 2048:
        tm, tn = _tile(m, 512), _tile(n, 512)
    nk = k // tk
    if mode == "tn":
        a_spec = pl.BlockSpec((tk, tm), lambda i, j, kk: (kk, i))
    else:
        a_spec = pl.BlockSpec((tm, tk), lambda i, j, kk: (i, kk))
    if b_spec is not None:
        b_spec = b_spec(tn, tk)
    elif mode == "nt":
        b_spec = pl.BlockSpec((tn, tk), lambda i, j, kk: (j, kk))
    else:
        b_spec = pl.BlockSpec((tk, tn), lambda i, j, kk: (kk, j))

    def e_spec(e):
        if e.shape[1] == tn and n != tn:
            return pl.BlockSpec((tm, tn), lambda i, j, kk: (i, 0))
        return pl.BlockSpec((tm, tn), lambda i, j, kk: (i, j))

    e_specs = [e_spec(e) for e in extras]
    if o_spec is not None:
        o_specs = [o_spec(tm, tn)]
        out_shape = [out_struct]
    else:
        o_specs = [pl.BlockSpec((tm, tn), lambda i, j, kk: (i, j)) for _ in out_dtypes]
        out_shape = [jax.ShapeDtypeStruct((m, n), dt) for dt in out_dtypes]
    outs = _mm_call(a, b, mode=mode, name=name, grid=(m // tm, n // tn, nk), kaxis=2, nk=nk, a_spec=a_spec,
                    b_spec=b_spec, o_specs=o_specs, out_shape=out_shape, acc_shape=(tm, tn), extras=extras,
                    e_specs=e_specs, epilogue=epilogue, plan=plan)
    return outs[0] if len(outs) == 1 else outs


def _branch_of(kb):
    return jnp.minimum(kb, N_BRANCH - 1)


def _proj_merge_fwd(act_cat, proj_cat, zg, name, plan=None):
    s = act_cat.shape[0]
    tm, tn = _tile(s, 1024), 1024
    nj = D_MODEL // tn
    last = N_BRANCH - 1

    def kb(b, k):
        return jnp.where(b < last, b, last + k)

    def body(a_ref, b_ref, z_ref, y_ref, m_ref, acc_ref, sum_ref):
        b, k = pl.program_id(2), pl.program_id(3)

        @pl.when(jnp.logical_or(b == last, k == 0))
        def _():
            prod = jnp.dot(a_ref[...], b_ref[...], preferred_element_type=F32)

            @pl.when(k == 0)
            def _():
                acc_ref[...] = prod

            @pl.when(k > 0)
            def _():
                acc_ref[...] += prod

        @pl.when(k == 1)
        def _():
            y = acc_ref[...]
            y_ref[...] = y.astype(BF16)
            gated = _sigmoid(z_ref[...]) * y

            @pl.when(b == 0)
            def _():
                sum_ref[...] = gated

            @pl.when(b > 0)
            def _():
                sum_ref[...] += gated

            @pl.when(b == last)
            def _():
                m_ref[...] = sum_ref[...].astype(BF16)

    wide = pl.BlockSpec((tm, tn), lambda i, j, b, k: (i, b * nj + j))
    y, merged = _pcall(
        body, (act_cat, proj_cat, zg), name=name, grid=(s // tm, nj, N_BRANCH, 2),
        in_specs=[pl.BlockSpec((tm, BR), lambda i, j, b, k: (i, kb(b, k))),
                  pl.BlockSpec((BR, tn), lambda i, j, b, k: (kb(b, k), j)), wide],
        out_specs=[wide, pl.BlockSpec((tm, tn), lambda i, j, b, k: (i, j))],
        out_shape=[jax.ShapeDtypeStruct((s, N_GATE), BF16), jax.ShapeDtypeStruct((s, D_MODEL), BF16)],
        scratch_shapes=[pltpu.VMEM((tm, tn), F32), pltpu.VMEM((tm, tn), F32)],
        sem=("parallel", "parallel", "arbitrary", "arbitrary"), vmem=VMEM_BIG, plan=plan)
    return y, merged


def _proj_bwd_act(dy, proj_cat, name, plan=None):
    s = dy.shape[0]
    tm, tk = _tile(s, 1024), 1024
    nkk = D_MODEL // tk
    nkb = ACT_CAT // BR
    out = _mm_call(
        dy, proj_cat, mode="nt", name=name, grid=(s // tm, nkb, nkk), kaxis=2, nk=nkk,
        a_spec=pl.BlockSpec((tm, tk), lambda i, kb, k: (i, _branch_of(kb) * nkk + k)),
        b_spec=pl.BlockSpec((BR, tk), lambda i, kb, k: (kb, k)),
        o_specs=[pl.BlockSpec((tm, BR), lambda i, kb, k: (i, kb))],
        out_shape=[jax.ShapeDtypeStruct((s, ACT_CAT), F32)], acc_shape=(tm, BR), plan=plan)
    return out[0]


def _proj_bwd_w(act_cat, dy, name):
    s = dy.shape[0]
    tms, tn = _tile(s, 1024), 1024
    nj = D_MODEL // tn
    nkb = ACT_CAT // BR
    nm = s // tms
    out = _mm_call(
        act_cat, dy, mode="tn", name=name, grid=(nkb, nj, nm), kaxis=2, nk=nm,
        a_spec=pl.BlockSpec((tms, BR), lambda kb, j, m: (m, kb)),
        b_spec=pl.BlockSpec((tms, tn), lambda kb, j, m: (m, _branch_of(kb) * nj + j)),
        o_specs=[pl.BlockSpec((BR, tn), lambda kb, j, m: (kb, j))],
        out_shape=[jax.ShapeDtypeStruct((ACT_CAT, D_MODEL), BF16)], acc_shape=(BR, tn))
    return out[0]


def _row_specs(ts, n_full, n_vec):
    return ([pl.BlockSpec((ts, D_MODEL), lambda i: (i, 0))] * n_full
            + [pl.BlockSpec((1, D_MODEL), lambda i: (0, 0))] * n_vec)


def _pre_norm(x, g, name):
    s = x.shape[0]
    ts = _tile(s, 256)

    def body(x_ref, g_ref, h_ref):
        h_ref[...] = _rms(x_ref[...], g_ref[...]).astype(BF16)

    return pl.pallas_call(body, name=name, grid=(s // ts,), in_specs=_row_specs(ts, 1, 1),
                          out_specs=pl.BlockSpec((ts, D_MODEL), lambda i: (i, 0)),
                          out_shape=jax.ShapeDtypeStruct((s, D_MODEL), BF16), compiler_params=_params(("parallel",)))(x, g)


def _post_pre(x, r, g_post, g_next, name):
    s = x.shape[0]
    ts = _tile(s, 256)

    def body(x_ref, r_ref, gp_ref, gn_ref, xn_ref, h_ref):
        xn = x_ref[...] + _rms(r_ref[...], gp_ref[...])
        xn_ref[...] = xn
        h_ref[...] = _rms(xn, gn_ref[...]).astype(BF16)

    spec = pl.BlockSpec((ts, D_MODEL), lambda i: (i, 0))
    return pl.pallas_call(body, name=name, grid=(s // ts,), in_specs=_row_specs(ts, 2, 2), out_specs=[spec, spec],
                          out_shape=[jax.ShapeDtypeStruct((s, D_MODEL), F32), jax.ShapeDtypeStruct((s, D_MODEL), BF16)],
                          compiler_params=_params(("parallel",)))(x, r, g_post, g_next)


def _final_loss(x, r, g_post, target, name):
    s = x.shape[0]
    ts = _tile(s, 256)

    def body(x_ref, r_ref, gp_ref, t_ref, dy_ref, dr_ref, dg_ref, loss_ref):
        first = pl.program_id(0) == 0
        rv, gp = r_ref[...], gp_ref[...]
        diff = x_ref[...] + _rms(rv, gp) - t_ref[...]
        part = 0.5 * jnp.sum(jnp.mean(diff * diff, axis=-1, keepdims=True), axis=0, keepdims=True)
        dy = diff * (1.0 / D_MODEL)
        dy_ref[...] = dy
        dr, dg_rows = _rms_bwd(rv, gp, dy)
        dr_ref[...] = dr.astype(BF16)
        _colsum_into(dg_ref, dg_rows, first)
        _colsum_into(loss_ref, jnp.broadcast_to(part, (1, 128)), first)

    spec = pl.BlockSpec((ts, D_MODEL), lambda i: (i, 0))
    vec = pl.BlockSpec((1, D_MODEL), lambda i: (0, 0))
    return pl.pallas_call(
        body, name=name, grid=(s // ts,), in_specs=[spec, spec, vec, spec],
        out_specs=[spec, spec, vec, pl.BlockSpec((1, 128), lambda i: (0, 0))],
        out_shape=[jax.ShapeDtypeStruct((s, D_MODEL), F32), jax.ShapeDtypeStruct((s, D_MODEL), BF16),
                   jax.ShapeDtypeStruct((1, D_MODEL), F32), jax.ShapeDtypeStruct((1, 128), F32)],
        compiler_params=_params(("arbitrary",)))(x, r, g_post, target)


def _pre_bwd(dh, x, g_pre, dx_res, name, r_prev=None, g_post_prev=None):
    s = x.shape[0]
    ts = _tile(s, 256)
    chain = r_prev is not None

    def body(*refs):
        if chain:
            dh_ref, x_ref, res_ref, r_ref, g_ref, gp_ref, dx_ref, dr_ref, dg_ref, dgp_ref = refs
        else:
            dh_ref, x_ref, res_ref, g_ref, dx_ref, dg_ref = refs
        first = pl.program_id(0) == 0
        dxn, dg_rows = _rms_bwd(x_ref[...], g_ref[...], dh_ref[...])
        dx = res_ref[...] + dxn
        dx_ref[...] = dx
        _colsum_into(dg_ref, dg_rows, first)
        if chain:
            dr, dgp_rows = _rms_bwd(r_ref[...], gp_ref[...], dx)
            dr_ref[...] = dr.astype(BF16)
            _colsum_into(dgp_ref, dgp_rows, first)

    spec = pl.BlockSpec((ts, D_MODEL), lambda i: (i, 0))
    vec = pl.BlockSpec((1, D_MODEL), lambda i: (0, 0))
    full = jax.ShapeDtypeStruct((s, D_MODEL), F32)
    vshape = jax.ShapeDtypeStruct((1, D_MODEL), F32)
    if chain:
        return pl.pallas_call(
            body, name=name, grid=(s // ts,), in_specs=[spec] * 4 + [vec] * 2, out_specs=[spec, spec, vec, vec],
            out_shape=[full, jax.ShapeDtypeStruct((s, D_MODEL), BF16), vshape, vshape],
            compiler_params=_params(("arbitrary",)))(dh, x, dx_res, r_prev, g_pre, g_post_prev)
    return pl.pallas_call(
        body, name=name, grid=(s // ts,), in_specs=[spec] * 3 + [vec], out_specs=[spec, vec],
        out_shape=[full, vshape], compiler_params=_params(("arbitrary",)))(dh, x, dx_res, g_pre)


def _merge_bwd(dm, y, zg, name, plan=None):
    s = y.shape[0]
    ts, tc = _tile(s, 512), 512
    nj = D_MODEL // tc

    def body(dm_ref, y_ref, z_ref, dy_ref, dz_ref):
        g = _sigmoid(z_ref[...])
        d = dm_ref[...]
        dy_ref[...] = (d * g).astype(BF16)
        dz_ref[...] = (d * y_ref[...] * g * (1.0 - g)).astype(BF16)

    blk = pl.BlockSpec((ts, tc), lambda i, j, b: (i, b * nj + j))
    shape = jax.ShapeDtypeStruct((s, N_GATE), BF16)
    return _pcall(body, (dm, y, zg), name=name, grid=(s // ts, nj, N_BRANCH),
                  in_specs=[pl.BlockSpec((ts, tc), lambda i, j, b: (i, j)), blk, blk], out_specs=[blk, blk],
                  out_shape=[shape, shape], sem=("parallel", "parallel", "parallel"), plan=plan)


POOL_HALO = 16


def _pool_windows(ext_ref, ts, first_row):
    outs = []
    t = first_row + lax.broadcasted_iota(jnp.int32, (ts, GDIM), 0)
    for gi, w in enumerate(POOL_WINDOWS):
        cols = pl.ds(gi * GDIM, GDIM)
        acc = ext_ref[pl.ds(POOL_HALO, ts), cols]
        cur = acc
        for k in range(1, w):
            acc = acc + ext_ref[pl.ds(POOL_HALO - k, ts), cols]
        cnt = jnp.minimum(t + 1, w).astype(F32)
        outs.append(acc / cnt - cur)
    return outs


def _pool_fwd(za, pool_w, pool_scale, name):
    s = za.shape[0]
    ts = _tile(s, 512)
    hb = ts // POOL_HALO

    def body(a_ref, halo_ref, w_ref, sc_ref, o_ref, ext_ref):
        i = pl.program_id(0)
        ext_ref[pl.ds(0, POOL_HALO), :] = jnp.where(i > 0, halo_ref[...], 0.0)
        ext_ref[pl.ds(POOL_HALO, ts), :] = a_ref[...]
        pooled = _pool_windows(ext_ref, ts, i * ts)
        for gi in range(len(POOL_WINDOWS)):
            mixed = jnp.dot(pooled[gi].astype(BF16), w_ref[gi].astype(BF16), preferred_element_type=F32)
            o_ref[:, pl.ds(gi * GDIM, GDIM)] = (mixed * sc_ref[:, pl.ds(gi * GDIM, GDIM)]).astype(BF16)

    return pl.pallas_call(
        body, name=name, grid=(s // ts,),
        in_specs=[pl.BlockSpec((ts, BR), lambda i: (i, 0)),
                  pl.BlockSpec((POOL_HALO, BR), lambda i: (jnp.maximum(i * hb - 1, 0), 0)),
                  pl.BlockSpec((4, GDIM, GDIM), lambda i: (0, 0, 0)), pl.BlockSpec((1, BR), lambda i: (0, 0))],
        out_specs=pl.BlockSpec((ts, BR), lambda i: (i, 0)), out_shape=jax.ShapeDtypeStruct((s, BR), BF16),
        scratch_shapes=[pltpu.VMEM((POOL_HALO + ts, BR), F32)], compiler_params=_params(("parallel",)))(
            za, za, pool_w, pool_scale)


def _pool_bwd(dact, za, pool_w, pool_scale, name):
    s = za.shape[0]
    ts = _tile(s, 512)
    hb = ts // POOL_HALO
    n_t = s // ts

    def body(d_ref, dhalo_ref, a_ref, halo_ref, w_ref, sc_ref, dz_ref, dw_ref, dsc_ref, ext_ref, f_ref):
        i = pl.program_id(0)
        first = i == 0
        ext_ref[pl.ds(0, POOL_HALO), :] = jnp.where(i > 0, halo_ref[...], 0.0)
        ext_ref[pl.ds(POOL_HALO, ts), :] = a_ref[...]
        pooled = _pool_windows(ext_ref, ts, i * ts)
        d_tile = d_ref[...]
        d_next = jnp.where(i < n_t - 1, dhalo_ref[...], 0.0)
        t_ext = i * ts + lax.broadcasted_iota(jnp.int32, (ts + POOL_HALO, GDIM), 0)
        dsc = []
        for gi, w in enumerate(POOL_WINDOWS):
            cols = pl.ds(gi * GDIM, GDIM)
            wg = w_ref[gi].astype(BF16)
            sc = sc_ref[:, cols]
            pg = pooled[gi].astype(BF16)
            mixed = jnp.dot(pg, wg, preferred_element_type=F32)
            dsc.append(jnp.sum(d_tile[:, gi * GDIM:(gi + 1) * GDIM] * mixed, axis=0, keepdims=True))
            dmix = jnp.concatenate([d_tile[:, gi * GDIM:(gi + 1) * GDIM], d_next[:, gi * GDIM:(gi + 1) * GDIM]], axis=0) * sc
            dmix = dmix.astype(BF16)
            dwg = lax.dot_general(pg, dmix[:ts], (((0,), (0,)), ((), ())), preferred_element_type=F32)

            @pl.when(first)
            def _():
                dw_ref[gi] = dwg

            @pl.when(jnp.logical_not(first))
            def _():
                dw_ref[gi] += dwg

            dpool = lax.dot_general(dmix, wg, (((1,), (1,)), ((), ())), preferred_element_type=F32)
            f_ref[:, cols] = dpool / jnp.minimum(t_ext + 1, w).astype(F32)
            acc = f_ref[pl.ds(0, ts), cols]
            for k in range(1, w):
                acc = acc + f_ref[pl.ds(k, ts), cols]
            dz_ref[:, cols] = (acc - dpool[:ts]).astype(BF16)
        dsc_all = jnp.concatenate(dsc, axis=1)

        @pl.when(first)
        def _():
            dsc_ref[...] = dsc_all

        @pl.when(jnp.logical_not(first))
        def _():
            dsc_ref[...] += dsc_all

    n_hb = s // POOL_HALO
    return pl.pallas_call(
        body, name=name, grid=(n_t,),
        in_specs=[pl.BlockSpec((ts, BR), lambda i: (i, 0)),
                  pl.BlockSpec((POOL_HALO, BR), lambda i: (jnp.minimum((i + 1) * hb, n_hb - 1), 0)),
                  pl.BlockSpec((ts, BR), lambda i: (i, 0)),
                  pl.BlockSpec((POOL_HALO, BR), lambda i: (jnp.maximum(i * hb - 1, 0), 0)),
                  pl.BlockSpec((4, GDIM, GDIM), lambda i: (0, 0, 0)), pl.BlockSpec((1, BR), lambda i: (0, 0))],
        out_specs=[pl.BlockSpec((ts, BR), lambda i: (i, 0)), pl.BlockSpec((4, GDIM, GDIM), lambda i: (0, 0, 0)),
                   pl.BlockSpec((1, BR), lambda i: (0, 0))],
        out_shape=[jax.ShapeDtypeStruct((s, BR), BF16), jax.ShapeDtypeStruct((4, GDIM, GDIM), F32),
                   jax.ShapeDtypeStruct((1, BR), F32)],
        scratch_shapes=[pltpu.VMEM((POOL_HALO + ts, BR), F32), pltpu.VMEM((ts + POOL_HALO, BR), F32)],
        compiler_params=_params(("arbitrary",)))(dact, dact, za, za, pool_w, pool_scale)


CONV_HALO = 32
CONV_LEAD = CONV_HALO - (CONV_WIDTH - 1)


def _conv_fwd(za, conv_w, conv_b, ng, nb, name):
    s = za.shape[0]
    ts = _tile(s, 512)
    hb = ts // CONV_HALO

    def body(a_ref, g_ref, ah_ref, gh_ref, w_ref, b_ref, ng_ref, nb_ref, yc_ref, act_ref, ext_ref):
        i = pl.program_id(0)
        ext_ref[pl.ds(0, CONV_HALO), :] = jnp.where(i > 0, ah_ref[...] * _sigmoid(gh_ref[...]), 0.0)
        ext_ref[pl.ds(CONV_HALO, ts), :] = a_ref[...] * _sigmoid(g_ref[...])
        acc = jnp.zeros((ts, BR), F32) + b_ref[...]
        for k in range(CONV_WIDTH):
            acc = acc + w_ref[pl.ds(k, 1), :] * ext_ref[pl.ds(CONV_LEAD + k, ts), :]
        yc_ref[...] = acc
        xh, _ = _ln_stats(acc)
        ln = xh * ng_ref[...] + nb_ref[...]
        act_ref[...] = (ln * _sigmoid(ln)).astype(BF16)

    tile = lambda c: pl.BlockSpec((ts, BR), lambda i: (i, c))
    halo = lambda c: pl.BlockSpec((CONV_HALO, BR), lambda i: (jnp.maximum(i * hb - 1, 0), c))
    vec = pl.BlockSpec((1, BR), lambda i: (0, 0))
    return pl.pallas_call(
        body, name=name, grid=(s // ts,),
        in_specs=[tile(1), tile(2), halo(1), halo(2), pl.BlockSpec((CONV_WIDTH, BR), lambda i: (0, 0)), vec, vec, vec],
        out_specs=[pl.BlockSpec((ts, BR), lambda i: (i, 0))] * 2,
        out_shape=[jax.ShapeDtypeStruct((s, BR), F32), jax.ShapeDtypeStruct((s, BR), BF16)],
        scratch_shapes=[pltpu.VMEM((CONV_HALO + ts, BR), F32)], compiler_params=_params(("parallel",)))(
            za, za, za, za, conv_w, conv_b, ng, nb)


def _conv_bwd_norm(dact, yc, ng, nb, name):
    s = yc.shape[0]
    ts = _tile(s, 512)

    def body(d_ref, y_ref, ng_ref, nb_ref, dy_ref, db_ref, dng_ref, dnb_ref):
        first = pl.program_id(0) == 0
        xh, r = _ln_stats(y_ref[...])
        g = ng_ref[...]
        ln = xh * g + nb_ref[...]
        sg = _sigmoid(ln)
        dln = d_ref[...] * sg * (1.0 + ln * (1.0 - sg))
        dy = _ln_bwd(xh, r, g, dln)
        dy_ref[...] = dy
        _colsum_into(db_ref, dy, first)
        _colsum_into(dng_ref, dln * xh, first)
        _colsum_into(dnb_ref, dln, first)

    vec = pl.BlockSpec((1, BR), lambda i: (0, 0))
    vshape = jax.ShapeDtypeStruct((1, BR), F32)
    return pl.pallas_call(
        body, name=name, grid=(s // ts,),
        in_specs=[pl.BlockSpec((ts, BR), lambda i: (i, 1)), pl.BlockSpec((ts, BR), lambda i: (i, 0)), vec, vec],
        out_specs=[pl.BlockSpec((ts, BR), lambda i: (i, 0)), vec, vec, vec],
        out_shape=[jax.ShapeDtypeStruct((s, BR), F32), vshape, vshape, vshape],
        compiler_params=_params(("arbitrary",)))(dact, yc, ng, nb)


def _conv_bwd_taps(dyc, za, conv_w, name):
    s = za.shape[0]
    ts = _tile(s, 512)
    hb = ts // CONV_HALO
    n_t = s // ts
    n_hb = s // CONV_HALO

    def body(d_ref, dh_ref, a_ref, g_ref, ah_ref, gh_ref, w_ref, dz_ref, dw_ref, ext_ref, f_ref):
        i = pl.program_id(0)
        first = i == 0
        a, sg = a_ref[...], _sigmoid(g_ref[...])
        ext_ref[pl.ds(0, CONV_HALO), :] = jnp.where(i > 0, ah_ref[...] * _sigmoid(gh_ref[...]), 0.0)
        ext_ref[pl.ds(CONV_HALO, ts), :] = a * sg
        d = d_ref[...]
        f_ref[pl.ds(0, ts), :] = d
        f_ref[pl.ds(ts, CONV_HALO), :] = jnp.where(i < n_t - 1, dh_ref[...], 0.0)
        dglu = jnp.zeros((ts, BR), F32)
        rows = []
        for k in range(CONV_WIDTH):
            rows.append(jnp.sum(d * ext_ref[pl.ds(CONV_LEAD + k, ts), :], axis=0, keepdims=True))
            dglu = dglu + w_ref[pl.ds(k, 1), :] * f_ref[pl.ds(CONV_WIDTH - 1 - k, ts), :]
        rows.append(jnp.zeros((1, BR), F32))
        dw = jnp.concatenate(rows, axis=0)

        @pl.when(first)
        def _():
            dw_ref[...] = dw

        @pl.when(jnp.logical_not(first))
        def _():
            dw_ref[...] += dw

        dz_ref[:, pl.ds(0, BR)] = (dglu * sg).astype(BF16)
        dz_ref[:, pl.ds(BR, BR)] = (dglu * a * sg * (1.0 - sg)).astype(BF16)

    tile = lambda c: pl.BlockSpec((ts, BR), lambda i: (i, c))
    halo = lambda c: pl.BlockSpec((CONV_HALO, BR), lambda i: (jnp.maximum(i * hb - 1, 0), c))
    return pl.pallas_call(
        body, name=name, grid=(n_t,),
        in_specs=[pl.BlockSpec((ts, BR), lambda i: (i, 0)),
                  pl.BlockSpec((CONV_HALO, BR), lambda i: (jnp.minimum((i + 1) * hb, n_hb - 1), 0)),
                  tile(1), tile(2), halo(1), halo(2), pl.BlockSpec((CONV_WIDTH, BR), lambda i: (0, 0))],
        out_specs=[pl.BlockSpec((ts, 2 * BR), lambda i: (i, 0)), pl.BlockSpec((CONV_WIDTH + 1, BR), lambda i: (0, 0))],
        out_shape=[jax.ShapeDtypeStruct((s, 2 * BR), BF16), jax.ShapeDtypeStruct((CONV_WIDTH + 1, BR), F32)],
        scratch_shapes=[pltpu.VMEM((CONV_HALO + ts, BR), F32), pltpu.VMEM((ts + CONV_HALO, BR), F32)],
        compiler_params=_params(("arbitrary",)))(dyc, dyc, za, za, za, za, conv_w)


def _tril(w):
    r = lax.broadcasted_iota(jnp.int32, (CHUNK, CHUNK), 0)
    c = lax.broadcasted_iota(jnp.int32, (CHUNK, CHUNK), 1)
    return jnp.where(c <= r, w, 0.0)


def _sgu_fwd(za, ng, nb, sgu_w, bias_b, name):
    s = za.shape[0]
    ts = _tile(s, 512)

    def body(u_ref, v_ref, ng_ref, nb_ref, w_ref, b_ref, o_ref):
        u = _gelu(u_ref[...])
        xh, _ = _ln_stats(_gelu(v_ref[...]))
        vln = (xh * ng_ref[...] + nb_ref[...]).astype(BF16)
        for gi in range(4):
            wg = _tril(w_ref[gi]).astype(BF16)
            for n in range(ts // CHUNK):
                blk = vln[n * CHUNK:(n + 1) * CHUNK, gi * GDIM:(gi + 1) * GDIM]
                sp = jnp.dot(wg, blk, preferred_element_type=F32) + b_ref[gi]
                o_ref[pl.ds(n * CHUNK, CHUNK), pl.ds(gi * GDIM, GDIM)] = (
                    u[n * CHUNK:(n + 1) * CHUNK, gi * GDIM:(gi + 1) * GDIM] * sp).astype(BF16)

    vec = pl.BlockSpec((1, BR), lambda i: (0, 0))
    cube = pl.BlockSpec((4, CHUNK, GDIM), lambda i: (0, 0, 0))
    return pl.pallas_call(
        body, name=name, grid=(s // ts,),
        in_specs=[pl.BlockSpec((ts, BR), lambda i: (i, 3)), pl.BlockSpec((ts, BR), lambda i: (i, 4)), vec, vec, cube, cube],
        out_specs=pl.BlockSpec((ts, BR), lambda i: (i, 0)), out_shape=jax.ShapeDtypeStruct((s, BR), BF16),
        compiler_params=_params(("parallel",)))(za, za, ng, nb, sgu_w, bias_b)


def _sgu_bwd(dact, za, ng, nb, sgu_w, bias_b, name):
    s = za.shape[0]
    ts = _tile(s, 512)

    def body(d_ref, u_ref, v_ref, ng_ref, nb_ref, w_ref, b_ref, dz_ref, dw_ref, db_ref, dng_ref, dnb_ref, dv_ref):
        first = pl.program_id(0) == 0
        u_raw, v_raw = u_ref[...], v_ref[...]
        u = _gelu(u_raw)
        xh, r = _ln_stats(_gelu(v_raw))
        g = ng_ref[...]
        vln = (xh * g + nb_ref[...]).astype(BF16)
        d = d_ref[...]
        dsp = d * u
        dsp16 = dsp.astype(BF16)
        for gi in range(4):
            wg = _tril(w_ref[gi]).astype(BF16)
            dwg = jnp.zeros((CHUNK, CHUNK), F32)
            dbg = jnp.zeros((CHUNK, 1), F32)
            for n in range(ts // CHUNK):
                rows, cols = slice(n * CHUNK, (n + 1) * CHUNK), slice(gi * GDIM, (gi + 1) * GDIM)
                blk = vln[rows, cols]
                sp = jnp.dot(wg, blk, preferred_element_type=F32) + b_ref[gi]
                dz_ref[pl.ds(n * CHUNK, CHUNK), pl.ds(gi * GDIM, GDIM)] = (
                    d[rows, cols] * sp * _gelu_grad(u_raw[rows, cols])).astype(BF16)
                dv_ref[pl.ds(n * CHUNK, CHUNK), pl.ds(gi * GDIM, GDIM)] = lax.dot_general(
                    wg, dsp16[rows, cols], (((0,), (0,)), ((), ())), preferred_element_type=F32)
                dwg = dwg + lax.dot_general(dsp16[rows, cols], blk, (((1,), (1,)), ((), ())), preferred_element_type=F32)
                dbg = dbg + jnp.sum(dsp[rows, cols], axis=1, keepdims=True)
            dwg = _tril(dwg)

            @pl.when(first)
            def _():
                dw_ref[gi] = dwg
                db_ref[gi] = dbg

            @pl.when(jnp.logical_not(first))
            def _():
                dw_ref[gi] += dwg
                db_ref[gi] += dbg

        dvln = dv_ref[...]
        dz_ref[:, pl.ds(BR, BR)] = (_ln_bwd(xh, r, g, dvln) * _gelu_grad(v_raw)).astype(BF16)
        _colsum_into(dng_ref, dvln * xh, first)
        _colsum_into(dnb_ref, dvln, first)

    vec = pl.BlockSpec((1, BR), lambda i: (0, 0))
    cube = pl.BlockSpec((4, CHUNK, GDIM), lambda i: (0, 0, 0))
    vshape = jax.ShapeDtypeStruct((1, BR), F32)
    return pl.pallas_call(
        body, name=name, grid=(s // ts,),
        in_specs=[pl.BlockSpec((ts, BR), lambda i: (i, 2)), pl.BlockSpec((ts, BR), lambda i: (i, 3)),
                  pl.BlockSpec((ts, BR), lambda i: (i, 4)), vec, vec, cube, cube],
        out_specs=[pl.BlockSpec((ts, 2 * BR), lambda i: (i, 0)), cube, pl.BlockSpec((4, CHUNK, 1), lambda i: (0, 0, 0)),
                   vec, vec],
        out_shape=[jax.ShapeDtypeStruct((s, 2 * BR), BF16), jax.ShapeDtypeStruct((4, CHUNK, CHUNK), F32),
                   jax.ShapeDtypeStruct((4, CHUNK, 1), F32), vshape, vshape],
        scratch_shapes=[pltpu.VMEM((ts, BR), F32)], compiler_params=_params(("arbitrary",)))(
            dact, za, za, ng, nb, sgu_w, bias_b)


KR_BLOCK = 3584 // 128


def _mla_prep(za, qg, kvg, ck, sk, name):
    s = za.shape[0]
    ts = _tile(s, 512)

    def body(cq_ref, ckv_ref, kr_ref, qg_ref, kvg_ref, c_ref, s_ref, qn_ref, kvn_ref, krr_ref):
        qn_ref[...] = _rms(cq_ref[...], qg_ref[...]).astype(BF16)
        kvn_ref[...] = _rms(ckv_ref[...], kvg_ref[...]).astype(BF16)
        kr = kr_ref[...]
        krr_ref[...] = (kr * c_ref[...] + _rot_half(kr, 128, 0) * s_ref[...]).astype(BF16)

    vec = pl.BlockSpec((1, BR), lambda i: (0, 0))
    tab = pl.BlockSpec((ts, 128), lambda i: (i, 0))
    row = pl.BlockSpec((ts, BR), lambda i: (i, 0))
    return pl.pallas_call(
        body, name=name, grid=(s // ts,),
        in_specs=[pl.BlockSpec((ts, BR), lambda i: (i, 5)), pl.BlockSpec((ts, BR), lambda i: (i, 6)),
                  pl.BlockSpec((ts, 128), lambda i: (i, KR_BLOCK)), vec, vec, tab, tab],
        out_specs=[row, row, tab],
        out_shape=[jax.ShapeDtypeStruct((s, BR), BF16), jax.ShapeDtypeStruct((s, BR), BF16),
                   jax.ShapeDtypeStruct((s, 128), BF16)],
        compiler_params=_params(("parallel",)))(za, za, za, qg, kvg, ck, sk)


def _mla_prep_bwd(dqn, dkvn, dkr_heads, za, qg, kvg, ck, sk, name):
    s = za.shape[0]
    ts = _tile(s, 512)

    def body(dq_ref, dkv_ref, dkr_ref, cq_ref, ckv_ref, qg_ref, kvg_ref, c_ref, s_ref, dz_ref, dqg_ref, dkvg_ref):
        first = pl.program_id(0) == 0
        dcq, rows_q = _rms_bwd(cq_ref[...], qg_ref[...], dq_ref[...])
        dckv, rows_kv = _rms_bwd(ckv_ref[...], kvg_ref[...], dkv_ref[...])
        dz_ref[:, pl.ds(0, BR)] = dcq.astype(BF16)
        dz_ref[:, pl.ds(BR, BR)] = dckv.astype(BF16)
        dk = dkr_ref[:, pl.ds(0, 128)]
        for h in range(1, MLA_HEADS):
            dk = dk + dkr_ref[:, pl.ds(h * 128, 128)]
        dz_ref[:, pl.ds(2 * BR, 128)] = (dk * c_ref[...] - _rot_half(dk, 128, 0) * s_ref[...]).astype(BF16)
        _colsum_into(dqg_ref, rows_q, first)
        _colsum_into(dkvg_ref, rows_kv, first)

    vec = pl.BlockSpec((1, BR), lambda i: (0, 0))
    tab = pl.BlockSpec((ts, 128), lambda i: (i, 0))
    row = pl.BlockSpec((ts, BR), lambda i: (i, 0))
    wide = 2 * BR + 128
    vshape = jax.ShapeDtypeStruct((1, BR), F32)
    return pl.pallas_call(
        body, name=name, grid=(s // ts,),
        in_specs=[row, row, pl.BlockSpec((ts, MLA_HEADS * 128), lambda i: (i, 0)),
                  pl.BlockSpec((ts, BR), lambda i: (i, 5)), pl.BlockSpec((ts, BR), lambda i: (i, 6)), vec, vec, tab, tab],
        out_specs=[pl.BlockSpec((ts, wide), lambda i: (i, 0)), vec, vec],
        out_shape=[jax.ShapeDtypeStruct((s, wide), BF16), vshape, vshape],
        compiler_params=_params(("arbitrary",)))(dqn, dkvn, dkr_heads, za, za, qg, kvg, ck, sk)


def _attn_tiles(s):
    tq, tk = _tile(s, 1024), _tile(s, 512)
    return tq, tk, tq // tk


def _causal(qi, ki, tq, tk):
    row = qi * tq + lax.broadcasted_iota(jnp.int32, (tq, tk), 0)
    col = ki * tk + lax.broadcasted_iota(jnp.int32, (tq, tk), 1)
    return col <= row


def _step_count(t, firsts):
    n = jnp.int32(0)
    for f in firsts[1:]:
        n = n + jnp.where(t >= f, 1, 0)
    return n


def _q_major_pairs(nq, r):
    firsts = [r * qq * (qq + 1) // 2 for qq in range(nq)]

    def pair(t):
        qi = _step_count(t, firsts)
        return qi, t - r * qi * (qi + 1) // 2

    return r * nq * (nq + 1) // 2, pair


def _k_major_pairs(nq, nk, r):
    counts = [nq - kk // r for kk in range(nk)]
    firsts = [sum(counts[:kk]) for kk in range(nk)]

    def pair(t):
        ki = _step_count(t, firsts)
        first = jnp.int32(0)
        for kk in range(1, nk):
            first = first + jnp.where(t >= firsts[kk], counts[kk - 1], 0)
        return ki // r + (t - first), ki

    return sum(counts), pair


FLASH_HEADS = 2


def _flash_fwd(q, kv, krr, name, plan=None):
    s = q.shape[0]
    tq, tk, r = _attn_tiles(s)
    n_pairs, pair = _q_major_pairs(s // tq, r)
    hp = FLASH_HEADS

    def body(q_ref, kv_ref, kr_ref, o_ref, lse_ref, m_sc, l_sc, acc_sc):
        qi, ki = pair(pl.program_id(1))

        @pl.when(ki == 0)
        def _():
            m_sc[...] = jnp.full((hp, tq, 1), NEG, F32)
            l_sc[...] = jnp.zeros((hp, tq, 1), F32)
            acc_sc[...] = jnp.zeros((hp, tq, V_DIM), F32)

        def step(masked):
            for h in range(hp):
                k = jnp.concatenate([kv_ref[:, pl.ds(h * QW, QK_NOPE)], kr_ref[...]], axis=1)
                sc = lax.dot_general(q_ref[:, pl.ds(h * QW, QW)], k, (((1,), (1,)), ((), ())), preferred_element_type=F32)
                if masked:
                    sc = jnp.where(_causal(qi, ki, tq, tk), sc, NEG)
                m_prev = m_sc[h]
                m_new = jnp.maximum(m_prev, jnp.max(sc, axis=1, keepdims=True))
                alpha = jnp.exp(m_prev - m_new)
                p = jnp.exp(sc - m_new)
                l_sc[h] = alpha * l_sc[h] + jnp.sum(p, axis=1, keepdims=True)
                acc_sc[h] = alpha * acc_sc[h] + jnp.dot(p.astype(BF16), kv_ref[:, pl.ds(h * QW + QK_NOPE, V_DIM)],
                                                        preferred_element_type=F32)
                m_sc[h] = m_new

        pl.when(ki < qi * r)(functools.partial(step, False))
        pl.when(ki >= qi * r)(functools.partial(step, True))

        @pl.when(ki == (qi + 1) * r - 1)
        def _():
            for h in range(hp):
                o_ref[:, pl.ds(h * V_DIM, V_DIM)] = (acc_sc[h] / l_sc[h]).astype(BF16)
                lse_ref[:, pl.ds(h * 128, 128)] = jnp.broadcast_to(m_sc[h] + jnp.log(l_sc[h]), (tq, 128))

    out_blk = pl.BlockSpec((tq, hp * 128), lambda g, t: (pair(t)[0], g))
    return _pcall(
        body, (q, kv, krr), name=name, grid=(MLA_HEADS // hp, n_pairs),
        in_specs=[pl.BlockSpec((tq, hp * QW), lambda g, t: (pair(t)[0], g)),
                  pl.BlockSpec((tk, hp * QW), lambda g, t: (pair(t)[1], g)),
                  pl.BlockSpec((tk, 128), lambda g, t: (pair(t)[1], 0))],
        out_specs=[out_blk, out_blk],
        out_shape=[jax.ShapeDtypeStruct((s, MLA_HEADS * V_DIM), BF16), jax.ShapeDtypeStruct((s, MLA_HEADS * 128), F32)],
        scratch_shapes=[pltpu.VMEM((hp, tq, 1), F32), pltpu.VMEM((hp, tq, 1), F32), pltpu.VMEM((hp, tq, V_DIM), F32)],
        sem=("parallel", "arbitrary"), vmem=VMEM_BIG, plan=plan)


DO_BLOCK = 3 * BR // 128


def _flash_probs(q_ref, kn_ref, kr_ref, v_ref, do_ref, o_ref, lse_ref, qi, ki, tq, tk, masked):
    k = jnp.concatenate([kn_ref[...], kr_ref[...]], axis=1)
    q = q_ref[...]
    sc = lax.dot_general(q, k, (((1,), (1,)), ((), ())), preferred_element_type=F32)
    lse = jnp.max(lse_ref[...], axis=1, keepdims=True)
    p = jnp.exp(sc - lse)
    if masked:
        p = jnp.where(_causal(qi, ki, tq, tk), p, 0.0)
    do = do_ref[...]
    delta = jnp.sum(do * o_ref[...].astype(F32), axis=1, keepdims=True)
    do = do.astype(BF16)
    dp = lax.dot_general(do, v_ref[...], (((1,), (1,)), ((), ())), preferred_element_type=F32)
    ds = (p * (dp - delta)).astype(BF16)
    return q, k, p, do, ds


def _flash_bwd(q, kv, krr, dact, o, lse, cq, sq, name, plan=None):
    s = q.shape[0]
    tq, tk, r = _attn_tiles(s)
    nq, nk = s // tq, s // tk
    n_pairs, pair = _k_major_pairs(nq, nk, r)

    def body(q_ref, kn_ref, kr_ref, v_ref, do_ref, o_ref, lse_ref, c_ref, s_ref, dkv_ref, dkr_ref, dq_ref,
             dk_sc, dv_sc, dq_sc):
        qi, ki = pair(pl.program_id(1))
        rows = pl.ds(pl.multiple_of(qi * tq, tq), tq)

        @pl.when(qi == ki // r)
        def _():
            dk_sc[...] = jnp.zeros((tk, QW), F32)
            dv_sc[...] = jnp.zeros((tk, V_DIM), F32)

        @pl.when(ki == 0)
        def _():
            dq_sc[rows, :] = jnp.zeros((tq, QW), F32)

        def step(masked):
            qv, k, p, dov, ds = _flash_probs(q_ref, kn_ref, kr_ref, v_ref, do_ref, o_ref, lse_ref, qi, ki, tq, tk, masked)
            dv_sc[...] += lax.dot_general(p.astype(BF16), dov, (((0,), (0,)), ((), ())), preferred_element_type=F32)
            dk_sc[...] += lax.dot_general(ds, qv, (((0,), (0,)), ((), ())), preferred_element_type=F32)
            dq_sc[rows, :] += jnp.dot(ds, k, preferred_element_type=F32)

        pl.when(ki < qi * r)(functools.partial(step, False))
        pl.when(ki >= qi * r)(functools.partial(step, True))

        @pl.when(qi == nq - 1)
        def _():
            dkv_ref[:, pl.ds(0, 128)] = dk_sc[:, pl.ds(0, 128)].astype(BF16)
            dkv_ref[:, pl.ds(128, 128)] = dv_sc[...].astype(BF16)
            dkr_ref[...] = dk_sc[:, pl.ds(128, 128)]

        @pl.when(ki == (qi + 1) * r - 1)
        def _():
            dq = dq_sc[rows, :] * ATT_SCALE
            dq_ref[...] = (dq * c_ref[...] - _rot_half(dq, QW, QK_NOPE) * s_ref[...]).astype(BF16)

    def next_done(t):
        qi, ki = pair(t)
        return jnp.minimum(ki // r + jnp.where(jnp.logical_and(ki % r == r - 1, qi > ki // r), 1, 0), nq - 1)

    qmap = lambda c: (lambda h, t: (pair(t)[0], c(h)))
    kmap = lambda c: (lambda h, t: (pair(t)[1], c(h)))
    last = lambda c: (lambda h, t: (next_done(t), c(h)))
    return _pcall(
        body, (q, kv, krr, kv, dact, o, lse, cq, sq), name=name, grid=(MLA_HEADS, n_pairs),
        in_specs=[pl.BlockSpec((tq, QW), qmap(lambda h: h)), pl.BlockSpec((tk, 128), kmap(lambda h: 2 * h)),
                  pl.BlockSpec((tk, 128), kmap(lambda h: 0)), pl.BlockSpec((tk, 128), kmap(lambda h: 2 * h + 1)),
                  pl.BlockSpec((tq, 128), qmap(lambda h: DO_BLOCK + h)), pl.BlockSpec((tq, 128), qmap(lambda h: h)),
                  pl.BlockSpec((tq, 128), qmap(lambda h: h)), pl.BlockSpec((tq, QW), last(lambda h: 0)),
                  pl.BlockSpec((tq, QW), last(lambda h: 0))],
        out_specs=[pl.BlockSpec((tk, QW), kmap(lambda h: h)), pl.BlockSpec((tk, 128), kmap(lambda h: h)),
                   pl.BlockSpec((tq, QW), last(lambda h: h))],
        out_shape=[jax.ShapeDtypeStruct((s, MLA_HEADS * QW), BF16), jax.ShapeDtypeStruct((s, MLA_HEADS * 128), F32),
                   jax.ShapeDtypeStruct((s, MLA_HEADS * QW), BF16)],
        scratch_shapes=[pltpu.VMEM((tk, QW), F32), pltpu.VMEM((tk, V_DIM), F32), pltpu.VMEM((s, QW), F32)],
        sem=("parallel", "arbitrary"), vmem=VMEM_BIG, plan=plan)


def _rope_tables(positions):
    inv_freq = ROPE_THETA ** (-jnp.arange(0, QK_ROPE, 2, dtype=F32) / QK_ROPE)
    ang = positions.reshape(-1).astype(F32)[:, None] * inv_freq
    cos, sin = jnp.cos(ang), jnp.sin(ang)
    s = cos.shape[0]
    one, zero = jnp.ones((s, 64), F32), jnp.zeros((s, 64), F32)
    ck = jnp.concatenate([cos, cos, one], axis=1)
    sk = jnp.concatenate([sin, sin, zero], axis=1)
    cq = jnp.concatenate([one, one, ck], axis=1)
    sq = jnp.concatenate([zero, zero, sk], axis=1)
    return ck, sk, cq, sq


def _cols_full(gathered):
    _, r, c = gathered.shape
    return gathered.transpose(1, 0, 2).reshape(r, N_DEV * c)


def _cols_by_owner(full):
    r, n = full.shape
    return full.reshape(r, N_DEV, n // N_DEV).transpose(1, 0, 2)


def _layer_weights(gat, small, l):
    shards = gat[("w_in", l)]
    cut, rem = divmod(N_IN_A, shards.shape[2])
    w = {
        "gat": gat, "layer": l,
        "w_a": jnp.concatenate([shards[d] for d in range(cut)] + [shards[cut][:, :rem],
                                                                 jnp.zeros((D_MODEL, ZA - N_IN_A), BF16)], axis=1),
        "w_g": jnp.concatenate([shards[cut][:, rem:]] + [shards[d] for d in range(cut + 1, N_DEV)], axis=1),
        "conv_w": small["conv_w"][l],
        "pool_w": small["pool_w"][l], "sgu_w": small["sgu_w"][l],
        "sgu_bias": jnp.broadcast_to(small["sgu_b"][l][:, :, None], (4, CHUNK, GDIM)),
    }
    for name in ("pre_mix_g", "pool_scale", "conv_b", "conv_norm_g", "conv_norm_b", "sgu_norm_g", "sgu_norm_b",
                 "q_norm_g", "kv_norm_g", "post_mix_g", "pre_mlp_g", "post_mlp_g"):
        w[name] = small[name][l][None, :]
    return w


def _late(w, name):
    if name not in w:
        gat, l = w["gat"], w["layer"]
        if name == "proj_cat":
            w[name] = jnp.concatenate([_cols_full(gat[(n, l)]) for n in ("pool_proj", "conv_proj", "sgu_proj", "attn_proj")],
                                      axis=0)
        elif name == "w_uq":
            w[name] = jnp.pad(gat[("w_uq", l)].transpose(1, 0, 2),
                              ((0, 0), (0, 0), (0, QW - QK_NOPE - QK_ROPE))).reshape(BR, MLA_HEADS * QW)
        elif name == "w_ukv":
            w[name] = _cols_full(gat[("w_ukv", l)])
        elif name == "w_up":
            w[name] = gat[("w_up", l)]
        else:
            arr = gat[(name, l)]
            w[name] = arr.reshape(N_DEV * arr.shape[1], arr.shape[2])
    return w[name]


def _layer_fwd(x, h1, w, tabs, l, plan):
    ck, sk, cq, sq = tabs
    n = f"l{l}_"
    za = _mm(h1, w["w_a"], mode="nn", name=n + "za", plan=plan)
    zg = _mm(h1, w["w_g"], mode="nn", name=n + "zg", plan=plan)
    a_pool = _pool_fwd(za, w["pool_w"], w["pool_scale"], n + "pool")
    yc, a_conv = _conv_fwd(za, w["conv_w"], w["conv_b"], w["conv_norm_g"], w["conv_norm_b"], n + "conv")
    a_sgu = _sgu_fwd(za, w["sgu_norm_g"], w["sgu_norm_b"], w["sgu_w"], w["sgu_bias"], n + "sgu")
    qn, kvn, krr = _mla_prep(za, w["q_norm_g"], w["kv_norm_g"], ck, sk, n + "mla_prep")
    q = _mm(qn, _late(w, "w_uq"), mode="nn", name=n + "q", out_dtypes=(BF16,), extras=(cq, sq), tn=QW,
            epilogue=lambda acc, c, sn: ((acc * c + _rot_half(acc, QW, QK_NOPE) * sn) * ATT_SCALE,))
    kv = _mm(kvn, _late(w, "w_ukv"), mode="nn", name=n + "kv", out_dtypes=(BF16,))
    o, lse = _flash_fwd(q, kv, krr, n + "flash", plan=plan)
    act_cat = jnp.concatenate([a_pool, a_conv, a_sgu, o], axis=1)
    y, merged = _proj_merge_fwd(act_cat, _late(w, "proj_cat"), zg, n + "proj", plan=plan)
    m2 = _mm(merged, _late(w, "w_out"), mode="nn", name=n + "out")
    x1, h2 = _post_pre(x, m2, w["post_mix_g"], w["pre_mlp_g"], n + "post_mix")
    up, act = _mm(h2, _late(w, "w_up"), mode="nn", name=n + "up", out_dtypes=(F32, BF16), plan=plan,
                  mnk=(x.shape[0], D_FF, D_MODEL), tn=UP_SHARD,
                  b_spec=lambda tn, tk: pl.BlockSpec((None, tk, tn), lambda i, j, kk: (j, kk, 0)),
                  epilogue=lambda acc: (acc, jnp.square(jnp.maximum(acc, 0.0))))
    f = _mm(act, _late(w, "w_down"), mode="nn", name=n + "down", plan=plan)
    saved = dict(x=x, h1=h1, za=za, zg=zg, yc=yc, qn=qn, kvn=kvn, krr=krr, q=q, kv=kv, o=o, lse=lse, act_cat=act_cat,
                 y=y, merged=merged, m2=m2, x1=x1, h2=h2, up=up, act=act, f=f)
    return x1, f, saved


def _layer_bwd(dx_out, df, sv, w, tabs, l, prev, plan, rs):
    ck, sk, cq, sq = tabs
    n = f"l{l}_b_"
    g = {}
    own, half = {}, {}

    def to_sibling(kernel, names):
        def done(results):
            for name, t in zip(names, results[0]):
                half[name] = _rs_chip_sum(own[name], t, f"l{l}_chip_sum_{name}")
        plan.at(n + kernel, lambda: [_scatter_sibling([own[name] for name in names])], done)

    def to_chips(kernel, names, prefix=n):
        def done(results):
            for name, t in zip(names, results[0]):
                rs[(name, l)] = (half[name], t)
        plan.at(prefix + kernel, lambda: [_scatter_chips([half[name] for name in names])], done)

    own["w_down"] = _mm(sv["act"], df, mode="tn", name=n + "dw_down", out_dtypes=(BF16,), plan=plan).reshape(
        N_DEV, D_FF // N_DEV, D_MODEL)
    to_sibling("dup", ["w_down"])
    dup = _mm(df, _late(w, "w_down"), mode="nt", name=n + "dup", out_dtypes=(BF16,), extras=(sv["up"],), plan=plan,
              epilogue=lambda acc, up: (acc * 2.0 * jnp.maximum(up, 0.0),))
    to_chips("dw_up", ["w_down"])
    own["w_up"] = _mm(sv["h2"], dup, mode="tn", name=n + "dw_up", tn=UP_SHARD, plan=plan,
                      o_spec=lambda tm, tn: pl.BlockSpec((None, tm, tn), lambda i, j, kk: (j, i, 0)),
                      out_struct=jax.ShapeDtypeStruct((N_DEV, D_MODEL, UP_SHARD), BF16))
    to_sibling("dh2", ["w_up"])
    dh2 = _mm(dup, _late(w, "w_up"), mode="nt", name=n + "dh2", mnk=(dup.shape[0], D_MODEL, D_FF), tk=UP_SHARD,
              plan=plan, b_spec=lambda tn, tk: pl.BlockSpec((None, tn, tk), lambda i, j, kk: (kk, j, 0)))
    dx1, dm2, g["pre_mlp_g"], g["post_mix_g"] = _pre_bwd(dh2, sv["x1"], w["pre_mlp_g"], dx_out, n + "pre_mlp",
                                                           r_prev=sv["m2"], g_post_prev=w["post_mix_g"])
    own["w_out"] = _mm(sv["merged"], dm2, mode="tn", name=n + "dw_out", out_dtypes=(BF16,)).reshape(
        N_DEV, D_MODEL // N_DEV, D_MODEL)
    to_sibling("dmerged", ["w_out"])
    dmerged = _mm(dm2, _late(w, "w_out"), mode="nt", name=n + "dmerged", plan=plan)
    to_chips("merge", ["w_up"])
    dy, dzg = _merge_bwd(dmerged, sv["y"], sv["zg"], n + "merge", plan=plan)
    d_proj = _proj_bwd_w(sv["act_cat"], dy, n + "dw_proj")
    projs = ["pool_proj", "conv_proj", "sgu_proj", "attn_proj"]
    for i, name in enumerate(projs):
        own[name] = _cols_by_owner(d_proj[i * BR:(i + 1) * BR] if i < 3 else d_proj[3 * BR:])
    to_chips("dact", ["w_out"])
    to_sibling("dact", projs)
    dact = _proj_bwd_act(dy, _late(w, "proj_cat"), n + "dact", plan=plan)
    dz_pool, g["pool_w"], g["pool_scale"] = _pool_bwd(dact, sv["za"], w["pool_w"], w["pool_scale"], n + "pool")
    dyc, g["conv_b"], g["conv_norm_g"], g["conv_norm_b"] = _conv_bwd_norm(dact, sv["yc"], w["conv_norm_g"],
                                                                          w["conv_norm_b"], n + "conv_norm")
    dz_conv, g["conv_w"] = _conv_bwd_taps(dyc, sv["za"], w["conv_w"], n + "conv_taps")
    dz_sgu, g["sgu_w"], g["sgu_b"], g["sgu_norm_g"], g["sgu_norm_b"] = _sgu_bwd(
        dact, sv["za"], w["sgu_norm_g"], w["sgu_norm_b"], w["sgu_w"], w["sgu_bias"], n + "sgu")
    to_chips("flash", projs)
    dkv, dkr_heads, dq = _flash_bwd(sv["q"], sv["kv"], sv["krr"], dact, sv["o"], sv["lse"], cq, sq, n + "flash",
                                    plan=plan)
    d_uq = _mm(sv["qn"], dq, mode="tn", name=n + "dw_uq", out_dtypes=(BF16,))
    own["w_uq"] = d_uq.reshape(BR, MLA_HEADS, QW)[:, :, :QK_NOPE + QK_ROPE].transpose(1, 0, 2)
    dqn = _mm(dq, _late(w, "w_uq"), mode="nt", name=n + "dqn")
    own["w_ukv"] = _cols_by_owner(_mm(sv["kvn"], dkv, mode="tn", name=n + "dw_ukv", out_dtypes=(BF16,)))
    to_sibling("dkvn", ["w_uq", "w_ukv"])
    dkvn = _mm(dkv, _late(w, "w_ukv"), mode="nt", name=n + "dkvn", plan=plan)
    dz_mla, g["q_norm_g"], g["kv_norm_g"] = _mla_prep_bwd(dqn, dkvn, dkr_heads, sv["za"], w["q_norm_g"],
                                                           w["kv_norm_g"], ck, sk, n + "mla_prep")
    dza = jnp.concatenate([dz_pool, dz_conv, dz_sgu, dz_mla], axis=1)
    to_chips("dw_a", ["w_uq", "w_ukv"])
    d_a = _mm(sv["h1"], dza, mode="tn", name=n + "dw_a", out_dtypes=(BF16,), plan=plan)
    d_g = _mm(sv["h1"], dzg, mode="tn", name=n + "dw_g", out_dtypes=(BF16,))
    own["w_in"] = _cols_by_owner(jnp.concatenate([d_a[:, :N_IN_A], d_g], axis=1)).transpose(0, 2, 1)
    to_sibling("dh1_a", ["w_in"])
    dh1 = _mm(dza, w["w_a"], mode="nt", name=n + "dh1_a", plan=plan)
    if prev is None:
        to_chips("dh1_g", ["w_in"])
    else:
        to_chips("flash", ["w_in"], prefix=f"l{l - 1}_b_")
    dh1 = _mm(dzg, w["w_g"], mode="nt", name=n + "dh1_g", extras=(dh1,), epilogue=lambda acc, e: (acc + e,), plan=plan)
    if prev is None:
        dx, g["pre_mix_g"] = _pre_bwd(dh1, sv["x"], w["pre_mix_g"], dx1, n + "pre_mix")
        return dx, None, g
    dx, df_prev, g["pre_mix_g"], g_prev_post = _pre_bwd(dh1, sv["x"], w["pre_mix_g"], dx1, n + "pre_mix",
                                                         r_prev=prev[0], g_post_prev=prev[1])
    g["prev_post_mlp_g"] = g_prev_post
    return dx, df_prev, g


MIDSIZE = ("pool_proj", "conv_proj", "sgu_proj", "w_uq", "w_ukv", "attn_proj")
GATHER_STEPS = (
    ("l0_za", tuple((name, 0) for name in MIDSIZE)),
    ("l0_zg", (("w_out", 0),) + tuple((name, 1) for name in MIDSIZE)),
    ("l0_flash", (("w_in", 1), ("w_up", 0))),
    ("l0_proj", (("w_down", 0),)),
    ("l0_up", (("w_up", 1),)),
    ("l0_down", (("w_out", 1),)),
    ("l1_za", ()),
    ("l1_zg", ()),
    ("l1_flash", (("w_down", 1),)),
    ("l1_proj", ()),
)


def _plan_gathers(plan, gat, shards):
    first_half = {}
    for step, (kernel, keys) in enumerate(GATHER_STEPS):
        before = GATHER_STEPS[step - 1][1] if step else ()
        if not keys and not before:
            continue

        def make(keys=keys, before=before):
            ops = [_gather_pass(first_half[before])] if before else []
            return ops + ([_gather_own([shards[key] for key in keys])] if keys else [])

        def done(results, keys=keys, before=before):
            if before:
                gat.update(zip(before, results[0]))
            if keys:
                first_half[keys] = results[-1]

        plan.at(kernel, make, done)


def _local_step(x, positions, target, gat, small, shards, plan):
    tabs = _rope_tables(positions)
    _plan_gathers(plan, gat, shards)
    ws = [_layer_weights(gat, small, 0)]
    saved = []
    h = _pre_norm(x, ws[0]["pre_mix_g"], "l0_pre_mix")
    cur = x
    for l in range(DEPTH):
        x1, f, sv = _layer_fwd(cur, h, ws[l], tabs, l, plan)
        saved.append(sv)
        if l + 1 < DEPTH:
            ws.append(_layer_weights(gat, small, l + 1))
            cur, h = _post_pre(x1, f, ws[l]["post_mlp_g"], ws[l + 1]["pre_mix_g"], f"l{l}_post_mlp")
    top = DEPTH - 1
    dx, df, dg_post, loss = _final_loss(saved[top]["x1"], saved[top]["f"], ws[top]["post_mlp_g"], target, "loss")
    grads = [None] * DEPTH
    post_mlp = {top: dg_post}
    rs = {}
    for l in range(top, -1, -1):
        prev = (saved[l - 1]["f"], ws[l - 1]["post_mlp_g"]) if l > 0 else None
        dx, df, g = _layer_bwd(dx, df, saved[l], ws[l], tabs, l, prev, plan, rs)
        if l > 0:
            post_mlp[l - 1] = g.pop("prev_post_mlp_g")
        grads[l] = g
    for l in range(DEPTH):
        grads[l]["post_mlp_g"] = post_mlp[l]
    assert not plan.jobs, sorted(plan.jobs)
    return loss[0, 0], dx, grads, rs


def _small_grads(grads):
    small = {}

    def stack(fn):
        return jnp.stack([fn(grads[l]) for l in range(DEPTH)])

    for name, shape in SMALL:
        if name == "sgu_b":
            small[name] = stack(lambda g: g["sgu_b"][:, :, 0])
        else:
            small[name] = stack(lambda g, name=name, shape=shape: g[name].reshape(shape))
    small["conv_w"] = stack(lambda g: g["conv_w"][:CONV_WIDTH])
    return small


SMALL_ROWS = sum(DEPTH * math.prod(shape) // 128 for _, shape in SMALL)
CONVW_ROWS = DEPTH * CONV_WIDTH * BR // 128


def _pack_small(parts):
    return jnp.concatenate([parts[name].astype(F32).reshape(-1, 128) for name, _ in SMALL], axis=0)


def _unpack_small(buf):
    out, off = {}, 0
    for name, shape in SMALL:
        rows = DEPTH * math.prod(shape) // 128
        out[name] = buf[off:off + rows].reshape((DEPTH,) + shape)
        off += rows
    return out


def _mesh_pos():
    return lax.axis_index("x"), lax.axis_index("y"), lax.axis_index("c")


def _all_gather(shards, name):
    n = len(shards)

    def body(*refs):
        x_refs, out_refs = refs[:n], refs[n:2 * n]
        send_sems, recv_sems, local_sems = refs[2 * n:]
        x, y, c = _mesh_pos()
        me, sibling = (x, y, c), (x, y, 1 - c)
        chips = [(1 - x, y), (x, 1 - y), (1 - x, 1 - y)]

        def rows(t, px, py, pc):
            return out_refs[t].at[4 * px + 2 * py + pc]

        def copy(t, k, block, to, src=None):
            return pltpu.make_async_remote_copy(
                src_ref=rows(t, *block) if src is None else src, dst_ref=rows(t, *block), send_sem=send_sems.at[t, k],
                recv_sem=recv_sems.at[t, k], device_id=to, device_id_type=MESH)

        mine = [pltpu.make_async_copy(x_refs[t], rows(t, *me), local_sems.at[t]) for t in range(n)]
        for cp in mine:
            cp.start()
        first = []
        for t in range(n):
            first.append(copy(t, 0, me, sibling, src=x_refs[t]))
            first += [copy(t, 1 + j, me, (*chip, c), src=x_refs[t]) for j, chip in enumerate(chips)]
        for cp in first:
            cp.start()
        passed = []
        for t in range(n):
            for j, chip in enumerate(chips):
                copy(t, 1 + j, (*chip, c), me).wait_recv()
                passed.append(copy(t, 4 + j, (*chip, c), sibling))
                passed[-1].start()
        for t in range(n):
            copy(t, 0, sibling, me).wait_recv()
            for j, chip in enumerate(chips):
                copy(t, 4 + j, (*chip, 1 - c), me).wait_recv()
        for cp in first + passed:
            cp.wait_send()
        for cp in mine:
            cp.wait()

    hbm = pl.BlockSpec(memory_space=pl.ANY)
    return pl.pallas_call(
        body, name=name, out_shape=[jax.ShapeDtypeStruct((N_DEV,) + a.shape, a.dtype) for a in shards],
        in_specs=[hbm] * n, out_specs=[hbm] * n,
        scratch_shapes=[pltpu.SemaphoreType.DMA((n, 7)), pltpu.SemaphoreType.DMA((n, 7)),
                        pltpu.SemaphoreType.DMA((n,))])(*shards)


def _shard_tile(r, c_, cap_bytes=3 << 19):
    best = 0
    for t in range(16, r + 1, 16):
        if r % t == 0 and t * c_ * 4 <= cap_bytes:
            best = t
    if best or r * c_ * 4 <= cap_bytes:
        return (best or r), c_
    tc = max(t for t in range(128, c_ + 1, 128) if c_ % t == 0 and r * t * 4 <= cap_bytes)
    return r, tc


def _rs_chip_sum(g, t, name):
    _, r, c_ = g.shape
    tr, tc = _shard_tile(r, c_)
    core = lax.axis_index("c").astype(jnp.int32).reshape(1)

    def body(core_ref, g_ref, t_ref, p_ref):
        p_ref[...] = (g_ref[...].astype(F32) + t_ref[...].astype(F32)).astype(p_ref.dtype)

    return pl.pallas_call(
        body, name=name, out_shape=jax.ShapeDtypeStruct((4, r, c_), g.dtype),
        grid_spec=pltpu.PrefetchScalarGridSpec(
            num_scalar_prefetch=1, grid=(4, r // tr, c_ // tc),
            in_specs=[pl.BlockSpec((1, tr, tc), lambda k, i, j, core_ref: (2 * k + core_ref[0], i, j)),
                      pl.BlockSpec((1, tr, tc), lambda k, i, j, core_ref: (k, i, j))],
            out_specs=pl.BlockSpec((1, tr, tc), lambda k, i, j, core_ref: (k, i, j))),
        compiler_params=_params(("parallel", "parallel", "parallel")))(core, g, t)


def _adamw_math(w, g, m, v):
    m = ADAM_B1 * m + (1.0 - ADAM_B1) * g
    v = ADAM_B2 * v + (1.0 - ADAM_B2) * jnp.square(g)
    m_hat = m / (1.0 - ADAM_B1 ** ADAM_STEP)
    v_hat = v / (1.0 - ADAM_B2 ** ADAM_STEP)
    delta = -ADAM_LR * (m_hat / (jnp.sqrt(v_hat) + ADAM_EPS) + ADAM_WD * w)
    return delta, m, v


def _adamw_big(ps, ts, w, m, v, name):
    _, r, c_ = w.shape
    tr, tc = _shard_tile(r, c_)
    chip = (2 * lax.axis_index("x") + lax.axis_index("y")).astype(jnp.int32).reshape(1)

    def body(chip_ref, p0, t0, p1, t1, w_ref, m_ref, v_ref, g_out, d_out, m_out, v_out):
        def update(p_ref, t_ref):
            g = p_ref[0].astype(F32) + t_ref[0].astype(F32) + t_ref[1].astype(F32) + t_ref[2].astype(F32)
            g_out[0] = g
            d_out[0], m_out[0], v_out[0] = _adamw_math(w_ref[0], g, m_ref[0], v_ref[0])

        pl.when(pl.program_id(0) == 0)(functools.partial(update, p0, t0))
        pl.when(pl.program_id(0) == 1)(functools.partial(update, p1, t1))

    def grad_specs(layer):
        def at(l, i, j):
            return jnp.where(l == layer, i, 0), jnp.where(l == layer, j, 0)
        return [pl.BlockSpec((1, tr, tc), lambda l, i, j, chip_ref: (chip_ref[0], *at(l, i, j))),
                pl.BlockSpec((3, tr, tc), lambda l, i, j, chip_ref: (0, *at(l, i, j)))]

    nat = pl.BlockSpec((1, tr, tc), lambda l, i, j, chip_ref: (l, i, j))
    shape = jax.ShapeDtypeStruct(w.shape, F32)
    return pl.pallas_call(
        body, name=name, out_shape=[shape] * 4,
        grid_spec=pltpu.PrefetchScalarGridSpec(
            num_scalar_prefetch=1, grid=(DEPTH, r // tr, c_ // tc),
            in_specs=grad_specs(0) + grad_specs(1) + [nat, nat, nat], out_specs=[nat] * 4),
        compiler_params=_params(("parallel", "parallel", "parallel"), VMEM_BIG))(
            chip, ps[0], ts[0], ps[1], ts[1], w, m, v)


def _sum_devices(parts, name):
    _, r, c_ = parts.shape

    def body(p_ref, o_ref):
        acc = p_ref[0]
        for d in range(1, N_DEV):
            acc = acc + p_ref[d]
        o_ref[...] = acc

    return pl.pallas_call(body, name=name, out_shape=jax.ShapeDtypeStruct((r, c_), F32),
                          compiler_params=_params(None, VMEM_BIG))(parts)


def _adamw_small(w, g, m, v, name):
    def body(w_ref, g_ref, m_ref, v_ref, d_out, m_out, v_out):
        d_out[...], m_out[...], v_out[...] = _adamw_math(w_ref[...], g_ref[...], m_ref[...], v_ref[...])

    shape = jax.ShapeDtypeStruct(w.shape, F32)
    return pl.pallas_call(body, name=name, out_shape=[shape] * 3)(w, g, m, v)


def kernel(x, positions, pre_mix_g, w_in, pool_w, pool_scale, pool_proj, conv_w, conv_b, conv_norm_g, conv_norm_b, conv_proj, sgu_norm_g, sgu_norm_b, sgu_w, sgu_b, sgu_proj, q_norm_g, w_uq, kv_norm_g, w_ukv, attn_proj, w_out, post_mix_g, pre_mlp_g, w_up, w_down, post_mlp_g, loss_target, m_pre_mix_g, m_w_in, m_pool_w, m_pool_scale, m_pool_proj, m_conv_w, m_conv_b, m_conv_norm_g, m_conv_norm_b, m_conv_proj, m_sgu_norm_g, m_sgu_norm_b, m_sgu_w, m_sgu_b, m_sgu_proj, m_q_norm_g, m_w_uq, m_kv_norm_g, m_w_ukv, m_attn_proj, m_w_out, m_post_mix_g, m_pre_mlp_g, m_w_up, m_w_down, m_post_mlp_g, v_pre_mix_g, v_w_in, v_pool_w, v_pool_scale, v_pool_proj, v_conv_w, v_conv_b, v_conv_norm_g, v_conv_norm_b, v_conv_proj, v_sgu_norm_g, v_sgu_norm_b, v_sgu_w, v_sgu_b, v_sgu_proj, v_q_norm_g, v_w_uq, v_kv_norm_g, v_w_ukv, v_attn_proj, v_w_out, v_post_mix_g, v_pre_mlp_g, v_w_up, v_w_down, v_post_mlp_g):
    args = dict(locals())
    wts = {n: args[n] for n in WEIGHTS}
    mom1 = {n: args["m_" + n] for n in WEIGHTS}
    mom2 = {n: args["v_" + n] for n in WEIGHTS}
    dev = 4 * lax.axis_index("x") + 2 * lax.axis_index("y") + lax.axis_index("c")

    shards = {(name, l): wts[name][l].astype(BF16) for name, _, _ in BIG for l in range(DEPTH)}
    taps = jnp.pad(conv_w.reshape(-1, 128), ((0, 1), (0, 0)))
    gathered = _all_gather([shards[("w_in", 0)], taps], "gather_first")
    gat = {("w_in", 0): gathered[0]}
    taps = gathered[-1][:, :CONV_WIDTH].reshape(N_DEV, DEPTH, CONV_WIDTH, BR // N_DEV)
    small = {n: wts[n] for n, _ in SMALL}
    small["conv_w"] = taps.transpose(1, 2, 0, 3).reshape(DEPTH, CONV_WIDTH, BR)

    loss_part, grad_x, grads, rs = _local_step(x[0], positions, loss_target[0], gat, small, shards, _Plan())
    small_g = _small_grads(grads)
    loss = lax.psum(loss_part, ("x", "y", "c"))

    out = {"grad": {}, "delta": {}, "new_m": {}, "new_v": {}}
    for name, _, _ in BIG:
        flip = (lambda a: a.swapaxes(1, 2)) if name == "w_in" else (lambda a: a)
        res = _adamw_big([rs[(name, l)][0] for l in range(DEPTH)], [rs[(name, l)][1] for l in range(DEPTH)],
                         flip(wts[name]), flip(mom1[name]), flip(mom2[name]), "adamw_" + name)
        res = [flip(buf) for buf in res]
        for key, buf in zip(("grad", "delta", "new_m", "new_v"), res):
            out[key][name] = buf

    part = jnp.concatenate([_pack_small(small_g), small_g["conv_w"].reshape(-1, 128)], axis=0)
    total = _sum_devices(_all_gather([part], "gather_small_grads")[0], "sum_small_grads")
    g_small = total[:SMALL_ROWS]
    d_small, m_small, v_small = _adamw_small(_pack_small(wts), g_small, _pack_small(mom1), _pack_small(mom2), "adamw_small")
    for key, buf in (("grad", g_small), ("delta", d_small), ("new_m", m_small), ("new_v", v_small)):
        out[key].update(_unpack_small(buf))
    g_taps = total[SMALL_ROWS:].reshape(DEPTH, CONV_WIDTH, N_DEV, BR // N_DEV)
    g_taps = lax.dynamic_index_in_dim(g_taps, dev, axis=2, keepdims=False)
    flat = lambda a: a.reshape(-1, 128)
    d_taps, m_taps, v_taps = _adamw_small(flat(conv_w), flat(g_taps), flat(m_conv_w), flat(v_conv_w), "adamw_taps")
    for key, buf in (("grad", g_taps), ("delta", d_taps), ("new_m", m_taps), ("new_v", v_taps)):
        out[key]["conv_w"] = buf.reshape(conv_w.shape)

    return (loss, grad_x[None], *[out["grad"][n] for n in WEIGHTS], *[out["delta"][n] for n in WEIGHTS],
            *[out["new_m"][n] for n in WEIGHTS], *[out["new_v"][n] for n in WEIGHTS])
```

```python
import collections
import functools
import math

import jax
import jax.numpy as jnp
from jax import lax
from jax.experimental import pallas as pl
from jax.experimental.pallas import tpu as pltpu

F32 = jnp.float32
BF16 = jnp.bfloat16

D_MODEL = 2048
DEPTH = 2
EPS = 1e-6
N_BRANCH = 4
D_FF = 4 * D_MODEL
UP_SHARD = D_FF // 8
POOL_WINDOWS = (2, 4, 8, 16)
CONV_WIDTH = 31
CHUNK = 128
MLA_HEADS = 8
QK_NOPE = 128
QK_ROPE = 64
V_DIM = 128
ROPE_THETA = 10000.0
GDIM = 128
BR = 512
N_IN_A = 3648
ZA = 3712
N_GATE = N_BRANCH * D_MODEL
N_IN = N_IN_A + N_GATE
QW = 256
ACT_CAT = 3 * BR + MLA_HEADS * V_DIM
ATT_SCALE = (QK_NOPE + QK_ROPE) ** -0.5
NEG = -1e30

ADAM_LR = 0.001
ADAM_B1 = 0.9
ADAM_B2 = 0.999
ADAM_EPS = 1e-08
ADAM_WD = 0.01
ADAM_STEP = 10

N_DEV = 8
PACK_W = 1024
VMEM_BIG = 48 * 1024 * 1024
MESH = pl.DeviceIdType.MESH

BIG = (
    ("w_in", 1, (2048, 1480)),
    ("pool_proj", 1, (512, 256)),
    ("conv_proj", 1, (512, 256)),
    ("sgu_proj", 1, (512, 256)),
    ("w_uq", 1, (512, 192)),
    ("w_ukv", 1, (512, 256)),
    ("attn_proj", 1, (1024, 256)),
    ("w_out", 0, (256, 2048)),
    ("w_up", 1, (2048, 1024)),
    ("w_down", 0, (1024, 2048)),
)
SMALL = (
    ("pre_mix_g", (2048,)), ("pool_w", (4, 128, 128)), ("pool_scale", (512,)), ("conv_b", (512,)),
    ("conv_norm_g", (512,)), ("conv_norm_b", (512,)), ("sgu_norm_g", (512,)), ("sgu_norm_b", (512,)),
    ("sgu_w", (4, 128, 128)), ("sgu_b", (4, 128)), ("q_norm_g", (512,)), ("kv_norm_g", (512,)),
    ("post_mix_g", (2048,)), ("pre_mlp_g", (2048,)), ("post_mlp_g", (2048,)),
)
WEIGHTS = ("pre_mix_g", "w_in", "pool_w", "pool_scale", "pool_proj", "conv_w", "conv_b", "conv_norm_g", "conv_norm_b",
           "conv_proj", "sgu_norm_g", "sgu_norm_b", "sgu_w", "sgu_b", "sgu_proj", "q_norm_g", "w_uq", "kv_norm_g",
           "w_ukv", "attn_proj", "w_out", "post_mix_g", "pre_mlp_g", "w_up", "w_down", "post_mlp_g")


def _params(sem=None, vmem=None):
    return pltpu.CompilerParams(dimension_semantics=sem, vmem_limit_bytes=vmem)


def _tile(dim, pref):
    if dim <= pref:
        return dim
    best = 0
    for t in range(128, pref + 1, 128):
        if dim % t == 0:
            best = t
    return best if best >= 256 else dim


def _sigmoid(x):
    return 1.0 / (1.0 + jnp.exp(-x))


def _gelu(x):
    k = math.sqrt(2.0 / math.pi)
    return 0.5 * x * (1.0 + jnp.tanh(k * (x + 0.044715 * x * x * x)))


def _gelu_grad(x):
    k = math.sqrt(2.0 / math.pi)
    t = jnp.tanh(k * (x + 0.044715 * x * x * x))
    return 0.5 * (1.0 + t) + 0.5 * x * (1.0 - t * t) * k * (1.0 + 3.0 * 0.044715 * x * x)


def _rms(x, g):
    r = lax.rsqrt(jnp.mean(x * x, axis=-1, keepdims=True) + EPS)
    return x * r * g


def _rms_bwd(x, g, dy):
    r = lax.rsqrt(jnp.mean(x * x, axis=-1, keepdims=True) + EPS)
    dyg = dy * g
    dx = r * dyg - x * (r * r * r) * jnp.mean(dyg * x, axis=-1, keepdims=True)
    return dx, dy * x * r


def _ln_stats(x):
    mu = jnp.mean(x, axis=-1, keepdims=True)
    xc = x - mu
    r = lax.rsqrt(jnp.mean(xc * xc, axis=-1, keepdims=True) + EPS)
    return xc * r, r


def _ln_bwd(xh, r, g, dy):
    dxh = dy * g
    return r * (dxh - jnp.mean(dxh, axis=-1, keepdims=True) - xh * jnp.mean(dxh * xh, axis=-1, keepdims=True))


def _rot_half(x, width, off):
    n = x.shape[-1]
    lane = lax.broadcasted_iota(jnp.int32, x.shape, x.ndim - 1) % width
    return jnp.where(lane - off < QK_ROPE // 2, -pltpu.roll(x, n - QK_ROPE // 2, x.ndim - 1),
                     pltpu.roll(x, QK_ROPE // 2, x.ndim - 1))


def _colsum_into(ref, val, first):
    s = jnp.sum(val, axis=0, keepdims=True)

    @pl.when(first)
    def _():
        ref[...] = s

    @pl.when(jnp.logical_not(first))
    def _():
        ref[...] += s


Exchange = collections.namedtuple("Exchange", "inputs out_shapes aliases n_pairs n_local build")


class _Plan:
    def __init__(self):
        self.jobs = {}

    def at(self, kernel, make, done):
        self.jobs.setdefault(kernel, []).append((make, done))

    def take(self, kernel):
        return self.jobs.pop(kernel, [])


def _pcall(body, operands, *, name, grid, in_specs, out_specs, out_shape, scratch_shapes=(), sem=None, vmem=None,
           plan=None):
    jobs = plan.take(name) if plan is not None else []
    if not jobs:
        return pl.pallas_call(body, name=name, grid=grid, in_specs=list(in_specs), out_specs=list(out_specs),
                              out_shape=list(out_shape), scratch_shapes=list(scratch_shapes),
                              compiler_params=_params(sem, vmem))(*operands)
    made = [(make(), done) for make, done in jobs]
    comm = [op for ops, _ in made for op in ops]
    n_in, n_out, n_scr = len(in_specs), len(out_shape), len(scratch_shapes)
    c_in = [a for op in comm for a in op.inputs]
    c_out = [s for op in comm for s in op.out_shapes]
    sems, aliases, i_off, o_off = [], {}, n_in, n_out
    for op in comm:
        sems += [pltpu.SemaphoreType.DMA((op.n_pairs,)), pltpu.SemaphoreType.DMA((op.n_pairs,)),
                 pltpu.SemaphoreType.DMA((max(op.n_local, 1),))]
        for src, dst in op.aliases.items():
            aliases[i_off + src] = o_off + dst
        i_off += len(op.inputs)
        o_off += len(op.out_shapes)

    def carrier(*refs):
        ins, cins = refs[:n_in], refs[n_in:n_in + len(c_in)]
        base = n_in + len(c_in)
        outs, couts = refs[base:base + n_out], refs[base + n_out:base + n_out + len(c_out)]
        base += n_out + len(c_out)
        scr, csems = refs[base:base + n_scr], refs[base + n_scr:]
        ids = [pl.program_id(ax) for ax in range(len(grid))]
        first = functools.reduce(jnp.logical_and, [i == 0 for i in ids])
        last = functools.reduce(jnp.logical_and, [i == g - 1 for i, g in zip(ids, grid)])

        def pieces():
            res, ci, co = [], 0, 0
            for k, op in enumerate(comm):
                res.append(op.build(cins[ci:ci + len(op.inputs)], couts[co:co + len(op.out_shapes)],
                                    *csems[3 * k:3 * k + 3]))
                ci += len(op.inputs)
                co += len(op.out_shapes)
            return res

        @pl.when(first)
        def _():
            for sends, _, local in pieces():
                for cp in local + sends:
                    cp.start()

        body(*ins, *outs, *scr)

        @pl.when(last)
        def _():
            for sends, recvs, local in pieces():
                for cp in recvs:
                    cp.wait_recv()
                for cp in sends:
                    cp.wait_send()
                for cp in local:
                    cp.wait()

    hbm = pl.BlockSpec(memory_space=pl.ANY)
    res = pl.pallas_call(
        carrier, name=name, grid=grid, in_specs=list(in_specs) + [hbm] * len(c_in),
        out_specs=list(out_specs) + [hbm] * len(c_out), out_shape=list(out_shape) + c_out,
        scratch_shapes=list(scratch_shapes) + sems, input_output_aliases=aliases,
        compiler_params=_params(("arbitrary",) * len(grid), vmem))(*operands, *c_in)
    pos = n_out
    for ops, done in made:
        results = []
        for op in ops:
            results.append(list(res[pos:pos + len(op.out_shapes)]))
            pos += len(op.out_shapes)
        done(results)
    return list(res[:n_out])


def _block(ref, px, py, pc):
    return ref.at[4 * px + 2 * py + pc]


def _remote(src, dst, send_sems, recv_sems, k, to):
    return pltpu.make_async_remote_copy(src_ref=src, dst_ref=dst, send_sem=send_sems.at[k], recv_sem=recv_sems.at[k],
                                        device_id=to, device_id_type=MESH)


def _gather_own(shards):
    n = len(shards)

    def build(ins, outs, send_sems, recv_sems, local_sems):
        x, y, c = _mesh_pos()
        peers = [(x, y, 1 - c), (1 - x, y, c), (x, 1 - y, c), (1 - x, 1 - y, c)]
        sends, recvs, local = [], [], []
        for t in range(n):
            local.append(pltpu.make_async_copy(ins[t], _block(outs[t], x, y, c), local_sems.at[t]))
            for k, peer in enumerate(peers):
                sends.append(_remote(ins[t], _block(outs[t], x, y, c), send_sems, recv_sems, 4 * t + k, peer))
                recvs.append(_remote(ins[t], _block(outs[t], *peer), send_sems, recv_sems, 4 * t + k, peer))
        return sends, recvs, local

    return Exchange(list(shards), [jax.ShapeDtypeStruct((N_DEV,) + a.shape, a.dtype) for a in shards], {}, 4 * n, n, build)


def _gather_pass(bufs):
    n = len(bufs)

    def build(ins, outs, send_sems, recv_sems, local_sems):
        x, y, c = _mesh_pos()
        chips = [(1 - x, y), (x, 1 - y), (1 - x, 1 - y)]
        sends, recvs = [], []
        for t in range(n):
            for j, chip in enumerate(chips):
                mine, theirs = _block(outs[t], *chip, c), _block(outs[t], *chip, 1 - c)
                sends.append(_remote(mine, mine, send_sems, recv_sems, 3 * t + j, (x, y, 1 - c)))
                recvs.append(_remote(mine, theirs, send_sems, recv_sems, 3 * t + j, (x, y, 1 - c)))
        return sends, recvs, []

    return Exchange(list(bufs), [jax.ShapeDtypeStruct(a.shape, a.dtype) for a in bufs], {t: t for t in range(n)},
                    3 * n, 0, build)


def _scatter_sibling(gs):
    n = len(gs)

    def build(ins, outs, send_sems, recv_sems, local_sems):
        x, y, c = _mesh_pos()
        sends, recvs = [], []
        for t in range(n):
            for k in range(4):
                cp = _remote(ins[t].at[2 * k + 1 - c], outs[t].at[k], send_sems, recv_sems, 4 * t + k, (x, y, 1 - c))
                sends.append(cp)
                recvs.append(cp)
        return sends, recvs, []

    return Exchange(list(gs), [jax.ShapeDtypeStruct((4,) + g.shape[1:], g.dtype) for g in gs], {}, 4 * n, 0, build)


def _scatter_chips(ps):
    n = len(ps)

    def build(ins, outs, send_sems, recv_sems, local_sems):
        x, y, c = _mesh_pos()
        chips = [(1 - x, y), (x, 1 - y), (1 - x, 1 - y)]
        sends, recvs = [], []
        for t in range(n):
            for j, (cx, cy) in enumerate(chips):
                cp = _remote(ins[t].at[2 * cx + cy], outs[t].at[j], send_sems, recv_sems, 3 * t + j, (cx, cy, c))
                sends.append(cp)
                recvs.append(cp)
        return sends, recvs, []

    return Exchange(list(ps), [jax.ShapeDtypeStruct((3,) + p.shape[1:], p.dtype) for p in ps], {}, 3 * n, 0, build)


_DIMS = {"nn": ((1,), (0,)), "nt": ((1,), (1,)), "tn": ((0,), (0,))}


def _mm_call(a, b, *, mode, name, grid, kaxis, nk, a_spec, b_spec, o_specs, out_shape, acc_shape,
             extras=(), e_specs=(), epilogue=None, active=None, plan=None):
    ne, no = len(extras), len(out_shape)

    def body(a_ref, b_ref, *rest):
        e_refs, o_refs, acc_ref = rest[:ne], rest[ne:ne + no], rest[ne + no]
        ids = [pl.program_id(ax) for ax in range(len(grid))]
        k = ids[kaxis]

        def finish(acc):
            outs = (acc,) if epilogue is None else epilogue(acc, *[e[...] for e in e_refs])
            for o_ref, val in zip(o_refs, outs):
                o_ref[...] = val.astype(o_ref.dtype)

        def step():
            prod = lax.dot_general(a_ref[...], b_ref[...], (_DIMS[mode], ((), ())), preferred_element_type=F32)
            if nk == 1:
                finish(prod)
                return

            @pl.when(k == 0)
            def _():
                acc_ref[...] = prod

            @pl.when(k > 0)
            def _():
                acc_ref[...] += prod

        if active is None:
            step()
        else:
            pl.when(active(*ids))(step)
        if nk > 1:
            @pl.when(k == nk - 1)
            def _():
                finish(acc_ref[...])

    sem = tuple("arbitrary" if ax == kaxis else "parallel" for ax in range(len(grid)))
    scratch = pltpu.VMEM(acc_shape if nk > 1 else (8, 128), F32)
    return _pcall(body, (a, b, *extras), name=name, grid=grid, in_specs=[a_spec, b_spec, *e_specs],
                  out_specs=list(o_specs), out_shape=list(out_shape), scratch_shapes=[scratch],
                  sem=sem, vmem=VMEM_BIG, plan=plan)


def _mm(a, b, *, mode, name, out_dtypes=(F32,), extras=(), epilogue=None, tm=1024, tn=1024, tk=2048,
        mnk=None, b_spec=None, o_spec=None, out_struct=None, plan=None):
    if mnk is not None:
        m, n, k = mnk
    elif mode == "nn":
        (m, k), (_, n) = a.shape, b.shape
    elif mode == "nt":
        (m, k), (n, _) = a.shape, b.shape
    else:
        (k, m), (_, n) = a.shape, b.shape
    tm, tn, tk = _tile(m, tm), _tile(n, tn), _tile(k, tk)
    if tn > 2048:
        tm, tk = _tile(m, 512), _tile(k, 512)
    if tk > 2048:
        tm, tn = _tile(m, 512), _tile(n, 512)
    nk = k // tk
    if mode == "tn":
        a_spec = pl.BlockSpec((tk, tm), lambda i, j, kk: (kk, i))
    else:
        a_spec = pl.BlockSpec((tm, tk), lambda i, j, kk: (i, kk))
    if b_spec is not None:
        b_spec = b_spec(tn, tk)
    elif mode == "nt":
        b_spec = pl.BlockSpec((tn, tk), lambda i, j, kk: (j, kk))
    else:
        b_spec = pl.BlockSpec((tk, tn), lambda i, j, kk: (kk, j))

    def e_spec(e):
        if e.shape[1] == tn and n != tn:
            return pl.BlockSpec((tm, tn), lambda i, j, kk: (i, 0))
        return pl.BlockSpec((tm, tn), lambda i, j, kk: (i, j))

    e_specs = [e_spec(e) for e in extras]
    if o_spec is not None:
        o_specs = [o_spec(tm, tn)]
        out_shape = [out_struct]
    else:
        o_specs = [pl.BlockSpec((tm, tn), lambda i, j, kk: (i, j)) for _ in out_dtypes]
        out_shape = [jax.ShapeDtypeStruct((m, n), dt) for dt in out_dtypes]
    outs = _mm_call(a, b, mode=mode, name=name, grid=(m // tm, n // tn, nk), kaxis=2, nk=nk, a_spec=a_spec,
                    b_spec=b_spec, o_specs=o_specs, out_shape=out_shape, acc_shape=(tm, tn), extras=extras,
                    e_specs=e_specs, epilogue=epilogue, plan=plan)
    return outs[0] if len(outs) == 1 else outs


def _branch_of(kb):
    return jnp.minimum(kb, N_BRANCH - 1)


def _proj_merge_fwd(act_cat, proj_cat, zg, name, plan=None):
    s = act_cat.shape[0]
    tm, tn = _tile(s, 1024), 1024
    nj = D_MODEL // tn
    last = N_BRANCH - 1

    def kb(b, k):
        return jnp.where(b < last, b, last + k)

    def body(a_ref, b_ref, z_ref, y_ref, m_ref, acc_ref, sum_ref):
        b, k = pl.program_id(2), pl.program_id(3)

        @pl.when(jnp.logical_or(b == last, k == 0))
        def _():
            prod = jnp.dot(a_ref[...], b_ref[...], preferred_element_type=F32)

            @pl.when(k == 0)
            def _():
                acc_ref[...] = prod

            @pl.when(k > 0)
            def _():
                acc_ref[...] += prod

        @pl.when(k == 1)
        def _():
            y = acc_ref[...]
            y_ref[...] = y.astype(BF16)
            gated = _sigmoid(z_ref[...]) * y

            @pl.when(b == 0)
            def _():
                sum_ref[...] = gated

            @pl.when(b > 0)
            def _():
                sum_ref[...] += gated

            @pl.when(b == last)
            def _():
                m_ref[...] = sum_ref[...].astype(BF16)

    wide = pl.BlockSpec((tm, tn), lambda i, j, b, k: (i, b * nj + j))
    y, merged = _pcall(
        body, (act_cat, proj_cat, zg), name=name, grid=(s // tm, nj, N_BRANCH, 2),
        in_specs=[pl.BlockSpec((tm, BR), lambda i, j, b, k: (i, kb(b, k))),
                  pl.BlockSpec((BR, tn), lambda i, j, b, k: (kb(b, k), j)), wide],
        out_specs=[wide, pl.BlockSpec((tm, tn), lambda i, j, b, k: (i, j))],
        out_shape=[jax.ShapeDtypeStruct((s, N_GATE), BF16), jax.ShapeDtypeStruct((s, D_MODEL), BF16)],
        scratch_shapes=[pltpu.VMEM((tm, tn), F32), pltpu.VMEM((tm, tn), F32)],
        sem=("parallel", "parallel", "arbitrary", "arbitrary"), vmem=VMEM_BIG, plan=plan)
    return y, merged


def _proj_bwd_act(dy, proj_cat, name, plan=None):
    s = dy.shape[0]
    tm, tk = _tile(s, 1024), D_MODEL
    nkk = D_MODEL // tk
    nkb = ACT_CAT // BR
    out = _mm_call(
        dy, proj_cat, mode="nt", name=name, grid=(s // tm, nkb, nkk), kaxis=2, nk=nkk,
        a_spec=pl.BlockSpec((tm, tk), lambda i, kb, k: (i, _branch_of(kb) * nkk + k)),
        b_spec=pl.BlockSpec((BR, tk), lambda i, kb, k: (kb, k)),
        o_specs=[pl.BlockSpec((tm, BR), lambda i, kb, k: (i, kb))],
        out_shape=[jax.ShapeDtypeStruct((s, ACT_CAT), F32)], acc_shape=(tm, BR), plan=plan)
    return out[0]


def _proj_bwd_w(act_cat, dy, name):
    s = dy.shape[0]
    tms, tn = _tile(s, 2048), 1024
    nj = D_MODEL // tn
    nkb = ACT_CAT // BR
    nm = s // tms
    out = _mm_call(
        act_cat, dy, mode="tn", name=name, grid=(nkb, nj, nm), kaxis=2, nk=nm,
        a_spec=pl.BlockSpec((tms, BR), lambda kb, j, m: (m, kb)),
        b_spec=pl.BlockSpec((tms, tn), lambda kb, j, m: (m, _branch_of(kb) * nj + j)),
        o_specs=[pl.BlockSpec((BR, tn), lambda kb, j, m: (kb, j))],
        out_shape=[jax.ShapeDtypeStruct((ACT_CAT, D_MODEL), BF16)], acc_shape=(BR, tn))
    return out[0]


def _row_specs(ts, n_full, n_vec):
    return ([pl.BlockSpec((ts, D_MODEL), lambda i: (i, 0))] * n_full
            + [pl.BlockSpec((1, D_MODEL), lambda i: (0, 0))] * n_vec)


def _pre_norm(x, g, name):
    s = x.shape[0]
    ts = _tile(s, 256)

    def body(x_ref, g_ref, h_ref):
        h_ref[...] = _rms(x_ref[...], g_ref[...]).astype(BF16)

    return pl.pallas_call(body, name=name, grid=(s // ts,), in_specs=_row_specs(ts, 1, 1),
                          out_specs=pl.BlockSpec((ts, D_MODEL), lambda i: (i, 0)),
                          out_shape=jax.ShapeDtypeStruct((s, D_MODEL), BF16), compiler_params=_params(("parallel",)))(x, g)


def _post_pre(x, r, g_post, g_next, name):
    s = x.shape[0]
    ts = _tile(s, 256)

    def body(x_ref, r_ref, gp_ref, gn_ref, xn_ref, h_ref):
        xn = x_ref[...] + _rms(r_ref[...], gp_ref[...])
        xn_ref[...] = xn
        h_ref[...] = _rms(xn, gn_ref[...]).astype(BF16)

    spec = pl.BlockSpec((ts, D_MODEL), lambda i: (i, 0))
    return pl.pallas_call(body, name=name, grid=(s // ts,), in_specs=_row_specs(ts, 2, 2), out_specs=[spec, spec],
                          out_shape=[jax.ShapeDtypeStruct((s, D_MODEL), F32), jax.ShapeDtypeStruct((s, D_MODEL), BF16)],
                          compiler_params=_params(("parallel",)))(x, r, g_post, g_next)


def _final_loss(x, r, g_post, target, name):
    s = x.shape[0]
    ts = _tile(s, 256)

    def body(x_ref, r_ref, gp_ref, t_ref, dy_ref, dr_ref, dg_ref, loss_ref):
        first = pl.program_id(0) == 0
        rv, gp = r_ref[...], gp_ref[...]
        diff = x_ref[...] + _rms(rv, gp) - t_ref[...]
        part = 0.5 * jnp.sum(jnp.mean(diff * diff, axis=-1, keepdims=True), axis=0, keepdims=True)
        dy = diff * (1.0 / D_MODEL)
        dy_ref[...] = dy
        dr, dg_rows = _rms_bwd(rv, gp, dy)
        dr_ref[...] = dr.astype(BF16)
        _colsum_into(dg_ref, dg_rows, first)
        _colsum_into(loss_ref, jnp.broadcast_to(part, (1, 128)), first)

    spec = pl.BlockSpec((ts, D_MODEL), lambda i: (i, 0))
    vec = pl.BlockSpec((1, D_MODEL), lambda i: (0, 0))
    return pl.pallas_call(
        body, name=name, grid=(s // ts,), in_specs=[spec, spec, vec, spec],
        out_specs=[spec, spec, vec, pl.BlockSpec((1, 128), lambda i: (0, 0))],
        out_shape=[jax.ShapeDtypeStruct((s, D_MODEL), F32), jax.ShapeDtypeStruct((s, D_MODEL), BF16),
                   jax.ShapeDtypeStruct((1, D_MODEL), F32), jax.ShapeDtypeStruct((1, 128), F32)],
        compiler_params=_params(("arbitrary",)))(x, r, g_post, target)


def _pre_bwd(dh, x, g_pre, dx_res, name, r_prev=None, g_post_prev=None):
    s = x.shape[0]
    ts = _tile(s, 256)
    chain = r_prev is not None

    def body(*refs):
        if chain:
            dh_ref, x_ref, res_ref, r_ref, g_ref, gp_ref, dx_ref, dr_ref, dg_ref, dgp_ref = refs
        else:
            dh_ref, x_ref, res_ref, g_ref, dx_ref, dg_ref = refs
        first = pl.program_id(0) == 0
        dxn, dg_rows = _rms_bwd(x_ref[...], g_ref[...], dh_ref[...])
        dx = res_ref[...] + dxn
        dx_ref[...] = dx
        _colsum_into(dg_ref, dg_rows, first)
        if chain:
            dr, dgp_rows = _rms_bwd(r_ref[...], gp_ref[...], dx)
            dr_ref[...] = dr.astype(BF16)
            _colsum_into(dgp_ref, dgp_rows, first)

    spec = pl.BlockSpec((ts, D_MODEL), lambda i: (i, 0))
    vec = pl.BlockSpec((1, D_MODEL), lambda i: (0, 0))
    full = jax.ShapeDtypeStruct((s, D_MODEL), F32)
    vshape = jax.ShapeDtypeStruct((1, D_MODEL), F32)
    if chain:
        return pl.pallas_call(
            body, name=name, grid=(s // ts,), in_specs=[spec] * 4 + [vec] * 2, out_specs=[spec, spec, vec, vec],
            out_shape=[full, jax.ShapeDtypeStruct((s, D_MODEL), BF16), vshape, vshape],
            compiler_params=_params(("arbitrary",)))(dh, x, dx_res, r_prev, g_pre, g_post_prev)
    return pl.pallas_call(
        body, name=name, grid=(s // ts,), in_specs=[spec] * 3 + [vec], out_specs=[spec, vec],
        out_shape=[full, vshape], compiler_params=_params(("arbitrary",)))(dh, x, dx_res, g_pre)


def _merge_bwd(dm, y, zg, name, plan=None):
    s = y.shape[0]
    ts, tc = _tile(s, 512), 1024
    nj = D_MODEL // tc

    def body(dm_ref, y_ref, z_ref, dy_ref, dz_ref):
        g = _sigmoid(z_ref[...])
        d = dm_ref[...]
        dy_ref[...] = (d * g).astype(BF16)
        dz_ref[...] = (d * y_ref[...] * g * (1.0 - g)).astype(BF16)

    blk = pl.BlockSpec((ts, tc), lambda i, j, b: (i, b * nj + j))
    shape = jax.ShapeDtypeStruct((s, N_GATE), BF16)
    return _pcall(body, (dm, y, zg), name=name, grid=(s // ts, nj, N_BRANCH),
                  in_specs=[pl.BlockSpec((ts, tc), lambda i, j, b: (i, j)), blk, blk], out_specs=[blk, blk],
                  out_shape=[shape, shape], sem=("parallel", "parallel", "parallel"), vmem=VMEM_BIG, plan=plan)


POOL_HALO = 16


def _pool_windows(ext_ref, ts, first_row):
    outs = []
    t = first_row + lax.broadcasted_iota(jnp.int32, (ts, GDIM), 0)
    for gi, w in enumerate(POOL_WINDOWS):
        cols = pl.ds(gi * GDIM, GDIM)
        acc = ext_ref[pl.ds(POOL_HALO, ts), cols]
        cur = acc
        for k in range(1, w):
            acc = acc + ext_ref[pl.ds(POOL_HALO - k, ts), cols]
        cnt = jnp.minimum(t + 1, w).astype(F32)
        outs.append(acc / cnt - cur)
    return outs


def _pool_fwd(za, pool_w, pool_scale, name):
    s = za.shape[0]
    ts = _tile(s, 512)
    hb = ts // POOL_HALO

    def body(a_ref, halo_ref, w_ref, sc_ref, o_ref, ext_ref):
        i = pl.program_id(0)
        ext_ref[pl.ds(0, POOL_HALO), :] = jnp.where(i > 0, halo_ref[...], 0.0)
        ext_ref[pl.ds(POOL_HALO, ts), :] = a_ref[...]
        pooled = _pool_windows(ext_ref, ts, i * ts)
        for gi in range(len(POOL_WINDOWS)):
            mixed = jnp.dot(pooled[gi].astype(BF16), w_ref[gi].astype(BF16), preferred_element_type=F32)
            o_ref[:, pl.ds(gi * GDIM, GDIM)] = (mixed * sc_ref[:, pl.ds(gi * GDIM, GDIM)]).astype(BF16)

    return pl.pallas_call(
        body, name=name, grid=(s // ts,),
        in_specs=[pl.BlockSpec((ts, BR), lambda i: (i, 0)),
                  pl.BlockSpec((POOL_HALO, BR), lambda i: (jnp.maximum(i * hb - 1, 0), 0)),
                  pl.BlockSpec((4, GDIM, GDIM), lambda i: (0, 0, 0)), pl.BlockSpec((1, BR), lambda i: (0, 0))],
        out_specs=pl.BlockSpec((ts, BR), lambda i: (i, 0)), out_shape=jax.ShapeDtypeStruct((s, BR), BF16),
        scratch_shapes=[pltpu.VMEM((POOL_HALO + ts, BR), F32)], compiler_params=_params(("parallel",)))(
            za, za, pool_w, pool_scale)


def _pool_bwd(dact, za, pool_w, pool_scale, name):
    s = za.shape[0]
    ts = _tile(s, 512)
    hb = ts // POOL_HALO
    n_t = s // ts

    def body(d_ref, dhalo_ref, a_ref, halo_ref, w_ref, sc_ref, dz_ref, dw_ref, dsc_ref, ext_ref, f_ref):
        i = pl.program_id(0)
        first = i == 0
        ext_ref[pl.ds(0, POOL_HALO), :] = jnp.where(i > 0, halo_ref[...], 0.0)
        ext_ref[pl.ds(POOL_HALO, ts), :] = a_ref[...]
        pooled = _pool_windows(ext_ref, ts, i * ts)
        d_tile = d_ref[...]
        d_next = jnp.where(i < n_t - 1, dhalo_ref[...], 0.0)
        t_ext = i * ts + lax.broadcasted_iota(jnp.int32, (ts + POOL_HALO, GDIM), 0)
        dsc = []
        for gi, w in enumerate(POOL_WINDOWS):
            cols = pl.ds(gi * GDIM, GDIM)
            wg = w_ref[gi].astype(BF16)
            sc = sc_ref[:, cols]
            pg = pooled[gi].astype(BF16)
            mixed = jnp.dot(pg, wg, preferred_element_type=F32)
            dsc.append(jnp.sum(d_tile[:, gi * GDIM:(gi + 1) * GDIM] * mixed, axis=0, keepdims=True))
            dmix = jnp.concatenate([d_tile[:, gi * GDIM:(gi + 1) * GDIM], d_next[:, gi * GDIM:(gi + 1) * GDIM]], axis=0) * sc
            dmix = dmix.astype(BF16)
            dwg = lax.dot_general(pg, dmix[:ts], (((0,), (0,)), ((), ())), preferred_element_type=F32)

            @pl.when(first)
            def _():
                dw_ref[gi] = dwg

            @pl.when(jnp.logical_not(first))
            def _():
                dw_ref[gi] += dwg

            dpool = lax.dot_general(dmix, wg, (((1,), (1,)), ((), ())), preferred_element_type=F32)
            f_ref[:, cols] = dpool / jnp.minimum(t_ext + 1, w).astype(F32)
            acc = f_ref[pl.ds(0, ts), cols]
            for k in range(1, w):
                acc = acc + f_ref[pl.ds(k, ts), cols]
            dz_ref[:, cols] = (acc - dpool[:ts]).astype(BF16)
        dsc_all = jnp.concatenate(dsc, axis=1)

        @pl.when(first)
        def _():
            dsc_ref[...] = dsc_all

        @pl.when(jnp.logical_not(first))
        def _():
            dsc_ref[...] += dsc_all

    n_hb = s // POOL_HALO
    return pl.pallas_call(
        body, name=name, grid=(n_t,),
        in_specs=[pl.BlockSpec((ts, BR), lambda i: (i, 0)),
                  pl.BlockSpec((POOL_HALO, BR), lambda i: (jnp.minimum((i + 1) * hb, n_hb - 1), 0)),
                  pl.BlockSpec((ts, BR), lambda i: (i, 0)),
                  pl.BlockSpec((POOL_HALO, BR), lambda i: (jnp.maximum(i * hb - 1, 0), 0)),
                  pl.BlockSpec((4, GDIM, GDIM), lambda i: (0, 0, 0)), pl.BlockSpec((1, BR), lambda i: (0, 0))],
        out_specs=[pl.BlockSpec((ts, BR), lambda i: (i, 0)), pl.BlockSpec((4, GDIM, GDIM), lambda i: (0, 0, 0)),
                   pl.BlockSpec((1, BR), lambda i: (0, 0))],
        out_shape=[jax.ShapeDtypeStruct((s, BR), BF16), jax.ShapeDtypeStruct((4, GDIM, GDIM), F32),
                   jax.ShapeDtypeStruct((1, BR), F32)],
        scratch_shapes=[pltpu.VMEM((POOL_HALO + ts, BR), F32), pltpu.VMEM((ts + POOL_HALO, BR), F32)],
        compiler_params=_params(("arbitrary",)))(dact, dact, za, za, pool_w, pool_scale)


CONV_HALO = 32
CONV_LEAD = CONV_HALO - (CONV_WIDTH - 1)


def _conv_fwd(za, conv_w, conv_b, ng, nb, name):
    s = za.shape[0]
    ts = _tile(s, 512)
    hb = ts // CONV_HALO

    def body(a_ref, g_ref, ah_ref, gh_ref, w_ref, b_ref, ng_ref, nb_ref, yc_ref, act_ref, ext_ref):
        i = pl.program_id(0)
        ext_ref[pl.ds(0, CONV_HALO), :] = jnp.where(i > 0, ah_ref[...] * _sigmoid(gh_ref[...]), 0.0)
        ext_ref[pl.ds(CONV_HALO, ts), :] = a_ref[...] * _sigmoid(g_ref[...])
        acc = jnp.zeros((ts, BR), F32) + b_ref[...]
        for k in range(CONV_WIDTH):
            acc = acc + w_ref[pl.ds(k, 1), :] * ext_ref[pl.ds(CONV_LEAD + k, ts), :]
        yc_ref[...] = acc
        xh, _ = _ln_stats(acc)
        ln = xh * ng_ref[...] + nb_ref[...]
        act_ref[...] = (ln * _sigmoid(ln)).astype(BF16)

    tile = lambda c: pl.BlockSpec((ts, BR), lambda i: (i, c))
    halo = lambda c: pl.BlockSpec((CONV_HALO, BR), lambda i: (jnp.maximum(i * hb - 1, 0), c))
    vec = pl.BlockSpec((1, BR), lambda i: (0, 0))
    return pl.pallas_call(
        body, name=name, grid=(s // ts,),
        in_specs=[tile(1), tile(2), halo(1), halo(2), pl.BlockSpec((CONV_WIDTH, BR), lambda i: (0, 0)), vec, vec, vec],
        out_specs=[pl.BlockSpec((ts, BR), lambda i: (i, 0))] * 2,
        out_shape=[jax.ShapeDtypeStruct((s, BR), F32), jax.ShapeDtypeStruct((s, BR), BF16)],
        scratch_shapes=[pltpu.VMEM((CONV_HALO + ts, BR), F32)], compiler_params=_params(("parallel",)))(
            za, za, za, za, conv_w, conv_b, ng, nb)


def _conv_bwd_norm(dact, yc, ng, nb, name):
    s = yc.shape[0]
    ts = _tile(s, 512)

    def body(d_ref, y_ref, ng_ref, nb_ref, dy_ref, db_ref, dng_ref, dnb_ref):
        first = pl.program_id(0) == 0
        xh, r = _ln_stats(y_ref[...])
        g = ng_ref[...]
        ln = xh * g + nb_ref[...]
        sg = _sigmoid(ln)
        dln = d_ref[...] * sg * (1.0 + ln * (1.0 - sg))
        dy = _ln_bwd(xh, r, g, dln)
        dy_ref[...] = dy
        _colsum_into(db_ref, dy, first)
        _colsum_into(dng_ref, dln * xh, first)
        _colsum_into(dnb_ref, dln, first)

    vec = pl.BlockSpec((1, BR), lambda i: (0, 0))
    vshape = jax.ShapeDtypeStruct((1, BR), F32)
    return pl.pallas_call(
        body, name=name, grid=(s // ts,),
        in_specs=[pl.BlockSpec((ts, BR), lambda i: (i, 1)), pl.BlockSpec((ts, BR), lambda i: (i, 0)), vec, vec],
        out_specs=[pl.BlockSpec((ts, BR), lambda i: (i, 0)), vec, vec, vec],
        out_shape=[jax.ShapeDtypeStruct((s, BR), F32), vshape, vshape, vshape],
        compiler_params=_params(("arbitrary",)))(dact, yc, ng, nb)


def _conv_bwd_taps(dyc, za, conv_w, name):
    s = za.shape[0]
    ts = _tile(s, 512)
    hb = ts // CONV_HALO
    n_t = s // ts
    n_hb = s // CONV_HALO

    def body(d_ref, dh_ref, a_ref, g_ref, ah_ref, gh_ref, w_ref, dz_ref, dw_ref, ext_ref, f_ref):
        i = pl.program_id(0)
        first = i == 0
        a, sg = a_ref[...], _sigmoid(g_ref[...])
        ext_ref[pl.ds(0, CONV_HALO), :] = jnp.where(i > 0, ah_ref[...] * _sigmoid(gh_ref[...]), 0.0)
        ext_ref[pl.ds(CONV_HALO, ts), :] = a * sg
        d = d_ref[...]
        f_ref[pl.ds(0, ts), :] = d
        f_ref[pl.ds(ts, CONV_HALO), :] = jnp.where(i < n_t - 1, dh_ref[...], 0.0)
        dglu = jnp.zeros((ts, BR), F32)
        rows = []
        for k in range(CONV_WIDTH):
            rows.append(jnp.sum(d * ext_ref[pl.ds(CONV_LEAD + k, ts), :], axis=0, keepdims=True))
            dglu = dglu + w_ref[pl.ds(k, 1), :] * f_ref[pl.ds(CONV_WIDTH - 1 - k, ts), :]
        rows.append(jnp.zeros((1, BR), F32))
        dw = jnp.concatenate(rows, axis=0)

        @pl.when(first)
        def _():
            dw_ref[...] = dw

        @pl.when(jnp.logical_not(first))
        def _():
            dw_ref[...] += dw

        dz_ref[:, pl.ds(0, BR)] = (dglu * sg).astype(BF16)
        dz_ref[:, pl.ds(BR, BR)] = (dglu * a * sg * (1.0 - sg)).astype(BF16)

    tile = lambda c: pl.BlockSpec((ts, BR), lambda i: (i, c))
    halo = lambda c: pl.BlockSpec((CONV_HALO, BR), lambda i: (jnp.maximum(i * hb - 1, 0), c))
    return pl.pallas_call(
        body, name=name, grid=(n_t,),
        in_specs=[pl.BlockSpec((ts, BR), lambda i: (i, 0)),
                  pl.BlockSpec((CONV_HALO, BR), lambda i: (jnp.minimum((i + 1) * hb, n_hb - 1), 0)),
                  tile(1), tile(2), halo(1), halo(2), pl.BlockSpec((CONV_WIDTH, BR), lambda i: (0, 0))],
        out_specs=[pl.BlockSpec((ts, 2 * BR), lambda i: (i, 0)), pl.BlockSpec((CONV_WIDTH + 1, BR), lambda i: (0, 0))],
        out_shape=[jax.ShapeDtypeStruct((s, 2 * BR), BF16), jax.ShapeDtypeStruct((CONV_WIDTH + 1, BR), F32)],
        scratch_shapes=[pltpu.VMEM((CONV_HALO + ts, BR), F32), pltpu.VMEM((ts + CONV_HALO, BR), F32)],
        compiler_params=_params(("arbitrary",)))(dyc, dyc, za, za, za, za, conv_w)


def _tril(w):
    r = lax.broadcasted_iota(jnp.int32, (CHUNK, CHUNK), 0)
    c = lax.broadcasted_iota(jnp.int32, (CHUNK, CHUNK), 1)
    return jnp.where(c <= r, w, 0.0)


def _sgu_fwd(za, ng, nb, sgu_w, bias_b, name):
    s = za.shape[0]
    ts = _tile(s, 512)

    def body(u_ref, v_ref, ng_ref, nb_ref, w_ref, b_ref, o_ref):
        u = _gelu(u_ref[...])
        xh, _ = _ln_stats(_gelu(v_ref[...]))
        vln = (xh * ng_ref[...] + nb_ref[...]).astype(BF16)
        for gi in range(4):
            wg = _tril(w_ref[gi]).astype(BF16)
            for n in range(ts // CHUNK):
                blk = vln[n * CHUNK:(n + 1) * CHUNK, gi * GDIM:(gi + 1) * GDIM]
                sp = jnp.dot(wg, blk, preferred_element_type=F32) + b_ref[gi]
                o_ref[pl.ds(n * CHUNK, CHUNK), pl.ds(gi * GDIM, GDIM)] = (
                    u[n * CHUNK:(n + 1) * CHUNK, gi * GDIM:(gi + 1) * GDIM] * sp).astype(BF16)

    vec = pl.BlockSpec((1, BR), lambda i: (0, 0))
    cube = pl.BlockSpec((4, CHUNK, GDIM), lambda i: (0, 0, 0))
    return pl.pallas_call(
        body, name=name, grid=(s // ts,),
        in_specs=[pl.BlockSpec((ts, BR), lambda i: (i, 3)), pl.BlockSpec((ts, BR), lambda i: (i, 4)), vec, vec, cube, cube],
        out_specs=pl.BlockSpec((ts, BR), lambda i: (i, 0)), out_shape=jax.ShapeDtypeStruct((s, BR), BF16),
        compiler_params=_params(("parallel",)))(za, za, ng, nb, sgu_w, bias_b)


def _sgu_bwd(dact, za, ng, nb, sgu_w, bias_b, name):
    s = za.shape[0]
    ts = _tile(s, 512)

    def body(d_ref, u_ref, v_ref, ng_ref, nb_ref, w_ref, b_ref, dz_ref, dw_ref, db_ref, dng_ref, dnb_ref, dv_ref):
        first = pl.program_id(0) == 0
        u_raw, v_raw = u_ref[...], v_ref[...]
        u = _gelu(u_raw)
        xh, r = _ln_stats(_gelu(v_raw))
        g = ng_ref[...]
        vln = (xh * g + nb_ref[...]).astype(BF16)
        d = d_ref[...]
        dsp = d * u
        dsp16 = dsp.astype(BF16)
        for gi in range(4):
            wg = _tril(w_ref[gi]).astype(BF16)
            dwg = jnp.zeros((CHUNK, CHUNK), F32)
            dbg = jnp.zeros((CHUNK, 1), F32)
            for n in range(ts // CHUNK):
                rows, cols = slice(n * CHUNK, (n + 1) * CHUNK), slice(gi * GDIM, (gi + 1) * GDIM)
                blk = vln[rows, cols]
                sp = jnp.dot(wg, blk, preferred_element_type=F32) + b_ref[gi]
                dz_ref[pl.ds(n * CHUNK, CHUNK), pl.ds(gi * GDIM, GDIM)] = (
                    d[rows, cols] * sp * _gelu_grad(u_raw[rows, cols])).astype(BF16)
                dv_ref[pl.ds(n * CHUNK, CHUNK), pl.ds(gi * GDIM, GDIM)] = lax.dot_general(
                    wg, dsp16[rows, cols], (((0,), (0,)), ((), ())), preferred_element_type=F32)
                dwg = dwg + lax.dot_general(dsp16[rows, cols], blk, (((1,), (1,)), ((), ())), preferred_element_type=F32)
                dbg = dbg + jnp.sum(dsp[rows, cols], axis=1, keepdims=True)
            dwg = _tril(dwg)

            @pl.when(first)
            def _():
                dw_ref[gi] = dwg
                db_ref[gi] = dbg

            @pl.when(jnp.logical_not(first))
            def _():
                dw_ref[gi] += dwg
                db_ref[gi] += dbg

        dvln = dv_ref[...]
        dz_ref[:, pl.ds(BR, BR)] = (_ln_bwd(xh, r, g, dvln) * _gelu_grad(v_raw)).astype(BF16)
        _colsum_into(dng_ref, dvln * xh, first)
        _colsum_into(dnb_ref, dvln, first)

    vec = pl.BlockSpec((1, BR), lambda i: (0, 0))
    cube = pl.BlockSpec((4, CHUNK, GDIM), lambda i: (0, 0, 0))
    vshape = jax.ShapeDtypeStruct((1, BR), F32)
    return pl.pallas_call(
        body, name=name, grid=(s // ts,),
        in_specs=[pl.BlockSpec((ts, BR), lambda i: (i, 2)), pl.BlockSpec((ts, BR), lambda i: (i, 3)),
                  pl.BlockSpec((ts, BR), lambda i: (i, 4)), vec, vec, cube, cube],
        out_specs=[pl.BlockSpec((ts, 2 * BR), lambda i: (i, 0)), cube, pl.BlockSpec((4, CHUNK, 1), lambda i: (0, 0, 0)),
                   vec, vec],
        out_shape=[jax.ShapeDtypeStruct((s, 2 * BR), BF16), jax.ShapeDtypeStruct((4, CHUNK, CHUNK), F32),
                   jax.ShapeDtypeStruct((4, CHUNK, 1), F32), vshape, vshape],
        scratch_shapes=[pltpu.VMEM((ts, BR), F32)], compiler_params=_params(("arbitrary",)))(
            dact, za, za, ng, nb, sgu_w, bias_b)


KR_BLOCK = 3584 // 128


def _mla_prep(za, qg, kvg, ck, sk, name):
    s = za.shape[0]
    ts = _tile(s, 512)

    def body(cq_ref, ckv_ref, kr_ref, qg_ref, kvg_ref, c_ref, s_ref, qn_ref, kvn_ref, krr_ref):
        qn_ref[...] = _rms(cq_ref[...], qg_ref[...]).astype(BF16)
        kvn_ref[...] = _rms(ckv_ref[...], kvg_ref[...]).astype(BF16)
        kr = kr_ref[...]
        krr_ref[...] = (kr * c_ref[...] + _rot_half(kr, 128, 0) * s_ref[...]).astype(BF16)

    vec = pl.BlockSpec((1, BR), lambda i: (0, 0))
    tab = pl.BlockSpec((ts, 128), lambda i: (i, 0))
    row = pl.BlockSpec((ts, BR), lambda i: (i, 0))
    return pl.pallas_call(
        body, name=name, grid=(s // ts,),
        in_specs=[pl.BlockSpec((ts, BR), lambda i: (i, 5)), pl.BlockSpec((ts, BR), lambda i: (i, 6)),
                  pl.BlockSpec((ts, 128), lambda i: (i, KR_BLOCK)), vec, vec, tab, tab],
        out_specs=[row, row, tab],
        out_shape=[jax.ShapeDtypeStruct((s, BR), BF16), jax.ShapeDtypeStruct((s, BR), BF16),
                   jax.ShapeDtypeStruct((s, 128), BF16)],
        compiler_params=_params(("parallel",)))(za, za, za, qg, kvg, ck, sk)


def _mla_prep_bwd(dqn, dkvn, dkr_heads, za, qg, kvg, ck, sk, name):
    s = za.shape[0]
    ts = _tile(s, 512)

    def body(dq_ref, dkv_ref, dkr_ref, cq_ref, ckv_ref, qg_ref, kvg_ref, c_ref, s_ref, dz_ref, dqg_ref, dkvg_ref):
        first = pl.program_id(0) == 0
        dcq, rows_q = _rms_bwd(cq_ref[...], qg_ref[...], dq_ref[...])
        dckv, rows_kv = _rms_bwd(ckv_ref[...], kvg_ref[...], dkv_ref[...])
        dz_ref[:, pl.ds(0, BR)] = dcq.astype(BF16)
        dz_ref[:, pl.ds(BR, BR)] = dckv.astype(BF16)
        dk = dkr_ref[:, pl.ds(0, 128)]
        for h in range(1, MLA_HEADS):
            dk = dk + dkr_ref[:, pl.ds(h * 128, 128)]
        dz_ref[:, pl.ds(2 * BR, 128)] = (dk * c_ref[...] - _rot_half(dk, 128, 0) * s_ref[...]).astype(BF16)
        _colsum_into(dqg_ref, rows_q, first)
        _colsum_into(dkvg_ref, rows_kv, first)

    vec = pl.BlockSpec((1, BR), lambda i: (0, 0))
    tab = pl.BlockSpec((ts, 128), lambda i: (i, 0))
    row = pl.BlockSpec((ts, BR), lambda i: (i, 0))
    wide = 2 * BR + 128
    vshape = jax.ShapeDtypeStruct((1, BR), F32)
    return pl.pallas_call(
        body, name=name, grid=(s // ts,),
        in_specs=[row, row, pl.BlockSpec((ts, MLA_HEADS * 128), lambda i: (i, 0)),
                  pl.BlockSpec((ts, BR), lambda i: (i, 5)), pl.BlockSpec((ts, BR), lambda i: (i, 6)), vec, vec, tab, tab],
        out_specs=[pl.BlockSpec((ts, wide), lambda i: (i, 0)), vec, vec],
        out_shape=[jax.ShapeDtypeStruct((s, wide), BF16), vshape, vshape],
        compiler_params=_params(("arbitrary",)))(dqn, dkvn, dkr_heads, za, za, qg, kvg, ck, sk)


def _attn_tiles(s):
    tq, tk = _tile(s, 1024), _tile(s, 512)
    return tq, tk, tq // tk


def _causal(qi, ki, tq, tk):
    row = qi * tq + lax.broadcasted_iota(jnp.int32, (tq, tk), 0)
    col = ki * tk + lax.broadcasted_iota(jnp.int32, (tq, tk), 1)
    return col <= row


def _step_count(t, firsts):
    n = jnp.int32(0)
    for f in firsts[1:]:
        n = n + jnp.where(t >= f, 1, 0)
    return n


def _q_major_pairs(nq, r):
    firsts = [r * qq * (qq + 1) // 2 for qq in range(nq)]

    def pair(t):
        qi = _step_count(t, firsts)
        return qi, t - r * qi * (qi + 1) // 2

    return r * nq * (nq + 1) // 2, pair


def _k_major_pairs(nq, nk, r):
    counts = [nq - kk // r for kk in range(nk)]
    firsts = [sum(counts[:kk]) for kk in range(nk)]

    def pair(t):
        ki = _step_count(t, firsts)
        first = jnp.int32(0)
        for kk in range(1, nk):
            first = first + jnp.where(t >= firsts[kk], counts[kk - 1], 0)
        return ki // r + (t - first), ki

    return sum(counts), pair


FLASH_HEADS = 2


def _flash_fwd(q, kv, krr, name, plan=None):
    s = q.shape[0]
    tq, tk, r = _attn_tiles(s)
    n_pairs, pair = _q_major_pairs(s // tq, r)
    hp = FLASH_HEADS

    def body(q_ref, kv_ref, kr_ref, o_ref, lse_ref, m_sc, l_sc, acc_sc):
        qi, ki = pair(pl.program_id(1))

        @pl.when(ki == 0)
        def _():
            m_sc[...] = jnp.full((hp, tq, 1), NEG, F32)
            l_sc[...] = jnp.zeros((hp, tq, 1), F32)
            acc_sc[...] = jnp.zeros((hp, tq, V_DIM), F32)

        def step(masked):
            for h in range(hp):
                k = jnp.concatenate([kv_ref[:, pl.ds(h * QW, QK_NOPE)], kr_ref[...]], axis=1)
                sc = lax.dot_general(q_ref[:, pl.ds(h * QW, QW)], k, (((1,), (1,)), ((), ())), preferred_element_type=F32)
                if masked:
                    sc = jnp.where(_causal(qi, ki, tq, tk), sc, NEG)
                m_prev = m_sc[h]
                m_new = jnp.maximum(m_prev, jnp.max(sc, axis=1, keepdims=True))
                alpha = jnp.exp(m_prev - m_new)
                p = jnp.exp(sc - m_new)
                l_sc[h] = alpha * l_sc[h] + jnp.sum(p, axis=1, keepdims=True)
                acc_sc[h] = alpha * acc_sc[h] + jnp.dot(p.astype(BF16), kv_ref[:, pl.ds(h * QW + QK_NOPE, V_DIM)],
                                                        preferred_element_type=F32)
                m_sc[h] = m_new

        pl.when(ki < qi * r)(functools.partial(step, False))
        pl.when(ki >= qi * r)(functools.partial(step, True))

        @pl.when(ki == (qi + 1) * r - 1)
        def _():
            for h in range(hp):
                o_ref[:, pl.ds(h * V_DIM, V_DIM)] = (acc_sc[h] / l_sc[h]).astype(BF16)
                lse_ref[:, pl.ds(h * 128, 128)] = jnp.broadcast_to(m_sc[h] + jnp.log(l_sc[h]), (tq, 128))

    out_blk = pl.BlockSpec((tq, hp * 128), lambda g, t: (pair(t)[0], g))
    return _pcall(
        body, (q, kv, krr), name=name, grid=(MLA_HEADS // hp, n_pairs),
        in_specs=[pl.BlockSpec((tq, hp * QW), lambda g, t: (pair(t)[0], g)),
                  pl.BlockSpec((tk, hp * QW), lambda g, t: (pair(t)[1], g)),
                  pl.BlockSpec((tk, 128), lambda g, t: (pair(t)[1], 0))],
        out_specs=[out_blk, out_blk],
        out_shape=[jax.ShapeDtypeStruct((s, MLA_HEADS * V_DIM), BF16), jax.ShapeDtypeStruct((s, MLA_HEADS * 128), F32)],
        scratch_shapes=[pltpu.VMEM((hp, tq, 1), F32), pltpu.VMEM((hp, tq, 1), F32), pltpu.VMEM((hp, tq, V_DIM), F32)],
        sem=("parallel", "arbitrary"), vmem=VMEM_BIG, plan=plan)


DO_BLOCK = 3 * BR // 128


def _flash_probs(q_ref, kn_ref, kr_ref, v_ref, do_ref, o_ref, lse_ref, qi, ki, tq, tk, masked):
    k = jnp.concatenate([kn_ref[...], kr_ref[...]], axis=1)
    q = q_ref[...]
    sc = lax.dot_general(q, k, (((1,), (1,)), ((), ())), preferred_element_type=F32)
    lse = jnp.max(lse_ref[...], axis=1, keepdims=True)
    p = jnp.exp(sc - lse)
    if masked:
        p = jnp.where(_causal(qi, ki, tq, tk), p, 0.0)
    do = do_ref[...]
    delta = jnp.sum(do * o_ref[...].astype(F32), axis=1, keepdims=True)
    do = do.astype(BF16)
    dp = lax.dot_general(do, v_ref[...], (((1,), (1,)), ((), ())), preferred_element_type=F32)
    ds = (p * (dp - delta)).astype(BF16)
    return q, k, p, do, ds


def _flash_bwd(q, kv, krr, dact, o, lse, cq, sq, name, plan=None):
    s = q.shape[0]
    tq, tk, r = _attn_tiles(s)
    nq, nk = s // tq, s // tk
    n_pairs, pair = _k_major_pairs(nq, nk, r)

    def body(q_ref, kn_ref, kr_ref, v_ref, do_ref, o_ref, lse_ref, c_ref, s_ref, dkv_ref, dkr_ref, dq_ref,
             dk_sc, dv_sc, dq_sc):
        qi, ki = pair(pl.program_id(1))
        rows = pl.ds(pl.multiple_of(qi * tq, tq), tq)

        @pl.when(qi == ki // r)
        def _():
            dk_sc[...] = jnp.zeros((tk, QW), F32)
            dv_sc[...] = jnp.zeros((tk, V_DIM), F32)

        @pl.when(ki == 0)
        def _():
            dq_sc[rows, :] = jnp.zeros((tq, QW), F32)

        def step(masked):
            qv, k, p, dov, ds = _flash_probs(q_ref, kn_ref, kr_ref, v_ref, do_ref, o_ref, lse_ref, qi, ki, tq, tk, masked)
            dv_sc[...] += lax.dot_general(p.astype(BF16), dov, (((0,), (0,)), ((), ())), preferred_element_type=F32)
            dk_sc[...] += lax.dot_general(ds, qv, (((0,), (0,)), ((), ())), preferred_element_type=F32)
            dq_sc[rows, :] += jnp.dot(ds, k, preferred_element_type=F32)

        pl.when(ki < qi * r)(functools.partial(step, False))
        pl.when(ki >= qi * r)(functools.partial(step, True))

        @pl.when(qi == nq - 1)
        def _():
            dkv_ref[:, pl.ds(0, 128)] = dk_sc[:, pl.ds(0, 128)].astype(BF16)
            dkv_ref[:, pl.ds(128, 128)] = dv_sc[...].astype(BF16)
            dkr_ref[...] = dk_sc[:, pl.ds(128, 128)]

        @pl.when(ki == (qi + 1) * r - 1)
        def _():
            dq = dq_sc[rows, :] * ATT_SCALE
            dq_ref[...] = (dq * c_ref[...] - _rot_half(dq, QW, QK_NOPE) * s_ref[...]).astype(BF16)

    def next_done(t):
        qi, ki = pair(t)
        return jnp.minimum(ki // r + jnp.where(jnp.logical_and(ki % r == r - 1, qi > ki // r), 1, 0), nq - 1)

    qmap = lambda c: (lambda h, t: (pair(t)[0], c(h)))
    kmap = lambda c: (lambda h, t: (pair(t)[1], c(h)))
    last = lambda c: (lambda h, t: (next_done(t), c(h)))
    return _pcall(
        body, (q, kv, krr, kv, dact, o, lse, cq, sq), name=name, grid=(MLA_HEADS, n_pairs),
        in_specs=[pl.BlockSpec((tq, QW), qmap(lambda h: h)), pl.BlockSpec((tk, 128), kmap(lambda h: 2 * h)),
                  pl.BlockSpec((tk, 128), kmap(lambda h: 0)), pl.BlockSpec((tk, 128), kmap(lambda h: 2 * h + 1)),
                  pl.BlockSpec((tq, 128), qmap(lambda h: DO_BLOCK + h)), pl.BlockSpec((tq, 128), qmap(lambda h: h)),
                  pl.BlockSpec((tq, 128), qmap(lambda h: h)), pl.BlockSpec((tq, QW), last(lambda h: 0)),
                  pl.BlockSpec((tq, QW), last(lambda h: 0))],
        out_specs=[pl.BlockSpec((tk, QW), kmap(lambda h: h)), pl.BlockSpec((tk, 128), kmap(lambda h: h)),
                   pl.BlockSpec((tq, QW), last(lambda h: h))],
        out_shape=[jax.ShapeDtypeStruct((s, MLA_HEADS * QW), BF16), jax.ShapeDtypeStruct((s, MLA_HEADS * 128), F32),
                   jax.ShapeDtypeStruct((s, MLA_HEADS * QW), BF16)],
        scratch_shapes=[pltpu.VMEM((tk, QW), F32), pltpu.VMEM((tk, V_DIM), F32), pltpu.VMEM((s, QW), F32)],
        sem=("parallel", "arbitrary"), vmem=VMEM_BIG, plan=plan)


def _rope_tables(positions):
    inv_freq = ROPE_THETA ** (-jnp.arange(0, QK_ROPE, 2, dtype=F32) / QK_ROPE)
    ang = positions.reshape(-1).astype(F32)[:, None] * inv_freq
    cos, sin = jnp.cos(ang), jnp.sin(ang)
    s = cos.shape[0]
    one, zero = jnp.ones((s, 64), F32), jnp.zeros((s, 64), F32)
    ck = jnp.concatenate([cos, cos, one], axis=1)
    sk = jnp.concatenate([sin, sin, zero], axis=1)
    cq = jnp.concatenate([one, one, ck], axis=1)
    sq = jnp.concatenate([zero, zero, sk], axis=1)
    return ck, sk, cq, sq


def _cols_full(gathered):
    _, r, c = gathered.shape
    return gathered.transpose(1, 0, 2).reshape(r, N_DEV * c)


def _cols_by_owner(full):
    r, n = full.shape
    return full.reshape(r, N_DEV, n // N_DEV).transpose(1, 0, 2)


def _layer_weights(gat, small, l):
    shards = gat[("w_in", l)]
    cut, rem = divmod(N_IN_A, shards.shape[2])
    w = {
        "gat": gat, "layer": l,
        "w_a": jnp.concatenate([shards[d] for d in range(cut)] + [shards[cut][:, :rem],
                                                                 jnp.zeros((D_MODEL, ZA - N_IN_A), BF16)], axis=1),
        "w_g": jnp.concatenate([shards[cut][:, rem:]] + [shards[d] for d in range(cut + 1, N_DEV)], axis=1),
        "conv_w": small["conv_w"][l],
        "pool_w": small["pool_w"][l], "sgu_w": small["sgu_w"][l],
        "sgu_bias": jnp.broadcast_to(small["sgu_b"][l][:, :, None], (4, CHUNK, GDIM)),
    }
    for name in ("pre_mix_g", "pool_scale", "conv_b", "conv_norm_g", "conv_norm_b", "sgu_norm_g", "sgu_norm_b",
                 "q_norm_g", "kv_norm_g", "post_mix_g", "pre_mlp_g", "post_mlp_g"):
        w[name] = small[name][l][None, :]
    return w


def _late(w, name):
    if name not in w:
        gat, l = w["gat"], w["layer"]
        if name == "proj_cat":
            w[name] = jnp.concatenate([_cols_full(gat[(n, l)]) for n in ("pool_proj", "conv_proj", "sgu_proj", "attn_proj")],
                                      axis=0)
        elif name == "w_uq":
            w[name] = jnp.pad(gat[("w_uq", l)].transpose(1, 0, 2),
                              ((0, 0), (0, 0), (0, QW - QK_NOPE - QK_ROPE))).reshape(BR, MLA_HEADS * QW)
        elif name == "w_ukv":
            w[name] = _cols_full(gat[("w_ukv", l)])
        elif name == "w_up":
            w[name] = gat[("w_up", l)]
        else:
            arr = gat[(name, l)]
            w[name] = arr.reshape(N_DEV * arr.shape[1], arr.shape[2])
    return w[name]


def _layer_fwd(x, h1, w, tabs, l, plan):
    ck, sk, cq, sq = tabs
    n = f"l{l}_"
    za = _mm(h1, w["w_a"], mode="nn", name=n + "za", plan=plan)
    zg = _mm(h1, w["w_g"], mode="nn", name=n + "zg", plan=plan)
    a_pool = _pool_fwd(za, w["pool_w"], w["pool_scale"], n + "pool")
    yc, a_conv = _conv_fwd(za, w["conv_w"], w["conv_b"], w["conv_norm_g"], w["conv_norm_b"], n + "conv")
    a_sgu = _sgu_fwd(za, w["sgu_norm_g"], w["sgu_norm_b"], w["sgu_w"], w["sgu_bias"], n + "sgu")
    qn, kvn, krr = _mla_prep(za, w["q_norm_g"], w["kv_norm_g"], ck, sk, n + "mla_prep")
    q = _mm(qn, _late(w, "w_uq"), mode="nn", name=n + "q", out_dtypes=(BF16,), extras=(cq, sq), tn=QW,
            epilogue=lambda acc, c, sn: ((acc * c + _rot_half(acc, QW, QK_NOPE) * sn) * ATT_SCALE,))
    kv = _mm(kvn, _late(w, "w_ukv"), mode="nn", name=n + "kv", out_dtypes=(BF16,))
    o, lse = _flash_fwd(q, kv, krr, n + "flash", plan=plan)
    act_cat = jnp.concatenate([a_pool, a_conv, a_sgu, o], axis=1)
    y, merged = _proj_merge_fwd(act_cat, _late(w, "proj_cat"), zg, n + "proj", plan=plan)
    m2 = _mm(merged, _late(w, "w_out"), mode="nn", name=n + "out")
    x1, h2 = _post_pre(x, m2, w["post_mix_g"], w["pre_mlp_g"], n + "post_mix")
    up, act = _mm(h2, _late(w, "w_up"), mode="nn", name=n + "up", out_dtypes=(BF16, BF16), plan=plan,
                  mnk=(x.shape[0], D_FF, D_MODEL), tn=UP_SHARD,
                  b_spec=lambda tn, tk: pl.BlockSpec((None, tk, tn), lambda i, j, kk: (j, kk, 0)),
                  epilogue=lambda acc: (acc, jnp.square(jnp.maximum(acc, 0.0))))
    f = _mm(act, _late(w, "w_down"), mode="nn", name=n + "down", plan=plan)
    saved = dict(x=x, h1=h1, za=za, zg=zg, yc=yc, qn=qn, kvn=kvn, krr=krr, q=q, kv=kv, o=o, lse=lse, act_cat=act_cat,
                 y=y, merged=merged, m2=m2, x1=x1, h2=h2, up=up, act=act, f=f)
    return x1, f, saved


def _layer_bwd(dx_out, df, sv, w, tabs, l, prev, plan, rs):
    ck, sk, cq, sq = tabs
    n = f"l{l}_b_"
    g = {}
    own, half = {}, {}

    def to_sibling(kernel, names):
        def done(results):
            for name, t in zip(names, results[0]):
                half[name] = _rs_chip_sum(own[name], t, f"l{l}_chip_sum_{name}")
        plan.at(n + kernel, lambda: [_scatter_sibling([own[name] for name in names])], done)

    def to_chips(kernel, names, prefix=n):
        def done(results):
            for name, t in zip(names, results[0]):
                rs[(name, l)] = (half[name], t)
        plan.at(prefix + kernel, lambda: [_scatter_chips([half[name] for name in names])], done)

    own["w_down"] = _mm(sv["act"], df, mode="tn", name=n + "dw_down", out_dtypes=(BF16,), plan=plan).reshape(
        N_DEV, D_FF // N_DEV, D_MODEL)
    to_sibling("dup", ["w_down"])
    dup = _mm(df, _late(w, "w_down"), mode="nt", name=n + "dup", out_dtypes=(BF16,), extras=(sv["up"],), plan=plan,
              epilogue=lambda acc, up: (acc * 2.0 * jnp.maximum(up.astype(F32), 0.0),))
    to_chips("dw_up", ["w_down"])
    own["w_up"] = _mm(sv["h2"], dup, mode="tn", name=n + "dw_up", tn=UP_SHARD, plan=plan,
                      o_spec=lambda tm, tn: pl.BlockSpec((None, tm, tn), lambda i, j, kk: (j, i, 0)),
                      out_struct=jax.ShapeDtypeStruct((N_DEV, D_MODEL, UP_SHARD), BF16))
    to_sibling("dh2", ["w_up"])
    dh2 = _mm(dup, _late(w, "w_up"), mode="nt", name=n + "dh2", mnk=(dup.shape[0], D_MODEL, D_FF), tk=UP_SHARD,
              plan=plan, b_spec=lambda tn, tk: pl.BlockSpec((None, tn, tk), lambda i, j, kk: (kk, j, 0)))
    dx1, dm2, g["pre_mlp_g"], g["post_mix_g"] = _pre_bwd(dh2, sv["x1"], w["pre_mlp_g"], dx_out, n + "pre_mlp",
                                                           r_prev=sv["m2"], g_post_prev=w["post_mix_g"])
    own["w_out"] = _mm(sv["merged"], dm2, mode="tn", name=n + "dw_out", out_dtypes=(BF16,)).reshape(
        N_DEV, D_MODEL // N_DEV, D_MODEL)
    to_sibling("dmerged", ["w_out"])
    dmerged = _mm(dm2, _late(w, "w_out"), mode="nt", name=n + "dmerged", plan=plan)
    to_chips("merge", ["w_up"])
    dy, dzg = _merge_bwd(dmerged, sv["y"], sv["zg"], n + "merge", plan=plan)
    d_proj = _proj_bwd_w(sv["act_cat"], dy, n + "dw_proj")
    projs = ["pool_proj", "conv_proj", "sgu_proj", "attn_proj"]
    for i, name in enumerate(projs):
        own[name] = _cols_by_owner(d_proj[i * BR:(i + 1) * BR] if i < 3 else d_proj[3 * BR:])
    to_chips("dact", ["w_out"])
    to_sibling("dact", projs)
    dact = _proj_bwd_act(dy, _late(w, "proj_cat"), n + "dact", plan=plan)
    dz_pool, g["pool_w"], g["pool_scale"] = _pool_bwd(dact, sv["za"], w["pool_w"], w["pool_scale"], n + "pool")
    dyc, g["conv_b"], g["conv_norm_g"], g["conv_norm_b"] = _conv_bwd_norm(dact, sv["yc"], w["conv_norm_g"],
                                                                          w["conv_norm_b"], n + "conv_norm")
    dz_conv, g["conv_w"] = _conv_bwd_taps(dyc, sv["za"], w["conv_w"], n + "conv_taps")
    dz_sgu, g["sgu_w"], g["sgu_b"], g["sgu_norm_g"], g["sgu_norm_b"] = _sgu_bwd(
        dact, sv["za"], w["sgu_norm_g"], w["sgu_norm_b"], w["sgu_w"], w["sgu_bias"], n + "sgu")
    to_chips("flash", projs)
    dkv, dkr_heads, dq = _flash_bwd(sv["q"], sv["kv"], sv["krr"], dact, sv["o"], sv["lse"], cq, sq, n + "flash",
                                    plan=plan)
    d_uq = _mm(sv["qn"], dq, mode="tn", name=n + "dw_uq", out_dtypes=(BF16,))
    own["w_uq"] = d_uq.reshape(BR, MLA_HEADS, QW)[:, :, :QK_NOPE + QK_ROPE].transpose(1, 0, 2)
    dqn = _mm(dq, _late(w, "w_uq"), mode="nt", name=n + "dqn")
    own["w_ukv"] = _cols_by_owner(_mm(sv["kvn"], dkv, mode="tn", name=n + "dw_ukv", out_dtypes=(BF16,)))
    to_sibling("dkvn", ["w_uq", "w_ukv"])
    dkvn = _mm(dkv, _late(w, "w_ukv"), mode="nt", name=n + "dkvn", plan=plan)
    dz_mla, g["q_norm_g"], g["kv_norm_g"] = _mla_prep_bwd(dqn, dkvn, dkr_heads, sv["za"], w["q_norm_g"],
                                                           w["kv_norm_g"], ck, sk, n + "mla_prep")
    dza = jnp.concatenate([dz_pool, dz_conv, dz_sgu, dz_mla], axis=1)
    to_chips("dw_a", ["w_uq", "w_ukv"])
    d_a = _mm(sv["h1"], dza, mode="tn", name=n + "dw_a", out_dtypes=(BF16,), plan=plan)
    d_g = _mm(sv["h1"], dzg, mode="tn", name=n + "dw_g", out_dtypes=(BF16,))
    own["w_in"] = _cols_by_owner(jnp.concatenate([d_a[:, :N_IN_A], d_g], axis=1)).transpose(0, 2, 1)
    to_sibling("dh1_a", ["w_in"])
    dh1 = _mm(dza, w["w_a"], mode="nt", name=n + "dh1_a", plan=plan)
    if prev is None:
        to_chips("dh1_g", ["w_in"])
    else:
        to_chips("flash", ["w_in"], prefix=f"l{l - 1}_b_")
    dh1 = _mm(dzg, w["w_g"], mode="nt", name=n + "dh1_g", extras=(dh1,), epilogue=lambda acc, e: (acc + e,), plan=plan)
    if prev is None:
        dx, g["pre_mix_g"] = _pre_bwd(dh1, sv["x"], w["pre_mix_g"], dx1, n + "pre_mix")
        return dx, None, g
    dx, df_prev, g["pre_mix_g"], g_prev_post = _pre_bwd(dh1, sv["x"], w["pre_mix_g"], dx1, n + "pre_mix",
                                                         r_prev=prev[0], g_post_prev=prev[1])
    g["prev_post_mlp_g"] = g_prev_post
    return dx, df_prev, g


MIDSIZE = ("pool_proj", "conv_proj", "sgu_proj", "w_uq", "w_ukv", "attn_proj")
GATHER_STEPS = (
    ("l0_za", tuple((name, 0) for name in MIDSIZE)),
    ("l0_zg", (("w_out", 0),) + tuple((name, 1) for name in MIDSIZE)),
    ("l0_flash", (("w_in", 1), ("w_up", 0))),
    ("l0_proj", (("w_down", 0),)),
    ("l0_up", (("w_up", 1),)),
    ("l0_down", (("w_out", 1),)),
    ("l1_za", ()),
    ("l1_zg", ()),
    ("l1_flash", (("w_down", 1),)),
    ("l1_proj", ()),
)


def _plan_gathers(plan, gat, shards):
    first_half = {}
    for step, (kernel, keys) in enumerate(GATHER_STEPS):
        before = GATHER_STEPS[step - 1][1] if step else ()
        if not keys and not before:
            continue

        def make(keys=keys, before=before):
            ops = [_gather_pass(first_half[before])] if before else []
            return ops + ([_gather_own([shards[key] for key in keys])] if keys else [])

        def done(results, keys=keys, before=before):
            if before:
                gat.update(zip(before, results[0]))
            if keys:
                first_half[keys] = results[-1]

        plan.at(kernel, make, done)


def _local_step(x, positions, target, gat, small, shards, plan):
    tabs = _rope_tables(positions)
    _plan_gathers(plan, gat, shards)
    ws = [_layer_weights(gat, small, 0)]
    saved = []
    h = _pre_norm(x, ws[0]["pre_mix_g"], "l0_pre_mix")
    cur = x
    for l in range(DEPTH):
        x1, f, sv = _layer_fwd(cur, h, ws[l], tabs, l, plan)
        saved.append(sv)
        if l + 1 < DEPTH:
            ws.append(_layer_weights(gat, small, l + 1))
            cur, h = _post_pre(x1, f, ws[l]["post_mlp_g"], ws[l + 1]["pre_mix_g"], f"l{l}_post_mlp")
    top = DEPTH - 1
    dx, df, dg_post, loss = _final_loss(saved[top]["x1"], saved[top]["f"], ws[top]["post_mlp_g"], target, "loss")
    grads = [None] * DEPTH
    post_mlp = {top: dg_post}
    rs = {}
    for l in range(top, -1, -1):
        prev = (saved[l - 1]["f"], ws[l - 1]["post_mlp_g"]) if l > 0 else None
        dx, df, g = _layer_bwd(dx, df, saved[l], ws[l], tabs, l, prev, plan, rs)
        if l > 0:
            post_mlp[l - 1] = g.pop("prev_post_mlp_g")
        grads[l] = g
    for l in range(DEPTH):
        grads[l]["post_mlp_g"] = post_mlp[l]
    assert not plan.jobs, sorted(plan.jobs)
    return loss[0, 0], dx, grads, rs


def _small_grads(grads):
    small = {}

    def stack(fn):
        return jnp.stack([fn(grads[l]) for l in range(DEPTH)])

    for name, shape in SMALL:
        if name == "sgu_b":
            small[name] = stack(lambda g: g["sgu_b"][:, :, 0])
        else:
            small[name] = stack(lambda g, name=name, shape=shape: g[name].reshape(shape))
    small["conv_w"] = stack(lambda g: g["conv_w"][:CONV_WIDTH])
    return small


SMALL_ROWS = sum(DEPTH * math.prod(shape) // 128 for _, shape in SMALL)
CONVW_ROWS = DEPTH * CONV_WIDTH * BR // 128


def _pack_small(parts):
    return jnp.concatenate([parts[name].astype(F32).reshape(-1, 128) for name, _ in SMALL], axis=0)


def _unpack_small(buf):
    out, off = {}, 0
    for name, shape in SMALL:
        rows = DEPTH * math.prod(shape) // 128
        out[name] = buf[off:off + rows].reshape((DEPTH,) + shape)
        off += rows
    return out


def _mesh_pos():
    return lax.axis_index("x"), lax.axis_index("y"), lax.axis_index("c")


def _all_gather(shards, name):
    n = len(shards)

    def body(*refs):
        x_refs, out_refs = refs[:n], refs[n:2 * n]
        send_sems, recv_sems, local_sems = refs[2 * n:]
        x, y, c = _mesh_pos()
        me, sibling = (x, y, c), (x, y, 1 - c)
        chips = [(1 - x, y), (x, 1 - y), (1 - x, 1 - y)]

        def rows(t, px, py, pc):
            return out_refs[t].at[4 * px + 2 * py + pc]

        def copy(t, k, block, to, src=None):
            return pltpu.make_async_remote_copy(
                src_ref=rows(t, *block) if src is None else src, dst_ref=rows(t, *block), send_sem=send_sems.at[t, k],
                recv_sem=recv_sems.at[t, k], device_id=to, device_id_type=MESH)

        mine = [pltpu.make_async_copy(x_refs[t], rows(t, *me), local_sems.at[t]) for t in range(n)]
        for cp in mine:
            cp.start()
        first = []
        for t in range(n):
            first.append(copy(t, 0, me, sibling, src=x_refs[t]))
            first += [copy(t, 1 + j, me, (*chip, c), src=x_refs[t]) for j, chip in enumerate(chips)]
        for cp in first:
            cp.start()
        passed = []
        for t in range(n):
            for j, chip in enumerate(chips):
                copy(t, 1 + j, (*chip, c), me).wait_recv()
                passed.append(copy(t, 4 + j, (*chip, c), sibling))
                passed[-1].start()
        for t in range(n):
            copy(t, 0, sibling, me).wait_recv()
            for j, chip in enumerate(chips):
                copy(t, 4 + j, (*chip, 1 - c), me).wait_recv()
        for cp in first + passed:
            cp.wait_send()
        for cp in mine:
            cp.wait()

    hbm = pl.BlockSpec(memory_space=pl.ANY)
    return pl.pallas_call(
        body, name=name, out_shape=[jax.ShapeDtypeStruct((N_DEV,) + a.shape, a.dtype) for a in shards],
        in_specs=[hbm] * n, out_specs=[hbm] * n,
        scratch_shapes=[pltpu.SemaphoreType.DMA((n, 7)), pltpu.SemaphoreType.DMA((n, 7)),
                        pltpu.SemaphoreType.DMA((n,))])(*shards)


def _shard_tile(r, c_, cap_bytes=3 << 19):
    best = 0
    for t in range(16, r + 1, 16):
        if r % t == 0 and t * c_ * 4 <= cap_bytes:
            best = t
    if best or r * c_ * 4 <= cap_bytes:
        return (best or r), c_
    tc = max(t for t in range(128, c_ + 1, 128) if c_ % t == 0 and r * t * 4 <= cap_bytes)
    return r, tc


def _rs_chip_sum(g, t, name):
    _, r, c_ = g.shape
    tr, tc = _shard_tile(r, c_)
    core = lax.axis_index("c").astype(jnp.int32).reshape(1)

    def body(core_ref, g_ref, t_ref, p_ref):
        p_ref[...] = (g_ref[...].astype(F32) + t_ref[...].astype(F32)).astype(p_ref.dtype)

    return pl.pallas_call(
        body, name=name, out_shape=jax.ShapeDtypeStruct((4, r, c_), g.dtype),
        grid_spec=pltpu.PrefetchScalarGridSpec(
            num_scalar_prefetch=1, grid=(4, r // tr, c_ // tc),
            in_specs=[pl.BlockSpec((1, tr, tc), lambda k, i, j, core_ref: (2 * k + core_ref[0], i, j)),
                      pl.BlockSpec((1, tr, tc), lambda k, i, j, core_ref: (k, i, j))],
            out_specs=pl.BlockSpec((1, tr, tc), lambda k, i, j, core_ref: (k, i, j))),
        compiler_params=_params(("parallel", "parallel", "parallel")))(core, g, t)


def _adamw_math(w, g, m, v):
    m = ADAM_B1 * m + (1.0 - ADAM_B1) * g
    v = ADAM_B2 * v + (1.0 - ADAM_B2) * jnp.square(g)
    m_hat = m / (1.0 - ADAM_B1 ** ADAM_STEP)
    v_hat = v / (1.0 - ADAM_B2 ** ADAM_STEP)
    delta = -ADAM_LR * (m_hat / (jnp.sqrt(v_hat) + ADAM_EPS) + ADAM_WD * w)
    return delta, m, v


def _adamw_big(ps, ts, w, m, v, name):
    _, r, c_ = w.shape
    tr, tc = _shard_tile(r, c_)
    chip = (2 * lax.axis_index("x") + lax.axis_index("y")).astype(jnp.int32).reshape(1)

    def body(chip_ref, p0, t0, p1, t1, w_ref, m_ref, v_ref, g_out, d_out, m_out, v_out):
        def update(p_ref, t_ref):
            g = p_ref[0].astype(F32) + t_ref[0].astype(F32) + t_ref[1].astype(F32) + t_ref[2].astype(F32)
            g_out[0] = g
            d_out[0], m_out[0], v_out[0] = _adamw_math(w_ref[0], g, m_ref[0], v_ref[0])

        pl.when(pl.program_id(0) == 0)(functools.partial(update, p0, t0))
        pl.when(pl.program_id(0) == 1)(functools.partial(update, p1, t1))

    def grad_specs(layer):
        def at(l, i, j):
            return jnp.where(l == layer, i, 0), jnp.where(l == layer, j, 0)
        return [pl.BlockSpec((1, tr, tc), lambda l, i, j, chip_ref: (chip_ref[0], *at(l, i, j))),
                pl.BlockSpec((3, tr, tc), lambda l, i, j, chip_ref: (0, *at(l, i, j)))]

    nat = pl.BlockSpec((1, tr, tc), lambda l, i, j, chip_ref: (l, i, j))
    shape = jax.ShapeDtypeStruct(w.shape, F32)
    return pl.pallas_call(
        body, name=name, out_shape=[shape] * 4,
        grid_spec=pltpu.PrefetchScalarGridSpec(
            num_scalar_prefetch=1, grid=(DEPTH, r // tr, c_ // tc),
            in_specs=grad_specs(0) + grad_specs(1) + [nat, nat, nat], out_specs=[nat] * 4),
        compiler_params=_params(("parallel", "parallel", "parallel"), VMEM_BIG))(
            chip, ps[0], ts[0], ps[1], ts[1], w, m, v)


def _sum_devices(parts, name):
    _, r, c_ = parts.shape

    def body(p_ref, o_ref):
        acc = p_ref[0]
        for d in range(1, N_DEV):
            acc = acc + p_ref[d]
        o_ref[...] = acc

    return pl.pallas_call(body, name=name, out_shape=jax.ShapeDtypeStruct((r, c_), F32),
                          compiler_params=_params(None, VMEM_BIG))(parts)


def _adamw_small(w, g, m, v, name):
    def body(w_ref, g_ref, m_ref, v_ref, d_out, m_out, v_out):
        d_out[...], m_out[...], v_out[...] = _adamw_math(w_ref[...], g_ref[...], m_ref[...], v_ref[...])

    shape = jax.ShapeDtypeStruct(w.shape, F32)
    return pl.pallas_call(body, name=name, out_shape=[shape] * 3)(w, g, m, v)


def kernel(x, positions, pre_mix_g, w_in, pool_w, pool_scale, pool_proj, conv_w, conv_b, conv_norm_g, conv_norm_b, conv_proj, sgu_norm_g, sgu_norm_b, sgu_w, sgu_b, sgu_proj, q_norm_g, w_uq, kv_norm_g, w_ukv, attn_proj, w_out, post_mix_g, pre_mlp_g, w_up, w_down, post_mlp_g, loss_target, m_pre_mix_g, m_w_in, m_pool_w, m_pool_scale, m_pool_proj, m_conv_w, m_conv_b, m_conv_norm_g, m_conv_norm_b, m_conv_proj, m_sgu_norm_g, m_sgu_norm_b, m_sgu_w, m_sgu_b, m_sgu_proj, m_q_norm_g, m_w_uq, m_kv_norm_g, m_w_ukv, m_attn_proj, m_w_out, m_post_mix_g, m_pre_mlp_g, m_w_up, m_w_down, m_post_mlp_g, v_pre_mix_g, v_w_in, v_pool_w, v_pool_scale, v_pool_proj, v_conv_w, v_conv_b, v_conv_norm_g, v_conv_norm_b, v_conv_proj, v_sgu_norm_g, v_sgu_norm_b, v_sgu_w, v_sgu_b, v_sgu_proj, v_q_norm_g, v_w_uq, v_kv_norm_g, v_w_ukv, v_attn_proj, v_w_out, v_post_mix_g, v_pre_mlp_g, v_w_up, v_w_down, v_post_mlp_g):
    args = dict(locals())
    wts = {n: args[n] for n in WEIGHTS}
    mom1 = {n: args["m_" + n] for n in WEIGHTS}
    mom2 = {n: args["v_" + n] for n in WEIGHTS}
    dev = 4 * lax.axis_index("x") + 2 * lax.axis_index("y") + lax.axis_index("c")

    shards = {(name, l): wts[name][l].astype(BF16) for name, _, _ in BIG for l in range(DEPTH)}
    taps = jnp.pad(conv_w.reshape(-1, 128), ((0, 1), (0, 0)))
    gathered = _all_gather([shards[("w_in", 0)], taps], "gather_first")
    gat = {("w_in", 0): gathered[0]}
    taps = gathered[-1][:, :CONV_WIDTH].reshape(N_DEV, DEPTH, CONV_WIDTH, BR // N_DEV)
    small = {n: wts[n] for n, _ in SMALL}
    small["conv_w"] = taps.transpose(1, 2, 0, 3).reshape(DEPTH, CONV_WIDTH, BR)

    loss_part, grad_x, grads, rs = _local_step(x[0], positions, loss_target[0], gat, small, shards, _Plan())
    small_g = _small_grads(grads)
    loss = lax.psum(loss_part, ("x", "y", "c"))

    out = {"grad": {}, "delta": {}, "new_m": {}, "new_v": {}}
    for name, _, _ in BIG:
        flip = (lambda a: a.swapaxes(1, 2)) if name == "w_in" else (lambda a: a)
        res = _adamw_big([rs[(name, l)][0] for l in range(DEPTH)], [rs[(name, l)][1] for l in range(DEPTH)],
                         flip(wts[name]), flip(mom1[name]), flip(mom2[name]), "adamw_" + name)
        res = [flip(buf) for buf in res]
        for key, buf in zip(("grad", "delta", "new_m", "new_v"), res):
            out[key][name] = buf

    part = jnp.concatenate([_pack_small(small_g), small_g["conv_w"].reshape(-1, 128)], axis=0)
    total = _sum_devices(_all_gather([part], "gather_small_grads")[0], "sum_small_grads")
    g_small = total[:SMALL_ROWS]
    d_small, m_small, v_small = _adamw_small(_pack_small(wts), g_small, _pack_small(mom1), _pack_small(mom2), "adamw_small")
    for key, buf in (("grad", g_small), ("delta", d_small), ("new_m", m_small), ("new_v", v_small)):
        out[key].update(_unpack_small(buf))
    g_taps = total[SMALL_ROWS:].reshape(DEPTH, CONV_WIDTH, N_DEV, BR // N_DEV)
    g_taps = lax.dynamic_index_in_dim(g_taps, dev, axis=2, keepdims=False)
    flat = lambda a: a.reshape(-1, 128)
    d_taps, m_taps, v_taps = _adamw_small(flat(conv_w), flat(g_taps), flat(m_conv_w), flat(v_conv_w), "adamw_taps")
    for key, buf in (("grad", g_taps), ("delta", d_taps), ("new_m", m_taps), ("new_v", v_taps)):
        out[key]["conv_w"] = buf.reshape(conv_w.shape)

    return (loss, grad_x[None], *[out["grad"][n] for n in WEIGHTS], *[out["delta"][n] for n in WEIGHTS],
            *[out["new_m"][n] for n in WEIGHTS], *[out["new_v"][n] for n in WEIGHTS])
```

```python
import collections
import functools
import math

import jax
import jax.numpy as jnp
from jax import lax
from jax.experimental import pallas as pl
from jax.experimental.pallas import tpu as pltpu

F32 = jnp.float32
BF16 = jnp.bfloat16

D_MODEL = 2048
DEPTH = 2
EPS = 1e-6
N_BRANCH = 4
D_FF = 4 * D_MODEL
UP_SHARD = D_FF // 8
POOL_WINDOWS = (2, 4, 8, 16)
CONV_WIDTH = 31
CHUNK = 128
MLA_HEADS = 8
QK_NOPE = 128
QK_ROPE = 64
V_DIM = 128
ROPE_THETA = 10000.0
GDIM = 128
BR = 512
N_IN_A = 3648
ZA = 3712
N_GATE = N_BRANCH * D_MODEL
N_IN = N_IN_A + N_GATE
QW = 256
ACT_CAT = 3 * BR + MLA_HEADS * V_DIM
ATT_SCALE = (QK_NOPE + QK_ROPE) ** -0.5
NEG = -1e30

ADAM_LR = 0.001
ADAM_B1 = 0.9
ADAM_B2 = 0.999
ADAM_EPS = 1e-08
ADAM_WD = 0.01
ADAM_STEP = 10

N_DEV = 8
PACK_W = 1024
VMEM_BIG = 48 * 1024 * 1024
MESH = pl.DeviceIdType.MESH

BIG = (
    ("w_in", 1, (2048, 1480)),
    ("pool_proj", 1, (512, 256)),
    ("conv_proj", 1, (512, 256)),
    ("sgu_proj", 1, (512, 256)),
    ("w_uq", 1, (512, 192)),
    ("w_ukv", 1, (512, 256)),
    ("attn_proj", 1, (1024, 256)),
    ("w_out", 0, (256, 2048)),
    ("w_up", 1, (2048, 1024)),
    ("w_down", 0, (1024, 2048)),
)
SMALL = (
    ("pre_mix_g", (2048,)), ("pool_w", (4, 128, 128)), ("pool_scale", (512,)), ("conv_b", (512,)),
    ("conv_norm_g", (512,)), ("conv_norm_b", (512,)), ("sgu_norm_g", (512,)), ("sgu_norm_b", (512,)),
    ("sgu_w", (4, 128, 128)), ("sgu_b", (4, 128)), ("q_norm_g", (512,)), ("kv_norm_g", (512,)),
    ("post_mix_g", (2048,)), ("pre_mlp_g", (2048,)), ("post_mlp_g", (2048,)),
)
WEIGHTS = ("pre_mix_g", "w_in", "pool_w", "pool_scale", "pool_proj", "conv_w", "conv_b", "conv_norm_g", "conv_norm_b",
           "conv_proj", "sgu_norm_g", "sgu_norm_b", "sgu_w", "sgu_b", "sgu_proj", "q_norm_g", "w_uq", "kv_norm_g",
           "w_ukv", "attn_proj", "w_out", "post_mix_g", "pre_mlp_g", "w_up", "w_down", "post_mlp_g")


def _params(sem=None, vmem=None):
    return pltpu.CompilerParams(dimension_semantics=sem, vmem_limit_bytes=vmem)


def _tile(dim, pref):
    if dim <= pref:
        return dim
    best = 0
    for t in range(128, pref + 1, 128):
        if dim % t == 0:
            best = t
    return best if best >= 256 else dim


def _sigmoid(x):
    return 1.0 / (1.0 + jnp.exp(-x))


def _gelu(x):
    k = math.sqrt(2.0 / math.pi)
    return 0.5 * x * (1.0 + jnp.tanh(k * (x + 0.044715 * x * x * x)))


def _gelu_grad(x):
    k = math.sqrt(2.0 / math.pi)
    t = jnp.tanh(k * (x + 0.044715 * x * x * x))
    return 0.5 * (1.0 + t) + 0.5 * x * (1.0 - t * t) * k * (1.0 + 3.0 * 0.044715 * x * x)


def _rms(x, g):
    r = lax.rsqrt(jnp.mean(x * x, axis=-1, keepdims=True) + EPS)
    return x * r * g


def _rms_bwd(x, g, dy):
    r = lax.rsqrt(jnp.mean(x * x, axis=-1, keepdims=True) + EPS)
    dyg = dy * g
    dx = r * dyg - x * (r * r * r) * jnp.mean(dyg * x, axis=-1, keepdims=True)
    return dx, dy * x * r


def _ln_stats(x):
    mu = jnp.mean(x, axis=-1, keepdims=True)
    xc = x - mu
    r = lax.rsqrt(jnp.mean(xc * xc, axis=-1, keepdims=True) + EPS)
    return xc * r, r


def _ln_bwd(xh, r, g, dy):
    dxh = dy * g
    return r * (dxh - jnp.mean(dxh, axis=-1, keepdims=True) - xh * jnp.mean(dxh * xh, axis=-1, keepdims=True))


def _rot_half(x, width, off):
    n = x.shape[-1]
    lane = lax.broadcasted_iota(jnp.int32, x.shape, x.ndim - 1) % width
    return jnp.where(lane - off < QK_ROPE // 2, -pltpu.roll(x, n - QK_ROPE // 2, x.ndim - 1),
                     pltpu.roll(x, QK_ROPE // 2, x.ndim - 1))


def _colsum_into(ref, val, first):
    s = jnp.sum(val, axis=0, keepdims=True)

    @pl.when(first)
    def _():
        ref[...] = s

    @pl.when(jnp.logical_not(first))
    def _():
        ref[...] += s


Exchange = collections.namedtuple("Exchange", "inputs out_shapes aliases n_pairs n_local build")


class _Plan:
    def __init__(self):
        self.jobs = {}

    def at(self, kernel, make, done):
        self.jobs.setdefault(kernel, []).append((make, done))

    def take(self, kernel):
        return self.jobs.pop(kernel, [])


def _pcall(body, operands, *, name, grid, in_specs, out_specs, out_shape, scratch_shapes=(), sem=None, vmem=None,
           plan=None):
    jobs = plan.take(name) if plan is not None else []
    if not jobs:
        return pl.pallas_call(body, name=name, grid=grid, in_specs=list(in_specs), out_specs=list(out_specs),
                              out_shape=list(out_shape), scratch_shapes=list(scratch_shapes),
                              compiler_params=_params(sem, vmem))(*operands)
    made = [(make(), done) for make, done in jobs]
    comm = [op for ops, _ in made for op in ops]
    n_in, n_out, n_scr = len(in_specs), len(out_shape), len(scratch_shapes)
    c_in = [a for op in comm for a in op.inputs]
    c_out = [s for op in comm for s in op.out_shapes]
    sems, aliases, i_off, o_off = [], {}, n_in, n_out
    for op in comm:
        sems += [pltpu.SemaphoreType.DMA((op.n_pairs,)), pltpu.SemaphoreType.DMA((op.n_pairs,)),
                 pltpu.SemaphoreType.DMA((max(op.n_local, 1),))]
        for src, dst in op.aliases.items():
            aliases[i_off + src] = o_off + dst
        i_off += len(op.inputs)
        o_off += len(op.out_shapes)

    def carrier(*refs):
        ins, cins = refs[:n_in], refs[n_in:n_in + len(c_in)]
        base = n_in + len(c_in)
        outs, couts = refs[base:base + n_out], refs[base + n_out:base + n_out + len(c_out)]
        base += n_out + len(c_out)
        scr, csems = refs[base:base + n_scr], refs[base + n_scr:]
        ids = [pl.program_id(ax) for ax in range(len(grid))]
        first = functools.reduce(jnp.logical_and, [i == 0 for i in ids])
        last = functools.reduce(jnp.logical_and, [i == g - 1 for i, g in zip(ids, grid)])

        def pieces():
            res, ci, co = [], 0, 0
            for k, op in enumerate(comm):
                res.append(op.build(cins[ci:ci + len(op.inputs)], couts[co:co + len(op.out_shapes)],
                                    *csems[3 * k:3 * k + 3]))
                ci += len(op.inputs)
                co += len(op.out_shapes)
            return res

        @pl.when(first)
        def _():
            for sends, _, local in pieces():
                for cp in local + sends:
                    cp.start()

        body(*ins, *outs, *scr)

        @pl.when(last)
        def _():
            for sends, recvs, local in pieces():
                for cp in recvs:
                    cp.wait_recv()
                for cp in sends:
                    cp.wait_send()
                for cp in local:
                    cp.wait()

    hbm = pl.BlockSpec(memory_space=pl.ANY)
    res = pl.pallas_call(
        carrier, name=name, grid=grid, in_specs=list(in_specs) + [hbm] * len(c_in),
        out_specs=list(out_specs) + [hbm] * len(c_out), out_shape=list(out_shape) + c_out,
        scratch_shapes=list(scratch_shapes) + sems, input_output_aliases=aliases,
        compiler_params=_params(("arbitrary",) * len(grid), vmem))(*operands, *c_in)
    pos = n_out
    for ops, done in made:
        results = []
        for op in ops:
            results.append(list(res[pos:pos + len(op.out_shapes)]))
            pos += len(op.out_shapes)
        done(results)
    return list(res[:n_out])


def _block(ref, px, py, pc):
    return ref.at[4 * px + 2 * py + pc]


def _remote(src, dst, send_sems, recv_sems, k, to):
    return pltpu.make_async_remote_copy(src_ref=src, dst_ref=dst, send_sem=send_sems.at[k], recv_sem=recv_sems.at[k],
                                        device_id=to, device_id_type=MESH)


def _gather_own(shards):
    n = len(shards)

    def build(ins, outs, send_sems, recv_sems, local_sems):
        x, y, c = _mesh_pos()
        peers = [(x, y, 1 - c), (1 - x, y, c), (x, 1 - y, c), (1 - x, 1 - y, c)]
        sends, recvs, local = [], [], []
        for t in range(n):
            local.append(pltpu.make_async_copy(ins[t], _block(outs[t], x, y, c), local_sems.at[t]))
            for k, peer in enumerate(peers):
                sends.append(_remote(ins[t], _block(outs[t], x, y, c), send_sems, recv_sems, 4 * t + k, peer))
                recvs.append(_remote(ins[t], _block(outs[t], *peer), send_sems, recv_sems, 4 * t + k, peer))
        return sends, recvs, local

    return Exchange(list(shards), [jax.ShapeDtypeStruct((N_DEV,) + a.shape, a.dtype) for a in shards], {}, 4 * n, n, build)


def _gather_pass(bufs):
    n = len(bufs)

    def build(ins, outs, send_sems, recv_sems, local_sems):
        x, y, c = _mesh_pos()
        chips = [(1 - x, y), (x, 1 - y), (1 - x, 1 - y)]
        sends, recvs = [], []
        for t in range(n):
            for j, chip in enumerate(chips):
                mine, theirs = _block(outs[t], *chip, c), _block(outs[t], *chip, 1 - c)
                sends.append(_remote(mine, mine, send_sems, recv_sems, 3 * t + j, (x, y, 1 - c)))
                recvs.append(_remote(mine, theirs, send_sems, recv_sems, 3 * t + j, (x, y, 1 - c)))
        return sends, recvs, []

    return Exchange(list(bufs), [jax.ShapeDtypeStruct(a.shape, a.dtype) for a in bufs], {t: t for t in range(n)},
                    3 * n, 0, build)


def _scatter_sibling(gs):
    n = len(gs)

    def build(ins, outs, send_sems, recv_sems, local_sems):
        x, y, c = _mesh_pos()
        sends, recvs = [], []
        for t in range(n):
            for k in range(4):
                cp = _remote(ins[t].at[2 * k + 1 - c], outs[t].at[k], send_sems, recv_sems, 4 * t + k, (x, y, 1 - c))
                sends.append(cp)
                recvs.append(cp)
        return sends, recvs, []

    return Exchange(list(gs), [jax.ShapeDtypeStruct((4,) + g.shape[1:], g.dtype) for g in gs], {}, 4 * n, 0, build)


def _scatter_chips(ps):
    n = len(ps)

    def build(ins, outs, send_sems, recv_sems, local_sems):
        x, y, c = _mesh_pos()
        chips = [(1 - x, y), (x, 1 - y), (1 - x, 1 - y)]
        sends, recvs = [], []
        for t in range(n):
            for j, (cx, cy) in enumerate(chips):
                cp = _remote(ins[t].at[2 * cx + cy], outs[t].at[j], send_sems, recv_sems, 3 * t + j, (cx, cy, c))
                sends.append(cp)
                recvs.append(cp)
        return sends, recvs, []

    return Exchange(list(ps), [jax.ShapeDtypeStruct((3,) + p.shape[1:], p.dtype) for p in ps], {}, 3 * n, 0, build)


_DIMS = {"nn": ((1,), (0,)), "nt": ((1,), (1,)), "tn": ((0,), (0,))}


def _mm_call(a, b, *, mode, name, grid, kaxis, nk, a_spec, b_spec, o_specs, out_shape, acc_shape,
             extras=(), e_specs=(), epilogue=None, active=None, plan=None):
    ne, no = len(extras), len(out_shape)

    def body(a_ref, b_ref, *rest):
        e_refs, o_refs, acc_ref = rest[:ne], rest[ne:ne + no], rest[ne + no]
        ids = [pl.program_id(ax) for ax in range(len(grid))]
        k = ids[kaxis]

        def finish(acc):
            outs = (acc,) if epilogue is None else epilogue(acc, *[e[...] for e in e_refs])
            for o_ref, val in zip(o_refs, outs):
                o_ref[...] = val.astype(o_ref.dtype)

        def step():
            prod = lax.dot_general(a_ref[...], b_ref[...], (_DIMS[mode], ((), ())), preferred_element_type=F32)
            if nk == 1:
                finish(prod)
                return

            @pl.when(k == 0)
            def _():
                acc_ref[...] = prod

            @pl.when(k > 0)
            def _():
                acc_ref[...] += prod

        if active is None:
            step()
        else:
            pl.when(active(*ids))(step)
        if nk > 1:
            @pl.when(k == nk - 1)
            def _():
                finish(acc_ref[...])

    sem = tuple("arbitrary" if ax == kaxis else "parallel" for ax in range(len(grid)))
    scratch = pltpu.VMEM(acc_shape if nk > 1 else (8, 128), F32)
    return _pcall(body, (a, b, *extras), name=name, grid=grid, in_specs=[a_spec, b_spec, *e_specs],
                  out_specs=list(o_specs), out_shape=list(out_shape), scratch_shapes=[scratch],
                  sem=sem, vmem=VMEM_BIG, plan=plan)


def _mm(a, b, *, mode, name, out_dtypes=(F32,), extras=(), epilogue=None, tm=1024, tn=1024, tk=2048,
        mnk=None, b_spec=None, o_spec=None, out_struct=None, plan=None):
    if mnk is not None:
        m, n, k = mnk
    elif mode == "nn":
        (m, k), (_, n) = a.shape, b.shape
    elif mode == "nt":
        (m, k), (n, _) = a.shape, b.shape
    else:
        (k, m), (_, n) = a.shape, b.shape
    tm, tn, tk = _tile(m, tm), _tile(n, tn), _tile(k, tk)
    if tn > 2048:
        tm, tk = _tile(m, 512), _tile(k, 512)
    if tk > 2048:
        tm, tn = _tile(m, 512), _tile(n, 512)
    nk = k // tk
    if mode == "tn":
        a_spec = pl.BlockSpec((tk, tm), lambda i, j, kk: (kk, i))
    else:
        a_spec = pl.BlockSpec((tm, tk), lambda i, j, kk: (i, kk))
    if b_spec is not None:
        b_spec = b_spec(tn, tk)
    elif mode == "nt":
        b_spec = pl.BlockSpec((tn, tk), lambda i, j, kk: (j, kk))
    else:
        b_spec = pl.BlockSpec((tk, tn), lambda i, j, kk: (kk, j))

    def e_spec(e):
        if e.shape[1] == tn and n != tn:
            return pl.BlockSpec((tm, tn), lambda i, j, kk: (i, 0))
        return pl.BlockSpec((tm, tn), lambda i, j, kk: (i, j))

    e_specs = [e_spec(e) for e in extras]
    if o_spec is not None:
        o_specs = [o_spec(tm, tn)]
        out_shape = [out_struct]
    else:
        o_specs = [pl.BlockSpec((tm, tn), lambda i, j, kk: (i, j)) for _ in out_dtypes]
        out_shape = [jax.ShapeDtypeStruct((m, n), dt) for dt in out_dtypes]
    outs = _mm_call(a, b, mode=mode, name=name, grid=(m // tm, n // tn, nk), kaxis=2, nk=nk, a_spec=a_spec,
                    b_spec=b_spec, o_specs=o_specs, out_shape=out_shape, acc_shape=(tm, tn), extras=extras,
                    e_specs=e_specs, epilogue=epilogue, plan=plan)
    return outs[0] if len(outs) == 1 else outs


def _branch_of(kb):
    return jnp.minimum(kb, N_BRANCH - 1)


def _proj_merge_fwd(act_cat, proj_cat, zg, name, plan=None):
    s = act_cat.shape[0]
    tm, tn = _tile(s, 1024), 1024
    nj = D_MODEL // tn
    last = N_BRANCH - 1

    def kb(b, k):
        return jnp.where(b < last, b, last + k)

    def body(a_ref, b_ref, z_ref, y_ref, m_ref, acc_ref, sum_ref):
        b, k = pl.program_id(2), pl.program_id(3)

        @pl.when(jnp.logical_or(b == last, k == 0))
        def _():
            prod = jnp.dot(a_ref[...], b_ref[...], preferred_element_type=F32)

            @pl.when(k == 0)
            def _():
                acc_ref[...] = prod

            @pl.when(k > 0)
            def _():
                acc_ref[...] += prod

        @pl.when(k == 1)
        def _():
            y = acc_ref[...]
            y_ref[...] = y.astype(BF16)
            gated = _sigmoid(z_ref[...].astype(F32)) * y

            @pl.when(b == 0)
            def _():
                sum_ref[...] = gated

            @pl.when(b > 0)
            def _():
                sum_ref[...] += gated

            @pl.when(b == last)
            def _():
                m_ref[...] = sum_ref[...].astype(BF16)

    wide = pl.BlockSpec((tm, tn), lambda i, j, b, k: (i, b * nj + j))
    y, merged = _pcall(
        body, (act_cat, proj_cat, zg), name=name, grid=(s // tm, nj, N_BRANCH, 2),
        in_specs=[pl.BlockSpec((tm, BR), lambda i, j, b, k: (i, kb(b, k))),
                  pl.BlockSpec((BR, tn), lambda i, j, b, k: (kb(b, k), j)), wide],
        out_specs=[wide, pl.BlockSpec((tm, tn), lambda i, j, b, k: (i, j))],
        out_shape=[jax.ShapeDtypeStruct((s, N_GATE), BF16), jax.ShapeDtypeStruct((s, D_MODEL), BF16)],
        scratch_shapes=[pltpu.VMEM((tm, tn), F32), pltpu.VMEM((tm, tn), F32)],
        sem=("parallel", "parallel", "arbitrary", "arbitrary"), vmem=VMEM_BIG, plan=plan)
    return y, merged


def _proj_bwd_act(dy, proj_cat, name, plan=None):
    s = dy.shape[0]
    tm, tk = _tile(s, 1024), D_MODEL
    nkk = D_MODEL // tk
    nkb = ACT_CAT // BR
    out = _mm_call(
        dy, proj_cat, mode="nt", name=name, grid=(s // tm, nkb, nkk), kaxis=2, nk=nkk,
        a_spec=pl.BlockSpec((tm, tk), lambda i, kb, k: (i, _branch_of(kb) * nkk + k)),
        b_spec=pl.BlockSpec((BR, tk), lambda i, kb, k: (kb, k)),
        o_specs=[pl.BlockSpec((tm, BR), lambda i, kb, k: (i, kb))],
        out_shape=[jax.ShapeDtypeStruct((s, ACT_CAT), F32)], acc_shape=(tm, BR), plan=plan)
    return out[0]


def _proj_bwd_w(act_cat, dy, name):
    s = dy.shape[0]
    tms, tn = _tile(s, 2048), 1024
    nj = D_MODEL // tn
    nkb = ACT_CAT // BR
    nm = s // tms
    out = _mm_call(
        act_cat, dy, mode="tn", name=name, grid=(nkb, nj, nm), kaxis=2, nk=nm,
        a_spec=pl.BlockSpec((tms, BR), lambda kb, j, m: (m, kb)),
        b_spec=pl.BlockSpec((tms, tn), lambda kb, j, m: (m, _branch_of(kb) * nj + j)),
        o_specs=[pl.BlockSpec((BR, tn), lambda kb, j, m: (kb, j))],
        out_shape=[jax.ShapeDtypeStruct((ACT_CAT, D_MODEL), BF16)], acc_shape=(BR, tn))
    return out[0]


def _row_specs(ts, n_full, n_vec):
    return ([pl.BlockSpec((ts, D_MODEL), lambda i: (i, 0))] * n_full
            + [pl.BlockSpec((1, D_MODEL), lambda i: (0, 0))] * n_vec)


def _pre_norm(x, g, name):
    s = x.shape[0]
    ts = _tile(s, 256)

    def body(x_ref, g_ref, h_ref):
        h_ref[...] = _rms(x_ref[...], g_ref[...]).astype(BF16)

    return pl.pallas_call(body, name=name, grid=(s // ts,), in_specs=_row_specs(ts, 1, 1),
                          out_specs=pl.BlockSpec((ts, D_MODEL), lambda i: (i, 0)),
                          out_shape=jax.ShapeDtypeStruct((s, D_MODEL), BF16), compiler_params=_params(("parallel",)))(x, g)


def _post_pre(x, r, g_post, g_next, name):
    s = x.shape[0]
    ts = _tile(s, 256)

    def body(x_ref, r_ref, gp_ref, gn_ref, xn_ref, h_ref):
        xn = x_ref[...] + _rms(r_ref[...], gp_ref[...])
        xn_ref[...] = xn
        h_ref[...] = _rms(xn, gn_ref[...]).astype(BF16)

    spec = pl.BlockSpec((ts, D_MODEL), lambda i: (i, 0))
    return pl.pallas_call(body, name=name, grid=(s // ts,), in_specs=_row_specs(ts, 2, 2), out_specs=[spec, spec],
                          out_shape=[jax.ShapeDtypeStruct((s, D_MODEL), F32), jax.ShapeDtypeStruct((s, D_MODEL), BF16)],
                          compiler_params=_params(("parallel",)))(x, r, g_post, g_next)


def _final_loss(x, r, g_post, target, name):
    s = x.shape[0]
    ts = _tile(s, 256)

    def body(x_ref, r_ref, gp_ref, t_ref, dy_ref, dr_ref, dg_ref, loss_ref):
        first = pl.program_id(0) == 0
        rv, gp = r_ref[...], gp_ref[...]
        diff = x_ref[...] + _rms(rv, gp) - t_ref[...]
        part = 0.5 * jnp.sum(jnp.mean(diff * diff, axis=-1, keepdims=True), axis=0, keepdims=True)
        dy = diff * (1.0 / D_MODEL)
        dy_ref[...] = dy
        dr, dg_rows = _rms_bwd(rv, gp, dy)
        dr_ref[...] = dr.astype(BF16)
        _colsum_into(dg_ref, dg_rows, first)
        _colsum_into(loss_ref, jnp.broadcast_to(part, (1, 128)), first)

    spec = pl.BlockSpec((ts, D_MODEL), lambda i: (i, 0))
    vec = pl.BlockSpec((1, D_MODEL), lambda i: (0, 0))
    return pl.pallas_call(
        body, name=name, grid=(s // ts,), in_specs=[spec, spec, vec, spec],
        out_specs=[spec, spec, vec, pl.BlockSpec((1, 128), lambda i: (0, 0))],
        out_shape=[jax.ShapeDtypeStruct((s, D_MODEL), F32), jax.ShapeDtypeStruct((s, D_MODEL), BF16),
                   jax.ShapeDtypeStruct((1, D_MODEL), F32), jax.ShapeDtypeStruct((1, 128), F32)],
        compiler_params=_params(("arbitrary",)))(x, r, g_post, target)


def _pre_bwd(dh, x, g_pre, dx_res, name, r_prev=None, g_post_prev=None):
    s = x.shape[0]
    ts = _tile(s, 256)
    chain = r_prev is not None

    def body(*refs):
        if chain:
            dh_ref, x_ref, res_ref, r_ref, g_ref, gp_ref, dx_ref, dr_ref, dg_ref, dgp_ref = refs
        else:
            dh_ref, x_ref, res_ref, g_ref, dx_ref, dg_ref = refs
        first = pl.program_id(0) == 0
        dxn, dg_rows = _rms_bwd(x_ref[...], g_ref[...], dh_ref[...])
        dx = res_ref[...] + dxn
        dx_ref[...] = dx
        _colsum_into(dg_ref, dg_rows, first)
        if chain:
            dr, dgp_rows = _rms_bwd(r_ref[...], gp_ref[...], dx)
            dr_ref[...] = dr.astype(BF16)
            _colsum_into(dgp_ref, dgp_rows, first)

    spec = pl.BlockSpec((ts, D_MODEL), lambda i: (i, 0))
    vec = pl.BlockSpec((1, D_MODEL), lambda i: (0, 0))
    full = jax.ShapeDtypeStruct((s, D_MODEL), F32)
    vshape = jax.ShapeDtypeStruct((1, D_MODEL), F32)
    if chain:
        return pl.pallas_call(
            body, name=name, grid=(s // ts,), in_specs=[spec] * 4 + [vec] * 2, out_specs=[spec, spec, vec, vec],
            out_shape=[full, jax.ShapeDtypeStruct((s, D_MODEL), BF16), vshape, vshape],
            compiler_params=_params(("arbitrary",)))(dh, x, dx_res, r_prev, g_pre, g_post_prev)
    return pl.pallas_call(
        body, name=name, grid=(s // ts,), in_specs=[spec] * 3 + [vec], out_specs=[spec, vec],
        out_shape=[full, vshape], compiler_params=_params(("arbitrary",)))(dh, x, dx_res, g_pre)


def _merge_bwd(dm, y, zg, name, plan=None):
    s = y.shape[0]
    ts, tc = _tile(s, 512), 1024
    nj = D_MODEL // tc

    def body(dm_ref, y_ref, z_ref, dy_ref, dz_ref):
        g = _sigmoid(z_ref[...].astype(F32))
        d = dm_ref[...]
        dy_ref[...] = (d * g).astype(BF16)
        dz_ref[...] = (d * y_ref[...] * g * (1.0 - g)).astype(BF16)

    blk = pl.BlockSpec((ts, tc), lambda i, j, b: (i, b * nj + j))
    shape = jax.ShapeDtypeStruct((s, N_GATE), BF16)
    return _pcall(body, (dm, y, zg), name=name, grid=(s // ts, nj, N_BRANCH),
                  in_specs=[pl.BlockSpec((ts, tc), lambda i, j, b: (i, j)), blk, blk], out_specs=[blk, blk],
                  out_shape=[shape, shape], sem=("parallel", "parallel", "parallel"), vmem=VMEM_BIG, plan=plan)


POOL_HALO = 16


def _pool_windows(ext_ref, ts, first_row):
    outs = []
    t = first_row + lax.broadcasted_iota(jnp.int32, (ts, GDIM), 0)
    for gi, w in enumerate(POOL_WINDOWS):
        cols = pl.ds(gi * GDIM, GDIM)
        acc = ext_ref[pl.ds(POOL_HALO, ts), cols]
        cur = acc
        for k in range(1, w):
            acc = acc + ext_ref[pl.ds(POOL_HALO - k, ts), cols]
        cnt = jnp.minimum(t + 1, w).astype(F32)
        outs.append(acc / cnt - cur)
    return outs


def _pool_fwd(za, pool_w, pool_scale, name):
    s = za.shape[0]
    ts = _tile(s, 512)
    hb = ts // POOL_HALO

    def body(a_ref, halo_ref, w_ref, sc_ref, o_ref, ext_ref):
        i = pl.program_id(0)
        ext_ref[pl.ds(0, POOL_HALO), :] = jnp.where(i > 0, halo_ref[...], 0.0)
        ext_ref[pl.ds(POOL_HALO, ts), :] = a_ref[...]
        pooled = _pool_windows(ext_ref, ts, i * ts)
        for gi in range(len(POOL_WINDOWS)):
            mixed = jnp.dot(pooled[gi].astype(BF16), w_ref[gi].astype(BF16), preferred_element_type=F32)
            o_ref[:, pl.ds(gi * GDIM, GDIM)] = (mixed * sc_ref[:, pl.ds(gi * GDIM, GDIM)]).astype(BF16)

    return pl.pallas_call(
        body, name=name, grid=(s // ts,),
        in_specs=[pl.BlockSpec((ts, BR), lambda i: (i, 0)),
                  pl.BlockSpec((POOL_HALO, BR), lambda i: (jnp.maximum(i * hb - 1, 0), 0)),
                  pl.BlockSpec((4, GDIM, GDIM), lambda i: (0, 0, 0)), pl.BlockSpec((1, BR), lambda i: (0, 0))],
        out_specs=pl.BlockSpec((ts, BR), lambda i: (i, 0)), out_shape=jax.ShapeDtypeStruct((s, BR), BF16),
        scratch_shapes=[pltpu.VMEM((POOL_HALO + ts, BR), F32)], compiler_params=_params(("parallel",)))(
            za, za, pool_w, pool_scale)


def _pool_bwd(dact, za, pool_w, pool_scale, name):
    s = za.shape[0]
    ts = _tile(s, 512)
    hb = ts // POOL_HALO
    n_t = s // ts

    def body(d_ref, dhalo_ref, a_ref, halo_ref, w_ref, sc_ref, dz_ref, dw_ref, dsc_ref, ext_ref, f_ref):
        i = pl.program_id(0)
        first = i == 0
        ext_ref[pl.ds(0, POOL_HALO), :] = jnp.where(i > 0, halo_ref[...], 0.0)
        ext_ref[pl.ds(POOL_HALO, ts), :] = a_ref[...]
        pooled = _pool_windows(ext_ref, ts, i * ts)
        d_tile = d_ref[...]
        d_next = jnp.where(i < n_t - 1, dhalo_ref[...], 0.0)
        t_ext = i * ts + lax.broadcasted_iota(jnp.int32, (ts + POOL_HALO, GDIM), 0)
        dsc = []
        for gi, w in enumerate(POOL_WINDOWS):
            cols = pl.ds(gi * GDIM, GDIM)
            wg = w_ref[gi].astype(BF16)
            sc = sc_ref[:, cols]
            pg = pooled[gi].astype(BF16)
            mixed = jnp.dot(pg, wg, preferred_element_type=F32)
            dsc.append(jnp.sum(d_tile[:, gi * GDIM:(gi + 1) * GDIM] * mixed, axis=0, keepdims=True))
            dmix = jnp.concatenate([d_tile[:, gi * GDIM:(gi + 1) * GDIM], d_next[:, gi * GDIM:(gi + 1) * GDIM]], axis=0) * sc
            dmix = dmix.astype(BF16)
            dwg = lax.dot_general(pg, dmix[:ts], (((0,), (0,)), ((), ())), preferred_element_type=F32)

            @pl.when(first)
            def _():
                dw_ref[gi] = dwg

            @pl.when(jnp.logical_not(first))
            def _():
                dw_ref[gi] += dwg

            dpool = lax.dot_general(dmix, wg, (((1,), (1,)), ((), ())), preferred_element_type=F32)
            f_ref[:, cols] = dpool / jnp.minimum(t_ext + 1, w).astype(F32)
            acc = f_ref[pl.ds(0, ts), cols]
            for k in range(1, w):
                acc = acc + f_ref[pl.ds(k, ts), cols]
            dz_ref[:, cols] = (acc - dpool[:ts]).astype(BF16)
        dsc_all = jnp.concatenate(dsc, axis=1)

        @pl.when(first)
        def _():
            dsc_ref[...] = dsc_all

        @pl.when(jnp.logical_not(first))
        def _():
            dsc_ref[...] += dsc_all

    n_hb = s // POOL_HALO
    return pl.pallas_call(
        body, name=name, grid=(n_t,),
        in_specs=[pl.BlockSpec((ts, BR), lambda i: (i, 0)),
                  pl.BlockSpec((POOL_HALO, BR), lambda i: (jnp.minimum((i + 1) * hb, n_hb - 1), 0)),
                  pl.BlockSpec((ts, BR), lambda i: (i, 0)),
                  pl.BlockSpec((POOL_HALO, BR), lambda i: (jnp.maximum(i * hb - 1, 0), 0)),
                  pl.BlockSpec((4, GDIM, GDIM), lambda i: (0, 0, 0)), pl.BlockSpec((1, BR), lambda i: (0, 0))],
        out_specs=[pl.BlockSpec((ts, BR), lambda i: (i, 0)), pl.BlockSpec((4, GDIM, GDIM), lambda i: (0, 0, 0)),
                   pl.BlockSpec((1, BR), lambda i: (0, 0))],
        out_shape=[jax.ShapeDtypeStruct((s, BR), BF16), jax.ShapeDtypeStruct((4, GDIM, GDIM), F32),
                   jax.ShapeDtypeStruct((1, BR), F32)],
        scratch_shapes=[pltpu.VMEM((POOL_HALO + ts, BR), F32), pltpu.VMEM((ts + POOL_HALO, BR), F32)],
        compiler_params=_params(("arbitrary",)))(dact, dact, za, za, pool_w, pool_scale)


CONV_HALO = 32
CONV_LEAD = CONV_HALO - (CONV_WIDTH - 1)


def _conv_fwd(za, conv_w, conv_b, ng, nb, name):
    s = za.shape[0]
    ts = _tile(s, 512)
    hb = ts // CONV_HALO

    def body(a_ref, g_ref, ah_ref, gh_ref, w_ref, b_ref, ng_ref, nb_ref, yc_ref, act_ref, ext_ref):
        i = pl.program_id(0)
        ext_ref[pl.ds(0, CONV_HALO), :] = jnp.where(i > 0, ah_ref[...] * _sigmoid(gh_ref[...]), 0.0)
        ext_ref[pl.ds(CONV_HALO, ts), :] = a_ref[...] * _sigmoid(g_ref[...])
        acc = jnp.zeros((ts, BR), F32) + b_ref[...]
        for k in range(CONV_WIDTH):
            acc = acc + w_ref[pl.ds(k, 1), :] * ext_ref[pl.ds(CONV_LEAD + k, ts), :]
        yc_ref[...] = acc
        xh, _ = _ln_stats(acc)
        ln = xh * ng_ref[...] + nb_ref[...]
        act_ref[...] = (ln * _sigmoid(ln)).astype(BF16)

    tile = lambda c: pl.BlockSpec((ts, BR), lambda i: (i, c))
    halo = lambda c: pl.BlockSpec((CONV_HALO, BR), lambda i: (jnp.maximum(i * hb - 1, 0), c))
    vec = pl.BlockSpec((1, BR), lambda i: (0, 0))
    return pl.pallas_call(
        body, name=name, grid=(s // ts,),
        in_specs=[tile(1), tile(2), halo(1), halo(2), pl.BlockSpec((CONV_WIDTH, BR), lambda i: (0, 0)), vec, vec, vec],
        out_specs=[pl.BlockSpec((ts, BR), lambda i: (i, 0))] * 2,
        out_shape=[jax.ShapeDtypeStruct((s, BR), F32), jax.ShapeDtypeStruct((s, BR), BF16)],
        scratch_shapes=[pltpu.VMEM((CONV_HALO + ts, BR), F32)], compiler_params=_params(("parallel",)))(
            za, za, za, za, conv_w, conv_b, ng, nb)


def _conv_bwd_norm(dact, yc, ng, nb, name):
    s = yc.shape[0]
    ts = _tile(s, 512)

    def body(d_ref, y_ref, ng_ref, nb_ref, dy_ref, db_ref, dng_ref, dnb_ref):
        first = pl.program_id(0) == 0
        xh, r = _ln_stats(y_ref[...])
        g = ng_ref[...]
        ln = xh * g + nb_ref[...]
        sg = _sigmoid(ln)
        dln = d_ref[...] * sg * (1.0 + ln * (1.0 - sg))
        dy = _ln_bwd(xh, r, g, dln)
        dy_ref[...] = dy
        _colsum_into(db_ref, dy, first)
        _colsum_into(dng_ref, dln * xh, first)
        _colsum_into(dnb_ref, dln, first)

    vec = pl.BlockSpec((1, BR), lambda i: (0, 0))
    vshape = jax.ShapeDtypeStruct((1, BR), F32)
    return pl.pallas_call(
        body, name=name, grid=(s // ts,),
        in_specs=[pl.BlockSpec((ts, BR), lambda i: (i, 1)), pl.BlockSpec((ts, BR), lambda i: (i, 0)), vec, vec],
        out_specs=[pl.BlockSpec((ts, BR), lambda i: (i, 0)), vec, vec, vec],
        out_shape=[jax.ShapeDtypeStruct((s, BR), F32), vshape, vshape, vshape],
        compiler_params=_params(("arbitrary",)))(dact, yc, ng, nb)


def _conv_bwd_taps(dyc, za, conv_w, name):
    s = za.shape[0]
    ts = _tile(s, 512)
    hb = ts // CONV_HALO
    n_t = s // ts
    n_hb = s // CONV_HALO

    def body(d_ref, dh_ref, a_ref, g_ref, ah_ref, gh_ref, w_ref, dz_ref, dw_ref, ext_ref, f_ref):
        i = pl.program_id(0)
        first = i == 0
        a, sg = a_ref[...], _sigmoid(g_ref[...])
        ext_ref[pl.ds(0, CONV_HALO), :] = jnp.where(i > 0, ah_ref[...] * _sigmoid(gh_ref[...]), 0.0)
        ext_ref[pl.ds(CONV_HALO, ts), :] = a * sg
        d = d_ref[...]
        f_ref[pl.ds(0, ts), :] = d
        f_ref[pl.ds(ts, CONV_HALO), :] = jnp.where(i < n_t - 1, dh_ref[...], 0.0)
        dglu = jnp.zeros((ts, BR), F32)
        rows = []
        for k in range(CONV_WIDTH):
            rows.append(jnp.sum(d * ext_ref[pl.ds(CONV_LEAD + k, ts), :], axis=0, keepdims=True))
            dglu = dglu + w_ref[pl.ds(k, 1), :] * f_ref[pl.ds(CONV_WIDTH - 1 - k, ts), :]
        rows.append(jnp.zeros((1, BR), F32))
        dw = jnp.concatenate(rows, axis=0)

        @pl.when(first)
        def _():
            dw_ref[...] = dw

        @pl.when(jnp.logical_not(first))
        def _():
            dw_ref[...] += dw

        dz_ref[:, pl.ds(0, BR)] = (dglu * sg).astype(BF16)
        dz_ref[:, pl.ds(BR, BR)] = (dglu * a * sg * (1.0 - sg)).astype(BF16)

    tile = lambda c: pl.BlockSpec((ts, BR), lambda i: (i, c))
    halo = lambda c: pl.BlockSpec((CONV_HALO, BR), lambda i: (jnp.maximum(i * hb - 1, 0), c))
    return pl.pallas_call(
        body, name=name, grid=(n_t,),
        in_specs=[pl.BlockSpec((ts, BR), lambda i: (i, 0)),
                  pl.BlockSpec((CONV_HALO, BR), lambda i: (jnp.minimum((i + 1) * hb, n_hb - 1), 0)),
                  tile(1), tile(2), halo(1), halo(2), pl.BlockSpec((CONV_WIDTH, BR), lambda i: (0, 0))],
        out_specs=[pl.BlockSpec((ts, 2 * BR), lambda i: (i, 0)), pl.BlockSpec((CONV_WIDTH + 1, BR), lambda i: (0, 0))],
        out_shape=[jax.ShapeDtypeStruct((s, 2 * BR), BF16), jax.ShapeDtypeStruct((CONV_WIDTH + 1, BR), F32)],
        scratch_shapes=[pltpu.VMEM((CONV_HALO + ts, BR), F32), pltpu.VMEM((ts + CONV_HALO, BR), F32)],
        compiler_params=_params(("arbitrary",)))(dyc, dyc, za, za, za, za, conv_w)


def _tril(w):
    r = lax.broadcasted_iota(jnp.int32, (CHUNK, CHUNK), 0)
    c = lax.broadcasted_iota(jnp.int32, (CHUNK, CHUNK), 1)
    return jnp.where(c <= r, w, 0.0)


def _sgu_fwd(za, ng, nb, sgu_w, bias_b, name):
    s = za.shape[0]
    ts = _tile(s, 512)

    def body(u_ref, v_ref, ng_ref, nb_ref, w_ref, b_ref, o_ref):
        u = _gelu(u_ref[...])
        xh, _ = _ln_stats(_gelu(v_ref[...]))
        vln = (xh * ng_ref[...] + nb_ref[...]).astype(BF16)
        for gi in range(4):
            wg = _tril(w_ref[gi]).astype(BF16)
            for n in range(ts // CHUNK):
                blk = vln[n * CHUNK:(n + 1) * CHUNK, gi * GDIM:(gi + 1) * GDIM]
                sp = jnp.dot(wg, blk, preferred_element_type=F32) + b_ref[gi]
                o_ref[pl.ds(n * CHUNK, CHUNK), pl.ds(gi * GDIM, GDIM)] = (
                    u[n * CHUNK:(n + 1) * CHUNK, gi * GDIM:(gi + 1) * GDIM] * sp).astype(BF16)

    vec = pl.BlockSpec((1, BR), lambda i: (0, 0))
    cube = pl.BlockSpec((4, CHUNK, GDIM), lambda i: (0, 0, 0))
    return pl.pallas_call(
        body, name=name, grid=(s // ts,),
        in_specs=[pl.BlockSpec((ts, BR), lambda i: (i, 3)), pl.BlockSpec((ts, BR), lambda i: (i, 4)), vec, vec, cube, cube],
        out_specs=pl.BlockSpec((ts, BR), lambda i: (i, 0)), out_shape=jax.ShapeDtypeStruct((s, BR), BF16),
        compiler_params=_params(("parallel",)))(za, za, ng, nb, sgu_w, bias_b)


def _sgu_bwd(dact, za, ng, nb, sgu_w, bias_b, name):
    s = za.shape[0]
    ts = _tile(s, 512)

    def body(d_ref, u_ref, v_ref, ng_ref, nb_ref, w_ref, b_ref, dz_ref, dw_ref, db_ref, dng_ref, dnb_ref, dv_ref):
        first = pl.program_id(0) == 0
        u_raw, v_raw = u_ref[...], v_ref[...]
        u = _gelu(u_raw)
        xh, r = _ln_stats(_gelu(v_raw))
        g = ng_ref[...]
        vln = (xh * g + nb_ref[...]).astype(BF16)
        d = d_ref[...]
        dsp = d * u
        dsp16 = dsp.astype(BF16)
        for gi in range(4):
            wg = _tril(w_ref[gi]).astype(BF16)
            dwg = jnp.zeros((CHUNK, CHUNK), F32)
            dbg = jnp.zeros((CHUNK, 1), F32)
            for n in range(ts // CHUNK):
                rows, cols = slice(n * CHUNK, (n + 1) * CHUNK), slice(gi * GDIM, (gi + 1) * GDIM)
                blk = vln[rows, cols]
                sp = jnp.dot(wg, blk, preferred_element_type=F32) + b_ref[gi]
                dz_ref[pl.ds(n * CHUNK, CHUNK), pl.ds(gi * GDIM, GDIM)] = (
                    d[rows, cols] * sp * _gelu_grad(u_raw[rows, cols])).astype(BF16)
                dv_ref[pl.ds(n * CHUNK, CHUNK), pl.ds(gi * GDIM, GDIM)] = lax.dot_general(
                    wg, dsp16[rows, cols], (((0,), (0,)), ((), ())), preferred_element_type=F32)
                dwg = dwg + lax.dot_general(dsp16[rows, cols], blk, (((1,), (1,)), ((), ())), preferred_element_type=F32)
                dbg = dbg + jnp.sum(dsp[rows, cols], axis=1, keepdims=True)
            dwg = _tril(dwg)

            @pl.when(first)
            def _():
                dw_ref[gi] = dwg
                db_ref[gi] = dbg

            @pl.when(jnp.logical_not(first))
            def _():
                dw_ref[gi] += dwg
                db_ref[gi] += dbg

        dvln = dv_ref[...]
        dz_ref[:, pl.ds(BR, BR)] = (_ln_bwd(xh, r, g, dvln) * _gelu_grad(v_raw)).astype(BF16)
        _colsum_into(dng_ref, dvln * xh, first)
        _colsum_into(dnb_ref, dvln, first)

    vec = pl.BlockSpec((1, BR), lambda i: (0, 0))
    cube = pl.BlockSpec((4, CHUNK, GDIM), lambda i: (0, 0, 0))
    vshape = jax.ShapeDtypeStruct((1, BR), F32)
    return pl.pallas_call(
        body, name=name, grid=(s // ts,),
        in_specs=[pl.BlockSpec((ts, BR), lambda i: (i, 2)), pl.BlockSpec((ts, BR), lambda i: (i, 3)),
                  pl.BlockSpec((ts, BR), lambda i: (i, 4)), vec, vec, cube, cube],
        out_specs=[pl.BlockSpec((ts, 2 * BR), lambda i: (i, 0)), cube, pl.BlockSpec((4, CHUNK, 1), lambda i: (0, 0, 0)),
                   vec, vec],
        out_shape=[jax.ShapeDtypeStruct((s, 2 * BR), BF16), jax.ShapeDtypeStruct((4, CHUNK, CHUNK), F32),
                   jax.ShapeDtypeStruct((4, CHUNK, 1), F32), vshape, vshape],
        scratch_shapes=[pltpu.VMEM((ts, BR), F32)], compiler_params=_params(("arbitrary",)))(
            dact, za, za, ng, nb, sgu_w, bias_b)


KR_BLOCK = 3584 // 128


def _mla_prep(za, qg, kvg, ck, sk, name):
    s = za.shape[0]
    ts = _tile(s, 512)

    def body(cq_ref, ckv_ref, kr_ref, qg_ref, kvg_ref, c_ref, s_ref, qn_ref, kvn_ref, krr_ref):
        qn_ref[...] = _rms(cq_ref[...], qg_ref[...]).astype(BF16)
        kvn_ref[...] = _rms(ckv_ref[...], kvg_ref[...]).astype(BF16)
        kr = kr_ref[...]
        krr_ref[...] = (kr * c_ref[...] + _rot_half(kr, 128, 0) * s_ref[...]).astype(BF16)

    vec = pl.BlockSpec((1, BR), lambda i: (0, 0))
    tab = pl.BlockSpec((ts, 128), lambda i: (i, 0))
    row = pl.BlockSpec((ts, BR), lambda i: (i, 0))
    return pl.pallas_call(
        body, name=name, grid=(s // ts,),
        in_specs=[pl.BlockSpec((ts, BR), lambda i: (i, 5)), pl.BlockSpec((ts, BR), lambda i: (i, 6)),
                  pl.BlockSpec((ts, 128), lambda i: (i, KR_BLOCK)), vec, vec, tab, tab],
        out_specs=[row, row, tab],
        out_shape=[jax.ShapeDtypeStruct((s, BR), BF16), jax.ShapeDtypeStruct((s, BR), BF16),
                   jax.ShapeDtypeStruct((s, 128), BF16)],
        compiler_params=_params(("parallel",)))(za, za, za, qg, kvg, ck, sk)


def _mla_prep_bwd(dqn, dkvn, dkr_heads, za, qg, kvg, ck, sk, name):
    s = za.shape[0]
    ts = _tile(s, 512)

    def body(dq_ref, dkv_ref, dkr_ref, cq_ref, ckv_ref, qg_ref, kvg_ref, c_ref, s_ref, dz_ref, dqg_ref, dkvg_ref):
        first = pl.program_id(0) == 0
        dcq, rows_q = _rms_bwd(cq_ref[...], qg_ref[...], dq_ref[...])
        dckv, rows_kv = _rms_bwd(ckv_ref[...], kvg_ref[...], dkv_ref[...])
        dz_ref[:, pl.ds(0, BR)] = dcq.astype(BF16)
        dz_ref[:, pl.ds(BR, BR)] = dckv.astype(BF16)
        dk = dkr_ref[:, pl.ds(0, 128)]
        for h in range(1, MLA_HEADS):
            dk = dk + dkr_ref[:, pl.ds(h * 128, 128)]
        dz_ref[:, pl.ds(2 * BR, 128)] = (dk * c_ref[...] - _rot_half(dk, 128, 0) * s_ref[...]).astype(BF16)
        _colsum_into(dqg_ref, rows_q, first)
        _colsum_into(dkvg_ref, rows_kv, first)

    vec = pl.BlockSpec((1, BR), lambda i: (0, 0))
    tab = pl.BlockSpec((ts, 128), lambda i: (i, 0))
    row = pl.BlockSpec((ts, BR), lambda i: (i, 0))
    wide = 2 * BR + 128
    vshape = jax.ShapeDtypeStruct((1, BR), F32)
    return pl.pallas_call(
        body, name=name, grid=(s // ts,),
        in_specs=[row, row, pl.BlockSpec((ts, MLA_HEADS * 128), lambda i: (i, 0)),
                  pl.BlockSpec((ts, BR), lambda i: (i, 5)), pl.BlockSpec((ts, BR), lambda i: (i, 6)), vec, vec, tab, tab],
        out_specs=[pl.BlockSpec((ts, wide), lambda i: (i, 0)), vec, vec],
        out_shape=[jax.ShapeDtypeStruct((s, wide), BF16), vshape, vshape],
        compiler_params=_params(("arbitrary",)))(dqn, dkvn, dkr_heads, za, za, qg, kvg, ck, sk)


def _attn_tiles(s):
    tq, tk = _tile(s, 1024), _tile(s, 512)
    return tq, tk, tq // tk


def _causal(qi, ki, tq, tk):
    row = qi * tq + lax.broadcasted_iota(jnp.int32, (tq, tk), 0)
    col = ki * tk + lax.broadcasted_iota(jnp.int32, (tq, tk), 1)
    return col <= row


def _step_count(t, firsts):
    n = jnp.int32(0)
    for f in firsts[1:]:
        n = n + jnp.where(t >= f, 1, 0)
    return n


def _q_major_pairs(nq, r):
    firsts = [r * qq * (qq + 1) // 2 for qq in range(nq)]

    def pair(t):
        qi = _step_count(t, firsts)
        return qi, t - r * qi * (qi + 1) // 2

    return r * nq * (nq + 1) // 2, pair


def _k_major_pairs(nq, nk, r):
    counts = [nq - kk // r for kk in range(nk)]
    firsts = [sum(counts[:kk]) for kk in range(nk)]

    def pair(t):
        ki = _step_count(t, firsts)
        first = jnp.int32(0)
        for kk in range(1, nk):
            first = first + jnp.where(t >= firsts[kk], counts[kk - 1], 0)
        return ki // r + (t - first), ki

    return sum(counts), pair


FLASH_HEADS = 2


def _flash_fwd(q, kv, krr, name, plan=None):
    s = q.shape[0]
    tq, tk, r = _attn_tiles(s)
    n_pairs, pair = _q_major_pairs(s // tq, r)
    hp = FLASH_HEADS

    def body(q_ref, kv_ref, kr_ref, o_ref, lse_ref, m_sc, l_sc, acc_sc):
        qi, ki = pair(pl.program_id(1))

        @pl.when(ki == 0)
        def _():
            m_sc[...] = jnp.full((hp, tq, 1), NEG, F32)
            l_sc[...] = jnp.zeros((hp, tq, 1), F32)
            acc_sc[...] = jnp.zeros((hp, tq, V_DIM), F32)

        def step(masked):
            for h in range(hp):
                k = jnp.concatenate([kv_ref[:, pl.ds(h * QW, QK_NOPE)], kr_ref[...]], axis=1)
                sc = lax.dot_general(q_ref[:, pl.ds(h * QW, QW)], k, (((1,), (1,)), ((), ())), preferred_element_type=F32)
                if masked:
                    sc = jnp.where(_causal(qi, ki, tq, tk), sc, NEG)
                m_prev = m_sc[h]
                m_new = jnp.maximum(m_prev, jnp.max(sc, axis=1, keepdims=True))
                alpha = jnp.exp(m_prev - m_new)
                p = jnp.exp(sc - m_new)
                l_sc[h] = alpha * l_sc[h] + jnp.sum(p, axis=1, keepdims=True)
                acc_sc[h] = alpha * acc_sc[h] + jnp.dot(p.astype(BF16), kv_ref[:, pl.ds(h * QW + QK_NOPE, V_DIM)],
                                                        preferred_element_type=F32)
                m_sc[h] = m_new

        pl.when(ki < qi * r)(functools.partial(step, False))
        pl.when(ki >= qi * r)(functools.partial(step, True))

        @pl.when(ki == (qi + 1) * r - 1)
        def _():
            for h in range(hp):
                o_ref[:, pl.ds(h * V_DIM, V_DIM)] = (acc_sc[h] / l_sc[h]).astype(BF16)
                lse_ref[:, pl.ds(h * 128, 128)] = jnp.broadcast_to(m_sc[h] + jnp.log(l_sc[h]), (tq, 128))

    out_blk = pl.BlockSpec((tq, hp * 128), lambda g, t: (pair(t)[0], g))
    return _pcall(
        body, (q, kv, krr), name=name, grid=(MLA_HEADS // hp, n_pairs),
        in_specs=[pl.BlockSpec((tq, hp * QW), lambda g, t: (pair(t)[0], g)),
                  pl.BlockSpec((tk, hp * QW), lambda g, t: (pair(t)[1], g)),
                  pl.BlockSpec((tk, 128), lambda g, t: (pair(t)[1], 0))],
        out_specs=[out_blk, out_blk],
        out_shape=[jax.ShapeDtypeStruct((s, MLA_HEADS * V_DIM), BF16), jax.ShapeDtypeStruct((s, MLA_HEADS * 128), F32)],
        scratch_shapes=[pltpu.VMEM((hp, tq, 1), F32), pltpu.VMEM((hp, tq, 1), F32), pltpu.VMEM((hp, tq, V_DIM), F32)],
        sem=("parallel", "arbitrary"), vmem=VMEM_BIG, plan=plan)


DO_BLOCK = 3 * BR // 128


def _flash_probs(q_ref, kn_ref, kr_ref, v_ref, do_ref, o_ref, lse_ref, qi, ki, tq, tk, masked):
    k = jnp.concatenate([kn_ref[...], kr_ref[...]], axis=1)
    q = q_ref[...]
    sc = lax.dot_general(q, k, (((1,), (1,)), ((), ())), preferred_element_type=F32)
    lse = jnp.max(lse_ref[...], axis=1, keepdims=True)
    p = jnp.exp(sc - lse)
    if masked:
        p = jnp.where(_causal(qi, ki, tq, tk), p, 0.0)
    do = do_ref[...]
    delta = jnp.sum(do * o_ref[...].astype(F32), axis=1, keepdims=True)
    do = do.astype(BF16)
    dp = lax.dot_general(do, v_ref[...], (((1,), (1,)), ((), ())), preferred_element_type=F32)
    ds = (p * (dp - delta)).astype(BF16)
    return q, k, p, do, ds


def _flash_bwd(q, kv, krr, dact, o, lse, cq, sq, name, plan=None):
    s = q.shape[0]
    tq, tk, r = _attn_tiles(s)
    nq, nk = s // tq, s // tk
    n_pairs, pair = _k_major_pairs(nq, nk, r)

    def body(q_ref, kn_ref, kr_ref, v_ref, do_ref, o_ref, lse_ref, c_ref, s_ref, dkv_ref, dkr_ref, dq_ref,
             dk_sc, dv_sc, dq_sc):
        qi, ki = pair(pl.program_id(1))
        rows = pl.ds(pl.multiple_of(qi * tq, tq), tq)

        @pl.when(qi == ki // r)
        def _():
            dk_sc[...] = jnp.zeros((tk, QW), F32)
            dv_sc[...] = jnp.zeros((tk, V_DIM), F32)

        @pl.when(ki == 0)
        def _():
            dq_sc[rows, :] = jnp.zeros((tq, QW), F32)

        def step(masked):
            qv, k, p, dov, ds = _flash_probs(q_ref, kn_ref, kr_ref, v_ref, do_ref, o_ref, lse_ref, qi, ki, tq, tk, masked)
            dv_sc[...] += lax.dot_general(p.astype(BF16), dov, (((0,), (0,)), ((), ())), preferred_element_type=F32)
            dk_sc[...] += lax.dot_general(ds, qv, (((0,), (0,)), ((), ())), preferred_element_type=F32)
            dq_sc[rows, :] += jnp.dot(ds, k, preferred_element_type=F32)

        pl.when(ki < qi * r)(functools.partial(step, False))
        pl.when(ki >= qi * r)(functools.partial(step, True))

        @pl.when(qi == nq - 1)
        def _():
            dkv_ref[:, pl.ds(0, 128)] = dk_sc[:, pl.ds(0, 128)].astype(BF16)
            dkv_ref[:, pl.ds(128, 128)] = dv_sc[...].astype(BF16)
            dkr_ref[...] = dk_sc[:, pl.ds(128, 128)]

        @pl.when(ki == (qi + 1) * r - 1)
        def _():
            dq = dq_sc[rows, :] * ATT_SCALE
            dq_ref[...] = (dq * c_ref[...] - _rot_half(dq, QW, QK_NOPE) * s_ref[...]).astype(BF16)

    def next_done(t):
        qi, ki = pair(t)
        return jnp.minimum(ki // r + jnp.where(jnp.logical_and(ki % r == r - 1, qi > ki // r), 1, 0), nq - 1)

    qmap = lambda c: (lambda h, t: (pair(t)[0], c(h)))
    kmap = lambda c: (lambda h, t: (pair(t)[1], c(h)))
    last = lambda c: (lambda h, t: (next_done(t), c(h)))
    return _pcall(
        body, (q, kv, krr, kv, dact, o, lse, cq, sq), name=name, grid=(MLA_HEADS, n_pairs),
        in_specs=[pl.BlockSpec((tq, QW), qmap(lambda h: h)), pl.BlockSpec((tk, 128), kmap(lambda h: 2 * h)),
                  pl.BlockSpec((tk, 128), kmap(lambda h: 0)), pl.BlockSpec((tk, 128), kmap(lambda h: 2 * h + 1)),
                  pl.BlockSpec((tq, 128), qmap(lambda h: DO_BLOCK + h)), pl.BlockSpec((tq, 128), qmap(lambda h: h)),
                  pl.BlockSpec((tq, 128), qmap(lambda h: h)), pl.BlockSpec((tq, QW), last(lambda h: 0)),
                  pl.BlockSpec((tq, QW), last(lambda h: 0))],
        out_specs=[pl.BlockSpec((tk, QW), kmap(lambda h: h)), pl.BlockSpec((tk, 128), kmap(lambda h: h)),
                   pl.BlockSpec((tq, QW), last(lambda h: h))],
        out_shape=[jax.ShapeDtypeStruct((s, MLA_HEADS * QW), BF16), jax.ShapeDtypeStruct((s, MLA_HEADS * 128), F32),
                   jax.ShapeDtypeStruct((s, MLA_HEADS * QW), BF16)],
        scratch_shapes=[pltpu.VMEM((tk, QW), F32), pltpu.VMEM((tk, V_DIM), F32), pltpu.VMEM((s, QW), F32)],
        sem=("parallel", "arbitrary"), vmem=VMEM_BIG, plan=plan)


def _rope_tables(positions):
    inv_freq = ROPE_THETA ** (-jnp.arange(0, QK_ROPE, 2, dtype=F32) / QK_ROPE)
    ang = positions.reshape(-1).astype(F32)[:, None] * inv_freq
    cos, sin = jnp.cos(ang), jnp.sin(ang)
    s = cos.shape[0]
    one, zero = jnp.ones((s, 64), F32), jnp.zeros((s, 64), F32)
    ck = jnp.concatenate([cos, cos, one], axis=1)
    sk = jnp.concatenate([sin, sin, zero], axis=1)
    cq = jnp.concatenate([one, one, ck], axis=1)
    sq = jnp.concatenate([zero, zero, sk], axis=1)
    return ck, sk, cq, sq


def _cols_full(gathered):
    _, r, c = gathered.shape
    return gathered.transpose(1, 0, 2).reshape(r, N_DEV * c)


def _cols_by_owner(full):
    r, n = full.shape
    return full.reshape(r, N_DEV, n // N_DEV).transpose(1, 0, 2)


def _layer_weights(gat, small, l):
    shards = gat[("w_in", l)]
    cut, rem = divmod(N_IN_A, shards.shape[2])
    w = {
        "gat": gat, "layer": l,
        "w_a": jnp.concatenate([shards[d] for d in range(cut)] + [shards[cut][:, :rem],
                                                                 jnp.zeros((D_MODEL, ZA - N_IN_A), BF16)], axis=1),
        "w_g": jnp.concatenate([shards[cut][:, rem:]] + [shards[d] for d in range(cut + 1, N_DEV)], axis=1),
        "conv_w": small["conv_w"][l],
        "pool_w": small["pool_w"][l], "sgu_w": small["sgu_w"][l],
        "sgu_bias": jnp.broadcast_to(small["sgu_b"][l][:, :, None], (4, CHUNK, GDIM)),
    }
    for name in ("pre_mix_g", "pool_scale", "conv_b", "conv_norm_g", "conv_norm_b", "sgu_norm_g", "sgu_norm_b",
                 "q_norm_g", "kv_norm_g", "post_mix_g", "pre_mlp_g", "post_mlp_g"):
        w[name] = small[name][l][None, :]
    return w


def _late(w, name):
    if name not in w:
        gat, l = w["gat"], w["layer"]
        if name == "proj_cat":
            w[name] = jnp.concatenate([_cols_full(gat[(n, l)]) for n in ("pool_proj", "conv_proj", "sgu_proj", "attn_proj")],
                                      axis=0)
        elif name == "w_uq":
            w[name] = jnp.pad(gat[("w_uq", l)].transpose(1, 0, 2),
                              ((0, 0), (0, 0), (0, QW - QK_NOPE - QK_ROPE))).reshape(BR, MLA_HEADS * QW)
        elif name == "w_ukv":
            w[name] = _cols_full(gat[("w_ukv", l)])
        elif name == "w_up":
            w[name] = gat[("w_up", l)]
        else:
            arr = gat[(name, l)]
            w[name] = arr.reshape(N_DEV * arr.shape[1], arr.shape[2])
    return w[name]


def _layer_fwd(x, h1, w, tabs, l, plan):
    ck, sk, cq, sq = tabs
    n = f"l{l}_"
    za = _mm(h1, w["w_a"], mode="nn", name=n + "za", plan=plan)
    zg = _mm(h1, w["w_g"], mode="nn", name=n + "zg", out_dtypes=(BF16,), plan=plan)
    a_pool = _pool_fwd(za, w["pool_w"], w["pool_scale"], n + "pool")
    yc, a_conv = _conv_fwd(za, w["conv_w"], w["conv_b"], w["conv_norm_g"], w["conv_norm_b"], n + "conv")
    a_sgu = _sgu_fwd(za, w["sgu_norm_g"], w["sgu_norm_b"], w["sgu_w"], w["sgu_bias"], n + "sgu")
    qn, kvn, krr = _mla_prep(za, w["q_norm_g"], w["kv_norm_g"], ck, sk, n + "mla_prep")
    q = _mm(qn, _late(w, "w_uq"), mode="nn", name=n + "q", out_dtypes=(BF16,), extras=(cq, sq), tn=QW,
            epilogue=lambda acc, c, sn: ((acc * c + _rot_half(acc, QW, QK_NOPE) * sn) * ATT_SCALE,))
    kv = _mm(kvn, _late(w, "w_ukv"), mode="nn", name=n + "kv", out_dtypes=(BF16,))
    o, lse = _flash_fwd(q, kv, krr, n + "flash", plan=plan)
    act_cat = jnp.concatenate([a_pool, a_conv, a_sgu, o], axis=1)
    y, merged = _proj_merge_fwd(act_cat, _late(w, "proj_cat"), zg, n + "proj", plan=plan)
    m2 = _mm(merged, _late(w, "w_out"), mode="nn", name=n + "out")
    x1, h2 = _post_pre(x, m2, w["post_mix_g"], w["pre_mlp_g"], n + "post_mix")
    up, act = _mm(h2, _late(w, "w_up"), mode="nn", name=n + "up", out_dtypes=(BF16, BF16), plan=plan,
                  mnk=(x.shape[0], D_FF, D_MODEL), tn=UP_SHARD,
                  b_spec=lambda tn, tk: pl.BlockSpec((None, tk, tn), lambda i, j, kk: (j, kk, 0)),
                  epilogue=lambda acc: (acc, jnp.square(jnp.maximum(acc, 0.0))))
    f = _mm(act, _late(w, "w_down"), mode="nn", name=n + "down", plan=plan)
    saved = dict(x=x, h1=h1, za=za, zg=zg, yc=yc, qn=qn, kvn=kvn, krr=krr, q=q, kv=kv, o=o, lse=lse, act_cat=act_cat,
                 y=y, merged=merged, m2=m2, x1=x1, h2=h2, up=up, act=act, f=f)
    return x1, f, saved


def _layer_bwd(dx_out, df, sv, w, tabs, l, prev, plan, rs):
    ck, sk, cq, sq = tabs
    n = f"l{l}_b_"
    g = {}
    own, half = {}, {}

    def to_sibling(kernel, names):
        def done(results):
            for name, t in zip(names, results[0]):
                half[name] = _rs_chip_sum(own[name], t, f"l{l}_chip_sum_{name}")
        plan.at(n + kernel, lambda: [_scatter_sibling([own[name] for name in names])], done)

    def to_chips(kernel, names, prefix=n):
        def done(results):
            for name, t in zip(names, results[0]):
                rs[(name, l)] = (half[name], t)
        plan.at(prefix + kernel, lambda: [_scatter_chips([half[name] for name in names])], done)

    own["w_down"] = _mm(sv["act"], df, mode="tn", name=n + "dw_down", out_dtypes=(BF16,), plan=plan).reshape(
        N_DEV, D_FF // N_DEV, D_MODEL)
    to_sibling("dup", ["w_down"])
    dup = _mm(df, _late(w, "w_down"), mode="nt", name=n + "dup", out_dtypes=(BF16,), extras=(sv["up"],), plan=plan,
              epilogue=lambda acc, up: (acc * 2.0 * jnp.maximum(up.astype(F32), 0.0),))
    to_chips("dw_up", ["w_down"])
    own["w_up"] = _mm(sv["h2"], dup, mode="tn", name=n + "dw_up", tn=UP_SHARD, plan=plan,
                      o_spec=lambda tm, tn: pl.BlockSpec((None, tm, tn), lambda i, j, kk: (j, i, 0)),
                      out_struct=jax.ShapeDtypeStruct((N_DEV, D_MODEL, UP_SHARD), BF16))
    to_sibling("dh2", ["w_up"])
    dh2 = _mm(dup, _late(w, "w_up"), mode="nt", name=n + "dh2", mnk=(dup.shape[0], D_MODEL, D_FF), tk=UP_SHARD,
              plan=plan, b_spec=lambda tn, tk: pl.BlockSpec((None, tn, tk), lambda i, j, kk: (kk, j, 0)))
    dx1, dm2, g["pre_mlp_g"], g["post_mix_g"] = _pre_bwd(dh2, sv["x1"], w["pre_mlp_g"], dx_out, n + "pre_mlp",
                                                           r_prev=sv["m2"], g_post_prev=w["post_mix_g"])
    own["w_out"] = _mm(sv["merged"], dm2, mode="tn", name=n + "dw_out", out_dtypes=(BF16,)).reshape(
        N_DEV, D_MODEL // N_DEV, D_MODEL)
    to_sibling("dmerged", ["w_out"])
    dmerged = _mm(dm2, _late(w, "w_out"), mode="nt", name=n + "dmerged", plan=plan)
    to_chips("merge", ["w_up"])
    dy, dzg = _merge_bwd(dmerged, sv["y"], sv["zg"], n + "merge", plan=plan)
    d_proj = _proj_bwd_w(sv["act_cat"], dy, n + "dw_proj")
    projs = ["pool_proj", "conv_proj", "sgu_proj", "attn_proj"]
    for i, name in enumerate(projs):
        own[name] = _cols_by_owner(d_proj[i * BR:(i + 1) * BR] if i < 3 else d_proj[3 * BR:])
    to_chips("dact", ["w_out"])
    to_sibling("dact", projs)
    dact = _proj_bwd_act(dy, _late(w, "proj_cat"), n + "dact", plan=plan)
    dz_pool, g["pool_w"], g["pool_scale"] = _pool_bwd(dact, sv["za"], w["pool_w"], w["pool_scale"], n + "pool")
    dyc, g["conv_b"], g["conv_norm_g"], g["conv_norm_b"] = _conv_bwd_norm(dact, sv["yc"], w["conv_norm_g"],
                                                                          w["conv_norm_b"], n + "conv_norm")
    dz_conv, g["conv_w"] = _conv_bwd_taps(dyc, sv["za"], w["conv_w"], n + "conv_taps")
    dz_sgu, g["sgu_w"], g["sgu_b"], g["sgu_norm_g"], g["sgu_norm_b"] = _sgu_bwd(
        dact, sv["za"], w["sgu_norm_g"], w["sgu_norm_b"], w["sgu_w"], w["sgu_bias"], n + "sgu")
    to_chips("flash", projs)
    dkv, dkr_heads, dq = _flash_bwd(sv["q"], sv["kv"], sv["krr"], dact, sv["o"], sv["lse"], cq, sq, n + "flash",
                                    plan=plan)
    d_uq = _mm(sv["qn"], dq, mode="tn", name=n + "dw_uq", out_dtypes=(BF16,))
    own["w_uq"] = d_uq.reshape(BR, MLA_HEADS, QW)[:, :, :QK_NOPE + QK_ROPE].transpose(1, 0, 2)
    dqn = _mm(dq, _late(w, "w_uq"), mode="nt", name=n + "dqn")
    own["w_ukv"] = _cols_by_owner(_mm(sv["kvn"], dkv, mode="tn", name=n + "dw_ukv", out_dtypes=(BF16,)))
    to_sibling("dkvn", ["w_uq", "w_ukv"])
    dkvn = _mm(dkv, _late(w, "w_ukv"), mode="nt", name=n + "dkvn", plan=plan)
    dz_mla, g["q_norm_g"], g["kv_norm_g"] = _mla_prep_bwd(dqn, dkvn, dkr_heads, sv["za"], w["q_norm_g"],
                                                           w["kv_norm_g"], ck, sk, n + "mla_prep")
    dza = jnp.concatenate([dz_pool, dz_conv, dz_sgu, dz_mla], axis=1)
    to_chips("dw_a", ["w_uq", "w_ukv"])
    d_a = _mm(sv["h1"], dza, mode="tn", name=n + "dw_a", out_dtypes=(BF16,), plan=plan)
    d_g = _mm(sv["h1"], dzg, mode="tn", name=n + "dw_g", out_dtypes=(BF16,))
    own["w_in"] = _cols_by_owner(jnp.concatenate([d_a[:, :N_IN_A], d_g], axis=1)).transpose(0, 2, 1)
    to_sibling("dh1_a", ["w_in"])
    dh1 = _mm(dza, w["w_a"], mode="nt", name=n + "dh1_a", plan=plan)
    if prev is None:
        to_chips("dh1_g", ["w_in"])
    else:
        to_chips("flash", ["w_in"], prefix=f"l{l - 1}_b_")
    dh1 = _mm(dzg, w["w_g"], mode="nt", name=n + "dh1_g", extras=(dh1,), epilogue=lambda acc, e: (acc + e,), plan=plan)
    if prev is None:
        dx, g["pre_mix_g"] = _pre_bwd(dh1, sv["x"], w["pre_mix_g"], dx1, n + "pre_mix")
        return dx, None, g
    dx, df_prev, g["pre_mix_g"], g_prev_post = _pre_bwd(dh1, sv["x"], w["pre_mix_g"], dx1, n + "pre_mix",
                                                         r_prev=prev[0], g_post_prev=prev[1])
    g["prev_post_mlp_g"] = g_prev_post
    return dx, df_prev, g


MIDSIZE = ("pool_proj", "conv_proj", "sgu_proj", "w_uq", "w_ukv", "attn_proj")
GATHER_STEPS = (
    ("l0_za", tuple((name, 0) for name in MIDSIZE)),
    ("l0_zg", (("w_out", 0),) + tuple((name, 1) for name in MIDSIZE)),
    ("l0_flash", (("w_in", 1), ("w_up", 0))),
    ("l0_proj", (("w_down", 0),)),
    ("l0_up", (("w_up", 1),)),
    ("l0_down", (("w_out", 1),)),
    ("l1_za", ()),
    ("l1_zg", ()),
    ("l1_flash", (("w_down", 1),)),
    ("l1_proj", ()),
)


def _plan_gathers(plan, gat, shards):
    first_half = {}
    for step, (kernel, keys) in enumerate(GATHER_STEPS):
        before = GATHER_STEPS[step - 1][1] if step else ()
        if not keys and not before:
            continue

        def make(keys=keys, before=before):
            ops = [_gather_pass(first_half[before])] if before else []
            return ops + ([_gather_own([shards[key] for key in keys])] if keys else [])

        def done(results, keys=keys, before=before):
            if before:
                gat.update(zip(before, results[0]))
            if keys:
                first_half[keys] = results[-1]

        plan.at(kernel, make, done)


def _local_step(x, positions, target, gat, small, shards, plan):
    tabs = _rope_tables(positions)
    _plan_gathers(plan, gat, shards)
    ws = [_layer_weights(gat, small, 0)]
    saved = []
    h = _pre_norm(x, ws[0]["pre_mix_g"], "l0_pre_mix")
    cur = x
    for l in range(DEPTH):
        x1, f, sv = _layer_fwd(cur, h, ws[l], tabs, l, plan)
        saved.append(sv)
        if l + 1 < DEPTH:
            ws.append(_layer_weights(gat, small, l + 1))
            cur, h = _post_pre(x1, f, ws[l]["post_mlp_g"], ws[l + 1]["pre_mix_g"], f"l{l}_post_mlp")
    top = DEPTH - 1
    dx, df, dg_post, loss = _final_loss(saved[top]["x1"], saved[top]["f"], ws[top]["post_mlp_g"], target, "loss")
    grads = [None] * DEPTH
    post_mlp = {top: dg_post}
    rs = {}
    for l in range(top, -1, -1):
        prev = (saved[l - 1]["f"], ws[l - 1]["post_mlp_g"]) if l > 0 else None
        dx, df, g = _layer_bwd(dx, df, saved[l], ws[l], tabs, l, prev, plan, rs)
        if l > 0:
            post_mlp[l - 1] = g.pop("prev_post_mlp_g")
        grads[l] = g
    for l in range(DEPTH):
        grads[l]["post_mlp_g"] = post_mlp[l]
    assert not plan.jobs, sorted(plan.jobs)
    return loss[0, 0], dx, grads, rs


def _small_grads(grads):
    small = {}

    def stack(fn):
        return jnp.stack([fn(grads[l]) for l in range(DEPTH)])

    for name, shape in SMALL:
        if name == "sgu_b":
            small[name] = stack(lambda g: g["sgu_b"][:, :, 0])
        else:
            small[name] = stack(lambda g, name=name, shape=shape: g[name].reshape(shape))
    small["conv_w"] = stack(lambda g: g["conv_w"][:CONV_WIDTH])
    return small


SMALL_ROWS = sum(DEPTH * math.prod(shape) // 128 for _, shape in SMALL)
CONVW_ROWS = DEPTH * CONV_WIDTH * BR // 128


def _pack_small(parts):
    return jnp.concatenate([parts[name].astype(F32).reshape(-1, 128) for name, _ in SMALL], axis=0)


def _unpack_small(buf):
    out, off = {}, 0
    for name, shape in SMALL:
        rows = DEPTH * math.prod(shape) // 128
        out[name] = buf[off:off + rows].reshape((DEPTH,) + shape)
        off += rows
    return out


def _mesh_pos():
    return lax.axis_index("x"), lax.axis_index("y"), lax.axis_index("c")


def _all_gather(shards, name):
    n = len(shards)

    def body(*refs):
        x_refs, out_refs = refs[:n], refs[n:2 * n]
        send_sems, recv_sems, local_sems = refs[2 * n:]
        x, y, c = _mesh_pos()
        me, sibling = (x, y, c), (x, y, 1 - c)
        chips = [(1 - x, y), (x, 1 - y), (1 - x, 1 - y)]

        def rows(t, px, py, pc):
            return out_refs[t].at[4 * px + 2 * py + pc]

        def copy(t, k, block, to, src=None):
            return pltpu.make_async_remote_copy(
                src_ref=rows(t, *block) if src is None else src, dst_ref=rows(t, *block), send_sem=send_sems.at[t, k],
                recv_sem=recv_sems.at[t, k], device_id=to, device_id_type=MESH)

        mine = [pltpu.make_async_copy(x_refs[t], rows(t, *me), local_sems.at[t]) for t in range(n)]
        for cp in mine:
            cp.start()
        first = []
        for t in range(n):
            first.append(copy(t, 0, me, sibling, src=x_refs[t]))
            first += [copy(t, 1 + j, me, (*chip, c), src=x_refs[t]) for j, chip in enumerate(chips)]
        for cp in first:
            cp.start()
        passed = []
        for t in range(n):
            for j, chip in enumerate(chips):
                copy(t, 1 + j, (*chip, c), me).wait_recv()
                passed.append(copy(t, 4 + j, (*chip, c), sibling))
                passed[-1].start()
        for t in range(n):
            copy(t, 0, sibling, me).wait_recv()
            for j, chip in enumerate(chips):
                copy(t, 4 + j, (*chip, 1 - c), me).wait_recv()
        for cp in first + passed:
            cp.wait_send()
        for cp in mine:
            cp.wait()

    hbm = pl.BlockSpec(memory_space=pl.ANY)
    return pl.pallas_call(
        body, name=name, out_shape=[jax.ShapeDtypeStruct((N_DEV,) + a.shape, a.dtype) for a in shards],
        in_specs=[hbm] * n, out_specs=[hbm] * n,
        scratch_shapes=[pltpu.SemaphoreType.DMA((n, 7)), pltpu.SemaphoreType.DMA((n, 7)),
                        pltpu.SemaphoreType.DMA((n,))])(*shards)


def _shard_tile(r, c_, cap_bytes=3 << 19):
    best = 0
    for t in range(16, r + 1, 16):
        if r % t == 0 and t * c_ * 4 <= cap_bytes:
            best = t
    if best or r * c_ * 4 <= cap_bytes:
        return (best or r), c_
    tc = max(t for t in range(128, c_ + 1, 128) if c_ % t == 0 and r * t * 4 <= cap_bytes)
    return r, tc


def _rs_chip_sum(g, t, name):
    _, r, c_ = g.shape
    tr, tc = _shard_tile(r, c_)
    core = lax.axis_index("c").astype(jnp.int32).reshape(1)

    def body(core_ref, g_ref, t_ref, p_ref):
        p_ref[...] = (g_ref[...].astype(F32) + t_ref[...].astype(F32)).astype(p_ref.dtype)

    return pl.pallas_call(
        body, name=name, out_shape=jax.ShapeDtypeStruct((4, r, c_), g.dtype),
        grid_spec=pltpu.PrefetchScalarGridSpec(
            num_scalar_prefetch=1, grid=(4, r // tr, c_ // tc),
            in_specs=[pl.BlockSpec((1, tr, tc), lambda k, i, j, core_ref: (2 * k + core_ref[0], i, j)),
                      pl.BlockSpec((1, tr, tc), lambda k, i, j, core_ref: (k, i, j))],
            out_specs=pl.BlockSpec((1, tr, tc), lambda k, i, j, core_ref: (k, i, j))),
        compiler_params=_params(("parallel", "parallel", "parallel")))(core, g, t)


def _adamw_math(w, g, m, v):
    m = ADAM_B1 * m + (1.0 - ADAM_B1) * g
    v = ADAM_B2 * v + (1.0 - ADAM_B2) * jnp.square(g)
    m_hat = m / (1.0 - ADAM_B1 ** ADAM_STEP)
    v_hat = v / (1.0 - ADAM_B2 ** ADAM_STEP)
    delta = -ADAM_LR * (m_hat / (jnp.sqrt(v_hat) + ADAM_EPS) + ADAM_WD * w)
    return delta, m, v


def _adamw_big(ps, ts, w, m, v, name):
    _, r, c_ = w.shape
    tr, tc = _shard_tile(r, c_)
    chip = (2 * lax.axis_index("x") + lax.axis_index("y")).astype(jnp.int32).reshape(1)

    def body(chip_ref, p0, t0, p1, t1, w_ref, m_ref, v_ref, g_out, d_out, m_out, v_out):
        def update(p_ref, t_ref):
            g = p_ref[0].astype(F32) + t_ref[0].astype(F32) + t_ref[1].astype(F32) + t_ref[2].astype(F32)
            g_out[0] = g
            d_out[0], m_out[0], v_out[0] = _adamw_math(w_ref[0], g, m_ref[0], v_ref[0])

        pl.when(pl.program_id(0) == 0)(functools.partial(update, p0, t0))
        pl.when(pl.program_id(0) == 1)(functools.partial(update, p1, t1))

    def grad_specs(layer):
        def at(l, i, j):
            return jnp.where(l == layer, i, 0), jnp.where(l == layer, j, 0)
        return [pl.BlockSpec((1, tr, tc), lambda l, i, j, chip_ref: (chip_ref[0], *at(l, i, j))),
                pl.BlockSpec((3, tr, tc), lambda l, i, j, chip_ref: (0, *at(l, i, j)))]

    nat = pl.BlockSpec((1, tr, tc), lambda l, i, j, chip_ref: (l, i, j))
    shape = jax.ShapeDtypeStruct(w.shape, F32)
    return pl.pallas_call(
        body, name=name, out_shape=[shape] * 4,
        grid_spec=pltpu.PrefetchScalarGridSpec(
            num_scalar_prefetch=1, grid=(DEPTH, r // tr, c_ // tc),
            in_specs=grad_specs(0) + grad_specs(1) + [nat, nat, nat], out_specs=[nat] * 4),
        compiler_params=_params(("parallel", "parallel", "parallel"), VMEM_BIG))(
            chip, ps[0], ts[0], ps[1], ts[1], w, m, v)


def _sum_devices(parts, name):
    _, r, c_ = parts.shape

    def body(p_ref, o_ref):
        acc = p_ref[0]
        for d in range(1, N_DEV):
            acc = acc + p_ref[d]
        o_ref[...] = acc

    return pl.pallas_call(body, name=name, out_shape=jax.ShapeDtypeStruct((r, c_), F32),
                          compiler_params=_params(None, VMEM_BIG))(parts)


def _adamw_small(w, g, m, v, name):
    def body(w_ref, g_ref, m_ref, v_ref, d_out, m_out, v_out):
        d_out[...], m_out[...], v_out[...] = _adamw_math(w_ref[...], g_ref[...], m_ref[...], v_ref[...])

    shape = jax.ShapeDtypeStruct(w.shape, F32)
    return pl.pallas_call(body, name=name, out_shape=[shape] * 3)(w, g, m, v)


def kernel(x, positions, pre_mix_g, w_in, pool_w, pool_scale, pool_proj, conv_w, conv_b, conv_norm_g, conv_norm_b, conv_proj, sgu_norm_g, sgu_norm_b, sgu_w, sgu_b, sgu_proj, q_norm_g, w_uq, kv_norm_g, w_ukv, attn_proj, w_out, post_mix_g, pre_mlp_g, w_up, w_down, post_mlp_g, loss_target, m_pre_mix_g, m_w_in, m_pool_w, m_pool_scale, m_pool_proj, m_conv_w, m_conv_b, m_conv_norm_g, m_conv_norm_b, m_conv_proj, m_sgu_norm_g, m_sgu_norm_b, m_sgu_w, m_sgu_b, m_sgu_proj, m_q_norm_g, m_w_uq, m_kv_norm_g, m_w_ukv, m_attn_proj, m_w_out, m_post_mix_g, m_pre_mlp_g, m_w_up, m_w_down, m_post_mlp_g, v_pre_mix_g, v_w_in, v_pool_w, v_pool_scale, v_pool_proj, v_conv_w, v_conv_b, v_conv_norm_g, v_conv_norm_b, v_conv_proj, v_sgu_norm_g, v_sgu_norm_b, v_sgu_w, v_sgu_b, v_sgu_proj, v_q_norm_g, v_w_uq, v_kv_norm_g, v_w_ukv, v_attn_proj, v_w_out, v_post_mix_g, v_pre_mlp_g, v_w_up, v_w_down, v_post_mlp_g):
    args = dict(locals())
    wts = {n: args[n] for n in WEIGHTS}
    mom1 = {n: args["m_" + n] for n in WEIGHTS}
    mom2 = {n: args["v_" + n] for n in WEIGHTS}
    dev = 4 * lax.axis_index("x") + 2 * lax.axis_index("y") + lax.axis_index("c")

    shards = {(name, l): wts[name][l].astype(BF16) for name, _, _ in BIG for l in range(DEPTH)}
    taps = jnp.pad(conv_w.reshape(-1, 128), ((0, 1), (0, 0)))
    gathered = _all_gather([shards[("w_in", 0)], taps], "gather_first")
    gat = {("w_in", 0): gathered[0]}
    taps = gathered[-1][:, :CONV_WIDTH].reshape(N_DEV, DEPTH, CONV_WIDTH, BR // N_DEV)
    small = {n: wts[n] for n, _ in SMALL}
    small["conv_w"] = taps.transpose(1, 2, 0, 3).reshape(DEPTH, CONV_WIDTH, BR)

    loss_part, grad_x, grads, rs = _local_step(x[0], positions, loss_target[0], gat, small, shards, _Plan())
    small_g = _small_grads(grads)
    loss = lax.psum(loss_part, ("x", "y", "c"))

    out = {"grad": {}, "delta": {}, "new_m": {}, "new_v": {}}
    for name, _, _ in BIG:
        flip = (lambda a: a.swapaxes(1, 2)) if name == "w_in" else (lambda a: a)
        res = _adamw_big([rs[(name, l)][0] for l in range(DEPTH)], [rs[(name, l)][1] for l in range(DEPTH)],
                         flip(wts[name]), flip(mom1[name]), flip(mom2[name]), "adamw_" + name)
        res = [flip(buf) for buf in res]
        for key, buf in zip(("grad", "delta", "new_m", "new_v"), res):
            out[key][name] = buf

    part = jnp.concatenate([_pack_small(small_g), small_g["conv_w"].reshape(-1, 128)], axis=0)
    total = _sum_devices(_all_gather([part], "gather_small_grads")[0], "sum_small_grads")
    g_small = total[:SMALL_ROWS]
    d_small, m_small, v_small = _adamw_small(_pack_small(wts), g_small, _pack_small(mom1), _pack_small(mom2), "adamw_small")
    for key, buf in (("grad", g_small), ("delta", d_small), ("new_m", m_small), ("new_v", v_small)):
        out[key].update(_unpack_small(buf))
    g_taps = total[SMALL_ROWS:].reshape(DEPTH, CONV_WIDTH, N_DEV, BR // N_DEV)
    g_taps = lax.dynamic_index_in_dim(g_taps, dev, axis=2, keepdims=False)
    flat = lambda a: a.reshape(-1, 128)
    d_taps, m_taps, v_taps = _adamw_small(flat(conv_w), flat(g_taps), flat(m_conv_w), flat(v_conv_w), "adamw_taps")
    for key, buf in (("grad", g_taps), ("delta", d_taps), ("new_m", m_taps), ("new_v", v_taps)):
        out[key]["conv_w"] = buf.reshape(conv_w.shape)

    return (loss, grad_x[None], *[out["grad"][n] for n in WEIGHTS], *[out["delta"][n] for n in WEIGHTS],
            *[out["new_m"][n] for n in WEIGHTS], *[out["new_v"][n] for n in WEIGHTS])
```

```python
import collections
import functools
import math

import jax
import jax.numpy as jnp
from jax import lax
from jax.experimental import pallas as pl
from jax.experimental.pallas import tpu as pltpu

F32 = jnp.float32
BF16 = jnp.bfloat16

D_MODEL = 2048
DEPTH = 2
EPS = 1e-6
N_BRANCH = 4
D_FF = 4 * D_MODEL
UP_SHARD = D_FF // 8
POOL_WINDOWS = (2, 4, 8, 16)
CONV_WIDTH = 31
CHUNK = 128
MLA_HEADS = 8
QK_NOPE = 128
QK_ROPE = 64
V_DIM = 128
ROPE_THETA = 10000.0
GDIM = 128
BR = 512
N_IN_A = 3648
ZA = 3712
N_GATE = N_BRANCH * D_MODEL
N_IN = N_IN_A + N_GATE
QW = 256
ACT_CAT = 3 * BR + MLA_HEADS * V_DIM
ATT_SCALE = (QK_NOPE + QK_ROPE) ** -0.5
NEG = -1e30

ADAM_LR = 0.001
ADAM_B1 = 0.9
ADAM_B2 = 0.999
ADAM_EPS = 1e-08
ADAM_WD = 0.01
ADAM_STEP = 10

N_DEV = 8
PACK_W = 1024
VMEM_BIG = 48 * 1024 * 1024
MESH = pl.DeviceIdType.MESH

BIG = (
    ("w_in", 1, (2048, 1480)),
    ("pool_proj", 1, (512, 256)),
    ("conv_proj", 1, (512, 256)),
    ("sgu_proj", 1, (512, 256)),
    ("w_uq", 1, (512, 192)),
    ("w_ukv", 1, (512, 256)),
    ("attn_proj", 1, (1024, 256)),
    ("w_out", 0, (256, 2048)),
    ("w_up", 1, (2048, 1024)),
    ("w_down", 0, (1024, 2048)),
)
SMALL = (
    ("pre_mix_g", (2048,)), ("pool_w", (4, 128, 128)), ("pool_scale", (512,)), ("conv_b", (512,)),
    ("conv_norm_g", (512,)), ("conv_norm_b", (512,)), ("sgu_norm_g", (512,)), ("sgu_norm_b", (512,)),
    ("sgu_w", (4, 128, 128)), ("sgu_b", (4, 128)), ("q_norm_g", (512,)), ("kv_norm_g", (512,)),
    ("post_mix_g", (2048,)), ("pre_mlp_g", (2048,)), ("post_mlp_g", (2048,)),
)
WEIGHTS = ("pre_mix_g", "w_in", "pool_w", "pool_scale", "pool_proj", "conv_w", "conv_b", "conv_norm_g", "conv_norm_b",
           "conv_proj", "sgu_norm_g", "sgu_norm_b", "sgu_w", "sgu_b", "sgu_proj", "q_norm_g", "w_uq", "kv_norm_g",
           "w_ukv", "attn_proj", "w_out", "post_mix_g", "pre_mlp_g", "w_up", "w_down", "post_mlp_g")


def _params(sem=None, vmem=None):
    return pltpu.CompilerParams(dimension_semantics=sem, vmem_limit_bytes=vmem)


def _tile(dim, pref):
    if dim <= pref:
        return dim
    best = 0
    for t in range(128, pref + 1, 128):
        if dim % t == 0:
            best = t
    return best if best >= 256 else dim


def _sigmoid(x):
    return 0.5 * jnp.tanh(0.5 * x) + 0.5


def _gelu(x):
    k = math.sqrt(2.0 / math.pi)
    return 0.5 * x * (1.0 + jnp.tanh(k * (x + 0.044715 * x * x * x)))


def _gelu_grad(x):
    k = math.sqrt(2.0 / math.pi)
    t = jnp.tanh(k * (x + 0.044715 * x * x * x))
    return 0.5 * (1.0 + t) + 0.5 * x * (1.0 - t * t) * k * (1.0 + 3.0 * 0.044715 * x * x)


def _rms(x, g):
    r = lax.rsqrt(jnp.mean(x * x, axis=-1, keepdims=True) + EPS)
    return x * r * g


def _rms_bwd(x, g, dy):
    r = lax.rsqrt(jnp.mean(x * x, axis=-1, keepdims=True) + EPS)
    dyg = dy * g
    dx = r * dyg - x * (r * r * r) * jnp.mean(dyg * x, axis=-1, keepdims=True)
    return dx, dy * x * r


def _ln_stats(x):
    mu = jnp.mean(x, axis=-1, keepdims=True)
    xc = x - mu
    r = lax.rsqrt(jnp.mean(xc * xc, axis=-1, keepdims=True) + EPS)
    return xc * r, r


def _ln_bwd(xh, r, g, dy):
    dxh = dy * g
    return r * (dxh - jnp.mean(dxh, axis=-1, keepdims=True) - xh * jnp.mean(dxh * xh, axis=-1, keepdims=True))


def _rot_half(x, width, off):
    n = x.shape[-1]
    lane = lax.broadcasted_iota(jnp.int32, x.shape, x.ndim - 1) % width
    return jnp.where(lane - off < QK_ROPE // 2, -pltpu.roll(x, n - QK_ROPE // 2, x.ndim - 1),
                     pltpu.roll(x, QK_ROPE // 2, x.ndim - 1))


def _colsum_into(ref, val, first):
    s = jnp.sum(val, axis=0, keepdims=True)

    @pl.when(first)
    def _():
        ref[...] = s

    @pl.when(jnp.logical_not(first))
    def _():
        ref[...] += s


Exchange = collections.namedtuple("Exchange", "inputs out_shapes aliases n_pairs n_local build")


class _Plan:
    def __init__(self):
        self.jobs = {}

    def at(self, kernel, make, done):
        self.jobs.setdefault(kernel, []).append((make, done))

    def take(self, kernel):
        return self.jobs.pop(kernel, [])


def _pcall(body, operands, *, name, grid, in_specs, out_specs, out_shape, scratch_shapes=(), sem=None, vmem=None,
           plan=None):
    jobs = plan.take(name) if plan is not None else []
    if not jobs:
        return pl.pallas_call(body, name=name, grid=grid, in_specs=list(in_specs), out_specs=list(out_specs),
                              out_shape=list(out_shape), scratch_shapes=list(scratch_shapes),
                              compiler_params=_params(sem, vmem))(*operands)
    made = [(make(), done) for make, done in jobs]
    comm = [op for ops, _ in made for op in ops]
    n_in, n_out, n_scr = len(in_specs), len(out_shape), len(scratch_shapes)
    c_in = [a for op in comm for a in op.inputs]
    c_out = [s for op in comm for s in op.out_shapes]
    sems, aliases, i_off, o_off = [], {}, n_in, n_out
    for op in comm:
        sems += [pltpu.SemaphoreType.DMA((op.n_pairs,)), pltpu.SemaphoreType.DMA((op.n_pairs,)),
                 pltpu.SemaphoreType.DMA((max(op.n_local, 1),))]
        for src, dst in op.aliases.items():
            aliases[i_off + src] = o_off + dst
        i_off += len(op.inputs)
        o_off += len(op.out_shapes)

    def carrier(*refs):
        ins, cins = refs[:n_in], refs[n_in:n_in + len(c_in)]
        base = n_in + len(c_in)
        outs, couts = refs[base:base + n_out], refs[base + n_out:base + n_out + len(c_out)]
        base += n_out + len(c_out)
        scr, csems = refs[base:base + n_scr], refs[base + n_scr:]
        ids = [pl.program_id(ax) for ax in range(len(grid))]
        first = functools.reduce(jnp.logical_and, [i == 0 for i in ids])
        last = functools.reduce(jnp.logical_and, [i == g - 1 for i, g in zip(ids, grid)])

        def pieces():
            res, ci, co = [], 0, 0
            for k, op in enumerate(comm):
                res.append(op.build(cins[ci:ci + len(op.inputs)], couts[co:co + len(op.out_shapes)],
                                    *csems[3 * k:3 * k + 3]))
                ci += len(op.inputs)
                co += len(op.out_shapes)
            return res

        @pl.when(first)
        def _():
            for sends, _, local in pieces():
                for cp in local + sends:
                    cp.start()

        body(*ins, *outs, *scr)

        @pl.when(last)
        def _():
            for sends, recvs, local in pieces():
                for cp in recvs:
                    cp.wait_recv()
                for cp in sends:
                    cp.wait_send()
                for cp in local:
                    cp.wait()

    hbm = pl.BlockSpec(memory_space=pl.ANY)
    res = pl.pallas_call(
        carrier, name=name, grid=grid, in_specs=list(in_specs) + [hbm] * len(c_in),
        out_specs=list(out_specs) + [hbm] * len(c_out), out_shape=list(out_shape) + c_out,
        scratch_shapes=list(scratch_shapes) + sems, input_output_aliases=aliases,
        compiler_params=_params(("arbitrary",) * len(grid), vmem))(*operands, *c_in)
    pos = n_out
    for ops, done in made:
        results = []
        for op in ops:
            results.append(list(res[pos:pos + len(op.out_shapes)]))
            pos += len(op.out_shapes)
        done(results)
    return list(res[:n_out])


def _block(ref, px, py, pc):
    return ref.at[4 * px + 2 * py + pc]


def _remote(src, dst, send_sems, recv_sems, k, to):
    return pltpu.make_async_remote_copy(src_ref=src, dst_ref=dst, send_sem=send_sems.at[k], recv_sem=recv_sems.at[k],
                                        device_id=to, device_id_type=MESH)


def _gather_own(shards):
    n = len(shards)

    def build(ins, outs, send_sems, recv_sems, local_sems):
        x, y, c = _mesh_pos()
        peers = [(x, y, 1 - c), (1 - x, y, c), (x, 1 - y, c), (1 - x, 1 - y, c)]
        sends, recvs, local = [], [], []
        for t in range(n):
            local.append(pltpu.make_async_copy(ins[t], _block(outs[t], x, y, c), local_sems.at[t]))
            for k, peer in enumerate(peers):
                sends.append(_remote(ins[t], _block(outs[t], x, y, c), send_sems, recv_sems, 4 * t + k, peer))
                recvs.append(_remote(ins[t], _block(outs[t], *peer), send_sems, recv_sems, 4 * t + k, peer))
        return sends, recvs, local

    return Exchange(list(shards), [jax.ShapeDtypeStruct((N_DEV,) + a.shape, a.dtype) for a in shards], {}, 4 * n, n, build)


def _gather_pass(bufs):
    n = len(bufs)

    def build(ins, outs, send_sems, recv_sems, local_sems):
        x, y, c = _mesh_pos()
        chips = [(1 - x, y), (x, 1 - y), (1 - x, 1 - y)]
        sends, recvs = [], []
        for t in range(n):
            for j, chip in enumerate(chips):
                mine, theirs = _block(outs[t], *chip, c), _block(outs[t], *chip, 1 - c)
                sends.append(_remote(mine, mine, send_sems, recv_sems, 3 * t + j, (x, y, 1 - c)))
                recvs.append(_remote(mine, theirs, send_sems, recv_sems, 3 * t + j, (x, y, 1 - c)))
        return sends, recvs, []

    return Exchange(list(bufs), [jax.ShapeDtypeStruct(a.shape, a.dtype) for a in bufs], {t: t for t in range(n)},
                    3 * n, 0, build)


def _scatter_sibling(gs):
    n = len(gs)

    def build(ins, outs, send_sems, recv_sems, local_sems):
        x, y, c = _mesh_pos()
        sends, recvs = [], []
        for t in range(n):
            for k in range(4):
                cp = _remote(ins[t].at[2 * k + 1 - c], outs[t].at[k], send_sems, recv_sems, 4 * t + k, (x, y, 1 - c))
                sends.append(cp)
                recvs.append(cp)
        return sends, recvs, []

    return Exchange(list(gs), [jax.ShapeDtypeStruct((4,) + g.shape[1:], g.dtype) for g in gs], {}, 4 * n, 0, build)


def _scatter_chips(ps):
    n = len(ps)

    def build(ins, outs, send_sems, recv_sems, local_sems):
        x, y, c = _mesh_pos()
        chips = [(1 - x, y), (x, 1 - y), (1 - x, 1 - y)]
        sends, recvs = [], []
        for t in range(n):
            for j, (cx, cy) in enumerate(chips):
                cp = _remote(ins[t].at[2 * cx + cy], outs[t].at[j], send_sems, recv_sems, 3 * t + j, (cx, cy, c))
                sends.append(cp)
                recvs.append(cp)
        return sends, recvs, []

    return Exchange(list(ps), [jax.ShapeDtypeStruct((3,) + p.shape[1:], p.dtype) for p in ps], {}, 3 * n, 0, build)


_DIMS = {"nn": ((1,), (0,)), "nt": ((1,), (1,)), "tn": ((0,), (0,))}


def _mm_call(a, b, *, mode, name, grid, kaxis, nk, a_spec, b_spec, o_specs, out_shape, acc_shape,
             extras=(), e_specs=(), epilogue=None, active=None, plan=None):
    ne, no = len(extras), len(out_shape)

    def body(a_ref, b_ref, *rest):
        e_refs, o_refs, acc_ref = rest[:ne], rest[ne:ne + no], rest[ne + no]
        ids = [pl.program_id(ax) for ax in range(len(grid))]
        k = ids[kaxis]

        def finish(acc):
            outs = (acc,) if epilogue is None else epilogue(acc, *[e[...] for e in e_refs])
            for o_ref, val in zip(o_refs, outs):
                o_ref[...] = val.astype(o_ref.dtype)

        def step():
            prod = lax.dot_general(a_ref[...], b_ref[...], (_DIMS[mode], ((), ())), preferred_element_type=F32)
            if nk == 1:
                finish(prod)
                return

            @pl.when(k == 0)
            def _():
                acc_ref[...] = prod

            @pl.when(k > 0)
            def _():
                acc_ref[...] += prod

        if active is None:
            step()
        else:
            pl.when(active(*ids))(step)
        if nk > 1:
            @pl.when(k == nk - 1)
            def _():
                finish(acc_ref[...])

    sem = tuple("arbitrary" if ax == kaxis else "parallel" for ax in range(len(grid)))
    scratch = pltpu.VMEM(acc_shape if nk > 1 else (8, 128), F32)
    return _pcall(body, (a, b, *extras), name=name, grid=grid, in_specs=[a_spec, b_spec, *e_specs],
                  out_specs=list(o_specs), out_shape=list(out_shape), scratch_shapes=[scratch],
                  sem=sem, vmem=VMEM_BIG, plan=plan)


def _mm(a, b, *, mode, name, out_dtypes=(F32,), extras=(), epilogue=None, tm=1024, tn=1024, tk=2048,
        mnk=None, b_spec=None, o_spec=None, out_struct=None, plan=None):
    if mnk is not None:
        m, n, k = mnk
    elif mode == "nn":
        (m, k), (_, n) = a.shape, b.shape
    elif mode == "nt":
        (m, k), (n, _) = a.shape, b.shape
    else:
        (k, m), (_, n) = a.shape, b.shape
    tm, tn, tk = _tile(m, tm), _tile(n, tn), _tile(k, tk)
    if tn > 2048:
        tm, tk = _tile(m, 512), _tile(k, 512)
    if tk > 2048:
        tm, tn = _tile(m, 512), _tile(n, 512)
    nk = k // tk
    if mode == "tn":
        a_spec = pl.BlockSpec((tk, tm), lambda i, j, kk: (kk, i))
    else:
        a_spec = pl.BlockSpec((tm, tk), lambda i, j, kk: (i, kk))
    if b_spec is not None:
        b_spec = b_spec(tn, tk)
    elif mode == "nt":
        b_spec = pl.BlockSpec((tn, tk), lambda i, j, kk: (j, kk))
    else:
        b_spec = pl.BlockSpec((tk, tn), lambda i, j, kk: (kk, j))

    def e_spec(e):
        if e.shape[1] == tn and n != tn:
            return pl.BlockSpec((tm, tn), lambda i, j, kk: (i, 0))
        return pl.BlockSpec((tm, tn), lambda i, j, kk: (i, j))

    e_specs = [e_spec(e) for e in extras]
    if o_spec is not None:
        o_specs = [o_spec(tm, tn)]
        out_shape = [out_struct]
    else:
        o_specs = [pl.BlockSpec((tm, tn), lambda i, j, kk: (i, j)) for _ in out_dtypes]
        out_shape = [jax.ShapeDtypeStruct((m, n), dt) for dt in out_dtypes]
    outs = _mm_call(a, b, mode=mode, name=name, grid=(m // tm, n // tn, nk), kaxis=2, nk=nk, a_spec=a_spec,
                    b_spec=b_spec, o_specs=o_specs, out_shape=out_shape, acc_shape=(tm, tn), extras=extras,
                    e_specs=e_specs, epilogue=epilogue, plan=plan)
    return outs[0] if len(outs) == 1 else outs


def _branch_of(kb):
    return jnp.minimum(kb, N_BRANCH - 1)


def _proj_merge_fwd(act_cat, proj_cat, zg, name, plan=None):
    s = act_cat.shape[0]
    tm, tn = _tile(s, 1024), 1024
    nj = D_MODEL // tn
    last = N_BRANCH - 1

    def kb(b, k):
        return jnp.where(b < last, b, last + k)

    def body(a_ref, b_ref, z_ref, y_ref, m_ref, acc_ref, sum_ref):
        b, k = pl.program_id(2), pl.program_id(3)

        @pl.when(jnp.logical_or(b == last, k == 0))
        def _():
            prod = jnp.dot(a_ref[...], b_ref[...], preferred_element_type=F32)

            @pl.when(k == 0)
            def _():
                acc_ref[...] = prod

            @pl.when(k > 0)
            def _():
                acc_ref[...] += prod

        @pl.when(k == 1)
        def _():
            y = acc_ref[...]
            y_ref[...] = y.astype(BF16)
            gated = _sigmoid(z_ref[...].astype(F32)) * y

            @pl.when(b == 0)
            def _():
                sum_ref[...] = gated

            @pl.when(b > 0)
            def _():
                sum_ref[...] += gated

            @pl.when(b == last)
            def _():
                m_ref[...] = sum_ref[...].astype(BF16)

    wide = pl.BlockSpec((tm, tn), lambda i, j, b, k: (i, b * nj + j))
    y, merged = _pcall(
        body, (act_cat, proj_cat, zg), name=name, grid=(s // tm, nj, N_BRANCH, 2),
        in_specs=[pl.BlockSpec((tm, BR), lambda i, j, b, k: (i, kb(b, k))),
                  pl.BlockSpec((BR, tn), lambda i, j, b, k: (kb(b, k), j)), wide],
        out_specs=[wide, pl.BlockSpec((tm, tn), lambda i, j, b, k: (i, j))],
        out_shape=[jax.ShapeDtypeStruct((s, N_GATE), BF16), jax.ShapeDtypeStruct((s, D_MODEL), BF16)],
        scratch_shapes=[pltpu.VMEM((tm, tn), F32), pltpu.VMEM((tm, tn), F32)],
        sem=("parallel", "parallel", "arbitrary", "arbitrary"), vmem=VMEM_BIG, plan=plan)
    return y, merged


def _proj_bwd_act(dy, proj_cat, name, plan=None):
    s = dy.shape[0]
    tm, tk = _tile(s, 1024), D_MODEL
    nkk = D_MODEL // tk
    nkb = ACT_CAT // BR
    out = _mm_call(
        dy, proj_cat, mode="nt", name=name, grid=(s // tm, nkb, nkk), kaxis=2, nk=nkk,
        a_spec=pl.BlockSpec((tm, tk), lambda i, kb, k: (i, _branch_of(kb) * nkk + k)),
        b_spec=pl.BlockSpec((BR, tk), lambda i, kb, k: (kb, k)),
        o_specs=[pl.BlockSpec((tm, BR), lambda i, kb, k: (i, kb))],
        out_shape=[jax.ShapeDtypeStruct((s, ACT_CAT), F32)], acc_shape=(tm, BR), plan=plan)
    return out[0]


def _proj_bwd_w(act_cat, dy, name):
    s = dy.shape[0]
    tms, tn = _tile(s, 2048), 1024
    nj = D_MODEL // tn
    nkb = ACT_CAT // BR
    nm = s // tms
    out = _mm_call(
        act_cat, dy, mode="tn", name=name, grid=(nkb, nj, nm), kaxis=2, nk=nm,
        a_spec=pl.BlockSpec((tms, BR), lambda kb, j, m: (m, kb)),
        b_spec=pl.BlockSpec((tms, tn), lambda kb, j, m: (m, _branch_of(kb) * nj + j)),
        o_specs=[pl.BlockSpec((BR, tn), lambda kb, j, m: (kb, j))],
        out_shape=[jax.ShapeDtypeStruct((ACT_CAT, D_MODEL), BF16)], acc_shape=(BR, tn))
    return out[0]


def _row_specs(ts, n_full, n_vec):
    return ([pl.BlockSpec((ts, D_MODEL), lambda i: (i, 0))] * n_full
            + [pl.BlockSpec((1, D_MODEL), lambda i: (0, 0))] * n_vec)


def _pre_norm(x, g, name):
    s = x.shape[0]
    ts = _tile(s, 256)

    def body(x_ref, g_ref, h_ref):
        h_ref[...] = _rms(x_ref[...], g_ref[...]).astype(BF16)

    return pl.pallas_call(body, name=name, grid=(s // ts,), in_specs=_row_specs(ts, 1, 1),
                          out_specs=pl.BlockSpec((ts, D_MODEL), lambda i: (i, 0)),
                          out_shape=jax.ShapeDtypeStruct((s, D_MODEL), BF16), compiler_params=_params(("parallel",)))(x, g)


def _post_pre(x, r, g_post, g_next, name):
    s = x.shape[0]
    ts = _tile(s, 256)

    def body(x_ref, r_ref, gp_ref, gn_ref, xn_ref, h_ref):
        xn = x_ref[...] + _rms(r_ref[...], gp_ref[...])
        xn_ref[...] = xn
        h_ref[...] = _rms(xn, gn_ref[...]).astype(BF16)

    spec = pl.BlockSpec((ts, D_MODEL), lambda i: (i, 0))
    return pl.pallas_call(body, name=name, grid=(s // ts,), in_specs=_row_specs(ts, 2, 2), out_specs=[spec, spec],
                          out_shape=[jax.ShapeDtypeStruct((s, D_MODEL), F32), jax.ShapeDtypeStruct((s, D_MODEL), BF16)],
                          compiler_params=_params(("parallel",)))(x, r, g_post, g_next)


def _final_loss(x, r, g_post, target, name):
    s = x.shape[0]
    ts = _tile(s, 256)

    def body(x_ref, r_ref, gp_ref, t_ref, dy_ref, dr_ref, dg_ref, loss_ref):
        first = pl.program_id(0) == 0
        rv, gp = r_ref[...], gp_ref[...]
        diff = x_ref[...] + _rms(rv, gp) - t_ref[...]
        part = 0.5 * jnp.sum(jnp.mean(diff * diff, axis=-1, keepdims=True), axis=0, keepdims=True)
        dy = diff * (1.0 / D_MODEL)
        dy_ref[...] = dy
        dr, dg_rows = _rms_bwd(rv, gp, dy)
        dr_ref[...] = dr.astype(BF16)
        _colsum_into(dg_ref, dg_rows, first)
        _colsum_into(loss_ref, jnp.broadcast_to(part, (1, 128)), first)

    spec = pl.BlockSpec((ts, D_MODEL), lambda i: (i, 0))
    vec = pl.BlockSpec((1, D_MODEL), lambda i: (0, 0))
    return pl.pallas_call(
        body, name=name, grid=(s // ts,), in_specs=[spec, spec, vec, spec],
        out_specs=[spec, spec, vec, pl.BlockSpec((1, 128), lambda i: (0, 0))],
        out_shape=[jax.ShapeDtypeStruct((s, D_MODEL), F32), jax.ShapeDtypeStruct((s, D_MODEL), BF16),
                   jax.ShapeDtypeStruct((1, D_MODEL), F32), jax.ShapeDtypeStruct((1, 128), F32)],
        compiler_params=_params(("arbitrary",)))(x, r, g_post, target)


def _pre_bwd(dh, x, g_pre, dx_res, name, r_prev=None, g_post_prev=None):
    s = x.shape[0]
    ts = _tile(s, 256)
    chain = r_prev is not None

    def body(*refs):
        if chain:
            dh_ref, x_ref, res_ref, r_ref, g_ref, gp_ref, dx_ref, dr_ref, dg_ref, dgp_ref = refs
        else:
            dh_ref, x_ref, res_ref, g_ref, dx_ref, dg_ref = refs
        first = pl.program_id(0) == 0
        dxn, dg_rows = _rms_bwd(x_ref[...], g_ref[...], dh_ref[...])
        dx = res_ref[...] + dxn
        dx_ref[...] = dx
        _colsum_into(dg_ref, dg_rows, first)
        if chain:
            dr, dgp_rows = _rms_bwd(r_ref[...], gp_ref[...], dx)
            dr_ref[...] = dr.astype(BF16)
            _colsum_into(dgp_ref, dgp_rows, first)

    spec = pl.BlockSpec((ts, D_MODEL), lambda i: (i, 0))
    vec = pl.BlockSpec((1, D_MODEL), lambda i: (0, 0))
    full = jax.ShapeDtypeStruct((s, D_MODEL), F32)
    vshape = jax.ShapeDtypeStruct((1, D_MODEL), F32)
    if chain:
        return pl.pallas_call(
            body, name=name, grid=(s // ts,), in_specs=[spec] * 4 + [vec] * 2, out_specs=[spec, spec, vec, vec],
            out_shape=[full, jax.ShapeDtypeStruct((s, D_MODEL), BF16), vshape, vshape],
            compiler_params=_params(("arbitrary",)))(dh, x, dx_res, r_prev, g_pre, g_post_prev)
    return pl.pallas_call(
        body, name=name, grid=(s // ts,), in_specs=[spec] * 3 + [vec], out_specs=[spec, vec],
        out_shape=[full, vshape], compiler_params=_params(("arbitrary",)))(dh, x, dx_res, g_pre)


def _merge_bwd(dm, y, zg, name, plan=None):
    s = y.shape[0]
    ts, tc = _tile(s, 512), 1024
    nj = D_MODEL // tc

    def body(dm_ref, y_ref, z_ref, dy_ref, dz_ref):
        g = _sigmoid(z_ref[...].astype(F32))
        d = dm_ref[...]
        dy_ref[...] = (d * g).astype(BF16)
        dz_ref[...] = (d * y_ref[...] * g * (1.0 - g)).astype(BF16)

    blk = pl.BlockSpec((ts, tc), lambda i, j, b: (i, b * nj + j))
    shape = jax.ShapeDtypeStruct((s, N_GATE), BF16)
    return _pcall(body, (dm, y, zg), name=name, grid=(s // ts, nj, N_BRANCH),
                  in_specs=[pl.BlockSpec((ts, tc), lambda i, j, b: (i, j)), blk, blk], out_specs=[blk, blk],
                  out_shape=[shape, shape], sem=("parallel", "parallel", "parallel"), vmem=VMEM_BIG, plan=plan)


POOL_HALO = 16


def _pool_windows(ext_ref, ts, first_row):
    outs = []
    t = first_row + lax.broadcasted_iota(jnp.int32, (ts, GDIM), 0)
    for gi, w in enumerate(POOL_WINDOWS):
        cols = pl.ds(gi * GDIM, GDIM)
        acc = ext_ref[pl.ds(POOL_HALO, ts), cols]
        cur = acc
        for k in range(1, w):
            acc = acc + ext_ref[pl.ds(POOL_HALO - k, ts), cols]
        cnt = jnp.minimum(t + 1, w).astype(F32)
        outs.append(acc / cnt - cur)
    return outs


def _pool_fwd(za, pool_w, pool_scale, name):
    s = za.shape[0]
    ts = _tile(s, 512)
    hb = ts // POOL_HALO

    def body(a_ref, halo_ref, w_ref, sc_ref, o_ref, ext_ref):
        i = pl.program_id(0)
        ext_ref[pl.ds(0, POOL_HALO), :] = jnp.where(i > 0, halo_ref[...], 0.0)
        ext_ref[pl.ds(POOL_HALO, ts), :] = a_ref[...]
        pooled = _pool_windows(ext_ref, ts, i * ts)
        for gi in range(len(POOL_WINDOWS)):
            mixed = jnp.dot(pooled[gi].astype(BF16), w_ref[gi].astype(BF16), preferred_element_type=F32)
            o_ref[:, pl.ds(gi * GDIM, GDIM)] = (mixed * sc_ref[:, pl.ds(gi * GDIM, GDIM)]).astype(BF16)

    return pl.pallas_call(
        body, name=name, grid=(s // ts,),
        in_specs=[pl.BlockSpec((ts, BR), lambda i: (i, 0)),
                  pl.BlockSpec((POOL_HALO, BR), lambda i: (jnp.maximum(i * hb - 1, 0), 0)),
                  pl.BlockSpec((4, GDIM, GDIM), lambda i: (0, 0, 0)), pl.BlockSpec((1, BR), lambda i: (0, 0))],
        out_specs=pl.BlockSpec((ts, BR), lambda i: (i, 0)), out_shape=jax.ShapeDtypeStruct((s, BR), BF16),
        scratch_shapes=[pltpu.VMEM((POOL_HALO + ts, BR), F32)], compiler_params=_params(("parallel",)))(
            za, za, pool_w, pool_scale)


def _pool_bwd(dact, za, pool_w, pool_scale, name):
    s = za.shape[0]
    ts = _tile(s, 512)
    hb = ts // POOL_HALO
    n_t = s // ts

    def body(d_ref, dhalo_ref, a_ref, halo_ref, w_ref, sc_ref, dz_ref, dw_ref, dsc_ref, ext_ref, f_ref):
        i = pl.program_id(0)
        first = i == 0
        ext_ref[pl.ds(0, POOL_HALO), :] = jnp.where(i > 0, halo_ref[...], 0.0)
        ext_ref[pl.ds(POOL_HALO, ts), :] = a_ref[...]
        pooled = _pool_windows(ext_ref, ts, i * ts)
        d_tile = d_ref[...]
        d_next = jnp.where(i < n_t - 1, dhalo_ref[...], 0.0)
        t_ext = i * ts + lax.broadcasted_iota(jnp.int32, (ts + POOL_HALO, GDIM), 0)
        dsc = []
        for gi, w in enumerate(POOL_WINDOWS):
            cols = pl.ds(gi * GDIM, GDIM)
            wg = w_ref[gi].astype(BF16)
            sc = sc_ref[:, cols]
            pg = pooled[gi].astype(BF16)
            mixed = jnp.dot(pg, wg, preferred_element_type=F32)
            dsc.append(jnp.sum(d_tile[:, gi * GDIM:(gi + 1) * GDIM] * mixed, axis=0, keepdims=True))
            dmix = jnp.concatenate([d_tile[:, gi * GDIM:(gi + 1) * GDIM], d_next[:, gi * GDIM:(gi + 1) * GDIM]], axis=0) * sc
            dmix = dmix.astype(BF16)
            dwg = lax.dot_general(pg, dmix[:ts], (((0,), (0,)), ((), ())), preferred_element_type=F32)

            @pl.when(first)
            def _():
                dw_ref[gi] = dwg

            @pl.when(jnp.logical_not(first))
            def _():
                dw_ref[gi] += dwg

            dpool = lax.dot_general(dmix, wg, (((1,), (1,)), ((), ())), preferred_element_type=F32)
            f_ref[:, cols] = dpool / jnp.minimum(t_ext + 1, w).astype(F32)
            acc = f_ref[pl.ds(0, ts), cols]
            for k in range(1, w):
                acc = acc + f_ref[pl.ds(k, ts), cols]
            dz_ref[:, cols] = (acc - dpool[:ts]).astype(BF16)
        dsc_all = jnp.concatenate(dsc, axis=1)

        @pl.when(first)
        def _():
            dsc_ref[...] = dsc_all

        @pl.when(jnp.logical_not(first))
        def _():
            dsc_ref[...] += dsc_all

    n_hb = s // POOL_HALO
    return pl.pallas_call(
        body, name=name, grid=(n_t,),
        in_specs=[pl.BlockSpec((ts, BR), lambda i: (i, 0)),
                  pl.BlockSpec((POOL_HALO, BR), lambda i: (jnp.minimum((i + 1) * hb, n_hb - 1), 0)),
                  pl.BlockSpec((ts, BR), lambda i: (i, 0)),
                  pl.BlockSpec((POOL_HALO, BR), lambda i: (jnp.maximum(i * hb - 1, 0), 0)),
                  pl.BlockSpec((4, GDIM, GDIM), lambda i: (0, 0, 0)), pl.BlockSpec((1, BR), lambda i: (0, 0))],
        out_specs=[pl.BlockSpec((ts, BR), lambda i: (i, 0)), pl.BlockSpec((4, GDIM, GDIM), lambda i: (0, 0, 0)),
                   pl.BlockSpec((1, BR), lambda i: (0, 0))],
        out_shape=[jax.ShapeDtypeStruct((s, BR), BF16), jax.ShapeDtypeStruct((4, GDIM, GDIM), F32),
                   jax.ShapeDtypeStruct((1, BR), F32)],
        scratch_shapes=[pltpu.VMEM((POOL_HALO + ts, BR), F32), pltpu.VMEM((ts + POOL_HALO, BR), F32)],
        compiler_params=_params(("arbitrary",)))(dact, dact, za, za, pool_w, pool_scale)


CONV_HALO = 32
CONV_LEAD = CONV_HALO - (CONV_WIDTH - 1)


def _conv_fwd(za, conv_w, conv_b, ng, nb, name):
    s = za.shape[0]
    ts = _tile(s, 512)
    hb = ts // CONV_HALO

    def body(a_ref, g_ref, ah_ref, gh_ref, w_ref, b_ref, ng_ref, nb_ref, yc_ref, act_ref, ext_ref):
        i = pl.program_id(0)
        ext_ref[pl.ds(0, CONV_HALO), :] = jnp.where(i > 0, ah_ref[...] * _sigmoid(gh_ref[...]), 0.0)
        ext_ref[pl.ds(CONV_HALO, ts), :] = a_ref[...] * _sigmoid(g_ref[...])
        acc = jnp.zeros((ts, BR), F32) + b_ref[...]
        for k in range(CONV_WIDTH):
            acc = acc + w_ref[pl.ds(k, 1), :] * ext_ref[pl.ds(CONV_LEAD + k, ts), :]
        yc_ref[...] = acc
        xh, _ = _ln_stats(acc)
        ln = xh * ng_ref[...] + nb_ref[...]
        act_ref[...] = (ln * _sigmoid(ln)).astype(BF16)

    tile = lambda c: pl.BlockSpec((ts, BR), lambda i: (i, c))
    halo = lambda c: pl.BlockSpec((CONV_HALO, BR), lambda i: (jnp.maximum(i * hb - 1, 0), c))
    vec = pl.BlockSpec((1, BR), lambda i: (0, 0))
    return pl.pallas_call(
        body, name=name, grid=(s // ts,),
        in_specs=[tile(1), tile(2), halo(1), halo(2), pl.BlockSpec((CONV_WIDTH, BR), lambda i: (0, 0)), vec, vec, vec],
        out_specs=[pl.BlockSpec((ts, BR), lambda i: (i, 0))] * 2,
        out_shape=[jax.ShapeDtypeStruct((s, BR), F32), jax.ShapeDtypeStruct((s, BR), BF16)],
        scratch_shapes=[pltpu.VMEM((CONV_HALO + ts, BR), F32)], compiler_params=_params(("parallel",)))(
            za, za, za, za, conv_w, conv_b, ng, nb)


def _conv_bwd_norm(dact, yc, ng, nb, name):
    s = yc.shape[0]
    ts = _tile(s, 512)

    def body(d_ref, y_ref, ng_ref, nb_ref, dy_ref, db_ref, dng_ref, dnb_ref):
        first = pl.program_id(0) == 0
        xh, r = _ln_stats(y_ref[...])
        g = ng_ref[...]
        ln = xh * g + nb_ref[...]
        sg = _sigmoid(ln)
        dln = d_ref[...] * sg * (1.0 + ln * (1.0 - sg))
        dy = _ln_bwd(xh, r, g, dln)
        dy_ref[...] = dy
        _colsum_into(db_ref, dy, first)
        _colsum_into(dng_ref, dln * xh, first)
        _colsum_into(dnb_ref, dln, first)

    vec = pl.BlockSpec((1, BR), lambda i: (0, 0))
    vshape = jax.ShapeDtypeStruct((1, BR), F32)
    return pl.pallas_call(
        body, name=name, grid=(s // ts,),
        in_specs=[pl.BlockSpec((ts, BR), lambda i: (i, 1)), pl.BlockSpec((ts, BR), lambda i: (i, 0)), vec, vec],
        out_specs=[pl.BlockSpec((ts, BR), lambda i: (i, 0)), vec, vec, vec],
        out_shape=[jax.ShapeDtypeStruct((s, BR), F32), vshape, vshape, vshape],
        compiler_params=_params(("arbitrary",)))(dact, yc, ng, nb)


def _conv_bwd_taps(dyc, za, conv_w, name):
    s = za.shape[0]
    ts = _tile(s, 512)
    hb = ts // CONV_HALO
    n_t = s // ts
    n_hb = s // CONV_HALO

    def body(d_ref, dh_ref, a_ref, g_ref, ah_ref, gh_ref, w_ref, dz_ref, dw_ref, ext_ref, f_ref):
        i = pl.program_id(0)
        first = i == 0
        a, sg = a_ref[...], _sigmoid(g_ref[...])
        ext_ref[pl.ds(0, CONV_HALO), :] = jnp.where(i > 0, ah_ref[...] * _sigmoid(gh_ref[...]), 0.0)
        ext_ref[pl.ds(CONV_HALO, ts), :] = a * sg
        d = d_ref[...]
        f_ref[pl.ds(0, ts), :] = d
        f_ref[pl.ds(ts, CONV_HALO), :] = jnp.where(i < n_t - 1, dh_ref[...], 0.0)
        dglu = jnp.zeros((ts, BR), F32)
        rows = []
        for k in range(CONV_WIDTH):
            rows.append(jnp.sum(d * ext_ref[pl.ds(CONV_LEAD + k, ts), :], axis=0, keepdims=True))
            dglu = dglu + w_ref[pl.ds(k, 1), :] * f_ref[pl.ds(CONV_WIDTH - 1 - k, ts), :]
        rows.append(jnp.zeros((1, BR), F32))
        dw = jnp.concatenate(rows, axis=0)

        @pl.when(first)
        def _():
            dw_ref[...] = dw

        @pl.when(jnp.logical_not(first))
        def _():
            dw_ref[...] += dw

        dz_ref[:, pl.ds(0, BR)] = (dglu * sg).astype(BF16)
        dz_ref[:, pl.ds(BR, BR)] = (dglu * a * sg * (1.0 - sg)).astype(BF16)

    tile = lambda c: pl.BlockSpec((ts, BR), lambda i: (i, c))
    halo = lambda c: pl.BlockSpec((CONV_HALO, BR), lambda i: (jnp.maximum(i * hb - 1, 0), c))
    return pl.pallas_call(
        body, name=name, grid=(n_t,),
        in_specs=[pl.BlockSpec((ts, BR), lambda i: (i, 0)),
                  pl.BlockSpec((CONV_HALO, BR), lambda i: (jnp.minimum((i + 1) * hb, n_hb - 1), 0)),
                  tile(1), tile(2), halo(1), halo(2), pl.BlockSpec((CONV_WIDTH, BR), lambda i: (0, 0))],
        out_specs=[pl.BlockSpec((ts, 2 * BR), lambda i: (i, 0)), pl.BlockSpec((CONV_WIDTH + 1, BR), lambda i: (0, 0))],
        out_shape=[jax.ShapeDtypeStruct((s, 2 * BR), BF16), jax.ShapeDtypeStruct((CONV_WIDTH + 1, BR), F32)],
        scratch_shapes=[pltpu.VMEM((CONV_HALO + ts, BR), F32), pltpu.VMEM((ts + CONV_HALO, BR), F32)],
        compiler_params=_params(("arbitrary",)))(dyc, dyc, za, za, za, za, conv_w)


def _tril(w):
    r = lax.broadcasted_iota(jnp.int32, (CHUNK, CHUNK), 0)
    c = lax.broadcasted_iota(jnp.int32, (CHUNK, CHUNK), 1)
    return jnp.where(c <= r, w, 0.0)


def _sgu_fwd(za, ng, nb, sgu_w, bias_b, name):
    s = za.shape[0]
    ts = _tile(s, 512)

    def body(u_ref, v_ref, ng_ref, nb_ref, w_ref, b_ref, o_ref):
        u = _gelu(u_ref[...])
        xh, _ = _ln_stats(_gelu(v_ref[...]))
        vln = (xh * ng_ref[...] + nb_ref[...]).astype(BF16)
        for gi in range(4):
            wg = _tril(w_ref[gi]).astype(BF16)
            for n in range(ts // CHUNK):
                blk = vln[n * CHUNK:(n + 1) * CHUNK, gi * GDIM:(gi + 1) * GDIM]
                sp = jnp.dot(wg, blk, preferred_element_type=F32) + b_ref[gi]
                o_ref[pl.ds(n * CHUNK, CHUNK), pl.ds(gi * GDIM, GDIM)] = (
                    u[n * CHUNK:(n + 1) * CHUNK, gi * GDIM:(gi + 1) * GDIM] * sp).astype(BF16)

    vec = pl.BlockSpec((1, BR), lambda i: (0, 0))
    cube = pl.BlockSpec((4, CHUNK, GDIM), lambda i: (0, 0, 0))
    return pl.pallas_call(
        body, name=name, grid=(s // ts,),
        in_specs=[pl.BlockSpec((ts, BR), lambda i: (i, 3)), pl.BlockSpec((ts, BR), lambda i: (i, 4)), vec, vec, cube, cube],
        out_specs=pl.BlockSpec((ts, BR), lambda i: (i, 0)), out_shape=jax.ShapeDtypeStruct((s, BR), BF16),
        compiler_params=_params(("parallel",)))(za, za, ng, nb, sgu_w, bias_b)


def _sgu_bwd(dact, za, ng, nb, sgu_w, bias_b, name):
    s = za.shape[0]
    ts = _tile(s, 512)

    def body(d_ref, u_ref, v_ref, ng_ref, nb_ref, w_ref, b_ref, dz_ref, dw_ref, db_ref, dng_ref, dnb_ref, dv_ref):
        first = pl.program_id(0) == 0
        u_raw, v_raw = u_ref[...], v_ref[...]
        u = _gelu(u_raw)
        xh, r = _ln_stats(_gelu(v_raw))
        g = ng_ref[...]
        vln = (xh * g + nb_ref[...]).astype(BF16)
        d = d_ref[...]
        dsp = d * u
        dsp16 = dsp.astype(BF16)
        for gi in range(4):
            wg = _tril(w_ref[gi]).astype(BF16)
            dwg = jnp.zeros((CHUNK, CHUNK), F32)
            dbg = jnp.zeros((CHUNK, 1), F32)
            for n in range(ts // CHUNK):
                rows, cols = slice(n * CHUNK, (n + 1) * CHUNK), slice(gi * GDIM, (gi + 1) * GDIM)
                blk = vln[rows, cols]
                sp = jnp.dot(wg, blk, preferred_element_type=F32) + b_ref[gi]
                dz_ref[pl.ds(n * CHUNK, CHUNK), pl.ds(gi * GDIM, GDIM)] = (
                    d[rows, cols] * sp * _gelu_grad(u_raw[rows, cols])).astype(BF16)
                dv_ref[pl.ds(n * CHUNK, CHUNK), pl.ds(gi * GDIM, GDIM)] = lax.dot_general(
                    wg, dsp16[rows, cols], (((0,), (0,)), ((), ())), preferred_element_type=F32)
                dwg = dwg + lax.dot_general(dsp16[rows, cols], blk, (((1,), (1,)), ((), ())), preferred_element_type=F32)
                dbg = dbg + jnp.sum(dsp[rows, cols], axis=1, keepdims=True)
            dwg = _tril(dwg)

            @pl.when(first)
            def _():
                dw_ref[gi] = dwg
                db_ref[gi] = dbg

            @pl.when(jnp.logical_not(first))
            def _():
                dw_ref[gi] += dwg
                db_ref[gi] += dbg

        dvln = dv_ref[...]
        dz_ref[:, pl.ds(BR, BR)] = (_ln_bwd(xh, r, g, dvln) * _gelu_grad(v_raw)).astype(BF16)
        _colsum_into(dng_ref, dvln * xh, first)
        _colsum_into(dnb_ref, dvln, first)

    vec = pl.BlockSpec((1, BR), lambda i: (0, 0))
    cube = pl.BlockSpec((4, CHUNK, GDIM), lambda i: (0, 0, 0))
    vshape = jax.ShapeDtypeStruct((1, BR), F32)
    return pl.pallas_call(
        body, name=name, grid=(s // ts,),
        in_specs=[pl.BlockSpec((ts, BR), lambda i: (i, 2)), pl.BlockSpec((ts, BR), lambda i: (i, 3)),
                  pl.BlockSpec((ts, BR), lambda i: (i, 4)), vec, vec, cube, cube],
        out_specs=[pl.BlockSpec((ts, 2 * BR), lambda i: (i, 0)), cube, pl.BlockSpec((4, CHUNK, 1), lambda i: (0, 0, 0)),
                   vec, vec],
        out_shape=[jax.ShapeDtypeStruct((s, 2 * BR), BF16), jax.ShapeDtypeStruct((4, CHUNK, CHUNK), F32),
                   jax.ShapeDtypeStruct((4, CHUNK, 1), F32), vshape, vshape],
        scratch_shapes=[pltpu.VMEM((ts, BR), F32)], compiler_params=_params(("arbitrary",)))(
            dact, za, za, ng, nb, sgu_w, bias_b)


KR_BLOCK = 3584 // 128


def _mla_prep(za, qg, kvg, ck, sk, name):
    s = za.shape[0]
    ts = _tile(s, 512)

    def body(cq_ref, ckv_ref, kr_ref, qg_ref, kvg_ref, c_ref, s_ref, qn_ref, kvn_ref, krr_ref):
        qn_ref[...] = _rms(cq_ref[...], qg_ref[...]).astype(BF16)
        kvn_ref[...] = _rms(ckv_ref[...], kvg_ref[...]).astype(BF16)
        kr = kr_ref[...]
        krr_ref[...] = (kr * c_ref[...] + _rot_half(kr, 128, 0) * s_ref[...]).astype(BF16)

    vec = pl.BlockSpec((1, BR), lambda i: (0, 0))
    tab = pl.BlockSpec((ts, 128), lambda i: (i, 0))
    row = pl.BlockSpec((ts, BR), lambda i: (i, 0))
    return pl.pallas_call(
        body, name=name, grid=(s // ts,),
        in_specs=[pl.BlockSpec((ts, BR), lambda i: (i, 5)), pl.BlockSpec((ts, BR), lambda i: (i, 6)),
                  pl.BlockSpec((ts, 128), lambda i: (i, KR_BLOCK)), vec, vec, tab, tab],
        out_specs=[row, row, tab],
        out_shape=[jax.ShapeDtypeStruct((s, BR), BF16), jax.ShapeDtypeStruct((s, BR), BF16),
                   jax.ShapeDtypeStruct((s, 128), BF16)],
        compiler_params=_params(("parallel",)))(za, za, za, qg, kvg, ck, sk)


def _mla_prep_bwd(dqn, dkvn, dkr_heads, za, qg, kvg, ck, sk, name):
    s = za.shape[0]
    ts = _tile(s, 512)

    def body(dq_ref, dkv_ref, dkr_ref, cq_ref, ckv_ref, qg_ref, kvg_ref, c_ref, s_ref, dz_ref, dqg_ref, dkvg_ref):
        first = pl.program_id(0) == 0
        dcq, rows_q = _rms_bwd(cq_ref[...], qg_ref[...], dq_ref[...])
        dckv, rows_kv = _rms_bwd(ckv_ref[...], kvg_ref[...], dkv_ref[...])
        dz_ref[:, pl.ds(0, BR)] = dcq.astype(BF16)
        dz_ref[:, pl.ds(BR, BR)] = dckv.astype(BF16)
        dk = dkr_ref[:, pl.ds(0, 128)]
        for h in range(1, MLA_HEADS):
            dk = dk + dkr_ref[:, pl.ds(h * 128, 128)]
        dz_ref[:, pl.ds(2 * BR, 128)] = (dk * c_ref[...] - _rot_half(dk, 128, 0) * s_ref[...]).astype(BF16)
        _colsum_into(dqg_ref, rows_q, first)
        _colsum_into(dkvg_ref, rows_kv, first)

    vec = pl.BlockSpec((1, BR), lambda i: (0, 0))
    tab = pl.BlockSpec((ts, 128), lambda i: (i, 0))
    row = pl.BlockSpec((ts, BR), lambda i: (i, 0))
    wide = 2 * BR + 128
    vshape = jax.ShapeDtypeStruct((1, BR), F32)
    return pl.pallas_call(
        body, name=name, grid=(s // ts,),
        in_specs=[row, row, pl.BlockSpec((ts, MLA_HEADS * 128), lambda i: (i, 0)),
                  pl.BlockSpec((ts, BR), lambda i: (i, 5)), pl.BlockSpec((ts, BR), lambda i: (i, 6)), vec, vec, tab, tab],
        out_specs=[pl.BlockSpec((ts, wide), lambda i: (i, 0)), vec, vec],
        out_shape=[jax.ShapeDtypeStruct((s, wide), BF16), vshape, vshape],
        compiler_params=_params(("arbitrary",)))(dqn, dkvn, dkr_heads, za, za, qg, kvg, ck, sk)


def _attn_tiles(s):
    tq, tk = _tile(s, 1024), _tile(s, 512)
    return tq, tk, tq // tk


def _causal(qi, ki, tq, tk):
    row = qi * tq + lax.broadcasted_iota(jnp.int32, (tq, tk), 0)
    col = ki * tk + lax.broadcasted_iota(jnp.int32, (tq, tk), 1)
    return col <= row


def _step_count(t, firsts):
    n = jnp.int32(0)
    for f in firsts[1:]:
        n = n + jnp.where(t >= f, 1, 0)
    return n


def _q_major_pairs(nq, r):
    firsts = [r * qq * (qq + 1) // 2 for qq in range(nq)]

    def pair(t):
        qi = _step_count(t, firsts)
        return qi, t - r * qi * (qi + 1) // 2

    return r * nq * (nq + 1) // 2, pair


def _k_major_pairs(nq, nk, r):
    counts = [nq - kk // r for kk in range(nk)]
    firsts = [sum(counts[:kk]) for kk in range(nk)]

    def pair(t):
        ki = _step_count(t, firsts)
        first = jnp.int32(0)
        for kk in range(1, nk):
            first = first + jnp.where(t >= firsts[kk], counts[kk - 1], 0)
        return ki // r + (t - first), ki

    return sum(counts), pair


FLASH_HEADS = 2


def _flash_fwd(q, kv, krr, name, plan=None):
    s = q.shape[0]
    tq, tk, r = _attn_tiles(s)
    n_pairs, pair = _q_major_pairs(s // tq, r)
    hp = FLASH_HEADS

    def body(q_ref, kv_ref, kr_ref, o_ref, lse_ref, m_sc, l_sc, acc_sc):
        qi, ki = pair(pl.program_id(1))

        @pl.when(ki == 0)
        def _():
            m_sc[...] = jnp.full((hp, tq, 1), NEG, F32)
            l_sc[...] = jnp.zeros((hp, tq, 1), F32)
            acc_sc[...] = jnp.zeros((hp, tq, V_DIM), F32)

        def step(masked):
            for h in range(hp):
                k = jnp.concatenate([kv_ref[:, pl.ds(h * QW, QK_NOPE)], kr_ref[...]], axis=1)
                sc = lax.dot_general(q_ref[:, pl.ds(h * QW, QW)], k, (((1,), (1,)), ((), ())), preferred_element_type=F32)
                if masked:
                    sc = jnp.where(_causal(qi, ki, tq, tk), sc, NEG)
                m_prev = m_sc[h]
                m_new = jnp.maximum(m_prev, jnp.max(sc, axis=1, keepdims=True))
                alpha = jnp.exp(m_prev - m_new)
                p = jnp.exp(sc - m_new)
                l_sc[h] = alpha * l_sc[h] + jnp.sum(p, axis=1, keepdims=True)
                acc_sc[h] = alpha * acc_sc[h] + jnp.dot(p.astype(BF16), kv_ref[:, pl.ds(h * QW + QK_NOPE, V_DIM)],
                                                        preferred_element_type=F32)
                m_sc[h] = m_new

        pl.when(ki < qi * r)(functools.partial(step, False))
        pl.when(ki >= qi * r)(functools.partial(step, True))

        @pl.when(ki == (qi + 1) * r - 1)
        def _():
            for h in range(hp):
                o_ref[:, pl.ds(h * V_DIM, V_DIM)] = (acc_sc[h] / l_sc[h]).astype(BF16)
                lse_ref[:, pl.ds(h * 128, 128)] = jnp.broadcast_to(m_sc[h] + jnp.log(l_sc[h]), (tq, 128))

    out_blk = pl.BlockSpec((tq, hp * 128), lambda g, t: (pair(t)[0], g))
    return _pcall(
        body, (q, kv, krr), name=name, grid=(MLA_HEADS // hp, n_pairs),
        in_specs=[pl.BlockSpec((tq, hp * QW), lambda g, t: (pair(t)[0], g)),
                  pl.BlockSpec((tk, hp * QW), lambda g, t: (pair(t)[1], g)),
                  pl.BlockSpec((tk, 128), lambda g, t: (pair(t)[1], 0))],
        out_specs=[out_blk, out_blk],
        out_shape=[jax.ShapeDtypeStruct((s, MLA_HEADS * V_DIM), BF16), jax.ShapeDtypeStruct((s, MLA_HEADS * 128), F32)],
        scratch_shapes=[pltpu.VMEM((hp, tq, 1), F32), pltpu.VMEM((hp, tq, 1), F32), pltpu.VMEM((hp, tq, V_DIM), F32)],
        sem=("parallel", "arbitrary"), vmem=VMEM_BIG, plan=plan)


DO_BLOCK = 3 * BR // 128


def _flash_probs(q_ref, kn_ref, kr_ref, v_ref, do_ref, o_ref, lse_ref, qi, ki, tq, tk, masked):
    k = jnp.concatenate([kn_ref[...], kr_ref[...]], axis=1)
    q = q_ref[...]
    sc = lax.dot_general(q, k, (((1,), (1,)), ((), ())), preferred_element_type=F32)
    lse = jnp.max(lse_ref[...], axis=1, keepdims=True)
    p = jnp.exp(sc - lse)
    if masked:
        p = jnp.where(_causal(qi, ki, tq, tk), p, 0.0)
    do = do_ref[...]
    delta = jnp.sum(do * o_ref[...].astype(F32), axis=1, keepdims=True)
    do = do.astype(BF16)
    dp = lax.dot_general(do, v_ref[...], (((1,), (1,)), ((), ())), preferred_element_type=F32)
    ds = (p * (dp - delta)).astype(BF16)
    return q, k, p, do, ds


def _flash_bwd(q, kv, krr, dact, o, lse, cq, sq, name, plan=None):
    s = q.shape[0]
    tq, tk, r = _attn_tiles(s)
    nq, nk = s // tq, s // tk
    n_pairs, pair = _k_major_pairs(nq, nk, r)

    def body(q_ref, kn_ref, kr_ref, v_ref, do_ref, o_ref, lse_ref, c_ref, s_ref, dkv_ref, dkr_ref, dq_ref,
             dk_sc, dv_sc, dq_sc):
        qi, ki = pair(pl.program_id(1))
        rows = pl.ds(pl.multiple_of(qi * tq, tq), tq)

        @pl.when(qi == ki // r)
        def _():
            dk_sc[...] = jnp.zeros((tk, QW), F32)
            dv_sc[...] = jnp.zeros((tk, V_DIM), F32)

        @pl.when(ki == 0)
        def _():
            dq_sc[rows, :] = jnp.zeros((tq, QW), F32)

        def step(masked):
            qv, k, p, dov, ds = _flash_probs(q_ref, kn_ref, kr_ref, v_ref, do_ref, o_ref, lse_ref, qi, ki, tq, tk, masked)
            dv_sc[...] += lax.dot_general(p.astype(BF16), dov, (((0,), (0,)), ((), ())), preferred_element_type=F32)
            dk_sc[...] += lax.dot_general(ds, qv, (((0,), (0,)), ((), ())), preferred_element_type=F32)
            dq_sc[rows, :] += jnp.dot(ds, k, preferred_element_type=F32)

        pl.when(ki < qi * r)(functools.partial(step, False))
        pl.when(ki >= qi * r)(functools.partial(step, True))

        @pl.when(qi == nq - 1)
        def _():
            dkv_ref[:, pl.ds(0, 128)] = dk_sc[:, pl.ds(0, 128)].astype(BF16)
            dkv_ref[:, pl.ds(128, 128)] = dv_sc[...].astype(BF16)
            dkr_ref[...] = dk_sc[:, pl.ds(128, 128)]

        @pl.when(ki == (qi + 1) * r - 1)
        def _():
            dq = dq_sc[rows, :] * ATT_SCALE
            dq_ref[...] = (dq * c_ref[...] - _rot_half(dq, QW, QK_NOPE) * s_ref[...]).astype(BF16)

    def next_done(t):
        qi, ki = pair(t)
        return jnp.minimum(ki // r + jnp.where(jnp.logical_and(ki % r == r - 1, qi > ki // r), 1, 0), nq - 1)

    qmap = lambda c: (lambda h, t: (pair(t)[0], c(h)))
    kmap = lambda c: (lambda h, t: (pair(t)[1], c(h)))
    last = lambda c: (lambda h, t: (next_done(t), c(h)))
    return _pcall(
        body, (q, kv, krr, kv, dact, o, lse, cq, sq), name=name, grid=(MLA_HEADS, n_pairs),
        in_specs=[pl.BlockSpec((tq, QW), qmap(lambda h: h)), pl.BlockSpec((tk, 128), kmap(lambda h: 2 * h)),
                  pl.BlockSpec((tk, 128), kmap(lambda h: 0)), pl.BlockSpec((tk, 128), kmap(lambda h: 2 * h + 1)),
                  pl.BlockSpec((tq, 128), qmap(lambda h: DO_BLOCK + h)), pl.BlockSpec((tq, 128), qmap(lambda h: h)),
                  pl.BlockSpec((tq, 128), qmap(lambda h: h)), pl.BlockSpec((tq, QW), last(lambda h: 0)),
                  pl.BlockSpec((tq, QW), last(lambda h: 0))],
        out_specs=[pl.BlockSpec((tk, QW), kmap(lambda h: h)), pl.BlockSpec((tk, 128), kmap(lambda h: h)),
                   pl.BlockSpec((tq, QW), last(lambda h: h))],
        out_shape=[jax.ShapeDtypeStruct((s, MLA_HEADS * QW), BF16), jax.ShapeDtypeStruct((s, MLA_HEADS * 128), F32),
                   jax.ShapeDtypeStruct((s, MLA_HEADS * QW), BF16)],
        scratch_shapes=[pltpu.VMEM((tk, QW), F32), pltpu.VMEM((tk, V_DIM), F32), pltpu.VMEM((s, QW), F32)],
        sem=("parallel", "arbitrary"), vmem=VMEM_BIG, plan=plan)


def _rope_tables(positions):
    inv_freq = ROPE_THETA ** (-jnp.arange(0, QK_ROPE, 2, dtype=F32) / QK_ROPE)
    ang = positions.reshape(-1).astype(F32)[:, None] * inv_freq
    cos, sin = jnp.cos(ang), jnp.sin(ang)
    s = cos.shape[0]
    one, zero = jnp.ones((s, 64), F32), jnp.zeros((s, 64), F32)
    ck = jnp.concatenate([cos, cos, one], axis=1)
    sk = jnp.concatenate([sin, sin, zero], axis=1)
    cq = jnp.concatenate([one, one, ck], axis=1)
    sq = jnp.concatenate([zero, zero, sk], axis=1)
    return ck, sk, cq, sq


def _cols_full(gathered):
    _, r, c = gathered.shape
    return gathered.transpose(1, 0, 2).reshape(r, N_DEV * c)


def _cols_by_owner(full):
    r, n = full.shape
    return full.reshape(r, N_DEV, n // N_DEV).transpose(1, 0, 2)


def _layer_weights(gat, small, l):
    shards = gat[("w_in", l)]
    cut, rem = divmod(N_IN_A, shards.shape[2])
    w = {
        "gat": gat, "layer": l,
        "w_a": jnp.concatenate([shards[d] for d in range(cut)] + [shards[cut][:, :rem],
                                                                 jnp.zeros((D_MODEL, ZA - N_IN_A), BF16)], axis=1),
        "w_g": jnp.concatenate([shards[cut][:, rem:]] + [shards[d] for d in range(cut + 1, N_DEV)], axis=1),
        "conv_w": small["conv_w"][l],
        "pool_w": small["pool_w"][l], "sgu_w": small["sgu_w"][l],
        "sgu_bias": jnp.broadcast_to(small["sgu_b"][l][:, :, None], (4, CHUNK, GDIM)),
    }
    for name in ("pre_mix_g", "pool_scale", "conv_b", "conv_norm_g", "conv_norm_b", "sgu_norm_g", "sgu_norm_b",
                 "q_norm_g", "kv_norm_g", "post_mix_g", "pre_mlp_g", "post_mlp_g"):
        w[name] = small[name][l][None, :]
    return w


def _late(w, name):
    if name not in w:
        gat, l = w["gat"], w["layer"]
        if name == "proj_cat":
            w[name] = jnp.concatenate([_cols_full(gat[(n, l)]) for n in ("pool_proj", "conv_proj", "sgu_proj", "attn_proj")],
                                      axis=0)
        elif name == "w_uq":
            w[name] = jnp.pad(gat[("w_uq", l)].transpose(1, 0, 2),
                              ((0, 0), (0, 0), (0, QW - QK_NOPE - QK_ROPE))).reshape(BR, MLA_HEADS * QW)
        elif name == "w_ukv":
            w[name] = _cols_full(gat[("w_ukv", l)])
        elif name == "w_up":
            w[name] = gat[("w_up", l)]
        else:
            arr = gat[(name, l)]
            w[name] = arr.reshape(N_DEV * arr.shape[1], arr.shape[2])
    return w[name]


def _layer_fwd(x, h1, w, tabs, l, plan):
    ck, sk, cq, sq = tabs
    n = f"l{l}_"
    za = _mm(h1, w["w_a"], mode="nn", name=n + "za", plan=plan)
    zg = _mm(h1, w["w_g"], mode="nn", name=n + "zg", out_dtypes=(BF16,), plan=plan)
    a_pool = _pool_fwd(za, w["pool_w"], w["pool_scale"], n + "pool")
    yc, a_conv = _conv_fwd(za, w["conv_w"], w["conv_b"], w["conv_norm_g"], w["conv_norm_b"], n + "conv")
    a_sgu = _sgu_fwd(za, w["sgu_norm_g"], w["sgu_norm_b"], w["sgu_w"], w["sgu_bias"], n + "sgu")
    qn, kvn, krr = _mla_prep(za, w["q_norm_g"], w["kv_norm_g"], ck, sk, n + "mla_prep")
    q = _mm(qn, _late(w, "w_uq"), mode="nn", name=n + "q", out_dtypes=(BF16,), extras=(cq, sq), tn=QW,
            epilogue=lambda acc, c, sn: ((acc * c + _rot_half(acc, QW, QK_NOPE) * sn) * ATT_SCALE,))
    kv = _mm(kvn, _late(w, "w_ukv"), mode="nn", name=n + "kv", out_dtypes=(BF16,))
    o, lse = _flash_fwd(q, kv, krr, n + "flash", plan=plan)
    act_cat = jnp.concatenate([a_pool, a_conv, a_sgu, o], axis=1)
    y, merged = _proj_merge_fwd(act_cat, _late(w, "proj_cat"), zg, n + "proj", plan=plan)
    m2 = _mm(merged, _late(w, "w_out"), mode="nn", name=n + "out")
    x1, h2 = _post_pre(x, m2, w["post_mix_g"], w["pre_mlp_g"], n + "post_mix")
    up, act = _mm(h2, _late(w, "w_up"), mode="nn", name=n + "up", out_dtypes=(BF16, BF16), plan=plan,
                  mnk=(x.shape[0], D_FF, D_MODEL), tn=UP_SHARD,
                  b_spec=lambda tn, tk: pl.BlockSpec((None, tk, tn), lambda i, j, kk: (j, kk, 0)),
                  epilogue=lambda acc: (acc, jnp.square(jnp.maximum(acc, 0.0))))
    f = _mm(act, _late(w, "w_down"), mode="nn", name=n + "down", plan=plan)
    saved = dict(x=x, h1=h1, za=za, zg=zg, yc=yc, qn=qn, kvn=kvn, krr=krr, q=q, kv=kv, o=o, lse=lse, act_cat=act_cat,
                 y=y, merged=merged, m2=m2, x1=x1, h2=h2, up=up, act=act, f=f)
    return x1, f, saved


def _layer_bwd(dx_out, df, sv, w, tabs, l, prev, plan, rs):
    ck, sk, cq, sq = tabs
    n = f"l{l}_b_"
    g = {}
    own, half = {}, {}

    def to_sibling(kernel, names):
        def done(results):
            for name, t in zip(names, results[0]):
                half[name] = _rs_chip_sum(own[name], t, f"l{l}_chip_sum_{name}")
        plan.at(n + kernel, lambda: [_scatter_sibling([own[name] for name in names])], done)

    def to_chips(kernel, names, prefix=n):
        def done(results):
            for name, t in zip(names, results[0]):
                rs[(name, l)] = (half[name], t)
        plan.at(prefix + kernel, lambda: [_scatter_chips([half[name] for name in names])], done)

    own["w_down"] = _mm(sv["act"], df, mode="tn", name=n + "dw_down", out_dtypes=(BF16,), plan=plan).reshape(
        N_DEV, D_FF // N_DEV, D_MODEL)
    to_sibling("dup", ["w_down"])
    dup = _mm(df, _late(w, "w_down"), mode="nt", name=n + "dup", out_dtypes=(BF16,), extras=(sv["up"],), plan=plan,
              epilogue=lambda acc, up: (acc * 2.0 * jnp.maximum(up.astype(F32), 0.0),))
    to_chips("dw_up", ["w_down"])
    own["w_up"] = _mm(sv["h2"], dup, mode="tn", name=n + "dw_up", tn=UP_SHARD, plan=plan,
                      o_spec=lambda tm, tn: pl.BlockSpec((None, tm, tn), lambda i, j, kk: (j, i, 0)),
                      out_struct=jax.ShapeDtypeStruct((N_DEV, D_MODEL, UP_SHARD), BF16))
    to_sibling("dh2", ["w_up"])
    dh2 = _mm(dup, _late(w, "w_up"), mode="nt", name=n + "dh2", mnk=(dup.shape[0], D_MODEL, D_FF), tk=UP_SHARD,
              plan=plan, b_spec=lambda tn, tk: pl.BlockSpec((None, tn, tk), lambda i, j, kk: (kk, j, 0)))
    dx1, dm2, g["pre_mlp_g"], g["post_mix_g"] = _pre_bwd(dh2, sv["x1"], w["pre_mlp_g"], dx_out, n + "pre_mlp",
                                                           r_prev=sv["m2"], g_post_prev=w["post_mix_g"])
    own["w_out"] = _mm(sv["merged"], dm2, mode="tn", name=n + "dw_out", out_dtypes=(BF16,)).reshape(
        N_DEV, D_MODEL // N_DEV, D_MODEL)
    to_sibling("dmerged", ["w_out"])
    dmerged = _mm(dm2, _late(w, "w_out"), mode="nt", name=n + "dmerged", plan=plan)
    to_chips("merge", ["w_up"])
    dy, dzg = _merge_bwd(dmerged, sv["y"], sv["zg"], n + "merge", plan=plan)
    d_proj = _proj_bwd_w(sv["act_cat"], dy, n + "dw_proj")
    projs = ["pool_proj", "conv_proj", "sgu_proj", "attn_proj"]
    for i, name in enumerate(projs):
        own[name] = _cols_by_owner(d_proj[i * BR:(i + 1) * BR] if i < 3 else d_proj[3 * BR:])
    to_chips("dact", ["w_out"])
    to_sibling("dact", projs)
    dact = _proj_bwd_act(dy, _late(w, "proj_cat"), n + "dact", plan=plan)
    dz_pool, g["pool_w"], g["pool_scale"] = _pool_bwd(dact, sv["za"], w["pool_w"], w["pool_scale"], n + "pool")
    dyc, g["conv_b"], g["conv_norm_g"], g["conv_norm_b"] = _conv_bwd_norm(dact, sv["yc"], w["conv_norm_g"],
                                                                          w["conv_norm_b"], n + "conv_norm")
    dz_conv, g["conv_w"] = _conv_bwd_taps(dyc, sv["za"], w["conv_w"], n + "conv_taps")
    dz_sgu, g["sgu_w"], g["sgu_b"], g["sgu_norm_g"], g["sgu_norm_b"] = _sgu_bwd(
        dact, sv["za"], w["sgu_norm_g"], w["sgu_norm_b"], w["sgu_w"], w["sgu_bias"], n + "sgu")
    to_chips("flash", projs)
    dkv, dkr_heads, dq = _flash_bwd(sv["q"], sv["kv"], sv["krr"], dact, sv["o"], sv["lse"], cq, sq, n + "flash",
                                    plan=plan)
    d_uq = _mm(sv["qn"], dq, mode="tn", name=n + "dw_uq", out_dtypes=(BF16,))
    own["w_uq"] = d_uq.reshape(BR, MLA_HEADS, QW)[:, :, :QK_NOPE + QK_ROPE].transpose(1, 0, 2)
    dqn = _mm(dq, _late(w, "w_uq"), mode="nt", name=n + "dqn")
    own["w_ukv"] = _cols_by_owner(_mm(sv["kvn"], dkv, mode="tn", name=n + "dw_ukv", out_dtypes=(BF16,)))
    to_sibling("dkvn", ["w_uq", "w_ukv"])
    dkvn = _mm(dkv, _late(w, "w_ukv"), mode="nt", name=n + "dkvn", plan=plan)
    dz_mla, g["q_norm_g"], g["kv_norm_g"] = _mla_prep_bwd(dqn, dkvn, dkr_heads, sv["za"], w["q_norm_g"],
                                                           w["kv_norm_g"], ck, sk, n + "mla_prep")
    dza = jnp.concatenate([dz_pool, dz_conv, dz_sgu, dz_mla], axis=1)
    to_chips("dw_a", ["w_uq", "w_ukv"])
    d_a = _mm(sv["h1"], dza, mode="tn", name=n + "dw_a", out_dtypes=(BF16,), plan=plan)
    d_g = _mm(sv["h1"], dzg, mode="tn", name=n + "dw_g", out_dtypes=(BF16,))
    own["w_in"] = _cols_by_owner(jnp.concatenate([d_a[:, :N_IN_A], d_g], axis=1)).transpose(0, 2, 1)
    to_sibling("dh1_a", ["w_in"])
    dh1 = _mm(dza, w["w_a"], mode="nt", name=n + "dh1_a", plan=plan)
    if prev is None:
        to_chips("dh1_g", ["w_in"])
    else:
        to_chips("flash", ["w_in"], prefix=f"l{l - 1}_b_")
    dh1 = _mm(dzg, w["w_g"], mode="nt", name=n + "dh1_g", extras=(dh1,), epilogue=lambda acc, e: (acc + e,), plan=plan)
    if prev is None:
        dx, g["pre_mix_g"] = _pre_bwd(dh1, sv["x"], w["pre_mix_g"], dx1, n + "pre_mix")
        return dx, None, g
    dx, df_prev, g["pre_mix_g"], g_prev_post = _pre_bwd(dh1, sv["x"], w["pre_mix_g"], dx1, n + "pre_mix",
                                                         r_prev=prev[0], g_post_prev=prev[1])
    g["prev_post_mlp_g"] = g_prev_post
    return dx, df_prev, g


MIDSIZE = ("pool_proj", "conv_proj", "sgu_proj", "w_uq", "w_ukv", "attn_proj")
GATHER_STEPS = (
    ("l0_za", tuple((name, 0) for name in MIDSIZE)),
    ("l0_zg", (("w_out", 0),) + tuple((name, 1) for name in MIDSIZE)),
    ("l0_flash", (("w_in", 1), ("w_up", 0))),
    ("l0_proj", (("w_down", 0),)),
    ("l0_up", (("w_up", 1),)),
    ("l0_down", (("w_out", 1),)),
    ("l1_za", ()),
    ("l1_zg", ()),
    ("l1_flash", (("w_down", 1),)),
    ("l1_proj", ()),
)


def _plan_gathers(plan, gat, shards):
    first_half = {}
    for step, (kernel, keys) in enumerate(GATHER_STEPS):
        before = GATHER_STEPS[step - 1][1] if step else ()
        if not keys and not before:
            continue

        def make(keys=keys, before=before):
            ops = [_gather_pass(first_half[before])] if before else []
            return ops + ([_gather_own([shards[key] for key in keys])] if keys else [])

        def done(results, keys=keys, before=before):
            if before:
                gat.update(zip(before, results[0]))
            if keys:
                first_half[keys] = results[-1]

        plan.at(kernel, make, done)


def _local_step(x, positions, target, gat, small, shards, plan):
    tabs = _rope_tables(positions)
    _plan_gathers(plan, gat, shards)
    ws = [_layer_weights(gat, small, 0)]
    saved = []
    h = _pre_norm(x, ws[0]["pre_mix_g"], "l0_pre_mix")
    cur = x
    for l in range(DEPTH):
        x1, f, sv = _layer_fwd(cur, h, ws[l], tabs, l, plan)
        saved.append(sv)
        if l + 1 < DEPTH:
            ws.append(_layer_weights(gat, small, l + 1))
            cur, h = _post_pre(x1, f, ws[l]["post_mlp_g"], ws[l + 1]["pre_mix_g"], f"l{l}_post_mlp")
    top = DEPTH - 1
    dx, df, dg_post, loss = _final_loss(saved[top]["x1"], saved[top]["f"], ws[top]["post_mlp_g"], target, "loss")
    grads = [None] * DEPTH
    post_mlp = {top: dg_post}
    rs = {}
    for l in range(top, -1, -1):
        prev = (saved[l - 1]["f"], ws[l - 1]["post_mlp_g"]) if l > 0 else None
        dx, df, g = _layer_bwd(dx, df, saved[l], ws[l], tabs, l, prev, plan, rs)
        if l > 0:
            post_mlp[l - 1] = g.pop("prev_post_mlp_g")
        grads[l] = g
    for l in range(DEPTH):
        grads[l]["post_mlp_g"] = post_mlp[l]
    assert not plan.jobs, sorted(plan.jobs)
    return loss[0, 0], dx, grads, rs


def _small_grads(grads):
    small = {}

    def stack(fn):
        return jnp.stack([fn(grads[l]) for l in range(DEPTH)])

    for name, shape in SMALL:
        if name == "sgu_b":
            small[name] = stack(lambda g: g["sgu_b"][:, :, 0])
        else:
            small[name] = stack(lambda g, name=name, shape=shape: g[name].reshape(shape))
    small["conv_w"] = stack(lambda g: g["conv_w"][:CONV_WIDTH])
    return small


SMALL_ROWS = sum(DEPTH * math.prod(shape) // 128 for _, shape in SMALL)
CONVW_ROWS = DEPTH * CONV_WIDTH * BR // 128


def _pack_small(parts):
    return jnp.concatenate([parts[name].astype(F32).reshape(-1, 128) for name, _ in SMALL], axis=0)


def _unpack_small(buf):
    out, off = {}, 0
    for name, shape in SMALL:
        rows = DEPTH * math.prod(shape) // 128
        out[name] = buf[off:off + rows].reshape((DEPTH,) + shape)
        off += rows
    return out


def _mesh_pos():
    return lax.axis_index("x"), lax.axis_index("y"), lax.axis_index("c")


def _all_gather(shards, name):
    n = len(shards)

    def body(*refs):
        x_refs, out_refs = refs[:n], refs[n:2 * n]
        send_sems, recv_sems, local_sems = refs[2 * n:]
        x, y, c = _mesh_pos()
        me, sibling = (x, y, c), (x, y, 1 - c)
        chips = [(1 - x, y), (x, 1 - y), (1 - x, 1 - y)]

        def rows(t, px, py, pc):
            return out_refs[t].at[4 * px + 2 * py + pc]

        def copy(t, k, block, to, src=None):
            return pltpu.make_async_remote_copy(
                src_ref=rows(t, *block) if src is None else src, dst_ref=rows(t, *block), send_sem=send_sems.at[t, k],
                recv_sem=recv_sems.at[t, k], device_id=to, device_id_type=MESH)

        mine = [pltpu.make_async_copy(x_refs[t], rows(t, *me), local_sems.at[t]) for t in range(n)]
        for cp in mine:
            cp.start()
        first = []
        for t in range(n):
            first.append(copy(t, 0, me, sibling, src=x_refs[t]))
            first += [copy(t, 1 + j, me, (*chip, c), src=x_refs[t]) for j, chip in enumerate(chips)]
        for cp in first:
            cp.start()
        passed = []
        for t in range(n):
            for j, chip in enumerate(chips):
                copy(t, 1 + j, (*chip, c), me).wait_recv()
                passed.append(copy(t, 4 + j, (*chip, c), sibling))
                passed[-1].start()
        for t in range(n):
            copy(t, 0, sibling, me).wait_recv()
            for j, chip in enumerate(chips):
                copy(t, 4 + j, (*chip, 1 - c), me).wait_recv()
        for cp in first + passed:
            cp.wait_send()
        for cp in mine:
            cp.wait()

    hbm = pl.BlockSpec(memory_space=pl.ANY)
    return pl.pallas_call(
        body, name=name, out_shape=[jax.ShapeDtypeStruct((N_DEV,) + a.shape, a.dtype) for a in shards],
        in_specs=[hbm] * n, out_specs=[hbm] * n,
        scratch_shapes=[pltpu.SemaphoreType.DMA((n, 7)), pltpu.SemaphoreType.DMA((n, 7)),
                        pltpu.SemaphoreType.DMA((n,))])(*shards)


def _shard_tile(r, c_, cap_bytes=3 << 19):
    best = 0
    for t in range(16, r + 1, 16):
        if r % t == 0 and t * c_ * 4 <= cap_bytes:
            best = t
    if best or r * c_ * 4 <= cap_bytes:
        return (best or r), c_
    tc = max(t for t in range(128, c_ + 1, 128) if c_ % t == 0 and r * t * 4 <= cap_bytes)
    return r, tc


def _rs_chip_sum(g, t, name):
    _, r, c_ = g.shape
    tr, tc = _shard_tile(r, c_)
    core = lax.axis_index("c").astype(jnp.int32).reshape(1)

    def body(core_ref, g_ref, t_ref, p_ref):
        p_ref[...] = (g_ref[...].astype(F32) + t_ref[...].astype(F32)).astype(p_ref.dtype)

    return pl.pallas_call(
        body, name=name, out_shape=jax.ShapeDtypeStruct((4, r, c_), g.dtype),
        grid_spec=pltpu.PrefetchScalarGridSpec(
            num_scalar_prefetch=1, grid=(4, r // tr, c_ // tc),
            in_specs=[pl.BlockSpec((1, tr, tc), lambda k, i, j, core_ref: (2 * k + core_ref[0], i, j)),
                      pl.BlockSpec((1, tr, tc), lambda k, i, j, core_ref: (k, i, j))],
            out_specs=pl.BlockSpec((1, tr, tc), lambda k, i, j, core_ref: (k, i, j))),
        compiler_params=_params(("parallel", "parallel", "parallel")))(core, g, t)


def _adamw_math(w, g, m, v):
    m = ADAM_B1 * m + (1.0 - ADAM_B1) * g
    v = ADAM_B2 * v + (1.0 - ADAM_B2) * jnp.square(g)
    m_hat = m / (1.0 - ADAM_B1 ** ADAM_STEP)
    v_hat = v / (1.0 - ADAM_B2 ** ADAM_STEP)
    delta = -ADAM_LR * (m_hat / (jnp.sqrt(v_hat) + ADAM_EPS) + ADAM_WD * w)
    return delta, m, v


def _adamw_big(ps, ts, w, m, v, name):
    _, r, c_ = w.shape
    tr, tc = _shard_tile(r, c_)
    chip = (2 * lax.axis_index("x") + lax.axis_index("y")).astype(jnp.int32).reshape(1)

    def body(chip_ref, p0, t0, p1, t1, w_ref, m_ref, v_ref, g_out, d_out, m_out, v_out):
        def update(p_ref, t_ref):
            g = p_ref[0].astype(F32) + t_ref[0].astype(F32) + t_ref[1].astype(F32) + t_ref[2].astype(F32)
            g_out[0] = g
            d_out[0], m_out[0], v_out[0] = _adamw_math(w_ref[0], g, m_ref[0], v_ref[0])

        pl.when(pl.program_id(0) == 0)(functools.partial(update, p0, t0))
        pl.when(pl.program_id(0) == 1)(functools.partial(update, p1, t1))

    def grad_specs(layer):
        def at(l, i, j):
            return jnp.where(l == layer, i, 0), jnp.where(l == layer, j, 0)
        return [pl.BlockSpec((1, tr, tc), lambda l, i, j, chip_ref: (chip_ref[0], *at(l, i, j))),
                pl.BlockSpec((3, tr, tc), lambda l, i, j, chip_ref: (0, *at(l, i, j)))]

    nat = pl.BlockSpec((1, tr, tc), lambda l, i, j, chip_ref: (l, i, j))
    shape = jax.ShapeDtypeStruct(w.shape, F32)
    return pl.pallas_call(
        body, name=name, out_shape=[shape] * 4,
        grid_spec=pltpu.PrefetchScalarGridSpec(
            num_scalar_prefetch=1, grid=(DEPTH, r // tr, c_ // tc),
            in_specs=grad_specs(0) + grad_specs(1) + [nat, nat, nat], out_specs=[nat] * 4),
        compiler_params=_params(("parallel", "parallel", "parallel"), VMEM_BIG))(
            chip, ps[0], ts[0], ps[1], ts[1], w, m, v)


def _sum_devices(parts, name):
    _, r, c_ = parts.shape

    def body(p_ref, o_ref):
        acc = p_ref[0]
        for d in range(1, N_DEV):
            acc = acc + p_ref[d]
        o_ref[...] = acc

    return pl.pallas_call(body, name=name, out_shape=jax.ShapeDtypeStruct((r, c_), F32),
                          compiler_params=_params(None, VMEM_BIG))(parts)


def _adamw_small(w, g, m, v, name):
    def body(w_ref, g_ref, m_ref, v_ref, d_out, m_out, v_out):
        d_out[...], m_out[...], v_out[...] = _adamw_math(w_ref[...], g_ref[...], m_ref[...], v_ref[...])

    shape = jax.ShapeDtypeStruct(w.shape, F32)
    return pl.pallas_call(body, name=name, out_shape=[shape] * 3)(w, g, m, v)


def kernel(x, positions, pre_mix_g, w_in, pool_w, pool_scale, pool_proj, conv_w, conv_b, conv_norm_g, conv_norm_b, conv_proj, sgu_norm_g, sgu_norm_b, sgu_w, sgu_b, sgu_proj, q_norm_g, w_uq, kv_norm_g, w_ukv, attn_proj, w_out, post_mix_g, pre_mlp_g, w_up, w_down, post_mlp_g, loss_target, m_pre_mix_g, m_w_in, m_pool_w, m_pool_scale, m_pool_proj, m_conv_w, m_conv_b, m_conv_norm_g, m_conv_norm_b, m_conv_proj, m_sgu_norm_g, m_sgu_norm_b, m_sgu_w, m_sgu_b, m_sgu_proj, m_q_norm_g, m_w_uq, m_kv_norm_g, m_w_ukv, m_attn_proj, m_w_out, m_post_mix_g, m_pre_mlp_g, m_w_up, m_w_down, m_post_mlp_g, v_pre_mix_g, v_w_in, v_pool_w, v_pool_scale, v_pool_proj, v_conv_w, v_conv_b, v_conv_norm_g, v_conv_norm_b, v_conv_proj, v_sgu_norm_g, v_sgu_norm_b, v_sgu_w, v_sgu_b, v_sgu_proj, v_q_norm_g, v_w_uq, v_kv_norm_g, v_w_ukv, v_attn_proj, v_w_out, v_post_mix_g, v_pre_mlp_g, v_w_up, v_w_down, v_post_mlp_g):
    args = dict(locals())
    wts = {n: args[n] for n in WEIGHTS}
    mom1 = {n: args["m_" + n] for n in WEIGHTS}
    mom2 = {n: args["v_" + n] for n in WEIGHTS}
    dev = 4 * lax.axis_index("x") + 2 * lax.axis_index("y") + lax.axis_index("c")

    shards = {(name, l): wts[name][l].astype(BF16) for name, _, _ in BIG for l in range(DEPTH)}
    taps = jnp.pad(conv_w.reshape(-1, 128), ((0, 1), (0, 0)))
    gathered = _all_gather([shards[("w_in", 0)], taps], "gather_first")
    gat = {("w_in", 0): gathered[0]}
    taps = gathered[-1][:, :CONV_WIDTH].reshape(N_DEV, DEPTH, CONV_WIDTH, BR // N_DEV)
    small = {n: wts[n] for n, _ in SMALL}
    small["conv_w"] = taps.transpose(1, 2, 0, 3).reshape(DEPTH, CONV_WIDTH, BR)

    loss_part, grad_x, grads, rs = _local_step(x[0], positions, loss_target[0], gat, small, shards, _Plan())
    small_g = _small_grads(grads)
    loss = lax.psum(loss_part, ("x", "y", "c"))

    out = {"grad": {}, "delta": {}, "new_m": {}, "new_v": {}}
    for name, _, _ in BIG:
        flip = (lambda a: a.swapaxes(1, 2)) if name == "w_in" else (lambda a: a)
        res = _adamw_big([rs[(name, l)][0] for l in range(DEPTH)], [rs[(name, l)][1] for l in range(DEPTH)],
                         flip(wts[name]), flip(mom1[name]), flip(mom2[name]), "adamw_" + name)
        res = [flip(buf) for buf in res]
        for key, buf in zip(("grad", "delta", "new_m", "new_v"), res):
            out[key][name] = buf

    part = jnp.concatenate([_pack_small(small_g), small_g["conv_w"].reshape(-1, 128)], axis=0)
    total = _sum_devices(_all_gather([part], "gather_small_grads")[0], "sum_small_grads")
    g_small = total[:SMALL_ROWS]
    d_small, m_small, v_small = _adamw_small(_pack_small(wts), g_small, _pack_small(mom1), _pack_small(mom2), "adamw_small")
    for key, buf in (("grad", g_small), ("delta", d_small), ("new_m", m_small), ("new_v", v_small)):
        out[key].update(_unpack_small(buf))
    g_taps = total[SMALL_ROWS:].reshape(DEPTH, CONV_WIDTH, N_DEV, BR // N_DEV)
    g_taps = lax.dynamic_index_in_dim(g_taps, dev, axis=2, keepdims=False)
    flat = lambda a: a.reshape(-1, 128)
    d_taps, m_taps, v_taps = _adamw_small(flat(conv_w), flat(g_taps), flat(m_conv_w), flat(v_conv_w), "adamw_taps")
    for key, buf in (("grad", g_taps), ("delta", d_taps), ("new_m", m_taps), ("new_v", v_taps)):
        out[key]["conv_w"] = buf.reshape(conv_w.shape)

    return (loss, grad_x[None], *[out["grad"][n] for n in WEIGHTS], *[out["delta"][n] for n in WEIGHTS],
            *[out["new_m"][n] for n in WEIGHTS], *[out["new_v"][n] for n in WEIGHTS])
```

```python
import collections
import functools
import math

import jax
import jax.numpy as jnp
from jax import lax
from jax.experimental import pallas as pl
from jax.experimental.pallas import tpu as pltpu

F32 = jnp.float32
BF16 = jnp.bfloat16

D_MODEL = 2048
DEPTH = 2
EPS = 1e-6
N_BRANCH = 4
D_FF = 4 * D_MODEL
UP_SHARD = D_FF // 8
POOL_WINDOWS = (2, 4, 8, 16)
CONV_WIDTH = 31
CHUNK = 128
MLA_HEADS = 8
QK_NOPE = 128
QK_ROPE = 64
V_DIM = 128
ROPE_THETA = 10000.0
GDIM = 128
BR = 512
N_IN_A = 3648
ZA = 3712
N_GATE = N_BRANCH * D_MODEL
N_IN = N_IN_A + N_GATE
QW = 256
ACT_CAT = 3 * BR + MLA_HEADS * V_DIM
ATT_SCALE = (QK_NOPE + QK_ROPE) ** -0.5
NEG = -1e30

ADAM_LR = 0.001
ADAM_B1 = 0.9
ADAM_B2 = 0.999
ADAM_EPS = 1e-08
ADAM_WD = 0.01
ADAM_STEP = 10

N_DEV = 8
PACK_W = 1024
VMEM_BIG = 48 * 1024 * 1024
MESH = pl.DeviceIdType.MESH

BIG = (
    ("w_in", 1, (2048, 1480)),
    ("pool_proj", 1, (512, 256)),
    ("conv_proj", 1, (512, 256)),
    ("sgu_proj", 1, (512, 256)),
    ("w_uq", 1, (512, 192)),
    ("w_ukv", 1, (512, 256)),
    ("attn_proj", 1, (1024, 256)),
    ("w_out", 0, (256, 2048)),
    ("w_up", 1, (2048, 1024)),
    ("w_down", 0, (1024, 2048)),
)
SMALL = (
    ("pre_mix_g", (2048,)), ("pool_w", (4, 128, 128)), ("pool_scale", (512,)), ("conv_b", (512,)),
    ("conv_norm_g", (512,)), ("conv_norm_b", (512,)), ("sgu_norm_g", (512,)), ("sgu_norm_b", (512,)),
    ("sgu_w", (4, 128, 128)), ("sgu_b", (4, 128)), ("q_norm_g", (512,)), ("kv_norm_g", (512,)),
    ("post_mix_g", (2048,)), ("pre_mlp_g", (2048,)), ("post_mlp_g", (2048,)),
)
WEIGHTS = ("pre_mix_g", "w_in", "pool_w", "pool_scale", "pool_proj", "conv_w", "conv_b", "conv_norm_g", "conv_norm_b",
           "conv_proj", "sgu_norm_g", "sgu_norm_b", "sgu_w", "sgu_b", "sgu_proj", "q_norm_g", "w_uq", "kv_norm_g",
           "w_ukv", "attn_proj", "w_out", "post_mix_g", "pre_mlp_g", "w_up", "w_down", "post_mlp_g")


def _params(sem=None, vmem=None):
    return pltpu.CompilerParams(dimension_semantics=sem, vmem_limit_bytes=vmem)


def _tile(dim, pref):
    if dim <= pref:
        return dim
    best = 0
    for t in range(128, pref + 1, 128):
        if dim % t == 0:
            best = t
    return best if best >= 256 else dim


def _sigmoid(x):
    return 1.0 / (1.0 + jnp.exp(-x))


def _gelu(x):
    k = math.sqrt(2.0 / math.pi)
    return 0.5 * x * (1.0 + jnp.tanh(k * (x + 0.044715 * x * x * x)))


def _gelu_grad(x):
    k = math.sqrt(2.0 / math.pi)
    t = jnp.tanh(k * (x + 0.044715 * x * x * x))
    return 0.5 * (1.0 + t) + 0.5 * x * (1.0 - t * t) * k * (1.0 + 3.0 * 0.044715 * x * x)


def _rms(x, g):
    r = lax.rsqrt(jnp.mean(x * x, axis=-1, keepdims=True) + EPS)
    return x * r * g


def _rms_bwd(x, g, dy):
    r = lax.rsqrt(jnp.mean(x * x, axis=-1, keepdims=True) + EPS)
    dyg = dy * g
    dx = r * dyg - x * (r * r * r) * jnp.mean(dyg * x, axis=-1, keepdims=True)
    return dx, dy * x * r


def _ln_stats(x):
    mu = jnp.mean(x, axis=-1, keepdims=True)
    xc = x - mu
    r = lax.rsqrt(jnp.mean(xc * xc, axis=-1, keepdims=True) + EPS)
    return xc * r, r


def _ln_bwd(xh, r, g, dy):
    dxh = dy * g
    return r * (dxh - jnp.mean(dxh, axis=-1, keepdims=True) - xh * jnp.mean(dxh * xh, axis=-1, keepdims=True))


def _rot_half(x, width, off):
    n = x.shape[-1]
    lane = lax.broadcasted_iota(jnp.int32, x.shape, x.ndim - 1) % width
    return jnp.where(lane - off < QK_ROPE // 2, -pltpu.roll(x, n - QK_ROPE // 2, x.ndim - 1),
                     pltpu.roll(x, QK_ROPE // 2, x.ndim - 1))


def _colsum_into(ref, val, first):
    s = jnp.sum(val, axis=0, keepdims=True)

    @pl.when(first)
    def _():
        ref[...] = s

    @pl.when(jnp.logical_not(first))
    def _():
        ref[...] += s


Exchange = collections.namedtuple("Exchange", "inputs out_shapes aliases n_pairs n_local build")


class _Plan:
    def __init__(self):
        self.jobs = {}

    def at(self, kernel, make, done):
        self.jobs.setdefault(kernel, []).append((make, done))

    def take(self, kernel):
        return self.jobs.pop(kernel, [])


def _pcall(body, operands, *, name, grid, in_specs, out_specs, out_shape, scratch_shapes=(), sem=None, vmem=None,
           plan=None):
    jobs = plan.take(name) if plan is not None else []
    if not jobs:
        return pl.pallas_call(body, name=name, grid=grid, in_specs=list(in_specs), out_specs=list(out_specs),
                              out_shape=list(out_shape), scratch_shapes=list(scratch_shapes),
                              compiler_params=_params(sem, vmem))(*operands)
    made = [(make(), done) for make, done in jobs]
    comm = [op for ops, _ in made for op in ops]
    n_in, n_out, n_scr = len(in_specs), len(out_shape), len(scratch_shapes)
    c_in = [a for op in comm for a in op.inputs]
    c_out = [s for op in comm for s in op.out_shapes]
    sems, aliases, i_off, o_off = [], {}, n_in, n_out
    for op in comm:
        sems += [pltpu.SemaphoreType.DMA((op.n_pairs,)), pltpu.SemaphoreType.DMA((op.n_pairs,)),
                 pltpu.SemaphoreType.DMA((max(op.n_local, 1),))]
        for src, dst in op.aliases.items():
            aliases[i_off + src] = o_off + dst
        i_off += len(op.inputs)
        o_off += len(op.out_shapes)

    def carrier(*refs):
        ins, cins = refs[:n_in], refs[n_in:n_in + len(c_in)]
        base = n_in + len(c_in)
        outs, couts = refs[base:base + n_out], refs[base + n_out:base + n_out + len(c_out)]
        base += n_out + len(c_out)
        scr, csems = refs[base:base + n_scr], refs[base + n_scr:]
        ids = [pl.program_id(ax) for ax in range(len(grid))]
        first = functools.reduce(jnp.logical_and, [i == 0 for i in ids])
        last = functools.reduce(jnp.logical_and, [i == g - 1 for i, g in zip(ids, grid)])

        def pieces():
            res, ci, co = [], 0, 0
            for k, op in enumerate(comm):
                res.append(op.build(cins[ci:ci + len(op.inputs)], couts[co:co + len(op.out_shapes)],
                                    *csems[3 * k:3 * k + 3]))
                ci += len(op.inputs)
                co += len(op.out_shapes)
            return res

        @pl.when(first)
        def _():
            for sends, _, local in pieces():
                for cp in local + sends:
                    cp.start()

        body(*ins, *outs, *scr)

        @pl.when(last)
        def _():
            for sends, recvs, local in pieces():
                for cp in recvs:
                    cp.wait_recv()
                for cp in sends:
                    cp.wait_send()
                for cp in local:
                    cp.wait()

    hbm = pl.BlockSpec(memory_space=pl.ANY)
    res = pl.pallas_call(
        carrier, name=name, grid=grid, in_specs=list(in_specs) + [hbm] * len(c_in),
        out_specs=list(out_specs) + [hbm] * len(c_out), out_shape=list(out_shape) + c_out,
        scratch_shapes=list(scratch_shapes) + sems, input_output_aliases=aliases,
        compiler_params=_params(("arbitrary",) * len(grid), vmem))(*operands, *c_in)
    pos = n_out
    for ops, done in made:
        results = []
        for op in ops:
            results.append(list(res[pos:pos + len(op.out_shapes)]))
            pos += len(op.out_shapes)
        done(results)
    return list(res[:n_out])


def _block(ref, px, py, pc):
    return ref.at[4 * px + 2 * py + pc]


def _remote(src, dst, send_sems, recv_sems, k, to):
    return pltpu.make_async_remote_copy(src_ref=src, dst_ref=dst, send_sem=send_sems.at[k], recv_sem=recv_sems.at[k],
                                        device_id=to, device_id_type=MESH)


def _gather_own(shards):
    n = len(shards)

    def build(ins, outs, send_sems, recv_sems, local_sems):
        x, y, c = _mesh_pos()
        peers = [(x, y, 1 - c), (1 - x, y, c), (x, 1 - y, c), (1 - x, 1 - y, c)]
        sends, recvs, local = [], [], []
        for t in range(n):
            local.append(pltpu.make_async_copy(ins[t], _block(outs[t], x, y, c), local_sems.at[t]))
            for k, peer in enumerate(peers):
                sends.append(_remote(ins[t], _block(outs[t], x, y, c), send_sems, recv_sems, 4 * t + k, peer))
                recvs.append(_remote(ins[t], _block(outs[t], *peer), send_sems, recv_sems, 4 * t + k, peer))
        return sends, recvs, local

    return Exchange(list(shards), [jax.ShapeDtypeStruct((N_DEV,) + a.shape, a.dtype) for a in shards], {}, 4 * n, n, build)


def _gather_pass(bufs):
    n = len(bufs)

    def build(ins, outs, send_sems, recv_sems, local_sems):
        x, y, c = _mesh_pos()
        chips = [(1 - x, y), (x, 1 - y), (1 - x, 1 - y)]
        sends, recvs = [], []
        for t in range(n):
            for j, chip in enumerate(chips):
                mine, theirs = _block(outs[t], *chip, c), _block(outs[t], *chip, 1 - c)
                sends.append(_remote(mine, mine, send_sems, recv_sems, 3 * t + j, (x, y, 1 - c)))
                recvs.append(_remote(mine, theirs, send_sems, recv_sems, 3 * t + j, (x, y, 1 - c)))
        return sends, recvs, []

    return Exchange(list(bufs), [jax.ShapeDtypeStruct(a.shape, a.dtype) for a in bufs], {t: t for t in range(n)},
                    3 * n, 0, build)


def _scatter_sibling(gs):
    n = len(gs)

    def build(ins, outs, send_sems, recv_sems, local_sems):
        x, y, c = _mesh_pos()
        sends, recvs = [], []
        for t in range(n):
            for k in range(4):
                cp = _remote(ins[t].at[2 * k + 1 - c], outs[t].at[k], send_sems, recv_sems, 4 * t + k, (x, y, 1 - c))
                sends.append(cp)
                recvs.append(cp)
        return sends, recvs, []

    return Exchange(list(gs), [jax.ShapeDtypeStruct((4,) + g.shape[1:], g.dtype) for g in gs], {}, 4 * n, 0, build)


def _scatter_chips(ps):
    n = len(ps)

    def build(ins, outs, send_sems, recv_sems, local_sems):
        x, y, c = _mesh_pos()
        chips = [(1 - x, y), (x, 1 - y), (1 - x, 1 - y)]
        sends, recvs = [], []
        for t in range(n):
            for j, (cx, cy) in enumerate(chips):
                cp = _remote(ins[t].at[2 * cx + cy], outs[t].at[j], send_sems, recv_sems, 3 * t + j, (cx, cy, c))
                sends.append(cp)
                recvs.append(cp)
        return sends, recvs, []

    return Exchange(list(ps), [jax.ShapeDtypeStruct((3,) + p.shape[1:], p.dtype) for p in ps], {}, 3 * n, 0, build)


_DIMS = {"nn": ((1,), (0,)), "nt": ((1,), (1,)), "tn": ((0,), (0,))}


def _mm_call(a, b, *, mode, name, grid, kaxis, nk, a_spec, b_spec, o_specs, out_shape, acc_shape,
             extras=(), e_specs=(), epilogue=None, active=None, plan=None):
    ne, no = len(extras), len(out_shape)

    def body(a_ref, b_ref, *rest):
        e_refs, o_refs, acc_ref = rest[:ne], rest[ne:ne + no], rest[ne + no]
        ids = [pl.program_id(ax) for ax in range(len(grid))]
        k = ids[kaxis]

        def finish(acc):
            outs = (acc,) if epilogue is None else epilogue(acc, *[e[...] for e in e_refs])
            for o_ref, val in zip(o_refs, outs):
                o_ref[...] = val.astype(o_ref.dtype)

        def step():
            prod = lax.dot_general(a_ref[...], b_ref[...], (_DIMS[mode], ((), ())), preferred_element_type=F32)
            if nk == 1:
                finish(prod)
                return

            @pl.when(k == 0)
            def _():
                acc_ref[...] = prod

            @pl.when(k > 0)
            def _():
                acc_ref[...] += prod

        if active is None:
            step()
        else:
            pl.when(active(*ids))(step)
        if nk > 1:
            @pl.when(k == nk - 1)
            def _():
                finish(acc_ref[...])

    sem = tuple("arbitrary" if ax == kaxis else "parallel" for ax in range(len(grid)))
    scratch = pltpu.VMEM(acc_shape if nk > 1 else (8, 128), F32)
    return _pcall(body, (a, b, *extras), name=name, grid=grid, in_specs=[a_spec, b_spec, *e_specs],
                  out_specs=list(o_specs), out_shape=list(out_shape), scratch_shapes=[scratch],
                  sem=sem, vmem=VMEM_BIG, plan=plan)


def _mm(a, b, *, mode, name, out_dtypes=(F32,), extras=(), epilogue=None, tm=1024, tn=1024, tk=2048,
        mnk=None, b_spec=None, o_spec=None, out_struct=None, plan=None):
    if mnk is not None:
        m, n, k = mnk
    elif mode == "nn":
        (m, k), (_, n) = a.shape, b.shape
    elif mode == "nt":
        (m, k), (n, _) = a.shape, b.shape
    else:
        (k, m), (_, n) = a.shape, b.shape
    if mode == "tn":
        tk = 2 * tk
    tm, tn, tk = _tile(m, tm), _tile(n, tn), _tile(k, tk)
    if tn > 2048:
        tm, tk = _tile(m, 512), _tile(k, 512)
    if tk > 2048 and mode != "tn":
        tm, tn = _tile(m, 512), _tile(n, 512)
    nk = k // tk
    if mode == "tn":
        a_spec = pl.BlockSpec((tk, tm), lambda i, j, kk: (kk, i))
    else:
        a_spec = pl.BlockSpec((tm, tk), lambda i, j, kk: (i, kk))
    if b_spec is not None:
        b_spec = b_spec(tn, tk)
    elif mode == "nt":
        b_spec = pl.BlockSpec((tn, tk), lambda i, j, kk: (j, kk))
    else:
        b_spec = pl.BlockSpec((tk, tn), lambda i, j, kk: (kk, j))

    def e_spec(e):
        if e.shape[1] == tn and n != tn:
            return pl.BlockSpec((tm, tn), lambda i, j, kk: (i, 0))
        return pl.BlockSpec((tm, tn), lambda i, j, kk: (i, j))

    e_specs = [e_spec(e) for e in extras]
    if o_spec is not None:
        o_specs = [o_spec(tm, tn)]
        out_shape = [out_struct]
    else:
        o_specs = [pl.BlockSpec((tm, tn), lambda i, j, kk: (i, j)) for _ in out_dtypes]
        out_shape = [jax.ShapeDtypeStruct((m, n), dt) for dt in out_dtypes]
    outs = _mm_call(a, b, mode=mode, name=name, grid=(m // tm, n // tn, nk), kaxis=2, nk=nk, a_spec=a_spec,
                    b_spec=b_spec, o_specs=o_specs, out_shape=out_shape, acc_shape=(tm, tn), extras=extras,
                    e_specs=e_specs, epilogue=epilogue, plan=plan)
    return outs[0] if len(outs) == 1 else outs


def _branch_of(kb):
    return jnp.minimum(kb, N_BRANCH - 1)


def _proj_merge_fwd(act_cat, proj_cat, zg, name, plan=None):
    s = act_cat.shape[0]
    tm, tn = _tile(s, 1024), 1024
    nj = D_MODEL // tn
    last = N_BRANCH - 1

    def kb(b, k):
        return jnp.where(b < last, b, last + k)

    def body(a_ref, b_ref, z_ref, y_ref, m_ref, acc_ref, sum_ref):
        b, k = pl.program_id(2), pl.program_id(3)

        @pl.when(jnp.logical_or(b == last, k == 0))
        def _():
            prod = jnp.dot(a_ref[...], b_ref[...], preferred_element_type=F32)

            @pl.when(k == 0)
            def _():
                acc_ref[...] = prod

            @pl.when(k > 0)
            def _():
                acc_ref[...] += prod

        @pl.when(k == 1)
        def _():
            y = acc_ref[...]
            y_ref[...] = y.astype(BF16)
            gated = _sigmoid(z_ref[...].astype(F32)) * y

            @pl.when(b == 0)
            def _():
                sum_ref[...] = gated

            @pl.when(b > 0)
            def _():
                sum_ref[...] += gated

            @pl.when(b == last)
            def _():
                m_ref[...] = sum_ref[...].astype(BF16)

    wide = pl.BlockSpec((tm, tn), lambda i, j, b, k: (i, b * nj + j))
    y, merged = _pcall(
        body, (act_cat, proj_cat, zg), name=name, grid=(s // tm, nj, N_BRANCH, 2),
        in_specs=[pl.BlockSpec((tm, BR), lambda i, j, b, k: (i, kb(b, k))),
                  pl.BlockSpec((BR, tn), lambda i, j, b, k: (kb(b, k), j)), wide],
        out_specs=[wide, pl.BlockSpec((tm, tn), lambda i, j, b, k: (i, j))],
        out_shape=[jax.ShapeDtypeStruct((s, N_GATE), BF16), jax.ShapeDtypeStruct((s, D_MODEL), BF16)],
        scratch_shapes=[pltpu.VMEM((tm, tn), F32), pltpu.VMEM((tm, tn), F32)],
        sem=("parallel", "parallel", "arbitrary", "arbitrary"), vmem=VMEM_BIG, plan=plan)
    return y, merged


def _proj_bwd_act(dy, proj_cat, name, plan=None):
    s = dy.shape[0]
    tm, tk = _tile(s, 1024), D_MODEL
    nkk = D_MODEL // tk
    nkb = ACT_CAT // BR
    out = _mm_call(
        dy, proj_cat, mode="nt", name=name, grid=(s // tm, nkb, nkk), kaxis=2, nk=nkk,
        a_spec=pl.BlockSpec((tm, tk), lambda i, kb, k: (i, _branch_of(kb) * nkk + k)),
        b_spec=pl.BlockSpec((BR, tk), lambda i, kb, k: (kb, k)),
        o_specs=[pl.BlockSpec((tm, BR), lambda i, kb, k: (i, kb))],
        out_shape=[jax.ShapeDtypeStruct((s, ACT_CAT), F32)], acc_shape=(tm, BR), plan=plan)
    return out[0]


def _proj_bwd_w(act_cat, dy, name):
    s = dy.shape[0]
    tms, tn = _tile(s, 2048), 1024
    nj = D_MODEL // tn
    nkb = ACT_CAT // BR
    nm = s // tms
    out = _mm_call(
        act_cat, dy, mode="tn", name=name, grid=(nkb, nj, nm), kaxis=2, nk=nm,
        a_spec=pl.BlockSpec((tms, BR), lambda kb, j, m: (m, kb)),
        b_spec=pl.BlockSpec((tms, tn), lambda kb, j, m: (m, _branch_of(kb) * nj + j)),
        o_specs=[pl.BlockSpec((BR, tn), lambda kb, j, m: (kb, j))],
        out_shape=[jax.ShapeDtypeStruct((ACT_CAT, D_MODEL), BF16)], acc_shape=(BR, tn))
    return out[0]


def _row_specs(ts, n_full, n_vec):
    return ([pl.BlockSpec((ts, D_MODEL), lambda i: (i, 0))] * n_full
            + [pl.BlockSpec((1, D_MODEL), lambda i: (0, 0))] * n_vec)


def _pre_norm(x, g, name):
    s = x.shape[0]
    ts = _tile(s, 256)

    def body(x_ref, g_ref, h_ref):
        h_ref[...] = _rms(x_ref[...], g_ref[...]).astype(BF16)

    return pl.pallas_call(body, name=name, grid=(s // ts,), in_specs=_row_specs(ts, 1, 1),
                          out_specs=pl.BlockSpec((ts, D_MODEL), lambda i: (i, 0)),
                          out_shape=jax.ShapeDtypeStruct((s, D_MODEL), BF16), compiler_params=_params(("parallel",)))(x, g)


def _post_pre(x, r, g_post, g_next, name):
    s = x.shape[0]
    ts = _tile(s, 256)

    def body(x_ref, r_ref, gp_ref, gn_ref, xn_ref, h_ref):
        xn = x_ref[...] + _rms(r_ref[...], gp_ref[...])
        xn_ref[...] = xn
        h_ref[...] = _rms(xn, gn_ref[...]).astype(BF16)

    spec = pl.BlockSpec((ts, D_MODEL), lambda i: (i, 0))
    return pl.pallas_call(body, name=name, grid=(s // ts,), in_specs=_row_specs(ts, 2, 2), out_specs=[spec, spec],
                          out_shape=[jax.ShapeDtypeStruct((s, D_MODEL), F32), jax.ShapeDtypeStruct((s, D_MODEL), BF16)],
                          compiler_params=_params(("parallel",)))(x, r, g_post, g_next)


def _final_loss(x, r, g_post, target, name):
    s = x.shape[0]
    ts = _tile(s, 256)

    def body(x_ref, r_ref, gp_ref, t_ref, dy_ref, dr_ref, dg_ref, loss_ref):
        first = pl.program_id(0) == 0
        rv, gp = r_ref[...], gp_ref[...]
        diff = x_ref[...] + _rms(rv, gp) - t_ref[...]
        part = 0.5 * jnp.sum(jnp.mean(diff * diff, axis=-1, keepdims=True), axis=0, keepdims=True)
        dy = diff * (1.0 / D_MODEL)
        dy_ref[...] = dy
        dr, dg_rows = _rms_bwd(rv, gp, dy)
        dr_ref[...] = dr.astype(BF16)
        _colsum_into(dg_ref, dg_rows, first)
        _colsum_into(loss_ref, jnp.broadcast_to(part, (1, 128)), first)

    spec = pl.BlockSpec((ts, D_MODEL), lambda i: (i, 0))
    vec = pl.BlockSpec((1, D_MODEL), lambda i: (0, 0))
    return pl.pallas_call(
        body, name=name, grid=(s // ts,), in_specs=[spec, spec, vec, spec],
        out_specs=[spec, spec, vec, pl.BlockSpec((1, 128), lambda i: (0, 0))],
        out_shape=[jax.ShapeDtypeStruct((s, D_MODEL), F32), jax.ShapeDtypeStruct((s, D_MODEL), BF16),
                   jax.ShapeDtypeStruct((1, D_MODEL), F32), jax.ShapeDtypeStruct((1, 128), F32)],
        compiler_params=_params(("arbitrary",)))(x, r, g_post, target)


def _pre_bwd(dh, x, g_pre, dx_res, name, r_prev=None, g_post_prev=None):
    s = x.shape[0]
    ts = _tile(s, 256)
    chain = r_prev is not None

    def body(*refs):
        if chain:
            dh_ref, x_ref, res_ref, r_ref, g_ref, gp_ref, dx_ref, dr_ref, dg_ref, dgp_ref = refs
        else:
            dh_ref, x_ref, res_ref, g_ref, dx_ref, dg_ref = refs
        first = pl.program_id(0) == 0
        dxn, dg_rows = _rms_bwd(x_ref[...], g_ref[...], dh_ref[...])
        dx = res_ref[...] + dxn
        dx_ref[...] = dx
        _colsum_into(dg_ref, dg_rows, first)
        if chain:
            dr, dgp_rows = _rms_bwd(r_ref[...], gp_ref[...], dx)
            dr_ref[...] = dr.astype(BF16)
            _colsum_into(dgp_ref, dgp_rows, first)

    spec = pl.BlockSpec((ts, D_MODEL), lambda i: (i, 0))
    vec = pl.BlockSpec((1, D_MODEL), lambda i: (0, 0))
    full = jax.ShapeDtypeStruct((s, D_MODEL), F32)
    vshape = jax.ShapeDtypeStruct((1, D_MODEL), F32)
    if chain:
        return pl.pallas_call(
            body, name=name, grid=(s // ts,), in_specs=[spec] * 4 + [vec] * 2, out_specs=[spec, spec, vec, vec],
            out_shape=[full, jax.ShapeDtypeStruct((s, D_MODEL), BF16), vshape, vshape],
            compiler_params=_params(("arbitrary",)))(dh, x, dx_res, r_prev, g_pre, g_post_prev)
    return pl.pallas_call(
        body, name=name, grid=(s // ts,), in_specs=[spec] * 3 + [vec], out_specs=[spec, vec],
        out_shape=[full, vshape], compiler_params=_params(("arbitrary",)))(dh, x, dx_res, g_pre)


def _merge_bwd(dm, y, zg, name, plan=None):
    s = y.shape[0]
    ts, tc = _tile(s, 512), 1024
    nj = D_MODEL // tc

    def body(dm_ref, y_ref, z_ref, dy_ref, dz_ref):
        g = _sigmoid(z_ref[...].astype(F32))
        d = dm_ref[...]
        dy_ref[...] = (d * g).astype(BF16)
        dz_ref[...] = (d * y_ref[...] * g * (1.0 - g)).astype(BF16)

    blk = pl.BlockSpec((ts, tc), lambda i, j, b: (i, b * nj + j))
    shape = jax.ShapeDtypeStruct((s, N_GATE), BF16)
    return _pcall(body, (dm, y, zg), name=name, grid=(s // ts, nj, N_BRANCH),
                  in_specs=[pl.BlockSpec((ts, tc), lambda i, j, b: (i, j)), blk, blk], out_specs=[blk, blk],
                  out_shape=[shape, shape], sem=("parallel", "parallel", "parallel"), vmem=VMEM_BIG, plan=plan)


POOL_HALO = 16


def _pool_windows(ext_ref, ts, first_row):
    outs = []
    t = first_row + lax.broadcasted_iota(jnp.int32, (ts, GDIM), 0)
    for gi, w in enumerate(POOL_WINDOWS):
        cols = pl.ds(gi * GDIM, GDIM)
        acc = ext_ref[pl.ds(POOL_HALO, ts), cols]
        cur = acc
        for k in range(1, w):
            acc = acc + ext_ref[pl.ds(POOL_HALO - k, ts), cols]
        cnt = jnp.minimum(t + 1, w).astype(F32)
        outs.append(acc / cnt - cur)
    return outs


def _pool_fwd(za, pool_w, pool_scale, name):
    s = za.shape[0]
    ts = _tile(s, 512)
    hb = ts // POOL_HALO

    def body(a_ref, halo_ref, w_ref, sc_ref, o_ref, ext_ref):
        i = pl.program_id(0)
        ext_ref[pl.ds(0, POOL_HALO), :] = jnp.where(i > 0, halo_ref[...], 0.0)
        ext_ref[pl.ds(POOL_HALO, ts), :] = a_ref[...]
        pooled = _pool_windows(ext_ref, ts, i * ts)
        for gi in range(len(POOL_WINDOWS)):
            mixed = jnp.dot(pooled[gi].astype(BF16), w_ref[gi].astype(BF16), preferred_element_type=F32)
            o_ref[:, pl.ds(gi * GDIM, GDIM)] = (mixed * sc_ref[:, pl.ds(gi * GDIM, GDIM)]).astype(BF16)

    return pl.pallas_call(
        body, name=name, grid=(s // ts,),
        in_specs=[pl.BlockSpec((ts, BR), lambda i: (i, 0)),
                  pl.BlockSpec((POOL_HALO, BR), lambda i: (jnp.maximum(i * hb - 1, 0), 0)),
                  pl.BlockSpec((4, GDIM, GDIM), lambda i: (0, 0, 0)), pl.BlockSpec((1, BR), lambda i: (0, 0))],
        out_specs=pl.BlockSpec((ts, BR), lambda i: (i, 0)), out_shape=jax.ShapeDtypeStruct((s, BR), BF16),
        scratch_shapes=[pltpu.VMEM((POOL_HALO + ts, BR), F32)], compiler_params=_params(("parallel",)))(
            za, za, pool_w, pool_scale)


def _pool_bwd(dact, za, pool_w, pool_scale, name):
    s = za.shape[0]
    ts = _tile(s, 512)
    hb = ts // POOL_HALO
    n_t = s // ts

    def body(d_ref, dhalo_ref, a_ref, halo_ref, w_ref, sc_ref, dz_ref, dw_ref, dsc_ref, ext_ref, f_ref):
        i = pl.program_id(0)
        first = i == 0
        ext_ref[pl.ds(0, POOL_HALO), :] = jnp.where(i > 0, halo_ref[...], 0.0)
        ext_ref[pl.ds(POOL_HALO, ts), :] = a_ref[...]
        pooled = _pool_windows(ext_ref, ts, i * ts)
        d_tile = d_ref[...]
        d_next = jnp.where(i < n_t - 1, dhalo_ref[...], 0.0)
        t_ext = i * ts + lax.broadcasted_iota(jnp.int32, (ts + POOL_HALO, GDIM), 0)
        dsc = []
        for gi, w in enumerate(POOL_WINDOWS):
            cols = pl.ds(gi * GDIM, GDIM)
            wg = w_ref[gi].astype(BF16)
            sc = sc_ref[:, cols]
            pg = pooled[gi].astype(BF16)
            mixed = jnp.dot(pg, wg, preferred_element_type=F32)
            dsc.append(jnp.sum(d_tile[:, gi * GDIM:(gi + 1) * GDIM] * mixed, axis=0, keepdims=True))
            dmix = jnp.concatenate([d_tile[:, gi * GDIM:(gi + 1) * GDIM], d_next[:, gi * GDIM:(gi + 1) * GDIM]], axis=0) * sc
            dmix = dmix.astype(BF16)
            dwg = lax.dot_general(pg, dmix[:ts], (((0,), (0,)), ((), ())), preferred_element_type=F32)

            @pl.when(first)
            def _():
                dw_ref[gi] = dwg

            @pl.when(jnp.logical_not(first))
            def _():
                dw_ref[gi] += dwg

            dpool = lax.dot_general(dmix, wg, (((1,), (1,)), ((), ())), preferred_element_type=F32)
            f_ref[:, cols] = dpool / jnp.minimum(t_ext + 1, w).astype(F32)
            acc = f_ref[pl.ds(0, ts), cols]
            for k in range(1, w):
                acc = acc + f_ref[pl.ds(k, ts), cols]
            dz_ref[:, cols] = (acc - dpool[:ts]).astype(BF16)
        dsc_all = jnp.concatenate(dsc, axis=1)

        @pl.when(first)
        def _():
            dsc_ref[...] = dsc_all

        @pl.when(jnp.logical_not(first))
        def _():
            dsc_ref[...] += dsc_all

    n_hb = s // POOL_HALO
    return pl.pallas_call(
        body, name=name, grid=(n_t,),
        in_specs=[pl.BlockSpec((ts, BR), lambda i: (i, 0)),
                  pl.BlockSpec((POOL_HALO, BR), lambda i: (jnp.minimum((i + 1) * hb, n_hb - 1), 0)),
                  pl.BlockSpec((ts, BR), lambda i: (i, 0)),
                  pl.BlockSpec((POOL_HALO, BR), lambda i: (jnp.maximum(i * hb - 1, 0), 0)),
                  pl.BlockSpec((4, GDIM, GDIM), lambda i: (0, 0, 0)), pl.BlockSpec((1, BR), lambda i: (0, 0))],
        out_specs=[pl.BlockSpec((ts, BR), lambda i: (i, 0)), pl.BlockSpec((4, GDIM, GDIM), lambda i: (0, 0, 0)),
                   pl.BlockSpec((1, BR), lambda i: (0, 0))],
        out_shape=[jax.ShapeDtypeStruct((s, BR), BF16), jax.ShapeDtypeStruct((4, GDIM, GDIM), F32),
                   jax.ShapeDtypeStruct((1, BR), F32)],
        scratch_shapes=[pltpu.VMEM((POOL_HALO + ts, BR), F32), pltpu.VMEM((ts + POOL_HALO, BR), F32)],
        compiler_params=_params(("arbitrary",)))(dact, dact, za, za, pool_w, pool_scale)


CONV_HALO = 32
CONV_LEAD = CONV_HALO - (CONV_WIDTH - 1)


def _conv_fwd(za, conv_w, conv_b, ng, nb, name):
    s = za.shape[0]
    ts = _tile(s, 512)
    hb = ts // CONV_HALO

    def body(a_ref, g_ref, ah_ref, gh_ref, w_ref, b_ref, ng_ref, nb_ref, yc_ref, act_ref, ext_ref):
        i = pl.program_id(0)
        ext_ref[pl.ds(0, CONV_HALO), :] = jnp.where(i > 0, ah_ref[...] * _sigmoid(gh_ref[...]), 0.0)
        ext_ref[pl.ds(CONV_HALO, ts), :] = a_ref[...] * _sigmoid(g_ref[...])
        acc = jnp.zeros((ts, BR), F32) + b_ref[...]
        for k in range(CONV_WIDTH):
            acc = acc + w_ref[pl.ds(k, 1), :] * ext_ref[pl.ds(CONV_LEAD + k, ts), :]
        yc_ref[...] = acc
        xh, _ = _ln_stats(acc)
        ln = xh * ng_ref[...] + nb_ref[...]
        act_ref[...] = (ln * _sigmoid(ln)).astype(BF16)

    tile = lambda c: pl.BlockSpec((ts, BR), lambda i: (i, c))
    halo = lambda c: pl.BlockSpec((CONV_HALO, BR), lambda i: (jnp.maximum(i * hb - 1, 0), c))
    vec = pl.BlockSpec((1, BR), lambda i: (0, 0))
    return pl.pallas_call(
        body, name=name, grid=(s // ts,),
        in_specs=[tile(1), tile(2), halo(1), halo(2), pl.BlockSpec((CONV_WIDTH, BR), lambda i: (0, 0)), vec, vec, vec],
        out_specs=[pl.BlockSpec((ts, BR), lambda i: (i, 0))] * 2,
        out_shape=[jax.ShapeDtypeStruct((s, BR), F32), jax.ShapeDtypeStruct((s, BR), BF16)],
        scratch_shapes=[pltpu.VMEM((CONV_HALO + ts, BR), F32)], compiler_params=_params(("parallel",)))(
            za, za, za, za, conv_w, conv_b, ng, nb)


def _conv_bwd_norm(dact, yc, ng, nb, name):
    s = yc.shape[0]
    ts = _tile(s, 512)

    def body(d_ref, y_ref, ng_ref, nb_ref, dy_ref, db_ref, dng_ref, dnb_ref):
        first = pl.program_id(0) == 0
        xh, r = _ln_stats(y_ref[...])
        g = ng_ref[...]
        ln = xh * g + nb_ref[...]
        sg = _sigmoid(ln)
        dln = d_ref[...] * sg * (1.0 + ln * (1.0 - sg))
        dy = _ln_bwd(xh, r, g, dln)
        dy_ref[...] = dy
        _colsum_into(db_ref, dy, first)
        _colsum_into(dng_ref, dln * xh, first)
        _colsum_into(dnb_ref, dln, first)

    vec = pl.BlockSpec((1, BR), lambda i: (0, 0))
    vshape = jax.ShapeDtypeStruct((1, BR), F32)
    return pl.pallas_call(
        body, name=name, grid=(s // ts,),
        in_specs=[pl.BlockSpec((ts, BR), lambda i: (i, 1)), pl.BlockSpec((ts, BR), lambda i: (i, 0)), vec, vec],
        out_specs=[pl.BlockSpec((ts, BR), lambda i: (i, 0)), vec, vec, vec],
        out_shape=[jax.ShapeDtypeStruct((s, BR), F32), vshape, vshape, vshape],
        compiler_params=_params(("arbitrary",)))(dact, yc, ng, nb)


def _conv_bwd_taps(dyc, za, conv_w, name):
    s = za.shape[0]
    ts = _tile(s, 512)
    hb = ts // CONV_HALO
    n_t = s // ts
    n_hb = s // CONV_HALO

    def body(d_ref, dh_ref, a_ref, g_ref, ah_ref, gh_ref, w_ref, dz_ref, dw_ref, ext_ref, f_ref):
        i = pl.program_id(0)
        first = i == 0
        a, sg = a_ref[...], _sigmoid(g_ref[...])
        ext_ref[pl.ds(0, CONV_HALO), :] = jnp.where(i > 0, ah_ref[...] * _sigmoid(gh_ref[...]), 0.0)
        ext_ref[pl.ds(CONV_HALO, ts), :] = a * sg
        d = d_ref[...]
        f_ref[pl.ds(0, ts), :] = d
        f_ref[pl.ds(ts, CONV_HALO), :] = jnp.where(i < n_t - 1, dh_ref[...], 0.0)
        dglu = jnp.zeros((ts, BR), F32)
        rows = []
        for k in range(CONV_WIDTH):
            rows.append(jnp.sum(d * ext_ref[pl.ds(CONV_LEAD + k, ts), :], axis=0, keepdims=True))
            dglu = dglu + w_ref[pl.ds(k, 1), :] * f_ref[pl.ds(CONV_WIDTH - 1 - k, ts), :]
        rows.append(jnp.zeros((1, BR), F32))
        dw = jnp.concatenate(rows, axis=0)

        @pl.when(first)
        def _():
            dw_ref[...] = dw

        @pl.when(jnp.logical_not(first))
        def _():
            dw_ref[...] += dw

        dz_ref[:, pl.ds(0, BR)] = (dglu * sg).astype(BF16)
        dz_ref[:, pl.ds(BR, BR)] = (dglu * a * sg * (1.0 - sg)).astype(BF16)

    tile = lambda c: pl.BlockSpec((ts, BR), lambda i: (i, c))
    halo = lambda c: pl.BlockSpec((CONV_HALO, BR), lambda i: (jnp.maximum(i * hb - 1, 0), c))
    return pl.pallas_call(
        body, name=name, grid=(n_t,),
        in_specs=[pl.BlockSpec((ts, BR), lambda i: (i, 0)),
                  pl.BlockSpec((CONV_HALO, BR), lambda i: (jnp.minimum((i + 1) * hb, n_hb - 1), 0)),
                  tile(1), tile(2), halo(1), halo(2), pl.BlockSpec((CONV_WIDTH, BR), lambda i: (0, 0))],
        out_specs=[pl.BlockSpec((ts, 2 * BR), lambda i: (i, 0)), pl.BlockSpec((CONV_WIDTH + 1, BR), lambda i: (0, 0))],
        out_shape=[jax.ShapeDtypeStruct((s, 2 * BR), BF16), jax.ShapeDtypeStruct((CONV_WIDTH + 1, BR), F32)],
        scratch_shapes=[pltpu.VMEM((CONV_HALO + ts, BR), F32), pltpu.VMEM((ts + CONV_HALO, BR), F32)],
        compiler_params=_params(("arbitrary",)))(dyc, dyc, za, za, za, za, conv_w)


def _tril(w):
    r = lax.broadcasted_iota(jnp.int32, (CHUNK, CHUNK), 0)
    c = lax.broadcasted_iota(jnp.int32, (CHUNK, CHUNK), 1)
    return jnp.where(c <= r, w, 0.0)


def _sgu_fwd(za, ng, nb, sgu_w, bias_b, name):
    s = za.shape[0]
    ts = _tile(s, 512)

    def body(u_ref, v_ref, ng_ref, nb_ref, w_ref, b_ref, o_ref):
        u = _gelu(u_ref[...])
        xh, _ = _ln_stats(_gelu(v_ref[...]))
        vln = (xh * ng_ref[...] + nb_ref[...]).astype(BF16)
        for gi in range(4):
            wg = _tril(w_ref[gi]).astype(BF16)
            for n in range(ts // CHUNK):
                blk = vln[n * CHUNK:(n + 1) * CHUNK, gi * GDIM:(gi + 1) * GDIM]
                sp = jnp.dot(wg, blk, preferred_element_type=F32) + b_ref[gi]
                o_ref[pl.ds(n * CHUNK, CHUNK), pl.ds(gi * GDIM, GDIM)] = (
                    u[n * CHUNK:(n + 1) * CHUNK, gi * GDIM:(gi + 1) * GDIM] * sp).astype(BF16)

    vec = pl.BlockSpec((1, BR), lambda i: (0, 0))
    cube = pl.BlockSpec((4, CHUNK, GDIM), lambda i: (0, 0, 0))
    return pl.pallas_call(
        body, name=name, grid=(s // ts,),
        in_specs=[pl.BlockSpec((ts, BR), lambda i: (i, 3)), pl.BlockSpec((ts, BR), lambda i: (i, 4)), vec, vec, cube, cube],
        out_specs=pl.BlockSpec((ts, BR), lambda i: (i, 0)), out_shape=jax.ShapeDtypeStruct((s, BR), BF16),
        compiler_params=_params(("parallel",)))(za, za, ng, nb, sgu_w, bias_b)


def _sgu_bwd(dact, za, ng, nb, sgu_w, bias_b, name):
    s = za.shape[0]
    ts = _tile(s, 512)

    def body(d_ref, u_ref, v_ref, ng_ref, nb_ref, w_ref, b_ref, dz_ref, dw_ref, db_ref, dng_ref, dnb_ref, dv_ref):
        first = pl.program_id(0) == 0
        u_raw, v_raw = u_ref[...], v_ref[...]
        u = _gelu(u_raw)
        xh, r = _ln_stats(_gelu(v_raw))
        g = ng_ref[...]
        vln = (xh * g + nb_ref[...]).astype(BF16)
        d = d_ref[...]
        dsp = d * u
        dsp16 = dsp.astype(BF16)
        for gi in range(4):
            wg = _tril(w_ref[gi]).astype(BF16)
            dwg = jnp.zeros((CHUNK, CHUNK), F32)
            dbg = jnp.zeros((CHUNK, 1), F32)
            for n in range(ts // CHUNK):
                rows, cols = slice(n * CHUNK, (n + 1) * CHUNK), slice(gi * GDIM, (gi + 1) * GDIM)
                blk = vln[rows, cols]
                sp = jnp.dot(wg, blk, preferred_element_type=F32) + b_ref[gi]
                dz_ref[pl.ds(n * CHUNK, CHUNK), pl.ds(gi * GDIM, GDIM)] = (
                    d[rows, cols] * sp * _gelu_grad(u_raw[rows, cols])).astype(BF16)
                dv_ref[pl.ds(n * CHUNK, CHUNK), pl.ds(gi * GDIM, GDIM)] = lax.dot_general(
                    wg, dsp16[rows, cols], (((0,), (0,)), ((), ())), preferred_element_type=F32)
                dwg = dwg + lax.dot_general(dsp16[rows, cols], blk, (((1,), (1,)), ((), ())), preferred_element_type=F32)
                dbg = dbg + jnp.sum(dsp[rows, cols], axis=1, keepdims=True)
            dwg = _tril(dwg)

            @pl.when(first)
            def _():
                dw_ref[gi] = dwg
                db_ref[gi] = dbg

            @pl.when(jnp.logical_not(first))
            def _():
                dw_ref[gi] += dwg
                db_ref[gi] += dbg

        dvln = dv_ref[...]
        dz_ref[:, pl.ds(BR, BR)] = (_ln_bwd(xh, r, g, dvln) * _gelu_grad(v_raw)).astype(BF16)
        _colsum_into(dng_ref, dvln * xh, first)
        _colsum_into(dnb_ref, dvln, first)

    vec = pl.BlockSpec((1, BR), lambda i: (0, 0))
    cube = pl.BlockSpec((4, CHUNK, GDIM), lambda i: (0, 0, 0))
    vshape = jax.ShapeDtypeStruct((1, BR), F32)
    return pl.pallas_call(
        body, name=name, grid=(s // ts,),
        in_specs=[pl.BlockSpec((ts, BR), lambda i: (i, 2)), pl.BlockSpec((ts, BR), lambda i: (i, 3)),
                  pl.BlockSpec((ts, BR), lambda i: (i, 4)), vec, vec, cube, cube],
        out_specs=[pl.BlockSpec((ts, 2 * BR), lambda i: (i, 0)), cube, pl.BlockSpec((4, CHUNK, 1), lambda i: (0, 0, 0)),
                   vec, vec],
        out_shape=[jax.ShapeDtypeStruct((s, 2 * BR), BF16), jax.ShapeDtypeStruct((4, CHUNK, CHUNK), F32),
                   jax.ShapeDtypeStruct((4, CHUNK, 1), F32), vshape, vshape],
        scratch_shapes=[pltpu.VMEM((ts, BR), F32)], compiler_params=_params(("arbitrary",)))(
            dact, za, za, ng, nb, sgu_w, bias_b)


KR_BLOCK = 3584 // 128


def _mla_prep(za, qg, kvg, ck, sk, name):
    s = za.shape[0]
    ts = _tile(s, 512)

    def body(cq_ref, ckv_ref, kr_ref, qg_ref, kvg_ref, c_ref, s_ref, qn_ref, kvn_ref, krr_ref):
        qn_ref[...] = _rms(cq_ref[...], qg_ref[...]).astype(BF16)
        kvn_ref[...] = _rms(ckv_ref[...], kvg_ref[...]).astype(BF16)
        kr = kr_ref[...]
        krr_ref[...] = (kr * c_ref[...] + _rot_half(kr, 128, 0) * s_ref[...]).astype(BF16)

    vec = pl.BlockSpec((1, BR), lambda i: (0, 0))
    tab = pl.BlockSpec((ts, 128), lambda i: (i, 0))
    row = pl.BlockSpec((ts, BR), lambda i: (i, 0))
    return pl.pallas_call(
        body, name=name, grid=(s // ts,),
        in_specs=[pl.BlockSpec((ts, BR), lambda i: (i, 5)), pl.BlockSpec((ts, BR), lambda i: (i, 6)),
                  pl.BlockSpec((ts, 128), lambda i: (i, KR_BLOCK)), vec, vec, tab, tab],
        out_specs=[row, row, tab],
        out_shape=[jax.ShapeDtypeStruct((s, BR), BF16), jax.ShapeDtypeStruct((s, BR), BF16),
                   jax.ShapeDtypeStruct((s, 128), BF16)],
        compiler_params=_params(("parallel",)))(za, za, za, qg, kvg, ck, sk)


def _mla_prep_bwd(dqn, dkvn, dkr_heads, za, qg, kvg, ck, sk, name):
    s = za.shape[0]
    ts = _tile(s, 512)

    def body(dq_ref, dkv_ref, dkr_ref, cq_ref, ckv_ref, qg_ref, kvg_ref, c_ref, s_ref, dz_ref, dqg_ref, dkvg_ref):
        first = pl.program_id(0) == 0
        dcq, rows_q = _rms_bwd(cq_ref[...], qg_ref[...], dq_ref[...])
        dckv, rows_kv = _rms_bwd(ckv_ref[...], kvg_ref[...], dkv_ref[...])
        dz_ref[:, pl.ds(0, BR)] = dcq.astype(BF16)
        dz_ref[:, pl.ds(BR, BR)] = dckv.astype(BF16)
        dk = dkr_ref[:, pl.ds(0, 128)]
        for h in range(1, MLA_HEADS):
            dk = dk + dkr_ref[:, pl.ds(h * 128, 128)]
        dz_ref[:, pl.ds(2 * BR, 128)] = (dk * c_ref[...] - _rot_half(dk, 128, 0) * s_ref[...]).astype(BF16)
        _colsum_into(dqg_ref, rows_q, first)
        _colsum_into(dkvg_ref, rows_kv, first)

    vec = pl.BlockSpec((1, BR), lambda i: (0, 0))
    tab = pl.BlockSpec((ts, 128), lambda i: (i, 0))
    row = pl.BlockSpec((ts, BR), lambda i: (i, 0))
    wide = 2 * BR + 128
    vshape = jax.ShapeDtypeStruct((1, BR), F32)
    return pl.pallas_call(
        body, name=name, grid=(s // ts,),
        in_specs=[row, row, pl.BlockSpec((ts, MLA_HEADS * 128), lambda i: (i, 0)),
                  pl.BlockSpec((ts, BR), lambda i: (i, 5)), pl.BlockSpec((ts, BR), lambda i: (i, 6)), vec, vec, tab, tab],
        out_specs=[pl.BlockSpec((ts, wide), lambda i: (i, 0)), vec, vec],
        out_shape=[jax.ShapeDtypeStruct((s, wide), BF16), vshape, vshape],
        compiler_params=_params(("arbitrary",)))(dqn, dkvn, dkr_heads, za, za, qg, kvg, ck, sk)


def _attn_tiles(s):
    tq, tk = _tile(s, 1024), _tile(s, 512)
    return tq, tk, tq // tk


def _causal(qi, ki, tq, tk):
    row = qi * tq + lax.broadcasted_iota(jnp.int32, (tq, tk), 0)
    col = ki * tk + lax.broadcasted_iota(jnp.int32, (tq, tk), 1)
    return col <= row


def _step_count(t, firsts):
    n = jnp.int32(0)
    for f in firsts[1:]:
        n = n + jnp.where(t >= f, 1, 0)
    return n


def _q_major_pairs(nq, r):
    firsts = [r * qq * (qq + 1) // 2 for qq in range(nq)]

    def pair(t):
        qi = _step_count(t, firsts)
        return qi, t - r * qi * (qi + 1) // 2

    return r * nq * (nq + 1) // 2, pair


def _k_major_pairs(nq, nk, r):
    counts = [nq - kk // r for kk in range(nk)]
    firsts = [sum(counts[:kk]) for kk in range(nk)]

    def pair(t):
        ki = _step_count(t, firsts)
        first = jnp.int32(0)
        for kk in range(1, nk):
            first = first + jnp.where(t >= firsts[kk], counts[kk - 1], 0)
        return ki // r + (t - first), ki

    return sum(counts), pair


FLASH_HEADS = 2


def _flash_fwd(q, kv, krr, name, plan=None):
    s = q.shape[0]
    tq, tk, r = _attn_tiles(s)
    n_pairs, pair = _q_major_pairs(s // tq, r)
    hp = FLASH_HEADS

    def body(q_ref, kv_ref, kr_ref, o_ref, lse_ref, m_sc, l_sc, acc_sc):
        qi, ki = pair(pl.program_id(1))

        @pl.when(ki == 0)
        def _():
            m_sc[...] = jnp.full((hp, tq, 1), NEG, F32)
            l_sc[...] = jnp.zeros((hp, tq, 1), F32)
            acc_sc[...] = jnp.zeros((hp, tq, V_DIM), F32)

        def step(masked):
            for h in range(hp):
                k = jnp.concatenate([kv_ref[:, pl.ds(h * QW, QK_NOPE)], kr_ref[...]], axis=1)
                sc = lax.dot_general(q_ref[:, pl.ds(h * QW, QW)], k, (((1,), (1,)), ((), ())), preferred_element_type=F32)
                if masked:
                    sc = jnp.where(_causal(qi, ki, tq, tk), sc, NEG)
                m_prev = m_sc[h]
                m_new = jnp.maximum(m_prev, jnp.max(sc, axis=1, keepdims=True))
                alpha = jnp.exp(m_prev - m_new)
                p = jnp.exp(sc - m_new)
                l_sc[h] = alpha * l_sc[h] + jnp.sum(p, axis=1, keepdims=True)
                acc_sc[h] = alpha * acc_sc[h] + jnp.dot(p.astype(BF16), kv_ref[:, pl.ds(h * QW + QK_NOPE, V_DIM)],
                                                        preferred_element_type=F32)
                m_sc[h] = m_new

        pl.when(ki < qi * r)(functools.partial(step, False))
        pl.when(ki >= qi * r)(functools.partial(step, True))

        @pl.when(ki == (qi + 1) * r - 1)
        def _():
            for h in range(hp):
                o_ref[:, pl.ds(h * V_DIM, V_DIM)] = (acc_sc[h] / l_sc[h]).astype(BF16)
                lse_ref[:, pl.ds(h * 128, 128)] = jnp.broadcast_to(m_sc[h] + jnp.log(l_sc[h]), (tq, 128))

    out_blk = pl.BlockSpec((tq, hp * 128), lambda g, t: (pair(t)[0], g))
    return _pcall(
        body, (q, kv, krr), name=name, grid=(MLA_HEADS // hp, n_pairs),
        in_specs=[pl.BlockSpec((tq, hp * QW), lambda g, t: (pair(t)[0], g)),
                  pl.BlockSpec((tk, hp * QW), lambda g, t: (pair(t)[1], g)),
                  pl.BlockSpec((tk, 128), lambda g, t: (pair(t)[1], 0))],
        out_specs=[out_blk, out_blk],
        out_shape=[jax.ShapeDtypeStruct((s, MLA_HEADS * V_DIM), BF16), jax.ShapeDtypeStruct((s, MLA_HEADS * 128), F32)],
        scratch_shapes=[pltpu.VMEM((hp, tq, 1), F32), pltpu.VMEM((hp, tq, 1), F32), pltpu.VMEM((hp, tq, V_DIM), F32)],
        sem=("parallel", "arbitrary"), vmem=VMEM_BIG, plan=plan)


DO_BLOCK = 3 * BR // 128


def _flash_probs(q_ref, kn_ref, kr_ref, v_ref, do_ref, o_ref, lse_ref, qi, ki, tq, tk, masked):
    k = jnp.concatenate([kn_ref[...], kr_ref[...]], axis=1)
    q = q_ref[...]
    sc = lax.dot_general(q, k, (((1,), (1,)), ((), ())), preferred_element_type=F32)
    lse = jnp.max(lse_ref[...], axis=1, keepdims=True)
    p = jnp.exp(sc - lse)
    if masked:
        p = jnp.where(_causal(qi, ki, tq, tk), p, 0.0)
    do = do_ref[...]
    delta = jnp.sum(do * o_ref[...].astype(F32), axis=1, keepdims=True)
    do = do.astype(BF16)
    dp = lax.dot_general(do, v_ref[...], (((1,), (1,)), ((), ())), preferred_element_type=F32)
    ds = (p * (dp - delta)).astype(BF16)
    return q, k, p, do, ds


def _flash_bwd(q, kv, krr, dact, o, lse, cq, sq, name, plan=None):
    s = q.shape[0]
    tq, tk, r = _attn_tiles(s)
    nq, nk = s // tq, s // tk
    n_pairs, pair = _k_major_pairs(nq, nk, r)

    def body(q_ref, kn_ref, kr_ref, v_ref, do_ref, o_ref, lse_ref, c_ref, s_ref, dkv_ref, dkr_ref, dq_ref,
             dk_sc, dv_sc, dq_sc):
        qi, ki = pair(pl.program_id(1))
        rows = pl.ds(pl.multiple_of(qi * tq, tq), tq)

        @pl.when(qi == ki // r)
        def _():
            dk_sc[...] = jnp.zeros((tk, QW), F32)
            dv_sc[...] = jnp.zeros((tk, V_DIM), F32)

        @pl.when(ki == 0)
        def _():
            dq_sc[rows, :] = jnp.zeros((tq, QW), F32)

        def step(masked):
            qv, k, p, dov, ds = _flash_probs(q_ref, kn_ref, kr_ref, v_ref, do_ref, o_ref, lse_ref, qi, ki, tq, tk, masked)
            dv_sc[...] += lax.dot_general(p.astype(BF16), dov, (((0,), (0,)), ((), ())), preferred_element_type=F32)
            dk_sc[...] += lax.dot_general(ds, qv, (((0,), (0,)), ((), ())), preferred_element_type=F32)
            dq_sc[rows, :] += jnp.dot(ds, k, preferred_element_type=F32)

        pl.when(ki < qi * r)(functools.partial(step, False))
        pl.when(ki >= qi * r)(functools.partial(step, True))

        @pl.when(qi == nq - 1)
        def _():
            dkv_ref[:, pl.ds(0, 128)] = dk_sc[:, pl.ds(0, 128)].astype(BF16)
            dkv_ref[:, pl.ds(128, 128)] = dv_sc[...].astype(BF16)
            dkr_ref[...] = dk_sc[:, pl.ds(128, 128)]

        @pl.when(ki == (qi + 1) * r - 1)
        def _():
            dq = dq_sc[rows, :] * ATT_SCALE
            dq_ref[...] = (dq * c_ref[...] - _rot_half(dq, QW, QK_NOPE) * s_ref[...]).astype(BF16)

    def next_done(t):
        qi, ki = pair(t)
        return jnp.minimum(ki // r + jnp.where(jnp.logical_and(ki % r == r - 1, qi > ki // r), 1, 0), nq - 1)

    qmap = lambda c: (lambda h, t: (pair(t)[0], c(h)))
    kmap = lambda c: (lambda h, t: (pair(t)[1], c(h)))
    last = lambda c: (lambda h, t: (next_done(t), c(h)))
    return _pcall(
        body, (q, kv, krr, kv, dact, o, lse, cq, sq), name=name, grid=(MLA_HEADS, n_pairs),
        in_specs=[pl.BlockSpec((tq, QW), qmap(lambda h: h)), pl.BlockSpec((tk, 128), kmap(lambda h: 2 * h)),
                  pl.BlockSpec((tk, 128), kmap(lambda h: 0)), pl.BlockSpec((tk, 128), kmap(lambda h: 2 * h + 1)),
                  pl.BlockSpec((tq, 128), qmap(lambda h: DO_BLOCK + h)), pl.BlockSpec((tq, 128), qmap(lambda h: h)),
                  pl.BlockSpec((tq, 128), qmap(lambda h: h)), pl.BlockSpec((tq, QW), last(lambda h: 0)),
                  pl.BlockSpec((tq, QW), last(lambda h: 0))],
        out_specs=[pl.BlockSpec((tk, QW), kmap(lambda h: h)), pl.BlockSpec((tk, 128), kmap(lambda h: h)),
                   pl.BlockSpec((tq, QW), last(lambda h: h))],
        out_shape=[jax.ShapeDtypeStruct((s, MLA_HEADS * QW), BF16), jax.ShapeDtypeStruct((s, MLA_HEADS * 128), F32),
                   jax.ShapeDtypeStruct((s, MLA_HEADS * QW), BF16)],
        scratch_shapes=[pltpu.VMEM((tk, QW), F32), pltpu.VMEM((tk, V_DIM), F32), pltpu.VMEM((s, QW), F32)],
        sem=("parallel", "arbitrary"), vmem=VMEM_BIG, plan=plan)


def _rope_tables(positions):
    inv_freq = ROPE_THETA ** (-jnp.arange(0, QK_ROPE, 2, dtype=F32) / QK_ROPE)
    ang = positions.reshape(-1).astype(F32)[:, None] * inv_freq
    cos, sin = jnp.cos(ang), jnp.sin(ang)
    s = cos.shape[0]
    one, zero = jnp.ones((s, 64), F32), jnp.zeros((s, 64), F32)
    ck = jnp.concatenate([cos, cos, one], axis=1)
    sk = jnp.concatenate([sin, sin, zero], axis=1)
    cq = jnp.concatenate([one, one, ck], axis=1)
    sq = jnp.concatenate([zero, zero, sk], axis=1)
    return ck, sk, cq, sq


def _cols_full(gathered):
    _, r, c = gathered.shape
    return gathered.transpose(1, 0, 2).reshape(r, N_DEV * c)


def _cols_by_owner(full):
    r, n = full.shape
    return full.reshape(r, N_DEV, n // N_DEV).transpose(1, 0, 2)


def _layer_weights(gat, small, l):
    shards = gat[("w_in", l)]
    cut, rem = divmod(N_IN_A, shards.shape[2])
    w = {
        "gat": gat, "layer": l,
        "w_a": jnp.concatenate([shards[d] for d in range(cut)] + [shards[cut][:, :rem],
                                                                 jnp.zeros((D_MODEL, ZA - N_IN_A), BF16)], axis=1),
        "w_g": jnp.concatenate([shards[cut][:, rem:]] + [shards[d] for d in range(cut + 1, N_DEV)], axis=1),
        "conv_w": small["conv_w"][l],
        "pool_w": small["pool_w"][l], "sgu_w": small["sgu_w"][l],
        "sgu_bias": jnp.broadcast_to(small["sgu_b"][l][:, :, None], (4, CHUNK, GDIM)),
    }
    for name in ("pre_mix_g", "pool_scale", "conv_b", "conv_norm_g", "conv_norm_b", "sgu_norm_g", "sgu_norm_b",
                 "q_norm_g", "kv_norm_g", "post_mix_g", "pre_mlp_g", "post_mlp_g"):
        w[name] = small[name][l][None, :]
    return w


def _late(w, name):
    if name not in w:
        gat, l = w["gat"], w["layer"]
        if name == "proj_cat":
            w[name] = jnp.concatenate([_cols_full(gat[(n, l)]) for n in ("pool_proj", "conv_proj", "sgu_proj", "attn_proj")],
                                      axis=0)
        elif name == "w_uq":
            w[name] = jnp.pad(gat[("w_uq", l)].transpose(1, 0, 2),
                              ((0, 0), (0, 0), (0, QW - QK_NOPE - QK_ROPE))).reshape(BR, MLA_HEADS * QW)
        elif name == "w_ukv":
            w[name] = _cols_full(gat[("w_ukv", l)])
        elif name == "w_up":
            w[name] = gat[("w_up", l)]
        else:
            arr = gat[(name, l)]
            w[name] = arr.reshape(N_DEV * arr.shape[1], arr.shape[2])
    return w[name]


def _layer_fwd(x, h1, w, tabs, l, plan):
    ck, sk, cq, sq = tabs
    n = f"l{l}_"
    za = _mm(h1, w["w_a"], mode="nn", name=n + "za", plan=plan)
    zg = _mm(h1, w["w_g"], mode="nn", name=n + "zg", out_dtypes=(BF16,), plan=plan)
    a_pool = _pool_fwd(za, w["pool_w"], w["pool_scale"], n + "pool")
    yc, a_conv = _conv_fwd(za, w["conv_w"], w["conv_b"], w["conv_norm_g"], w["conv_norm_b"], n + "conv")
    a_sgu = _sgu_fwd(za, w["sgu_norm_g"], w["sgu_norm_b"], w["sgu_w"], w["sgu_bias"], n + "sgu")
    qn, kvn, krr = _mla_prep(za, w["q_norm_g"], w["kv_norm_g"], ck, sk, n + "mla_prep")
    q = _mm(qn, _late(w, "w_uq"), mode="nn", name=n + "q", out_dtypes=(BF16,), extras=(cq, sq), tn=QW,
            epilogue=lambda acc, c, sn: ((acc * c + _rot_half(acc, QW, QK_NOPE) * sn) * ATT_SCALE,))
    kv = _mm(kvn, _late(w, "w_ukv"), mode="nn", name=n + "kv", out_dtypes=(BF16,))
    o, lse = _flash_fwd(q, kv, krr, n + "flash", plan=plan)
    act_cat = jnp.concatenate([a_pool, a_conv, a_sgu, o], axis=1)
    y, merged = _proj_merge_fwd(act_cat, _late(w, "proj_cat"), zg, n + "proj", plan=plan)
    m2 = _mm(merged, _late(w, "w_out"), mode="nn", name=n + "out")
    x1, h2 = _post_pre(x, m2, w["post_mix_g"], w["pre_mlp_g"], n + "post_mix")
    up, act = _mm(h2, _late(w, "w_up"), mode="nn", name=n + "up", out_dtypes=(BF16, BF16), plan=plan,
                  mnk=(x.shape[0], D_FF, D_MODEL), tn=UP_SHARD,
                  b_spec=lambda tn, tk: pl.BlockSpec((None, tk, tn), lambda i, j, kk: (j, kk, 0)),
                  epilogue=lambda acc: (acc, jnp.square(jnp.maximum(acc, 0.0))))
    f = _mm(act, _late(w, "w_down"), mode="nn", name=n + "down", plan=plan)
    saved = dict(x=x, h1=h1, za=za, zg=zg, yc=yc, qn=qn, kvn=kvn, krr=krr, q=q, kv=kv, o=o, lse=lse, act_cat=act_cat,
                 y=y, merged=merged, m2=m2, x1=x1, h2=h2, up=up, act=act, f=f)
    return x1, f, saved


def _layer_bwd(dx_out, df, sv, w, tabs, l, prev, plan, rs):
    ck, sk, cq, sq = tabs
    n = f"l{l}_b_"
    g = {}
    own, half = {}, {}

    def to_sibling(kernel, names):
        def done(results):
            for name, t in zip(names, results[0]):
                half[name] = _rs_chip_sum(own[name], t, f"l{l}_chip_sum_{name}")
        plan.at(n + kernel, lambda: [_scatter_sibling([own[name] for name in names])], done)

    def to_chips(kernel, names, prefix=n):
        def done(results):
            for name, t in zip(names, results[0]):
                rs[(name, l)] = (half[name], t)
        plan.at(prefix + kernel, lambda: [_scatter_chips([half[name] for name in names])], done)

    own["w_down"] = _mm(sv["act"], df, mode="tn", name=n + "dw_down", out_dtypes=(BF16,), plan=plan).reshape(
        N_DEV, D_FF // N_DEV, D_MODEL)
    to_sibling("dup", ["w_down"])
    dup = _mm(df, _late(w, "w_down"), mode="nt", name=n + "dup", out_dtypes=(BF16,), extras=(sv["up"],), plan=plan,
              epilogue=lambda acc, up: (acc * 2.0 * jnp.maximum(up.astype(F32), 0.0),))
    to_chips("dw_up", ["w_down"])
    own["w_up"] = _mm(sv["h2"], dup, mode="tn", name=n + "dw_up", tn=UP_SHARD, plan=plan,
                      o_spec=lambda tm, tn: pl.BlockSpec((None, tm, tn), lambda i, j, kk: (j, i, 0)),
                      out_struct=jax.ShapeDtypeStruct((N_DEV, D_MODEL, UP_SHARD), BF16))
    to_sibling("dh2", ["w_up"])
    dh2 = _mm(dup, _late(w, "w_up"), mode="nt", name=n + "dh2", mnk=(dup.shape[0], D_MODEL, D_FF), tk=UP_SHARD,
              plan=plan, b_spec=lambda tn, tk: pl.BlockSpec((None, tn, tk), lambda i, j, kk: (kk, j, 0)))
    dx1, dm2, g["pre_mlp_g"], g["post_mix_g"] = _pre_bwd(dh2, sv["x1"], w["pre_mlp_g"], dx_out, n + "pre_mlp",
                                                           r_prev=sv["m2"], g_post_prev=w["post_mix_g"])
    own["w_out"] = _mm(sv["merged"], dm2, mode="tn", name=n + "dw_out", out_dtypes=(BF16,)).reshape(
        N_DEV, D_MODEL // N_DEV, D_MODEL)
    to_sibling("dmerged", ["w_out"])
    dmerged = _mm(dm2, _late(w, "w_out"), mode="nt", name=n + "dmerged", plan=plan)
    to_chips("merge", ["w_up"])
    dy, dzg = _merge_bwd(dmerged, sv["y"], sv["zg"], n + "merge", plan=plan)
    d_proj = _proj_bwd_w(sv["act_cat"], dy, n + "dw_proj")
    projs = ["pool_proj", "conv_proj", "sgu_proj", "attn_proj"]
    for i, name in enumerate(projs):
        own[name] = _cols_by_owner(d_proj[i * BR:(i + 1) * BR] if i < 3 else d_proj[3 * BR:])
    to_chips("dact", ["w_out"])
    to_sibling("dact", projs)
    dact = _proj_bwd_act(dy, _late(w, "proj_cat"), n + "dact", plan=plan)
    dz_pool, g["pool_w"], g["pool_scale"] = _pool_bwd(dact, sv["za"], w["pool_w"], w["pool_scale"], n + "pool")
    dyc, g["conv_b"], g["conv_norm_g"], g["conv_norm_b"] = _conv_bwd_norm(dact, sv["yc"], w["conv_norm_g"],
                                                                          w["conv_norm_b"], n + "conv_norm")
    dz_conv, g["conv_w"] = _conv_bwd_taps(dyc, sv["za"], w["conv_w"], n + "conv_taps")
    dz_sgu, g["sgu_w"], g["sgu_b"], g["sgu_norm_g"], g["sgu_norm_b"] = _sgu_bwd(
        dact, sv["za"], w["sgu_norm_g"], w["sgu_norm_b"], w["sgu_w"], w["sgu_bias"], n + "sgu")
    to_chips("flash", projs)
    dkv, dkr_heads, dq = _flash_bwd(sv["q"], sv["kv"], sv["krr"], dact, sv["o"], sv["lse"], cq, sq, n + "flash",
                                    plan=plan)
    d_uq = _mm(sv["qn"], dq, mode="tn", name=n + "dw_uq", out_dtypes=(BF16,))
    own["w_uq"] = d_uq.reshape(BR, MLA_HEADS, QW)[:, :, :QK_NOPE + QK_ROPE].transpose(1, 0, 2)
    dqn = _mm(dq, _late(w, "w_uq"), mode="nt", name=n + "dqn")
    own["w_ukv"] = _cols_by_owner(_mm(sv["kvn"], dkv, mode="tn", name=n + "dw_ukv", out_dtypes=(BF16,)))
    to_sibling("dkvn", ["w_uq", "w_ukv"])
    dkvn = _mm(dkv, _late(w, "w_ukv"), mode="nt", name=n + "dkvn", plan=plan)
    dz_mla, g["q_norm_g"], g["kv_norm_g"] = _mla_prep_bwd(dqn, dkvn, dkr_heads, sv["za"], w["q_norm_g"],
                                                           w["kv_norm_g"], ck, sk, n + "mla_prep")
    dza = jnp.concatenate([dz_pool, dz_conv, dz_sgu, dz_mla], axis=1)
    to_chips("dw_a", ["w_uq", "w_ukv"])
    d_a = _mm(sv["h1"], dza, mode="tn", name=n + "dw_a", out_dtypes=(BF16,), plan=plan)
    d_g = _mm(sv["h1"], dzg, mode="tn", name=n + "dw_g", out_dtypes=(BF16,))
    own["w_in"] = _cols_by_owner(jnp.concatenate([d_a[:, :N_IN_A], d_g], axis=1)).transpose(0, 2, 1)
    to_sibling("dh1_a", ["w_in"])
    dh1 = _mm(dza, w["w_a"], mode="nt", name=n + "dh1_a", plan=plan)
    if prev is None:
        to_chips("dh1_g", ["w_in"])
    else:
        to_chips("flash", ["w_in"], prefix=f"l{l - 1}_b_")
    dh1 = _mm(dzg, w["w_g"], mode="nt", name=n + "dh1_g", extras=(dh1,), epilogue=lambda acc, e: (acc + e,), plan=plan)
    if prev is None:
        dx, g["pre_mix_g"] = _pre_bwd(dh1, sv["x"], w["pre_mix_g"], dx1, n + "pre_mix")
        return dx, None, g
    dx, df_prev, g["pre_mix_g"], g_prev_post = _pre_bwd(dh1, sv["x"], w["pre_mix_g"], dx1, n + "pre_mix",
                                                         r_prev=prev[0], g_post_prev=prev[1])
    g["prev_post_mlp_g"] = g_prev_post
    return dx, df_prev, g


MIDSIZE = ("pool_proj", "conv_proj", "sgu_proj", "w_uq", "w_ukv", "attn_proj")
GATHER_STEPS = (
    ("l0_za", tuple((name, 0) for name in MIDSIZE)),
    ("l0_zg", (("w_out", 0),) + tuple((name, 1) for name in MIDSIZE)),
    ("l0_flash", (("w_in", 1), ("w_up", 0))),
    ("l0_proj", (("w_down", 0),)),
    ("l0_up", (("w_up", 1),)),
    ("l0_down", (("w_out", 1),)),
    ("l1_za", ()),
    ("l1_zg", ()),
    ("l1_flash", (("w_down", 1),)),
    ("l1_proj", ()),
)


def _plan_gathers(plan, gat, shards):
    first_half = {}
    for step, (kernel, keys) in enumerate(GATHER_STEPS):
        before = GATHER_STEPS[step - 1][1] if step else ()
        if not keys and not before:
            continue

        def make(keys=keys, before=before):
            ops = [_gather_pass(first_half[before])] if before else []
            return ops + ([_gather_own([shards[key] for key in keys])] if keys else [])

        def done(results, keys=keys, before=before):
            if before:
                gat.update(zip(before, results[0]))
            if keys:
                first_half[keys] = results[-1]

        plan.at(kernel, make, done)


def _local_step(x, positions, target, gat, small, shards, plan):
    tabs = _rope_tables(positions)
    _plan_gathers(plan, gat, shards)
    ws = [_layer_weights(gat, small, 0)]
    saved = []
    h = _pre_norm(x, ws[0]["pre_mix_g"], "l0_pre_mix")
    cur = x
    for l in range(DEPTH):
        x1, f, sv = _layer_fwd(cur, h, ws[l], tabs, l, plan)
        saved.append(sv)
        if l + 1 < DEPTH:
            ws.append(_layer_weights(gat, small, l + 1))
            cur, h = _post_pre(x1, f, ws[l]["post_mlp_g"], ws[l + 1]["pre_mix_g"], f"l{l}_post_mlp")
    top = DEPTH - 1
    dx, df, dg_post, loss = _final_loss(saved[top]["x1"], saved[top]["f"], ws[top]["post_mlp_g"], target, "loss")
    grads = [None] * DEPTH
    post_mlp = {top: dg_post}
    rs = {}
    for l in range(top, -1, -1):
        prev = (saved[l - 1]["f"], ws[l - 1]["post_mlp_g"]) if l > 0 else None
        dx, df, g = _layer_bwd(dx, df, saved[l], ws[l], tabs, l, prev, plan, rs)
        if l > 0:
            post_mlp[l - 1] = g.pop("prev_post_mlp_g")
        grads[l] = g
    for l in range(DEPTH):
        grads[l]["post_mlp_g"] = post_mlp[l]
    assert not plan.jobs, sorted(plan.jobs)
    return loss[0, 0], dx, grads, rs


def _small_grads(grads):
    small = {}

    def stack(fn):
        return jnp.stack([fn(grads[l]) for l in range(DEPTH)])

    for name, shape in SMALL:
        if name == "sgu_b":
            small[name] = stack(lambda g: g["sgu_b"][:, :, 0])
        else:
            small[name] = stack(lambda g, name=name, shape=shape: g[name].reshape(shape))
    small["conv_w"] = stack(lambda g: g["conv_w"][:CONV_WIDTH])
    return small


SMALL_ROWS = sum(DEPTH * math.prod(shape) // 128 for _, shape in SMALL)
CONVW_ROWS = DEPTH * CONV_WIDTH * BR // 128


def _pack_small(parts):
    return jnp.concatenate([parts[name].astype(F32).reshape(-1, 128) for name, _ in SMALL], axis=0)


def _unpack_small(buf):
    out, off = {}, 0
    for name, shape in SMALL:
        rows = DEPTH * math.prod(shape) // 128
        out[name] = buf[off:off + rows].reshape((DEPTH,) + shape)
        off += rows
    return out


def _mesh_pos():
    return lax.axis_index("x"), lax.axis_index("y"), lax.axis_index("c")


def _all_gather(shards, name):
    n = len(shards)

    def body(*refs):
        x_refs, out_refs = refs[:n], refs[n:2 * n]
        send_sems, recv_sems, local_sems = refs[2 * n:]
        x, y, c = _mesh_pos()
        me, sibling = (x, y, c), (x, y, 1 - c)
        chips = [(1 - x, y), (x, 1 - y), (1 - x, 1 - y)]

        def rows(t, px, py, pc):
            return out_refs[t].at[4 * px + 2 * py + pc]

        def copy(t, k, block, to, src=None):
            return pltpu.make_async_remote_copy(
                src_ref=rows(t, *block) if src is None else src, dst_ref=rows(t, *block), send_sem=send_sems.at[t, k],
                recv_sem=recv_sems.at[t, k], device_id=to, device_id_type=MESH)

        mine = [pltpu.make_async_copy(x_refs[t], rows(t, *me), local_sems.at[t]) for t in range(n)]
        for cp in mine:
            cp.start()
        first = []
        for t in range(n):
            first.append(copy(t, 0, me, sibling, src=x_refs[t]))
            first += [copy(t, 1 + j, me, (*chip, c), src=x_refs[t]) for j, chip in enumerate(chips)]
        for cp in first:
            cp.start()
        passed = []
        for t in range(n):
            for j, chip in enumerate(chips):
                copy(t, 1 + j, (*chip, c), me).wait_recv()
                passed.append(copy(t, 4 + j, (*chip, c), sibling))
                passed[-1].start()
        for t in range(n):
            copy(t, 0, sibling, me).wait_recv()
            for j, chip in enumerate(chips):
                copy(t, 4 + j, (*chip, 1 - c), me).wait_recv()
        for cp in first + passed:
            cp.wait_send()
        for cp in mine:
            cp.wait()

    hbm = pl.BlockSpec(memory_space=pl.ANY)
    return pl.pallas_call(
        body, name=name, out_shape=[jax.ShapeDtypeStruct((N_DEV,) + a.shape, a.dtype) for a in shards],
        in_specs=[hbm] * n, out_specs=[hbm] * n,
        scratch_shapes=[pltpu.SemaphoreType.DMA((n, 7)), pltpu.SemaphoreType.DMA((n, 7)),
                        pltpu.SemaphoreType.DMA((n,))])(*shards)


def _shard_tile(r, c_, cap_bytes=3 << 19):
    best = 0
    for t in range(16, r + 1, 16):
        if r % t == 0 and t * c_ * 4 <= cap_bytes:
            best = t
    if best or r * c_ * 4 <= cap_bytes:
        return (best or r), c_
    tc = max(t for t in range(128, c_ + 1, 128) if c_ % t == 0 and r * t * 4 <= cap_bytes)
    return r, tc


def _rs_chip_sum(g, t, name):
    _, r, c_ = g.shape
    tr, tc = _shard_tile(r, c_)
    core = lax.axis_index("c").astype(jnp.int32).reshape(1)

    def body(core_ref, g_ref, t_ref, p_ref):
        p_ref[...] = (g_ref[...].astype(F32) + t_ref[...].astype(F32)).astype(p_ref.dtype)

    return pl.pallas_call(
        body, name=name, out_shape=jax.ShapeDtypeStruct((4, r, c_), g.dtype),
        grid_spec=pltpu.PrefetchScalarGridSpec(
            num_scalar_prefetch=1, grid=(4, r // tr, c_ // tc),
            in_specs=[pl.BlockSpec((1, tr, tc), lambda k, i, j, core_ref: (2 * k + core_ref[0], i, j)),
                      pl.BlockSpec((1, tr, tc), lambda k, i, j, core_ref: (k, i, j))],
            out_specs=pl.BlockSpec((1, tr, tc), lambda k, i, j, core_ref: (k, i, j))),
        compiler_params=_params(("parallel", "parallel", "parallel")))(core, g, t)


def _adamw_math(w, g, m, v):
    m = ADAM_B1 * m + (1.0 - ADAM_B1) * g
    v = ADAM_B2 * v + (1.0 - ADAM_B2) * jnp.square(g)
    m_hat = m / (1.0 - ADAM_B1 ** ADAM_STEP)
    v_hat = v / (1.0 - ADAM_B2 ** ADAM_STEP)
    delta = -ADAM_LR * (m_hat / (jnp.sqrt(v_hat) + ADAM_EPS) + ADAM_WD * w)
    return delta, m, v


def _adamw_big(ps, ts, w, m, v, name):
    _, r, c_ = w.shape
    tr, tc = _shard_tile(r, c_)
    chip = (2 * lax.axis_index("x") + lax.axis_index("y")).astype(jnp.int32).reshape(1)

    def body(chip_ref, p0, t0, p1, t1, w_ref, m_ref, v_ref, g_out, d_out, m_out, v_out):
        def update(p_ref, t_ref):
            g = p_ref[0].astype(F32) + t_ref[0].astype(F32) + t_ref[1].astype(F32) + t_ref[2].astype(F32)
            g_out[0] = g
            d_out[0], m_out[0], v_out[0] = _adamw_math(w_ref[0], g, m_ref[0], v_ref[0])

        pl.when(pl.program_id(0) == 0)(functools.partial(update, p0, t0))
        pl.when(pl.program_id(0) == 1)(functools.partial(update, p1, t1))

    def grad_specs(layer):
        def at(l, i, j):
            return jnp.where(l == layer, i, 0), jnp.where(l == layer, j, 0)
        return [pl.BlockSpec((1, tr, tc), lambda l, i, j, chip_ref: (chip_ref[0], *at(l, i, j))),
                pl.BlockSpec((3, tr, tc), lambda l, i, j, chip_ref: (0, *at(l, i, j)))]

    nat = pl.BlockSpec((1, tr, tc), lambda l, i, j, chip_ref: (l, i, j))
    shape = jax.ShapeDtypeStruct(w.shape, F32)
    return pl.pallas_call(
        body, name=name, out_shape=[shape] * 4,
        grid_spec=pltpu.PrefetchScalarGridSpec(
            num_scalar_prefetch=1, grid=(DEPTH, r // tr, c_ // tc),
            in_specs=grad_specs(0) + grad_specs(1) + [nat, nat, nat], out_specs=[nat] * 4),
        compiler_params=_params(("parallel", "parallel", "parallel"), VMEM_BIG))(
            chip, ps[0], ts[0], ps[1], ts[1], w, m, v)


def _sum_devices(parts, name):
    _, r, c_ = parts.shape

    def body(p_ref, o_ref):
        acc = p_ref[0]
        for d in range(1, N_DEV):
            acc = acc + p_ref[d]
        o_ref[...] = acc

    return pl.pallas_call(body, name=name, out_shape=jax.ShapeDtypeStruct((r, c_), F32),
                          compiler_params=_params(None, VMEM_BIG))(parts)


def _adamw_small(w, g, m, v, name):
    def body(w_ref, g_ref, m_ref, v_ref, d_out, m_out, v_out):
        d_out[...], m_out[...], v_out[...] = _adamw_math(w_ref[...], g_ref[...], m_ref[...], v_ref[...])

    shape = jax.ShapeDtypeStruct(w.shape, F32)
    return pl.pallas_call(body, name=name, out_shape=[shape] * 3)(w, g, m, v)


def kernel(x, positions, pre_mix_g, w_in, pool_w, pool_scale, pool_proj, conv_w, conv_b, conv_norm_g, conv_norm_b, conv_proj, sgu_norm_g, sgu_norm_b, sgu_w, sgu_b, sgu_proj, q_norm_g, w_uq, kv_norm_g, w_ukv, attn_proj, w_out, post_mix_g, pre_mlp_g, w_up, w_down, post_mlp_g, loss_target, m_pre_mix_g, m_w_in, m_pool_w, m_pool_scale, m_pool_proj, m_conv_w, m_conv_b, m_conv_norm_g, m_conv_norm_b, m_conv_proj, m_sgu_norm_g, m_sgu_norm_b, m_sgu_w, m_sgu_b, m_sgu_proj, m_q_norm_g, m_w_uq, m_kv_norm_g, m_w_ukv, m_attn_proj, m_w_out, m_post_mix_g, m_pre_mlp_g, m_w_up, m_w_down, m_post_mlp_g, v_pre_mix_g, v_w_in, v_pool_w, v_pool_scale, v_pool_proj, v_conv_w, v_conv_b, v_conv_norm_g, v_conv_norm_b, v_conv_proj, v_sgu_norm_g, v_sgu_norm_b, v_sgu_w, v_sgu_b, v_sgu_proj, v_q_norm_g, v_w_uq, v_kv_norm_g, v_w_ukv, v_attn_proj, v_w_out, v_post_mix_g, v_pre_mlp_g, v_w_up, v_w_down, v_post_mlp_g):
    args = dict(locals())
    wts = {n: args[n] for n in WEIGHTS}
    mom1 = {n: args["m_" + n] for n in WEIGHTS}
    mom2 = {n: args["v_" + n] for n in WEIGHTS}
    dev = 4 * lax.axis_index("x") + 2 * lax.axis_index("y") + lax.axis_index("c")

    shards = {(name, l): wts[name][l].astype(BF16) for name, _, _ in BIG for l in range(DEPTH)}
    taps = jnp.pad(conv_w.reshape(-1, 128), ((0, 1), (0, 0)))
    gathered = _all_gather([shards[("w_in", 0)], taps], "gather_first")
    gat = {("w_in", 0): gathered[0]}
    taps = gathered[-1][:, :CONV_WIDTH].reshape(N_DEV, DEPTH, CONV_WIDTH, BR // N_DEV)
    small = {n: wts[n] for n, _ in SMALL}
    small["conv_w"] = taps.transpose(1, 2, 0, 3).reshape(DEPTH, CONV_WIDTH, BR)

    loss_part, grad_x, grads, rs = _local_step(x[0], positions, loss_target[0], gat, small, shards, _Plan())
    small_g = _small_grads(grads)
    loss = lax.psum(loss_part, ("x", "y", "c"))

    out = {"grad": {}, "delta": {}, "new_m": {}, "new_v": {}}
    for name, _, _ in BIG:
        flip = (lambda a: a.swapaxes(1, 2)) if name == "w_in" else (lambda a: a)
        res = _adamw_big([rs[(name, l)][0] for l in range(DEPTH)], [rs[(name, l)][1] for l in range(DEPTH)],
                         flip(wts[name]), flip(mom1[name]), flip(mom2[name]), "adamw_" + name)
        res = [flip(buf) for buf in res]
        for key, buf in zip(("grad", "delta", "new_m", "new_v"), res):
            out[key][name] = buf

    part = jnp.concatenate([_pack_small(small_g), small_g["conv_w"].reshape(-1, 128)], axis=0)
    total = _sum_devices(_all_gather([part], "gather_small_grads")[0], "sum_small_grads")
    g_small = total[:SMALL_ROWS]
    d_small, m_small, v_small = _adamw_small(_pack_small(wts), g_small, _pack_small(mom1), _pack_small(mom2), "adamw_small")
    for key, buf in (("grad", g_small), ("delta", d_small), ("new_m", m_small), ("new_v", v_small)):
        out[key].update(_unpack_small(buf))
    g_taps = total[SMALL_ROWS:].reshape(DEPTH, CONV_WIDTH, N_DEV, BR // N_DEV)
    g_taps = lax.dynamic_index_in_dim(g_taps, dev, axis=2, keepdims=False)
    flat = lambda a: a.reshape(-1, 128)
    d_taps, m_taps, v_taps = _adamw_small(flat(conv_w), flat(g_taps), flat(m_conv_w), flat(v_conv_w), "adamw_taps")
    for key, buf in (("grad", g_taps), ("delta", d_taps), ("new_m", m_taps), ("new_v", v_taps)):
        out[key]["conv_w"] = buf.reshape(conv_w.shape)

    return (loss, grad_x[None], *[out["grad"][n] for n in WEIGHTS], *[out["delta"][n] for n in WEIGHTS],
            *[out["new_m"][n] for n in WEIGHTS], *[out["new_v"][n] for n in WEIGHTS])
```

```python
import collections
import functools
import math

import jax
import jax.numpy as jnp
from jax import lax
from jax.experimental import pallas as pl
from jax.experimental.pallas import tpu as pltpu

F32 = jnp.float32
BF16 = jnp.bfloat16

D_MODEL = 2048
DEPTH = 2
EPS = 1e-6
N_BRANCH = 4
D_FF = 4 * D_MODEL
UP_SHARD = D_FF // 8
POOL_WINDOWS = (2, 4, 8, 16)
CONV_WIDTH = 31
CHUNK = 128
MLA_HEADS = 8
QK_NOPE = 128
QK_ROPE = 64
V_DIM = 128
ROPE_THETA = 10000.0
GDIM = 128
BR = 512
N_IN_A = 3648
ZA = 3712
N_GATE = N_BRANCH * D_MODEL
N_IN = N_IN_A + N_GATE
QW = 256
ACT_CAT = 3 * BR + MLA_HEADS * V_DIM
ATT_SCALE = (QK_NOPE + QK_ROPE) ** -0.5
NEG = -1e30

ADAM_LR = 0.001
ADAM_B1 = 0.9
ADAM_B2 = 0.999
ADAM_EPS = 1e-08
ADAM_WD = 0.01
ADAM_STEP = 10

N_DEV = 8
PACK_W = 1024
VMEM_BIG = 48 * 1024 * 1024
MESH = pl.DeviceIdType.MESH

BIG = (
    ("w_in", 1, (2048, 1480)),
    ("pool_proj", 1, (512, 256)),
    ("conv_proj", 1, (512, 256)),
    ("sgu_proj", 1, (512, 256)),
    ("w_uq", 1, (512, 192)),
    ("w_ukv", 1, (512, 256)),
    ("attn_proj", 1, (1024, 256)),
    ("w_out", 0, (256, 2048)),
    ("w_up", 1, (2048, 1024)),
    ("w_down", 0, (1024, 2048)),
)
SMALL = (
    ("pre_mix_g", (2048,)), ("pool_w", (4, 128, 128)), ("pool_scale", (512,)), ("conv_b", (512,)),
    ("conv_norm_g", (512,)), ("conv_norm_b", (512,)), ("sgu_norm_g", (512,)), ("sgu_norm_b", (512,)),
    ("sgu_w", (4, 128, 128)), ("sgu_b", (4, 128)), ("q_norm_g", (512,)), ("kv_norm_g", (512,)),
    ("post_mix_g", (2048,)), ("pre_mlp_g", (2048,)), ("post_mlp_g", (2048,)),
)
WEIGHTS = ("pre_mix_g", "w_in", "pool_w", "pool_scale", "pool_proj", "conv_w", "conv_b", "conv_norm_g", "conv_norm_b",
           "conv_proj", "sgu_norm_g", "sgu_norm_b", "sgu_w", "sgu_b", "sgu_proj", "q_norm_g", "w_uq", "kv_norm_g",
           "w_ukv", "attn_proj", "w_out", "post_mix_g", "pre_mlp_g", "w_up", "w_down", "post_mlp_g")


def _params(sem=None, vmem=None):
    return pltpu.CompilerParams(dimension_semantics=sem, vmem_limit_bytes=vmem)


def _tile(dim, pref):
    if dim <= pref:
        return dim
    best = 0
    for t in range(128, pref + 1, 128):
        if dim % t == 0:
            best = t
    return best if best >= 256 else dim


def _sigmoid(x):
    return 1.0 / (1.0 + jnp.exp(-x))


def _gelu(x):
    k = math.sqrt(2.0 / math.pi)
    return 0.5 * x * (1.0 + jnp.tanh(k * (x + 0.044715 * x * x * x)))


def _gelu_grad(x):
    k = math.sqrt(2.0 / math.pi)
    t = jnp.tanh(k * (x + 0.044715 * x * x * x))
    return 0.5 * (1.0 + t) + 0.5 * x * (1.0 - t * t) * k * (1.0 + 3.0 * 0.044715 * x * x)


def _rms(x, g):
    r = lax.rsqrt(jnp.mean(x * x, axis=-1, keepdims=True) + EPS)
    return x * r * g


def _rms_bwd(x, g, dy):
    r = lax.rsqrt(jnp.mean(x * x, axis=-1, keepdims=True) + EPS)
    dyg = dy * g
    dx = r * dyg - x * (r * r * r) * jnp.mean(dyg * x, axis=-1, keepdims=True)
    return dx, dy * x * r


def _ln_stats(x):
    mu = jnp.mean(x, axis=-1, keepdims=True)
    xc = x - mu
    r = lax.rsqrt(jnp.mean(xc * xc, axis=-1, keepdims=True) + EPS)
    return xc * r, r


def _ln_bwd(xh, r, g, dy):
    dxh = dy * g
    return r * (dxh - jnp.mean(dxh, axis=-1, keepdims=True) - xh * jnp.mean(dxh * xh, axis=-1, keepdims=True))


def _rot_half(x, width, off):
    n = x.shape[-1]
    lane = lax.broadcasted_iota(jnp.int32, x.shape, x.ndim - 1) % width
    return jnp.where(lane - off < QK_ROPE // 2, -pltpu.roll(x, n - QK_ROPE // 2, x.ndim - 1),
                     pltpu.roll(x, QK_ROPE // 2, x.ndim - 1))


def _colsum_into(ref, val, first):
    s = jnp.sum(val, axis=0, keepdims=True)

    @pl.when(first)
    def _():
        ref[...] = s

    @pl.when(jnp.logical_not(first))
    def _():
        ref[...] += s


Exchange = collections.namedtuple("Exchange", "inputs out_shapes aliases n_pairs n_local build")


class _Plan:
    def __init__(self):
        self.jobs = {}

    def at(self, kernel, make, done):
        self.jobs.setdefault(kernel, []).append((make, done))

    def take(self, kernel):
        return self.jobs.pop(kernel, [])


def _pcall(body, operands, *, name, grid, in_specs, out_specs, out_shape, scratch_shapes=(), sem=None, vmem=None,
           plan=None):
    jobs = plan.take(name) if plan is not None else []
    if not jobs:
        return pl.pallas_call(body, name=name, grid=grid, in_specs=list(in_specs), out_specs=list(out_specs),
                              out_shape=list(out_shape), scratch_shapes=list(scratch_shapes),
                              compiler_params=_params(sem, vmem))(*operands)
    made = [(make(), done) for make, done in jobs]
    comm = [op for ops, _ in made for op in ops]
    n_in, n_out, n_scr = len(in_specs), len(out_shape), len(scratch_shapes)
    c_in = [a for op in comm for a in op.inputs]
    c_out = [s for op in comm for s in op.out_shapes]
    sems, aliases, i_off, o_off = [], {}, n_in, n_out
    for op in comm:
        sems += [pltpu.SemaphoreType.DMA((op.n_pairs,)), pltpu.SemaphoreType.DMA((op.n_pairs,)),
                 pltpu.SemaphoreType.DMA((max(op.n_local, 1),))]
        for src, dst in op.aliases.items():
            aliases[i_off + src] = o_off + dst
        i_off += len(op.inputs)
        o_off += len(op.out_shapes)

    def carrier(*refs):
        ins, cins = refs[:n_in], refs[n_in:n_in + len(c_in)]
        base = n_in + len(c_in)
        outs, couts = refs[base:base + n_out], refs[base + n_out:base + n_out + len(c_out)]
        base += n_out + len(c_out)
        scr, csems = refs[base:base + n_scr], refs[base + n_scr:]
        ids = [pl.program_id(ax) for ax in range(len(grid))]
        first = functools.reduce(jnp.logical_and, [i == 0 for i in ids])
        last = functools.reduce(jnp.logical_and, [i == g - 1 for i, g in zip(ids, grid)])

        def pieces():
            res, ci, co = [], 0, 0
            for k, op in enumerate(comm):
                res.append(op.build(cins[ci:ci + len(op.inputs)], couts[co:co + len(op.out_shapes)],
                                    *csems[3 * k:3 * k + 3]))
                ci += len(op.inputs)
                co += len(op.out_shapes)
            return res

        @pl.when(first)
        def _():
            for sends, _, local in pieces():
                for cp in local + sends:
                    cp.start()

        body(*ins, *outs, *scr)

        @pl.when(last)
        def _():
            for sends, recvs, local in pieces():
                for cp in recvs:
                    cp.wait_recv()
                for cp in sends:
                    cp.wait_send()
                for cp in local:
                    cp.wait()

    hbm = pl.BlockSpec(memory_space=pl.ANY)
    res = pl.pallas_call(
        carrier, name=name, grid=grid, in_specs=list(in_specs) + [hbm] * len(c_in),
        out_specs=list(out_specs) + [hbm] * len(c_out), out_shape=list(out_shape) + c_out,
        scratch_shapes=list(scratch_shapes) + sems, input_output_aliases=aliases,
        compiler_params=_params(("arbitrary",) * len(grid), vmem))(*operands, *c_in)
    pos = n_out
    for ops, done in made:
        results = []
        for op in ops:
            results.append(list(res[pos:pos + len(op.out_shapes)]))
            pos += len(op.out_shapes)
        done(results)
    return list(res[:n_out])


def _block(ref, px, py, pc):
    return ref.at[4 * px + 2 * py + pc]


def _remote(src, dst, send_sems, recv_sems, k, to):
    return pltpu.make_async_remote_copy(src_ref=src, dst_ref=dst, send_sem=send_sems.at[k], recv_sem=recv_sems.at[k],
                                        device_id=to, device_id_type=MESH)


def _gather_own(shards):
    n = len(shards)

    def build(ins, outs, send_sems, recv_sems, local_sems):
        x, y, c = _mesh_pos()
        peers = [(x, y, 1 - c), (1 - x, y, c), (x, 1 - y, c), (1 - x, 1 - y, c)]
        sends, recvs, local = [], [], []
        for t in range(n):
            local.append(pltpu.make_async_copy(ins[t], _block(outs[t], x, y, c), local_sems.at[t]))
            for k, peer in enumerate(peers):
                sends.append(_remote(ins[t], _block(outs[t], x, y, c), send_sems, recv_sems, 4 * t + k, peer))
                recvs.append(_remote(ins[t], _block(outs[t], *peer), send_sems, recv_sems, 4 * t + k, peer))
        return sends, recvs, local

    return Exchange(list(shards), [jax.ShapeDtypeStruct((N_DEV,) + a.shape, a.dtype) for a in shards], {}, 4 * n, n, build)


def _gather_pass(bufs):
    n = len(bufs)

    def build(ins, outs, send_sems, recv_sems, local_sems):
        x, y, c = _mesh_pos()
        chips = [(1 - x, y), (x, 1 - y), (1 - x, 1 - y)]
        sends, recvs = [], []
        for t in range(n):
            for j, chip in enumerate(chips):
                mine, theirs = _block(outs[t], *chip, c), _block(outs[t], *chip, 1 - c)
                sends.append(_remote(mine, mine, send_sems, recv_sems, 3 * t + j, (x, y, 1 - c)))
                recvs.append(_remote(mine, theirs, send_sems, recv_sems, 3 * t + j, (x, y, 1 - c)))
        return sends, recvs, []

    return Exchange(list(bufs), [jax.ShapeDtypeStruct(a.shape, a.dtype) for a in bufs], {t: t for t in range(n)},
                    3 * n, 0, build)


def _scatter_sibling(gs):
    n = len(gs)

    def build(ins, outs, send_sems, recv_sems, local_sems):
        x, y, c = _mesh_pos()
        sends, recvs = [], []
        for t in range(n):
            for k in range(4):
                cp = _remote(ins[t].at[2 * k + 1 - c], outs[t].at[k], send_sems, recv_sems, 4 * t + k, (x, y, 1 - c))
                sends.append(cp)
                recvs.append(cp)
        return sends, recvs, []

    return Exchange(list(gs), [jax.ShapeDtypeStruct((4,) + g.shape[1:], g.dtype) for g in gs], {}, 4 * n, 0, build)


def _scatter_chips(ps):
    n = len(ps)

    def build(ins, outs, send_sems, recv_sems, local_sems):
        x, y, c = _mesh_pos()
        chips = [(1 - x, y), (x, 1 - y), (1 - x, 1 - y)]
        sends, recvs = [], []
        for t in range(n):
            for j, (cx, cy) in enumerate(chips):
                cp = _remote(ins[t].at[2 * cx + cy], outs[t].at[j], send_sems, recv_sems, 3 * t + j, (cx, cy, c))
                sends.append(cp)
                recvs.append(cp)
        return sends, recvs, []

    return Exchange(list(ps), [jax.ShapeDtypeStruct((3,) + p.shape[1:], p.dtype) for p in ps], {}, 3 * n, 0, build)


_DIMS = {"nn": ((1,), (0,)), "nt": ((1,), (1,)), "tn": ((0,), (0,))}


def _mm_call(a, b, *, mode, name, grid, kaxis, nk, a_spec, b_spec, o_specs, out_shape, acc_shape,
             extras=(), e_specs=(), epilogue=None, active=None, plan=None):
    ne, no = len(extras), len(out_shape)

    def body(a_ref, b_ref, *rest):
        e_refs, o_refs, acc_ref = rest[:ne], rest[ne:ne + no], rest[ne + no]
        ids = [pl.program_id(ax) for ax in range(len(grid))]
        k = ids[kaxis]

        def finish(acc):
            outs = (acc,) if epilogue is None else epilogue(acc, *[e[...] for e in e_refs])
            for o_ref, val in zip(o_refs, outs):
                o_ref[...] = val.astype(o_ref.dtype)

        def step():
            prod = lax.dot_general(a_ref[...], b_ref[...], (_DIMS[mode], ((), ())), preferred_element_type=F32)
            if nk == 1:
                finish(prod)
                return

            @pl.when(k == 0)
            def _():
                acc_ref[...] = prod

            @pl.when(k > 0)
            def _():
                acc_ref[...] += prod

        if active is None:
            step()
        else:
            pl.when(active(*ids))(step)
        if nk > 1:
            @pl.when(k == nk - 1)
            def _():
                finish(acc_ref[...])

    sem = tuple("arbitrary" if ax == kaxis else "parallel" for ax in range(len(grid)))
    scratch = pltpu.VMEM(acc_shape if nk > 1 else (8, 128), F32)
    return _pcall(body, (a, b, *extras), name=name, grid=grid, in_specs=[a_spec, b_spec, *e_specs],
                  out_specs=list(o_specs), out_shape=list(out_shape), scratch_shapes=[scratch],
                  sem=sem, vmem=VMEM_BIG, plan=plan)


def _mm(a, b, *, mode, name, out_dtypes=(F32,), extras=(), epilogue=None, tm=1024, tn=1024, tk=2048,
        mnk=None, b_spec=None, o_spec=None, out_struct=None, plan=None):
    if mnk is not None:
        m, n, k = mnk
    elif mode == "nn":
        (m, k), (_, n) = a.shape, b.shape
    elif mode == "nt":
        (m, k), (n, _) = a.shape, b.shape
    else:
        (k, m), (_, n) = a.shape, b.shape
    if mode == "tn":
        tk = 2 * tk
    tm, tn, tk = _tile(m, tm), _tile(n, tn), _tile(k, tk)
    if tn > 2048:
        tm, tk = _tile(m, 512), _tile(k, 512)
    if tk > 2048 and mode != "tn":
        tm, tn = _tile(m, 512), _tile(n, 512)
    nk = k // tk
    if mode == "tn":
        a_spec = pl.BlockSpec((tk, tm), lambda i, j, kk: (kk, i))
    else:
        a_spec = pl.BlockSpec((tm, tk), lambda i, j, kk: (i, kk))
    if b_spec is not None:
        b_spec = b_spec(tn, tk)
    elif mode == "nt":
        b_spec = pl.BlockSpec((tn, tk), lambda i, j, kk: (j, kk))
    else:
        b_spec = pl.BlockSpec((tk, tn), lambda i, j, kk: (kk, j))

    def e_spec(e):
        if e.shape[1] == tn and n != tn:
            return pl.BlockSpec((tm, tn), lambda i, j, kk: (i, 0))
        return pl.BlockSpec((tm, tn), lambda i, j, kk: (i, j))

    e_specs = [e_spec(e) for e in extras]
    if o_spec is not None:
        o_specs = [o_spec(tm, tn)]
        out_shape = [out_struct]
    else:
        o_specs = [pl.BlockSpec((tm, tn), lambda i, j, kk: (i, j)) for _ in out_dtypes]
        out_shape = [jax.ShapeDtypeStruct((m, n), dt) for dt in out_dtypes]
    outs = _mm_call(a, b, mode=mode, name=name, grid=(m // tm, n // tn, nk), kaxis=2, nk=nk, a_spec=a_spec,
                    b_spec=b_spec, o_specs=o_specs, out_shape=out_shape, acc_shape=(tm, tn), extras=extras,
                    e_specs=e_specs, epilogue=epilogue, plan=plan)
    return outs[0] if len(outs) == 1 else outs


def _branch_of(kb):
    return jnp.minimum(kb, N_BRANCH - 1)


def _proj_merge_fwd(act_cat, proj_cat, zg, name, plan=None):
    s = act_cat.shape[0]
    tm, tn = _tile(s, 1024), 1024
    nj = D_MODEL // tn
    last = N_BRANCH - 1

    def kb(b, k):
        return jnp.where(b < last, b, last + k)

    def body(a_ref, b_ref, z_ref, y_ref, m_ref, acc_ref, sum_ref):
        b, k = pl.program_id(2), pl.program_id(3)

        @pl.when(jnp.logical_or(b == last, k == 0))
        def _():
            prod = jnp.dot(a_ref[...], b_ref[...], preferred_element_type=F32)

            @pl.when(k == 0)
            def _():
                acc_ref[...] = prod

            @pl.when(k > 0)
            def _():
                acc_ref[...] += prod

        @pl.when(k == 1)
        def _():
            y = acc_ref[...]
            y_ref[...] = y.astype(BF16)
            gated = _sigmoid(z_ref[...].astype(F32)) * y

            @pl.when(b == 0)
            def _():
                sum_ref[...] = gated

            @pl.when(b > 0)
            def _():
                sum_ref[...] += gated

            @pl.when(b == last)
            def _():
                m_ref[...] = sum_ref[...].astype(BF16)

    wide = pl.BlockSpec((tm, tn), lambda i, j, b, k: (i, b * nj + j))
    y, merged = _pcall(
        body, (act_cat, proj_cat, zg), name=name, grid=(s // tm, nj, N_BRANCH, 2),
        in_specs=[pl.BlockSpec((tm, BR), lambda i, j, b, k: (i, kb(b, k))),
                  pl.BlockSpec((BR, tn), lambda i, j, b, k: (kb(b, k), j)), wide],
        out_specs=[wide, pl.BlockSpec((tm, tn), lambda i, j, b, k: (i, j))],
        out_shape=[jax.ShapeDtypeStruct((s, N_GATE), BF16), jax.ShapeDtypeStruct((s, D_MODEL), BF16)],
        scratch_shapes=[pltpu.VMEM((tm, tn), F32), pltpu.VMEM((tm, tn), F32)],
        sem=("parallel", "parallel", "arbitrary", "arbitrary"), vmem=VMEM_BIG, plan=plan)
    return y, merged


def _proj_bwd_act(dy, proj_cat, name, plan=None):
    s = dy.shape[0]
    tm, tk = _tile(s, 1024), D_MODEL
    nkk = D_MODEL // tk
    nkb = ACT_CAT // BR
    out = _mm_call(
        dy, proj_cat, mode="nt", name=name, grid=(s // tm, nkb, nkk), kaxis=2, nk=nkk,
        a_spec=pl.BlockSpec((tm, tk), lambda i, kb, k: (i, _branch_of(kb) * nkk + k)),
        b_spec=pl.BlockSpec((BR, tk), lambda i, kb, k: (kb, k)),
        o_specs=[pl.BlockSpec((tm, BR), lambda i, kb, k: (i, kb))],
        out_shape=[jax.ShapeDtypeStruct((s, ACT_CAT), F32)], acc_shape=(tm, BR), plan=plan)
    return out[0]


def _proj_bwd_w(act_cat, dy, name):
    s = dy.shape[0]
    tms, tn = _tile(s, 2048), 1024
    nj = D_MODEL // tn
    nkb = ACT_CAT // BR
    nm = s // tms
    out = _mm_call(
        act_cat, dy, mode="tn", name=name, grid=(nkb, nj, nm), kaxis=2, nk=nm,
        a_spec=pl.BlockSpec((tms, BR), lambda kb, j, m: (m, kb)),
        b_spec=pl.BlockSpec((tms, tn), lambda kb, j, m: (m, _branch_of(kb) * nj + j)),
        o_specs=[pl.BlockSpec((BR, tn), lambda kb, j, m: (kb, j))],
        out_shape=[jax.ShapeDtypeStruct((ACT_CAT, D_MODEL), BF16)], acc_shape=(BR, tn))
    return out[0]


def _row_specs(ts, n_full, n_vec):
    return ([pl.BlockSpec((ts, D_MODEL), lambda i: (i, 0))] * n_full
            + [pl.BlockSpec((1, D_MODEL), lambda i: (0, 0))] * n_vec)


def _pre_norm(x, g, name):
    s = x.shape[0]
    ts = _tile(s, 256)

    def body(x_ref, g_ref, h_ref):
        h_ref[...] = _rms(x_ref[...], g_ref[...]).astype(BF16)

    return pl.pallas_call(body, name=name, grid=(s // ts,), in_specs=_row_specs(ts, 1, 1),
                          out_specs=pl.BlockSpec((ts, D_MODEL), lambda i: (i, 0)),
                          out_shape=jax.ShapeDtypeStruct((s, D_MODEL), BF16), compiler_params=_params(("parallel",)))(x, g)


def _post_pre(x, r, g_post, g_next, name):
    s = x.shape[0]
    ts = _tile(s, 256)

    def body(x_ref, r_ref, gp_ref, gn_ref, xn_ref, h_ref):
        xn = x_ref[...] + _rms(r_ref[...], gp_ref[...])
        xn_ref[...] = xn
        h_ref[...] = _rms(xn, gn_ref[...]).astype(BF16)

    spec = pl.BlockSpec((ts, D_MODEL), lambda i: (i, 0))
    return pl.pallas_call(body, name=name, grid=(s // ts,), in_specs=_row_specs(ts, 2, 2), out_specs=[spec, spec],
                          out_shape=[jax.ShapeDtypeStruct((s, D_MODEL), F32), jax.ShapeDtypeStruct((s, D_MODEL), BF16)],
                          compiler_params=_params(("parallel",)))(x, r, g_post, g_next)


def _final_loss(x, r, g_post, target, name):
    s = x.shape[0]
    ts = _tile(s, 256)

    def body(x_ref, r_ref, gp_ref, t_ref, dy_ref, dr_ref, dg_ref, loss_ref):
        first = pl.program_id(0) == 0
        rv, gp = r_ref[...], gp_ref[...]
        diff = x_ref[...] + _rms(rv, gp) - t_ref[...]
        part = 0.5 * jnp.sum(jnp.mean(diff * diff, axis=-1, keepdims=True), axis=0, keepdims=True)
        dy = diff * (1.0 / D_MODEL)
        dy_ref[...] = dy
        dr, dg_rows = _rms_bwd(rv, gp, dy)
        dr_ref[...] = dr.astype(BF16)
        _colsum_into(dg_ref, dg_rows, first)
        _colsum_into(loss_ref, jnp.broadcast_to(part, (1, 128)), first)

    spec = pl.BlockSpec((ts, D_MODEL), lambda i: (i, 0))
    vec = pl.BlockSpec((1, D_MODEL), lambda i: (0, 0))
    return pl.pallas_call(
        body, name=name, grid=(s // ts,), in_specs=[spec, spec, vec, spec],
        out_specs=[spec, spec, vec, pl.BlockSpec((1, 128), lambda i: (0, 0))],
        out_shape=[jax.ShapeDtypeStruct((s, D_MODEL), F32), jax.ShapeDtypeStruct((s, D_MODEL), BF16),
                   jax.ShapeDtypeStruct((1, D_MODEL), F32), jax.ShapeDtypeStruct((1, 128), F32)],
        compiler_params=_params(("arbitrary",)))(x, r, g_post, target)


def _pre_bwd(dh, x, g_pre, dx_res, name, r_prev=None, g_post_prev=None):
    s = x.shape[0]
    ts = _tile(s, 256)
    chain = r_prev is not None

    def body(*refs):
        if chain:
            dh_ref, x_ref, res_ref, r_ref, g_ref, gp_ref, dx_ref, dr_ref, dg_ref, dgp_ref = refs
        else:
            dh_ref, x_ref, res_ref, g_ref, dx_ref, dg_ref = refs
        first = pl.program_id(0) == 0
        dxn, dg_rows = _rms_bwd(x_ref[...], g_ref[...], dh_ref[...])
        dx = res_ref[...] + dxn
        dx_ref[...] = dx
        _colsum_into(dg_ref, dg_rows, first)
        if chain:
            dr, dgp_rows = _rms_bwd(r_ref[...], gp_ref[...], dx)
            dr_ref[...] = dr.astype(BF16)
            _colsum_into(dgp_ref, dgp_rows, first)

    spec = pl.BlockSpec((ts, D_MODEL), lambda i: (i, 0))
    vec = pl.BlockSpec((1, D_MODEL), lambda i: (0, 0))
    full = jax.ShapeDtypeStruct((s, D_MODEL), F32)
    vshape = jax.ShapeDtypeStruct((1, D_MODEL), F32)
    if chain:
        return pl.pallas_call(
            body, name=name, grid=(s // ts,), in_specs=[spec] * 4 + [vec] * 2, out_specs=[spec, spec, vec, vec],
            out_shape=[full, jax.ShapeDtypeStruct((s, D_MODEL), BF16), vshape, vshape],
            compiler_params=_params(("arbitrary",)))(dh, x, dx_res, r_prev, g_pre, g_post_prev)
    return pl.pallas_call(
        body, name=name, grid=(s // ts,), in_specs=[spec] * 3 + [vec], out_specs=[spec, vec],
        out_shape=[full, vshape], compiler_params=_params(("arbitrary",)))(dh, x, dx_res, g_pre)


def _merge_bwd(dm, y, zg, name, plan=None):
    s = y.shape[0]
    ts, tc = _tile(s, 512), 1024
    nj = D_MODEL // tc

    def body(dm_ref, y_ref, z_ref, dy_ref, dz_ref):
        g = _sigmoid(z_ref[...].astype(F32))
        d = dm_ref[...]
        dy_ref[...] = (d * g).astype(BF16)
        dz_ref[...] = (d * y_ref[...] * g * (1.0 - g)).astype(BF16)

    blk = pl.BlockSpec((ts, tc), lambda i, j, b: (i, b * nj + j))
    shape = jax.ShapeDtypeStruct((s, N_GATE), BF16)
    return _pcall(body, (dm, y, zg), name=name, grid=(s // ts, nj, N_BRANCH),
                  in_specs=[pl.BlockSpec((ts, tc), lambda i, j, b: (i, j)), blk, blk], out_specs=[blk, blk],
                  out_shape=[shape, shape], sem=("parallel", "parallel", "parallel"), vmem=VMEM_BIG, plan=plan)


POOL_HALO = 16


def _pool_windows(ext_ref, ts, first_row):
    outs = []
    t = first_row + lax.broadcasted_iota(jnp.int32, (ts, GDIM), 0)
    for gi, w in enumerate(POOL_WINDOWS):
        cols = pl.ds(gi * GDIM, GDIM)
        acc = ext_ref[pl.ds(POOL_HALO, ts), cols]
        cur = acc
        for k in range(1, w):
            acc = acc + ext_ref[pl.ds(POOL_HALO - k, ts), cols]
        cnt = jnp.minimum(t + 1, w).astype(F32)
        outs.append(acc / cnt - cur)
    return outs


def _pool_fwd(za, pool_w, pool_scale, name):
    s = za.shape[0]
    ts = _tile(s, 512)
    hb = ts // POOL_HALO

    def body(a_ref, halo_ref, w_ref, sc_ref, o_ref, ext_ref):
        i = pl.program_id(0)
        ext_ref[pl.ds(0, POOL_HALO), :] = jnp.where(i > 0, halo_ref[...], 0.0)
        ext_ref[pl.ds(POOL_HALO, ts), :] = a_ref[...]
        pooled = _pool_windows(ext_ref, ts, i * ts)
        for gi in range(len(POOL_WINDOWS)):
            mixed = jnp.dot(pooled[gi].astype(BF16), w_ref[gi].astype(BF16), preferred_element_type=F32)
            o_ref[:, pl.ds(gi * GDIM, GDIM)] = (mixed * sc_ref[:, pl.ds(gi * GDIM, GDIM)]).astype(BF16)

    return pl.pallas_call(
        body, name=name, grid=(s // ts,),
        in_specs=[pl.BlockSpec((ts, BR), lambda i: (i, 0)),
                  pl.BlockSpec((POOL_HALO, BR), lambda i: (jnp.maximum(i * hb - 1, 0), 0)),
                  pl.BlockSpec((4, GDIM, GDIM), lambda i: (0, 0, 0)), pl.BlockSpec((1, BR), lambda i: (0, 0))],
        out_specs=pl.BlockSpec((ts, BR), lambda i: (i, 0)), out_shape=jax.ShapeDtypeStruct((s, BR), BF16),
        scratch_shapes=[pltpu.VMEM((POOL_HALO + ts, BR), F32)], compiler_params=_params(("parallel",)))(
            za, za, pool_w, pool_scale)


def _pool_bwd(dact, za, pool_w, pool_scale, name):
    s = za.shape[0]
    ts = _tile(s, 512)
    hb = ts // POOL_HALO
    n_t = s // ts

    def body(d_ref, dhalo_ref, a_ref, halo_ref, w_ref, sc_ref, dz_ref, dw_ref, dsc_ref, ext_ref, f_ref):
        i = pl.program_id(0)
        first = i == 0
        ext_ref[pl.ds(0, POOL_HALO), :] = jnp.where(i > 0, halo_ref[...], 0.0)
        ext_ref[pl.ds(POOL_HALO, ts), :] = a_ref[...]
        pooled = _pool_windows(ext_ref, ts, i * ts)
        d_tile = d_ref[...]
        d_next = jnp.where(i < n_t - 1, dhalo_ref[...], 0.0)
        t_ext = i * ts + lax.broadcasted_iota(jnp.int32, (ts + POOL_HALO, GDIM), 0)
        dsc = []
        for gi, w in enumerate(POOL_WINDOWS):
            cols = pl.ds(gi * GDIM, GDIM)
            wg = w_ref[gi].astype(BF16)
            sc = sc_ref[:, cols]
            pg = pooled[gi].astype(BF16)
            mixed = jnp.dot(pg, wg, preferred_element_type=F32)
            dsc.append(jnp.sum(d_tile[:, gi * GDIM:(gi + 1) * GDIM] * mixed, axis=0, keepdims=True))
            dmix = jnp.concatenate([d_tile[:, gi * GDIM:(gi + 1) * GDIM], d_next[:, gi * GDIM:(gi + 1) * GDIM]], axis=0) * sc
            dmix = dmix.astype(BF16)
            dwg = lax.dot_general(pg, dmix[:ts], (((0,), (0,)), ((), ())), preferred_element_type=F32)

            @pl.when(first)
            def _():
                dw_ref[gi] = dwg

            @pl.when(jnp.logical_not(first))
            def _():
                dw_ref[gi] += dwg

            dpool = lax.dot_general(dmix, wg, (((1,), (1,)), ((), ())), preferred_element_type=F32)
            f_ref[:, cols] = dpool / jnp.minimum(t_ext + 1, w).astype(F32)
            acc = f_ref[pl.ds(0, ts), cols]
            for k in range(1, w):
                acc = acc + f_ref[pl.ds(k, ts), cols]
            dz_ref[:, cols] = (acc - dpool[:ts]).astype(BF16)
        dsc_all = jnp.concatenate(dsc, axis=1)

        @pl.when(first)
        def _():
            dsc_ref[...] = dsc_all

        @pl.when(jnp.logical_not(first))
        def _():
            dsc_ref[...] += dsc_all

    n_hb = s // POOL_HALO
    return pl.pallas_call(
        body, name=name, grid=(n_t,),
        in_specs=[pl.BlockSpec((ts, BR), lambda i: (i, 0)),
                  pl.BlockSpec((POOL_HALO, BR), lambda i: (jnp.minimum((i + 1) * hb, n_hb - 1), 0)),
                  pl.BlockSpec((ts, BR), lambda i: (i, 0)),
                  pl.BlockSpec((POOL_HALO, BR), lambda i: (jnp.maximum(i * hb - 1, 0), 0)),
                  pl.BlockSpec((4, GDIM, GDIM), lambda i: (0, 0, 0)), pl.BlockSpec((1, BR), lambda i: (0, 0))],
        out_specs=[pl.BlockSpec((ts, BR), lambda i: (i, 0)), pl.BlockSpec((4, GDIM, GDIM), lambda i: (0, 0, 0)),
                   pl.BlockSpec((1, BR), lambda i: (0, 0))],
        out_shape=[jax.ShapeDtypeStruct((s, BR), BF16), jax.ShapeDtypeStruct((4, GDIM, GDIM), F32),
                   jax.ShapeDtypeStruct((1, BR), F32)],
        scratch_shapes=[pltpu.VMEM((POOL_HALO + ts, BR), F32), pltpu.VMEM((ts + POOL_HALO, BR), F32)],
        compiler_params=_params(("arbitrary",)))(dact, dact, za, za, pool_w, pool_scale)


CONV_HALO = 32
CONV_LEAD = CONV_HALO - (CONV_WIDTH - 1)


def _conv_fwd(za, conv_w, conv_b, ng, nb, name):
    s = za.shape[0]
    ts = _tile(s, 512)
    hb = ts // CONV_HALO

    def body(a_ref, g_ref, ah_ref, gh_ref, w_ref, b_ref, ng_ref, nb_ref, yc_ref, act_ref, ext_ref):
        i = pl.program_id(0)
        ext_ref[pl.ds(0, CONV_HALO), :] = jnp.where(i > 0, ah_ref[...] * _sigmoid(gh_ref[...]), 0.0)
        ext_ref[pl.ds(CONV_HALO, ts), :] = a_ref[...] * _sigmoid(g_ref[...])
        acc = jnp.zeros((ts, BR), F32) + b_ref[...]
        for k in range(CONV_WIDTH):
            acc = acc + w_ref[pl.ds(k, 1), :] * ext_ref[pl.ds(CONV_LEAD + k, ts), :]
        yc_ref[...] = acc
        xh, _ = _ln_stats(acc)
        ln = xh * ng_ref[...] + nb_ref[...]
        act_ref[...] = (ln * _sigmoid(ln)).astype(BF16)

    tile = lambda c: pl.BlockSpec((ts, BR), lambda i: (i, c))
    halo = lambda c: pl.BlockSpec((CONV_HALO, BR), lambda i: (jnp.maximum(i * hb - 1, 0), c))
    vec = pl.BlockSpec((1, BR), lambda i: (0, 0))
    return pl.pallas_call(
        body, name=name, grid=(s // ts,),
        in_specs=[tile(1), tile(2), halo(1), halo(2), pl.BlockSpec((CONV_WIDTH, BR), lambda i: (0, 0)), vec, vec, vec],
        out_specs=[pl.BlockSpec((ts, BR), lambda i: (i, 0))] * 2,
        out_shape=[jax.ShapeDtypeStruct((s, BR), F32), jax.ShapeDtypeStruct((s, BR), BF16)],
        scratch_shapes=[pltpu.VMEM((CONV_HALO + ts, BR), F32)], compiler_params=_params(("parallel",)))(
            za, za, za, za, conv_w, conv_b, ng, nb)


def _conv_bwd_norm(dact, yc, ng, nb, name):
    s = yc.shape[0]
    ts = _tile(s, 512)

    def body(d_ref, y_ref, ng_ref, nb_ref, dy_ref, db_ref, dng_ref, dnb_ref):
        first = pl.program_id(0) == 0
        xh, r = _ln_stats(y_ref[...])
        g = ng_ref[...]
        ln = xh * g + nb_ref[...]
        sg = _sigmoid(ln)
        dln = d_ref[...] * sg * (1.0 + ln * (1.0 - sg))
        dy = _ln_bwd(xh, r, g, dln)
        dy_ref[...] = dy
        _colsum_into(db_ref, dy, first)
        _colsum_into(dng_ref, dln * xh, first)
        _colsum_into(dnb_ref, dln, first)

    vec = pl.BlockSpec((1, BR), lambda i: (0, 0))
    vshape = jax.ShapeDtypeStruct((1, BR), F32)
    return pl.pallas_call(
        body, name=name, grid=(s // ts,),
        in_specs=[pl.BlockSpec((ts, BR), lambda i: (i, 1)), pl.BlockSpec((ts, BR), lambda i: (i, 0)), vec, vec],
        out_specs=[pl.BlockSpec((ts, BR), lambda i: (i, 0)), vec, vec, vec],
        out_shape=[jax.ShapeDtypeStruct((s, BR), F32), vshape, vshape, vshape],
        compiler_params=_params(("arbitrary",)))(dact, yc, ng, nb)


def _conv_bwd_taps(dyc, za, conv_w, name):
    s = za.shape[0]
    ts = _tile(s, 512)
    hb = ts // CONV_HALO
    n_t = s // ts
    n_hb = s // CONV_HALO

    def body(d_ref, dh_ref, a_ref, g_ref, ah_ref, gh_ref, w_ref, dz_ref, dw_ref, ext_ref, f_ref):
        i = pl.program_id(0)
        first = i == 0
        a, sg = a_ref[...], _sigmoid(g_ref[...])
        ext_ref[pl.ds(0, CONV_HALO), :] = jnp.where(i > 0, ah_ref[...] * _sigmoid(gh_ref[...]), 0.0)
        ext_ref[pl.ds(CONV_HALO, ts), :] = a * sg
        d = d_ref[...]
        f_ref[pl.ds(0, ts), :] = d
        f_ref[pl.ds(ts, CONV_HALO), :] = jnp.where(i < n_t - 1, dh_ref[...], 0.0)
        dglu = jnp.zeros((ts, BR), F32)
        rows = []
        for k in range(CONV_WIDTH):
            rows.append(jnp.sum(d * ext_ref[pl.ds(CONV_LEAD + k, ts), :], axis=0, keepdims=True))
            dglu = dglu + w_ref[pl.ds(k, 1), :] * f_ref[pl.ds(CONV_WIDTH - 1 - k, ts), :]
        rows.append(jnp.zeros((1, BR), F32))
        dw = jnp.concatenate(rows, axis=0)

        @pl.when(first)
        def _():
            dw_ref[...] = dw

        @pl.when(jnp.logical_not(first))
        def _():
            dw_ref[...] += dw

        dz_ref[:, pl.ds(0, BR)] = (dglu * sg).astype(BF16)
        dz_ref[:, pl.ds(BR, BR)] = (dglu * a * sg * (1.0 - sg)).astype(BF16)

    tile = lambda c: pl.BlockSpec((ts, BR), lambda i: (i, c))
    halo = lambda c: pl.BlockSpec((CONV_HALO, BR), lambda i: (jnp.maximum(i * hb - 1, 0), c))
    return pl.pallas_call(
        body, name=name, grid=(n_t,),
        in_specs=[pl.BlockSpec((ts, BR), lambda i: (i, 0)),
                  pl.BlockSpec((CONV_HALO, BR), lambda i: (jnp.minimum((i + 1) * hb, n_hb - 1), 0)),
                  tile(1), tile(2), halo(1), halo(2), pl.BlockSpec((CONV_WIDTH, BR), lambda i: (0, 0))],
        out_specs=[pl.BlockSpec((ts, 2 * BR), lambda i: (i, 0)), pl.BlockSpec((CONV_WIDTH + 1, BR), lambda i: (0, 0))],
        out_shape=[jax.ShapeDtypeStruct((s, 2 * BR), BF16), jax.ShapeDtypeStruct((CONV_WIDTH + 1, BR), F32)],
        scratch_shapes=[pltpu.VMEM((CONV_HALO + ts, BR), F32), pltpu.VMEM((ts + CONV_HALO, BR), F32)],
        compiler_params=_params(("arbitrary",)))(dyc, dyc, za, za, za, za, conv_w)


def _tril(w):
    r = lax.broadcasted_iota(jnp.int32, (CHUNK, CHUNK), 0)
    c = lax.broadcasted_iota(jnp.int32, (CHUNK, CHUNK), 1)
    return jnp.where(c <= r, w, 0.0)


def _sgu_fwd(za, ng, nb, sgu_w, bias_b, name):
    s = za.shape[0]
    ts = _tile(s, 512)

    def body(u_ref, v_ref, ng_ref, nb_ref, w_ref, b_ref, o_ref):
        u = _gelu(u_ref[...])
        xh, _ = _ln_stats(_gelu(v_ref[...]))
        vln = (xh * ng_ref[...] + nb_ref[...]).astype(BF16)
        for gi in range(4):
            wg = _tril(w_ref[gi]).astype(BF16)
            for n in range(ts // CHUNK):
                blk = vln[n * CHUNK:(n + 1) * CHUNK, gi * GDIM:(gi + 1) * GDIM]
                sp = jnp.dot(wg, blk, preferred_element_type=F32) + b_ref[gi]
                o_ref[pl.ds(n * CHUNK, CHUNK), pl.ds(gi * GDIM, GDIM)] = (
                    u[n * CHUNK:(n + 1) * CHUNK, gi * GDIM:(gi + 1) * GDIM] * sp).astype(BF16)

    vec = pl.BlockSpec((1, BR), lambda i: (0, 0))
    cube = pl.BlockSpec((4, CHUNK, GDIM), lambda i: (0, 0, 0))
    return pl.pallas_call(
        body, name=name, grid=(s // ts,),
        in_specs=[pl.BlockSpec((ts, BR), lambda i: (i, 3)), pl.BlockSpec((ts, BR), lambda i: (i, 4)), vec, vec, cube, cube],
        out_specs=pl.BlockSpec((ts, BR), lambda i: (i, 0)), out_shape=jax.ShapeDtypeStruct((s, BR), BF16),
        compiler_params=_params(("parallel",)))(za, za, ng, nb, sgu_w, bias_b)


def _sgu_bwd(dact, za, ng, nb, sgu_w, bias_b, name):
    s = za.shape[0]
    ts = _tile(s, 512)

    def body(d_ref, u_ref, v_ref, ng_ref, nb_ref, w_ref, b_ref, dz_ref, dw_ref, db_ref, dng_ref, dnb_ref, dv_ref):
        first = pl.program_id(0) == 0
        u_raw, v_raw = u_ref[...], v_ref[...]
        u = _gelu(u_raw)
        xh, r = _ln_stats(_gelu(v_raw))
        g = ng_ref[...]
        vln = (xh * g + nb_ref[...]).astype(BF16)
        d = d_ref[...]
        dsp = d * u
        dsp16 = dsp.astype(BF16)
        for gi in range(4):
            wg = _tril(w_ref[gi]).astype(BF16)
            dwg = jnp.zeros((CHUNK, CHUNK), F32)
            dbg = jnp.zeros((CHUNK, 1), F32)
            for n in range(ts // CHUNK):
                rows, cols = slice(n * CHUNK, (n + 1) * CHUNK), slice(gi * GDIM, (gi + 1) * GDIM)
                blk = vln[rows, cols]
                sp = jnp.dot(wg, blk, preferred_element_type=F32) + b_ref[gi]
                dz_ref[pl.ds(n * CHUNK, CHUNK), pl.ds(gi * GDIM, GDIM)] = (
                    d[rows, cols] * sp * _gelu_grad(u_raw[rows, cols])).astype(BF16)
                dv_ref[pl.ds(n * CHUNK, CHUNK), pl.ds(gi * GDIM, GDIM)] = lax.dot_general(
                    wg, dsp16[rows, cols], (((0,), (0,)), ((), ())), preferred_element_type=F32)
                dwg = dwg + lax.dot_general(dsp16[rows, cols], blk, (((1,), (1,)), ((), ())), preferred_element_type=F32)
                dbg = dbg + jnp.sum(dsp[rows, cols], axis=1, keepdims=True)
            dwg = _tril(dwg)

            @pl.when(first)
            def _():
                dw_ref[gi] = dwg
                db_ref[gi] = dbg

            @pl.when(jnp.logical_not(first))
            def _():
                dw_ref[gi] += dwg
                db_ref[gi] += dbg

        dvln = dv_ref[...]
        dz_ref[:, pl.ds(BR, BR)] = (_ln_bwd(xh, r, g, dvln) * _gelu_grad(v_raw)).astype(BF16)
        _colsum_into(dng_ref, dvln * xh, first)
        _colsum_into(dnb_ref, dvln, first)

    vec = pl.BlockSpec((1, BR), lambda i: (0, 0))
    cube = pl.BlockSpec((4, CHUNK, GDIM), lambda i: (0, 0, 0))
    vshape = jax.ShapeDtypeStruct((1, BR), F32)
    return pl.pallas_call(
        body, name=name, grid=(s // ts,),
        in_specs=[pl.BlockSpec((ts, BR), lambda i: (i, 2)), pl.BlockSpec((ts, BR), lambda i: (i, 3)),
                  pl.BlockSpec((ts, BR), lambda i: (i, 4)), vec, vec, cube, cube],
        out_specs=[pl.BlockSpec((ts, 2 * BR), lambda i: (i, 0)), cube, pl.BlockSpec((4, CHUNK, 1), lambda i: (0, 0, 0)),
                   vec, vec],
        out_shape=[jax.ShapeDtypeStruct((s, 2 * BR), BF16), jax.ShapeDtypeStruct((4, CHUNK, CHUNK), F32),
                   jax.ShapeDtypeStruct((4, CHUNK, 1), F32), vshape, vshape],
        scratch_shapes=[pltpu.VMEM((ts, BR), F32)], compiler_params=_params(("arbitrary",)))(
            dact, za, za, ng, nb, sgu_w, bias_b)


KR_BLOCK = 3584 // 128


def _mla_prep(za, qg, kvg, ck, sk, name):
    s = za.shape[0]
    ts = _tile(s, 512)

    def body(cq_ref, ckv_ref, kr_ref, qg_ref, kvg_ref, c_ref, s_ref, qn_ref, kvn_ref, krr_ref):
        qn_ref[...] = _rms(cq_ref[...], qg_ref[...]).astype(BF16)
        kvn_ref[...] = _rms(ckv_ref[...], kvg_ref[...]).astype(BF16)
        kr = kr_ref[...]
        krr_ref[...] = (kr * c_ref[...] + _rot_half(kr, 128, 0) * s_ref[...]).astype(BF16)

    vec = pl.BlockSpec((1, BR), lambda i: (0, 0))
    tab = pl.BlockSpec((ts, 128), lambda i: (i, 0))
    row = pl.BlockSpec((ts, BR), lambda i: (i, 0))
    return pl.pallas_call(
        body, name=name, grid=(s // ts,),
        in_specs=[pl.BlockSpec((ts, BR), lambda i: (i, 5)), pl.BlockSpec((ts, BR), lambda i: (i, 6)),
                  pl.BlockSpec((ts, 128), lambda i: (i, KR_BLOCK)), vec, vec, tab, tab],
        out_specs=[row, row, tab],
        out_shape=[jax.ShapeDtypeStruct((s, BR), BF16), jax.ShapeDtypeStruct((s, BR), BF16),
                   jax.ShapeDtypeStruct((s, 128), BF16)],
        compiler_params=_params(("parallel",)))(za, za, za, qg, kvg, ck, sk)


def _mla_prep_bwd(dqn, dkvn, dkr_heads, za, qg, kvg, ck, sk, name):
    s = za.shape[0]
    ts = _tile(s, 512)

    def body(dq_ref, dkv_ref, dkr_ref, cq_ref, ckv_ref, qg_ref, kvg_ref, c_ref, s_ref, dz_ref, dqg_ref, dkvg_ref):
        first = pl.program_id(0) == 0
        dcq, rows_q = _rms_bwd(cq_ref[...], qg_ref[...], dq_ref[...])
        dckv, rows_kv = _rms_bwd(ckv_ref[...], kvg_ref[...], dkv_ref[...])
        dz_ref[:, pl.ds(0, BR)] = dcq.astype(BF16)
        dz_ref[:, pl.ds(BR, BR)] = dckv.astype(BF16)
        dk = dkr_ref[:, pl.ds(0, 128)]
        for h in range(1, MLA_HEADS):
            dk = dk + dkr_ref[:, pl.ds(h * 128, 128)]
        dz_ref[:, pl.ds(2 * BR, 128)] = (dk * c_ref[...] - _rot_half(dk, 128, 0) * s_ref[...]).astype(BF16)
        _colsum_into(dqg_ref, rows_q, first)
        _colsum_into(dkvg_ref, rows_kv, first)

    vec = pl.BlockSpec((1, BR), lambda i: (0, 0))
    tab = pl.BlockSpec((ts, 128), lambda i: (i, 0))
    row = pl.BlockSpec((ts, BR), lambda i: (i, 0))
    wide = 2 * BR + 128
    vshape = jax.ShapeDtypeStruct((1, BR), F32)
    return pl.pallas_call(
        body, name=name, grid=(s // ts,),
        in_specs=[row, row, pl.BlockSpec((ts, MLA_HEADS * 128), lambda i: (i, 0)),
                  pl.BlockSpec((ts, BR), lambda i: (i, 5)), pl.BlockSpec((ts, BR), lambda i: (i, 6)), vec, vec, tab, tab],
        out_specs=[pl.BlockSpec((ts, wide), lambda i: (i, 0)), vec, vec],
        out_shape=[jax.ShapeDtypeStruct((s, wide), BF16), vshape, vshape],
        compiler_params=_params(("arbitrary",)))(dqn, dkvn, dkr_heads, za, za, qg, kvg, ck, sk)


def _attn_tiles(s):
    tq, tk = _tile(s, 1024), _tile(s, 512)
    return tq, tk, tq // tk


def _causal(qi, ki, tq, tk):
    row = qi * tq + lax.broadcasted_iota(jnp.int32, (tq, tk), 0)
    col = ki * tk + lax.broadcasted_iota(jnp.int32, (tq, tk), 1)
    return col <= row


def _step_count(t, firsts):
    n = jnp.int32(0)
    for f in firsts[1:]:
        n = n + jnp.where(t >= f, 1, 0)
    return n


def _q_major_pairs(nq, r):
    firsts = [r * qq * (qq + 1) // 2 for qq in range(nq)]

    def pair(t):
        qi = _step_count(t, firsts)
        return qi, t - r * qi * (qi + 1) // 2

    return r * nq * (nq + 1) // 2, pair


def _k_major_pairs(nq, nk, r):
    counts = [nq - kk // r for kk in range(nk)]
    firsts = [sum(counts[:kk]) for kk in range(nk)]

    def pair(t):
        ki = _step_count(t, firsts)
        first = jnp.int32(0)
        for kk in range(1, nk):
            first = first + jnp.where(t >= firsts[kk], counts[kk - 1], 0)
        return ki // r + (t - first), ki

    return sum(counts), pair


FLASH_HEADS = 2
FLASH_ROWS = 256


def _flash_fwd(q, kv, krr, name, plan=None):
    s = q.shape[0]
    tq, tk, r = _attn_tiles(s)
    n_pairs, pair = _q_major_pairs(s // tq, r)
    hp = FLASH_HEADS
    chunk = min(FLASH_ROWS, tq)

    def body(q_ref, kv_ref, kr_ref, o_ref, lse_ref, m_sc, l_sc, acc_sc):
        qi, ki = pair(pl.program_id(1))

        @pl.when(ki == 0)
        def _():
            m_sc[...] = jnp.full((hp, tq, 1), NEG, F32)
            l_sc[...] = jnp.zeros((hp, tq, 1), F32)
            acc_sc[...] = jnp.zeros((hp, tq, V_DIM), F32)

        def step(masked):
            for h in range(hp):
                k = jnp.concatenate([kv_ref[:, pl.ds(h * QW, QK_NOPE)], kr_ref[...]], axis=1)
                v = kv_ref[:, pl.ds(h * QW + QK_NOPE, V_DIM)]
                for c in range(tq // chunk):
                    rows = pl.ds(c * chunk, chunk)
                    sc = lax.dot_general(q_ref[rows, pl.ds(h * QW, QW)], k, (((1,), (1,)), ((), ())),
                                         preferred_element_type=F32)
                    if masked:
                        row = qi * tq + c * chunk + lax.broadcasted_iota(jnp.int32, (chunk, tk), 0)
                        col = ki * tk + lax.broadcasted_iota(jnp.int32, (chunk, tk), 1)
                        sc = jnp.where(col <= row, sc, NEG)
                    m_prev = m_sc[h, rows, :]
                    m_new = jnp.maximum(m_prev, jnp.max(sc, axis=1, keepdims=True))
                    alpha = jnp.exp(m_prev - m_new)
                    p = jnp.exp(sc - m_new)
                    l_sc[h, rows, :] = alpha * l_sc[h, rows, :] + jnp.sum(p, axis=1, keepdims=True)
                    acc_sc[h, rows, :] = alpha * acc_sc[h, rows, :] + jnp.dot(p.astype(BF16), v,
                                                                              preferred_element_type=F32)
                    m_sc[h, rows, :] = m_new

        pl.when(ki < qi * r)(functools.partial(step, False))
        pl.when(ki >= qi * r)(functools.partial(step, True))

        @pl.when(ki == (qi + 1) * r - 1)
        def _():
            for h in range(hp):
                o_ref[:, pl.ds(h * V_DIM, V_DIM)] = (acc_sc[h] / l_sc[h]).astype(BF16)
                lse_ref[:, pl.ds(h * 128, 128)] = jnp.broadcast_to(m_sc[h] + jnp.log(l_sc[h]), (tq, 128))

    out_blk = pl.BlockSpec((tq, hp * 128), lambda g, t: (pair(t)[0], g))
    return _pcall(
        body, (q, kv, krr), name=name, grid=(MLA_HEADS // hp, n_pairs),
        in_specs=[pl.BlockSpec((tq, hp * QW), lambda g, t: (pair(t)[0], g)),
                  pl.BlockSpec((tk, hp * QW), lambda g, t: (pair(t)[1], g)),
                  pl.BlockSpec((tk, 128), lambda g, t: (pair(t)[1], 0))],
        out_specs=[out_blk, out_blk],
        out_shape=[jax.ShapeDtypeStruct((s, MLA_HEADS * V_DIM), BF16), jax.ShapeDtypeStruct((s, MLA_HEADS * 128), F32)],
        scratch_shapes=[pltpu.VMEM((hp, tq, 1), F32), pltpu.VMEM((hp, tq, 1), F32), pltpu.VMEM((hp, tq, V_DIM), F32)],
        sem=("parallel", "arbitrary"), vmem=VMEM_BIG, plan=plan)


DO_BLOCK = 3 * BR // 128


def _flash_probs(q_ref, kn_ref, kr_ref, v_ref, do_ref, o_ref, lse_ref, qi, ki, tq, tk, masked):
    k = jnp.concatenate([kn_ref[...], kr_ref[...]], axis=1)
    q = q_ref[...]
    sc = lax.dot_general(q, k, (((1,), (1,)), ((), ())), preferred_element_type=F32)
    lse = jnp.max(lse_ref[...], axis=1, keepdims=True)
    p = jnp.exp(sc - lse)
    if masked:
        p = jnp.where(_causal(qi, ki, tq, tk), p, 0.0)
    do = do_ref[...]
    delta = jnp.sum(do * o_ref[...].astype(F32), axis=1, keepdims=True)
    do = do.astype(BF16)
    dp = lax.dot_general(do, v_ref[...], (((1,), (1,)), ((), ())), preferred_element_type=F32)
    ds = (p * (dp - delta)).astype(BF16)
    return q, k, p, do, ds


def _flash_bwd(q, kv, krr, dact, o, lse, cq, sq, name, plan=None):
    s = q.shape[0]
    tq, tk, r = _attn_tiles(s)
    nq, nk = s // tq, s // tk
    n_pairs, pair = _k_major_pairs(nq, nk, r)

    def body(q_ref, kn_ref, kr_ref, v_ref, do_ref, o_ref, lse_ref, c_ref, s_ref, dkv_ref, dkr_ref, dq_ref,
             dk_sc, dv_sc, dq_sc):
        qi, ki = pair(pl.program_id(1))
        rows = pl.ds(pl.multiple_of(qi * tq, tq), tq)

        @pl.when(qi == ki // r)
        def _():
            dk_sc[...] = jnp.zeros((tk, QW), F32)
            dv_sc[...] = jnp.zeros((tk, V_DIM), F32)

        @pl.when(ki == 0)
        def _():
            dq_sc[rows, :] = jnp.zeros((tq, QW), F32)

        def step(masked):
            qv, k, p, dov, ds = _flash_probs(q_ref, kn_ref, kr_ref, v_ref, do_ref, o_ref, lse_ref, qi, ki, tq, tk, masked)
            dv_sc[...] += lax.dot_general(p.astype(BF16), dov, (((0,), (0,)), ((), ())), preferred_element_type=F32)
            dk_sc[...] += lax.dot_general(ds, qv, (((0,), (0,)), ((), ())), preferred_element_type=F32)
            dq_sc[rows, :] += jnp.dot(ds, k, preferred_element_type=F32)

        pl.when(ki < qi * r)(functools.partial(step, False))
        pl.when(ki >= qi * r)(functools.partial(step, True))

        @pl.when(qi == nq - 1)
        def _():
            dkv_ref[:, pl.ds(0, 128)] = dk_sc[:, pl.ds(0, 128)].astype(BF16)
            dkv_ref[:, pl.ds(128, 128)] = dv_sc[...].astype(BF16)
            dkr_ref[...] = dk_sc[:, pl.ds(128, 128)]

        @pl.when(ki == (qi + 1) * r - 1)
        def _():
            dq = dq_sc[rows, :] * ATT_SCALE
            dq_ref[...] = (dq * c_ref[...] - _rot_half(dq, QW, QK_NOPE) * s_ref[...]).astype(BF16)

    def next_done(t):
        qi, ki = pair(t)
        return jnp.minimum(ki // r + jnp.where(jnp.logical_and(ki % r == r - 1, qi > ki // r), 1, 0), nq - 1)

    qmap = lambda c: (lambda h, t: (pair(t)[0], c(h)))
    kmap = lambda c: (lambda h, t: (pair(t)[1], c(h)))
    last = lambda c: (lambda h, t: (next_done(t), c(h)))
    return _pcall(
        body, (q, kv, krr, kv, dact, o, lse, cq, sq), name=name, grid=(MLA_HEADS, n_pairs),
        in_specs=[pl.BlockSpec((tq, QW), qmap(lambda h: h)), pl.BlockSpec((tk, 128), kmap(lambda h: 2 * h)),
                  pl.BlockSpec((tk, 128), kmap(lambda h: 0)), pl.BlockSpec((tk, 128), kmap(lambda h: 2 * h + 1)),
                  pl.BlockSpec((tq, 128), qmap(lambda h: DO_BLOCK + h)), pl.BlockSpec((tq, 128), qmap(lambda h: h)),
                  pl.BlockSpec((tq, 128), qmap(lambda h: h)), pl.BlockSpec((tq, QW), last(lambda h: 0)),
                  pl.BlockSpec((tq, QW), last(lambda h: 0))],
        out_specs=[pl.BlockSpec((tk, QW), kmap(lambda h: h)), pl.BlockSpec((tk, 128), kmap(lambda h: h)),
                   pl.BlockSpec((tq, QW), last(lambda h: h))],
        out_shape=[jax.ShapeDtypeStruct((s, MLA_HEADS * QW), BF16), jax.ShapeDtypeStruct((s, MLA_HEADS * 128), F32),
                   jax.ShapeDtypeStruct((s, MLA_HEADS * QW), BF16)],
        scratch_shapes=[pltpu.VMEM((tk, QW), F32), pltpu.VMEM((tk, V_DIM), F32), pltpu.VMEM((s, QW), F32)],
        sem=("parallel", "arbitrary"), vmem=VMEM_BIG, plan=plan)


def _rope_tables(positions):
    inv_freq = ROPE_THETA ** (-jnp.arange(0, QK_ROPE, 2, dtype=F32) / QK_ROPE)
    ang = positions.reshape(-1).astype(F32)[:, None] * inv_freq
    cos, sin = jnp.cos(ang), jnp.sin(ang)
    s = cos.shape[0]
    one, zero = jnp.ones((s, 64), F32), jnp.zeros((s, 64), F32)
    ck = jnp.concatenate([cos, cos, one], axis=1)
    sk = jnp.concatenate([sin, sin, zero], axis=1)
    cq = jnp.concatenate([one, one, ck], axis=1)
    sq = jnp.concatenate([zero, zero, sk], axis=1)
    return ck, sk, cq, sq


def _cols_full(gathered):
    _, r, c = gathered.shape
    return gathered.transpose(1, 0, 2).reshape(r, N_DEV * c)


def _cols_by_owner(full):
    r, n = full.shape
    return full.reshape(r, N_DEV, n // N_DEV).transpose(1, 0, 2)


def _layer_weights(gat, small, l):
    shards = gat[("w_in", l)]
    cut, rem = divmod(N_IN_A, shards.shape[2])
    w = {
        "gat": gat, "layer": l,
        "w_a": jnp.concatenate([shards[d] for d in range(cut)] + [shards[cut][:, :rem],
                                                                 jnp.zeros((D_MODEL, ZA - N_IN_A), BF16)], axis=1),
        "w_g": jnp.concatenate([shards[cut][:, rem:]] + [shards[d] for d in range(cut + 1, N_DEV)], axis=1),
        "conv_w": small["conv_w"][l],
        "pool_w": small["pool_w"][l], "sgu_w": small["sgu_w"][l],
        "sgu_bias": jnp.broadcast_to(small["sgu_b"][l][:, :, None], (4, CHUNK, GDIM)),
    }
    for name in ("pre_mix_g", "pool_scale", "conv_b", "conv_norm_g", "conv_norm_b", "sgu_norm_g", "sgu_norm_b",
                 "q_norm_g", "kv_norm_g", "post_mix_g", "pre_mlp_g", "post_mlp_g"):
        w[name] = small[name][l][None, :]
    return w


def _late(w, name):
    if name not in w:
        gat, l = w["gat"], w["layer"]
        if name == "proj_cat":
            w[name] = jnp.concatenate([_cols_full(gat[(n, l)]) for n in ("pool_proj", "conv_proj", "sgu_proj", "attn_proj")],
                                      axis=0)
        elif name == "w_uq":
            w[name] = jnp.pad(gat[("w_uq", l)].transpose(1, 0, 2),
                              ((0, 0), (0, 0), (0, QW - QK_NOPE - QK_ROPE))).reshape(BR, MLA_HEADS * QW)
        elif name == "w_ukv":
            w[name] = _cols_full(gat[("w_ukv", l)])
        elif name == "w_up":
            w[name] = gat[("w_up", l)]
        else:
            arr = gat[(name, l)]
            w[name] = arr.reshape(N_DEV * arr.shape[1], arr.shape[2])
    return w[name]


def _layer_fwd(x, h1, w, tabs, l, plan):
    ck, sk, cq, sq = tabs
    n = f"l{l}_"
    za = _mm(h1, w["w_a"], mode="nn", name=n + "za", plan=plan)
    zg = _mm(h1, w["w_g"], mode="nn", name=n + "zg", out_dtypes=(BF16,), plan=plan)
    a_pool = _pool_fwd(za, w["pool_w"], w["pool_scale"], n + "pool")
    yc, a_conv = _conv_fwd(za, w["conv_w"], w["conv_b"], w["conv_norm_g"], w["conv_norm_b"], n + "conv")
    a_sgu = _sgu_fwd(za, w["sgu_norm_g"], w["sgu_norm_b"], w["sgu_w"], w["sgu_bias"], n + "sgu")
    qn, kvn, krr = _mla_prep(za, w["q_norm_g"], w["kv_norm_g"], ck, sk, n + "mla_prep")
    q = _mm(qn, _late(w, "w_uq"), mode="nn", name=n + "q", out_dtypes=(BF16,), extras=(cq, sq), tn=QW,
            epilogue=lambda acc, c, sn: ((acc * c + _rot_half(acc, QW, QK_NOPE) * sn) * ATT_SCALE,))
    kv = _mm(kvn, _late(w, "w_ukv"), mode="nn", name=n + "kv", out_dtypes=(BF16,))
    o, lse = _flash_fwd(q, kv, krr, n + "flash", plan=plan)
    act_cat = jnp.concatenate([a_pool, a_conv, a_sgu, o], axis=1)
    y, merged = _proj_merge_fwd(act_cat, _late(w, "proj_cat"), zg, n + "proj", plan=plan)
    m2 = _mm(merged, _late(w, "w_out"), mode="nn", name=n + "out")
    x1, h2 = _post_pre(x, m2, w["post_mix_g"], w["pre_mlp_g"], n + "post_mix")
    up, act = _mm(h2, _late(w, "w_up"), mode="nn", name=n + "up", out_dtypes=(BF16, BF16), plan=plan,
                  mnk=(x.shape[0], D_FF, D_MODEL), tn=UP_SHARD,
                  b_spec=lambda tn, tk: pl.BlockSpec((None, tk, tn), lambda i, j, kk: (j, kk, 0)),
                  epilogue=lambda acc: (acc, jnp.square(jnp.maximum(acc, 0.0))))
    f = _mm(act, _late(w, "w_down"), mode="nn", name=n + "down", plan=plan)
    saved = dict(x=x, h1=h1, za=za, zg=zg, yc=yc, qn=qn, kvn=kvn, krr=krr, q=q, kv=kv, o=o, lse=lse, act_cat=act_cat,
                 y=y, merged=merged, m2=m2, x1=x1, h2=h2, up=up, act=act, f=f)
    return x1, f, saved


def _layer_bwd(dx_out, df, sv, w, tabs, l, prev, plan, rs):
    ck, sk, cq, sq = tabs
    n = f"l{l}_b_"
    g = {}
    own, half = {}, {}

    def to_sibling(kernel, names):
        def done(results):
            for name, t in zip(names, results[0]):
                half[name] = _rs_chip_sum(own[name], t, f"l{l}_chip_sum_{name}")
        plan.at(n + kernel, lambda: [_scatter_sibling([own[name] for name in names])], done)

    def to_chips(kernel, names, prefix=n):
        def done(results):
            for name, t in zip(names, results[0]):
                rs[(name, l)] = (half[name], t)
        plan.at(prefix + kernel, lambda: [_scatter_chips([half[name] for name in names])], done)

    own["w_down"] = _mm(sv["act"], df, mode="tn", name=n + "dw_down", out_dtypes=(BF16,), plan=plan).reshape(
        N_DEV, D_FF // N_DEV, D_MODEL)
    to_sibling("dup", ["w_down"])
    dup = _mm(df, _late(w, "w_down"), mode="nt", name=n + "dup", out_dtypes=(BF16,), extras=(sv["up"],), plan=plan,
              epilogue=lambda acc, up: (acc * 2.0 * jnp.maximum(up.astype(F32), 0.0),))
    to_chips("dw_up", ["w_down"])
    own["w_up"] = _mm(sv["h2"], dup, mode="tn", name=n + "dw_up", tn=UP_SHARD, plan=plan,
                      o_spec=lambda tm, tn: pl.BlockSpec((None, tm, tn), lambda i, j, kk: (j, i, 0)),
                      out_struct=jax.ShapeDtypeStruct((N_DEV, D_MODEL, UP_SHARD), BF16))
    to_sibling("dh2", ["w_up"])
    dh2 = _mm(dup, _late(w, "w_up"), mode="nt", name=n + "dh2", mnk=(dup.shape[0], D_MODEL, D_FF), tk=UP_SHARD,
              plan=plan, b_spec=lambda tn, tk: pl.BlockSpec((None, tn, tk), lambda i, j, kk: (kk, j, 0)))
    dx1, dm2, g["pre_mlp_g"], g["post_mix_g"] = _pre_bwd(dh2, sv["x1"], w["pre_mlp_g"], dx_out, n + "pre_mlp",
                                                           r_prev=sv["m2"], g_post_prev=w["post_mix_g"])
    own["w_out"] = _mm(sv["merged"], dm2, mode="tn", name=n + "dw_out", out_dtypes=(BF16,)).reshape(
        N_DEV, D_MODEL // N_DEV, D_MODEL)
    to_sibling("dmerged", ["w_out"])
    dmerged = _mm(dm2, _late(w, "w_out"), mode="nt", name=n + "dmerged", plan=plan)
    to_chips("merge", ["w_up"])
    dy, dzg = _merge_bwd(dmerged, sv["y"], sv["zg"], n + "merge", plan=plan)
    d_proj = _proj_bwd_w(sv["act_cat"], dy, n + "dw_proj")
    projs = ["pool_proj", "conv_proj", "sgu_proj", "attn_proj"]
    for i, name in enumerate(projs):
        own[name] = _cols_by_owner(d_proj[i * BR:(i + 1) * BR] if i < 3 else d_proj[3 * BR:])
    to_chips("dact", ["w_out"])
    to_sibling("dact", projs)
    dact = _proj_bwd_act(dy, _late(w, "proj_cat"), n + "dact", plan=plan)
    dz_pool, g["pool_w"], g["pool_scale"] = _pool_bwd(dact, sv["za"], w["pool_w"], w["pool_scale"], n + "pool")
    dyc, g["conv_b"], g["conv_norm_g"], g["conv_norm_b"] = _conv_bwd_norm(dact, sv["yc"], w["conv_norm_g"],
                                                                          w["conv_norm_b"], n + "conv_norm")
    dz_conv, g["conv_w"] = _conv_bwd_taps(dyc, sv["za"], w["conv_w"], n + "conv_taps")
    dz_sgu, g["sgu_w"], g["sgu_b"], g["sgu_norm_g"], g["sgu_norm_b"] = _sgu_bwd(
        dact, sv["za"], w["sgu_norm_g"], w["sgu_norm_b"], w["sgu_w"], w["sgu_bias"], n + "sgu")
    to_chips("flash", projs)
    dkv, dkr_heads, dq = _flash_bwd(sv["q"], sv["kv"], sv["krr"], dact, sv["o"], sv["lse"], cq, sq, n + "flash",
                                    plan=plan)
    d_uq = _mm(sv["qn"], dq, mode="tn", name=n + "dw_uq", out_dtypes=(BF16,))
    own["w_uq"] = d_uq.reshape(BR, MLA_HEADS, QW)[:, :, :QK_NOPE + QK_ROPE].transpose(1, 0, 2)
    dqn = _mm(dq, _late(w, "w_uq"), mode="nt", name=n + "dqn")
    own["w_ukv"] = _cols_by_owner(_mm(sv["kvn"], dkv, mode="tn", name=n + "dw_ukv", out_dtypes=(BF16,)))
    to_sibling("dkvn", ["w_uq", "w_ukv"])
    dkvn = _mm(dkv, _late(w, "w_ukv"), mode="nt", name=n + "dkvn", plan=plan)
    dz_mla, g["q_norm_g"], g["kv_norm_g"] = _mla_prep_bwd(dqn, dkvn, dkr_heads, sv["za"], w["q_norm_g"],
                                                           w["kv_norm_g"], ck, sk, n + "mla_prep")
    dza = jnp.concatenate([dz_pool, dz_conv, dz_sgu, dz_mla], axis=1)
    to_chips("dw_a", ["w_uq", "w_ukv"])
    d_a = _mm(sv["h1"], dza, mode="tn", name=n + "dw_a", out_dtypes=(BF16,), plan=plan)
    d_g = _mm(sv["h1"], dzg, mode="tn", name=n + "dw_g", out_dtypes=(BF16,))
    own["w_in"] = _cols_by_owner(jnp.concatenate([d_a[:, :N_IN_A], d_g], axis=1)).transpose(0, 2, 1)
    to_sibling("dh1_a", ["w_in"])
    dh1 = _mm(dza, w["w_a"], mode="nt", name=n + "dh1_a", plan=plan)
    if prev is None:
        to_chips("dh1_g", ["w_in"])
    else:
        to_chips("flash", ["w_in"], prefix=f"l{l - 1}_b_")
    dh1 = _mm(dzg, w["w_g"], mode="nt", name=n + "dh1_g", extras=(dh1,), epilogue=lambda acc, e: (acc + e,), plan=plan)
    if prev is None:
        dx, g["pre_mix_g"] = _pre_bwd(dh1, sv["x"], w["pre_mix_g"], dx1, n + "pre_mix")
        return dx, None, g
    dx, df_prev, g["pre_mix_g"], g_prev_post = _pre_bwd(dh1, sv["x"], w["pre_mix_g"], dx1, n + "pre_mix",
                                                         r_prev=prev[0], g_post_prev=prev[1])
    g["prev_post_mlp_g"] = g_prev_post
    return dx, df_prev, g


MIDSIZE = ("pool_proj", "conv_proj", "sgu_proj", "w_uq", "w_ukv", "attn_proj")
GATHER_STEPS = (
    ("l0_za", tuple((name, 0) for name in MIDSIZE)),
    ("l0_zg", (("w_out", 0),) + tuple((name, 1) for name in MIDSIZE)),
    ("l0_flash", (("w_in", 1), ("w_up", 0))),
    ("l0_proj", (("w_down", 0),)),
    ("l0_up", (("w_up", 1),)),
    ("l0_down", (("w_out", 1),)),
    ("l1_za", ()),
    ("l1_zg", ()),
    ("l1_flash", (("w_down", 1),)),
    ("l1_proj", ()),
)


def _plan_gathers(plan, gat, shards):
    first_half = {}
    for step, (kernel, keys) in enumerate(GATHER_STEPS):
        before = GATHER_STEPS[step - 1][1] if step else ()
        if not keys and not before:
            continue

        def make(keys=keys, before=before):
            ops = [_gather_pass(first_half[before])] if before else []
            return ops + ([_gather_own([shards[key] for key in keys])] if keys else [])

        def done(results, keys=keys, before=before):
            if before:
                gat.update(zip(before, results[0]))
            if keys:
                first_half[keys] = results[-1]

        plan.at(kernel, make, done)


def _local_step(x, positions, target, gat, small, shards, plan):
    tabs = _rope_tables(positions)
    _plan_gathers(plan, gat, shards)
    ws = [_layer_weights(gat, small, 0)]
    saved = []
    h = _pre_norm(x, ws[0]["pre_mix_g"], "l0_pre_mix")
    cur = x
    for l in range(DEPTH):
        x1, f, sv = _layer_fwd(cur, h, ws[l], tabs, l, plan)
        saved.append(sv)
        if l + 1 < DEPTH:
            ws.append(_layer_weights(gat, small, l + 1))
            cur, h = _post_pre(x1, f, ws[l]["post_mlp_g"], ws[l + 1]["pre_mix_g"], f"l{l}_post_mlp")
    top = DEPTH - 1
    dx, df, dg_post, loss = _final_loss(saved[top]["x1"], saved[top]["f"], ws[top]["post_mlp_g"], target, "loss")
    grads = [None] * DEPTH
    post_mlp = {top: dg_post}
    rs = {}
    for l in range(top, -1, -1):
        prev = (saved[l - 1]["f"], ws[l - 1]["post_mlp_g"]) if l > 0 else None
        dx, df, g = _layer_bwd(dx, df, saved[l], ws[l], tabs, l, prev, plan, rs)
        if l > 0:
            post_mlp[l - 1] = g.pop("prev_post_mlp_g")
        grads[l] = g
    for l in range(DEPTH):
        grads[l]["post_mlp_g"] = post_mlp[l]
    assert not plan.jobs, sorted(plan.jobs)
    return loss[0, 0], dx, grads, rs


def _small_grads(grads):
    small = {}

    def stack(fn):
        return jnp.stack([fn(grads[l]) for l in range(DEPTH)])

    for name, shape in SMALL:
        if name == "sgu_b":
            small[name] = stack(lambda g: g["sgu_b"][:, :, 0])
        else:
            small[name] = stack(lambda g, name=name, shape=shape: g[name].reshape(shape))
    small["conv_w"] = stack(lambda g: g["conv_w"][:CONV_WIDTH])
    return small


SMALL_ROWS = sum(DEPTH * math.prod(shape) // 128 for _, shape in SMALL)
CONVW_ROWS = DEPTH * CONV_WIDTH * BR // 128


def _pack_small(parts):
    return jnp.concatenate([parts[name].astype(F32).reshape(-1, 128) for name, _ in SMALL], axis=0)


def _unpack_small(buf):
    out, off = {}, 0
    for name, shape in SMALL:
        rows = DEPTH * math.prod(shape) // 128
        out[name] = buf[off:off + rows].reshape((DEPTH,) + shape)
        off += rows
    return out


def _mesh_pos():
    return lax.axis_index("x"), lax.axis_index("y"), lax.axis_index("c")


def _all_gather(shards, name):
    n = len(shards)

    def body(*refs):
        x_refs, out_refs = refs[:n], refs[n:2 * n]
        send_sems, recv_sems, local_sems = refs[2 * n:]
        x, y, c = _mesh_pos()
        me, sibling = (x, y, c), (x, y, 1 - c)
        chips = [(1 - x, y), (x, 1 - y), (1 - x, 1 - y)]

        def rows(t, px, py, pc):
            return out_refs[t].at[4 * px + 2 * py + pc]

        def copy(t, k, block, to, src=None):
            return pltpu.make_async_remote_copy(
                src_ref=rows(t, *block) if src is None else src, dst_ref=rows(t, *block), send_sem=send_sems.at[t, k],
                recv_sem=recv_sems.at[t, k], device_id=to, device_id_type=MESH)

        mine = [pltpu.make_async_copy(x_refs[t], rows(t, *me), local_sems.at[t]) for t in range(n)]
        for cp in mine:
            cp.start()
        first = []
        for t in range(n):
            first.append(copy(t, 0, me, sibling, src=x_refs[t]))
            first += [copy(t, 1 + j, me, (*chip, c), src=x_refs[t]) for j, chip in enumerate(chips)]
        for cp in first:
            cp.start()
        passed = []
        for t in range(n):
            for j, chip in enumerate(chips):
                copy(t, 1 + j, (*chip, c), me).wait_recv()
                passed.append(copy(t, 4 + j, (*chip, c), sibling))
                passed[-1].start()
        for t in range(n):
            copy(t, 0, sibling, me).wait_recv()
            for j, chip in enumerate(chips):
                copy(t, 4 + j, (*chip, 1 - c), me).wait_recv()
        for cp in first + passed:
            cp.wait_send()
        for cp in mine:
            cp.wait()

    hbm = pl.BlockSpec(memory_space=pl.ANY)
    return pl.pallas_call(
        body, name=name, out_shape=[jax.ShapeDtypeStruct((N_DEV,) + a.shape, a.dtype) for a in shards],
        in_specs=[hbm] * n, out_specs=[hbm] * n,
        scratch_shapes=[pltpu.SemaphoreType.DMA((n, 7)), pltpu.SemaphoreType.DMA((n, 7)),
                        pltpu.SemaphoreType.DMA((n,))])(*shards)


def _shard_tile(r, c_, cap_bytes=3 << 19):
    best = 0
    for t in range(16, r + 1, 16):
        if r % t == 0 and t * c_ * 4 <= cap_bytes:
            best = t
    if best or r * c_ * 4 <= cap_bytes:
        return (best or r), c_
    tc = max(t for t in range(128, c_ + 1, 128) if c_ % t == 0 and r * t * 4 <= cap_bytes)
    return r, tc


def _rs_chip_sum(g, t, name):
    _, r, c_ = g.shape
    tr, tc = _shard_tile(r, c_)
    core = lax.axis_index("c").astype(jnp.int32).reshape(1)

    def body(core_ref, g_ref, t_ref, p_ref):
        p_ref[...] = (g_ref[...].astype(F32) + t_ref[...].astype(F32)).astype(p_ref.dtype)

    return pl.pallas_call(
        body, name=name, out_shape=jax.ShapeDtypeStruct((4, r, c_), g.dtype),
        grid_spec=pltpu.PrefetchScalarGridSpec(
            num_scalar_prefetch=1, grid=(4, r // tr, c_ // tc),
            in_specs=[pl.BlockSpec((1, tr, tc), lambda k, i, j, core_ref: (2 * k + core_ref[0], i, j)),
                      pl.BlockSpec((1, tr, tc), lambda k, i, j, core_ref: (k, i, j))],
            out_specs=pl.BlockSpec((1, tr, tc), lambda k, i, j, core_ref: (k, i, j))),
        compiler_params=_params(("parallel", "parallel", "parallel")))(core, g, t)


def _adamw_math(w, g, m, v):
    m = ADAM_B1 * m + (1.0 - ADAM_B1) * g
    v = ADAM_B2 * v + (1.0 - ADAM_B2) * jnp.square(g)
    m_hat = m / (1.0 - ADAM_B1 ** ADAM_STEP)
    v_hat = v / (1.0 - ADAM_B2 ** ADAM_STEP)
    delta = -ADAM_LR * (m_hat / (jnp.sqrt(v_hat) + ADAM_EPS) + ADAM_WD * w)
    return delta, m, v


def _adamw_big(ps, ts, w, m, v, name):
    _, r, c_ = w.shape
    tr, tc = _shard_tile(r, c_)
    chip = (2 * lax.axis_index("x") + lax.axis_index("y")).astype(jnp.int32).reshape(1)

    def body(chip_ref, p0, t0, p1, t1, w_ref, m_ref, v_ref, g_out, d_out, m_out, v_out):
        def update(p_ref, t_ref):
            g = p_ref[0].astype(F32) + t_ref[0].astype(F32) + t_ref[1].astype(F32) + t_ref[2].astype(F32)
            g_out[0] = g
            d_out[0], m_out[0], v_out[0] = _adamw_math(w_ref[0], g, m_ref[0], v_ref[0])

        pl.when(pl.program_id(0) == 0)(functools.partial(update, p0, t0))
        pl.when(pl.program_id(0) == 1)(functools.partial(update, p1, t1))

    def grad_specs(layer):
        def at(l, i, j):
            return jnp.where(l == layer, i, 0), jnp.where(l == layer, j, 0)
        return [pl.BlockSpec((1, tr, tc), lambda l, i, j, chip_ref: (chip_ref[0], *at(l, i, j))),
                pl.BlockSpec((3, tr, tc), lambda l, i, j, chip_ref: (0, *at(l, i, j)))]

    nat = pl.BlockSpec((1, tr, tc), lambda l, i, j, chip_ref: (l, i, j))
    shape = jax.ShapeDtypeStruct(w.shape, F32)
    return pl.pallas_call(
        body, name=name, out_shape=[shape] * 4,
        grid_spec=pltpu.PrefetchScalarGridSpec(
            num_scalar_prefetch=1, grid=(DEPTH, r // tr, c_ // tc),
            in_specs=grad_specs(0) + grad_specs(1) + [nat, nat, nat], out_specs=[nat] * 4),
        compiler_params=_params(("parallel", "parallel", "parallel"), VMEM_BIG))(
            chip, ps[0], ts[0], ps[1], ts[1], w, m, v)


def _sum_devices(parts, name):
    _, r, c_ = parts.shape

    def body(p_ref, o_ref):
        acc = p_ref[0]
        for d in range(1, N_DEV):
            acc = acc + p_ref[d]
        o_ref[...] = acc

    return pl.pallas_call(body, name=name, out_shape=jax.ShapeDtypeStruct((r, c_), F32),
                          compiler_params=_params(None, VMEM_BIG))(parts)


def _adamw_small(w, g, m, v, name):
    def body(w_ref, g_ref, m_ref, v_ref, d_out, m_out, v_out):
        d_out[...], m_out[...], v_out[...] = _adamw_math(w_ref[...], g_ref[...], m_ref[...], v_ref[...])

    shape = jax.ShapeDtypeStruct(w.shape, F32)
    return pl.pallas_call(body, name=name, out_shape=[shape] * 3)(w, g, m, v)


def kernel(x, positions, pre_mix_g, w_in, pool_w, pool_scale, pool_proj, conv_w, conv_b, conv_norm_g, conv_norm_b, conv_proj, sgu_norm_g, sgu_norm_b, sgu_w, sgu_b, sgu_proj, q_norm_g, w_uq, kv_norm_g, w_ukv, attn_proj, w_out, post_mix_g, pre_mlp_g, w_up, w_down, post_mlp_g, loss_target, m_pre_mix_g, m_w_in, m_pool_w, m_pool_scale, m_pool_proj, m_conv_w, m_conv_b, m_conv_norm_g, m_conv_norm_b, m_conv_proj, m_sgu_norm_g, m_sgu_norm_b, m_sgu_w, m_sgu_b, m_sgu_proj, m_q_norm_g, m_w_uq, m_kv_norm_g, m_w_ukv, m_attn_proj, m_w_out, m_post_mix_g, m_pre_mlp_g, m_w_up, m_w_down, m_post_mlp_g, v_pre_mix_g, v_w_in, v_pool_w, v_pool_scale, v_pool_proj, v_conv_w, v_conv_b, v_conv_norm_g, v_conv_norm_b, v_conv_proj, v_sgu_norm_g, v_sgu_norm_b, v_sgu_w, v_sgu_b, v_sgu_proj, v_q_norm_g, v_w_uq, v_kv_norm_g, v_w_ukv, v_attn_proj, v_w_out, v_post_mix_g, v_pre_mlp_g, v_w_up, v_w_down, v_post_mlp_g):
    args = dict(locals())
    wts = {n: args[n] for n in WEIGHTS}
    mom1 = {n: args["m_" + n] for n in WEIGHTS}
    mom2 = {n: args["v_" + n] for n in WEIGHTS}
    dev = 4 * lax.axis_index("x") + 2 * lax.axis_index("y") + lax.axis_index("c")

    shards = {(name, l): wts[name][l].astype(BF16) for name, _, _ in BIG for l in range(DEPTH)}
    taps = jnp.pad(conv_w.reshape(-1, 128), ((0, 1), (0, 0)))
    gathered = _all_gather([shards[("w_in", 0)], taps], "gather_first")
    gat = {("w_in", 0): gathered[0]}
    taps = gathered[-1][:, :CONV_WIDTH].reshape(N_DEV, DEPTH, CONV_WIDTH, BR // N_DEV)
    small = {n: wts[n] for n, _ in SMALL}
    small["conv_w"] = taps.transpose(1, 2, 0, 3).reshape(DEPTH, CONV_WIDTH, BR)

    loss_part, grad_x, grads, rs = _local_step(x[0], positions, loss_target[0], gat, small, shards, _Plan())
    small_g = _small_grads(grads)
    loss = lax.psum(loss_part, ("x", "y", "c"))

    out = {"grad": {}, "delta": {}, "new_m": {}, "new_v": {}}
    for name, _, _ in BIG:
        flip = (lambda a: a.swapaxes(1, 2)) if name == "w_in" else (lambda a: a)
        res = _adamw_big([rs[(name, l)][0] for l in range(DEPTH)], [rs[(name, l)][1] for l in range(DEPTH)],
                         flip(wts[name]), flip(mom1[name]), flip(mom2[name]), "adamw_" + name)
        res = [flip(buf) for buf in res]
        for key, buf in zip(("grad", "delta", "new_m", "new_v"), res):
            out[key][name] = buf

    part = jnp.concatenate([_pack_small(small_g), small_g["conv_w"].reshape(-1, 128)], axis=0)
    total = _sum_devices(_all_gather([part], "gather_small_grads")[0], "sum_small_grads")
    g_small = total[:SMALL_ROWS]
    d_small, m_small, v_small = _adamw_small(_pack_small(wts), g_small, _pack_small(mom1), _pack_small(mom2), "adamw_small")
    for key, buf in (("grad", g_small), ("delta", d_small), ("new_m", m_small), ("new_v", v_small)):
        out[key].update(_unpack_small(buf))
    g_taps = total[SMALL_ROWS:].reshape(DEPTH, CONV_WIDTH, N_DEV, BR // N_DEV)
    g_taps = lax.dynamic_index_in_dim(g_taps, dev, axis=2, keepdims=False)
    flat = lambda a: a.reshape(-1, 128)
    d_taps, m_taps, v_taps = _adamw_small(flat(conv_w), flat(g_taps), flat(m_conv_w), flat(v_conv_w), "adamw_taps")
    for key, buf in (("grad", g_taps), ("delta", d_taps), ("new_m", m_taps), ("new_v", v_taps)):
        out[key]["conv_w"] = buf.reshape(conv_w.shape)

    return (loss, grad_x[None], *[out["grad"][n] for n in WEIGHTS], *[out["delta"][n] for n in WEIGHTS],
            *[out["new_m"][n] for n in WEIGHTS], *[out["new_v"][n] for n in WEIGHTS])
```
